```python
import math
import jax
import jax.numpy as jnp
from jax import lax
import numpy as np

D_MODEL = 1024
BATCH = 4
SEQ = 4096
DEPTH = 2
DEC_BATCH = 128
DEC_SEQ = 4
PAST_LEN = 8192
PAGE_SIZE = 128

HEAD_DIM = 64
MIX_WIDTH = D_MODEL
GROUP_WIDTH = MIX_WIDTH // 2
N_ATT_HEADS = GROUP_WIDTH // HEAD_DIM
KV_HEADS_A = N_ATT_HEADS // 2
KV_HEADS_C = N_ATT_HEADS // 4
MOBA_BLOCK = 256
MOBA_TOPK = 3
MOBA_QUERY_ROWS = 128
GDN_HEADS = GROUP_WIDTH // HEAD_DIM
GDN_DK = HEAD_DIM
GDN_DV = HEAD_DIM
GDN_CHUNK = 64
GDN_CONV_CH = GDN_HEADS * (2 * GDN_DK + GDN_DV)
CONV_WIDTH = 4
WINDOW = 128
D_RNN = GROUP_WIDTH
LRU_BLOCKS = 8
LRU_BLOCK_W = D_RNN // LRU_BLOCKS
LRU_C = 8.0
T5_BUCKETS = 32
T5_MAX_DIST = 128
N_EXPERTS = 16
N_GROUPS = 4
EXPERTS_PER_GROUP = N_EXPERTS // N_GROUPS
TOP_K = 2
D_EXPERT = D_MODEL // 4
N_EVEN = (DEPTH + 1) // 2
N_ODD = DEPTH // 2
DEEPNORM_ALPHA = (2 * DEPTH) ** 0.25
DEEPNORM_BETA = (8 * DEPTH) ** -0.25
LN_EPS = 1e-5
NORM_EPS = 1e-6
EVEN_SIZES = (N_ATT_HEADS * HEAD_DIM, KV_HEADS_A * HEAD_DIM, KV_HEADS_A * HEAD_DIM,
              GDN_CONV_CH, GDN_HEADS * GDN_DV, GDN_HEADS, GDN_HEADS)
ODD_SIZES = (N_ATT_HEADS * HEAD_DIM, KV_HEADS_C * HEAD_DIM, KV_HEADS_C * HEAD_DIM, D_RNN, D_RNN)
P_EVEN = sum(EVEN_SIZES)
P_ODD = sum(ODD_SIZES)
F32 = jnp.float32

kernel_name = 'hybrid_moba_gdn_swa_rglru_moe_step'


def _split(p, sizes):
    return jnp.split(p, np.cumsum(sizes)[:-1].tolist(), axis=-1)


def layer_norm(x, g, b):
    xf = x.astype(F32)
    mu = xf.mean(-1, keepdims=True)
    var = jnp.square(xf - mu).mean(-1, keepdims=True)
    return (xf - mu) * lax.rsqrt(var + LN_EPS) * g.astype(F32) + b.astype(F32)


def _post_norm(h, f, g, b):
    return layer_norm(DEEPNORM_ALPHA * h.astype(F32) + f.astype(F32), g, b).astype(h.dtype)


def _l2norm(x):
    return x * lax.rsqrt(jnp.sum(x * x, -1, keepdims=True) + NORM_EPS)


def _rms(x, w):
    return x * lax.rsqrt(jnp.mean(x * x, -1, keepdims=True) + NORM_EPS) * w


def t5_bucket(rel):
    n = jnp.maximum(rel, 0)
    max_exact = T5_BUCKETS // 2
    nf = jnp.maximum(n, 1).astype(F32)
    large = max_exact + (jnp.log(nf / max_exact) / math.log(T5_MAX_DIST / max_exact)
                         * (T5_BUCKETS - max_exact)).astype(jnp.int32)
    large = jnp.minimum(large, T5_BUCKETS - 1)
    return jnp.where(n < max_exact, n, large)


def causal_conv(x, buf, w, b=None):
    L = x.shape[1]
    xp = jnp.concatenate([buf.astype(x.dtype), x], axis=1)
    y = xp[:, CONV_WIDTH - 1:] * w[CONV_WIDTH - 1]
    for j in range(CONV_WIDTH - 1):
        y = y + xp[:, j:j + L] * w[j]
    if b is not None:
        y = y + b
    return y, xp[:, L:]


def _query_chunk(n_q, batch):
    cap = max(1, min(n_q, MOBA_QUERY_ROWS // batch))
    return max(d for d in range(1, cap + 1) if n_q % d == 0)


def moba_attention(q, k, v, q0, bias_t):
    B, Lq, Hq, hd = q.shape
    Lk, KV = k.shape[1], k.shape[2]
    G = Hq // KV
    n_full = Lk // MOBA_BLOCK
    n_sel = min(MOBA_TOPK, n_full)
    head_kv = jnp.arange(Hq) // G
    kmean = k[:, :n_full * MOBA_BLOCK].reshape(B, n_full, MOBA_BLOCK, KV, hd).mean(2, dtype=F32)[:, :, head_kv]
    qc = _query_chunk(Lq, B)
    nq = Lq // qc
    q_chunks = q.reshape(B, nq, qc, Hq, hd).swapaxes(0, 1)
    starts = q0 + qc * jnp.arange(nq, dtype=jnp.int32)
    b_ix = jnp.arange(B)[:, None, None, None, None]
    h_ix = jnp.arange(Hq)[None, :, None, None, None]
    kv_ix = head_kv[None, :, None, None, None]
    offs = jnp.arange(MOBA_BLOCK)
    scale = hd ** -0.5

    def one_chunk(args):
        qb, s0 = args
        q_pos = s0 + jnp.arange(qc)
        own = jnp.broadcast_to((q_pos // MOBA_BLOCK)[None, None, :, None], (B, Hq, qc, 1))
        if n_sel > 0:
            gate = jnp.einsum('bqhd,bnhd->bhqn', qb, kmean, preferred_element_type=F32)
            gate = jnp.where(jnp.arange(n_full) < own, gate, -jnp.inf)
            _, sel = lax.top_k(gate, n_sel)
            blocks = jnp.concatenate([sel, own], -1)
            live = jnp.concatenate([sel < own, jnp.ones_like(own, dtype=bool)], -1)
        else:
            blocks, live = own, jnp.ones_like(own, dtype=bool)
        k_pos = blocks[..., None] * MOBA_BLOCK + offs
        rows = jnp.minimum(k_pos, Lk - 1)
        kg = k[b_ix, rows, kv_ix]
        vg = v[b_ix, rows, kv_ix]
        rel = q_pos[None, None, :, None, None] - k_pos
        s = jnp.einsum('bqhd,bhqsrd->bhqsr', qb, kg, preferred_element_type=F32) * scale
        s = s + bias_t[h_ix, t5_bucket(rel)].astype(F32)
        s = jnp.where(live[..., None] & (rel >= 0), s, -jnp.inf)
        n_keys = s.shape[3] * MOBA_BLOCK
        p = jax.nn.softmax(s.reshape(B, Hq, qc, n_keys), axis=-1).reshape(s.shape)
        o = jnp.einsum('bhqsr,bhqsrd->bqhd', p, vg, preferred_element_type=F32)
        return o.astype(q.dtype)

    out = lax.map(one_chunk, (q_chunks, starts))
    return out.swapaxes(0, 1).reshape(B, Lq, Hq, hd)


def gated_delta_rule(q, k, v, g, beta, s0):
    B, L, H, dk = q.shape
    dv = v.shape[-1]
    C = min(GDN_CHUNK, L)
    pad = (-L) % C
    N = (L + pad) // C

    def chunks(t):
        t = jnp.pad(t.astype(F32), [(0, 0), (0, pad)] + [(0, 0)] * (t.ndim - 2))
        t = t.reshape((B, N, C) + t.shape[2:])
        return jnp.swapaxes(jnp.swapaxes(t, 0, 1), 2, 3)

    q = chunks(q) * dk ** -0.5
    k = chunks(k)
    v = chunks(v)
    g = chunks(g)
    beta = chunks(beta)
    gc = jnp.cumsum(g, -1)
    incl = jnp.tril(jnp.ones((C, C), bool))
    strict = jnp.tril(jnp.ones((C, C), bool), -1)
    decay = jnp.exp(jnp.where(incl, gc[..., :, None] - gc[..., None, :], -jnp.inf))
    kb = k * beta[..., None]
    lower = jnp.where(strict, jnp.einsum('nbhcd,nbhsd->nbhcs', kb, k) * decay, 0.0)
    t_mat = lower + jnp.eye(C, dtype=F32)
    rhs = jnp.concatenate([v * beta[..., None], kb * jnp.exp(gc)[..., None]], -1)
    sol = lax.linalg.triangular_solve(t_mat, rhs, left_side=True, lower=True, unit_diagonal=True)
    u, w = sol[..., :dv], sol[..., dv:]
    qk = jnp.einsum('nbhcd,nbhsd->nbhcs', q, k) * decay
    qg = q * jnp.exp(gc)[..., None]
    kd = k * jnp.exp(gc[..., -1:] - gc)[..., None]
    glast = jnp.exp(gc[..., -1])[..., None, None]

    def step(S, xs):
        u_n, w_n, qk_n, qg_n, kd_n, gl_n = xs
        v_new = u_n - jnp.einsum('bhcd,bhde->bhce', w_n, S)
        o_n = jnp.einsum('bhcd,bhde->bhce', qg_n, S) + jnp.einsum('bhcs,bhse->bhce', qk_n, v_new)
        S = S * gl_n + jnp.einsum('bhcd,bhce->bhde', kd_n, v_new)
        return S, o_n

    S, o = lax.scan(step, s0.astype(F32), (u, w, qk, qg, kd, glast))
    o = jnp.swapaxes(jnp.swapaxes(o, 2, 3), 0, 1).reshape(B, N * C, H, dv)[:, :L]
    return o, S


def sink_attention(q, k, v, q_pos, k_pos, sinks, bias_t):
    *lead, Lq, Hq, hd = q.shape
    KV = k.shape[-2]
    G = Hq // KV
    qg = q.reshape(*lead, Lq, KV, G, hd)
    s = jnp.einsum('...qkgd,...skd->...kgqs', qg, k, preferred_element_type=F32) * hd ** -0.5
    rel = q_pos[..., :, None] - k_pos[..., None, :]
    ok = (rel >= 0) & (rel <= WINDOW) & (k_pos[..., None, :] >= 0)
    bias = jnp.moveaxis(bias_t.reshape(KV, G, T5_BUCKETS)[:, :, t5_bucket(rel)], (0, 1), (-4, -3))
    s = jnp.where(ok[..., None, None, :, :], s + bias.astype(F32), -jnp.inf)
    sink = sinks.astype(F32).reshape(KV, G)[:, :, None, None]
    m = jnp.maximum(s.max(-1, keepdims=True), sink)
    p = jnp.exp(s - m)
    den = p.sum(-1, keepdims=True) + jnp.exp(sink - m)
    o = jnp.einsum('...kgqs,...skd->...qkgd', p / den, v, preferred_element_type=F32)
    return o.reshape(*lead, Lq, Hq, hd).astype(q.dtype)


def swa_prompt(q, k, v, sinks, bias_t):
    B, L, Hq, hd = q.shape
    W = WINDOW
    pad = (-L) % W
    nb = (L + pad) // W

    def pad_l(t):
        return jnp.pad(t, [(0, 0), (0, pad), (0, 0), (0, 0)])

    def band(t):
        tb = pad_l(t).reshape(B, nb, W, t.shape[2], hd)
        prev = jnp.concatenate([jnp.zeros_like(tb[:, :1]), tb[:, :-1]], axis=1)
        return jnp.concatenate([prev, tb], axis=2)

    base = W * jnp.arange(nb)[:, None]
    q_pos = base + jnp.arange(W)
    k_pos = base - W + jnp.arange(2 * W)
    qb = pad_l(q).reshape(B, nb, W, Hq, hd)
    o = sink_attention(qb, band(k), band(v), q_pos, k_pos, sinks, bias_t)
    return o.reshape(B, nb * W, Hq, hd)[:, :L]


def _lru_combine(x, y):
    a1, b1 = x
    a2, b2 = y
    return a1 * a2, a2 * b1 + b2


def rg_lru(x, h0, w_a, b_a, w_x, b_x, lam):
    B, L, _ = x.shape
    xf = x.astype(F32)
    xb = xf.reshape(B, L, LRU_BLOCKS, LRU_BLOCK_W)
    r = jax.nn.sigmoid(jnp.einsum('blni,nij->blnj', xb, w_a).reshape(B, L, D_RNN) + b_a)
    i = jax.nn.sigmoid(jnp.einsum('blni,nij->blnj', xb, w_x).reshape(B, L, D_RNN) + b_x)
    log_a = -LRU_C * r * jax.nn.softplus(-lam.astype(F32))
    a = jnp.exp(log_a)
    b = jnp.sqrt(-jnp.expm1(2.0 * log_a)) * (i * xf)
    b = b.at[:, 0].add(a[:, 0] * h0.astype(F32))
    _, h = lax.associative_scan(_lru_combine, (a, b), axis=1)
    return h


def even_mixer(x, past_k, past_v, s0, conv0, w_in, w_out, conv_w, a_log, dt_bias, norm_w, bias_t):
    b, l, _ = x.shape
    qa, ka, va, qkv, z, beta_in, decay_in = _split(x @ w_in, EVEN_SIZES)
    qa = qa.reshape(b, l, N_ATT_HEADS, HEAD_DIM)
    ka = ka.reshape(b, l, KV_HEADS_A, HEAD_DIM)
    va = va.reshape(b, l, KV_HEADS_A, HEAD_DIM)
    if past_k is None:
        k_all, v_all, past = ka, va, 0
    else:
        past = past_k.shape[1]
        k_all = jnp.concatenate([past_k.astype(ka.dtype), ka], axis=1)
        v_all = jnp.concatenate([past_v.astype(va.dtype), va], axis=1)
    o_a = moba_attention(qa, k_all, v_all, past, bias_t)
    c, conv_new = causal_conv(qkv, conv0, conv_w)
    c = jax.nn.silu(c.astype(F32))
    qb, kb, vb = jnp.split(c, [GDN_HEADS * GDN_DK, 2 * GDN_HEADS * GDN_DK], axis=-1)
    qb = _l2norm(qb.reshape(b, l, GDN_HEADS, GDN_DK))
    kb = _l2norm(kb.reshape(b, l, GDN_HEADS, GDN_DK))
    vb = vb.reshape(b, l, GDN_HEADS, GDN_DV)
    beta = jax.nn.sigmoid(beta_in.astype(F32))
    g = -jnp.exp(a_log.astype(F32)) * jax.nn.softplus(decay_in.astype(F32) + dt_bias.astype(F32))
    o_b, s_new = gated_delta_rule(qb, kb, vb, g, beta, s0)
    gate = jax.nn.silu(z.reshape(b, l, GDN_HEADS, GDN_DV).astype(F32))
    o_b = _rms(o_b, norm_w.astype(F32)) * gate
    mixed = jnp.concatenate([o_a.reshape(b, l, -1), o_b.reshape(b, l, -1).astype(x.dtype)], axis=-1)
    return mixed @ w_out, ka, va, s_new, conv_new


def odd_mixer(x, buf_k, buf_v, h0, conv0, past_len, w_in, w_out, sinks, conv_w, conv_b,
              w_a, b_a, w_x, b_x, lam, bias_t):
    b, l, _ = x.shape
    qc, kc, vc, xr, gr = _split(x @ w_in, ODD_SIZES)
    qc = qc.reshape(b, l, N_ATT_HEADS, HEAD_DIM)
    kc = kc.reshape(b, l, KV_HEADS_C, HEAD_DIM)
    vc = vc.reshape(b, l, KV_HEADS_C, HEAD_DIM)
    if buf_k is None:
        o_c = swa_prompt(qc, kc, vc, sinks, bias_t)
        k_keep, v_keep = kc[:, -WINDOW:], vc[:, -WINDOW:]
    else:
        nbuf = buf_k.shape[1]
        k_all = jnp.concatenate([buf_k.astype(kc.dtype), kc], axis=1)
        v_all = jnp.concatenate([buf_v.astype(vc.dtype), vc], axis=1)
        q_pos = past_len + jnp.arange(l)
        k_pos = past_len - nbuf + jnp.arange(nbuf + l)
        o_c = sink_attention(qc, k_all, v_all, q_pos, k_pos, sinks, bias_t)
        k_keep, v_keep = k_all[:, -nbuf:], v_all[:, -nbuf:]
    xc, conv_new = causal_conv(xr, conv0, conv_w, conv_b)
    h = rg_lru(xc, h0, w_a, b_a, w_x, b_x, lam)
    o_d = h * jax.nn.gelu(gr.astype(F32))
    mixed = jnp.concatenate([o_c.reshape(b, l, -1), o_d.astype(x.dtype)], axis=-1)
    return mixed @ w_out, k_keep, v_keep, h[:, -1], conv_new


def moe(x, router_w, router_b, w_gate, w_up, w_down):
    B, L, D = x.shape
    t = x.reshape(B * L, D)
    s = jax.nn.sigmoid(jnp.dot(t, router_w, preferred_element_type=F32))
    sb = s + router_b.astype(F32)
    gscore = lax.top_k(sb.reshape(-1, N_GROUPS, EXPERTS_PER_GROUP), TOP_K)[0].sum(-1)
    gsel = jnp.argmax(gscore, -1)
    in_grp = (jnp.arange(N_EXPERTS) // EXPERTS_PER_GROUP)[None, :] == gsel[:, None]
    _, eidx = lax.top_k(jnp.where(in_grp, sb, -jnp.inf), TOP_K)
    wsel = jnp.take_along_axis(s, eidx, -1)
    wsel = wsel / wsel.sum(-1, keepdims=True)
    gates = jnp.einsum('tk,tke->te', wsel, jax.nn.one_hot(eidx, N_EXPERTS, dtype=F32))
    h = jax.nn.silu(jnp.einsum('td,edf->tef', t, w_gate)) * jnp.einsum('td,edf->tef', t, w_up)
    y = jnp.einsum('tef,efd->td', h * gates[..., None].astype(h.dtype), w_down)
    return y.reshape(B, L, D)


def setup_inputs(seed: int = 0) -> dict:
    key = jax.random.key(seed)
    keys = iter(jax.random.split(key, 48))

    def nrm(shape, scale=1.0):
        return scale * jax.random.normal(next(keys), shape, F32)

    def unif(shape, lo, hi):
        return jax.random.uniform(next(keys), shape, F32, lo, hi)

    n_pages = PAST_LEN // PAGE_SIZE
    n_pool = (DEC_BATCH * n_pages * 5) // 4
    wbuf = min(WINDOW, PAST_LEN)
    page_table = jax.random.permutation(next(keys), n_pool)[:DEC_BATCH * n_pages]
    page_table = page_table.reshape(DEC_BATCH, n_pages).astype(jnp.int32)
    dt = jnp.exp(unif((N_EVEN, GDN_HEADS), math.log(1e-3), math.log(1e-1)))
    a_lru = unif((N_ODD, D_RNN), 0.9, 0.999) ** (1.0 / LRU_C)
    return {
        'x_prompt': nrm((BATCH, SEQ, D_MODEL)),
        'x_sample': nrm((DEC_BATCH, DEC_SEQ, D_MODEL)),
        'cache_k_moba': nrm((n_pool, PAGE_SIZE, N_EVEN, KV_HEADS_A, HEAD_DIM)),
        'cache_v_moba': nrm((n_pool, PAGE_SIZE, N_EVEN, KV_HEADS_A, HEAD_DIM)),
        'state_gdn': nrm((N_EVEN, DEC_BATCH, GDN_HEADS, GDN_DK, GDN_DV), 0.1),
        'state_gdn_conv': nrm((N_EVEN, DEC_BATCH, CONV_WIDTH - 1, GDN_CONV_CH)),
        'cache_k_swa': nrm((N_ODD, DEC_BATCH, wbuf, KV_HEADS_C, HEAD_DIM)),
        'cache_v_swa': nrm((N_ODD, DEC_BATCH, wbuf, KV_HEADS_C, HEAD_DIM)),
        'state_lru': nrm((N_ODD, DEC_BATCH, D_RNN), 0.5),
        'state_lru_conv': nrm((N_ODD, DEC_BATCH, CONV_WIDTH - 1, D_RNN)),
        'page_table': page_table,
        'w_in_even': nrm((N_EVEN, D_MODEL, P_EVEN), D_MODEL ** -0.5),
        'w_out_even': nrm((N_EVEN, MIX_WIDTH, D_MODEL), DEEPNORM_BETA * MIX_WIDTH ** -0.5),
        'gdn_conv_w': nrm((N_EVEN, CONV_WIDTH, GDN_CONV_CH), CONV_WIDTH ** -0.5),
        'gdn_a_log': jnp.log(unif((N_EVEN, GDN_HEADS), 1.0, 16.0)),
        'gdn_dt_bias': dt + jnp.log(-jnp.expm1(-dt)),
        'gdn_norm_w': 1.0 + nrm((N_EVEN, GDN_DV), 0.02),
        'w_in_odd': nrm((N_ODD, D_MODEL, P_ODD), D_MODEL ** -0.5),
        'w_out_odd': nrm((N_ODD, MIX_WIDTH, D_MODEL), DEEPNORM_BETA * MIX_WIDTH ** -0.5),
        'swa_sinks': nrm((N_ODD, N_ATT_HEADS)),
        'lru_conv_w': nrm((N_ODD, CONV_WIDTH, D_RNN), CONV_WIDTH ** -0.5),
        'lru_conv_b': nrm((N_ODD, D_RNN), 0.02),
        'lru_w_a': nrm((N_ODD, LRU_BLOCKS, LRU_BLOCK_W, LRU_BLOCK_W), LRU_BLOCK_W ** -0.5),
        'lru_b_a': nrm((N_ODD, D_RNN), 0.02),
        'lru_w_x': nrm((N_ODD, LRU_BLOCKS, LRU_BLOCK_W, LRU_BLOCK_W), LRU_BLOCK_W ** -0.5),
        'lru_b_x': nrm((N_ODD, D_RNN), 0.02),
        'lru_lambda': jnp.log(a_lru) - jnp.log1p(-a_lru),
        'rel_bias': nrm((T5_BUCKETS, N_ATT_HEADS), 0.5),
        'ln_g': 1.0 + nrm((DEPTH, 2, D_MODEL), 0.02),
        'ln_b': nrm((DEPTH, 2, D_MODEL), 0.02),
        'router_w': nrm((D_MODEL, N_EXPERTS), D_MODEL ** -0.5),
        'router_b': nrm((N_EXPERTS,), 0.01),
        'w_gate': nrm((DEPTH, N_EXPERTS, D_MODEL, D_EXPERT), D_MODEL ** -0.5),
        'w_up': nrm((DEPTH, N_EXPERTS, D_MODEL, D_EXPERT), D_MODEL ** -0.5),
        'w_down': nrm((DEPTH, N_EXPERTS, D_EXPERT, D_MODEL), DEEPNORM_BETA * D_EXPERT ** -0.5),
    }


def reference(x_prompt, x_sample, cache_k_moba, cache_v_moba, state_gdn, state_gdn_conv,
              cache_k_swa, cache_v_swa, state_lru, state_lru_conv, page_table,
              w_in_even, w_out_even, gdn_conv_w, gdn_a_log, gdn_dt_bias, gdn_norm_w,
              w_in_odd, w_out_odd, swa_sinks, lru_conv_w, lru_conv_b, lru_w_a, lru_b_a,
              lru_w_x, lru_b_x, lru_lambda, rel_bias, ln_g, ln_b, router_w, router_b,
              w_gate, w_up, w_down):
    n_pr = x_prompt.shape[0]
    n_dec = x_sample.shape[0]
    past_len = page_table.shape[1] * PAGE_SIZE
    bias_t = rel_bias.T
    hp, hs = x_prompt, x_sample
    kp_l, vp_l, ks_l, vs_l = [], [], [], []
    gp_l, gs_l, gcp_l, gcs_l = [], [], [], []
    wkp_l, wvp_l, wks_l, wvs_l = [], [], [], []
    lp_l, ls_l, lcp_l, lcs_l = [], [], [], []
    for layer in range(DEPTH):
        j = layer // 2
        if layer % 2 == 0:
            mp, kp, vp, sp, cp = even_mixer(
                hp, None, None, jnp.zeros((n_pr, GDN_HEADS, GDN_DK, GDN_DV), F32),
                jnp.zeros((n_pr, CONV_WIDTH - 1, GDN_CONV_CH), hp.dtype),
                w_in_even[j], w_out_even[j], gdn_conv_w[j], gdn_a_log[j], gdn_dt_bias[j], gdn_norm_w[j], bias_t)
            past_k = cache_k_moba[page_table][:, :, :, j].reshape(n_dec, past_len, KV_HEADS_A, HEAD_DIM)
            past_v = cache_v_moba[page_table][:, :, :, j].reshape(n_dec, past_len, KV_HEADS_A, HEAD_DIM)
            ms, ks, vs, ss, cs = even_mixer(
                hs, past_k, past_v, state_gdn[j], state_gdn_conv[j],
                w_in_even[j], w_out_even[j], gdn_conv_w[j], gdn_a_log[j], gdn_dt_bias[j], gdn_norm_w[j], bias_t)
            kp_l.append(kp)
            vp_l.append(vp)
            ks_l.append(ks)
            vs_l.append(vs)
            gp_l.append(sp)
            gs_l.append(ss)
            gcp_l.append(cp)
            gcs_l.append(cs)
        else:
            mp, kp, vp, hp_last, cp = odd_mixer(
                hp, None, None, jnp.zeros((n_pr, D_RNN), F32),
                jnp.zeros((n_pr, CONV_WIDTH - 1, D_RNN), hp.dtype), 0,
                w_in_odd[j], w_out_odd[j], swa_sinks[j], lru_conv_w[j], lru_conv_b[j],
                lru_w_a[j], lru_b_a[j], lru_w_x[j], lru_b_x[j], lru_lambda[j], bias_t)
            ms, ks, vs, hs_last, cs = odd_mixer(
                hs, cache_k_swa[j], cache_v_swa[j], state_lru[j], state_lru_conv[j], past_len,
                w_in_odd[j], w_out_odd[j], swa_sinks[j], lru_conv_w[j], lru_conv_b[j],
                lru_w_a[j], lru_b_a[j], lru_w_x[j], lru_b_x[j], lru_lambda[j], bias_t)
            wkp_l.append(kp)
            wvp_l.append(vp)
            wks_l.append(ks)
            wvs_l.append(vs)
            lp_l.append(hp_last)
            ls_l.append(hs_last)
            lcp_l.append(cp)
            lcs_l.append(cs)
        hp = _post_norm(hp, mp, ln_g[layer, 0], ln_b[layer, 0])
        hs = _post_norm(hs, ms, ln_g[layer, 0], ln_b[layer, 0])
        hp = _post_norm(hp, moe(hp, router_w, router_b, w_gate[layer], w_up[layer], w_down[layer]),
                        ln_g[layer, 1], ln_b[layer, 1])
        hs = _post_norm(hs, moe(hs, router_w, router_b, w_gate[layer], w_up[layer], w_down[layer]),
                        ln_g[layer, 1], ln_b[layer, 1])
    y_prompt = hp
    y_sample = hs
    moba_k_prompt = jnp.stack(kp_l, axis=2)
    moba_v_prompt = jnp.stack(vp_l, axis=2)
    moba_k_sample = jnp.stack(ks_l, axis=2)
    moba_v_sample = jnp.stack(vs_l, axis=2)
    gdn_state_prompt = jnp.stack(gp_l, axis=0)
    gdn_state_sample = jnp.stack(gs_l, axis=0)
    gdn_conv_prompt = jnp.stack(gcp_l, axis=0)
    gdn_conv_sample = jnp.stack(gcs_l, axis=0)
    swa_k_prompt = jnp.stack(wkp_l, axis=0)
    swa_v_prompt = jnp.stack(wvp_l, axis=0)
    swa_k_sample = jnp.stack(wks_l, axis=0)
    swa_v_sample = jnp.stack(wvs_l, axis=0)
    lru_state_prompt = jnp.stack(lp_l, axis=0)
    lru_state_sample = jnp.stack(ls_l, axis=0)
    lru_conv_prompt = jnp.stack(lcp_l, axis=0)
    lru_conv_sample = jnp.stack(lcs_l, axis=0)
    return (y_prompt, y_sample, moba_k_prompt, moba_v_prompt, moba_k_sample, moba_v_sample,
            gdn_state_prompt, gdn_state_sample, gdn_conv_prompt, gdn_conv_sample,
            swa_k_prompt, swa_v_prompt, swa_k_sample, swa_v_sample,
            lru_state_prompt, lru_state_sample, lru_conv_prompt, lru_conv_sample)
```

```python
import functools
import math

import jax
import jax.numpy as jnp
import numpy as np
from jax import lax
from jax.experimental import pallas as pl
from jax.experimental.pallas import tpu as pltpu

F32 = jnp.float32
BF16 = jnp.bfloat16

D_MODEL = 1024
HEAD_DIM = 64
N_ATT_HEADS = 8
KV_HEADS_A = 4
KV_HEADS_C = 2
MOBA_BLOCK = 256
MOBA_TOPK = 3
PAGE_SIZE = 128
GDN_HEADS = 8
GDN_CHUNK = 64
GDN_CONV_CH = 1536
CONV_WIDTH = 4
WINDOW = 128
D_RNN = 512
LRU_BLOCKS = 8
LRU_C = 8.0
T5_BUCKETS = 32
T5_MAX_DIST = 128
N_EXPERTS = 16
N_GROUPS = 4
EXPERTS_PER_GROUP = 4
D_EXPERT = 256
DEPTH = 2
DEEPNORM_ALPHA = (2 * DEPTH) ** 0.25
LN_EPS = 1e-5
NORM_EPS = 1e-6
EVEN_SIZES = (512, 256, 256, 1536, 512, 8, 8)
ODD_SIZES = (512, 128, 128, 512, 512)

LANES = 128
VMEM_LIMIT = 56 * 1024 * 1024
NEG_INF = float("-inf")


def _cparams(sem):
    return pltpu.CompilerParams(dimension_semantics=sem, vmem_limit_bytes=VMEM_LIMIT)


def _dot(a, b):
    return jnp.dot(a.astype(BF16), b.astype(BF16), preferred_element_type=F32)


def _dot_nt(a, b):
    return lax.dot_general(a.astype(BF16), b.astype(BF16), (((1,), (1,)), ((), ())),
                           preferred_element_type=F32)


def _mm_kernel(x_ref, w_ref, o_ref):
    o_ref[...] = jnp.dot(x_ref[...].astype(BF16), w_ref[...], preferred_element_type=F32)


def _mm(x, w, tm, tn):
    m, k = x.shape
    n = w.shape[1]
    assert m % tm == 0 and n % tn == 0
    return pl.pallas_call(
        _mm_kernel,
        grid=(m // tm, n // tn),
        in_specs=[pl.BlockSpec((tm, k), lambda i, j: (i, 0)),
                  pl.BlockSpec((k, tn), lambda i, j: (0, j))],
        out_specs=pl.BlockSpec((tm, tn), lambda i, j: (i, j)),
        out_shape=jax.ShapeDtypeStruct((m, n), F32),
        compiler_params=_cparams(("parallel", "arbitrary")),
        name="in_proj",
    )(x, w)


def _layer_norm_rows(z, g, b):
    mu = jnp.mean(z, axis=-1, keepdims=True)
    zc = z - mu
    var = jnp.mean(zc * zc, axis=-1, keepdims=True)
    return zc * lax.rsqrt(var + LN_EPS) * g + b


def _mm_ln_kernel(x_ref, w_ref, h_ref, g_ref, b_ref, o_ref):
    f = jnp.dot(x_ref[...].astype(BF16), w_ref[...], preferred_element_type=F32)
    o_ref[...] = _layer_norm_rows(DEEPNORM_ALPHA * h_ref[...] + f, g_ref[...], b_ref[...])


def _mm_ln(x, w, h, g, b, tm):
    m, k = x.shape
    n = w.shape[1]
    assert m % tm == 0
    return pl.pallas_call(
        _mm_ln_kernel,
        grid=(m // tm,),
        in_specs=[pl.BlockSpec((tm, k), lambda i: (i, 0)),
                  pl.BlockSpec((k, n), lambda i: (0, 0)),
                  pl.BlockSpec((tm, n), lambda i: (i, 0)),
                  pl.BlockSpec((1, n), lambda i: (0, 0)),
                  pl.BlockSpec((1, n), lambda i: (0, 0))],
        out_specs=pl.BlockSpec((tm, n), lambda i: (i, 0)),
        out_shape=jax.ShapeDtypeStruct((m, n), F32),
        compiler_params=_cparams(("parallel",)),
        name="out_proj_ln",
    )(x, w, h, g.reshape(1, n), b.reshape(1, n))


def _row_max_first(vals, lane):
    mx = jnp.max(vals, axis=1, keepdims=True)
    idx = jnp.min(jnp.where(vals == mx, lane, LANES), axis=1, keepdims=True)
    return mx, idx


def _moe_ln_kernel(t_ref, rw_ref, rb_ref, wg_ref, wu_ref, wd_ref, g_ref, b_ref, o_ref,
                   gates_ref, tb_ref, acc_ref):
    e = pl.program_id(1)

    @pl.when(e == 0)
    def _route():
        t = t_ref[...]
        logits = jnp.dot(t, rw_ref[...], preferred_element_type=F32,
                         precision=lax.Precision.HIGHEST)
        s = jax.nn.sigmoid(logits)
        sb = s + rb_ref[...]
        lane = lax.broadcasted_iota(jnp.int32, sb.shape, 1)
        grp = lane // EXPERTS_PER_GROUP
        best = None
        gsel = None
        for gi in range(N_GROUPS):
            vals = jnp.where(grp == gi, sb, NEG_INF)
            m1, i1 = _row_max_first(vals, lane)
            m2 = jnp.max(jnp.where(lane == i1, NEG_INF, vals), axis=1, keepdims=True)
            score = m1 + m2
            if gi == 0:
                best, gsel = score, jnp.zeros_like(i1)
            else:
                upd = score > best
                best = jnp.where(upd, score, best)
                gsel = jnp.where(upd, gi, gsel)
        vals = jnp.where(grp == gsel, sb, NEG_INF)
        _, i1 = _row_max_first(vals, lane)
        _, i2 = _row_max_first(jnp.where(lane == i1, NEG_INF, vals), lane)
        s1 = jnp.sum(jnp.where(lane == i1, s, 0.0), axis=1, keepdims=True)
        s2 = jnp.sum(jnp.where(lane == i2, s, 0.0), axis=1, keepdims=True)
        den = s1 + s2
        gates_ref[...] = jnp.where(lane == i1, s1 / den, 0.0) + jnp.where(lane == i2, s2 / den, 0.0)
        tb_ref[...] = t.astype(BF16)
        acc_ref[...] = jnp.zeros_like(acc_ref)

    tb = tb_ref[...]
    hg = jnp.dot(tb, wg_ref[...], preferred_element_type=F32)
    hu = jnp.dot(tb, wu_ref[...], preferred_element_type=F32)
    gates = gates_ref[...]
    lane = lax.broadcasted_iota(jnp.int32, gates.shape, 1)
    ge = jnp.sum(jnp.where(lane == e, gates, 0.0), axis=1, keepdims=True)
    h = (hg * jax.nn.sigmoid(hg)) * hu * ge
    acc_ref[...] += jnp.dot(h.astype(BF16), wd_ref[...], preferred_element_type=F32)

    @pl.when(e == pl.num_programs(1) - 1)
    def _finish():
        o_ref[...] = _layer_norm_rows(DEEPNORM_ALPHA * t_ref[...] + acc_ref[...], g_ref[...], b_ref[...])


def _moe_ln(t, router_w_pad, router_b_pad, wg, wu, wd, g, b, tm):
    m, d = t.shape
    ne, _, f = wg.shape
    assert m % tm == 0
    return pl.pallas_call(
        _moe_ln_kernel,
        grid=(m // tm, ne),
        in_specs=[pl.BlockSpec((tm, d), lambda i, e: (i, 0)),
                  pl.BlockSpec((d, LANES), lambda i, e: (0, 0)),
                  pl.BlockSpec((1, LANES), lambda i, e: (0, 0)),
                  pl.BlockSpec((None, d, f), lambda i, e: (e, 0, 0)),
                  pl.BlockSpec((None, d, f), lambda i, e: (e, 0, 0)),
                  pl.BlockSpec((None, f, d), lambda i, e: (e, 0, 0)),
                  pl.BlockSpec((1, d), lambda i, e: (0, 0)),
                  pl.BlockSpec((1, d), lambda i, e: (0, 0))],
        out_specs=pl.BlockSpec((tm, d), lambda i, e: (i, 0)),
        out_shape=jax.ShapeDtypeStruct((m, d), F32),
        scratch_shapes=[pltpu.VMEM((tm, LANES), F32),
                        pltpu.VMEM((tm, d), BF16),
                        pltpu.VMEM((tm, d), F32)],
        compiler_params=_cparams(("parallel", "arbitrary")),
        name="moe_ln",
    )(t, router_w_pad, router_b_pad, wg, wu, wd, g.reshape(1, d), b.reshape(1, d))


def _t5_bucket_np(rel):
    n = np.maximum(rel, 0)
    max_exact = T5_BUCKETS // 2
    nf = np.maximum(n, 1).astype(np.float32)
    large = max_exact + (np.log(nf / np.float32(max_exact)) / np.float32(math.log(T5_MAX_DIST / max_exact))
                         * np.float32(T5_BUCKETS - max_exact)).astype(np.int32)
    large = np.minimum(large, T5_BUCKETS - 1)
    return np.where(n < max_exact, n, large).astype(np.int32)


def _moba_prompt_kernel(q_ref, k_ref, v_ref, bd_ref, bp_ref, bf_ref, o_ref,
                        kmean_ref, sel_ref, m_ref, l_ref, acc_ref):
    i = pl.program_id(2)
    blk = MOBA_BLOCK
    nblk = k_ref.shape[0] // blk
    hd = HEAD_DIM

    @pl.when(i == 0)
    def _means():
        for n in range(nblk):
            kmean_ref[n:n + 1, :] = jnp.mean(k_ref[n * blk:(n + 1) * blk, :], axis=0, keepdims=True)

    q4 = q_ref[...] * (hd ** -0.5)
    for kvl in range(2):
        lo = kvl * 2 * hd
        q2 = jnp.concatenate([q4[:, lo:lo + hd], q4[:, lo + hd:lo + 2 * hd]], axis=0)
        q2b = q2.astype(BF16)
        kcol = slice(kvl * hd, (kvl + 1) * hd)

        kmean = kmean_ref[:, kcol]
        gate = lax.dot_general(q2, kmean, (((1,), (1,)), ((), ())), preferred_element_type=F32,
                               precision=lax.Precision.HIGHEST)
        col = lax.broadcasted_iota(jnp.int32, gate.shape, 1)
        gate = jnp.where(col < i, gate, NEG_INF)
        sel = jnp.zeros(gate.shape, F32)
        for _ in range(MOBA_TOPK):
            mx = jnp.max(gate, axis=1, keepdims=True)
            idx = jnp.min(jnp.where(gate == mx, col, nblk), axis=1, keepdims=True)
            hit = (col == idx) & (mx > NEG_INF)
            sel = jnp.where(hit, 1.0, sel)
            gate = jnp.where(col == idx, NEG_INF, gate)
        sel_ref[...] = sel

        m_ref[...] = jnp.full(m_ref.shape, NEG_INF, F32)
        l_ref[...] = jnp.zeros(l_ref.shape, F32)
        acc_ref[...] = jnp.zeros(acc_ref.shape, F32)

        def update(s, vb):
            m_old = m_ref[...]
            m_new = jnp.maximum(m_old, jnp.max(s, axis=1, keepdims=True))
            alpha = jnp.exp(m_old - m_new)
            p = jnp.exp(s - m_new)
            l_ref[...] = alpha * l_ref[...] + jnp.sum(p, axis=1, keepdims=True)
            acc_ref[...] = alpha * acc_ref[...] + jnp.dot(p.astype(BF16), vb, preferred_element_type=F32)
            m_ref[...] = m_new

        def tile(j):
            r0 = pl.multiple_of(j * blk, blk)
            kb = k_ref[pl.ds(r0, blk), :][:, kcol].astype(BF16)
            vb = v_ref[pl.ds(r0, blk), :][:, kcol].astype(BF16)
            return _dot_nt(q2b, kb), vb

        s, vb = tile(i)
        update(s + bd_ref[kvl], vb)

        @pl.when(i >= 1)
        def _prev():
            s, vb = tile(i - 1)
            live = jnp.sum(jnp.where(col == i - 1, sel_ref[...], 0.0), axis=1, keepdims=True) > 0.0
            update(jnp.where(live, s + bp_ref[kvl], NEG_INF), vb)

        def far(j, carry):
            s, vb = tile(j)
            live = jnp.sum(jnp.where(col == j, sel_ref[...], 0.0), axis=1, keepdims=True) > 0.0
            rb = jnp.where(live, bf_ref[kvl], NEG_INF)
            update(s + rb, vb)
            return carry

        lax.fori_loop(0, jnp.maximum(i - 1, 0), far, 0)

        o = acc_ref[...] / l_ref[...]
        o_ref[:, lo:lo + hd] = o[:blk]
        o_ref[:, lo + hd:lo + 2 * hd] = o[blk:]


def _moba_prompt(proj, bias_rel, n_batch, seq, q_col, k_col, v_col):
    blk = MOBA_BLOCK
    nblk = seq // blk
    iq = np.arange(blk)[:, None]
    ik = np.arange(blk)[None, :]
    idx_diag = _t5_bucket_np(iq - ik)
    idx_prev = _t5_bucket_np(blk + iq - ik)
    bt = bias_rel.T
    diag = jnp.where(jnp.asarray(iq >= ik)[None], bt[:, idx_diag], NEG_INF)
    prev = bt[:, idx_prev]
    far = jnp.broadcast_to(bt[:, T5_BUCKETS - 1][:, None, None], (N_ATT_HEADS, blk, 1))
    diag = diag.reshape(KV_HEADS_A, 2 * blk, blk)
    prev = prev.reshape(KV_HEADS_A, 2 * blk, blk)
    far = far.reshape(KV_HEADS_A, 2 * blk, 1)
    rows_per_b = seq // blk
    return pl.pallas_call(
        _moba_prompt_kernel,
        grid=(n_batch, 2, nblk),
        in_specs=[pl.BlockSpec((blk, 2 * LANES), lambda b, p, i: (b * rows_per_b + i, q_col // 2 + p)),
                  pl.BlockSpec((seq, LANES), lambda b, p, i: (b, k_col + p)),
                  pl.BlockSpec((seq, LANES), lambda b, p, i: (b, v_col + p)),
                  pl.BlockSpec((2, 2 * blk, blk), lambda b, p, i: (p, 0, 0)),
                  pl.BlockSpec((2, 2 * blk, blk), lambda b, p, i: (p, 0, 0)),
                  pl.BlockSpec((2, 2 * blk, 1), lambda b, p, i: (p, 0, 0))],
        out_specs=pl.BlockSpec((blk, 2 * LANES), lambda b, p, i: (b * rows_per_b + i, p)),
        out_shape=jax.ShapeDtypeStruct((n_batch * seq, 4 * LANES), F32),
        scratch_shapes=[pltpu.VMEM((nblk, LANES), F32),
                        pltpu.VMEM((2 * blk, nblk), F32),
                        pltpu.VMEM((2 * blk, 1), F32),
                        pltpu.VMEM((2 * blk, 1), F32),
                        pltpu.VMEM((2 * blk, HEAD_DIM), F32)],
        compiler_params=_cparams(("parallel", "parallel", "arbitrary")),
        name="moba_prompt",
    )(proj, proj, proj, diag, prev, far)


def _x_l2norm(x):
    return x * lax.rsqrt(jnp.sum(x * x, -1, keepdims=True) + NORM_EPS)


def _x_rms(x, w):
    return x * lax.rsqrt(jnp.mean(x * x, -1, keepdims=True) + NORM_EPS) * w


def _x_t5_bucket(rel):
    n = jnp.maximum(rel, 0)
    max_exact = T5_BUCKETS // 2
    nf = jnp.maximum(n, 1).astype(F32)
    large = max_exact + (jnp.log(nf / max_exact) / math.log(T5_MAX_DIST / max_exact)
                         * (T5_BUCKETS - max_exact)).astype(jnp.int32)
    large = jnp.minimum(large, T5_BUCKETS - 1)
    return jnp.where(n < max_exact, n, large)


def _x_causal_conv(x, buf, w, b=None):
    L = x.shape[1]
    xp = jnp.concatenate([buf.astype(x.dtype), x], axis=1)
    y = xp[:, CONV_WIDTH - 1:] * w[CONV_WIDTH - 1]
    for j in range(CONV_WIDTH - 1):
        y = y + xp[:, j:j + L] * w[j]
    if b is not None:
        y = y + b
    return y, xp[:, L:]


def _x_query_chunk(n_q, batch):
    cap = max(1, min(n_q, 128 // batch))
    return max(d for d in range(1, cap + 1) if n_q % d == 0)


def _x_moba_attention(q, k, v, q0, bias_t):
    B, Lq, Hq, hd = q.shape
    Lk, KV = k.shape[1], k.shape[2]
    G = Hq // KV
    n_full = Lk // MOBA_BLOCK
    n_sel = min(MOBA_TOPK, n_full)
    head_kv = jnp.arange(Hq) // G
    kmean = k[:, :n_full * MOBA_BLOCK].reshape(B, n_full, MOBA_BLOCK, KV, hd).mean(2, dtype=F32)[:, :, head_kv]
    qc = _x_query_chunk(Lq, B)
    nq = Lq // qc
    q_chunks = q.reshape(B, nq, qc, Hq, hd).swapaxes(0, 1)
    starts = q0 + qc * jnp.arange(nq, dtype=jnp.int32)
    b_ix = jnp.arange(B)[:, None, None, None, None]
    h_ix = jnp.arange(Hq)[None, :, None, None, None]
    kv_ix = head_kv[None, :, None, None, None]
    offs = jnp.arange(MOBA_BLOCK)
    scale = hd ** -0.5

    def one_chunk(args):
        qb, s0 = args
        q_pos = s0 + jnp.arange(qc)
        own = jnp.broadcast_to((q_pos // MOBA_BLOCK)[None, None, :, None], (B, Hq, qc, 1))
        gate = jnp.einsum('bqhd,bnhd->bhqn', qb, kmean, preferred_element_type=F32)
        gate = jnp.where(jnp.arange(n_full) < own, gate, -jnp.inf)
        _, sel = lax.top_k(gate, n_sel)
        blocks = jnp.concatenate([sel, own], -1)
        live = jnp.concatenate([sel < own, jnp.ones_like(own, dtype=bool)], -1)
        k_pos = blocks[..., None] * MOBA_BLOCK + offs
        rows = jnp.minimum(k_pos, Lk - 1)
        kg = k[b_ix, rows, kv_ix]
        vg = v[b_ix, rows, kv_ix]
        rel = q_pos[None, None, :, None, None] - k_pos
        s = jnp.einsum('bqhd,bhqsrd->bhqsr', qb, kg, preferred_element_type=F32) * scale
        s = s + bias_t[h_ix, _x_t5_bucket(rel)].astype(F32)
        s = jnp.where(live[..., None] & (rel >= 0), s, -jnp.inf)
        n_keys = s.shape[3] * MOBA_BLOCK
        p = jax.nn.softmax(s.reshape(B, Hq, qc, n_keys), axis=-1).reshape(s.shape)
        o = jnp.einsum('bhqsr,bhqsrd->bqhd', p, vg, preferred_element_type=F32)
        return o.astype(q.dtype)

    out = lax.map(one_chunk, (q_chunks, starts))
    return out.swapaxes(0, 1).reshape(B, Lq, Hq, hd)


def _x_gated_delta_rule(q, k, v, g, beta, s0):
    B, L, H, dk = q.shape
    dv = v.shape[-1]
    C = min(GDN_CHUNK, L)
    pad = (-L) % C
    N = (L + pad) // C

    def chunks(t):
        t = jnp.pad(t.astype(F32), [(0, 0), (0, pad)] + [(0, 0)] * (t.ndim - 2))
        t = t.reshape((B, N, C) + t.shape[2:])
        return jnp.swapaxes(jnp.swapaxes(t, 0, 1), 2, 3)

    q = chunks(q) * dk ** -0.5
    k = chunks(k)
    v = chunks(v)
    g = chunks(g)
    beta = chunks(beta)
    gc = jnp.cumsum(g, -1)
    incl = jnp.tril(jnp.ones((C, C), bool))
    strict = jnp.tril(jnp.ones((C, C), bool), -1)
    decay = jnp.exp(jnp.where(incl, gc[..., :, None] - gc[..., None, :], -jnp.inf))
    kb = k * beta[..., None]
    lower = jnp.where(strict, jnp.einsum('nbhcd,nbhsd->nbhcs', kb, k) * decay, 0.0)
    t_mat = lower + jnp.eye(C, dtype=F32)
    rhs = jnp.concatenate([v * beta[..., None], kb * jnp.exp(gc)[..., None]], -1)
    sol = lax.linalg.triangular_solve(t_mat, rhs, left_side=True, lower=True, unit_diagonal=True)
    u, w = sol[..., :dv], sol[..., dv:]
    qk = jnp.einsum('nbhcd,nbhsd->nbhcs', q, k) * decay
    qg = q * jnp.exp(gc)[..., None]
    kd = k * jnp.exp(gc[..., -1:] - gc)[..., None]
    glast = jnp.exp(gc[..., -1])[..., None, None]

    def step(S, xs):
        u_n, w_n, qk_n, qg_n, kd_n, gl_n = xs
        v_new = u_n - jnp.einsum('bhcd,bhde->bhce', w_n, S)
        o_n = jnp.einsum('bhcd,bhde->bhce', qg_n, S) + jnp.einsum('bhcs,bhse->bhce', qk_n, v_new)
        S = S * gl_n + jnp.einsum('bhcd,bhce->bhde', kd_n, v_new)
        return S, o_n

    S, o = lax.scan(step, s0.astype(F32), (u, w, qk, qg, kd, glast))
    o = jnp.swapaxes(jnp.swapaxes(o, 2, 3), 0, 1).reshape(B, N * C, H, dv)[:, :L]
    return o, S


def _x_sink_attention(q, k, v, q_pos, k_pos, sinks, bias_t):
    *lead, Lq, Hq, hd = q.shape
    KV = k.shape[-2]
    G = Hq // KV
    qg = q.reshape(*lead, Lq, KV, G, hd)
    s = jnp.einsum('...qkgd,...skd->...kgqs', qg, k, preferred_element_type=F32) * hd ** -0.5
    rel = q_pos[..., :, None] - k_pos[..., None, :]
    ok = (rel >= 0) & (rel <= WINDOW) & (k_pos[..., None, :] >= 0)
    bias = jnp.moveaxis(bias_t.reshape(KV, G, T5_BUCKETS)[:, :, _x_t5_bucket(rel)], (0, 1), (-4, -3))
    s = jnp.where(ok[..., None, None, :, :], s + bias.astype(F32), -jnp.inf)
    sink = sinks.astype(F32).reshape(KV, G)[:, :, None, None]
    m = jnp.maximum(s.max(-1, keepdims=True), sink)
    p = jnp.exp(s - m)
    den = p.sum(-1, keepdims=True) + jnp.exp(sink - m)
    o = jnp.einsum('...kgqs,...skd->...qkgd', p / den, v, preferred_element_type=F32)
    return o.reshape(*lead, Lq, Hq, hd).astype(q.dtype)


def _x_swa_prompt(q, k, v, sinks, bias_t):
    B, L, Hq, hd = q.shape
    W = WINDOW
    nb = L // W

    def band(t):
        tb = t.reshape(B, nb, W, t.shape[2], hd)
        prev = jnp.concatenate([jnp.zeros_like(tb[:, :1]), tb[:, :-1]], axis=1)
        return jnp.concatenate([prev, tb], axis=2)

    base = W * jnp.arange(nb)[:, None]
    q_pos = base + jnp.arange(W)
    k_pos = base - W + jnp.arange(2 * W)
    qb = q.reshape(B, nb, W, Hq, hd)
    o = _x_sink_attention(qb, band(k), band(v), q_pos, k_pos, sinks, bias_t)
    return o.reshape(B, nb * W, Hq, hd)


def _x_lru_combine(x, y):
    a1, b1 = x
    a2, b2 = y
    return a1 * a2, a2 * b1 + b2


def _x_rg_lru(x, h0, w_a, b_a, w_x, b_x, lam):
    B, L, _ = x.shape
    xf = x.astype(F32)
    xb = xf.reshape(B, L, LRU_BLOCKS, D_RNN // LRU_BLOCKS)
    r = jax.nn.sigmoid(jnp.einsum('blni,nij->blnj', xb, w_a).reshape(B, L, D_RNN) + b_a)
    i = jax.nn.sigmoid(jnp.einsum('blni,nij->blnj', xb, w_x).reshape(B, L, D_RNN) + b_x)
    log_a = -LRU_C * r * jax.nn.softplus(-lam.astype(F32))
    a = jnp.exp(log_a)
    b = jnp.sqrt(-jnp.expm1(2.0 * log_a)) * (i * xf)
    b = b.at[:, 0].add(a[:, 0] * h0.astype(F32))
    _, h = lax.associative_scan(_x_lru_combine, (a, b), axis=1)
    return h


def _x_even_rest(p, b, l, past_k, past_v, s0, conv0, conv_w, a_log, dt_bias, norm_w, bias_t, o_a=None):
    qa, ka, va, qkv, z, beta_in, decay_in = p
    if o_a is None:
        qa4 = qa.reshape(b, l, N_ATT_HEADS, HEAD_DIM)
        ka4 = ka.reshape(b, l, KV_HEADS_A, HEAD_DIM)
        va4 = va.reshape(b, l, KV_HEADS_A, HEAD_DIM)
        past = past_k.shape[1]
        k_all = jnp.concatenate([past_k, ka4], axis=1)
        v_all = jnp.concatenate([past_v, va4], axis=1)
        o_a = _x_moba_attention(qa4, k_all, v_all, past, bias_t).reshape(b, l, -1)
    c, conv_new = _x_causal_conv(qkv, conv0, conv_w)
    c = jax.nn.silu(c)
    qb, kb, vb = jnp.split(c, [512, 1024], axis=-1)
    qb = _x_l2norm(qb.reshape(b, l, GDN_HEADS, 64))
    kb = _x_l2norm(kb.reshape(b, l, GDN_HEADS, 64))
    vb = vb.reshape(b, l, GDN_HEADS, 64)
    beta = jax.nn.sigmoid(beta_in)
    g = -jnp.exp(a_log) * jax.nn.softplus(decay_in + dt_bias)
    o_b, s_new = _x_gated_delta_rule(qb, kb, vb, g, beta, s0)
    gate = jax.nn.silu(z.reshape(b, l, GDN_HEADS, 64))
    o_b = _x_rms(o_b, norm_w) * gate
    mixed = jnp.concatenate([o_a, o_b.reshape(b, l, -1)], axis=-1)
    return mixed, s_new, conv_new


def _x_odd_rest(p, b, l, buf_k, buf_v, h0, conv0, past_len, sinks, conv_w, conv_b,
                w_a, b_a, w_x, b_x, lam, bias_t):
    qc, kc, vc, xr, gr = p
    qc = qc.reshape(b, l, N_ATT_HEADS, HEAD_DIM)
    kc = kc.reshape(b, l, KV_HEADS_C, HEAD_DIM)
    vc = vc.reshape(b, l, KV_HEADS_C, HEAD_DIM)
    if buf_k is None:
        o_c = _x_swa_prompt(qc, kc, vc, sinks, bias_t)
        k_keep, v_keep = kc[:, -WINDOW:], vc[:, -WINDOW:]
    else:
        nbuf = buf_k.shape[1]
        k_all = jnp.concatenate([buf_k, kc], axis=1)
        v_all = jnp.concatenate([buf_v, vc], axis=1)
        q_pos = past_len + jnp.arange(l)
        k_pos = past_len - nbuf + jnp.arange(nbuf + l)
        o_c = _x_sink_attention(qc, k_all, v_all, q_pos, k_pos, sinks, bias_t)
        k_keep, v_keep = k_all[:, -nbuf:], v_all[:, -nbuf:]
    xc, conv_new = _x_causal_conv(xr, conv0, conv_w, conv_b)
    h = _x_rg_lru(xc, h0, w_a, b_a, w_x, b_x, lam)
    o_d = h * jax.nn.gelu(gr)
    mixed = jnp.concatenate([o_c.reshape(b, l, -1), o_d], axis=-1)
    return mixed, k_keep, v_keep, h[:, -1], conv_new


TM = 512


def _pad_cols(w, n):
    return jnp.pad(w, ((0, 0), (0, n - w.shape[1])))


def kernel(x_prompt, x_sample, cache_k_moba, cache_v_moba, state_gdn, state_gdn_conv, cache_k_swa, cache_v_swa, state_lru, state_lru_conv, page_table, w_in_even, w_out_even, gdn_conv_w, gdn_a_log, gdn_dt_bias, gdn_norm_w, w_in_odd, w_out_odd, swa_sinks, lru_conv_w, lru_conv_b, lru_w_a, lru_b_a, lru_w_x, lru_b_x, lru_lambda, rel_bias, ln_g, ln_b, router_w, router_b, w_gate, w_up, w_down):
    n_pr, seq, d = x_prompt.shape
    n_dec, dec_seq, _ = x_sample.shape
    tp = n_pr * seq
    ts = n_dec * dec_seq
    past_len = page_table.shape[1] * PAGE_SIZE
    bias_t = rel_bias.T
    h = jnp.concatenate([x_prompt.reshape(tp, d), x_sample.reshape(ts, d)], axis=0)

    rw_pad = _pad_cols(router_w, LANES)
    rb_pad = _pad_cols(router_b.reshape(1, -1), LANES)
    wg_b, wu_b, wd_b = w_gate.astype(BF16), w_up.astype(BF16), w_down.astype(BF16)

    p_even = sum(EVEN_SIZES)
    n_even = 3200
    w0 = _pad_cols(w_in_even[0], n_even).astype(BF16)
    proj = _mm(h, w0, TM, 640)
    splits = np.cumsum(EVEN_SIZES)[:-1].tolist()
    pp = [t.reshape(n_pr, seq, -1) for t in jnp.split(proj[:tp, :p_even], splits, axis=-1)]
    ps = [t.reshape(n_dec, dec_seq, -1) for t in jnp.split(proj[tp:, :p_even], splits, axis=-1)]
    moba_k_prompt = pp[1].reshape(n_pr, seq, 1, KV_HEADS_A, HEAD_DIM)
    moba_v_prompt = pp[2].reshape(n_pr, seq, 1, KV_HEADS_A, HEAD_DIM)
    moba_k_sample = ps[1].reshape(n_dec, dec_seq, 1, KV_HEADS_A, HEAD_DIM)
    moba_v_sample = ps[2].reshape(n_dec, dec_seq, 1, KV_HEADS_A, HEAD_DIM)

    o_a_p = _moba_prompt(proj, rel_bias, n_pr, seq, 0, 4, 6).reshape(n_pr, seq, -1)
    mixed_p, gdn_sp, gdn_cp = _x_even_rest(
        pp, n_pr, seq, None, None, jnp.zeros((n_pr, GDN_HEADS, 64, 64), F32),
        jnp.zeros((n_pr, CONV_WIDTH - 1, GDN_CONV_CH), F32),
        gdn_conv_w[0], gdn_a_log[0], gdn_dt_bias[0], gdn_norm_w[0], bias_t, o_a=o_a_p)
    past_k = cache_k_moba[page_table][:, :, :, 0].reshape(n_dec, past_len, KV_HEADS_A, HEAD_DIM)
    past_v = cache_v_moba[page_table][:, :, :, 0].reshape(n_dec, past_len, KV_HEADS_A, HEAD_DIM)
    mixed_s, gdn_ss, gdn_cs = _x_even_rest(
        ps, n_dec, dec_seq, past_k, past_v, state_gdn[0], state_gdn_conv[0],
        gdn_conv_w[0], gdn_a_log[0], gdn_dt_bias[0], gdn_norm_w[0], bias_t)
    mixed = jnp.concatenate([mixed_p.reshape(tp, -1), mixed_s.reshape(ts, -1)], axis=0)
    h = _mm_ln(mixed, w_out_even[0].astype(BF16), h, ln_g[0, 0], ln_b[0, 0], TM)
    h = _moe_ln(h, rw_pad, rb_pad, wg_b[0], wu_b[0], wd_b[0], ln_g[0, 1], ln_b[0, 1], TM)

    p_odd = sum(ODD_SIZES)
    proj = _mm(h, w_in_odd[0].astype(BF16), TM, 896)
    splits = np.cumsum(ODD_SIZES)[:-1].tolist()
    pp = [t.reshape(n_pr, seq, -1) for t in jnp.split(proj[:tp], splits, axis=-1)]
    ps = [t.reshape(n_dec, dec_seq, -1) for t in jnp.split(proj[tp:], splits, axis=-1)]
    mixed_p, swa_kp, swa_vp, lru_p, lru_cp = _x_odd_rest(
        pp, n_pr, seq, None, None, jnp.zeros((n_pr, D_RNN), F32),
        jnp.zeros((n_pr, CONV_WIDTH - 1, D_RNN), F32), 0,
        swa_sinks[0], lru_conv_w[0], lru_conv_b[0], lru_w_a[0], lru_b_a[0], lru_w_x[0], lru_b_x[0],
        lru_lambda[0], bias_t)
    mixed_s, swa_ks, swa_vs, lru_s, lru_cs = _x_odd_rest(
        ps, n_dec, dec_seq, cache_k_swa[0], cache_v_swa[0], state_lru[0], state_lru_conv[0], past_len,
        swa_sinks[0], lru_conv_w[0], lru_conv_b[0], lru_w_a[0], lru_b_a[0], lru_w_x[0], lru_b_x[0],
        lru_lambda[0], bias_t)
    mixed = jnp.concatenate([mixed_p.reshape(tp, -1), mixed_s.reshape(ts, -1)], axis=0)
    h = _mm_ln(mixed, w_out_odd[0].astype(BF16), h, ln_g[1, 0], ln_b[1, 0], TM)
    h = _moe_ln(h, rw_pad, rb_pad, wg_b[1], wu_b[1], wd_b[1], ln_g[1, 1], ln_b[1, 1], TM)

    y_prompt = h[:tp].reshape(n_pr, seq, d)
    y_sample = h[tp:].reshape(n_dec, dec_seq, d)
    return (y_prompt, y_sample, moba_k_prompt, moba_v_prompt, moba_k_sample, moba_v_sample,
            gdn_sp[None], gdn_ss[None], gdn_cp[None], gdn_cs[None],
            swa_kp[None], swa_vp[None], swa_ks[None], swa_vs[None],
            lru_p[None], lru_s[None], lru_cp[None], lru_cs[None])
```

```python
import functools
import math

import jax
import jax.numpy as jnp
import numpy as np
from jax import lax
from jax.experimental import pallas as pl
from jax.experimental.pallas import tpu as pltpu

F32 = jnp.float32
BF16 = jnp.bfloat16

D_MODEL = 1024
HEAD_DIM = 64
N_ATT_HEADS = 8
KV_HEADS_A = 4
KV_HEADS_C = 2
MOBA_BLOCK = 256
MOBA_TOPK = 3
PAGE_SIZE = 128
GDN_HEADS = 8
GDN_CHUNK = 64
GDN_CONV_CH = 1536
CONV_WIDTH = 4
WINDOW = 128
D_RNN = 512
LRU_BLOCKS = 8
LRU_C = 8.0
T5_BUCKETS = 32
T5_MAX_DIST = 128
N_EXPERTS = 16
N_GROUPS = 4
EXPERTS_PER_GROUP = 4
D_EXPERT = 256
DEPTH = 2
DEEPNORM_ALPHA = (2 * DEPTH) ** 0.25
LN_EPS = 1e-5
NORM_EPS = 1e-6
EVEN_SIZES = (512, 256, 256, 1536, 512, 8, 8)
ODD_SIZES = (512, 128, 128, 512, 512)

LANES = 128
VMEM_LIMIT = 56 * 1024 * 1024
NEG_INF = float("-inf")


def _cparams(sem):
    return pltpu.CompilerParams(dimension_semantics=sem, vmem_limit_bytes=VMEM_LIMIT)


def _dot(a, b):
    return jnp.dot(a.astype(BF16), b.astype(BF16), preferred_element_type=F32)


def _dot_nt(a, b):
    return lax.dot_general(a.astype(BF16), b.astype(BF16), (((1,), (1,)), ((), ())),
                           preferred_element_type=F32)


def _mm_kernel(x_ref, w_ref, o_ref):
    o_ref[...] = jnp.dot(x_ref[...].astype(BF16), w_ref[...], preferred_element_type=F32)


def _mm(x, w, tm, tn):
    m, k = x.shape
    n = w.shape[1]
    assert m % tm == 0 and n % tn == 0
    return pl.pallas_call(
        _mm_kernel,
        grid=(m // tm, n // tn),
        in_specs=[pl.BlockSpec((tm, k), lambda i, j: (i, 0)),
                  pl.BlockSpec((k, tn), lambda i, j: (0, j))],
        out_specs=pl.BlockSpec((tm, tn), lambda i, j: (i, j)),
        out_shape=jax.ShapeDtypeStruct((m, n), F32),
        compiler_params=_cparams(("parallel", "arbitrary")),
        name="in_proj",
    )(x, w)


def _layer_norm_rows(z, g, b):
    mu = jnp.mean(z, axis=-1, keepdims=True)
    zc = z - mu
    var = jnp.mean(zc * zc, axis=-1, keepdims=True)
    return zc * lax.rsqrt(var + LN_EPS) * g + b


def _mm_ln_kernel(xa_ref, xb_ref, wa_ref, wb_ref, h_ref, g_ref, b_ref, o_ref):
    f = (jnp.dot(xa_ref[...].astype(BF16), wa_ref[...], preferred_element_type=F32)
         + jnp.dot(xb_ref[...].astype(BF16), wb_ref[...], preferred_element_type=F32))
    o_ref[...] = _layer_norm_rows(DEEPNORM_ALPHA * h_ref[...] + f, g_ref[...], b_ref[...])


def _mm_ln(xa, xb, w, h, g, b, tm):
    m, ka = xa.shape
    kb = xb.shape[1]
    n = w.shape[1]
    assert m % tm == 0 and ka == kb
    return pl.pallas_call(
        _mm_ln_kernel,
        grid=(m // tm,),
        in_specs=[pl.BlockSpec((tm, ka), lambda i: (i, 0)),
                  pl.BlockSpec((tm, kb), lambda i: (i, 0)),
                  pl.BlockSpec((ka, n), lambda i: (0, 0)),
                  pl.BlockSpec((kb, n), lambda i: (1, 0)),
                  pl.BlockSpec((tm, n), lambda i: (i, 0)),
                  pl.BlockSpec((1, n), lambda i: (0, 0)),
                  pl.BlockSpec((1, n), lambda i: (0, 0))],
        out_specs=pl.BlockSpec((tm, n), lambda i: (i, 0)),
        out_shape=jax.ShapeDtypeStruct((m, n), F32),
        compiler_params=_cparams(("parallel",)),
        name="out_proj_ln",
    )(xa, xb, w, w, h, g.reshape(1, n), b.reshape(1, n))


def _row_max_first(vals, lane):
    mx = jnp.max(vals, axis=1, keepdims=True)
    idx = jnp.min(jnp.where(vals == mx, lane, LANES), axis=1, keepdims=True)
    return mx, idx


def _moe_ln_kernel(t_ref, rw_ref, rb_ref, wg_ref, wu_ref, wd_ref, g_ref, b_ref, o_ref,
                   gates_ref, tb_ref, acc_ref):
    e = pl.program_id(1)

    @pl.when(e == 0)
    def _route():
        t = t_ref[...]
        logits = jnp.dot(t, rw_ref[...], preferred_element_type=F32,
                         precision=lax.Precision.HIGHEST)
        s = jax.nn.sigmoid(logits)
        sb = s + rb_ref[...]
        lane = lax.broadcasted_iota(jnp.int32, sb.shape, 1)
        grp = lane // EXPERTS_PER_GROUP
        best = None
        gsel = None
        for gi in range(N_GROUPS):
            vals = jnp.where(grp == gi, sb, NEG_INF)
            m1, i1 = _row_max_first(vals, lane)
            m2 = jnp.max(jnp.where(lane == i1, NEG_INF, vals), axis=1, keepdims=True)
            score = m1 + m2
            if gi == 0:
                best, gsel = score, jnp.zeros_like(i1)
            else:
                upd = score > best
                best = jnp.where(upd, score, best)
                gsel = jnp.where(upd, gi, gsel)
        vals = jnp.where(grp == gsel, sb, NEG_INF)
        _, i1 = _row_max_first(vals, lane)
        _, i2 = _row_max_first(jnp.where(lane == i1, NEG_INF, vals), lane)
        s1 = jnp.sum(jnp.where(lane == i1, s, 0.0), axis=1, keepdims=True)
        s2 = jnp.sum(jnp.where(lane == i2, s, 0.0), axis=1, keepdims=True)
        den = s1 + s2
        gates_ref[...] = jnp.where(lane == i1, s1 / den, 0.0) + jnp.where(lane == i2, s2 / den, 0.0)
        tb_ref[...] = t.astype(BF16)
        acc_ref[...] = jnp.zeros_like(acc_ref)

    tb = tb_ref[...]
    hg = jnp.dot(tb, wg_ref[...], preferred_element_type=F32)
    hu = jnp.dot(tb, wu_ref[...], preferred_element_type=F32)
    gates = gates_ref[...]
    lane = lax.broadcasted_iota(jnp.int32, gates.shape, 1)
    ge = jnp.sum(jnp.where(lane == e, gates, 0.0), axis=1, keepdims=True)
    h = (hg * jax.nn.sigmoid(hg)) * hu * ge
    acc_ref[...] += jnp.dot(h.astype(BF16), wd_ref[...], preferred_element_type=F32)

    @pl.when(e == pl.num_programs(1) - 1)
    def _finish():
        o_ref[...] = _layer_norm_rows(DEEPNORM_ALPHA * t_ref[...] + acc_ref[...], g_ref[...], b_ref[...])


def _moe_ln(t, router_w_pad, router_b_pad, wg, wu, wd, g, b, tm):
    m, d = t.shape
    ne, _, f = wg.shape
    assert m % tm == 0
    return pl.pallas_call(
        _moe_ln_kernel,
        grid=(m // tm, ne),
        in_specs=[pl.BlockSpec((tm, d), lambda i, e: (i, 0)),
                  pl.BlockSpec((d, LANES), lambda i, e: (0, 0)),
                  pl.BlockSpec((1, LANES), lambda i, e: (0, 0)),
                  pl.BlockSpec((None, d, f), lambda i, e: (e, 0, 0)),
                  pl.BlockSpec((None, d, f), lambda i, e: (e, 0, 0)),
                  pl.BlockSpec((None, f, d), lambda i, e: (e, 0, 0)),
                  pl.BlockSpec((1, d), lambda i, e: (0, 0)),
                  pl.BlockSpec((1, d), lambda i, e: (0, 0))],
        out_specs=pl.BlockSpec((tm, d), lambda i, e: (i, 0)),
        out_shape=jax.ShapeDtypeStruct((m, d), F32),
        scratch_shapes=[pltpu.VMEM((tm, LANES), F32),
                        pltpu.VMEM((tm, d), BF16),
                        pltpu.VMEM((tm, d), F32)],
        compiler_params=_cparams(("parallel", "arbitrary")),
        name="moe_ln",
    )(t, router_w_pad, router_b_pad, wg, wu, wd, g.reshape(1, d), b.reshape(1, d))


def _t5_bucket_np(rel):
    n = np.maximum(rel, 0)
    max_exact = T5_BUCKETS // 2
    nf = np.maximum(n, 1).astype(np.float32)
    large = max_exact + (np.log(nf / np.float32(max_exact)) / np.float32(math.log(T5_MAX_DIST / max_exact))
                         * np.float32(T5_BUCKETS - max_exact)).astype(np.int32)
    large = np.minimum(large, T5_BUCKETS - 1)
    return np.where(n < max_exact, n, large).astype(np.int32)


def _moba_prompt_kernel(q_ref, k_ref, v_ref, bd_ref, bp_ref, bf_ref, o_ref,
                        kmean_ref, sel_ref, m_ref, l_ref, acc_ref):
    i = pl.program_id(2)
    blk = MOBA_BLOCK
    nblk = k_ref.shape[0] // blk
    hd = HEAD_DIM

    @pl.when(i == 0)
    def _means():
        for n in range(nblk):
            kmean_ref[n:n + 1, :] = jnp.mean(k_ref[n * blk:(n + 1) * blk, :], axis=0, keepdims=True)

    q4 = q_ref[...] * (hd ** -0.5)
    for kvl in range(2):
        lo = kvl * 2 * hd
        q2 = jnp.concatenate([q4[:, lo:lo + hd], q4[:, lo + hd:lo + 2 * hd]], axis=0)
        q2b = q2.astype(BF16)
        kcol = slice(kvl * hd, (kvl + 1) * hd)

        kmean = kmean_ref[:, kcol]
        gate = lax.dot_general(q2, kmean, (((1,), (1,)), ((), ())), preferred_element_type=F32,
                               precision=lax.Precision.HIGHEST)
        col = lax.broadcasted_iota(jnp.int32, gate.shape, 1)
        gate = jnp.where(col < i, gate, NEG_INF)
        sel = jnp.zeros(gate.shape, F32)
        for _ in range(MOBA_TOPK):
            mx = jnp.max(gate, axis=1, keepdims=True)
            idx = jnp.min(jnp.where(gate == mx, col, nblk), axis=1, keepdims=True)
            hit = (col == idx) & (mx > NEG_INF)
            sel = jnp.where(hit, 1.0, sel)
            gate = jnp.where(col == idx, NEG_INF, gate)
        sel_ref[...] = sel

        m_ref[...] = jnp.full(m_ref.shape, NEG_INF, F32)
        l_ref[...] = jnp.zeros(l_ref.shape, F32)
        acc_ref[...] = jnp.zeros(acc_ref.shape, F32)

        def update(s, vb):
            m_old = m_ref[...]
            m_new = jnp.maximum(m_old, jnp.max(s, axis=1, keepdims=True))
            alpha = jnp.exp(m_old - m_new)
            p = jnp.exp(s - m_new)
            l_ref[...] = alpha * l_ref[...] + jnp.sum(p, axis=1, keepdims=True)
            acc_ref[...] = alpha * acc_ref[...] + jnp.dot(p.astype(BF16), vb, preferred_element_type=F32)
            m_ref[...] = m_new

        def tile(j):
            r0 = pl.multiple_of(j * blk, blk)
            kb = k_ref[pl.ds(r0, blk), :][:, kcol].astype(BF16)
            vb = v_ref[pl.ds(r0, blk), :][:, kcol].astype(BF16)
            return _dot_nt(q2b, kb), vb

        s, vb = tile(i)
        update(s + bd_ref[kvl], vb)

        @pl.when(i >= 1)
        def _prev():
            s, vb = tile(i - 1)
            live = jnp.sum(jnp.where(col == i - 1, sel_ref[...], 0.0), axis=1, keepdims=True) > 0.0
            update(jnp.where(live, s + bp_ref[kvl], NEG_INF), vb)

        def far(j, carry):
            s, vb = tile(j)
            live = jnp.sum(jnp.where(col == j, sel_ref[...], 0.0), axis=1, keepdims=True) > 0.0
            rb = jnp.where(live, bf_ref[kvl], NEG_INF)
            update(s + rb, vb)
            return carry

        lax.fori_loop(0, jnp.maximum(i - 1, 0), far, 0)

        o = acc_ref[...] / l_ref[...]
        o_ref[:, lo:lo + hd] = o[:blk]
        o_ref[:, lo + hd:lo + 2 * hd] = o[blk:]


def _moba_prompt(proj, bias_rel, n_batch, seq, q_col, k_col, v_col):
    blk = MOBA_BLOCK
    nblk = seq // blk
    iq = np.arange(blk)[:, None]
    ik = np.arange(blk)[None, :]
    idx_diag = _t5_bucket_np(iq - ik)
    idx_prev = _t5_bucket_np(blk + iq - ik)
    bt = bias_rel.T
    diag = jnp.where(jnp.asarray(iq >= ik)[None], bt[:, idx_diag], NEG_INF)
    prev = bt[:, idx_prev]
    far = jnp.broadcast_to(bt[:, T5_BUCKETS - 1][:, None, None], (N_ATT_HEADS, blk, 1))
    diag = diag.reshape(KV_HEADS_A, 2 * blk, blk)
    prev = prev.reshape(KV_HEADS_A, 2 * blk, blk)
    far = far.reshape(KV_HEADS_A, 2 * blk, 1)
    rows_per_b = seq // blk
    return pl.pallas_call(
        _moba_prompt_kernel,
        grid=(n_batch, 2, nblk),
        in_specs=[pl.BlockSpec((blk, 2 * LANES), lambda b, p, i: (b * rows_per_b + i, q_col // 2 + p)),
                  pl.BlockSpec((seq, LANES), lambda b, p, i: (b, k_col + p)),
                  pl.BlockSpec((seq, LANES), lambda b, p, i: (b, v_col + p)),
                  pl.BlockSpec((2, 2 * blk, blk), lambda b, p, i: (p, 0, 0)),
                  pl.BlockSpec((2, 2 * blk, blk), lambda b, p, i: (p, 0, 0)),
                  pl.BlockSpec((2, 2 * blk, 1), lambda b, p, i: (p, 0, 0))],
        out_specs=pl.BlockSpec((blk, 2 * LANES), lambda b, p, i: (b * rows_per_b + i, p)),
        out_shape=jax.ShapeDtypeStruct((n_batch * seq, 4 * LANES), F32),
        scratch_shapes=[pltpu.VMEM((nblk, LANES), F32),
                        pltpu.VMEM((2 * blk, nblk), F32),
                        pltpu.VMEM((2 * blk, 1), F32),
                        pltpu.VMEM((2 * blk, 1), F32),
                        pltpu.VMEM((2 * blk, HEAD_DIM), F32)],
        compiler_params=_cparams(("parallel", "parallel", "arbitrary")),
        name="moba_prompt",
    )(proj, proj, proj, diag, prev, far)


MOBA_DEC_BLOCKS_PER_STEP = 4
PAGES_PER_BLOCK = MOBA_BLOCK // PAGE_SIZE


def _moba_sample_kernel(pt_ref, q_ref, kn_ref, vn_ref, b31_ref, bfar_ref, bown_ref, *rest,
                        n_blocks):
    nb = MOBA_DEC_BLOCKS_PER_STEP
    npg = nb * PAGES_PER_BLOCK
    k_pages = rest[:npg]
    v_pages = rest[npg:2 * npg]
    o_ref = rest[2 * npg]
    g_ref, m_ref, l_ref, part_ref = rest[2 * npg + 1:]
    s = pl.program_id(1)
    q = q_ref[...]
    qb = q.astype(BF16)
    lane = lax.broadcasted_iota(jnp.int32, g_ref.shape, 1)

    @pl.when(s == 0)
    def _init():
        g_ref[...] = jnp.zeros_like(g_ref)
        m_ref[...] = jnp.zeros_like(m_ref)
        l_ref[...] = jnp.zeros_like(l_ref)

    for j in range(nb):
        n = s * nb + j
        kblk = jnp.concatenate([k_pages[j * PAGES_PER_BLOCK + t][...] for t in range(PAGES_PER_BLOCK)], axis=0)
        vblk = jnp.concatenate([v_pages[j * PAGES_PER_BLOCK + t][...] for t in range(PAGES_PER_BLOCK)], axis=0)
        kmean = jnp.mean(kblk, axis=0, keepdims=True)
        gate = jnp.sum(q * kmean, axis=1, keepdims=True)
        sc = _dot_nt(qb, kblk)
        sc = sc + jnp.where(n == n_blocks - 1, b31_ref[...], bfar_ref[...])
        mx = jnp.max(sc, axis=1, keepdims=True)
        p = jnp.exp(sc - mx)
        hit = lane == n
        g_ref[...] = jnp.where(hit, gate, g_ref[...])
        m_ref[...] = jnp.where(hit, mx, m_ref[...])
        l_ref[...] = jnp.where(hit, jnp.sum(p, axis=1, keepdims=True), l_ref[...])
        part_ref[n] = _dot(p, vblk)

    @pl.when(s == pl.num_programs(1) - 1)
    def _merge():
        gate = jnp.where(lane < n_blocks, g_ref[...], NEG_INF)
        sel = jnp.zeros(gate.shape, jnp.bool_)
        for _ in range(MOBA_TOPK):
            mxg = jnp.max(gate, axis=1, keepdims=True)
            idx = jnp.min(jnp.where(gate == mxg, lane, LANES), axis=1, keepdims=True)
            hit = (lane == idx) & (mxg > NEG_INF)
            sel = sel | hit
            gate = jnp.where(lane == idx, NEG_INF, gate)
        s_own = _dot_nt(qb, kn_ref[...]) + bown_ref[...]
        m_sel = jnp.where(sel, m_ref[...], NEG_INF)
        m_all = jnp.maximum(jnp.max(m_sel, axis=1, keepdims=True), jnp.max(s_own, axis=1, keepdims=True))
        w = jnp.where(sel, jnp.exp(m_sel - m_all), 0.0)
        p_own = jnp.exp(s_own - m_all)
        den = jnp.sum(w * l_ref[...], axis=1, keepdims=True) + jnp.sum(p_own, axis=1, keepdims=True)
        acc = _dot(p_own, vn_ref[...])
        for n in range(n_blocks):
            wn = jnp.sum(jnp.where(lane == n, w, 0.0), axis=1, keepdims=True)
            acc = acc + wn * part_ref[n]
        o_ref[...] = acc / den


def _moba_sample(q_s, k_new, v_new, cache_k, cache_v, page_table, bias_rel):
    n_dec, t_dec = q_s.shape[:2]
    n_pages = page_table.shape[1]
    n_blocks = n_pages // PAGES_PER_BLOCK
    nb = MOBA_DEC_BLOCKS_PER_STEP
    npg = nb * PAGES_PER_BLOCK
    assert n_blocks % nb == 0 and n_blocks <= LANES and t_dec <= 8
    rows = N_ATT_HEADS * t_dec
    kv_w = KV_HEADS_A * HEAD_DIM
    head_kv = np.arange(N_ATT_HEADS) // (N_ATT_HEADS // KV_HEADS_A)
    qh = jnp.transpose(q_s, (0, 2, 1, 3)) * (HEAD_DIM ** -0.5)
    onehot = jnp.asarray(np.eye(KV_HEADS_A, dtype=np.float32)[head_kv])
    qbd = (qh[:, :, :, None, :] * onehot[None, :, None, :, None]).reshape(n_dec, rows, kv_w)
    pad = 8 - t_dec
    kn = jnp.pad(k_new, ((0, 0), (0, pad), (0, 0)))
    vn = jnp.pad(v_new, ((0, 0), (0, pad), (0, 0)))
    bt = bias_rel.T
    past = n_pages * PAGE_SIZE
    tq = np.arange(t_dec)[None, :, None]
    ik = np.arange(MOBA_BLOCK)[None, None, :]
    idx31 = _t5_bucket_np(past + tq - ((n_blocks - 1) * MOBA_BLOCK + ik))[0]
    b31 = bt[:, idx31].reshape(rows, MOBA_BLOCK)
    bfar = jnp.broadcast_to(bt[:, T5_BUCKETS - 1][:, None, None], (N_ATT_HEADS, t_dec, 1)).reshape(rows, 1)
    tk = np.arange(8)[None, :]
    rel_own = np.arange(t_dec)[:, None] - tk
    ok_own = (rel_own >= 0) & (tk < t_dec)
    bown = jnp.where(jnp.asarray(ok_own)[None], bt[:, _t5_bucket_np(rel_own)], NEG_INF).reshape(rows, 8)

    def page_spec(k):
        return pl.BlockSpec((None, PAGE_SIZE, kv_w), lambda b, s, pt, k=k: (pt[b, s * npg + k], 0, 0))

    const2 = lambda b, s, pt: (0, 0)
    grid_spec = pltpu.PrefetchScalarGridSpec(
        num_scalar_prefetch=1,
        grid=(n_dec, n_blocks // nb),
        in_specs=[pl.BlockSpec((None, rows, kv_w), lambda b, s, pt: (b, 0, 0)),
                  pl.BlockSpec((None, 8, kv_w), lambda b, s, pt: (b, 0, 0)),
                  pl.BlockSpec((None, 8, kv_w), lambda b, s, pt: (b, 0, 0)),
                  pl.BlockSpec((rows, MOBA_BLOCK), const2),
                  pl.BlockSpec((rows, 1), const2),
                  pl.BlockSpec((rows, 8), const2)]
                 + [page_spec(k) for k in range(npg)] + [page_spec(k) for k in range(npg)],
        out_specs=pl.BlockSpec((None, rows, kv_w), lambda b, s, pt: (b, 0, 0)),
        scratch_shapes=[pltpu.VMEM((rows, LANES), F32),
                        pltpu.VMEM((rows, LANES), F32),
                        pltpu.VMEM((rows, LANES), F32),
                        pltpu.VMEM((n_blocks, rows, kv_w), F32)],
    )
    out = pl.pallas_call(
        functools.partial(_moba_sample_kernel, n_blocks=n_blocks),
        grid_spec=grid_spec,
        out_shape=jax.ShapeDtypeStruct((n_dec, rows, kv_w), F32),
        compiler_params=_cparams(("parallel", "arbitrary")),
        name="moba_sample",
    )(page_table, qbd, kn, vn, b31, bfar, bown, *([cache_k] * npg), *([cache_v] * npg))
    out = out.reshape(n_dec, N_ATT_HEADS, t_dec, KV_HEADS_A, HEAD_DIM)
    out = out[:, np.arange(N_ATT_HEADS), :, head_kv, :]
    return jnp.transpose(out, (1, 2, 0, 3)).reshape(n_dec, t_dec, N_ATT_HEADS * HEAD_DIM)


SWA_GROUP = N_ATT_HEADS // KV_HEADS_C


def _sink_softmax_pv(parts, sink):
    m = sink
    for s, _ in parts:
        m = jnp.maximum(m, jnp.max(s, axis=1, keepdims=True))
    den = jnp.exp(sink - m)
    ps = []
    for s, _ in parts:
        p = jnp.exp(s - m)
        den = den + jnp.sum(p, axis=1, keepdims=True)
        ps.append(p)
    o = None
    for p, (_, v) in zip(ps, parts):
        t = _dot(p / den, v)
        o = t if o is None else o + t
    return o


def _swa_prompt_kernel(q_ref, kp_ref, ko_ref, vp_ref, vo_ref, bias_ref, sink_ref, o_ref):
    i = pl.program_id(1)
    w, hd, g = WINDOW, HEAD_DIM, SWA_GROUP
    q = q_ref[...] * (hd ** -0.5)
    kp, ko, vp, vo = kp_ref[...], ko_ref[...], vp_ref[...], vo_ref[...]
    for c in range(KV_HEADS_C):
        qs = jnp.concatenate([q[:, (c * g + j) * hd:(c * g + j + 1) * hd] for j in range(g)], axis=0)
        kcol = slice(c * hd, (c + 1) * hd)
        kcat = jnp.concatenate([kp[:, kcol], ko[:, kcol]], axis=0)
        vcat = jnp.concatenate([vp[:, kcol], vo[:, kcol]], axis=0)
        s = _dot_nt(qs, kcat) + bias_ref[c]
        col = lax.broadcasted_iota(jnp.int32, s.shape, 1)
        s = jnp.where((col >= w) | (i > 0), s, NEG_INF)
        o = _sink_softmax_pv([(s, vcat)], sink_ref[c])
        for j in range(g):
            o_ref[:, (c * g + j) * hd:(c * g + j + 1) * hd] = o[j * w:(j + 1) * w]


def _swa_prompt(proj, bias_rel, sinks, n_batch, seq, q_col, k_col, v_col):
    w, g = WINDOW, SWA_GROUP
    nb = seq // w
    iq = np.arange(w)[:, None]
    ik = np.arange(2 * w)[None, :]
    rel = iq + w - ik
    bt = bias_rel.T
    band = jnp.where(jnp.asarray((rel >= 0) & (rel <= w))[None], bt[:, _t5_bucket_np(rel)], NEG_INF)
    band = band.reshape(KV_HEADS_C, g * w, 2 * w)
    sink = jnp.broadcast_to(sinks[:, None, None], (N_ATT_HEADS, w, 1)).reshape(KV_HEADS_C, g * w, 1)
    own = lambda c: (lambda b, i: (b * nb + i, c))
    prev = lambda c: (lambda b, i: (b * nb + jnp.maximum(i - 1, 0), c))
    return pl.pallas_call(
        _swa_prompt_kernel,
        grid=(n_batch, nb),
        in_specs=[pl.BlockSpec((w, 4 * LANES), lambda b, i: (b * nb + i, q_col // 4)),
                  pl.BlockSpec((w, LANES), prev(k_col)),
                  pl.BlockSpec((w, LANES), own(k_col)),
                  pl.BlockSpec((w, LANES), prev(v_col)),
                  pl.BlockSpec((w, LANES), own(v_col)),
                  pl.BlockSpec((KV_HEADS_C, g * w, 2 * w), lambda b, i: (0, 0, 0)),
                  pl.BlockSpec((KV_HEADS_C, g * w, 1), lambda b, i: (0, 0, 0))],
        out_specs=pl.BlockSpec((w, 4 * LANES), lambda b, i: (b * nb + i, 0)),
        out_shape=jax.ShapeDtypeStruct((n_batch * seq, 4 * LANES), F32),
        compiler_params=_cparams(("parallel", "parallel")),
        name="swa_prompt",
    )(proj, proj, proj, proj, proj, band, sink)


SWA_DEC_SEQS_PER_STEP = 8


def _swa_sample_kernel(q_ref, kb_ref, vb_ref, kn_ref, vn_ref, bbuf_ref, bnew_ref, sink_ref, o_ref):
    for i in range(SWA_DEC_SEQS_PER_STEP):
        q = q_ref[i]
        s_buf = _dot_nt(q, kb_ref[i]) + bbuf_ref[...]
        s_new = _dot_nt(q, kn_ref[i]) + bnew_ref[...]
        o_ref[i] = _sink_softmax_pv([(s_buf, vb_ref[i]), (s_new, vn_ref[i])], sink_ref[...])


def _swa_sample(q_s, k_new, v_new, buf_k, buf_v, bias_rel, sinks):
    n_dec, t_dec = q_s.shape[:2]
    nbuf = buf_k.shape[1]
    rows = N_ATT_HEADS * t_dec
    kv_w = KV_HEADS_C * HEAD_DIM
    sb = SWA_DEC_SEQS_PER_STEP
    assert n_dec % sb == 0 and t_dec <= 8
    head_kv = np.arange(N_ATT_HEADS) // SWA_GROUP
    qh = jnp.transpose(q_s, (0, 2, 1, 3)) * (HEAD_DIM ** -0.5)
    onehot = jnp.asarray(np.eye(KV_HEADS_C, dtype=np.float32)[head_kv])
    qbd = (qh[:, :, :, None, :] * onehot[None, :, None, :, None]).reshape(n_dec, rows, kv_w)
    pad = 8 - t_dec
    kn = jnp.pad(k_new, ((0, 0), (0, pad), (0, 0)))
    vn = jnp.pad(v_new, ((0, 0), (0, pad), (0, 0)))
    bt = bias_rel.T
    t = np.arange(t_dec)[:, None]
    rel_buf = nbuf + t - np.arange(nbuf)[None, :]
    ok_buf = (rel_buf >= 0) & (rel_buf <= WINDOW)
    bbuf = jnp.where(jnp.asarray(ok_buf)[None], bt[:, _t5_bucket_np(rel_buf)], NEG_INF).reshape(rows, nbuf)
    tk = np.arange(8)[None, :]
    rel_new = t - tk
    ok_new = (rel_new >= 0) & (tk < t_dec)
    bnew = jnp.where(jnp.asarray(ok_new)[None], bt[:, _t5_bucket_np(rel_new)], NEG_INF).reshape(rows, 8)
    sink = jnp.broadcast_to(sinks[:, None, None], (N_ATT_HEADS, t_dec, 1)).reshape(rows, 1)
    seq3 = lambda r, c: pl.BlockSpec((sb, r, c), lambda i: (i, 0, 0))
    const2 = lambda r, c: pl.BlockSpec((r, c), lambda i: (0, 0))
    out = pl.pallas_call(
        _swa_sample_kernel,
        grid=(n_dec // sb,),
        in_specs=[seq3(rows, kv_w), seq3(nbuf, kv_w), seq3(nbuf, kv_w), seq3(8, kv_w), seq3(8, kv_w),
                  const2(rows, nbuf), const2(rows, 8), const2(rows, 1)],
        out_specs=seq3(rows, kv_w),
        out_shape=jax.ShapeDtypeStruct((n_dec, rows, kv_w), F32),
        compiler_params=_cparams(("parallel",)),
        name="swa_sample",
    )(qbd, buf_k, buf_v, kn, vn, bbuf, bnew, sink)
    out = out.reshape(n_dec, N_ATT_HEADS, t_dec, KV_HEADS_C, HEAD_DIM)
    out = out[:, np.arange(N_ATT_HEADS), :, head_kv, :]
    return jnp.transpose(out, (1, 2, 0, 3)).reshape(n_dec, t_dec, N_ATT_HEADS * HEAD_DIM)


def _softplus(x):
    return jnp.maximum(x, 0.0) + jnp.log1p(jnp.exp(-jnp.abs(x)))


def _neg_expm1(y):
    return -jnp.tanh(0.5 * y) * (jnp.exp(y) + 1.0)


def _shifted_rows(prev8, cur, k):
    if k == 0:
        return cur
    ext = jnp.concatenate([prev8, cur], axis=0)
    return pltpu.roll(ext, k, axis=0)[8:]


def _causal_conv_rows(prev8, cur, w):
    y = cur * w[CONV_WIDTH - 1:CONV_WIDTH, :]
    for k in range(1, CONV_WIDTH):
        y = y + _shifted_rows(prev8, cur, k) * w[CONV_WIDTH - 1 - k:CONV_WIDTH - k, :]
    return y


def _lru_gates(xc, wa, ba, wx, bx, sp):
    r = jax.nn.sigmoid(_dot(xc, wa) + ba)
    i = jax.nn.sigmoid(_dot(xc, wx) + bx)
    log_a = -LRU_C * r * sp
    a = jnp.exp(log_a)
    b = jnp.sqrt(_neg_expm1(2.0 * log_a)) * (i * xc)
    return a, b


LRU_ROWS = 256


def _lru_prompt_kernel(x_ref, gr_ref, cw_ref, cb_ref, wa_ref, ba_ref, wx_ref, bx_ref, lam_ref,
                       o_ref, hl_ref):
    n = x_ref.shape[0] // LRU_ROWS
    sp = _softplus(-lam_ref[...])
    row = lax.broadcasted_iota(jnp.int32, (LRU_ROWS, LANES), 0)

    def body(c, carry):
        prev8, h = carry
        r0 = pl.multiple_of(c * LRU_ROWS, LRU_ROWS)
        cur = x_ref[pl.ds(r0, LRU_ROWS), :]
        xc = _causal_conv_rows(prev8, cur, cw_ref[...]) + cb_ref[...]
        a, b = _lru_gates(xc, wa_ref[...], ba_ref[...], wx_ref[...], bx_ref[...], sp)
        d = 1
        while d < LRU_ROWS:
            a_s = jnp.where(row >= d, pltpu.roll(a, d, axis=0), 1.0)
            b_s = jnp.where(row >= d, pltpu.roll(b, d, axis=0), 0.0)
            b = a * b_s + b
            a = a * a_s
            d *= 2
        hs = a * h + b
        o_ref[pl.ds(r0, LRU_ROWS), :] = hs * jax.nn.gelu(gr_ref[pl.ds(r0, LRU_ROWS), :])
        return cur[LRU_ROWS - 8:], hs[LRU_ROWS - 1:]

    _, h = lax.fori_loop(0, n, body, (jnp.zeros((8, LANES), F32), jnp.zeros((1, LANES), F32)))
    hl_ref[...] = h


def _lru_prompt(proj, n_batch, seq, x_col, g_col, conv_w, conv_b, wa_bd, b_a, wx_bd, b_x, lam):
    ncb = D_RNN // LANES
    vec = lambda v: v.reshape(1, D_RNN)
    vspec = pl.BlockSpec((1, LANES), lambda b, c: (0, c))
    out, hl = pl.pallas_call(
        _lru_prompt_kernel,
        grid=(n_batch, ncb),
        in_specs=[pl.BlockSpec((seq, LANES), lambda b, c: (b, x_col + c)),
                  pl.BlockSpec((seq, LANES), lambda b, c: (b, g_col + c)),
                  pl.BlockSpec((CONV_WIDTH, LANES), lambda b, c: (0, c)),
                  vspec,
                  pl.BlockSpec((LANES, LANES), lambda b, c: (c, c)), vspec,
                  pl.BlockSpec((LANES, LANES), lambda b, c: (c, c)), vspec,
                  vspec],
        out_specs=[pl.BlockSpec((seq, LANES), lambda b, c: (b, c)),
                   pl.BlockSpec((None, 1, LANES), lambda b, c: (b, 0, c))],
        out_shape=[jax.ShapeDtypeStruct((n_batch * seq, D_RNN), F32),
                   jax.ShapeDtypeStruct((n_batch, 1, D_RNN), F32)],
        compiler_params=_cparams(("parallel", "parallel")),
        name="lru_prompt",
    )(proj, proj, conv_w, vec(conv_b), wa_bd, vec(b_a), wx_bd, vec(b_x), vec(lam))
    return out, hl.reshape(n_batch, D_RNN)


def _lru_sample_kernel(xp_ref, gr_ref, h0_ref, cw_ref, cb_ref, wa_ref, ba_ref, wx_ref, bx_ref, lam_ref,
                       o_ref, hl_ref):
    t_dec = gr_ref.shape[0]
    sp = _softplus(-lam_ref[...])
    h = h0_ref[...]
    for t in range(t_dec):
        xc = cb_ref[...]
        for j in range(CONV_WIDTH):
            xc = xc + xp_ref[t + j] * cw_ref[j:j + 1, :]
        a, b = _lru_gates(xc, wa_ref[...], ba_ref[...], wx_ref[...], bx_ref[...], sp)
        h = a * h + b
        o_ref[t] = h * jax.nn.gelu(gr_ref[t])
    hl_ref[...] = h


def _lru_sample(xr, gr, h0, conv0, conv_w, conv_b, wa_bd, b_a, wx_bd, b_x, lam):
    n_dec, t_dec, _ = xr.shape
    xp = jnp.transpose(jnp.concatenate([conv0, xr], axis=1), (1, 0, 2))
    grt = jnp.transpose(gr, (1, 0, 2))
    vec = lambda v: v.reshape(1, D_RNN)
    full = lambda *s: pl.BlockSpec(s, lambda i: (0,) * len(s))
    out, hl = pl.pallas_call(
        _lru_sample_kernel,
        grid=(1,),
        in_specs=[full(t_dec + CONV_WIDTH - 1, n_dec, D_RNN), full(t_dec, n_dec, D_RNN), full(n_dec, D_RNN),
                  full(CONV_WIDTH, D_RNN), full(1, D_RNN), full(D_RNN, D_RNN), full(1, D_RNN),
                  full(D_RNN, D_RNN), full(1, D_RNN), full(1, D_RNN)],
        out_specs=[full(t_dec, n_dec, D_RNN), full(n_dec, D_RNN)],
        out_shape=[jax.ShapeDtypeStruct((t_dec, n_dec, D_RNN), F32),
                   jax.ShapeDtypeStruct((n_dec, D_RNN), F32)],
        compiler_params=_cparams(("arbitrary",)),
        name="lru_sample",
    )(xp, grt, h0, conv_w, vec(conv_b), wa_bd, vec(b_a), wx_bd, vec(b_x), vec(lam))
    return jnp.transpose(out, (1, 0, 2)), hl


def _block_diag(w):
    n, k, _ = w.shape
    eye = jnp.asarray(np.eye(n, dtype=np.float32))
    return (w[:, :, None, :] * eye[:, None, :, None]).reshape(n * k, n * k)


GDN_ROWS = 256


def _bmm(a, b):
    return jnp.einsum('nij,njk->nik', a.astype(BF16), b.astype(BF16), preferred_element_type=F32)


def _bmm_nt(a, b):
    return jnp.einsum('nid,njd->nij', a.astype(BF16), b.astype(BF16), preferred_element_type=F32)


def _col(x, lane, idx):
    return jnp.sum(jnp.where(lane == idx, x, 0.0), axis=1, keepdims=True)


def _gdn_gates_kernel(x_ref, a_ref, dt_ref, beta_ref, gc_ref, *, chunk, pad_period, pad_rows):
    x = x_ref[...]
    tm = x.shape[0]
    beta = jax.nn.sigmoid(x)
    g = -jnp.exp(a_ref[...]) * _softplus(x + dt_ref[...])
    if pad_period:
        row = lax.broadcasted_iota(jnp.int32, x.shape, 0)
        valid = (row & (pad_period - 1)) >= pad_rows
        beta = jnp.where(valid, beta, 0.0)
        g = jnp.where(valid, g, 0.0)
    r = lax.broadcasted_iota(jnp.int32, (tm, tm), 0)
    c = lax.broadcasted_iota(jnp.int32, (tm, tm), 1)
    sh = chunk.bit_length() - 1
    tri = jnp.where(((r >> sh) == (c >> sh)) & (c <= r), 1.0, 0.0)
    beta_ref[...] = beta
    gc_ref[...] = jnp.dot(tri, g, preferred_element_type=F32, precision=lax.Precision.HIGHEST)


def _gdn_gates(x, col, rows, a_log, dt_bias, chunk, pad_period=0, pad_rows=0):
    tm = GDN_ROWS
    assert rows % tm == 0 and tm % chunk == 0
    a_pad = jnp.pad(a_log.reshape(1, -1), ((0, 0), (GDN_HEADS, LANES - 2 * GDN_HEADS)))
    dt_pad = jnp.pad(dt_bias.reshape(1, -1), ((0, 0), (GDN_HEADS, LANES - 2 * GDN_HEADS)))
    return pl.pallas_call(
        functools.partial(_gdn_gates_kernel, chunk=chunk, pad_period=pad_period, pad_rows=pad_rows),
        grid=(rows // tm,),
        in_specs=[pl.BlockSpec((tm, LANES), lambda i: (i, col)),
                  pl.BlockSpec((1, LANES), lambda i: (0, 0)),
                  pl.BlockSpec((1, LANES), lambda i: (0, 0))],
        out_specs=[pl.BlockSpec((tm, LANES), lambda i: (i, 0))] * 2,
        out_shape=[jax.ShapeDtypeStruct((rows, LANES), F32)] * 2,
        compiler_params=_cparams(("parallel",)),
        name="gdn_gates",
    )(x, a_pad, dt_pad)


def _gdn_prep_kernel(x_ref, cw_ref, beta_ref, gc_ref, *out_refs, mode):
    hd = HEAD_DIM
    n = x_ref.shape[0] // GDN_ROWS
    h0 = 2 * pl.program_id(1)
    lane = lax.broadcasted_iota(jnp.int32, (GDN_ROWS, LANES), 1)
    lo = lane < hd

    def halves(c0, c1):
        return jnp.where(lo, c0, c1)

    def put(ref, r0, val):
        ref[0, pl.ds(r0, GDN_ROWS), :] = val[:, :hd]
        ref[1, pl.ds(r0, GDN_ROWS), :] = val[:, hd:]

    def body(c, prev8):
        r0 = pl.multiple_of(c * GDN_ROWS, GDN_ROWS)
        cur = x_ref[pl.ds(r0, GDN_ROWS), :]
        y = _causal_conv_rows(prev8, cur, cw_ref[...])
        y = y * jax.nn.sigmoid(y)
        if mode in ("q", "k"):
            ss = y * y
            s0 = jnp.sum(jnp.where(lo, ss, 0.0), axis=1, keepdims=True)
            s1 = jnp.sum(jnp.where(lo, 0.0, ss), axis=1, keepdims=True)
            y = y * halves(lax.rsqrt(s0 + NORM_EPS), lax.rsqrt(s1 + NORM_EPS))
        if mode == "q":
            put(out_refs[0], r0, y * (hd ** -0.5))
        else:
            beta = beta_ref[pl.ds(r0, GDN_ROWS), :]
            bb = halves(_col(beta, lane, h0), _col(beta, lane, h0 + 1))
            if mode == "k":
                gc = gc_ref[pl.ds(r0, GDN_ROWS), :]
                put(out_refs[0], r0, y)
                put(out_refs[1], r0, y * bb)
                put(out_refs[2], r0, halves(_col(gc, lane, GDN_HEADS + h0), _col(gc, lane, GDN_HEADS + h0 + 1)))
            else:
                put(out_refs[0], r0, y * bb)
        return cur[GDN_ROWS - 8:]

    lax.fori_loop(0, n, body, jnp.zeros((8, LANES), F32))


def _gdn_prep(x, x_col, conv_w, beta, gc, n_batch, seq, mode):
    sec = {"q": 0, "k": 4, "v": 8}[mode]
    n_out = {"q": 1, "k": 3, "v": 1}[mode]
    hspec = pl.BlockSpec((2, seq, HEAD_DIM), lambda b, c: (b * 4 + c, 0, 0))
    gspec = pl.BlockSpec((seq, LANES), lambda b, c: (b, 0))
    outs = pl.pallas_call(
        functools.partial(_gdn_prep_kernel, mode=mode),
        grid=(n_batch, 4),
        in_specs=[pl.BlockSpec((seq, LANES), lambda b, c: (b, x_col + c)),
                  pl.BlockSpec((CONV_WIDTH, LANES), lambda b, c: (0, sec + c)),
                  gspec, gspec],
        out_specs=[hspec] * n_out,
        out_shape=[jax.ShapeDtypeStruct((n_batch * GDN_HEADS, seq, HEAD_DIM), F32)] * n_out,
        compiler_params=_cparams(("parallel", "parallel")),
        name="gdn_prep_" + mode,
    )(x, conv_w, beta, gc)
    return outs


def _gdn_intra_kernel(q_ref, k_ref, kb_ref, vb_ref, gcb_ref, gcr_ref,
                      u_ref, w_ref, qg_ref, kd_ref, qk_ref, gl_ref, *, chunk):
    hd = HEAD_DIM
    rows = q_ref.shape[0]
    n = rows // chunk
    r3 = lambda ref: ref[...].reshape(n, chunk, hd)
    q, k, kb, vb, gcb = r3(q_ref), r3(k_ref), r3(kb_ref), r3(vb_ref), r3(gcb_ref)
    gcr = gcr_ref[...]
    ci = lax.broadcasted_iota(jnp.int32, (n, chunk, chunk), 1)
    si = lax.broadcasted_iota(jnp.int32, (n, chunk, chunk), 2)
    decay = jnp.exp(jnp.where(ci >= si, gcb[:, :, :chunk] - gcr[:, None, :], NEG_INF))
    a = jnp.where(ci > si, _bmm_nt(kb, k) * decay, 0.0)
    b = -a
    y = b
    p = _bmm(b, b)
    stages = chunk.bit_length() - 2
    for t in range(stages):
        y = y + p + _bmm(y, p)
        if t < stages - 1:
            p = _bmm(p, p)
    eg = jnp.exp(gcb)
    rhs = jnp.concatenate([vb, kb * eg], axis=-1)
    sol = rhs + _bmm(y, rhs)
    gl = gcb[:, chunk - 1:chunk, :]
    u_ref[...] = sol[:, :, :hd].reshape(rows, hd)
    w_ref[...] = sol[:, :, hd:].reshape(rows, hd)
    qg_ref[...] = (q * eg).reshape(rows, hd)
    kd_ref[...] = (k * jnp.exp(gl - gcb)).reshape(rows, hd)
    qk_ref[...] = (_bmm_nt(q, k) * decay).reshape(rows, chunk)
    gl_ref[...] = jnp.exp(gl).reshape(n, hd)


def _gdn_intra(q, k, kb, vb, gcb, gc_row, chunk, rows_per_step):
    bh, seq, hd = q.shape
    tr = rows_per_step
    n = tr // chunk
    assert seq % tr == 0
    hspec = pl.BlockSpec((None, tr, hd), lambda h, i: (h, i, 0))
    shape = jax.ShapeDtypeStruct((bh, seq, hd), F32)
    return pl.pallas_call(
        functools.partial(_gdn_intra_kernel, chunk=chunk),
        grid=(bh, seq // tr),
        in_specs=[hspec] * 5 + [pl.BlockSpec((None, n, chunk), lambda h, i: (h, i, 0))],
        out_specs=[hspec] * 4 + [pl.BlockSpec((None, tr, chunk), lambda h, i: (h, i, 0)),
                                 pl.BlockSpec((None, n, hd), lambda h, i: (h, i, 0))],
        out_shape=[shape] * 4 + [jax.ShapeDtypeStruct((bh, seq, chunk), F32),
                                 jax.ShapeDtypeStruct((bh, seq // chunk, hd), F32)],
        compiler_params=_cparams(("parallel", "parallel")),
        name="gdn_intra",
    )(q, k, kb, vb, gcb, gc_row)


def _gdn_chunk_update(s, u, w, qg, kd, qk, gl):
    v_new = u - _bmm(w, s)
    o = _bmm(qg, s) + _bmm(qk, v_new)
    s = s * gl + jnp.einsum('ncd,nce->nde', kd.astype(BF16), v_new.astype(BF16), preferred_element_type=F32)
    return s, o


def _gdn_scan_kernel(u_ref, w_ref, qg_ref, kd_ref, qk_ref, gl_ref, o_ref, sfin_ref, s_scr, *, chunk):
    j = pl.program_id(1)

    @pl.when(j == 0)
    def _zero():
        s_scr[...] = jnp.zeros_like(s_scr)

    n = u_ref.shape[1] // chunk
    s = s_scr[...]
    for c in range(n):
        sl = slice(c * chunk, (c + 1) * chunk)
        s, o = _gdn_chunk_update(s, u_ref[:, sl, :], w_ref[:, sl, :], qg_ref[:, sl, :], kd_ref[:, sl, :],
                                 qk_ref[:, sl, :], gl_ref[:, c:c + 1, :])
        o_ref[:, sl, :] = o
    s_scr[...] = s

    @pl.when(j == pl.num_programs(1) - 1)
    def _final():
        sfin_ref[...] = s


def _gdn_scan(u, w, qg, kd, qk, gl, chunk, heads_per_step, rows_per_step):
    bh, seq, hd = u.shape
    hb, tr = heads_per_step, rows_per_step
    n = tr // chunk
    hspec = pl.BlockSpec((hb, tr, hd), lambda h, i: (h, i, 0))
    return pl.pallas_call(
        functools.partial(_gdn_scan_kernel, chunk=chunk),
        grid=(bh // hb, seq // tr),
        in_specs=[hspec] * 4 + [pl.BlockSpec((hb, tr, chunk), lambda h, i: (h, i, 0)),
                                pl.BlockSpec((hb, n, hd), lambda h, i: (h, i, 0))],
        out_specs=[hspec, pl.BlockSpec((hb, hd, hd), lambda h, i: (h, 0, 0))],
        out_shape=[jax.ShapeDtypeStruct((bh, seq, hd), F32), jax.ShapeDtypeStruct((bh, hd, hd), F32)],
        scratch_shapes=[pltpu.VMEM((hb, hd, hd), F32)],
        compiler_params=_cparams(("parallel", "arbitrary")),
        name="gdn_scan",
    )(u, w, qg, kd, qk, gl)


GDN_DEC_SEQS_PER_STEP = 16


def _gdn_step_kernel(s0_ref, u_ref, w_ref, qg_ref, kd_ref, qk_ref, gl_ref, o_ref, s_ref, *, chunk):
    hd = HEAD_DIM
    n = s0_ref.shape[0]
    r3 = lambda ref, last: ref[...].reshape(n, chunk, last)
    s, o = _gdn_chunk_update(s0_ref[...], r3(u_ref, hd), r3(w_ref, hd), r3(qg_ref, hd), r3(kd_ref, hd),
                             r3(qk_ref, chunk), gl_ref[...][:, None, :])
    o_ref[...] = o.reshape(n * chunk, hd)
    s_ref[...] = s


def _gdn_step(s0, u, w, qg, kd, qk, gl, chunk):
    n_dec, nh, hd, _ = s0.shape
    sb = GDN_DEC_SEQS_PER_STEP
    assert n_dec % sb == 0
    hspec = pl.BlockSpec((None, sb * chunk, hd), lambda h, i: (h, i, 0))
    sspec = pl.BlockSpec((sb, None, hd, hd), lambda h, i: (i, h, 0, 0))
    return pl.pallas_call(
        functools.partial(_gdn_step_kernel, chunk=chunk),
        grid=(nh, n_dec // sb),
        in_specs=[sspec] + [hspec] * 4 + [pl.BlockSpec((None, sb * chunk, chunk), lambda h, i: (h, i, 0)),
                                         pl.BlockSpec((None, sb, hd), lambda h, i: (h, i, 0))],
        out_specs=[hspec, sspec],
        out_shape=[jax.ShapeDtypeStruct((nh, n_dec * chunk, hd), F32),
                   jax.ShapeDtypeStruct((n_dec, nh, hd, hd), F32)],
        compiler_params=_cparams(("parallel", "parallel")),
        name="gdn_step",
    )(s0, u, w, qg, kd, qk, gl)


def _gdn_post_kernel(o_ref, z_ref, w_ref, out_ref):
    hd = HEAD_DIM
    for h in range(GDN_HEADS):
        o = o_ref[h]
        r = o * lax.rsqrt(jnp.mean(o * o, axis=-1, keepdims=True) + NORM_EPS) * w_ref[...]
        z = z_ref[:, h * hd:(h + 1) * hd]
        out_ref[:, h * hd:(h + 1) * hd] = r * (z * jax.nn.sigmoid(z))


def _gdn_post(o, z, z_col, norm_w, n_batch, seq, tm):
    nt = seq // tm
    return pl.pallas_call(
        _gdn_post_kernel,
        grid=(n_batch, nt),
        in_specs=[pl.BlockSpec((GDN_HEADS, tm, HEAD_DIM), lambda b, i: (b, i, 0)),
                  pl.BlockSpec((tm, 4 * LANES), lambda b, i: (b * nt + i, z_col)),
                  pl.BlockSpec((1, HEAD_DIM), lambda b, i: (0, 0))],
        out_specs=pl.BlockSpec((tm, 4 * LANES), lambda b, i: (b * nt + i, 0)),
        out_shape=jax.ShapeDtypeStruct((n_batch * seq, 4 * LANES), F32),
        compiler_params=_cparams(("parallel", "parallel")),
        name="gdn_post",
    )(o, z, norm_w.reshape(1, HEAD_DIM))


def _gdn_front(x, qkv_col, gate_col, conv_w, a_log, dt_bias, n_batch, seq, chunk, pad_period=0, pad_rows=0):
    rows = n_batch * seq
    beta, gc = _gdn_gates(x, gate_col, rows, a_log, dt_bias, chunk, pad_period, pad_rows)
    (q,) = _gdn_prep(x, qkv_col, conv_w, beta, gc, n_batch, seq, "q")
    k, kb, gcb = _gdn_prep(x, qkv_col + 4, conv_w, beta, gc, n_batch, seq, "k")
    (vb,) = _gdn_prep(x, qkv_col + 8, conv_w, beta, gc, n_batch, seq, "v")
    gc_row = jnp.transpose(gc[:, GDN_HEADS:2 * GDN_HEADS].reshape(n_batch, seq, GDN_HEADS), (0, 2, 1))
    gc_row = gc_row.reshape(n_batch * GDN_HEADS, seq // chunk, chunk)
    return _gdn_intra(q, k, kb, vb, gcb, gc_row, chunk, min(seq, 512))


def _x_l2norm(x):
    return x * lax.rsqrt(jnp.sum(x * x, -1, keepdims=True) + NORM_EPS)


def _x_rms(x, w):
    return x * lax.rsqrt(jnp.mean(x * x, -1, keepdims=True) + NORM_EPS) * w


def _x_t5_bucket(rel):
    n = jnp.maximum(rel, 0)
    max_exact = T5_BUCKETS // 2
    nf = jnp.maximum(n, 1).astype(F32)
    large = max_exact + (jnp.log(nf / max_exact) / math.log(T5_MAX_DIST / max_exact)
                         * (T5_BUCKETS - max_exact)).astype(jnp.int32)
    large = jnp.minimum(large, T5_BUCKETS - 1)
    return jnp.where(n < max_exact, n, large)


def _x_causal_conv(x, buf, w, b=None):
    L = x.shape[1]
    xp = jnp.concatenate([buf.astype(x.dtype), x], axis=1)
    y = xp[:, CONV_WIDTH - 1:] * w[CONV_WIDTH - 1]
    for j in range(CONV_WIDTH - 1):
        y = y + xp[:, j:j + L] * w[j]
    if b is not None:
        y = y + b
    return y, xp[:, L:]


def _x_query_chunk(n_q, batch):
    cap = max(1, min(n_q, 128 // batch))
    return max(d for d in range(1, cap + 1) if n_q % d == 0)


def _x_moba_attention(q, k, v, q0, bias_t):
    B, Lq, Hq, hd = q.shape
    Lk, KV = k.shape[1], k.shape[2]
    G = Hq // KV
    n_full = Lk // MOBA_BLOCK
    n_sel = min(MOBA_TOPK, n_full)
    head_kv = jnp.arange(Hq) // G
    kmean = k[:, :n_full * MOBA_BLOCK].reshape(B, n_full, MOBA_BLOCK, KV, hd).mean(2, dtype=F32)[:, :, head_kv]
    qc = _x_query_chunk(Lq, B)
    nq = Lq // qc
    q_chunks = q.reshape(B, nq, qc, Hq, hd).swapaxes(0, 1)
    starts = q0 + qc * jnp.arange(nq, dtype=jnp.int32)
    b_ix = jnp.arange(B)[:, None, None, None, None]
    h_ix = jnp.arange(Hq)[None, :, None, None, None]
    kv_ix = head_kv[None, :, None, None, None]
    offs = jnp.arange(MOBA_BLOCK)
    scale = hd ** -0.5

    def one_chunk(args):
        qb, s0 = args
        q_pos = s0 + jnp.arange(qc)
        own = jnp.broadcast_to((q_pos // MOBA_BLOCK)[None, None, :, None], (B, Hq, qc, 1))
        gate = jnp.einsum('bqhd,bnhd->bhqn', qb, kmean, preferred_element_type=F32)
        gate = jnp.where(jnp.arange(n_full) < own, gate, -jnp.inf)
        _, sel = lax.top_k(gate, n_sel)
        blocks = jnp.concatenate([sel, own], -1)
        live = jnp.concatenate([sel < own, jnp.ones_like(own, dtype=bool)], -1)
        k_pos = blocks[..., None] * MOBA_BLOCK + offs
        rows = jnp.minimum(k_pos, Lk - 1)
        kg = k[b_ix, rows, kv_ix]
        vg = v[b_ix, rows, kv_ix]
        rel = q_pos[None, None, :, None, None] - k_pos
        s = jnp.einsum('bqhd,bhqsrd->bhqsr', qb, kg, preferred_element_type=F32) * scale
        s = s + bias_t[h_ix, _x_t5_bucket(rel)].astype(F32)
        s = jnp.where(live[..., None] & (rel >= 0), s, -jnp.inf)
        n_keys = s.shape[3] * MOBA_BLOCK
        p = jax.nn.softmax(s.reshape(B, Hq, qc, n_keys), axis=-1).reshape(s.shape)
        o = jnp.einsum('bhqsr,bhqsrd->bqhd', p, vg, preferred_element_type=F32)
        return o.astype(q.dtype)

    out = lax.map(one_chunk, (q_chunks, starts))
    return out.swapaxes(0, 1).reshape(B, Lq, Hq, hd)


def _x_gated_delta_rule(q, k, v, g, beta, s0):
    B, L, H, dk = q.shape
    dv = v.shape[-1]
    C = min(GDN_CHUNK, L)
    pad = (-L) % C
    N = (L + pad) // C

    def chunks(t):
        t = jnp.pad(t.astype(F32), [(0, 0), (0, pad)] + [(0, 0)] * (t.ndim - 2))
        t = t.reshape((B, N, C) + t.shape[2:])
        return jnp.swapaxes(jnp.swapaxes(t, 0, 1), 2, 3)

    q = chunks(q) * dk ** -0.5
    k = chunks(k)
    v = chunks(v)
    g = chunks(g)
    beta = chunks(beta)
    gc = jnp.cumsum(g, -1)
    incl = jnp.tril(jnp.ones((C, C), bool))
    strict = jnp.tril(jnp.ones((C, C), bool), -1)
    decay = jnp.exp(jnp.where(incl, gc[..., :, None] - gc[..., None, :], -jnp.inf))
    kb = k * beta[..., None]
    lower = jnp.where(strict, jnp.einsum('nbhcd,nbhsd->nbhcs', kb, k) * decay, 0.0)
    t_mat = lower + jnp.eye(C, dtype=F32)
    rhs = jnp.concatenate([v * beta[..., None], kb * jnp.exp(gc)[..., None]], -1)
    sol = lax.linalg.triangular_solve(t_mat, rhs, left_side=True, lower=True, unit_diagonal=True)
    u, w = sol[..., :dv], sol[..., dv:]
    qk = jnp.einsum('nbhcd,nbhsd->nbhcs', q, k) * decay
    qg = q * jnp.exp(gc)[..., None]
    kd = k * jnp.exp(gc[..., -1:] - gc)[..., None]
    glast = jnp.exp(gc[..., -1])[..., None, None]

    def step(S, xs):
        u_n, w_n, qk_n, qg_n, kd_n, gl_n = xs
        v_new = u_n - jnp.einsum('bhcd,bhde->bhce', w_n, S)
        o_n = jnp.einsum('bhcd,bhde->bhce', qg_n, S) + jnp.einsum('bhcs,bhse->bhce', qk_n, v_new)
        S = S * gl_n + jnp.einsum('bhcd,bhce->bhde', kd_n, v_new)
        return S, o_n

    S, o = lax.scan(step, s0.astype(F32), (u, w, qk, qg, kd, glast))
    o = jnp.swapaxes(jnp.swapaxes(o, 2, 3), 0, 1).reshape(B, N * C, H, dv)[:, :L]
    return o, S


def _x_sink_attention(q, k, v, q_pos, k_pos, sinks, bias_t):
    *lead, Lq, Hq, hd = q.shape
    KV = k.shape[-2]
    G = Hq // KV
    qg = q.reshape(*lead, Lq, KV, G, hd)
    s = jnp.einsum('...qkgd,...skd->...kgqs', qg, k, preferred_element_type=F32) * hd ** -0.5
    rel = q_pos[..., :, None] - k_pos[..., None, :]
    ok = (rel >= 0) & (rel <= WINDOW) & (k_pos[..., None, :] >= 0)
    bias = jnp.moveaxis(bias_t.reshape(KV, G, T5_BUCKETS)[:, :, _x_t5_bucket(rel)], (0, 1), (-4, -3))
    s = jnp.where(ok[..., None, None, :, :], s + bias.astype(F32), -jnp.inf)
    sink = sinks.astype(F32).reshape(KV, G)[:, :, None, None]
    m = jnp.maximum(s.max(-1, keepdims=True), sink)
    p = jnp.exp(s - m)
    den = p.sum(-1, keepdims=True) + jnp.exp(sink - m)
    o = jnp.einsum('...kgqs,...skd->...qkgd', p / den, v, preferred_element_type=F32)
    return o.reshape(*lead, Lq, Hq, hd).astype(q.dtype)


def _x_swa_prompt(q, k, v, sinks, bias_t):
    B, L, Hq, hd = q.shape
    W = WINDOW
    nb = L // W

    def band(t):
        tb = t.reshape(B, nb, W, t.shape[2], hd)
        prev = jnp.concatenate([jnp.zeros_like(tb[:, :1]), tb[:, :-1]], axis=1)
        return jnp.concatenate([prev, tb], axis=2)

    base = W * jnp.arange(nb)[:, None]
    q_pos = base + jnp.arange(W)
    k_pos = base - W + jnp.arange(2 * W)
    qb = q.reshape(B, nb, W, Hq, hd)
    o = _x_sink_attention(qb, band(k), band(v), q_pos, k_pos, sinks, bias_t)
    return o.reshape(B, nb * W, Hq, hd)


def _x_lru_combine(x, y):
    a1, b1 = x
    a2, b2 = y
    return a1 * a2, a2 * b1 + b2


def _x_rg_lru(x, h0, w_a, b_a, w_x, b_x, lam):
    B, L, _ = x.shape
    xf = x.astype(F32)
    xb = xf.reshape(B, L, LRU_BLOCKS, D_RNN // LRU_BLOCKS)
    r = jax.nn.sigmoid(jnp.einsum('blni,nij->blnj', xb, w_a).reshape(B, L, D_RNN) + b_a)
    i = jax.nn.sigmoid(jnp.einsum('blni,nij->blnj', xb, w_x).reshape(B, L, D_RNN) + b_x)
    log_a = -LRU_C * r * jax.nn.softplus(-lam.astype(F32))
    a = jnp.exp(log_a)
    b = jnp.sqrt(-jnp.expm1(2.0 * log_a)) * (i * xf)
    b = b.at[:, 0].add(a[:, 0] * h0.astype(F32))
    _, h = lax.associative_scan(_x_lru_combine, (a, b), axis=1)
    return h


def _x_even_rest(p, b, l, past_k, past_v, s0, conv0, conv_w, a_log, dt_bias, norm_w, bias_t, o_a=None):
    qa, ka, va, qkv, z, beta_in, decay_in = p
    if o_a is None:
        qa4 = qa.reshape(b, l, N_ATT_HEADS, HEAD_DIM)
        ka4 = ka.reshape(b, l, KV_HEADS_A, HEAD_DIM)
        va4 = va.reshape(b, l, KV_HEADS_A, HEAD_DIM)
        past = past_k.shape[1]
        k_all = jnp.concatenate([past_k, ka4], axis=1)
        v_all = jnp.concatenate([past_v, va4], axis=1)
        o_a = _x_moba_attention(qa4, k_all, v_all, past, bias_t).reshape(b, l, -1)
    c, conv_new = _x_causal_conv(qkv, conv0, conv_w)
    c = jax.nn.silu(c)
    qb, kb, vb = jnp.split(c, [512, 1024], axis=-1)
    qb = _x_l2norm(qb.reshape(b, l, GDN_HEADS, 64))
    kb = _x_l2norm(kb.reshape(b, l, GDN_HEADS, 64))
    vb = vb.reshape(b, l, GDN_HEADS, 64)
    beta = jax.nn.sigmoid(beta_in)
    g = -jnp.exp(a_log) * jax.nn.softplus(decay_in + dt_bias)
    o_b, s_new = _x_gated_delta_rule(qb, kb, vb, g, beta, s0)
    gate = jax.nn.silu(z.reshape(b, l, GDN_HEADS, 64))
    o_b = _x_rms(o_b, norm_w) * gate
    mixed = jnp.concatenate([o_a, o_b.reshape(b, l, -1)], axis=-1)
    return mixed, s_new, conv_new


def _x_odd_rest(p, b, l, buf_k, buf_v, h0, conv0, past_len, sinks, conv_w, conv_b,
                w_a, b_a, w_x, b_x, lam, bias_t):
    qc, kc, vc, xr, gr = p
    qc = qc.reshape(b, l, N_ATT_HEADS, HEAD_DIM)
    kc = kc.reshape(b, l, KV_HEADS_C, HEAD_DIM)
    vc = vc.reshape(b, l, KV_HEADS_C, HEAD_DIM)
    if buf_k is None:
        o_c = _x_swa_prompt(qc, kc, vc, sinks, bias_t)
        k_keep, v_keep = kc[:, -WINDOW:], vc[:, -WINDOW:]
    else:
        nbuf = buf_k.shape[1]
        k_all = jnp.concatenate([buf_k, kc], axis=1)
        v_all = jnp.concatenate([buf_v, vc], axis=1)
        q_pos = past_len + jnp.arange(l)
        k_pos = past_len - nbuf + jnp.arange(nbuf + l)
        o_c = _x_sink_attention(qc, k_all, v_all, q_pos, k_pos, sinks, bias_t)
        k_keep, v_keep = k_all[:, -nbuf:], v_all[:, -nbuf:]
    xc, conv_new = _x_causal_conv(xr, conv0, conv_w, conv_b)
    h = _x_rg_lru(xc, h0, w_a, b_a, w_x, b_x, lam)
    o_d = h * jax.nn.gelu(gr)
    mixed = jnp.concatenate([o_c.reshape(b, l, -1), o_d], axis=-1)
    return mixed, k_keep, v_keep, h[:, -1], conv_new


TM = 512


def _pad_cols(w, n):
    return jnp.pad(w, ((0, 0), (0, n - w.shape[1])))


def _x_kernel_fallback(x_prompt, x_sample, cache_k_moba, cache_v_moba, state_gdn, state_gdn_conv, cache_k_swa, cache_v_swa, state_lru, state_lru_conv, page_table, w_in_even, w_out_even, gdn_conv_w, gdn_a_log, gdn_dt_bias, gdn_norm_w, w_in_odd, w_out_odd, swa_sinks, lru_conv_w, lru_conv_b, lru_w_a, lru_b_a, lru_w_x, lru_b_x, lru_lambda, rel_bias, ln_g, ln_b, router_w, router_b, w_gate, w_up, w_down):
    n_pr, seq, d = x_prompt.shape
    n_dec, dec_seq, _ = x_sample.shape
    tp = n_pr * seq
    ts = n_dec * dec_seq
    past_len = page_table.shape[1] * PAGE_SIZE
    bias_t = rel_bias.T
    h = jnp.concatenate([x_prompt.reshape(tp, d), x_sample.reshape(ts, d)], axis=0)

    rw_pad = _pad_cols(router_w, LANES)
    rb_pad = _pad_cols(router_b.reshape(1, -1), LANES)
    wg_b, wu_b, wd_b = w_gate.astype(BF16), w_up.astype(BF16), w_down.astype(BF16)

    p_even = sum(EVEN_SIZES)
    n_even = 3200
    w0 = _pad_cols(w_in_even[0], n_even).astype(BF16)
    proj = _mm(h, w0, TM, 640)
    splits = np.cumsum(EVEN_SIZES)[:-1].tolist()
    pp = [t.reshape(n_pr, seq, -1) for t in jnp.split(proj[:tp, :p_even], splits, axis=-1)]
    ps = [t.reshape(n_dec, dec_seq, -1) for t in jnp.split(proj[tp:, :p_even], splits, axis=-1)]
    moba_k_prompt = pp[1].reshape(n_pr, seq, 1, KV_HEADS_A, HEAD_DIM)
    moba_v_prompt = pp[2].reshape(n_pr, seq, 1, KV_HEADS_A, HEAD_DIM)
    moba_k_sample = ps[1].reshape(n_dec, dec_seq, 1, KV_HEADS_A, HEAD_DIM)
    moba_v_sample = ps[2].reshape(n_dec, dec_seq, 1, KV_HEADS_A, HEAD_DIM)

    o_a_p = _moba_prompt(proj, rel_bias, n_pr, seq, 0, 4, 6).reshape(n_pr, seq, -1)
    mixed_p, gdn_sp, gdn_cp = _x_even_rest(
        pp, n_pr, seq, None, None, jnp.zeros((n_pr, GDN_HEADS, 64, 64), F32),
        jnp.zeros((n_pr, CONV_WIDTH - 1, GDN_CONV_CH), F32),
        gdn_conv_w[0], gdn_a_log[0], gdn_dt_bias[0], gdn_norm_w[0], bias_t, o_a=o_a_p)
    past_k = cache_k_moba[page_table][:, :, :, 0].reshape(n_dec, past_len, KV_HEADS_A, HEAD_DIM)
    past_v = cache_v_moba[page_table][:, :, :, 0].reshape(n_dec, past_len, KV_HEADS_A, HEAD_DIM)
    mixed_s, gdn_ss, gdn_cs = _x_even_rest(
        ps, n_dec, dec_seq, past_k, past_v, state_gdn[0], state_gdn_conv[0],
        gdn_conv_w[0], gdn_a_log[0], gdn_dt_bias[0], gdn_norm_w[0], bias_t)
    mixed = jnp.concatenate([mixed_p.reshape(tp, -1), mixed_s.reshape(ts, -1)], axis=0)
    h = _mm_ln(mixed, w_out_even[0].astype(BF16), h, ln_g[0, 0], ln_b[0, 0], TM)
    h = _moe_ln(h, rw_pad, rb_pad, wg_b[0], wu_b[0], wd_b[0], ln_g[0, 1], ln_b[0, 1], TM)

    p_odd = sum(ODD_SIZES)
    proj = _mm(h, w_in_odd[0].astype(BF16), TM, 896)
    splits = np.cumsum(ODD_SIZES)[:-1].tolist()
    pp = [t.reshape(n_pr, seq, -1) for t in jnp.split(proj[:tp], splits, axis=-1)]
    ps = [t.reshape(n_dec, dec_seq, -1) for t in jnp.split(proj[tp:], splits, axis=-1)]
    mixed_p, swa_kp, swa_vp, lru_p, lru_cp = _x_odd_rest(
        pp, n_pr, seq, None, None, jnp.zeros((n_pr, D_RNN), F32),
        jnp.zeros((n_pr, CONV_WIDTH - 1, D_RNN), F32), 0,
        swa_sinks[0], lru_conv_w[0], lru_conv_b[0], lru_w_a[0], lru_b_a[0], lru_w_x[0], lru_b_x[0],
        lru_lambda[0], bias_t)
    mixed_s, swa_ks, swa_vs, lru_s, lru_cs = _x_odd_rest(
        ps, n_dec, dec_seq, cache_k_swa[0], cache_v_swa[0], state_lru[0], state_lru_conv[0], past_len,
        swa_sinks[0], lru_conv_w[0], lru_conv_b[0], lru_w_a[0], lru_b_a[0], lru_w_x[0], lru_b_x[0],
        lru_lambda[0], bias_t)
    mixed = jnp.concatenate([mixed_p.reshape(tp, -1), mixed_s.reshape(ts, -1)], axis=0)
    h = _mm_ln(mixed, w_out_odd[0].astype(BF16), h, ln_g[1, 0], ln_b[1, 0], TM)
    h = _moe_ln(h, rw_pad, rb_pad, wg_b[1], wu_b[1], wd_b[1], ln_g[1, 1], ln_b[1, 1], TM)

    y_prompt = h[:tp].reshape(n_pr, seq, d)
    y_sample = h[tp:].reshape(n_dec, dec_seq, d)
    return (y_prompt, y_sample, moba_k_prompt, moba_v_prompt, moba_k_sample, moba_v_sample,
            gdn_sp[None], gdn_ss[None], gdn_cp[None], gdn_cs[None],
            swa_kp[None], swa_vp[None], swa_ks[None], swa_vs[None],
            lru_p[None], lru_s[None], lru_cp[None], lru_cs[None])


GDN_DEC_PERIOD = 8
EVEN_COLS = 3200


def kernel(x_prompt, x_sample, cache_k_moba, cache_v_moba, state_gdn, state_gdn_conv, cache_k_swa, cache_v_swa, state_lru, state_lru_conv, page_table, w_in_even, w_out_even, gdn_conv_w, gdn_a_log, gdn_dt_bias, gdn_norm_w, w_in_odd, w_out_odd, swa_sinks, lru_conv_w, lru_conv_b, lru_w_a, lru_b_a, lru_w_x, lru_b_x, lru_lambda, rel_bias, ln_g, ln_b, router_w, router_b, w_gate, w_up, w_down):
    n_pr, seq, d = x_prompt.shape
    n_dec, dec_seq, _ = x_sample.shape
    tp = n_pr * seq
    ts = n_dec * dec_seq
    n_pool = cache_k_moba.shape[0]
    kva = KV_HEADS_A * HEAD_DIM
    kvc = KV_HEADS_C * HEAD_DIM
    h = jnp.concatenate([x_prompt.reshape(tp, d), x_sample.reshape(ts, d)], axis=0)

    rw_pad = _pad_cols(router_w, LANES)
    rb_pad = _pad_cols(router_b.reshape(1, -1), LANES)
    wg_b, wu_b, wd_b = w_gate.astype(BF16), w_up.astype(BF16), w_down.astype(BF16)

    proj = _mm(h, _pad_cols(w_in_even[0], EVEN_COLS).astype(BF16), TM, 640)
    ps = proj[tp:]
    moba_k_prompt = proj[:tp, 512:512 + kva].reshape(n_pr, seq, 1, KV_HEADS_A, HEAD_DIM)
    moba_v_prompt = proj[:tp, 768:768 + kva].reshape(n_pr, seq, 1, KV_HEADS_A, HEAD_DIM)
    k_new = ps[:, 512:512 + kva].reshape(n_dec, dec_seq, kva)
    v_new = ps[:, 768:768 + kva].reshape(n_dec, dec_seq, kva)
    moba_k_sample = k_new.reshape(n_dec, dec_seq, 1, KV_HEADS_A, HEAD_DIM)
    moba_v_sample = v_new.reshape(n_dec, dec_seq, 1, KV_HEADS_A, HEAD_DIM)

    oa_p = _moba_prompt(proj, rel_bias, n_pr, seq, 0, 4, 6)
    oa_s = _moba_sample(ps[:, :512].reshape(n_dec, dec_seq, N_ATT_HEADS, HEAD_DIM), k_new, v_new,
                        cache_k_moba[:, :, 0].reshape(n_pool, PAGE_SIZE, kva),
                        cache_v_moba[:, :, 0].reshape(n_pool, PAGE_SIZE, kva), page_table, rel_bias)

    parts = _gdn_front(proj, 8, 24, gdn_conv_w[0], gdn_a_log[0], gdn_dt_bias[0], n_pr, seq, GDN_CHUNK)
    o_p, s_fin = _gdn_scan(*parts, GDN_CHUNK, GDN_HEADS, 512)
    ob_p = _gdn_post(o_p, proj, 5, gdn_norm_w[0], n_pr, seq, 512)
    gdn_state_prompt = s_fin.reshape(1, n_pr, GDN_HEADS, HEAD_DIM, HEAD_DIM)
    gdn_conv_prompt = proj[:tp, 1024:1024 + GDN_CONV_CH].reshape(n_pr, seq, GDN_CONV_CH)[None, :, seq - 3:]

    period = GDN_DEC_PERIOD
    lead = period - dec_seq
    qkv_s = ps[:, 1024:1024 + GDN_CONV_CH].reshape(n_dec, dec_seq, GDN_CONV_CH)
    x8 = jnp.concatenate([jnp.zeros((n_dec, lead - 3, GDN_CONV_CH), F32), state_gdn_conv[0], qkv_s], axis=1)
    tail8 = jnp.pad(ps[:, 2560:EVEN_COLS].reshape(n_dec, dec_seq, EVEN_COLS - 2560), ((0, 0), (lead, 0), (0, 0)))
    xs = jnp.concatenate([x8, tail8], axis=-1).reshape(n_dec * period, EVEN_COLS - 1024)
    parts = _gdn_front(xs, 0, 16, gdn_conv_w[0], gdn_a_log[0], gdn_dt_bias[0], 1, n_dec * period, period,
                       period, lead)
    o_s, gdn_state_sample = _gdn_step(state_gdn[0], *parts, period)
    ob_s = _gdn_post(o_s, xs, 3, gdn_norm_w[0], 1, n_dec * period, 512)
    ob_s = ob_s.reshape(n_dec, period, 512)[:, lead:].reshape(ts, 512)
    gdn_conv_sample = qkv_s[None, :, dec_seq - 3:]

    oa = jnp.concatenate([oa_p, oa_s.reshape(ts, 512)], axis=0)
    ob = jnp.concatenate([ob_p, ob_s], axis=0)
    h = _mm_ln(oa, ob, w_out_even[0].astype(BF16), h, ln_g[0, 0], ln_b[0, 0], TM)
    h = _moe_ln(h, rw_pad, rb_pad, wg_b[0], wu_b[0], wd_b[0], ln_g[0, 1], ln_b[0, 1], TM)

    proj = _mm(h, w_in_odd[0].astype(BF16), TM, 896)
    ps = proj[tp:]
    wa_bd, wx_bd = _block_diag(lru_w_a[0]), _block_diag(lru_w_x[0])
    oc_p = _swa_prompt(proj, rel_bias, swa_sinks[0], n_pr, seq, 0, 4, 5)
    od_p, lru_p = _lru_prompt(proj, n_pr, seq, 6, 10, lru_conv_w[0], lru_conv_b[0], wa_bd, lru_b_a[0],
                              wx_bd, lru_b_x[0], lru_lambda[0])
    swa_k_prompt = proj[:tp, 512:512 + kvc].reshape(n_pr, seq, KV_HEADS_C, HEAD_DIM)[None, :, seq - WINDOW:]
    swa_v_prompt = proj[:tp, 640:640 + kvc].reshape(n_pr, seq, KV_HEADS_C, HEAD_DIM)[None, :, seq - WINDOW:]
    lru_conv_prompt = proj[:tp, 768:768 + D_RNN].reshape(n_pr, seq, D_RNN)[None, :, seq - 3:]

    nbuf = cache_k_swa.shape[2]
    kc_s = ps[:, 512:512 + kvc].reshape(n_dec, dec_seq, kvc)
    vc_s = ps[:, 640:640 + kvc].reshape(n_dec, dec_seq, kvc)
    buf_k = cache_k_swa[0].reshape(n_dec, nbuf, kvc)
    buf_v = cache_v_swa[0].reshape(n_dec, nbuf, kvc)
    oc_s = _swa_sample(ps[:, :512].reshape(n_dec, dec_seq, N_ATT_HEADS, HEAD_DIM), kc_s, vc_s, buf_k, buf_v,
                       rel_bias, swa_sinks[0])
    swa_k_sample = jnp.concatenate([buf_k, kc_s], axis=1)[:, dec_seq:].reshape(1, n_dec, nbuf, KV_HEADS_C, HEAD_DIM)
    swa_v_sample = jnp.concatenate([buf_v, vc_s], axis=1)[:, dec_seq:].reshape(1, n_dec, nbuf, KV_HEADS_C, HEAD_DIM)
    xr_s = ps[:, 768:768 + D_RNN].reshape(n_dec, dec_seq, D_RNN)
    gr_s = ps[:, 1280:1280 + D_RNN].reshape(n_dec, dec_seq, D_RNN)
    od_s, lru_s = _lru_sample(xr_s, gr_s, state_lru[0], state_lru_conv[0], lru_conv_w[0], lru_conv_b[0],
                              wa_bd, lru_b_a[0], wx_bd, lru_b_x[0], lru_lambda[0])
    lru_conv_sample = jnp.concatenate([state_lru_conv[0], xr_s], axis=1)[None, :, dec_seq:]

    oc = jnp.concatenate([oc_p, oc_s.reshape(ts, 512)], axis=0)
    od = jnp.concatenate([od_p, od_s.reshape(ts, 512)], axis=0)
    h = _mm_ln(oc, od, w_out_odd[0].astype(BF16), h, ln_g[1, 0], ln_b[1, 0], TM)
    h = _moe_ln(h, rw_pad, rb_pad, wg_b[1], wu_b[1], wd_b[1], ln_g[1, 1], ln_b[1, 1], TM)

    y_prompt = h[:tp].reshape(n_pr, seq, d)
    y_sample = h[tp:].reshape(n_dec, dec_seq, d)
    return (y_prompt, y_sample, moba_k_prompt, moba_v_prompt, moba_k_sample, moba_v_sample,
            gdn_state_prompt, gdn_state_sample[None], gdn_conv_prompt, gdn_conv_sample,
            swa_k_prompt, swa_v_prompt, swa_k_sample, swa_v_sample,
            lru_p[None], lru_s[None], lru_conv_prompt, lru_conv_sample)
```

```python
import functools
import math

import jax
import jax.numpy as jnp
import numpy as np
from jax import lax
from jax.experimental import pallas as pl
from jax.experimental.pallas import tpu as pltpu

F32 = jnp.float32
BF16 = jnp.bfloat16

D_MODEL = 1024
HEAD_DIM = 64
N_ATT_HEADS = 8
KV_HEADS_A = 4
KV_HEADS_C = 2
MOBA_BLOCK = 256
MOBA_TOPK = 3
PAGE_SIZE = 128
GDN_HEADS = 8
GDN_CHUNK = 64
GDN_CONV_CH = 1536
CONV_WIDTH = 4
WINDOW = 128
D_RNN = 512
LRU_BLOCKS = 8
LRU_C = 8.0
T5_BUCKETS = 32
T5_MAX_DIST = 128
N_EXPERTS = 16
N_GROUPS = 4
EXPERTS_PER_GROUP = 4
D_EXPERT = 256
DEPTH = 2
DEEPNORM_ALPHA = (2 * DEPTH) ** 0.25
LN_EPS = 1e-5
NORM_EPS = 1e-6
EVEN_SIZES = (512, 256, 256, 1536, 512, 8, 8)
ODD_SIZES = (512, 128, 128, 512, 512)

LANES = 128
VMEM_LIMIT = 56 * 1024 * 1024
NEG_INF = float("-inf")


def _cparams(sem):
    return pltpu.CompilerParams(dimension_semantics=sem, vmem_limit_bytes=VMEM_LIMIT)


def _dot(a, b):
    return jnp.dot(a.astype(BF16), b.astype(BF16), preferred_element_type=F32)


def _dot_nt(a, b):
    return lax.dot_general(a.astype(BF16), b.astype(BF16), (((1,), (1,)), ((), ())),
                           preferred_element_type=F32)


def _mm_kernel(x_ref, w_ref, o_ref):
    o_ref[...] = jnp.dot(x_ref[...].astype(BF16), w_ref[...], preferred_element_type=F32)


def _mm(x, w, tm, tn):
    m, k = x.shape
    n = w.shape[1]
    assert m % tm == 0 and n % tn == 0
    return pl.pallas_call(
        _mm_kernel,
        grid=(m // tm, n // tn),
        in_specs=[pl.BlockSpec((tm, k), lambda i, j: (i, 0)),
                  pl.BlockSpec((k, tn), lambda i, j: (0, j))],
        out_specs=pl.BlockSpec((tm, tn), lambda i, j: (i, j)),
        out_shape=jax.ShapeDtypeStruct((m, n), F32),
        compiler_params=_cparams(("parallel", "arbitrary")),
        name="in_proj",
    )(x, w)


def _layer_norm_rows(z, g, b):
    mu = jnp.mean(z, axis=-1, keepdims=True)
    zc = z - mu
    var = jnp.mean(zc * zc, axis=-1, keepdims=True)
    return zc * lax.rsqrt(var + LN_EPS) * g + b


def _mm_ln_kernel(xa_ref, xb_ref, wa_ref, wb_ref, h_ref, g_ref, b_ref, o_ref):
    f = (jnp.dot(xa_ref[...].astype(BF16), wa_ref[...], preferred_element_type=F32)
         + jnp.dot(xb_ref[...].astype(BF16), wb_ref[...], preferred_element_type=F32))
    o_ref[...] = _layer_norm_rows(DEEPNORM_ALPHA * h_ref[...] + f, g_ref[...], b_ref[...])


def _mm_ln(xa, xb, w, h, g, b, tm):
    m, ka = xa.shape
    kb = xb.shape[1]
    n = w.shape[1]
    assert m % tm == 0 and ka == kb
    return pl.pallas_call(
        _mm_ln_kernel,
        grid=(m // tm,),
        in_specs=[pl.BlockSpec((tm, ka), lambda i: (i, 0)),
                  pl.BlockSpec((tm, kb), lambda i: (i, 0)),
                  pl.BlockSpec((ka, n), lambda i: (0, 0)),
                  pl.BlockSpec((kb, n), lambda i: (1, 0)),
                  pl.BlockSpec((tm, n), lambda i: (i, 0)),
                  pl.BlockSpec((1, n), lambda i: (0, 0)),
                  pl.BlockSpec((1, n), lambda i: (0, 0))],
        out_specs=pl.BlockSpec((tm, n), lambda i: (i, 0)),
        out_shape=jax.ShapeDtypeStruct((m, n), F32),
        compiler_params=_cparams(("parallel",)),
        name="out_proj_ln",
    )(xa, xb, w, w, h, g.reshape(1, n), b.reshape(1, n))


def _row_max_first(vals, lane):
    mx = jnp.max(vals, axis=1, keepdims=True)
    idx = jnp.min(jnp.where(vals == mx, lane, LANES), axis=1, keepdims=True)
    return mx, idx


def _moe_ln_kernel(t_ref, rw_ref, rb_ref, wg_ref, wu_ref, wd_ref, g_ref, b_ref, o_ref,
                   gates_ref, tb_ref, acc_ref):
    e = pl.program_id(1)

    @pl.when(e == 0)
    def _route():
        t = t_ref[...]
        logits = jnp.dot(t, rw_ref[...], preferred_element_type=F32,
                         precision=lax.Precision.HIGHEST)
        s = jax.nn.sigmoid(logits)
        sb = s + rb_ref[...]
        lane = lax.broadcasted_iota(jnp.int32, sb.shape, 1)
        grp = lane // EXPERTS_PER_GROUP
        best = None
        gsel = None
        for gi in range(N_GROUPS):
            vals = jnp.where(grp == gi, sb, NEG_INF)
            m1, i1 = _row_max_first(vals, lane)
            m2 = jnp.max(jnp.where(lane == i1, NEG_INF, vals), axis=1, keepdims=True)
            score = m1 + m2
            if gi == 0:
                best, gsel = score, jnp.zeros_like(i1)
            else:
                upd = score > best
                best = jnp.where(upd, score, best)
                gsel = jnp.where(upd, gi, gsel)
        vals = jnp.where(grp == gsel, sb, NEG_INF)
        _, i1 = _row_max_first(vals, lane)
        _, i2 = _row_max_first(jnp.where(lane == i1, NEG_INF, vals), lane)
        s1 = jnp.sum(jnp.where(lane == i1, s, 0.0), axis=1, keepdims=True)
        s2 = jnp.sum(jnp.where(lane == i2, s, 0.0), axis=1, keepdims=True)
        den = s1 + s2
        gates_ref[...] = jnp.where(lane == i1, s1 / den, 0.0) + jnp.where(lane == i2, s2 / den, 0.0)
        tb_ref[...] = t.astype(BF16)
        acc_ref[...] = jnp.zeros_like(acc_ref)

    tb = tb_ref[...]
    hg = jnp.dot(tb, wg_ref[...], preferred_element_type=F32)
    hu = jnp.dot(tb, wu_ref[...], preferred_element_type=F32)
    gates = gates_ref[...]
    lane = lax.broadcasted_iota(jnp.int32, gates.shape, 1)
    ge = jnp.sum(jnp.where(lane == e, gates, 0.0), axis=1, keepdims=True)
    h = (hg * jax.nn.sigmoid(hg)) * hu * ge
    acc_ref[...] += jnp.dot(h.astype(BF16), wd_ref[...], preferred_element_type=F32)

    @pl.when(e == pl.num_programs(1) - 1)
    def _finish():
        o_ref[...] = _layer_norm_rows(DEEPNORM_ALPHA * t_ref[...] + acc_ref[...], g_ref[...], b_ref[...])


def _moe_ln(t, router_w_pad, router_b_pad, wg, wu, wd, g, b, tm):
    m, d = t.shape
    ne, _, f = wg.shape
    assert m % tm == 0
    return pl.pallas_call(
        _moe_ln_kernel,
        grid=(m // tm, ne),
        in_specs=[pl.BlockSpec((tm, d), lambda i, e: (i, 0)),
                  pl.BlockSpec((d, LANES), lambda i, e: (0, 0)),
                  pl.BlockSpec((1, LANES), lambda i, e: (0, 0)),
                  pl.BlockSpec((None, d, f), lambda i, e: (e, 0, 0)),
                  pl.BlockSpec((None, d, f), lambda i, e: (e, 0, 0)),
                  pl.BlockSpec((None, f, d), lambda i, e: (e, 0, 0)),
                  pl.BlockSpec((1, d), lambda i, e: (0, 0)),
                  pl.BlockSpec((1, d), lambda i, e: (0, 0))],
        out_specs=pl.BlockSpec((tm, d), lambda i, e: (i, 0)),
        out_shape=jax.ShapeDtypeStruct((m, d), F32),
        scratch_shapes=[pltpu.VMEM((tm, LANES), F32),
                        pltpu.VMEM((tm, d), BF16),
                        pltpu.VMEM((tm, d), F32)],
        compiler_params=_cparams(("parallel", "arbitrary")),
        name="moe_ln",
    )(t, router_w_pad, router_b_pad, wg, wu, wd, g.reshape(1, d), b.reshape(1, d))


def _t5_bucket_np(rel):
    n = np.maximum(rel, 0)
    max_exact = T5_BUCKETS // 2
    nf = np.maximum(n, 1).astype(np.float32)
    large = max_exact + (np.log(nf / np.float32(max_exact)) / np.float32(math.log(T5_MAX_DIST / max_exact))
                         * np.float32(T5_BUCKETS - max_exact)).astype(np.int32)
    large = np.minimum(large, T5_BUCKETS - 1)
    return np.where(n < max_exact, n, large).astype(np.int32)


def _old_moba_prompt_kernel(q_ref, k_ref, v_ref, bd_ref, bp_ref, bf_ref, o_ref,
                        kmean_ref, sel_ref, m_ref, l_ref, acc_ref):
    i = pl.program_id(2)
    blk = MOBA_BLOCK
    nblk = k_ref.shape[0] // blk
    hd = HEAD_DIM

    @pl.when(i == 0)
    def _means():
        for n in range(nblk):
            kmean_ref[n:n + 1, :] = jnp.mean(k_ref[n * blk:(n + 1) * blk, :], axis=0, keepdims=True)

    q4 = q_ref[...] * (hd ** -0.5)
    for kvl in range(2):
        lo = kvl * 2 * hd
        q2 = jnp.concatenate([q4[:, lo:lo + hd], q4[:, lo + hd:lo + 2 * hd]], axis=0)
        q2b = q2.astype(BF16)
        kcol = slice(kvl * hd, (kvl + 1) * hd)

        kmean = kmean_ref[:, kcol]
        gate = lax.dot_general(q2, kmean, (((1,), (1,)), ((), ())), preferred_element_type=F32,
                               precision=lax.Precision.HIGHEST)
        col = lax.broadcasted_iota(jnp.int32, gate.shape, 1)
        gate = jnp.where(col < i, gate, NEG_INF)
        sel = jnp.zeros(gate.shape, F32)
        for _ in range(MOBA_TOPK):
            mx = jnp.max(gate, axis=1, keepdims=True)
            idx = jnp.min(jnp.where(gate == mx, col, nblk), axis=1, keepdims=True)
            hit = (col == idx) & (mx > NEG_INF)
            sel = jnp.where(hit, 1.0, sel)
            gate = jnp.where(col == idx, NEG_INF, gate)
        sel_ref[...] = sel

        m_ref[...] = jnp.full(m_ref.shape, NEG_INF, F32)
        l_ref[...] = jnp.zeros(l_ref.shape, F32)
        acc_ref[...] = jnp.zeros(acc_ref.shape, F32)

        def update(s, vb):
            m_old = m_ref[...]
            m_new = jnp.maximum(m_old, jnp.max(s, axis=1, keepdims=True))
            alpha = jnp.exp(m_old - m_new)
            p = jnp.exp(s - m_new)
            l_ref[...] = alpha * l_ref[...] + jnp.sum(p, axis=1, keepdims=True)
            acc_ref[...] = alpha * acc_ref[...] + jnp.dot(p.astype(BF16), vb, preferred_element_type=F32)
            m_ref[...] = m_new

        def tile(j):
            r0 = pl.multiple_of(j * blk, blk)
            kb = k_ref[pl.ds(r0, blk), :][:, kcol].astype(BF16)
            vb = v_ref[pl.ds(r0, blk), :][:, kcol].astype(BF16)
            return _dot_nt(q2b, kb), vb

        s, vb = tile(i)
        update(s + bd_ref[kvl], vb)

        @pl.when(i >= 1)
        def _prev():
            s, vb = tile(i - 1)
            live = jnp.sum(jnp.where(col == i - 1, sel_ref[...], 0.0), axis=1, keepdims=True) > 0.0
            update(jnp.where(live, s + bp_ref[kvl], NEG_INF), vb)

        def far(j, carry):
            s, vb = tile(j)
            live = jnp.sum(jnp.where(col == j, sel_ref[...], 0.0), axis=1, keepdims=True) > 0.0
            rb = jnp.where(live, bf_ref[kvl], NEG_INF)
            update(s + rb, vb)
            return carry

        lax.fori_loop(0, jnp.maximum(i - 1, 0), far, 0)

        o = acc_ref[...] / l_ref[...]
        o_ref[:, lo:lo + hd] = o[:blk]
        o_ref[:, lo + hd:lo + 2 * hd] = o[blk:]


def _old_moba_prompt(proj, bias_rel, n_batch, seq, q_col, k_col, v_col):
    blk = MOBA_BLOCK
    nblk = seq // blk
    iq = np.arange(blk)[:, None]
    ik = np.arange(blk)[None, :]
    idx_diag = _t5_bucket_np(iq - ik)
    idx_prev = _t5_bucket_np(blk + iq - ik)
    bt = bias_rel.T
    diag = jnp.where(jnp.asarray(iq >= ik)[None], bt[:, idx_diag], NEG_INF)
    prev = bt[:, idx_prev]
    far = jnp.broadcast_to(bt[:, T5_BUCKETS - 1][:, None, None], (N_ATT_HEADS, blk, 1))
    diag = diag.reshape(KV_HEADS_A, 2 * blk, blk)
    prev = prev.reshape(KV_HEADS_A, 2 * blk, blk)
    far = far.reshape(KV_HEADS_A, 2 * blk, 1)
    rows_per_b = seq // blk
    return pl.pallas_call(
        _moba_prompt_kernel,
        grid=(n_batch, 2, nblk),
        in_specs=[pl.BlockSpec((blk, 2 * LANES), lambda b, p, i: (b * rows_per_b + i, q_col // 2 + p)),
                  pl.BlockSpec((seq, LANES), lambda b, p, i: (b, k_col + p)),
                  pl.BlockSpec((seq, LANES), lambda b, p, i: (b, v_col + p)),
                  pl.BlockSpec((2, 2 * blk, blk), lambda b, p, i: (p, 0, 0)),
                  pl.BlockSpec((2, 2 * blk, blk), lambda b, p, i: (p, 0, 0)),
                  pl.BlockSpec((2, 2 * blk, 1), lambda b, p, i: (p, 0, 0))],
        out_specs=pl.BlockSpec((blk, 2 * LANES), lambda b, p, i: (b * rows_per_b + i, p)),
        out_shape=jax.ShapeDtypeStruct((n_batch * seq, 4 * LANES), F32),
        scratch_shapes=[pltpu.VMEM((nblk, LANES), F32),
                        pltpu.VMEM((2 * blk, nblk), F32),
                        pltpu.VMEM((2 * blk, 1), F32),
                        pltpu.VMEM((2 * blk, 1), F32),
                        pltpu.VMEM((2 * blk, HEAD_DIM), F32)],
        compiler_params=_cparams(("parallel", "parallel", "arbitrary")),
        name="moba_prompt",
    )(proj, proj, proj, diag, prev, far)


def _t5_thresholds():
    b = _t5_bucket_np(np.arange(4 * T5_MAX_DIST))
    return [int(np.argmax(b >= k)) for k in range(1, T5_BUCKETS)]


def _bias_tile_kernel(rb_ref, o_ref, *, off, sign, lo, hi):
    h = pl.program_id(0)
    row = lax.broadcasted_iota(jnp.int32, o_ref.shape, 0)
    col = lax.broadcasted_iota(jnp.int32, o_ref.shape, 1)
    rel = off + sign * (row - col)
    t = jnp.full(o_ref.shape, rb_ref[0, h], F32)
    for k, thr in enumerate(_t5_thresholds(), start=1):
        t = jnp.where(rel >= thr, rb_ref[k, h], t)
    if lo is not None:
        t = jnp.where((rel >= lo) & (rel <= hi), t, NEG_INF)
    o_ref[...] = t


def _bias_tiles(rel_bias, rows, cols, off, sign=1, lo=None, hi=None):
    return pl.pallas_call(
        functools.partial(_bias_tile_kernel, off=off, sign=sign, lo=lo, hi=hi),
        grid=(N_ATT_HEADS,),
        in_specs=[pl.BlockSpec(memory_space=pltpu.SMEM)],
        out_specs=pl.BlockSpec((None, rows, cols), lambda h: (h, 0, 0)),
        out_shape=jax.ShapeDtypeStruct((N_ATT_HEADS, rows, cols), F32),
        compiler_params=_cparams(("parallel",)),
        name="bias_tiles",
    )(rel_bias)


def _moba_prompt_kernel(q_ref, k_ref, v_ref, bd_ref, bp_ref, bf_ref, o_ref, kmean_ref, vt_ref, sel_ref):
    i = pl.program_id(2)
    blk = MOBA_BLOCK
    nblk = k_ref.shape[0] // blk
    hd = HEAD_DIM

    @pl.when(i == 0)
    def _per_sequence():
        for n in range(nblk):
            kmean_ref[n:n + 1, :] = jnp.mean(k_ref[n * blk:(n + 1) * blk, :], axis=0, keepdims=True)
            vt_ref[n] = v_ref[n * blk:(n + 1) * blk, :].T.astype(BF16)

    q4 = q_ref[...] * (hd ** -0.5)
    q2b = []
    for kvl in range(2):
        lo = kvl * 2 * hd
        q2 = jnp.concatenate([q4[:, lo:lo + hd], q4[:, lo + hd:lo + 2 * hd]], axis=0)
        q2b.append(q2.astype(BF16))
        gate = lax.dot_general(kmean_ref[:, kvl * hd:(kvl + 1) * hd], q2, (((1,), (1,)), ((), ())),
                               preferred_element_type=F32, precision=lax.Precision.HIGHEST)
        row = lax.broadcasted_iota(jnp.int32, gate.shape, 0)
        gate = jnp.where(row < i, gate, NEG_INF)
        sel = jnp.zeros(gate.shape, F32)
        for _ in range(MOBA_TOPK):
            mx = jnp.max(gate, axis=0, keepdims=True)
            idx = jnp.min(jnp.where(gate == mx, row, nblk), axis=0, keepdims=True)
            hit = (row == idx) & (mx > NEG_INF)
            sel = jnp.where(hit, 1.0, sel)
            gate = jnp.where(row == idx, NEG_INF, gate)
        sel_ref[kvl] = sel

    def scores(j, kvl):
        r0 = pl.multiple_of(j * blk, blk)
        kb = k_ref[pl.ds(r0, blk), :][:, kvl * hd:(kvl + 1) * hd].astype(BF16)
        return _dot_nt(kb, q2b[kvl])

    def update(state, s, j, kvl):
        m_old, l_old, acc = state
        m_new = jnp.maximum(m_old, jnp.max(s, axis=0, keepdims=True))
        alpha = jnp.exp(m_old - m_new)
        p = jnp.exp(s - m_new)
        l_new = alpha * l_old + jnp.sum(p, axis=0, keepdims=True)
        vt = vt_ref[j][kvl * hd:(kvl + 1) * hd, :]
        acc = alpha * acc + jnp.dot(vt, p.astype(BF16), preferred_element_type=F32)
        return m_new, l_new, acc

    def live_row(j, kvl):
        return sel_ref[kvl, pl.ds(j, 1), :] > 0.0

    init = (jnp.full((1, 2 * blk), NEG_INF, F32), jnp.zeros((1, 2 * blk), F32), jnp.zeros((hd, 2 * blk), F32))
    jp = jnp.maximum(i - 1, 0)
    states = []
    for kvl in range(2):
        st = update(init, scores(i, kvl) + bd_ref[kvl], i, kvl)
        s = jnp.where(live_row(jp, kvl), scores(jp, kvl) + bp_ref[kvl], NEG_INF)
        states.append(update(st, s, jp, kvl))

    def far(j, carry):
        out = []
        for kvl in range(2):
            s = scores(j, kvl) + jnp.where(live_row(j, kvl), bf_ref[kvl], NEG_INF)
            out.append(update(carry[kvl], s, j, kvl))
        return tuple(out)

    states = lax.fori_loop(0, jp, far, tuple(states))
    ot = jnp.concatenate([acc / l for (_, l, acc) in states], axis=0)
    o = ot.T
    for kvl in range(2):
        for hh in range(2):
            o_ref[:, (2 * kvl + hh) * hd:(2 * kvl + hh + 1) * hd] = o[hh * blk:(hh + 1) * blk, kvl * hd:(kvl + 1) * hd]


def _moba_prompt(proj, bias_rel, n_batch, seq, q_col, k_col, v_col):
    blk = MOBA_BLOCK
    nblk = seq // blk

    def pair_up(t):
        return jnp.transpose(t.reshape(KV_HEADS_A, 2, blk, blk), (0, 2, 1, 3)).reshape(KV_HEADS_A, blk, 2 * blk)

    diag = pair_up(_bias_tiles(bias_rel, blk, blk, 0, sign=-1, lo=0, hi=2 * blk))
    prev = pair_up(_bias_tiles(bias_rel, blk, blk, blk, sign=-1))
    far = jnp.broadcast_to(bias_rel[T5_BUCKETS - 1].reshape(KV_HEADS_A, 1, 2, 1), (KV_HEADS_A, 1, 2, blk))
    far = far.reshape(KV_HEADS_A, 1, 2 * blk)
    rows_per_b = seq // blk
    return pl.pallas_call(
        _moba_prompt_kernel,
        grid=(n_batch, 2, nblk),
        in_specs=[pl.BlockSpec((blk, 2 * LANES), lambda b, p, i: (b * rows_per_b + i, q_col // 2 + p)),
                  pl.BlockSpec((seq, LANES), lambda b, p, i: (b, k_col + p)),
                  pl.BlockSpec((seq, LANES), lambda b, p, i: (b, v_col + p)),
                  pl.BlockSpec((2, blk, 2 * blk), lambda b, p, i: (p, 0, 0)),
                  pl.BlockSpec((2, blk, 2 * blk), lambda b, p, i: (p, 0, 0)),
                  pl.BlockSpec((2, 1, 2 * blk), lambda b, p, i: (p, 0, 0))],
        out_specs=pl.BlockSpec((blk, 2 * LANES), lambda b, p, i: (b * rows_per_b + i, p)),
        out_shape=jax.ShapeDtypeStruct((n_batch * seq, 4 * LANES), F32),
        scratch_shapes=[pltpu.VMEM((nblk, LANES), F32),
                        pltpu.VMEM((nblk, LANES, blk), BF16),
                        pltpu.VMEM((2, nblk, 2 * blk), F32)],
        compiler_params=_cparams(("parallel", "parallel", "arbitrary")),
        name="moba_prompt",
    )(proj, proj, proj, diag, prev, far)


MOBA_DEC_BLOCKS_PER_STEP = 4
PAGES_PER_BLOCK = MOBA_BLOCK // PAGE_SIZE


def _moba_sample_kernel(pt_ref, q_ref, kn_ref, vn_ref, b31_ref, bfar_ref, bown_ref, *rest,
                        n_blocks):
    nb = MOBA_DEC_BLOCKS_PER_STEP
    npg = nb * PAGES_PER_BLOCK
    k_pages = rest[:npg]
    v_pages = rest[npg:2 * npg]
    o_ref = rest[2 * npg]
    g_ref, m_ref, l_ref, part_ref = rest[2 * npg + 1:]
    s = pl.program_id(1)
    q = q_ref[...]
    qb = q.astype(BF16)
    lane = lax.broadcasted_iota(jnp.int32, g_ref.shape, 1)

    @pl.when(s == 0)
    def _init():
        g_ref[...] = jnp.zeros_like(g_ref)
        m_ref[...] = jnp.zeros_like(m_ref)
        l_ref[...] = jnp.zeros_like(l_ref)

    for j in range(nb):
        n = s * nb + j
        kblk = jnp.concatenate([k_pages[j * PAGES_PER_BLOCK + t][...] for t in range(PAGES_PER_BLOCK)], axis=0)
        vblk = jnp.concatenate([v_pages[j * PAGES_PER_BLOCK + t][...] for t in range(PAGES_PER_BLOCK)], axis=0)
        kmean = jnp.mean(kblk, axis=0, keepdims=True)
        gate = jnp.sum(q * kmean, axis=1, keepdims=True)
        sc = _dot_nt(qb, kblk)
        sc = sc + jnp.where(n == n_blocks - 1, b31_ref[...], bfar_ref[...])
        mx = jnp.max(sc, axis=1, keepdims=True)
        p = jnp.exp(sc - mx)
        hit = lane == n
        g_ref[...] = jnp.where(hit, gate, g_ref[...])
        m_ref[...] = jnp.where(hit, mx, m_ref[...])
        l_ref[...] = jnp.where(hit, jnp.sum(p, axis=1, keepdims=True), l_ref[...])
        part_ref[n] = _dot(p, vblk)

    @pl.when(s == pl.num_programs(1) - 1)
    def _merge():
        gate = jnp.where(lane < n_blocks, g_ref[...], NEG_INF)
        sel = jnp.zeros(gate.shape, jnp.bool_)
        for _ in range(MOBA_TOPK):
            mxg = jnp.max(gate, axis=1, keepdims=True)
            idx = jnp.min(jnp.where(gate == mxg, lane, LANES), axis=1, keepdims=True)
            hit = (lane == idx) & (mxg > NEG_INF)
            sel = sel | hit
            gate = jnp.where(lane == idx, NEG_INF, gate)
        s_own = _dot_nt(qb, kn_ref[...]) + bown_ref[...]
        m_sel = jnp.where(sel, m_ref[...], NEG_INF)
        m_all = jnp.maximum(jnp.max(m_sel, axis=1, keepdims=True), jnp.max(s_own, axis=1, keepdims=True))
        w = jnp.where(sel, jnp.exp(m_sel - m_all), 0.0)
        p_own = jnp.exp(s_own - m_all)
        den = jnp.sum(w * l_ref[...], axis=1, keepdims=True) + jnp.sum(p_own, axis=1, keepdims=True)
        acc = _dot(p_own, vn_ref[...])
        for n in range(n_blocks):
            wn = jnp.sum(jnp.where(lane == n, w, 0.0), axis=1, keepdims=True)
            acc = acc + wn * part_ref[n]
        o_ref[...] = acc / den


def _moba_sample(q_s, k_new, v_new, cache_k, cache_v, page_table, bias_rel, layer=0):
    n_dec, t_dec = q_s.shape[:2]
    n_pages = page_table.shape[1]
    n_blocks = n_pages // PAGES_PER_BLOCK
    nb = MOBA_DEC_BLOCKS_PER_STEP
    npg = nb * PAGES_PER_BLOCK
    assert n_blocks % nb == 0 and n_blocks <= LANES and t_dec <= 8
    rows = N_ATT_HEADS * t_dec
    kv_w = KV_HEADS_A * HEAD_DIM
    head_kv = np.arange(N_ATT_HEADS) // (N_ATT_HEADS // KV_HEADS_A)
    qh = jnp.transpose(q_s, (0, 2, 1, 3)) * (HEAD_DIM ** -0.5)
    onehot = jnp.asarray(np.eye(KV_HEADS_A, dtype=np.float32)[head_kv])
    qbd = (qh[:, :, :, None, :] * onehot[None, :, None, :, None]).reshape(n_dec, rows, kv_w)
    pad = 8 - t_dec
    kn = jnp.pad(k_new, ((0, 0), (0, pad), (0, 0)))
    vn = jnp.pad(v_new, ((0, 0), (0, pad), (0, 0)))
    bt = bias_rel.T
    past = n_pages * PAGE_SIZE
    tq = np.arange(t_dec)[None, :, None]
    ik = np.arange(MOBA_BLOCK)[None, None, :]
    idx31 = _t5_bucket_np(past + tq - ((n_blocks - 1) * MOBA_BLOCK + ik))[0]
    b31 = bt[:, idx31].reshape(rows, MOBA_BLOCK)
    bfar = jnp.broadcast_to(bt[:, T5_BUCKETS - 1][:, None, None], (N_ATT_HEADS, t_dec, 1)).reshape(rows, 1)
    tk = np.arange(8)[None, :]
    rel_own = np.arange(t_dec)[:, None] - tk
    ok_own = (rel_own >= 0) & (tk < t_dec)
    bown = jnp.where(jnp.asarray(ok_own)[None], bt[:, _t5_bucket_np(rel_own)], NEG_INF).reshape(rows, 8)

    def page_spec(k):
        return pl.BlockSpec((None, PAGE_SIZE, kv_w), lambda b, s, pt, k=k: (pt[b, s * npg + k], 0, layer))

    const2 = lambda b, s, pt: (0, 0)
    grid_spec = pltpu.PrefetchScalarGridSpec(
        num_scalar_prefetch=1,
        grid=(n_dec, n_blocks // nb),
        in_specs=[pl.BlockSpec((None, rows, kv_w), lambda b, s, pt: (b, 0, 0)),
                  pl.BlockSpec((None, 8, kv_w), lambda b, s, pt: (b, 0, 0)),
                  pl.BlockSpec((None, 8, kv_w), lambda b, s, pt: (b, 0, 0)),
                  pl.BlockSpec((rows, MOBA_BLOCK), const2),
                  pl.BlockSpec((rows, 1), const2),
                  pl.BlockSpec((rows, 8), const2)]
                 + [page_spec(k) for k in range(npg)] + [page_spec(k) for k in range(npg)],
        out_specs=pl.BlockSpec((None, rows, kv_w), lambda b, s, pt: (b, 0, 0)),
        scratch_shapes=[pltpu.VMEM((rows, LANES), F32),
                        pltpu.VMEM((rows, LANES), F32),
                        pltpu.VMEM((rows, LANES), F32),
                        pltpu.VMEM((n_blocks, rows, kv_w), F32)],
    )
    out = pl.pallas_call(
        functools.partial(_moba_sample_kernel, n_blocks=n_blocks),
        grid_spec=grid_spec,
        out_shape=jax.ShapeDtypeStruct((n_dec, rows, kv_w), F32),
        compiler_params=_cparams(("parallel", "arbitrary")),
        name="moba_sample",
    )(page_table, qbd, kn, vn, b31, bfar, bown, *([cache_k] * npg), *([cache_v] * npg))
    out = out.reshape(n_dec, N_ATT_HEADS, t_dec, KV_HEADS_A, HEAD_DIM)
    out = out[:, np.arange(N_ATT_HEADS), :, head_kv, :]
    return jnp.transpose(out, (1, 2, 0, 3)).reshape(n_dec, t_dec, N_ATT_HEADS * HEAD_DIM)


SWA_GROUP = N_ATT_HEADS // KV_HEADS_C


def _sink_softmax_pv(parts, sink):
    m = sink
    for s, _ in parts:
        m = jnp.maximum(m, jnp.max(s, axis=1, keepdims=True))
    den = jnp.exp(sink - m)
    ps = []
    for s, _ in parts:
        p = jnp.exp(s - m)
        den = den + jnp.sum(p, axis=1, keepdims=True)
        ps.append(p)
    o = None
    for p, (_, v) in zip(ps, parts):
        t = _dot(p / den, v)
        o = t if o is None else o + t
    return o


def _swa_prompt_kernel(q_ref, kp_ref, ko_ref, vp_ref, vo_ref, bias_ref, sink_ref, o_ref):
    i = pl.program_id(1)
    w, hd, g = WINDOW, HEAD_DIM, SWA_GROUP
    q = q_ref[...] * (hd ** -0.5)
    kp, ko, vp, vo = kp_ref[...], ko_ref[...], vp_ref[...], vo_ref[...]
    for c in range(KV_HEADS_C):
        qs = jnp.concatenate([q[:, (c * g + j) * hd:(c * g + j + 1) * hd] for j in range(g)], axis=0)
        kcol = slice(c * hd, (c + 1) * hd)
        kcat = jnp.concatenate([kp[:, kcol], ko[:, kcol]], axis=0)
        vcat = jnp.concatenate([vp[:, kcol], vo[:, kcol]], axis=0)
        s = _dot_nt(qs, kcat) + bias_ref[c]
        col = lax.broadcasted_iota(jnp.int32, s.shape, 1)
        s = jnp.where((col >= w) | (i > 0), s, NEG_INF)
        o = _sink_softmax_pv([(s, vcat)], sink_ref[c])
        for j in range(g):
            o_ref[:, (c * g + j) * hd:(c * g + j + 1) * hd] = o[j * w:(j + 1) * w]


def _swa_prompt(proj, bias_rel, sinks, n_batch, seq, q_col, k_col, v_col):
    w, g = WINDOW, SWA_GROUP
    nb = seq // w
    band = _bias_tiles(bias_rel, w, 2 * w, w, sign=1, lo=0, hi=w).reshape(KV_HEADS_C, g * w, 2 * w)
    sink = jnp.broadcast_to(sinks[:, None, None], (N_ATT_HEADS, w, 1)).reshape(KV_HEADS_C, g * w, 1)
    own = lambda c: (lambda b, i: (b * nb + i, c))
    prev = lambda c: (lambda b, i: (b * nb + jnp.maximum(i - 1, 0), c))
    return pl.pallas_call(
        _swa_prompt_kernel,
        grid=(n_batch, nb),
        in_specs=[pl.BlockSpec((w, 4 * LANES), lambda b, i: (b * nb + i, q_col // 4)),
                  pl.BlockSpec((w, LANES), prev(k_col)),
                  pl.BlockSpec((w, LANES), own(k_col)),
                  pl.BlockSpec((w, LANES), prev(v_col)),
                  pl.BlockSpec((w, LANES), own(v_col)),
                  pl.BlockSpec((KV_HEADS_C, g * w, 2 * w), lambda b, i: (0, 0, 0)),
                  pl.BlockSpec((KV_HEADS_C, g * w, 1), lambda b, i: (0, 0, 0))],
        out_specs=pl.BlockSpec((w, 4 * LANES), lambda b, i: (b * nb + i, 0)),
        out_shape=jax.ShapeDtypeStruct((n_batch * seq, 4 * LANES), F32),
        compiler_params=_cparams(("parallel", "parallel")),
        name="swa_prompt",
    )(proj, proj, proj, proj, proj, band, sink)


SWA_DEC_SEQS_PER_STEP = 8


def _swa_sample_kernel(q_ref, kb_ref, vb_ref, kn_ref, vn_ref, bbuf_ref, bnew_ref, sink_ref, o_ref):
    for i in range(SWA_DEC_SEQS_PER_STEP):
        q = q_ref[i]
        s_buf = _dot_nt(q, kb_ref[i]) + bbuf_ref[...]
        s_new = _dot_nt(q, kn_ref[i]) + bnew_ref[...]
        o_ref[i] = _sink_softmax_pv([(s_buf, vb_ref[i]), (s_new, vn_ref[i])], sink_ref[...])


def _swa_sample(q_s, k_new, v_new, buf_k, buf_v, bias_rel, sinks):
    n_dec, t_dec = q_s.shape[:2]
    nbuf = buf_k.shape[1]
    rows = N_ATT_HEADS * t_dec
    kv_w = KV_HEADS_C * HEAD_DIM
    sb = SWA_DEC_SEQS_PER_STEP
    assert n_dec % sb == 0 and t_dec <= 8
    head_kv = np.arange(N_ATT_HEADS) // SWA_GROUP
    qh = jnp.transpose(q_s, (0, 2, 1, 3)) * (HEAD_DIM ** -0.5)
    onehot = jnp.asarray(np.eye(KV_HEADS_C, dtype=np.float32)[head_kv])
    qbd = (qh[:, :, :, None, :] * onehot[None, :, None, :, None]).reshape(n_dec, rows, kv_w)
    pad = 8 - t_dec
    kn = jnp.pad(k_new, ((0, 0), (0, pad), (0, 0)))
    vn = jnp.pad(v_new, ((0, 0), (0, pad), (0, 0)))
    bt = bias_rel.T
    t = np.arange(t_dec)[:, None]
    rel_buf = nbuf + t - np.arange(nbuf)[None, :]
    ok_buf = (rel_buf >= 0) & (rel_buf <= WINDOW)
    bbuf = jnp.where(jnp.asarray(ok_buf)[None], bt[:, _t5_bucket_np(rel_buf)], NEG_INF).reshape(rows, nbuf)
    tk = np.arange(8)[None, :]
    rel_new = t - tk
    ok_new = (rel_new >= 0) & (tk < t_dec)
    bnew = jnp.where(jnp.asarray(ok_new)[None], bt[:, _t5_bucket_np(rel_new)], NEG_INF).reshape(rows, 8)
    sink = jnp.broadcast_to(sinks[:, None, None], (N_ATT_HEADS, t_dec, 1)).reshape(rows, 1)
    seq3 = lambda r, c: pl.BlockSpec((sb, r, c), lambda i: (i, 0, 0))
    const2 = lambda r, c: pl.BlockSpec((r, c), lambda i: (0, 0))
    out = pl.pallas_call(
        _swa_sample_kernel,
        grid=(n_dec // sb,),
        in_specs=[seq3(rows, kv_w), seq3(nbuf, kv_w), seq3(nbuf, kv_w), seq3(8, kv_w), seq3(8, kv_w),
                  const2(rows, nbuf), const2(rows, 8), const2(rows, 1)],
        out_specs=seq3(rows, kv_w),
        out_shape=jax.ShapeDtypeStruct((n_dec, rows, kv_w), F32),
        compiler_params=_cparams(("parallel",)),
        name="swa_sample",
    )(qbd, buf_k, buf_v, kn, vn, bbuf, bnew, sink)
    out = out.reshape(n_dec, N_ATT_HEADS, t_dec, KV_HEADS_C, HEAD_DIM)
    out = out[:, np.arange(N_ATT_HEADS), :, head_kv, :]
    return jnp.transpose(out, (1, 2, 0, 3)).reshape(n_dec, t_dec, N_ATT_HEADS * HEAD_DIM)


def _softplus(x):
    return jnp.maximum(x, 0.0) + jnp.log1p(jnp.exp(-jnp.abs(x)))


def _neg_expm1(y):
    return -jnp.tanh(0.5 * y) * (jnp.exp(y) + 1.0)


def _shifted_rows(prev8, cur, k):
    if k == 0:
        return cur
    ext = jnp.concatenate([prev8, cur], axis=0)
    return pltpu.roll(ext, k, axis=0)[8:]


def _causal_conv_rows(prev8, cur, w):
    y = cur * w[CONV_WIDTH - 1:CONV_WIDTH, :]
    for k in range(1, CONV_WIDTH):
        y = y + _shifted_rows(prev8, cur, k) * w[CONV_WIDTH - 1 - k:CONV_WIDTH - k, :]
    return y


def _lru_gates(xc, wa, ba, wx, bx, sp):
    r = jax.nn.sigmoid(_dot(xc, wa) + ba)
    i = jax.nn.sigmoid(_dot(xc, wx) + bx)
    log_a = -LRU_C * r * sp
    a = jnp.exp(log_a)
    b = jnp.sqrt(_neg_expm1(2.0 * log_a)) * (i * xc)
    return a, b


LRU_ROWS = 256


def _lru_prompt_kernel(x_ref, gr_ref, cw_ref, cb_ref, wa_ref, ba_ref, wx_ref, bx_ref, lam_ref,
                       o_ref, hl_ref):
    n = x_ref.shape[0] // LRU_ROWS
    sp = _softplus(-lam_ref[...])
    row = lax.broadcasted_iota(jnp.int32, (LRU_ROWS, LANES), 0)

    def body(c, carry):
        prev8, h = carry
        r0 = pl.multiple_of(c * LRU_ROWS, LRU_ROWS)
        cur = x_ref[pl.ds(r0, LRU_ROWS), :]
        xc = _causal_conv_rows(prev8, cur, cw_ref[...]) + cb_ref[...]
        a, b = _lru_gates(xc, wa_ref[...], ba_ref[...], wx_ref[...], bx_ref[...], sp)
        d = 1
        while d < LRU_ROWS:
            a_s = jnp.where(row >= d, pltpu.roll(a, d, axis=0), 1.0)
            b_s = jnp.where(row >= d, pltpu.roll(b, d, axis=0), 0.0)
            b = a * b_s + b
            a = a * a_s
            d *= 2
        hs = a * h + b
        o_ref[pl.ds(r0, LRU_ROWS), :] = hs * jax.nn.gelu(gr_ref[pl.ds(r0, LRU_ROWS), :])
        return cur[LRU_ROWS - 8:], hs[LRU_ROWS - 1:]

    _, h = lax.fori_loop(0, n, body, (jnp.zeros((8, LANES), F32), jnp.zeros((1, LANES), F32)))
    hl_ref[...] = h


def _lru_prompt(proj, n_batch, seq, x_col, g_col, conv_w, conv_b, wa_bd, b_a, wx_bd, b_x, lam):
    ncb = D_RNN // LANES
    vec = lambda v: v.reshape(1, D_RNN)
    vspec = pl.BlockSpec((1, LANES), lambda b, c: (0, c))
    out, hl = pl.pallas_call(
        _lru_prompt_kernel,
        grid=(n_batch, ncb),
        in_specs=[pl.BlockSpec((seq, LANES), lambda b, c: (b, x_col + c)),
                  pl.BlockSpec((seq, LANES), lambda b, c: (b, g_col + c)),
                  pl.BlockSpec((CONV_WIDTH, LANES), lambda b, c: (0, c)),
                  vspec,
                  pl.BlockSpec((LANES, LANES), lambda b, c: (c, c)), vspec,
                  pl.BlockSpec((LANES, LANES), lambda b, c: (c, c)), vspec,
                  vspec],
        out_specs=[pl.BlockSpec((seq, LANES), lambda b, c: (b, c)),
                   pl.BlockSpec((None, 1, LANES), lambda b, c: (b, 0, c))],
        out_shape=[jax.ShapeDtypeStruct((n_batch * seq, D_RNN), F32),
                   jax.ShapeDtypeStruct((n_batch, 1, D_RNN), F32)],
        compiler_params=_cparams(("parallel", "parallel")),
        name="lru_prompt",
    )(proj, proj, conv_w, vec(conv_b), wa_bd, vec(b_a), wx_bd, vec(b_x), vec(lam))
    return out, hl.reshape(n_batch, D_RNN)


def _lru_sample_kernel(xp_ref, gr_ref, h0_ref, cw_ref, cb_ref, wa_ref, ba_ref, wx_ref, bx_ref, lam_ref,
                       o_ref, hl_ref):
    t_dec = gr_ref.shape[0]
    sp = _softplus(-lam_ref[...])
    h = h0_ref[...]
    for t in range(t_dec):
        xc = cb_ref[...]
        for j in range(CONV_WIDTH):
            xc = xc + xp_ref[t + j] * cw_ref[j:j + 1, :]
        a, b = _lru_gates(xc, wa_ref[...], ba_ref[...], wx_ref[...], bx_ref[...], sp)
        h = a * h + b
        o_ref[t] = h * jax.nn.gelu(gr_ref[t])
    hl_ref[...] = h


def _lru_sample(xr, gr, h0, conv0, conv_w, conv_b, wa_bd, b_a, wx_bd, b_x, lam):
    n_dec, t_dec, _ = xr.shape
    xp = jnp.transpose(jnp.concatenate([conv0, xr], axis=1), (1, 0, 2))
    grt = jnp.transpose(gr, (1, 0, 2))
    vec = lambda v: v.reshape(1, D_RNN)
    full = lambda *s: pl.BlockSpec(s, lambda i: (0,) * len(s))
    out, hl = pl.pallas_call(
        _lru_sample_kernel,
        grid=(1,),
        in_specs=[full(t_dec + CONV_WIDTH - 1, n_dec, D_RNN), full(t_dec, n_dec, D_RNN), full(n_dec, D_RNN),
                  full(CONV_WIDTH, D_RNN), full(1, D_RNN), full(D_RNN, D_RNN), full(1, D_RNN),
                  full(D_RNN, D_RNN), full(1, D_RNN), full(1, D_RNN)],
        out_specs=[full(t_dec, n_dec, D_RNN), full(n_dec, D_RNN)],
        out_shape=[jax.ShapeDtypeStruct((t_dec, n_dec, D_RNN), F32),
                   jax.ShapeDtypeStruct((n_dec, D_RNN), F32)],
        compiler_params=_cparams(("arbitrary",)),
        name="lru_sample",
    )(xp, grt, h0, conv_w, vec(conv_b), wa_bd, vec(b_a), wx_bd, vec(b_x), vec(lam))
    return jnp.transpose(out, (1, 0, 2)), hl


def _block_diag(w):
    n, k, _ = w.shape
    eye = jnp.asarray(np.eye(n, dtype=np.float32))
    return (w[:, :, None, :] * eye[:, None, :, None]).reshape(n * k, n * k)


GDN_ROWS = 256


def _bmm(a, b):
    return jnp.einsum('nij,njk->nik', a.astype(BF16), b.astype(BF16), preferred_element_type=F32)


def _bmm_nt(a, b):
    return jnp.einsum('nid,njd->nij', a.astype(BF16), b.astype(BF16), preferred_element_type=F32)


def _col(x, lane, idx):
    return jnp.sum(jnp.where(lane == idx, x, 0.0), axis=1, keepdims=True)


def _gdn_gates_kernel(x_ref, a_ref, dt_ref, beta_ref, gc_ref, *, chunk, pad_period, pad_rows):
    x = x_ref[...]
    tm = x.shape[0]
    beta = jax.nn.sigmoid(x)
    g = -jnp.exp(a_ref[...]) * _softplus(x + dt_ref[...])
    if pad_period:
        row = lax.broadcasted_iota(jnp.int32, x.shape, 0)
        valid = (row & (pad_period - 1)) >= pad_rows
        beta = jnp.where(valid, beta, 0.0)
        g = jnp.where(valid, g, 0.0)
    r = lax.broadcasted_iota(jnp.int32, (tm, tm), 0)
    c = lax.broadcasted_iota(jnp.int32, (tm, tm), 1)
    sh = chunk.bit_length() - 1
    tri = jnp.where(((r >> sh) == (c >> sh)) & (c <= r), 1.0, 0.0)
    beta_ref[...] = beta
    gc_ref[...] = jnp.dot(tri, g, preferred_element_type=F32, precision=lax.Precision.HIGHEST)


def _gdn_gates(x, col, rows, a_log, dt_bias, chunk, pad_period=0, pad_rows=0):
    tm = GDN_ROWS
    assert rows % tm == 0 and tm % chunk == 0
    a_pad = jnp.pad(a_log.reshape(1, -1), ((0, 0), (GDN_HEADS, LANES - 2 * GDN_HEADS)))
    dt_pad = jnp.pad(dt_bias.reshape(1, -1), ((0, 0), (GDN_HEADS, LANES - 2 * GDN_HEADS)))
    return pl.pallas_call(
        functools.partial(_gdn_gates_kernel, chunk=chunk, pad_period=pad_period, pad_rows=pad_rows),
        grid=(rows // tm,),
        in_specs=[pl.BlockSpec((tm, LANES), lambda i: (i, col)),
                  pl.BlockSpec((1, LANES), lambda i: (0, 0)),
                  pl.BlockSpec((1, LANES), lambda i: (0, 0))],
        out_specs=[pl.BlockSpec((tm, LANES), lambda i: (i, 0))] * 2,
        out_shape=[jax.ShapeDtypeStruct((rows, LANES), F32)] * 2,
        compiler_params=_cparams(("parallel",)),
        name="gdn_gates",
    )(x, a_pad, dt_pad)


def _gdn_prep_kernel(x_ref, cw_ref, beta_ref, gc_ref, *out_refs, mode):
    hd = HEAD_DIM
    n = x_ref.shape[0] // GDN_ROWS
    h0 = 2 * pl.program_id(1)
    lane = lax.broadcasted_iota(jnp.int32, (GDN_ROWS, LANES), 1)
    lo = lane < hd

    def halves(c0, c1):
        return jnp.where(lo, c0, c1)

    def put(ref, r0, val):
        ref[0, pl.ds(r0, GDN_ROWS), :] = val[:, :hd]
        ref[1, pl.ds(r0, GDN_ROWS), :] = val[:, hd:]

    def body(c, prev8):
        r0 = pl.multiple_of(c * GDN_ROWS, GDN_ROWS)
        cur = x_ref[pl.ds(r0, GDN_ROWS), :]
        y = _causal_conv_rows(prev8, cur, cw_ref[...])
        y = y * jax.nn.sigmoid(y)
        if mode in ("q", "k"):
            ss = y * y
            s0 = jnp.sum(jnp.where(lo, ss, 0.0), axis=1, keepdims=True)
            s1 = jnp.sum(jnp.where(lo, 0.0, ss), axis=1, keepdims=True)
            y = y * halves(lax.rsqrt(s0 + NORM_EPS), lax.rsqrt(s1 + NORM_EPS))
        if mode == "q":
            put(out_refs[0], r0, y * (hd ** -0.5))
        else:
            beta = beta_ref[pl.ds(r0, GDN_ROWS), :]
            bb = halves(_col(beta, lane, h0), _col(beta, lane, h0 + 1))
            if mode == "k":
                gc = gc_ref[pl.ds(r0, GDN_ROWS), :]
                put(out_refs[0], r0, y)
                put(out_refs[1], r0, y * bb)
                put(out_refs[2], r0, halves(_col(gc, lane, GDN_HEADS + h0), _col(gc, lane, GDN_HEADS + h0 + 1)))
            else:
                put(out_refs[0], r0, y * bb)
        return cur[GDN_ROWS - 8:]

    lax.fori_loop(0, n, body, jnp.zeros((8, LANES), F32))


def _gdn_prep(x, x_col, conv_w, beta, gc, n_batch, seq, mode):
    sec = {"q": 0, "k": 4, "v": 8}[mode]
    n_out = {"q": 1, "k": 3, "v": 1}[mode]
    hspec = pl.BlockSpec((2, seq, HEAD_DIM), lambda b, c: (b * 4 + c, 0, 0))
    gspec = pl.BlockSpec((seq, LANES), lambda b, c: (b, 0))
    outs = pl.pallas_call(
        functools.partial(_gdn_prep_kernel, mode=mode),
        grid=(n_batch, 4),
        in_specs=[pl.BlockSpec((seq, LANES), lambda b, c: (b, x_col + c)),
                  pl.BlockSpec((CONV_WIDTH, LANES), lambda b, c: (0, sec + c)),
                  gspec, gspec],
        out_specs=[hspec] * n_out,
        out_shape=[jax.ShapeDtypeStruct((n_batch * GDN_HEADS, seq, HEAD_DIM), F32)] * n_out,
        compiler_params=_cparams(("parallel", "parallel")),
        name="gdn_prep_" + mode,
    )(x, conv_w, beta, gc)
    return outs


def _gdn_intra_kernel(q_ref, k_ref, kb_ref, vb_ref, gcb_ref, gcr_ref,
                      u_ref, w_ref, qg_ref, kd_ref, qk_ref, gl_ref, *, chunk):
    hd = HEAD_DIM
    rows = q_ref.shape[0]
    n = rows // chunk
    r3 = lambda ref: ref[...].reshape(n, chunk, hd)
    q, k, kb, vb, gcb = r3(q_ref), r3(k_ref), r3(kb_ref), r3(vb_ref), r3(gcb_ref)
    gcr = gcr_ref[...]
    ci = lax.broadcasted_iota(jnp.int32, (n, chunk, chunk), 1)
    si = lax.broadcasted_iota(jnp.int32, (n, chunk, chunk), 2)
    decay = jnp.exp(jnp.where(ci >= si, gcb[:, :, :chunk] - gcr[:, None, :], NEG_INF))
    a = jnp.where(ci > si, _bmm_nt(kb, k) * decay, 0.0)
    b = -a
    y = b
    p = _bmm(b, b)
    stages = chunk.bit_length() - 2
    for t in range(stages):
        y = y + p + _bmm(y, p)
        if t < stages - 1:
            p = _bmm(p, p)
    eg = jnp.exp(gcb)
    rhs = jnp.concatenate([vb, kb * eg], axis=-1)
    sol = rhs + _bmm(y, rhs)
    gl = gcb[:, chunk - 1:chunk, :]
    u_ref[...] = sol[:, :, :hd].reshape(rows, hd)
    w_ref[...] = sol[:, :, hd:].reshape(rows, hd)
    qg_ref[...] = (q * eg).reshape(rows, hd)
    kd_ref[...] = (k * jnp.exp(gl - gcb)).reshape(rows, hd)
    qk_ref[...] = (_bmm_nt(q, k) * decay).reshape(rows, chunk)
    gl_ref[...] = jnp.exp(gl).reshape(n, hd)


def _gdn_intra(q, k, kb, vb, gcb, gc_row, chunk, rows_per_step):
    bh, seq, hd = q.shape
    tr = rows_per_step
    n = tr // chunk
    assert seq % tr == 0
    hspec = pl.BlockSpec((None, tr, hd), lambda h, i: (h, i, 0))
    shape = jax.ShapeDtypeStruct((bh, seq, hd), F32)
    return pl.pallas_call(
        functools.partial(_gdn_intra_kernel, chunk=chunk),
        grid=(bh, seq // tr),
        in_specs=[hspec] * 5 + [pl.BlockSpec((None, n, chunk), lambda h, i: (h, i, 0))],
        out_specs=[hspec] * 4 + [pl.BlockSpec((None, tr, chunk), lambda h, i: (h, i, 0)),
                                 pl.BlockSpec((None, n, hd), lambda h, i: (h, i, 0))],
        out_shape=[shape] * 4 + [jax.ShapeDtypeStruct((bh, seq, chunk), F32),
                                 jax.ShapeDtypeStruct((bh, seq // chunk, hd), F32)],
        compiler_params=_cparams(("parallel", "parallel")),
        name="gdn_intra",
    )(q, k, kb, vb, gcb, gc_row)


def _gdn_chunk_update(s, u, w, qg, kd, qk, gl):
    v_new = u - _bmm(w, s)
    o = _bmm(qg, s) + _bmm(qk, v_new)
    s = s * gl + jnp.einsum('ncd,nce->nde', kd.astype(BF16), v_new.astype(BF16), preferred_element_type=F32)
    return s, o


def _gdn_scan_kernel(u_ref, w_ref, qg_ref, kd_ref, qk_ref, gl_ref, o_ref, sfin_ref, s_scr, *, chunk):
    j = pl.program_id(1)

    @pl.when(j == 0)
    def _zero():
        s_scr[...] = jnp.zeros_like(s_scr)

    n = u_ref.shape[1] // chunk
    s = s_scr[...]
    for c in range(n):
        sl = slice(c * chunk, (c + 1) * chunk)
        s, o = _gdn_chunk_update(s, u_ref[:, sl, :], w_ref[:, sl, :], qg_ref[:, sl, :], kd_ref[:, sl, :],
                                 qk_ref[:, sl, :], gl_ref[:, c:c + 1, :])
        o_ref[:, sl, :] = o
    s_scr[...] = s

    @pl.when(j == pl.num_programs(1) - 1)
    def _final():
        sfin_ref[...] = s


def _gdn_scan(u, w, qg, kd, qk, gl, chunk, heads_per_step, rows_per_step):
    bh, seq, hd = u.shape
    hb, tr = heads_per_step, rows_per_step
    n = tr // chunk
    hspec = pl.BlockSpec((hb, tr, hd), lambda h, i: (h, i, 0))
    return pl.pallas_call(
        functools.partial(_gdn_scan_kernel, chunk=chunk),
        grid=(bh // hb, seq // tr),
        in_specs=[hspec] * 4 + [pl.BlockSpec((hb, tr, chunk), lambda h, i: (h, i, 0)),
                                pl.BlockSpec((hb, n, hd), lambda h, i: (h, i, 0))],
        out_specs=[hspec, pl.BlockSpec((hb, hd, hd), lambda h, i: (h, 0, 0))],
        out_shape=[jax.ShapeDtypeStruct((bh, seq, hd), F32), jax.ShapeDtypeStruct((bh, hd, hd), F32)],
        scratch_shapes=[pltpu.VMEM((hb, hd, hd), F32)],
        compiler_params=_cparams(("parallel", "arbitrary")),
        name="gdn_scan",
    )(u, w, qg, kd, qk, gl)


GDN_DEC_SEQS_PER_STEP = 16


def _gdn_step_kernel(s0_ref, u_ref, w_ref, qg_ref, kd_ref, qk_ref, gl_ref, o_ref, s_ref, *, chunk):
    hd = HEAD_DIM
    n = s0_ref.shape[0]
    r3 = lambda ref, last: ref[...].reshape(n, chunk, last)
    s, o = _gdn_chunk_update(s0_ref[...], r3(u_ref, hd), r3(w_ref, hd), r3(qg_ref, hd), r3(kd_ref, hd),
                             r3(qk_ref, chunk), gl_ref[...][:, None, :])
    o_ref[...] = o.reshape(n * chunk, hd)
    s_ref[...] = s


def _gdn_step(s0, u, w, qg, kd, qk, gl, chunk):
    n_dec, nh, hd, _ = s0.shape
    sb = GDN_DEC_SEQS_PER_STEP
    assert n_dec % sb == 0
    hspec = pl.BlockSpec((None, sb * chunk, hd), lambda h, i: (h, i, 0))
    sspec = pl.BlockSpec((sb, None, hd, hd), lambda h, i: (i, h, 0, 0))
    return pl.pallas_call(
        functools.partial(_gdn_step_kernel, chunk=chunk),
        grid=(nh, n_dec // sb),
        in_specs=[sspec] + [hspec] * 4 + [pl.BlockSpec((None, sb * chunk, chunk), lambda h, i: (h, i, 0)),
                                         pl.BlockSpec((None, sb, hd), lambda h, i: (h, i, 0))],
        out_specs=[hspec, sspec],
        out_shape=[jax.ShapeDtypeStruct((nh, n_dec * chunk, hd), F32),
                   jax.ShapeDtypeStruct((n_dec, nh, hd, hd), F32)],
        compiler_params=_cparams(("parallel", "parallel")),
        name="gdn_step",
    )(s0, u, w, qg, kd, qk, gl)


def _gdn_post_kernel(o_ref, z_ref, w_ref, out_ref):
    hd = HEAD_DIM
    for h in range(GDN_HEADS):
        o = o_ref[h]
        r = o * lax.rsqrt(jnp.mean(o * o, axis=-1, keepdims=True) + NORM_EPS) * w_ref[...]
        z = z_ref[:, h * hd:(h + 1) * hd]
        out_ref[:, h * hd:(h + 1) * hd] = r * (z * jax.nn.sigmoid(z))


def _gdn_post(o, z, z_col, norm_w, n_batch, seq, tm):
    nt = seq // tm
    return pl.pallas_call(
        _gdn_post_kernel,
        grid=(n_batch, nt),
        in_specs=[pl.BlockSpec((GDN_HEADS, tm, HEAD_DIM), lambda b, i: (b, i, 0)),
                  pl.BlockSpec((tm, 4 * LANES), lambda b, i: (b * nt + i, z_col)),
                  pl.BlockSpec((1, HEAD_DIM), lambda b, i: (0, 0))],
        out_specs=pl.BlockSpec((tm, 4 * LANES), lambda b, i: (b * nt + i, 0)),
        out_shape=jax.ShapeDtypeStruct((n_batch * seq, 4 * LANES), F32),
        compiler_params=_cparams(("parallel", "parallel")),
        name="gdn_post",
    )(o, z, norm_w.reshape(1, HEAD_DIM))


def _gdn_front(x, qkv_col, gate_col, conv_w, a_log, dt_bias, n_batch, seq, chunk, pad_period=0, pad_rows=0):
    rows = n_batch * seq
    beta, gc = _gdn_gates(x, gate_col, rows, a_log, dt_bias, chunk, pad_period, pad_rows)
    (q,) = _gdn_prep(x, qkv_col, conv_w, beta, gc, n_batch, seq, "q")
    k, kb, gcb = _gdn_prep(x, qkv_col + 4, conv_w, beta, gc, n_batch, seq, "k")
    (vb,) = _gdn_prep(x, qkv_col + 8, conv_w, beta, gc, n_batch, seq, "v")
    gc_row = jnp.transpose(gc[:, GDN_HEADS:2 * GDN_HEADS].reshape(n_batch, seq, GDN_HEADS), (0, 2, 1))
    gc_row = gc_row.reshape(n_batch * GDN_HEADS, seq // chunk, chunk)
    return _gdn_intra(q, k, kb, vb, gcb, gc_row, chunk, min(seq, 512))


def _x_l2norm(x):
    return x * lax.rsqrt(jnp.sum(x * x, -1, keepdims=True) + NORM_EPS)


def _x_rms(x, w):
    return x * lax.rsqrt(jnp.mean(x * x, -1, keepdims=True) + NORM_EPS) * w


def _x_t5_bucket(rel):
    n = jnp.maximum(rel, 0)
    max_exact = T5_BUCKETS // 2
    nf = jnp.maximum(n, 1).astype(F32)
    large = max_exact + (jnp.log(nf / max_exact) / math.log(T5_MAX_DIST / max_exact)
                         * (T5_BUCKETS - max_exact)).astype(jnp.int32)
    large = jnp.minimum(large, T5_BUCKETS - 1)
    return jnp.where(n < max_exact, n, large)


def _x_causal_conv(x, buf, w, b=None):
    L = x.shape[1]
    xp = jnp.concatenate([buf.astype(x.dtype), x], axis=1)
    y = xp[:, CONV_WIDTH - 1:] * w[CONV_WIDTH - 1]
    for j in range(CONV_WIDTH - 1):
        y = y + xp[:, j:j + L] * w[j]
    if b is not None:
        y = y + b
    return y, xp[:, L:]


def _x_query_chunk(n_q, batch):
    cap = max(1, min(n_q, 128 // batch))
    return max(d for d in range(1, cap + 1) if n_q % d == 0)


def _x_moba_attention(q, k, v, q0, bias_t):
    B, Lq, Hq, hd = q.shape
    Lk, KV = k.shape[1], k.shape[2]
    G = Hq // KV
    n_full = Lk // MOBA_BLOCK
    n_sel = min(MOBA_TOPK, n_full)
    head_kv = jnp.arange(Hq) // G
    kmean = k[:, :n_full * MOBA_BLOCK].reshape(B, n_full, MOBA_BLOCK, KV, hd).mean(2, dtype=F32)[:, :, head_kv]
    qc = _x_query_chunk(Lq, B)
    nq = Lq // qc
    q_chunks = q.reshape(B, nq, qc, Hq, hd).swapaxes(0, 1)
    starts = q0 + qc * jnp.arange(nq, dtype=jnp.int32)
    b_ix = jnp.arange(B)[:, None, None, None, None]
    h_ix = jnp.arange(Hq)[None, :, None, None, None]
    kv_ix = head_kv[None, :, None, None, None]
    offs = jnp.arange(MOBA_BLOCK)
    scale = hd ** -0.5

    def one_chunk(args):
        qb, s0 = args
        q_pos = s0 + jnp.arange(qc)
        own = jnp.broadcast_to((q_pos // MOBA_BLOCK)[None, None, :, None], (B, Hq, qc, 1))
        gate = jnp.einsum('bqhd,bnhd->bhqn', qb, kmean, preferred_element_type=F32)
        gate = jnp.where(jnp.arange(n_full) < own, gate, -jnp.inf)
        _, sel = lax.top_k(gate, n_sel)
        blocks = jnp.concatenate([sel, own], -1)
        live = jnp.concatenate([sel < own, jnp.ones_like(own, dtype=bool)], -1)
        k_pos = blocks[..., None] * MOBA_BLOCK + offs
        rows = jnp.minimum(k_pos, Lk - 1)
        kg = k[b_ix, rows, kv_ix]
        vg = v[b_ix, rows, kv_ix]
        rel = q_pos[None, None, :, None, None] - k_pos
        s = jnp.einsum('bqhd,bhqsrd->bhqsr', qb, kg, preferred_element_type=F32) * scale
        s = s + bias_t[h_ix, _x_t5_bucket(rel)].astype(F32)
        s = jnp.where(live[..., None] & (rel >= 0), s, -jnp.inf)
        n_keys = s.shape[3] * MOBA_BLOCK
        p = jax.nn.softmax(s.reshape(B, Hq, qc, n_keys), axis=-1).reshape(s.shape)
        o = jnp.einsum('bhqsr,bhqsrd->bqhd', p, vg, preferred_element_type=F32)
        return o.astype(q.dtype)

    out = lax.map(one_chunk, (q_chunks, starts))
    return out.swapaxes(0, 1).reshape(B, Lq, Hq, hd)


def _x_gated_delta_rule(q, k, v, g, beta, s0):
    B, L, H, dk = q.shape
    dv = v.shape[-1]
    C = min(GDN_CHUNK, L)
    pad = (-L) % C
    N = (L + pad) // C

    def chunks(t):
        t = jnp.pad(t.astype(F32), [(0, 0), (0, pad)] + [(0, 0)] * (t.ndim - 2))
        t = t.reshape((B, N, C) + t.shape[2:])
        return jnp.swapaxes(jnp.swapaxes(t, 0, 1), 2, 3)

    q = chunks(q) * dk ** -0.5
    k = chunks(k)
    v = chunks(v)
    g = chunks(g)
    beta = chunks(beta)
    gc = jnp.cumsum(g, -1)
    incl = jnp.tril(jnp.ones((C, C), bool))
    strict = jnp.tril(jnp.ones((C, C), bool), -1)
    decay = jnp.exp(jnp.where(incl, gc[..., :, None] - gc[..., None, :], -jnp.inf))
    kb = k * beta[..., None]
    lower = jnp.where(strict, jnp.einsum('nbhcd,nbhsd->nbhcs', kb, k) * decay, 0.0)
    t_mat = lower + jnp.eye(C, dtype=F32)
    rhs = jnp.concatenate([v * beta[..., None], kb * jnp.exp(gc)[..., None]], -1)
    sol = lax.linalg.triangular_solve(t_mat, rhs, left_side=True, lower=True, unit_diagonal=True)
    u, w = sol[..., :dv], sol[..., dv:]
    qk = jnp.einsum('nbhcd,nbhsd->nbhcs', q, k) * decay
    qg = q * jnp.exp(gc)[..., None]
    kd = k * jnp.exp(gc[..., -1:] - gc)[..., None]
    glast = jnp.exp(gc[..., -1])[..., None, None]

    def step(S, xs):
        u_n, w_n, qk_n, qg_n, kd_n, gl_n = xs
        v_new = u_n - jnp.einsum('bhcd,bhde->bhce', w_n, S)
        o_n = jnp.einsum('bhcd,bhde->bhce', qg_n, S) + jnp.einsum('bhcs,bhse->bhce', qk_n, v_new)
        S = S * gl_n + jnp.einsum('bhcd,bhce->bhde', kd_n, v_new)
        return S, o_n

    S, o = lax.scan(step, s0.astype(F32), (u, w, qk, qg, kd, glast))
    o = jnp.swapaxes(jnp.swapaxes(o, 2, 3), 0, 1).reshape(B, N * C, H, dv)[:, :L]
    return o, S


def _x_sink_attention(q, k, v, q_pos, k_pos, sinks, bias_t):
    *lead, Lq, Hq, hd = q.shape
    KV = k.shape[-2]
    G = Hq // KV
    qg = q.reshape(*lead, Lq, KV, G, hd)
    s = jnp.einsum('...qkgd,...skd->...kgqs', qg, k, preferred_element_type=F32) * hd ** -0.5
    rel = q_pos[..., :, None] - k_pos[..., None, :]
    ok = (rel >= 0) & (rel <= WINDOW) & (k_pos[..., None, :] >= 0)
    bias = jnp.moveaxis(bias_t.reshape(KV, G, T5_BUCKETS)[:, :, _x_t5_bucket(rel)], (0, 1), (-4, -3))
    s = jnp.where(ok[..., None, None, :, :], s + bias.astype(F32), -jnp.inf)
    sink = sinks.astype(F32).reshape(KV, G)[:, :, None, None]
    m = jnp.maximum(s.max(-1, keepdims=True), sink)
    p = jnp.exp(s - m)
    den = p.sum(-1, keepdims=True) + jnp.exp(sink - m)
    o = jnp.einsum('...kgqs,...skd->...qkgd', p / den, v, preferred_element_type=F32)
    return o.reshape(*lead, Lq, Hq, hd).astype(q.dtype)


def _x_swa_prompt(q, k, v, sinks, bias_t):
    B, L, Hq, hd = q.shape
    W = WINDOW
    nb = L // W

    def band(t):
        tb = t.reshape(B, nb, W, t.shape[2], hd)
        prev = jnp.concatenate([jnp.zeros_like(tb[:, :1]), tb[:, :-1]], axis=1)
        return jnp.concatenate([prev, tb], axis=2)

    base = W * jnp.arange(nb)[:, None]
    q_pos = base + jnp.arange(W)
    k_pos = base - W + jnp.arange(2 * W)
    qb = q.reshape(B, nb, W, Hq, hd)
    o = _x_sink_attention(qb, band(k), band(v), q_pos, k_pos, sinks, bias_t)
    return o.reshape(B, nb * W, Hq, hd)


def _x_lru_combine(x, y):
    a1, b1 = x
    a2, b2 = y
    return a1 * a2, a2 * b1 + b2


def _x_rg_lru(x, h0, w_a, b_a, w_x, b_x, lam):
    B, L, _ = x.shape
    xf = x.astype(F32)
    xb = xf.reshape(B, L, LRU_BLOCKS, D_RNN // LRU_BLOCKS)
    r = jax.nn.sigmoid(jnp.einsum('blni,nij->blnj', xb, w_a).reshape(B, L, D_RNN) + b_a)
    i = jax.nn.sigmoid(jnp.einsum('blni,nij->blnj', xb, w_x).reshape(B, L, D_RNN) + b_x)
    log_a = -LRU_C * r * jax.nn.softplus(-lam.astype(F32))
    a = jnp.exp(log_a)
    b = jnp.sqrt(-jnp.expm1(2.0 * log_a)) * (i * xf)
    b = b.at[:, 0].add(a[:, 0] * h0.astype(F32))
    _, h = lax.associative_scan(_x_lru_combine, (a, b), axis=1)
    return h


def _x_even_rest(p, b, l, past_k, past_v, s0, conv0, conv_w, a_log, dt_bias, norm_w, bias_t, o_a=None):
    qa, ka, va, qkv, z, beta_in, decay_in = p
    if o_a is None:
        qa4 = qa.reshape(b, l, N_ATT_HEADS, HEAD_DIM)
        ka4 = ka.reshape(b, l, KV_HEADS_A, HEAD_DIM)
        va4 = va.reshape(b, l, KV_HEADS_A, HEAD_DIM)
        past = past_k.shape[1]
        k_all = jnp.concatenate([past_k, ka4], axis=1)
        v_all = jnp.concatenate([past_v, va4], axis=1)
        o_a = _x_moba_attention(qa4, k_all, v_all, past, bias_t).reshape(b, l, -1)
    c, conv_new = _x_causal_conv(qkv, conv0, conv_w)
    c = jax.nn.silu(c)
    qb, kb, vb = jnp.split(c, [512, 1024], axis=-1)
    qb = _x_l2norm(qb.reshape(b, l, GDN_HEADS, 64))
    kb = _x_l2norm(kb.reshape(b, l, GDN_HEADS, 64))
    vb = vb.reshape(b, l, GDN_HEADS, 64)
    beta = jax.nn.sigmoid(beta_in)
    g = -jnp.exp(a_log) * jax.nn.softplus(decay_in + dt_bias)
    o_b, s_new = _x_gated_delta_rule(qb, kb, vb, g, beta, s0)
    gate = jax.nn.silu(z.reshape(b, l, GDN_HEADS, 64))
    o_b = _x_rms(o_b, norm_w) * gate
    mixed = jnp.concatenate([o_a, o_b.reshape(b, l, -1)], axis=-1)
    return mixed, s_new, conv_new


def _x_odd_rest(p, b, l, buf_k, buf_v, h0, conv0, past_len, sinks, conv_w, conv_b,
                w_a, b_a, w_x, b_x, lam, bias_t):
    qc, kc, vc, xr, gr = p
    qc = qc.reshape(b, l, N_ATT_HEADS, HEAD_DIM)
    kc = kc.reshape(b, l, KV_HEADS_C, HEAD_DIM)
    vc = vc.reshape(b, l, KV_HEADS_C, HEAD_DIM)
    if buf_k is None:
        o_c = _x_swa_prompt(qc, kc, vc, sinks, bias_t)
        k_keep, v_keep = kc[:, -WINDOW:], vc[:, -WINDOW:]
    else:
        nbuf = buf_k.shape[1]
        k_all = jnp.concatenate([buf_k, kc], axis=1)
        v_all = jnp.concatenate([buf_v, vc], axis=1)
        q_pos = past_len + jnp.arange(l)
        k_pos = past_len - nbuf + jnp.arange(nbuf + l)
        o_c = _x_sink_attention(qc, k_all, v_all, q_pos, k_pos, sinks, bias_t)
        k_keep, v_keep = k_all[:, -nbuf:], v_all[:, -nbuf:]
    xc, conv_new = _x_causal_conv(xr, conv0, conv_w, conv_b)
    h = _x_rg_lru(xc, h0, w_a, b_a, w_x, b_x, lam)
    o_d = h * jax.nn.gelu(gr)
    mixed = jnp.concatenate([o_c.reshape(b, l, -1), o_d], axis=-1)
    return mixed, k_keep, v_keep, h[:, -1], conv_new


TM = 512


def _pad_cols(w, n):
    return jnp.pad(w, ((0, 0), (0, n - w.shape[1])))


def _x_kernel_fallback(x_prompt, x_sample, cache_k_moba, cache_v_moba, state_gdn, state_gdn_conv, cache_k_swa, cache_v_swa, state_lru, state_lru_conv, page_table, w_in_even, w_out_even, gdn_conv_w, gdn_a_log, gdn_dt_bias, gdn_norm_w, w_in_odd, w_out_odd, swa_sinks, lru_conv_w, lru_conv_b, lru_w_a, lru_b_a, lru_w_x, lru_b_x, lru_lambda, rel_bias, ln_g, ln_b, router_w, router_b, w_gate, w_up, w_down):
    n_pr, seq, d = x_prompt.shape
    n_dec, dec_seq, _ = x_sample.shape
    tp = n_pr * seq
    ts = n_dec * dec_seq
    past_len = page_table.shape[1] * PAGE_SIZE
    bias_t = rel_bias.T
    h = jnp.concatenate([x_prompt.reshape(tp, d), x_sample.reshape(ts, d)], axis=0)

    rw_pad = _pad_cols(router_w, LANES)
    rb_pad = _pad_cols(router_b.reshape(1, -1), LANES)
    wg_b, wu_b, wd_b = w_gate.astype(BF16), w_up.astype(BF16), w_down.astype(BF16)

    p_even = sum(EVEN_SIZES)
    n_even = 3200
    w0 = _pad_cols(w_in_even[0], n_even).astype(BF16)
    proj = _mm(h, w0, TM, 640)
    splits = np.cumsum(EVEN_SIZES)[:-1].tolist()
    pp = [t.reshape(n_pr, seq, -1) for t in jnp.split(proj[:tp, :p_even], splits, axis=-1)]
    ps = [t.reshape(n_dec, dec_seq, -1) for t in jnp.split(proj[tp:, :p_even], splits, axis=-1)]
    moba_k_prompt = pp[1].reshape(n_pr, seq, 1, KV_HEADS_A, HEAD_DIM)
    moba_v_prompt = pp[2].reshape(n_pr, seq, 1, KV_HEADS_A, HEAD_DIM)
    moba_k_sample = ps[1].reshape(n_dec, dec_seq, 1, KV_HEADS_A, HEAD_DIM)
    moba_v_sample = ps[2].reshape(n_dec, dec_seq, 1, KV_HEADS_A, HEAD_DIM)

    o_a_p = _moba_prompt(proj, rel_bias, n_pr, seq, 0, 4, 6).reshape(n_pr, seq, -1)
    mixed_p, gdn_sp, gdn_cp = _x_even_rest(
        pp, n_pr, seq, None, None, jnp.zeros((n_pr, GDN_HEADS, 64, 64), F32),
        jnp.zeros((n_pr, CONV_WIDTH - 1, GDN_CONV_CH), F32),
        gdn_conv_w[0], gdn_a_log[0], gdn_dt_bias[0], gdn_norm_w[0], bias_t, o_a=o_a_p)
    past_k = cache_k_moba[page_table][:, :, :, 0].reshape(n_dec, past_len, KV_HEADS_A, HEAD_DIM)
    past_v = cache_v_moba[page_table][:, :, :, 0].reshape(n_dec, past_len, KV_HEADS_A, HEAD_DIM)
    mixed_s, gdn_ss, gdn_cs = _x_even_rest(
        ps, n_dec, dec_seq, past_k, past_v, state_gdn[0], state_gdn_conv[0],
        gdn_conv_w[0], gdn_a_log[0], gdn_dt_bias[0], gdn_norm_w[0], bias_t)
    mixed = jnp.concatenate([mixed_p.reshape(tp, -1), mixed_s.reshape(ts, -1)], axis=0)
    h = _mm_ln(mixed, w_out_even[0].astype(BF16), h, ln_g[0, 0], ln_b[0, 0], TM)
    h = _moe_ln(h, rw_pad, rb_pad, wg_b[0], wu_b[0], wd_b[0], ln_g[0, 1], ln_b[0, 1], TM)

    p_odd = sum(ODD_SIZES)
    proj = _mm(h, w_in_odd[0].astype(BF16), TM, 896)
    splits = np.cumsum(ODD_SIZES)[:-1].tolist()
    pp = [t.reshape(n_pr, seq, -1) for t in jnp.split(proj[:tp], splits, axis=-1)]
    ps = [t.reshape(n_dec, dec_seq, -1) for t in jnp.split(proj[tp:], splits, axis=-1)]
    mixed_p, swa_kp, swa_vp, lru_p, lru_cp = _x_odd_rest(
        pp, n_pr, seq, None, None, jnp.zeros((n_pr, D_RNN), F32),
        jnp.zeros((n_pr, CONV_WIDTH - 1, D_RNN), F32), 0,
        swa_sinks[0], lru_conv_w[0], lru_conv_b[0], lru_w_a[0], lru_b_a[0], lru_w_x[0], lru_b_x[0],
        lru_lambda[0], bias_t)
    mixed_s, swa_ks, swa_vs, lru_s, lru_cs = _x_odd_rest(
        ps, n_dec, dec_seq, cache_k_swa[0], cache_v_swa[0], state_lru[0], state_lru_conv[0], past_len,
        swa_sinks[0], lru_conv_w[0], lru_conv_b[0], lru_w_a[0], lru_b_a[0], lru_w_x[0], lru_b_x[0],
        lru_lambda[0], bias_t)
    mixed = jnp.concatenate([mixed_p.reshape(tp, -1), mixed_s.reshape(ts, -1)], axis=0)
    h = _mm_ln(mixed, w_out_odd[0].astype(BF16), h, ln_g[1, 0], ln_b[1, 0], TM)
    h = _moe_ln(h, rw_pad, rb_pad, wg_b[1], wu_b[1], wd_b[1], ln_g[1, 1], ln_b[1, 1], TM)

    y_prompt = h[:tp].reshape(n_pr, seq, d)
    y_sample = h[tp:].reshape(n_dec, dec_seq, d)
    return (y_prompt, y_sample, moba_k_prompt, moba_v_prompt, moba_k_sample, moba_v_sample,
            gdn_sp[None], gdn_ss[None], gdn_cp[None], gdn_cs[None],
            swa_kp[None], swa_vp[None], swa_ks[None], swa_vs[None],
            lru_p[None], lru_s[None], lru_cp[None], lru_cs[None])


GDN_DEC_PERIOD = 8
EVEN_COLS = 3200


def kernel(x_prompt, x_sample, cache_k_moba, cache_v_moba, state_gdn, state_gdn_conv, cache_k_swa, cache_v_swa, state_lru, state_lru_conv, page_table, w_in_even, w_out_even, gdn_conv_w, gdn_a_log, gdn_dt_bias, gdn_norm_w, w_in_odd, w_out_odd, swa_sinks, lru_conv_w, lru_conv_b, lru_w_a, lru_b_a, lru_w_x, lru_b_x, lru_lambda, rel_bias, ln_g, ln_b, router_w, router_b, w_gate, w_up, w_down):
    n_pr, seq, d = x_prompt.shape
    n_dec, dec_seq, _ = x_sample.shape
    tp = n_pr * seq
    ts = n_dec * dec_seq
    n_pool = cache_k_moba.shape[0]
    kva = KV_HEADS_A * HEAD_DIM
    kvc = KV_HEADS_C * HEAD_DIM
    h = jnp.concatenate([x_prompt.reshape(tp, d), x_sample.reshape(ts, d)], axis=0)

    rw_pad = _pad_cols(router_w, LANES)
    rb_pad = _pad_cols(router_b.reshape(1, -1), LANES)
    wg_b, wu_b, wd_b = w_gate.astype(BF16), w_up.astype(BF16), w_down.astype(BF16)

    proj = _mm(h, _pad_cols(w_in_even[0], EVEN_COLS).astype(BF16), TM, 640)
    ps = proj[tp:]
    moba_k_prompt = proj[:tp, 512:512 + kva].reshape(n_pr, seq, 1, KV_HEADS_A, HEAD_DIM)
    moba_v_prompt = proj[:tp, 768:768 + kva].reshape(n_pr, seq, 1, KV_HEADS_A, HEAD_DIM)
    k_new = ps[:, 512:512 + kva].reshape(n_dec, dec_seq, kva)
    v_new = ps[:, 768:768 + kva].reshape(n_dec, dec_seq, kva)
    moba_k_sample = k_new.reshape(n_dec, dec_seq, 1, KV_HEADS_A, HEAD_DIM)
    moba_v_sample = v_new.reshape(n_dec, dec_seq, 1, KV_HEADS_A, HEAD_DIM)

    oa_p = _moba_prompt(proj, rel_bias, n_pr, seq, 0, 4, 6)
    oa_s = _moba_sample(ps[:, :512].reshape(n_dec, dec_seq, N_ATT_HEADS, HEAD_DIM), k_new, v_new,
                        cache_k_moba.reshape(n_pool, PAGE_SIZE, -1),
                        cache_v_moba.reshape(n_pool, PAGE_SIZE, -1), page_table, rel_bias, layer=0)

    parts = _gdn_front(proj, 8, 24, gdn_conv_w[0], gdn_a_log[0], gdn_dt_bias[0], n_pr, seq, GDN_CHUNK)
    o_p, s_fin = _gdn_scan(*parts, GDN_CHUNK, GDN_HEADS, 512)
    ob_p = _gdn_post(o_p, proj, 5, gdn_norm_w[0], n_pr, seq, 512)
    gdn_state_prompt = s_fin.reshape(1, n_pr, GDN_HEADS, HEAD_DIM, HEAD_DIM)
    gdn_conv_prompt = proj[:tp, 1024:1024 + GDN_CONV_CH].reshape(n_pr, seq, GDN_CONV_CH)[None, :, seq - 3:]

    period = GDN_DEC_PERIOD
    lead = period - dec_seq
    qkv_s = ps[:, 1024:1024 + GDN_CONV_CH].reshape(n_dec, dec_seq, GDN_CONV_CH)
    x8 = jnp.concatenate([jnp.zeros((n_dec, lead - 3, GDN_CONV_CH), F32), state_gdn_conv[0], qkv_s], axis=1)
    tail8 = jnp.pad(ps[:, 2560:EVEN_COLS].reshape(n_dec, dec_seq, EVEN_COLS - 2560), ((0, 0), (lead, 0), (0, 0)))
    xs = jnp.concatenate([x8, tail8], axis=-1).reshape(n_dec * period, EVEN_COLS - 1024)
    parts = _gdn_front(xs, 0, 16, gdn_conv_w[0], gdn_a_log[0], gdn_dt_bias[0], 1, n_dec * period, period,
                       period, lead)
    o_s, gdn_state_sample = _gdn_step(state_gdn[0], *parts, period)
    ob_s = _gdn_post(o_s, xs, 3, gdn_norm_w[0], 1, n_dec * period, 512)
    ob_s = ob_s.reshape(n_dec, period, 512)[:, lead:].reshape(ts, 512)
    gdn_conv_sample = qkv_s[None, :, dec_seq - 3:]

    oa = jnp.concatenate([oa_p, oa_s.reshape(ts, 512)], axis=0)
    ob = jnp.concatenate([ob_p, ob_s], axis=0)
    h = _mm_ln(oa, ob, w_out_even[0].astype(BF16), h, ln_g[0, 0], ln_b[0, 0], TM)
    h = _moe_ln(h, rw_pad, rb_pad, wg_b[0], wu_b[0], wd_b[0], ln_g[0, 1], ln_b[0, 1], TM)

    proj = _mm(h, w_in_odd[0].astype(BF16), TM, 896)
    ps = proj[tp:]
    wa_bd, wx_bd = _block_diag(lru_w_a[0]), _block_diag(lru_w_x[0])
    oc_p = _swa_prompt(proj, rel_bias, swa_sinks[0], n_pr, seq, 0, 4, 5)
    od_p, lru_p = _lru_prompt(proj, n_pr, seq, 6, 10, lru_conv_w[0], lru_conv_b[0], wa_bd, lru_b_a[0],
                              wx_bd, lru_b_x[0], lru_lambda[0])
    swa_k_prompt = proj[:tp, 512:512 + kvc].reshape(n_pr, seq, KV_HEADS_C, HEAD_DIM)[None, :, seq - WINDOW:]
    swa_v_prompt = proj[:tp, 640:640 + kvc].reshape(n_pr, seq, KV_HEADS_C, HEAD_DIM)[None, :, seq - WINDOW:]
    lru_conv_prompt = proj[:tp, 768:768 + D_RNN].reshape(n_pr, seq, D_RNN)[None, :, seq - 3:]

    nbuf = cache_k_swa.shape[2]
    kc_s = ps[:, 512:512 + kvc].reshape(n_dec, dec_seq, kvc)
    vc_s = ps[:, 640:640 + kvc].reshape(n_dec, dec_seq, kvc)
    buf_k = cache_k_swa[0].reshape(n_dec, nbuf, kvc)
    buf_v = cache_v_swa[0].reshape(n_dec, nbuf, kvc)
    oc_s = _swa_sample(ps[:, :512].reshape(n_dec, dec_seq, N_ATT_HEADS, HEAD_DIM), kc_s, vc_s, buf_k, buf_v,
                       rel_bias, swa_sinks[0])
    swa_k_sample = jnp.concatenate([buf_k, kc_s], axis=1)[:, dec_seq:].reshape(1, n_dec, nbuf, KV_HEADS_C, HEAD_DIM)
    swa_v_sample = jnp.concatenate([buf_v, vc_s], axis=1)[:, dec_seq:].reshape(1, n_dec, nbuf, KV_HEADS_C, HEAD_DIM)
    xr_s = ps[:, 768:768 + D_RNN].reshape(n_dec, dec_seq, D_RNN)
    gr_s = ps[:, 1280:1280 + D_RNN].reshape(n_dec, dec_seq, D_RNN)
    od_s, lru_s = _lru_sample(xr_s, gr_s, state_lru[0], state_lru_conv[0], lru_conv_w[0], lru_conv_b[0],
                              wa_bd, lru_b_a[0], wx_bd, lru_b_x[0], lru_lambda[0])
    lru_conv_sample = jnp.concatenate([state_lru_conv[0], xr_s], axis=1)[None, :, dec_seq:]

    oc = jnp.concatenate([oc_p, oc_s.reshape(ts, 512)], axis=0)
    od = jnp.concatenate([od_p, od_s.reshape(ts, 512)], axis=0)
    h = _mm_ln(oc, od, w_out_odd[0].astype(BF16), h, ln_g[1, 0], ln_b[1, 0], TM)
    h = _moe_ln(h, rw_pad, rb_pad, wg_b[1], wu_b[1], wd_b[1], ln_g[1, 1], ln_b[1, 1], TM)

    y_prompt = h[:tp].reshape(n_pr, seq, d)
    y_sample = h[tp:].reshape(n_dec, dec_seq, d)
    return (y_prompt, y_sample, moba_k_prompt, moba_v_prompt, moba_k_sample, moba_v_sample,
            gdn_state_prompt, gdn_state_sample[None], gdn_conv_prompt, gdn_conv_sample,
            swa_k_prompt, swa_v_prompt, swa_k_sample, swa_v_sample,
            lru_p[None], lru_s[None], lru_conv_prompt, lru_conv_sample)
```

```python
import functools
import math

import jax
import jax.numpy as jnp
import numpy as np
from jax import lax
from jax.experimental import pallas as pl
from jax.experimental.pallas import tpu as pltpu

F32 = jnp.float32
BF16 = jnp.bfloat16

D_MODEL = 1024
HEAD_DIM = 64
N_ATT_HEADS = 8
KV_HEADS_A = 4
KV_HEADS_C = 2
MOBA_BLOCK = 256
MOBA_TOPK = 3
PAGE_SIZE = 128
GDN_HEADS = 8
GDN_CHUNK = 64
GDN_CONV_CH = 1536
CONV_WIDTH = 4
WINDOW = 128
D_RNN = 512
LRU_BLOCKS = 8
LRU_C = 8.0
T5_BUCKETS = 32
T5_MAX_DIST = 128
N_EXPERTS = 16
N_GROUPS = 4
EXPERTS_PER_GROUP = 4
D_EXPERT = 256
DEPTH = 2
DEEPNORM_ALPHA = (2 * DEPTH) ** 0.25
LN_EPS = 1e-5
NORM_EPS = 1e-6
EVEN_SIZES = (512, 256, 256, 1536, 512, 8, 8)
ODD_SIZES = (512, 128, 128, 512, 512)

LANES = 128
VMEM_LIMIT = 56 * 1024 * 1024
NEG_INF = float("-inf")


def _cparams(sem):
    return pltpu.CompilerParams(dimension_semantics=sem, vmem_limit_bytes=VMEM_LIMIT)


def _dot(a, b):
    return jnp.dot(a.astype(BF16), b.astype(BF16), preferred_element_type=F32)


def _dot_nt(a, b):
    return lax.dot_general(a.astype(BF16), b.astype(BF16), (((1,), (1,)), ((), ())),
                           preferred_element_type=F32)


def _mm_kernel(x_ref, w_ref, o_ref, *, tn):
    xb = x_ref[...].astype(BF16)
    for c in range(w_ref.shape[1] // tn):
        o_ref[:, c * tn:(c + 1) * tn] = jnp.dot(xb, w_ref[:, c * tn:(c + 1) * tn], preferred_element_type=F32)


def _mm(x, w, tm, tn):
    m, k = x.shape
    n = w.shape[1]
    assert m % tm == 0 and n % tn == 0
    return pl.pallas_call(
        functools.partial(_mm_kernel, tn=tn),
        grid=(m // tm,),
        in_specs=[pl.BlockSpec((tm, k), lambda i: (i, 0)),
                  pl.BlockSpec((k, n), lambda i: (0, 0))],
        out_specs=pl.BlockSpec((tm, n), lambda i: (i, 0)),
        out_shape=jax.ShapeDtypeStruct((m, n), F32),
        compiler_params=_cparams(("parallel",)),
        name="in_proj",
    )(x, w)


def _layer_norm_rows(z, g, b):
    mu = jnp.mean(z, axis=-1, keepdims=True)
    zc = z - mu
    var = jnp.mean(zc * zc, axis=-1, keepdims=True)
    return zc * lax.rsqrt(var + LN_EPS) * g + b


def _mm_ln_kernel(xa_ref, xb_ref, wa_ref, wb_ref, h_ref, g_ref, b_ref, o_ref):
    f = (jnp.dot(xa_ref[...].astype(BF16), wa_ref[...], preferred_element_type=F32)
         + jnp.dot(xb_ref[...].astype(BF16), wb_ref[...], preferred_element_type=F32))
    o_ref[...] = _layer_norm_rows(DEEPNORM_ALPHA * h_ref[...] + f, g_ref[...], b_ref[...])


def _mm_ln(xa, xb, w, h, g, b, tm):
    m, ka = xa.shape
    kb = xb.shape[1]
    n = w.shape[1]
    assert m % tm == 0 and ka == kb
    return pl.pallas_call(
        _mm_ln_kernel,
        grid=(m // tm,),
        in_specs=[pl.BlockSpec((tm, ka), lambda i: (i, 0)),
                  pl.BlockSpec((tm, kb), lambda i: (i, 0)),
                  pl.BlockSpec((ka, n), lambda i: (0, 0)),
                  pl.BlockSpec((kb, n), lambda i: (1, 0)),
                  pl.BlockSpec((tm, n), lambda i: (i, 0)),
                  pl.BlockSpec((1, n), lambda i: (0, 0)),
                  pl.BlockSpec((1, n), lambda i: (0, 0))],
        out_specs=pl.BlockSpec((tm, n), lambda i: (i, 0)),
        out_shape=jax.ShapeDtypeStruct((m, n), F32),
        compiler_params=_cparams(("parallel",)),
        name="out_proj_ln",
    )(xa, xb, w, w, h, g.reshape(1, n), b.reshape(1, n))


def _row_max_first(vals, lane):
    mx = jnp.max(vals, axis=1, keepdims=True)
    idx = jnp.min(jnp.where(vals == mx, lane, LANES), axis=1, keepdims=True)
    return mx, idx


def _moe_ln_kernel(t_ref, rw_ref, rb_ref, wg_ref, wu_ref, wd_ref, g_ref, b_ref, o_ref,
                   gates_ref, tb_ref, hb_ref, acc_ref):
    e = pl.program_id(1)

    @pl.when(e == 0)
    def _route():
        t = t_ref[...]
        logits = lax.dot_general(rw_ref[...], t, (((1,), (1,)), ((), ())), preferred_element_type=F32,
                                 precision=lax.Precision.HIGHEST)[:N_EXPERTS]
        s = jax.nn.sigmoid(logits)
        sb = s + rb_ref[...]
        row = lax.broadcasted_iota(jnp.int32, sb.shape, 0)
        grp = row // EXPERTS_PER_GROUP

        def max_first(vals):
            mx = jnp.max(vals, axis=0, keepdims=True)
            return mx, jnp.min(jnp.where(vals == mx, row, N_EXPERTS), axis=0, keepdims=True)

        best = None
        gsel = None
        for gi in range(N_GROUPS):
            vals = jnp.where(grp == gi, sb, NEG_INF)
            m1, i1 = max_first(vals)
            m2 = jnp.max(jnp.where(row == i1, NEG_INF, vals), axis=0, keepdims=True)
            score = m1 + m2
            if gi == 0:
                best, gsel = score, jnp.zeros_like(i1)
            else:
                upd = score > best
                best = jnp.where(upd, score, best)
                gsel = jnp.where(upd, gi, gsel)
        vals = jnp.where(grp == gsel, sb, NEG_INF)
        _, i1 = max_first(vals)
        _, i2 = max_first(jnp.where(row == i1, NEG_INF, vals))
        s1 = jnp.sum(jnp.where(row == i1, s, 0.0), axis=0, keepdims=True)
        s2 = jnp.sum(jnp.where(row == i2, s, 0.0), axis=0, keepdims=True)
        den = s1 + s2
        gates_t = jnp.where(row == i1, s1 / den, 0.0) + jnp.where(row == i2, s2 / den, 0.0)
        gates_t = jnp.concatenate([gates_t, jnp.zeros((LANES - N_EXPERTS, gates_t.shape[1]), F32)], axis=0)
        gates_ref[...] = gates_t.T
        tb_ref[...] = t.astype(BF16)
        acc_ref[...] = jnp.zeros_like(acc_ref)

    tb = tb_ref[...]
    gates = gates_ref[...]
    lane = lax.broadcasted_iota(jnp.int32, gates.shape, 1)
    hg = jnp.dot(tb, wg_ref[...], preferred_element_type=F32)
    hu = jnp.dot(tb, wu_ref[...], preferred_element_type=F32)
    for j in range(EXPERTS_PER_GROUP):
        sl = slice(j * D_EXPERT, (j + 1) * D_EXPERT)
        ge = jnp.sum(jnp.where(lane == e * EXPERTS_PER_GROUP + j, gates, 0.0), axis=1, keepdims=True)
        g = hg[:, sl]
        hb_ref[:, sl] = ((g * jax.nn.sigmoid(g)) * hu[:, sl] * ge).astype(BF16)
    acc_ref[...] += jnp.dot(hb_ref[...], wd_ref[...], preferred_element_type=F32)

    @pl.when(e == pl.num_programs(1) - 1)
    def _finish():
        o_ref[...] = _layer_norm_rows(DEEPNORM_ALPHA * t_ref[...] + acc_ref[...], g_ref[...], b_ref[...])


def _group_experts(w_gate, w_up, w_down):
    ne, d, f = w_gate.shape
    es = EXPERTS_PER_GROUP
    side = lambda w: jnp.transpose(w.astype(BF16).reshape(ne // es, es, d, f), (0, 2, 1, 3)).reshape(ne // es, d, es * f)
    return side(w_gate), side(w_up), w_down.astype(BF16).reshape(ne // es, es * f, d)


def _moe_ln(t, router_w_pad, router_b_pad, wg, wu, wd, g, b, tm):
    m, d = t.shape
    ng, _, gf = wg.shape
    assert m % tm == 0
    return pl.pallas_call(
        _moe_ln_kernel,
        grid=(m // tm, ng),
        in_specs=[pl.BlockSpec((tm, d), lambda i, e: (i, 0)),
                  pl.BlockSpec((LANES, d), lambda i, e: (0, 0)),
                  pl.BlockSpec((N_EXPERTS, 1), lambda i, e: (0, 0)),
                  pl.BlockSpec((None, d, gf), lambda i, e: (e, 0, 0)),
                  pl.BlockSpec((None, d, gf), lambda i, e: (e, 0, 0)),
                  pl.BlockSpec((None, gf, d), lambda i, e: (e, 0, 0)),
                  pl.BlockSpec((1, d), lambda i, e: (0, 0)),
                  pl.BlockSpec((1, d), lambda i, e: (0, 0))],
        out_specs=pl.BlockSpec((tm, d), lambda i, e: (i, 0)),
        out_shape=jax.ShapeDtypeStruct((m, d), F32),
        scratch_shapes=[pltpu.VMEM((tm, LANES), F32),
                        pltpu.VMEM((tm, d), BF16),
                        pltpu.VMEM((tm, gf), BF16),
                        pltpu.VMEM((tm, d), F32)],
        compiler_params=_cparams(("parallel", "arbitrary")),
        name="moe_ln",
    )(t, router_w_pad, router_b_pad, wg, wu, wd, g.reshape(1, d), b.reshape(1, d))


def _t5_bucket_np(rel):
    n = np.maximum(rel, 0)
    max_exact = T5_BUCKETS // 2
    nf = np.maximum(n, 1).astype(np.float32)
    large = max_exact + (np.log(nf / np.float32(max_exact)) / np.float32(math.log(T5_MAX_DIST / max_exact))
                         * np.float32(T5_BUCKETS - max_exact)).astype(np.int32)
    large = np.minimum(large, T5_BUCKETS - 1)
    return np.where(n < max_exact, n, large).astype(np.int32)


def _old_moba_prompt_kernel(q_ref, k_ref, v_ref, bd_ref, bp_ref, bf_ref, o_ref,
                        kmean_ref, sel_ref, m_ref, l_ref, acc_ref):
    i = pl.program_id(2)
    blk = MOBA_BLOCK
    nblk = k_ref.shape[0] // blk
    hd = HEAD_DIM

    @pl.when(i == 0)
    def _means():
        for n in range(nblk):
            kmean_ref[n:n + 1, :] = jnp.mean(k_ref[n * blk:(n + 1) * blk, :], axis=0, keepdims=True)

    q4 = q_ref[...] * (hd ** -0.5)
    for kvl in range(2):
        lo = kvl * 2 * hd
        q2 = jnp.concatenate([q4[:, lo:lo + hd], q4[:, lo + hd:lo + 2 * hd]], axis=0)
        q2b = q2.astype(BF16)
        kcol = slice(kvl * hd, (kvl + 1) * hd)

        kmean = kmean_ref[:, kcol]
        gate = lax.dot_general(q2, kmean, (((1,), (1,)), ((), ())), preferred_element_type=F32,
                               precision=lax.Precision.HIGHEST)
        col = lax.broadcasted_iota(jnp.int32, gate.shape, 1)
        gate = jnp.where(col < i, gate, NEG_INF)
        sel = jnp.zeros(gate.shape, F32)
        for _ in range(MOBA_TOPK):
            mx = jnp.max(gate, axis=1, keepdims=True)
            idx = jnp.min(jnp.where(gate == mx, col, nblk), axis=1, keepdims=True)
            hit = (col == idx) & (mx > NEG_INF)
            sel = jnp.where(hit, 1.0, sel)
            gate = jnp.where(col == idx, NEG_INF, gate)
        sel_ref[...] = sel

        m_ref[...] = jnp.full(m_ref.shape, NEG_INF, F32)
        l_ref[...] = jnp.zeros(l_ref.shape, F32)
        acc_ref[...] = jnp.zeros(acc_ref.shape, F32)

        def update(s, vb):
            m_old = m_ref[...]
            m_new = jnp.maximum(m_old, jnp.max(s, axis=1, keepdims=True))
            alpha = jnp.exp(m_old - m_new)
            p = jnp.exp(s - m_new)
            l_ref[...] = alpha * l_ref[...] + jnp.sum(p, axis=1, keepdims=True)
            acc_ref[...] = alpha * acc_ref[...] + jnp.dot(p.astype(BF16), vb, preferred_element_type=F32)
            m_ref[...] = m_new

        def tile(j):
            r0 = pl.multiple_of(j * blk, blk)
            kb = k_ref[pl.ds(r0, blk), :][:, kcol].astype(BF16)
            vb = v_ref[pl.ds(r0, blk), :][:, kcol].astype(BF16)
            return _dot_nt(q2b, kb), vb

        s, vb = tile(i)
        update(s + bd_ref[kvl], vb)

        @pl.when(i >= 1)
        def _prev():
            s, vb = tile(i - 1)
            live = jnp.sum(jnp.where(col == i - 1, sel_ref[...], 0.0), axis=1, keepdims=True) > 0.0
            update(jnp.where(live, s + bp_ref[kvl], NEG_INF), vb)

        def far(j, carry):
            s, vb = tile(j)
            live = jnp.sum(jnp.where(col == j, sel_ref[...], 0.0), axis=1, keepdims=True) > 0.0
            rb = jnp.where(live, bf_ref[kvl], NEG_INF)
            update(s + rb, vb)
            return carry

        lax.fori_loop(0, jnp.maximum(i - 1, 0), far, 0)

        o = acc_ref[...] / l_ref[...]
        o_ref[:, lo:lo + hd] = o[:blk]
        o_ref[:, lo + hd:lo + 2 * hd] = o[blk:]


def _old_moba_prompt(proj, bias_rel, n_batch, seq, q_col, k_col, v_col):
    blk = MOBA_BLOCK
    nblk = seq // blk
    iq = np.arange(blk)[:, None]
    ik = np.arange(blk)[None, :]
    idx_diag = _t5_bucket_np(iq - ik)
    idx_prev = _t5_bucket_np(blk + iq - ik)
    bt = bias_rel.T
    diag = jnp.where(jnp.asarray(iq >= ik)[None], bt[:, idx_diag], NEG_INF)
    prev = bt[:, idx_prev]
    far = jnp.broadcast_to(bt[:, T5_BUCKETS - 1][:, None, None], (N_ATT_HEADS, blk, 1))
    diag = diag.reshape(KV_HEADS_A, 2 * blk, blk)
    prev = prev.reshape(KV_HEADS_A, 2 * blk, blk)
    far = far.reshape(KV_HEADS_A, 2 * blk, 1)
    rows_per_b = seq // blk
    return pl.pallas_call(
        _moba_prompt_kernel,
        grid=(n_batch, 2, nblk),
        in_specs=[pl.BlockSpec((blk, 2 * LANES), lambda b, p, i: (b * rows_per_b + i, q_col // 2 + p)),
                  pl.BlockSpec((seq, LANES), lambda b, p, i: (b, k_col + p)),
                  pl.BlockSpec((seq, LANES), lambda b, p, i: (b, v_col + p)),
                  pl.BlockSpec((2, 2 * blk, blk), lambda b, p, i: (p, 0, 0)),
                  pl.BlockSpec((2, 2 * blk, blk), lambda b, p, i: (p, 0, 0)),
                  pl.BlockSpec((2, 2 * blk, 1), lambda b, p, i: (p, 0, 0))],
        out_specs=pl.BlockSpec((blk, 2 * LANES), lambda b, p, i: (b * rows_per_b + i, p)),
        out_shape=jax.ShapeDtypeStruct((n_batch * seq, 4 * LANES), F32),
        scratch_shapes=[pltpu.VMEM((nblk, LANES), F32),
                        pltpu.VMEM((2 * blk, nblk), F32),
                        pltpu.VMEM((2 * blk, 1), F32),
                        pltpu.VMEM((2 * blk, 1), F32),
                        pltpu.VMEM((2 * blk, HEAD_DIM), F32)],
        compiler_params=_cparams(("parallel", "parallel", "arbitrary")),
        name="moba_prompt",
    )(proj, proj, proj, diag, prev, far)


def _t5_thresholds():
    b = _t5_bucket_np(np.arange(4 * T5_MAX_DIST))
    return [int(np.argmax(b >= k)) for k in range(1, T5_BUCKETS)]


def _bias_tile_kernel(rb_ref, o_ref, *, off, sign, lo, hi):
    h = pl.program_id(0)
    row = lax.broadcasted_iota(jnp.int32, o_ref.shape, 0)
    col = lax.broadcasted_iota(jnp.int32, o_ref.shape, 1)
    rel = off + sign * (row - col)
    t = jnp.full(o_ref.shape, rb_ref[0, h], F32)
    for k, thr in enumerate(_t5_thresholds(), start=1):
        t = jnp.where(rel >= thr, rb_ref[k, h], t)
    if lo is not None:
        t = jnp.where((rel >= lo) & (rel <= hi), t, NEG_INF)
    o_ref[...] = t


def _bias_tiles(rel_bias, rows, cols, off, sign=1, lo=None, hi=None):
    return pl.pallas_call(
        functools.partial(_bias_tile_kernel, off=off, sign=sign, lo=lo, hi=hi),
        grid=(N_ATT_HEADS,),
        in_specs=[pl.BlockSpec(memory_space=pltpu.SMEM)],
        out_specs=pl.BlockSpec((None, rows, cols), lambda h: (h, 0, 0)),
        out_shape=jax.ShapeDtypeStruct((N_ATT_HEADS, rows, cols), F32),
        compiler_params=_cparams(("parallel",)),
        name="bias_tiles",
    )(rel_bias)


def _moba_prompt_kernel(q_ref, k_ref, v_ref, bd_ref, bp_ref, bf_ref, o_ref, kmean_ref, vt_ref, sel_ref):
    i = pl.program_id(2)
    blk = MOBA_BLOCK
    nblk = k_ref.shape[0] // blk
    hd = HEAD_DIM

    @pl.when(i == 0)
    def _per_sequence():
        for n in range(nblk):
            kmean_ref[n:n + 1, :] = jnp.mean(k_ref[n * blk:(n + 1) * blk, :], axis=0, keepdims=True)
            vt_ref[n] = v_ref[n * blk:(n + 1) * blk, :].T.astype(BF16)

    q4 = q_ref[...] * (hd ** -0.5)
    q2b = []
    for kvl in range(2):
        lo = kvl * 2 * hd
        q2 = jnp.concatenate([q4[:, lo:lo + hd], q4[:, lo + hd:lo + 2 * hd]], axis=0)
        q2b.append(q2.astype(BF16))
        gate = lax.dot_general(kmean_ref[:, kvl * hd:(kvl + 1) * hd], q2, (((1,), (1,)), ((), ())),
                               preferred_element_type=F32, precision=lax.Precision.HIGHEST)
        row = lax.broadcasted_iota(jnp.int32, gate.shape, 0)
        gate = jnp.where(row < i, gate, NEG_INF)
        sel = jnp.zeros(gate.shape, F32)
        for _ in range(MOBA_TOPK):
            mx = jnp.max(gate, axis=0, keepdims=True)
            idx = jnp.min(jnp.where(gate == mx, row, nblk), axis=0, keepdims=True)
            hit = (row == idx) & (mx > NEG_INF)
            sel = jnp.where(hit, 1.0, sel)
            gate = jnp.where(row == idx, NEG_INF, gate)
        sel_ref[kvl] = sel

    def scores(j, kvl):
        r0 = pl.multiple_of(j * blk, blk)
        kb = k_ref[pl.ds(r0, blk), :][:, kvl * hd:(kvl + 1) * hd].astype(BF16)
        return _dot_nt(kb, q2b[kvl])

    def update(state, s, j, kvl):
        m_old, l_old, acc = state
        m_new = jnp.maximum(m_old, jnp.max(s, axis=0, keepdims=True))
        alpha = jnp.exp(m_old - m_new)
        p = jnp.exp(s - m_new)
        l_new = alpha * l_old + jnp.sum(p, axis=0, keepdims=True)
        vt = vt_ref[j][kvl * hd:(kvl + 1) * hd, :]
        acc = alpha * acc + jnp.dot(vt, p.astype(BF16), preferred_element_type=F32)
        return m_new, l_new, acc

    def live_row(j, kvl):
        return sel_ref[kvl, pl.ds(j, 1), :] > 0.0

    init = (jnp.full((1, 2 * blk), NEG_INF, F32), jnp.zeros((1, 2 * blk), F32), jnp.zeros((hd, 2 * blk), F32))
    jp = jnp.maximum(i - 1, 0)
    states = []
    for kvl in range(2):
        st = update(init, scores(i, kvl) + bd_ref[kvl], i, kvl)
        s = jnp.where(live_row(jp, kvl), scores(jp, kvl) + bp_ref[kvl], NEG_INF)
        states.append(update(st, s, jp, kvl))

    def far(j, carry):
        out = []
        for kvl in range(2):
            s = scores(j, kvl) + jnp.where(live_row(j, kvl), bf_ref[kvl], NEG_INF)
            out.append(update(carry[kvl], s, j, kvl))
        return tuple(out)

    states = lax.fori_loop(0, jp, far, tuple(states))
    ot = jnp.concatenate([acc / l for (_, l, acc) in states], axis=0)
    o = ot.T
    for kvl in range(2):
        for hh in range(2):
            o_ref[:, (2 * kvl + hh) * hd:(2 * kvl + hh + 1) * hd] = o[hh * blk:(hh + 1) * blk, kvl * hd:(kvl + 1) * hd]


def _moba_prompt(proj, bias_rel, n_batch, seq, q_col, k_col, v_col):
    blk = MOBA_BLOCK
    nblk = seq // blk

    def pair_up(t):
        return jnp.transpose(t.reshape(KV_HEADS_A, 2, blk, blk), (0, 2, 1, 3)).reshape(KV_HEADS_A, blk, 2 * blk)

    diag = pair_up(_bias_tiles(bias_rel, blk, blk, 0, sign=-1, lo=0, hi=2 * blk))
    prev = pair_up(_bias_tiles(bias_rel, blk, blk, blk, sign=-1))
    far = jnp.broadcast_to(bias_rel[T5_BUCKETS - 1].reshape(KV_HEADS_A, 1, 2, 1), (KV_HEADS_A, 1, 2, blk))
    far = far.reshape(KV_HEADS_A, 1, 2 * blk)
    rows_per_b = seq // blk
    return pl.pallas_call(
        _moba_prompt_kernel,
        grid=(n_batch, 2, nblk),
        in_specs=[pl.BlockSpec((blk, 2 * LANES), lambda b, p, i: (b * rows_per_b + i, q_col // 2 + p)),
                  pl.BlockSpec((seq, LANES), lambda b, p, i: (b, k_col + p)),
                  pl.BlockSpec((seq, LANES), lambda b, p, i: (b, v_col + p)),
                  pl.BlockSpec((2, blk, 2 * blk), lambda b, p, i: (p, 0, 0)),
                  pl.BlockSpec((2, blk, 2 * blk), lambda b, p, i: (p, 0, 0)),
                  pl.BlockSpec((2, 1, 2 * blk), lambda b, p, i: (p, 0, 0))],
        out_specs=pl.BlockSpec((blk, 2 * LANES), lambda b, p, i: (b * rows_per_b + i, p)),
        out_shape=jax.ShapeDtypeStruct((n_batch * seq, 4 * LANES), F32),
        scratch_shapes=[pltpu.VMEM((nblk, LANES), F32),
                        pltpu.VMEM((nblk, LANES, blk), BF16),
                        pltpu.VMEM((2, nblk, 2 * blk), F32)],
        compiler_params=_cparams(("parallel", "parallel", "arbitrary")),
        name="moba_prompt",
    )(proj, proj, proj, diag, prev, far)


MOBA_DEC_BLOCKS_PER_STEP = 4
PAGES_PER_BLOCK = MOBA_BLOCK // PAGE_SIZE


def _moba_sample_kernel(pt_ref, q_ref, kn_ref, vn_ref, b31_ref, bfar_ref, bown_ref, *rest,
                        n_blocks):
    nb = MOBA_DEC_BLOCKS_PER_STEP
    npg = nb * PAGES_PER_BLOCK
    k_pages = rest[:npg]
    v_pages = rest[npg:2 * npg]
    o_ref = rest[2 * npg]
    g_ref, m_ref, l_ref, part_ref = rest[2 * npg + 1:]
    s = pl.program_id(1)
    q = q_ref[...]
    qb = q.astype(BF16)
    lane = lax.broadcasted_iota(jnp.int32, g_ref.shape, 1)

    @pl.when(s == 0)
    def _init():
        g_ref[...] = jnp.zeros_like(g_ref)
        m_ref[...] = jnp.zeros_like(m_ref)
        l_ref[...] = jnp.zeros_like(l_ref)

    for j in range(nb):
        n = s * nb + j
        kts = [k_pages[j * PAGES_PER_BLOCK + t][...] for t in range(PAGES_PER_BLOCK)]
        vts = [v_pages[j * PAGES_PER_BLOCK + t][...] for t in range(PAGES_PER_BLOCK)]
        sc = jnp.concatenate([_dot(qb, kt) for kt in kts], axis=1)
        gate = jnp.sum(sc, axis=1, keepdims=True) * (1.0 / MOBA_BLOCK)
        sc = sc + jnp.where(n == n_blocks - 1, b31_ref[...], bfar_ref[...])
        mx = jnp.max(sc, axis=1, keepdims=True)
        p = jnp.exp(sc - mx)
        hit = lane == n
        g_ref[...] = jnp.where(hit, gate, g_ref[...])
        m_ref[...] = jnp.where(hit, mx, m_ref[...])
        l_ref[...] = jnp.where(hit, jnp.sum(p, axis=1, keepdims=True), l_ref[...])
        o = None
        for t, vt in enumerate(vts):
            part = _dot_nt(p[:, t * PAGE_SIZE:(t + 1) * PAGE_SIZE], vt)
            o = part if o is None else o + part
        part_ref[n] = o

    @pl.when(s == pl.num_programs(1) - 1)
    def _merge():
        gate = jnp.where(lane < n_blocks, g_ref[...], NEG_INF)
        sel = jnp.zeros(gate.shape, jnp.bool_)
        for _ in range(MOBA_TOPK):
            mxg = jnp.max(gate, axis=1, keepdims=True)
            idx = jnp.min(jnp.where(gate == mxg, lane, LANES), axis=1, keepdims=True)
            hit = (lane == idx) & (mxg > NEG_INF)
            sel = sel | hit
            gate = jnp.where(lane == idx, NEG_INF, gate)
        s_own = _dot_nt(qb, kn_ref[...]) + bown_ref[...]
        m_sel = jnp.where(sel, m_ref[...], NEG_INF)
        m_all = jnp.maximum(jnp.max(m_sel, axis=1, keepdims=True), jnp.max(s_own, axis=1, keepdims=True))
        w = jnp.where(sel, jnp.exp(m_sel - m_all), 0.0)
        p_own = jnp.exp(s_own - m_all)
        den = jnp.sum(w * l_ref[...], axis=1, keepdims=True) + jnp.sum(p_own, axis=1, keepdims=True)
        acc = _dot(p_own, vn_ref[...])
        for n in range(n_blocks):
            wn = jnp.sum(jnp.where(lane == n, w, 0.0), axis=1, keepdims=True)
            acc = acc + wn * part_ref[n]
        o_ref[...] = acc / den


def _moba_sample(q_s, k_new, v_new, cache_k, cache_v, page_table, bias_rel, layer=0):
    n_dec, t_dec = q_s.shape[:2]
    n_pages = page_table.shape[1]
    n_blocks = n_pages // PAGES_PER_BLOCK
    nb = MOBA_DEC_BLOCKS_PER_STEP
    npg = nb * PAGES_PER_BLOCK
    assert n_blocks % nb == 0 and n_blocks <= LANES and t_dec <= 8
    rows = N_ATT_HEADS * t_dec
    kv_w = KV_HEADS_A * HEAD_DIM
    head_kv = np.arange(N_ATT_HEADS) // (N_ATT_HEADS // KV_HEADS_A)
    qh = jnp.transpose(q_s, (0, 2, 1, 3)) * (HEAD_DIM ** -0.5)
    onehot = jnp.asarray(np.eye(KV_HEADS_A, dtype=np.float32)[head_kv])
    qbd = (qh[:, :, :, None, :] * onehot[None, :, None, :, None]).reshape(n_dec, rows, kv_w)
    pad = 8 - t_dec
    kn = jnp.pad(k_new, ((0, 0), (0, pad), (0, 0)))
    vn = jnp.pad(v_new, ((0, 0), (0, pad), (0, 0)))
    bt = bias_rel.T
    past = n_pages * PAGE_SIZE
    tq = np.arange(t_dec)[None, :, None]
    ik = np.arange(MOBA_BLOCK)[None, None, :]
    idx31 = _t5_bucket_np(past + tq - ((n_blocks - 1) * MOBA_BLOCK + ik))[0]
    b31 = bt[:, idx31].reshape(rows, MOBA_BLOCK)
    bfar = jnp.broadcast_to(bt[:, T5_BUCKETS - 1][:, None, None], (N_ATT_HEADS, t_dec, 1)).reshape(rows, 1)
    tk = np.arange(8)[None, :]
    rel_own = np.arange(t_dec)[:, None] - tk
    ok_own = (rel_own >= 0) & (tk < t_dec)
    bown = jnp.where(jnp.asarray(ok_own)[None], bt[:, _t5_bucket_np(rel_own)], NEG_INF).reshape(rows, 8)

    def page_spec(k):
        return pl.BlockSpec((None, kv_w, PAGE_SIZE), lambda b, s, pt, k=k: (pt[b, s * npg + k], layer, 0))

    const2 = lambda b, s, pt: (0, 0)
    grid_spec = pltpu.PrefetchScalarGridSpec(
        num_scalar_prefetch=1,
        grid=(n_dec, n_blocks // nb),
        in_specs=[pl.BlockSpec((None, rows, kv_w), lambda b, s, pt: (b, 0, 0)),
                  pl.BlockSpec((None, 8, kv_w), lambda b, s, pt: (b, 0, 0)),
                  pl.BlockSpec((None, 8, kv_w), lambda b, s, pt: (b, 0, 0)),
                  pl.BlockSpec((rows, MOBA_BLOCK), const2),
                  pl.BlockSpec((rows, 1), const2),
                  pl.BlockSpec((rows, 8), const2)]
                 + [page_spec(k) for k in range(npg)] + [page_spec(k) for k in range(npg)],
        out_specs=pl.BlockSpec((None, rows, kv_w), lambda b, s, pt: (b, 0, 0)),
        scratch_shapes=[pltpu.VMEM((rows, LANES), F32),
                        pltpu.VMEM((rows, LANES), F32),
                        pltpu.VMEM((rows, LANES), F32),
                        pltpu.VMEM((n_blocks, rows, kv_w), F32)],
    )
    out = pl.pallas_call(
        functools.partial(_moba_sample_kernel, n_blocks=n_blocks),
        grid_spec=grid_spec,
        out_shape=jax.ShapeDtypeStruct((n_dec, rows, kv_w), F32),
        compiler_params=_cparams(("parallel", "arbitrary")),
        name="moba_sample",
    )(page_table, qbd, kn, vn, b31, bfar, bown, *([cache_k] * npg), *([cache_v] * npg))
    out = out.reshape(n_dec, N_ATT_HEADS, t_dec, KV_HEADS_A, HEAD_DIM)
    out = out[:, np.arange(N_ATT_HEADS), :, head_kv, :]
    return jnp.transpose(out, (1, 2, 0, 3)).reshape(n_dec, t_dec, N_ATT_HEADS * HEAD_DIM)


SWA_GROUP = N_ATT_HEADS // KV_HEADS_C


def _sink_softmax_pv(parts, sink):
    m = sink
    for s, _ in parts:
        m = jnp.maximum(m, jnp.max(s, axis=1, keepdims=True))
    den = jnp.exp(sink - m)
    ps = []
    for s, _ in parts:
        p = jnp.exp(s - m)
        den = den + jnp.sum(p, axis=1, keepdims=True)
        ps.append(p)
    o = None
    for p, (_, v) in zip(ps, parts):
        t = _dot(p / den, v)
        o = t if o is None else o + t
    return o


def _swa_prompt_kernel(q_ref, kp_ref, ko_ref, vp_ref, vo_ref, bias_ref, sink_ref, o_ref):
    i = pl.program_id(1)
    w, hd, g = WINDOW, HEAD_DIM, SWA_GROUP
    q = q_ref[...] * (hd ** -0.5)
    kp, ko, vp, vo = kp_ref[...], ko_ref[...], vp_ref[...], vo_ref[...]
    for c in range(KV_HEADS_C):
        qs = jnp.concatenate([q[:, (c * g + j) * hd:(c * g + j + 1) * hd] for j in range(g)], axis=0)
        kcol = slice(c * hd, (c + 1) * hd)
        kcat = jnp.concatenate([kp[:, kcol], ko[:, kcol]], axis=0)
        vcat = jnp.concatenate([vp[:, kcol], vo[:, kcol]], axis=0)
        s = _dot_nt(qs, kcat) + bias_ref[c]
        col = lax.broadcasted_iota(jnp.int32, s.shape, 1)
        s = jnp.where((col >= w) | (i > 0), s, NEG_INF)
        o = _sink_softmax_pv([(s, vcat)], sink_ref[c])
        for j in range(g):
            o_ref[:, (c * g + j) * hd:(c * g + j + 1) * hd] = o[j * w:(j + 1) * w]


def _swa_prompt(proj, bias_rel, sinks, n_batch, seq, q_col, k_col, v_col):
    w, g = WINDOW, SWA_GROUP
    nb = seq // w
    band = _bias_tiles(bias_rel, w, 2 * w, w, sign=1, lo=0, hi=w).reshape(KV_HEADS_C, g * w, 2 * w)
    sink = jnp.broadcast_to(sinks[:, None, None], (N_ATT_HEADS, w, 1)).reshape(KV_HEADS_C, g * w, 1)
    own = lambda c: (lambda b, i: (b * nb + i, c))
    prev = lambda c: (lambda b, i: (b * nb + jnp.maximum(i - 1, 0), c))
    return pl.pallas_call(
        _swa_prompt_kernel,
        grid=(n_batch, nb),
        in_specs=[pl.BlockSpec((w, 4 * LANES), lambda b, i: (b * nb + i, q_col // 4)),
                  pl.BlockSpec((w, LANES), prev(k_col)),
                  pl.BlockSpec((w, LANES), own(k_col)),
                  pl.BlockSpec((w, LANES), prev(v_col)),
                  pl.BlockSpec((w, LANES), own(v_col)),
                  pl.BlockSpec((KV_HEADS_C, g * w, 2 * w), lambda b, i: (0, 0, 0)),
                  pl.BlockSpec((KV_HEADS_C, g * w, 1), lambda b, i: (0, 0, 0))],
        out_specs=pl.BlockSpec((w, 4 * LANES), lambda b, i: (b * nb + i, 0)),
        out_shape=jax.ShapeDtypeStruct((n_batch * seq, 4 * LANES), F32),
        compiler_params=_cparams(("parallel", "parallel")),
        name="swa_prompt",
    )(proj, proj, proj, proj, proj, band, sink)


SWA_DEC_SEQS_PER_STEP = 8


def _swa_sample_kernel(q_ref, kb_ref, vb_ref, kn_ref, vn_ref, bbuf_ref, bnew_ref, sink_ref, o_ref):
    for i in range(SWA_DEC_SEQS_PER_STEP):
        q = q_ref[i]
        s_buf = _dot_nt(q, kb_ref[i]) + bbuf_ref[...]
        s_new = _dot_nt(q, kn_ref[i]) + bnew_ref[...]
        o_ref[i] = _sink_softmax_pv([(s_buf, vb_ref[i]), (s_new, vn_ref[i])], sink_ref[...])


def _swa_sample(q_s, k_new, v_new, buf_k, buf_v, bias_rel, sinks):
    n_dec, t_dec = q_s.shape[:2]
    nbuf = buf_k.shape[1]
    rows = N_ATT_HEADS * t_dec
    kv_w = KV_HEADS_C * HEAD_DIM
    sb = SWA_DEC_SEQS_PER_STEP
    assert n_dec % sb == 0 and t_dec <= 8
    head_kv = np.arange(N_ATT_HEADS) // SWA_GROUP
    qh = jnp.transpose(q_s, (0, 2, 1, 3)) * (HEAD_DIM ** -0.5)
    onehot = jnp.asarray(np.eye(KV_HEADS_C, dtype=np.float32)[head_kv])
    qbd = (qh[:, :, :, None, :] * onehot[None, :, None, :, None]).reshape(n_dec, rows, kv_w)
    pad = 8 - t_dec
    kn = jnp.pad(k_new, ((0, 0), (0, pad), (0, 0)))
    vn = jnp.pad(v_new, ((0, 0), (0, pad), (0, 0)))
    bt = bias_rel.T
    t = np.arange(t_dec)[:, None]
    rel_buf = nbuf + t - np.arange(nbuf)[None, :]
    ok_buf = (rel_buf >= 0) & (rel_buf <= WINDOW)
    bbuf = jnp.where(jnp.asarray(ok_buf)[None], bt[:, _t5_bucket_np(rel_buf)], NEG_INF).reshape(rows, nbuf)
    tk = np.arange(8)[None, :]
    rel_new = t - tk
    ok_new = (rel_new >= 0) & (tk < t_dec)
    bnew = jnp.where(jnp.asarray(ok_new)[None], bt[:, _t5_bucket_np(rel_new)], NEG_INF).reshape(rows, 8)
    sink = jnp.broadcast_to(sinks[:, None, None], (N_ATT_HEADS, t_dec, 1)).reshape(rows, 1)
    seq3 = lambda r, c: pl.BlockSpec((sb, r, c), lambda i: (i, 0, 0))
    const2 = lambda r, c: pl.BlockSpec((r, c), lambda i: (0, 0))
    out = pl.pallas_call(
        _swa_sample_kernel,
        grid=(n_dec // sb,),
        in_specs=[seq3(rows, kv_w), seq3(nbuf, kv_w), seq3(nbuf, kv_w), seq3(8, kv_w), seq3(8, kv_w),
                  const2(rows, nbuf), const2(rows, 8), const2(rows, 1)],
        out_specs=seq3(rows, kv_w),
        out_shape=jax.ShapeDtypeStruct((n_dec, rows, kv_w), F32),
        compiler_params=_cparams(("parallel",)),
        name="swa_sample",
    )(qbd, buf_k, buf_v, kn, vn, bbuf, bnew, sink)
    out = out.reshape(n_dec, N_ATT_HEADS, t_dec, KV_HEADS_C, HEAD_DIM)
    out = out[:, np.arange(N_ATT_HEADS), :, head_kv, :]
    return jnp.transpose(out, (1, 2, 0, 3)).reshape(n_dec, t_dec, N_ATT_HEADS * HEAD_DIM)


def _softplus(x):
    return jnp.maximum(x, 0.0) + jnp.log1p(jnp.exp(-jnp.abs(x)))


def _neg_expm1(y):
    return -jnp.tanh(0.5 * y) * (jnp.exp(y) + 1.0)


def _shifted_rows(prev8, cur, k):
    if k == 0:
        return cur
    ext = jnp.concatenate([prev8, cur], axis=0)
    return pltpu.roll(ext, k, axis=0)[8:]


def _causal_conv_rows(prev8, cur, w):
    y = cur * w[CONV_WIDTH - 1:CONV_WIDTH, :]
    for k in range(1, CONV_WIDTH):
        y = y + _shifted_rows(prev8, cur, k) * w[CONV_WIDTH - 1 - k:CONV_WIDTH - k, :]
    return y


def _lru_gates(xc, wa, ba, wx, bx, sp):
    r = jax.nn.sigmoid(_dot(xc, wa) + ba)
    i = jax.nn.sigmoid(_dot(xc, wx) + bx)
    log_a = -LRU_C * r * sp
    a = jnp.exp(log_a)
    b = jnp.sqrt(_neg_expm1(2.0 * log_a)) * (i * xc)
    return a, b


LRU_ROWS = 256


def _lru_prompt_kernel(x_ref, gr_ref, cw_ref, cb_ref, wa_ref, ba_ref, wx_ref, bx_ref, lam_ref,
                       o_ref, hl_ref):
    n = x_ref.shape[0] // LRU_ROWS
    sp = _softplus(-lam_ref[...])
    row = lax.broadcasted_iota(jnp.int32, (LRU_ROWS, LANES), 0)

    def body(c, carry):
        prev8, h = carry
        r0 = pl.multiple_of(c * LRU_ROWS, LRU_ROWS)
        cur = x_ref[pl.ds(r0, LRU_ROWS), :]
        xc = _causal_conv_rows(prev8, cur, cw_ref[...]) + cb_ref[...]
        a, b = _lru_gates(xc, wa_ref[...], ba_ref[...], wx_ref[...], bx_ref[...], sp)
        d = 1
        while d < LRU_ROWS:
            a_s = jnp.where(row >= d, pltpu.roll(a, d, axis=0), 1.0)
            b_s = jnp.where(row >= d, pltpu.roll(b, d, axis=0), 0.0)
            b = a * b_s + b
            a = a * a_s
            d *= 2
        hs = a * h + b
        o_ref[pl.ds(r0, LRU_ROWS), :] = hs * jax.nn.gelu(gr_ref[pl.ds(r0, LRU_ROWS), :])
        return cur[LRU_ROWS - 8:], hs[LRU_ROWS - 1:]

    _, h = lax.fori_loop(0, n, body, (jnp.zeros((8, LANES), F32), jnp.zeros((1, LANES), F32)))
    hl_ref[...] = h


def _lru_prompt(proj, n_batch, seq, x_col, g_col, conv_w, conv_b, wa_bd, b_a, wx_bd, b_x, lam):
    ncb = D_RNN // LANES
    vec = lambda v: v.reshape(1, D_RNN)
    vspec = pl.BlockSpec((1, LANES), lambda b, c: (0, c))
    out, hl = pl.pallas_call(
        _lru_prompt_kernel,
        grid=(n_batch, ncb),
        in_specs=[pl.BlockSpec((seq, LANES), lambda b, c: (b, x_col + c)),
                  pl.BlockSpec((seq, LANES), lambda b, c: (b, g_col + c)),
                  pl.BlockSpec((CONV_WIDTH, LANES), lambda b, c: (0, c)),
                  vspec,
                  pl.BlockSpec((LANES, LANES), lambda b, c: (c, c)), vspec,
                  pl.BlockSpec((LANES, LANES), lambda b, c: (c, c)), vspec,
                  vspec],
        out_specs=[pl.BlockSpec((seq, LANES), lambda b, c: (b, c)),
                   pl.BlockSpec((None, 1, LANES), lambda b, c: (b, 0, c))],
        out_shape=[jax.ShapeDtypeStruct((n_batch * seq, D_RNN), F32),
                   jax.ShapeDtypeStruct((n_batch, 1, D_RNN), F32)],
        compiler_params=_cparams(("parallel", "parallel")),
        name="lru_prompt",
    )(proj, proj, conv_w, vec(conv_b), wa_bd, vec(b_a), wx_bd, vec(b_x), vec(lam))
    return out, hl.reshape(n_batch, D_RNN)


def _lru_sample_kernel(xp_ref, gr_ref, h0_ref, cw_ref, cb_ref, wa_ref, ba_ref, wx_ref, bx_ref, lam_ref,
                       o_ref, hl_ref):
    t_dec = gr_ref.shape[0]
    sp = _softplus(-lam_ref[...])
    h = h0_ref[...]
    for t in range(t_dec):
        xc = cb_ref[...]
        for j in range(CONV_WIDTH):
            xc = xc + xp_ref[t + j] * cw_ref[j:j + 1, :]
        a, b = _lru_gates(xc, wa_ref[...], ba_ref[...], wx_ref[...], bx_ref[...], sp)
        h = a * h + b
        o_ref[t] = h * jax.nn.gelu(gr_ref[t])
    hl_ref[...] = h


def _lru_sample(xr, gr, h0, conv0, conv_w, conv_b, wa_bd, b_a, wx_bd, b_x, lam):
    n_dec, t_dec, _ = xr.shape
    xp = jnp.transpose(jnp.concatenate([conv0, xr], axis=1), (1, 0, 2))
    grt = jnp.transpose(gr, (1, 0, 2))
    vec = lambda v: v.reshape(1, D_RNN)
    full = lambda *s: pl.BlockSpec(s, lambda i: (0,) * len(s))
    out, hl = pl.pallas_call(
        _lru_sample_kernel,
        grid=(1,),
        in_specs=[full(t_dec + CONV_WIDTH - 1, n_dec, D_RNN), full(t_dec, n_dec, D_RNN), full(n_dec, D_RNN),
                  full(CONV_WIDTH, D_RNN), full(1, D_RNN), full(D_RNN, D_RNN), full(1, D_RNN),
                  full(D_RNN, D_RNN), full(1, D_RNN), full(1, D_RNN)],
        out_specs=[full(t_dec, n_dec, D_RNN), full(n_dec, D_RNN)],
        out_shape=[jax.ShapeDtypeStruct((t_dec, n_dec, D_RNN), F32),
                   jax.ShapeDtypeStruct((n_dec, D_RNN), F32)],
        compiler_params=_cparams(("arbitrary",)),
        name="lru_sample",
    )(xp, grt, h0, conv_w, vec(conv_b), wa_bd, vec(b_a), wx_bd, vec(b_x), vec(lam))
    return jnp.transpose(out, (1, 0, 2)), hl


def _block_diag(w):
    n, k, _ = w.shape
    eye = jnp.asarray(np.eye(n, dtype=np.float32))
    return (w[:, :, None, :] * eye[:, None, :, None]).reshape(n * k, n * k)


GDN_ROWS = 256


def _bmm(a, b):
    return jnp.einsum('nij,njk->nik', a.astype(BF16), b.astype(BF16), preferred_element_type=F32)


def _bmm_nt(a, b):
    return jnp.einsum('nid,njd->nij', a.astype(BF16), b.astype(BF16), preferred_element_type=F32)


def _col(x, lane, idx):
    return jnp.sum(jnp.where(lane == idx, x, 0.0), axis=1, keepdims=True)


def _gdn_gates_kernel(x_ref, a_ref, dt_ref, beta_ref, gc_ref, *, chunk, pad_period, pad_rows):
    x = x_ref[...]
    tm = x.shape[0]
    beta = jax.nn.sigmoid(x)
    g = -jnp.exp(a_ref[...]) * _softplus(x + dt_ref[...])
    if pad_period:
        row = lax.broadcasted_iota(jnp.int32, x.shape, 0)
        valid = (row & (pad_period - 1)) >= pad_rows
        beta = jnp.where(valid, beta, 0.0)
        g = jnp.where(valid, g, 0.0)
    r = lax.broadcasted_iota(jnp.int32, (tm, tm), 0)
    c = lax.broadcasted_iota(jnp.int32, (tm, tm), 1)
    sh = chunk.bit_length() - 1
    tri = jnp.where(((r >> sh) == (c >> sh)) & (c <= r), 1.0, 0.0)
    beta_ref[...] = beta
    gc_ref[...] = jnp.dot(tri, g, preferred_element_type=F32, precision=lax.Precision.HIGHEST)


def _gdn_gates(x, col, rows, a_log, dt_bias, chunk, pad_period=0, pad_rows=0):
    tm = GDN_ROWS
    assert rows % tm == 0 and tm % chunk == 0
    a_pad = jnp.pad(a_log.reshape(1, -1), ((0, 0), (GDN_HEADS, LANES - 2 * GDN_HEADS)))
    dt_pad = jnp.pad(dt_bias.reshape(1, -1), ((0, 0), (GDN_HEADS, LANES - 2 * GDN_HEADS)))
    return pl.pallas_call(
        functools.partial(_gdn_gates_kernel, chunk=chunk, pad_period=pad_period, pad_rows=pad_rows),
        grid=(rows // tm,),
        in_specs=[pl.BlockSpec((tm, LANES), lambda i: (i, col)),
                  pl.BlockSpec((1, LANES), lambda i: (0, 0)),
                  pl.BlockSpec((1, LANES), lambda i: (0, 0))],
        out_specs=[pl.BlockSpec((tm, LANES), lambda i: (i, 0))] * 2,
        out_shape=[jax.ShapeDtypeStruct((rows, LANES), F32)] * 2,
        compiler_params=_cparams(("parallel",)),
        name="gdn_gates",
    )(x, a_pad, dt_pad)


def _gdn_prep_kernel(x_ref, cw_ref, beta_ref, gc_ref, *out_refs, mode):
    hd = HEAD_DIM
    n = x_ref.shape[0] // GDN_ROWS
    h0 = 2 * pl.program_id(1)
    lane = lax.broadcasted_iota(jnp.int32, (GDN_ROWS, LANES), 1)
    lo = lane < hd

    def halves(c0, c1):
        return jnp.where(lo, c0, c1)

    def put(ref, r0, val):
        ref[0, pl.ds(r0, GDN_ROWS), :] = val[:, :hd]
        ref[1, pl.ds(r0, GDN_ROWS), :] = val[:, hd:]

    def body(c, prev8):
        r0 = pl.multiple_of(c * GDN_ROWS, GDN_ROWS)
        cur = x_ref[pl.ds(r0, GDN_ROWS), :]
        y = _causal_conv_rows(prev8, cur, cw_ref[...])
        y = y * jax.nn.sigmoid(y)
        if mode in ("q", "k"):
            ss = y * y
            s0 = jnp.sum(jnp.where(lo, ss, 0.0), axis=1, keepdims=True)
            s1 = jnp.sum(jnp.where(lo, 0.0, ss), axis=1, keepdims=True)
            y = y * halves(lax.rsqrt(s0 + NORM_EPS), lax.rsqrt(s1 + NORM_EPS))
        if mode == "q":
            put(out_refs[0], r0, y * (hd ** -0.5))
        else:
            beta = beta_ref[pl.ds(r0, GDN_ROWS), :]
            bb = halves(_col(beta, lane, h0), _col(beta, lane, h0 + 1))
            if mode == "k":
                gc = gc_ref[pl.ds(r0, GDN_ROWS), :]
                put(out_refs[0], r0, y)
                put(out_refs[1], r0, y * bb)
                put(out_refs[2], r0, halves(_col(gc, lane, GDN_HEADS + h0), _col(gc, lane, GDN_HEADS + h0 + 1)))
            else:
                put(out_refs[0], r0, y * bb)
        return cur[GDN_ROWS - 8:]

    lax.fori_loop(0, n, body, jnp.zeros((8, LANES), F32))


def _gdn_prep(x, x_col, conv_w, beta, gc, n_batch, seq, mode):
    sec = {"q": 0, "k": 4, "v": 8}[mode]
    n_out = {"q": 1, "k": 3, "v": 1}[mode]
    hspec = pl.BlockSpec((2, seq, HEAD_DIM), lambda b, c: (b * 4 + c, 0, 0))
    gspec = pl.BlockSpec((seq, LANES), lambda b, c: (b, 0))
    outs = pl.pallas_call(
        functools.partial(_gdn_prep_kernel, mode=mode),
        grid=(n_batch, 4),
        in_specs=[pl.BlockSpec((seq, LANES), lambda b, c: (b, x_col + c)),
                  pl.BlockSpec((CONV_WIDTH, LANES), lambda b, c: (0, sec + c)),
                  gspec, gspec],
        out_specs=[hspec] * n_out,
        out_shape=[jax.ShapeDtypeStruct((n_batch * GDN_HEADS, seq, HEAD_DIM), F32)] * n_out,
        compiler_params=_cparams(("parallel", "parallel")),
        name="gdn_prep_" + mode,
    )(x, conv_w, beta, gc)
    return outs


def _gdn_intra_kernel(q_ref, k_ref, kb_ref, vb_ref, gcb_ref, gcr_ref,
                      u_ref, w_ref, qg_ref, kd_ref, qk_ref, gl_ref, *, chunk):
    hd = HEAD_DIM
    rows = q_ref.shape[0]
    n = rows // chunk
    r3 = lambda ref: ref[...].reshape(n, chunk, hd)
    q, k, kb, vb, gcb = r3(q_ref), r3(k_ref), r3(kb_ref), r3(vb_ref), r3(gcb_ref)
    gcr = gcr_ref[...]
    ci = lax.broadcasted_iota(jnp.int32, (n, chunk, chunk), 1)
    si = lax.broadcasted_iota(jnp.int32, (n, chunk, chunk), 2)
    decay = jnp.exp(jnp.where(ci >= si, gcb[:, :, :chunk] - gcr[:, None, :], NEG_INF))
    a = jnp.where(ci > si, _bmm_nt(kb, k) * decay, 0.0)
    b = -a
    y = b
    p = _bmm(b, b)
    stages = chunk.bit_length() - 2
    for t in range(stages):
        y = y + p + _bmm(y, p)
        if t < stages - 1:
            p = _bmm(p, p)
    eg = jnp.exp(gcb)
    rhs = jnp.concatenate([vb, kb * eg], axis=-1)
    sol = rhs + _bmm(y, rhs)
    gl = gcb[:, chunk - 1:chunk, :]
    u_ref[...] = sol[:, :, :hd].reshape(rows, hd)
    w_ref[...] = sol[:, :, hd:].reshape(rows, hd)
    qg_ref[...] = (q * eg).reshape(rows, hd)
    kd_ref[...] = (k * jnp.exp(gl - gcb)).reshape(rows, hd)
    qk_ref[...] = (_bmm_nt(q, k) * decay).reshape(rows, chunk)
    gl_ref[...] = jnp.exp(gl).reshape(n, hd)


def _gdn_intra(q, k, kb, vb, gcb, gc_row, chunk, rows_per_step):
    bh, seq, hd = q.shape
    tr = rows_per_step
    n = tr // chunk
    assert seq % tr == 0
    hspec = pl.BlockSpec((None, tr, hd), lambda h, i: (h, i, 0))
    shape = jax.ShapeDtypeStruct((bh, seq, hd), F32)
    return pl.pallas_call(
        functools.partial(_gdn_intra_kernel, chunk=chunk),
        grid=(bh, seq // tr),
        in_specs=[hspec] * 5 + [pl.BlockSpec((None, n, chunk), lambda h, i: (h, i, 0))],
        out_specs=[hspec] * 4 + [pl.BlockSpec((None, tr, chunk), lambda h, i: (h, i, 0)),
                                 pl.BlockSpec((None, n, hd), lambda h, i: (h, i, 0))],
        out_shape=[shape] * 4 + [jax.ShapeDtypeStruct((bh, seq, chunk), F32),
                                 jax.ShapeDtypeStruct((bh, seq // chunk, hd), F32)],
        compiler_params=_cparams(("parallel", "parallel")),
        name="gdn_intra",
    )(q, k, kb, vb, gcb, gc_row)


def _gdn_chunk_update(s, u, w, qg, kd, qk, gl):
    v_new = u - _bmm(w, s)
    o = _bmm(qg, s) + _bmm(qk, v_new)
    s = s * gl + jnp.einsum('ncd,nce->nde', kd.astype(BF16), v_new.astype(BF16), preferred_element_type=F32)
    return s, o


def _gdn_scan_kernel(u_ref, w_ref, qg_ref, kd_ref, qk_ref, gl_ref, o_ref, sfin_ref, s_scr, *, chunk):
    j = pl.program_id(1)

    @pl.when(j == 0)
    def _zero():
        s_scr[...] = jnp.zeros_like(s_scr)

    n = u_ref.shape[1] // chunk
    s = s_scr[...]
    for c in range(n):
        sl = slice(c * chunk, (c + 1) * chunk)
        s, o = _gdn_chunk_update(s, u_ref[:, sl, :], w_ref[:, sl, :], qg_ref[:, sl, :], kd_ref[:, sl, :],
                                 qk_ref[:, sl, :], gl_ref[:, c:c + 1, :])
        o_ref[:, sl, :] = o
    s_scr[...] = s

    @pl.when(j == pl.num_programs(1) - 1)
    def _final():
        sfin_ref[...] = s


def _gdn_scan(u, w, qg, kd, qk, gl, chunk, heads_per_step, rows_per_step):
    bh, seq, hd = u.shape
    hb, tr = heads_per_step, rows_per_step
    n = tr // chunk
    hspec = pl.BlockSpec((hb, tr, hd), lambda h, i: (h, i, 0))
    return pl.pallas_call(
        functools.partial(_gdn_scan_kernel, chunk=chunk),
        grid=(bh // hb, seq // tr),
        in_specs=[hspec] * 4 + [pl.BlockSpec((hb, tr, chunk), lambda h, i: (h, i, 0)),
                                pl.BlockSpec((hb, n, hd), lambda h, i: (h, i, 0))],
        out_specs=[hspec, pl.BlockSpec((hb, hd, hd), lambda h, i: (h, 0, 0))],
        out_shape=[jax.ShapeDtypeStruct((bh, seq, hd), F32), jax.ShapeDtypeStruct((bh, hd, hd), F32)],
        scratch_shapes=[pltpu.VMEM((hb, hd, hd), F32)],
        compiler_params=_cparams(("parallel", "arbitrary")),
        name="gdn_scan",
    )(u, w, qg, kd, qk, gl)


GDN_DEC_SEQS_PER_STEP = 16


def _gdn_step_kernel(s0_ref, u_ref, w_ref, qg_ref, kd_ref, qk_ref, gl_ref, o_ref, s_ref, *, chunk):
    hd = HEAD_DIM
    n = s0_ref.shape[0]
    r3 = lambda ref, last: ref[...].reshape(n, chunk, last)
    s, o = _gdn_chunk_update(s0_ref[...], r3(u_ref, hd), r3(w_ref, hd), r3(qg_ref, hd), r3(kd_ref, hd),
                             r3(qk_ref, chunk), gl_ref[...][:, None, :])
    o_ref[...] = o.reshape(n * chunk, hd)
    s_ref[...] = s


def _gdn_step(s0, u, w, qg, kd, qk, gl, chunk):
    n_dec, nh, hd, _ = s0.shape
    sb = GDN_DEC_SEQS_PER_STEP
    assert n_dec % sb == 0
    hspec = pl.BlockSpec((None, sb * chunk, hd), lambda h, i: (h, i, 0))
    sspec = pl.BlockSpec((sb, None, hd, hd), lambda h, i: (i, h, 0, 0))
    return pl.pallas_call(
        functools.partial(_gdn_step_kernel, chunk=chunk),
        grid=(nh, n_dec // sb),
        in_specs=[sspec] + [hspec] * 4 + [pl.BlockSpec((None, sb * chunk, chunk), lambda h, i: (h, i, 0)),
                                         pl.BlockSpec((None, sb, hd), lambda h, i: (h, i, 0))],
        out_specs=[hspec, sspec],
        out_shape=[jax.ShapeDtypeStruct((nh, n_dec * chunk, hd), F32),
                   jax.ShapeDtypeStruct((n_dec, nh, hd, hd), F32)],
        compiler_params=_cparams(("parallel", "parallel")),
        name="gdn_step",
    )(s0, u, w, qg, kd, qk, gl)


def _gdn_post_kernel(o_ref, z_ref, w_ref, out_ref):
    hd = HEAD_DIM
    for h in range(GDN_HEADS):
        o = o_ref[h]
        r = o * lax.rsqrt(jnp.mean(o * o, axis=-1, keepdims=True) + NORM_EPS) * w_ref[...]
        z = z_ref[:, h * hd:(h + 1) * hd]
        out_ref[:, h * hd:(h + 1) * hd] = r * (z * jax.nn.sigmoid(z))


def _gdn_post(o, z, z_col, norm_w, n_batch, seq, tm):
    nt = seq // tm
    return pl.pallas_call(
        _gdn_post_kernel,
        grid=(n_batch, nt),
        in_specs=[pl.BlockSpec((GDN_HEADS, tm, HEAD_DIM), lambda b, i: (b, i, 0)),
                  pl.BlockSpec((tm, 4 * LANES), lambda b, i: (b * nt + i, z_col)),
                  pl.BlockSpec((1, HEAD_DIM), lambda b, i: (0, 0))],
        out_specs=pl.BlockSpec((tm, 4 * LANES), lambda b, i: (b * nt + i, 0)),
        out_shape=jax.ShapeDtypeStruct((n_batch * seq, 4 * LANES), F32),
        compiler_params=_cparams(("parallel", "parallel")),
        name="gdn_post",
    )(o, z, norm_w.reshape(1, HEAD_DIM))


def _gdn_front(x, qkv_col, gate_col, conv_w, a_log, dt_bias, n_batch, seq, chunk, pad_period=0, pad_rows=0):
    rows = n_batch * seq
    beta, gc = _gdn_gates(x, gate_col, rows, a_log, dt_bias, chunk, pad_period, pad_rows)
    (q,) = _gdn_prep(x, qkv_col, conv_w, beta, gc, n_batch, seq, "q")
    k, kb, gcb = _gdn_prep(x, qkv_col + 4, conv_w, beta, gc, n_batch, seq, "k")
    (vb,) = _gdn_prep(x, qkv_col + 8, conv_w, beta, gc, n_batch, seq, "v")
    gc_row = jnp.transpose(gc[:, GDN_HEADS:2 * GDN_HEADS].reshape(n_batch, seq, GDN_HEADS), (0, 2, 1))
    gc_row = gc_row.reshape(n_batch * GDN_HEADS, seq // chunk, chunk)
    return _gdn_intra(q, k, kb, vb, gcb, gc_row, chunk, min(seq, 512))


def _x_l2norm(x):
    return x * lax.rsqrt(jnp.sum(x * x, -1, keepdims=True) + NORM_EPS)


def _x_rms(x, w):
    return x * lax.rsqrt(jnp.mean(x * x, -1, keepdims=True) + NORM_EPS) * w


def _x_t5_bucket(rel):
    n = jnp.maximum(rel, 0)
    max_exact = T5_BUCKETS // 2
    nf = jnp.maximum(n, 1).astype(F32)
    large = max_exact + (jnp.log(nf / max_exact) / math.log(T5_MAX_DIST / max_exact)
                         * (T5_BUCKETS - max_exact)).astype(jnp.int32)
    large = jnp.minimum(large, T5_BUCKETS - 1)
    return jnp.where(n < max_exact, n, large)


def _x_causal_conv(x, buf, w, b=None):
    L = x.shape[1]
    xp = jnp.concatenate([buf.astype(x.dtype), x], axis=1)
    y = xp[:, CONV_WIDTH - 1:] * w[CONV_WIDTH - 1]
    for j in range(CONV_WIDTH - 1):
        y = y + xp[:, j:j + L] * w[j]
    if b is not None:
        y = y + b
    return y, xp[:, L:]


def _x_query_chunk(n_q, batch):
    cap = max(1, min(n_q, 128 // batch))
    return max(d for d in range(1, cap + 1) if n_q % d == 0)


def _x_moba_attention(q, k, v, q0, bias_t):
    B, Lq, Hq, hd = q.shape
    Lk, KV = k.shape[1], k.shape[2]
    G = Hq // KV
    n_full = Lk // MOBA_BLOCK
    n_sel = min(MOBA_TOPK, n_full)
    head_kv = jnp.arange(Hq) // G
    kmean = k[:, :n_full * MOBA_BLOCK].reshape(B, n_full, MOBA_BLOCK, KV, hd).mean(2, dtype=F32)[:, :, head_kv]
    qc = _x_query_chunk(Lq, B)
    nq = Lq // qc
    q_chunks = q.reshape(B, nq, qc, Hq, hd).swapaxes(0, 1)
    starts = q0 + qc * jnp.arange(nq, dtype=jnp.int32)
    b_ix = jnp.arange(B)[:, None, None, None, None]
    h_ix = jnp.arange(Hq)[None, :, None, None, None]
    kv_ix = head_kv[None, :, None, None, None]
    offs = jnp.arange(MOBA_BLOCK)
    scale = hd ** -0.5

    def one_chunk(args):
        qb, s0 = args
        q_pos = s0 + jnp.arange(qc)
        own = jnp.broadcast_to((q_pos // MOBA_BLOCK)[None, None, :, None], (B, Hq, qc, 1))
        gate = jnp.einsum('bqhd,bnhd->bhqn', qb, kmean, preferred_element_type=F32)
        gate = jnp.where(jnp.arange(n_full) < own, gate, -jnp.inf)
        _, sel = lax.top_k(gate, n_sel)
        blocks = jnp.concatenate([sel, own], -1)
        live = jnp.concatenate([sel < own, jnp.ones_like(own, dtype=bool)], -1)
        k_pos = blocks[..., None] * MOBA_BLOCK + offs
        rows = jnp.minimum(k_pos, Lk - 1)
        kg = k[b_ix, rows, kv_ix]
        vg = v[b_ix, rows, kv_ix]
        rel = q_pos[None, None, :, None, None] - k_pos
        s = jnp.einsum('bqhd,bhqsrd->bhqsr', qb, kg, preferred_element_type=F32) * scale
        s = s + bias_t[h_ix, _x_t5_bucket(rel)].astype(F32)
        s = jnp.where(live[..., None] & (rel >= 0), s, -jnp.inf)
        n_keys = s.shape[3] * MOBA_BLOCK
        p = jax.nn.softmax(s.reshape(B, Hq, qc, n_keys), axis=-1).reshape(s.shape)
        o = jnp.einsum('bhqsr,bhqsrd->bqhd', p, vg, preferred_element_type=F32)
        return o.astype(q.dtype)

    out = lax.map(one_chunk, (q_chunks, starts))
    return out.swapaxes(0, 1).reshape(B, Lq, Hq, hd)


def _x_gated_delta_rule(q, k, v, g, beta, s0):
    B, L, H, dk = q.shape
    dv = v.shape[-1]
    C = min(GDN_CHUNK, L)
    pad = (-L) % C
    N = (L + pad) // C

    def chunks(t):
        t = jnp.pad(t.astype(F32), [(0, 0), (0, pad)] + [(0, 0)] * (t.ndim - 2))
        t = t.reshape((B, N, C) + t.shape[2:])
        return jnp.swapaxes(jnp.swapaxes(t, 0, 1), 2, 3)

    q = chunks(q) * dk ** -0.5
    k = chunks(k)
    v = chunks(v)
    g = chunks(g)
    beta = chunks(beta)
    gc = jnp.cumsum(g, -1)
    incl = jnp.tril(jnp.ones((C, C), bool))
    strict = jnp.tril(jnp.ones((C, C), bool), -1)
    decay = jnp.exp(jnp.where(incl, gc[..., :, None] - gc[..., None, :], -jnp.inf))
    kb = k * beta[..., None]
    lower = jnp.where(strict, jnp.einsum('nbhcd,nbhsd->nbhcs', kb, k) * decay, 0.0)
    t_mat = lower + jnp.eye(C, dtype=F32)
    rhs = jnp.concatenate([v * beta[..., None], kb * jnp.exp(gc)[..., None]], -1)
    sol = lax.linalg.triangular_solve(t_mat, rhs, left_side=True, lower=True, unit_diagonal=True)
    u, w = sol[..., :dv], sol[..., dv:]
    qk = jnp.einsum('nbhcd,nbhsd->nbhcs', q, k) * decay
    qg = q * jnp.exp(gc)[..., None]
    kd = k * jnp.exp(gc[..., -1:] - gc)[..., None]
    glast = jnp.exp(gc[..., -1])[..., None, None]

    def step(S, xs):
        u_n, w_n, qk_n, qg_n, kd_n, gl_n = xs
        v_new = u_n - jnp.einsum('bhcd,bhde->bhce', w_n, S)
        o_n = jnp.einsum('bhcd,bhde->bhce', qg_n, S) + jnp.einsum('bhcs,bhse->bhce', qk_n, v_new)
        S = S * gl_n + jnp.einsum('bhcd,bhce->bhde', kd_n, v_new)
        return S, o_n

    S, o = lax.scan(step, s0.astype(F32), (u, w, qk, qg, kd, glast))
    o = jnp.swapaxes(jnp.swapaxes(o, 2, 3), 0, 1).reshape(B, N * C, H, dv)[:, :L]
    return o, S


def _x_sink_attention(q, k, v, q_pos, k_pos, sinks, bias_t):
    *lead, Lq, Hq, hd = q.shape
    KV = k.shape[-2]
    G = Hq // KV
    qg = q.reshape(*lead, Lq, KV, G, hd)
    s = jnp.einsum('...qkgd,...skd->...kgqs', qg, k, preferred_element_type=F32) * hd ** -0.5
    rel = q_pos[..., :, None] - k_pos[..., None, :]
    ok = (rel >= 0) & (rel <= WINDOW) & (k_pos[..., None, :] >= 0)
    bias = jnp.moveaxis(bias_t.reshape(KV, G, T5_BUCKETS)[:, :, _x_t5_bucket(rel)], (0, 1), (-4, -3))
    s = jnp.where(ok[..., None, None, :, :], s + bias.astype(F32), -jnp.inf)
    sink = sinks.astype(F32).reshape(KV, G)[:, :, None, None]
    m = jnp.maximum(s.max(-1, keepdims=True), sink)
    p = jnp.exp(s - m)
    den = p.sum(-1, keepdims=True) + jnp.exp(sink - m)
    o = jnp.einsum('...kgqs,...skd->...qkgd', p / den, v, preferred_element_type=F32)
    return o.reshape(*lead, Lq, Hq, hd).astype(q.dtype)


def _x_swa_prompt(q, k, v, sinks, bias_t):
    B, L, Hq, hd = q.shape
    W = WINDOW
    nb = L // W

    def band(t):
        tb = t.reshape(B, nb, W, t.shape[2], hd)
        prev = jnp.concatenate([jnp.zeros_like(tb[:, :1]), tb[:, :-1]], axis=1)
        return jnp.concatenate([prev, tb], axis=2)

    base = W * jnp.arange(nb)[:, None]
    q_pos = base + jnp.arange(W)
    k_pos = base - W + jnp.arange(2 * W)
    qb = q.reshape(B, nb, W, Hq, hd)
    o = _x_sink_attention(qb, band(k), band(v), q_pos, k_pos, sinks, bias_t)
    return o.reshape(B, nb * W, Hq, hd)


def _x_lru_combine(x, y):
    a1, b1 = x
    a2, b2 = y
    return a1 * a2, a2 * b1 + b2


def _x_rg_lru(x, h0, w_a, b_a, w_x, b_x, lam):
    B, L, _ = x.shape
    xf = x.astype(F32)
    xb = xf.reshape(B, L, LRU_BLOCKS, D_RNN // LRU_BLOCKS)
    r = jax.nn.sigmoid(jnp.einsum('blni,nij->blnj', xb, w_a).reshape(B, L, D_RNN) + b_a)
    i = jax.nn.sigmoid(jnp.einsum('blni,nij->blnj', xb, w_x).reshape(B, L, D_RNN) + b_x)
    log_a = -LRU_C * r * jax.nn.softplus(-lam.astype(F32))
    a = jnp.exp(log_a)
    b = jnp.sqrt(-jnp.expm1(2.0 * log_a)) * (i * xf)
    b = b.at[:, 0].add(a[:, 0] * h0.astype(F32))
    _, h = lax.associative_scan(_x_lru_combine, (a, b), axis=1)
    return h


def _x_even_rest(p, b, l, past_k, past_v, s0, conv0, conv_w, a_log, dt_bias, norm_w, bias_t, o_a=None):
    qa, ka, va, qkv, z, beta_in, decay_in = p
    if o_a is None:
        qa4 = qa.reshape(b, l, N_ATT_HEADS, HEAD_DIM)
        ka4 = ka.reshape(b, l, KV_HEADS_A, HEAD_DIM)
        va4 = va.reshape(b, l, KV_HEADS_A, HEAD_DIM)
        past = past_k.shape[1]
        k_all = jnp.concatenate([past_k, ka4], axis=1)
        v_all = jnp.concatenate([past_v, va4], axis=1)
        o_a = _x_moba_attention(qa4, k_all, v_all, past, bias_t).reshape(b, l, -1)
    c, conv_new = _x_causal_conv(qkv, conv0, conv_w)
    c = jax.nn.silu(c)
    qb, kb, vb = jnp.split(c, [512, 1024], axis=-1)
    qb = _x_l2norm(qb.reshape(b, l, GDN_HEADS, 64))
    kb = _x_l2norm(kb.reshape(b, l, GDN_HEADS, 64))
    vb = vb.reshape(b, l, GDN_HEADS, 64)
    beta = jax.nn.sigmoid(beta_in)
    g = -jnp.exp(a_log) * jax.nn.softplus(decay_in + dt_bias)
    o_b, s_new = _x_gated_delta_rule(qb, kb, vb, g, beta, s0)
    gate = jax.nn.silu(z.reshape(b, l, GDN_HEADS, 64))
    o_b = _x_rms(o_b, norm_w) * gate
    mixed = jnp.concatenate([o_a, o_b.reshape(b, l, -1)], axis=-1)
    return mixed, s_new, conv_new


def _x_odd_rest(p, b, l, buf_k, buf_v, h0, conv0, past_len, sinks, conv_w, conv_b,
                w_a, b_a, w_x, b_x, lam, bias_t):
    qc, kc, vc, xr, gr = p
    qc = qc.reshape(b, l, N_ATT_HEADS, HEAD_DIM)
    kc = kc.reshape(b, l, KV_HEADS_C, HEAD_DIM)
    vc = vc.reshape(b, l, KV_HEADS_C, HEAD_DIM)
    if buf_k is None:
        o_c = _x_swa_prompt(qc, kc, vc, sinks, bias_t)
        k_keep, v_keep = kc[:, -WINDOW:], vc[:, -WINDOW:]
    else:
        nbuf = buf_k.shape[1]
        k_all = jnp.concatenate([buf_k, kc], axis=1)
        v_all = jnp.concatenate([buf_v, vc], axis=1)
        q_pos = past_len + jnp.arange(l)
        k_pos = past_len - nbuf + jnp.arange(nbuf + l)
        o_c = _x_sink_attention(qc, k_all, v_all, q_pos, k_pos, sinks, bias_t)
        k_keep, v_keep = k_all[:, -nbuf:], v_all[:, -nbuf:]
    xc, conv_new = _x_causal_conv(xr, conv0, conv_w, conv_b)
    h = _x_rg_lru(xc, h0, w_a, b_a, w_x, b_x, lam)
    o_d = h * jax.nn.gelu(gr)
    mixed = jnp.concatenate([o_c.reshape(b, l, -1), o_d], axis=-1)
    return mixed, k_keep, v_keep, h[:, -1], conv_new


TM = 768


def _pad_cols(w, n):
    return jnp.pad(w, ((0, 0), (0, n - w.shape[1])))


def _x_kernel_fallback(x_prompt, x_sample, cache_k_moba, cache_v_moba, state_gdn, state_gdn_conv, cache_k_swa, cache_v_swa, state_lru, state_lru_conv, page_table, w_in_even, w_out_even, gdn_conv_w, gdn_a_log, gdn_dt_bias, gdn_norm_w, w_in_odd, w_out_odd, swa_sinks, lru_conv_w, lru_conv_b, lru_w_a, lru_b_a, lru_w_x, lru_b_x, lru_lambda, rel_bias, ln_g, ln_b, router_w, router_b, w_gate, w_up, w_down):
    n_pr, seq, d = x_prompt.shape
    n_dec, dec_seq, _ = x_sample.shape
    tp = n_pr * seq
    ts = n_dec * dec_seq
    past_len = page_table.shape[1] * PAGE_SIZE
    bias_t = rel_bias.T
    h = jnp.concatenate([x_prompt.reshape(tp, d), x_sample.reshape(ts, d)], axis=0)

    rw_pad = _pad_cols(router_w, LANES)
    rb_pad = _pad_cols(router_b.reshape(1, -1), LANES)
    wg_b, wu_b, wd_b = w_gate.astype(BF16), w_up.astype(BF16), w_down.astype(BF16)

    p_even = sum(EVEN_SIZES)
    n_even = 3200
    w0 = _pad_cols(w_in_even[0], n_even).astype(BF16)
    proj = _mm(h, w0, TM, 640)
    splits = np.cumsum(EVEN_SIZES)[:-1].tolist()
    pp = [t.reshape(n_pr, seq, -1) for t in jnp.split(proj[:tp, :p_even], splits, axis=-1)]
    ps = [t.reshape(n_dec, dec_seq, -1) for t in jnp.split(proj[tp:, :p_even], splits, axis=-1)]
    moba_k_prompt = pp[1].reshape(n_pr, seq, 1, KV_HEADS_A, HEAD_DIM)
    moba_v_prompt = pp[2].reshape(n_pr, seq, 1, KV_HEADS_A, HEAD_DIM)
    moba_k_sample = ps[1].reshape(n_dec, dec_seq, 1, KV_HEADS_A, HEAD_DIM)
    moba_v_sample = ps[2].reshape(n_dec, dec_seq, 1, KV_HEADS_A, HEAD_DIM)

    o_a_p = _moba_prompt(proj, rel_bias, n_pr, seq, 0, 4, 6).reshape(n_pr, seq, -1)
    mixed_p, gdn_sp, gdn_cp = _x_even_rest(
        pp, n_pr, seq, None, None, jnp.zeros((n_pr, GDN_HEADS, 64, 64), F32),
        jnp.zeros((n_pr, CONV_WIDTH - 1, GDN_CONV_CH), F32),
        gdn_conv_w[0], gdn_a_log[0], gdn_dt_bias[0], gdn_norm_w[0], bias_t, o_a=o_a_p)
    past_k = cache_k_moba[page_table][:, :, :, 0].reshape(n_dec, past_len, KV_HEADS_A, HEAD_DIM)
    past_v = cache_v_moba[page_table][:, :, :, 0].reshape(n_dec, past_len, KV_HEADS_A, HEAD_DIM)
    mixed_s, gdn_ss, gdn_cs = _x_even_rest(
        ps, n_dec, dec_seq, past_k, past_v, state_gdn[0], state_gdn_conv[0],
        gdn_conv_w[0], gdn_a_log[0], gdn_dt_bias[0], gdn_norm_w[0], bias_t)
    mixed = jnp.concatenate([mixed_p.reshape(tp, -1), mixed_s.reshape(ts, -1)], axis=0)
    h = _mm_ln(mixed, w_out_even[0].astype(BF16), h, ln_g[0, 0], ln_b[0, 0], TM)
    h = _moe_ln(h, rw_pad, rb_pad, wg_b[0], wu_b[0], wd_b[0], ln_g[0, 1], ln_b[0, 1], TM)

    p_odd = sum(ODD_SIZES)
    proj = _mm(h, w_in_odd[0].astype(BF16), TM, 896)
    splits = np.cumsum(ODD_SIZES)[:-1].tolist()
    pp = [t.reshape(n_pr, seq, -1) for t in jnp.split(proj[:tp], splits, axis=-1)]
    ps = [t.reshape(n_dec, dec_seq, -1) for t in jnp.split(proj[tp:], splits, axis=-1)]
    mixed_p, swa_kp, swa_vp, lru_p, lru_cp = _x_odd_rest(
        pp, n_pr, seq, None, None, jnp.zeros((n_pr, D_RNN), F32),
        jnp.zeros((n_pr, CONV_WIDTH - 1, D_RNN), F32), 0,
        swa_sinks[0], lru_conv_w[0], lru_conv_b[0], lru_w_a[0], lru_b_a[0], lru_w_x[0], lru_b_x[0],
        lru_lambda[0], bias_t)
    mixed_s, swa_ks, swa_vs, lru_s, lru_cs = _x_odd_rest(
        ps, n_dec, dec_seq, cache_k_swa[0], cache_v_swa[0], state_lru[0], state_lru_conv[0], past_len,
        swa_sinks[0], lru_conv_w[0], lru_conv_b[0], lru_w_a[0], lru_b_a[0], lru_w_x[0], lru_b_x[0],
        lru_lambda[0], bias_t)
    mixed = jnp.concatenate([mixed_p.reshape(tp, -1), mixed_s.reshape(ts, -1)], axis=0)
    h = _mm_ln(mixed, w_out_odd[0].astype(BF16), h, ln_g[1, 0], ln_b[1, 0], TM)
    h = _moe_ln(h, rw_pad, rb_pad, wg_b[1], wu_b[1], wd_b[1], ln_g[1, 1], ln_b[1, 1], TM)

    y_prompt = h[:tp].reshape(n_pr, seq, d)
    y_sample = h[tp:].reshape(n_dec, dec_seq, d)
    return (y_prompt, y_sample, moba_k_prompt, moba_v_prompt, moba_k_sample, moba_v_sample,
            gdn_sp[None], gdn_ss[None], gdn_cp[None], gdn_cs[None],
            swa_kp[None], swa_vp[None], swa_ks[None], swa_vs[None],
            lru_p[None], lru_s[None], lru_cp[None], lru_cs[None])


GDN_DEC_PERIOD = 8
EVEN_COLS = 3200


def kernel(x_prompt, x_sample, cache_k_moba, cache_v_moba, state_gdn, state_gdn_conv, cache_k_swa, cache_v_swa, state_lru, state_lru_conv, page_table, w_in_even, w_out_even, gdn_conv_w, gdn_a_log, gdn_dt_bias, gdn_norm_w, w_in_odd, w_out_odd, swa_sinks, lru_conv_w, lru_conv_b, lru_w_a, lru_b_a, lru_w_x, lru_b_x, lru_lambda, rel_bias, ln_g, ln_b, router_w, router_b, w_gate, w_up, w_down):
    n_pr, seq, d = x_prompt.shape
    n_dec, dec_seq, _ = x_sample.shape
    tp = n_pr * seq
    ts = n_dec * dec_seq
    n_pool = cache_k_moba.shape[0]
    kva = KV_HEADS_A * HEAD_DIM
    kvc = KV_HEADS_C * HEAD_DIM
    h = jnp.concatenate([x_prompt.reshape(tp, d), x_sample.reshape(ts, d)], axis=0)

    rw_pad = _pad_cols(router_w, LANES).T
    rb_pad = router_b.reshape(N_EXPERTS, 1)
    experts = [_group_experts(w_gate[layer], w_up[layer], w_down[layer]) for layer in range(DEPTH)]

    proj = _mm(h, _pad_cols(w_in_even[0], EVEN_COLS).astype(BF16), TM, 640)
    ps = proj[tp:]
    moba_k_prompt = proj[:tp, 512:512 + kva].reshape(n_pr, seq, 1, KV_HEADS_A, HEAD_DIM)
    moba_v_prompt = proj[:tp, 768:768 + kva].reshape(n_pr, seq, 1, KV_HEADS_A, HEAD_DIM)
    k_new = ps[:, 512:512 + kva].reshape(n_dec, dec_seq, kva)
    v_new = ps[:, 768:768 + kva].reshape(n_dec, dec_seq, kva)
    moba_k_sample = k_new.reshape(n_dec, dec_seq, 1, KV_HEADS_A, HEAD_DIM)
    moba_v_sample = v_new.reshape(n_dec, dec_seq, 1, KV_HEADS_A, HEAD_DIM)

    oa_p = _moba_prompt(proj, rel_bias, n_pr, seq, 0, 4, 6)
    oa_s = _moba_sample(ps[:, :512].reshape(n_dec, dec_seq, N_ATT_HEADS, HEAD_DIM), k_new, v_new,
                        jnp.swapaxes(cache_k_moba.reshape(n_pool, PAGE_SIZE, -1), 1, 2),
                        jnp.swapaxes(cache_v_moba.reshape(n_pool, PAGE_SIZE, -1), 1, 2),
                        page_table, rel_bias, layer=0)

    parts = _gdn_front(proj, 8, 24, gdn_conv_w[0], gdn_a_log[0], gdn_dt_bias[0], n_pr, seq, GDN_CHUNK)
    o_p, s_fin = _gdn_scan(*parts, GDN_CHUNK, GDN_HEADS, 512)
    ob_p = _gdn_post(o_p, proj, 5, gdn_norm_w[0], n_pr, seq, 512)
    gdn_state_prompt = s_fin.reshape(1, n_pr, GDN_HEADS, HEAD_DIM, HEAD_DIM)
    gdn_conv_prompt = proj[:tp, 1024:1024 + GDN_CONV_CH].reshape(n_pr, seq, GDN_CONV_CH)[None, :, seq - 3:]

    period = GDN_DEC_PERIOD
    lead = period - dec_seq
    qkv_s = ps[:, 1024:1024 + GDN_CONV_CH].reshape(n_dec, dec_seq, GDN_CONV_CH)
    x8 = jnp.concatenate([jnp.zeros((n_dec, lead - 3, GDN_CONV_CH), F32), state_gdn_conv[0], qkv_s], axis=1)
    tail8 = jnp.pad(ps[:, 2560:EVEN_COLS].reshape(n_dec, dec_seq, EVEN_COLS - 2560), ((0, 0), (lead, 0), (0, 0)))
    xs = jnp.concatenate([x8, tail8], axis=-1).reshape(n_dec * period, EVEN_COLS - 1024)
    parts = _gdn_front(xs, 0, 16, gdn_conv_w[0], gdn_a_log[0], gdn_dt_bias[0], 1, n_dec * period, period,
                       period, lead)
    o_s, gdn_state_sample = _gdn_step(state_gdn[0], *parts, period)
    ob_s = _gdn_post(o_s, xs, 3, gdn_norm_w[0], 1, n_dec * period, 512)
    ob_s = ob_s.reshape(n_dec, period, 512)[:, lead:].reshape(ts, 512)
    gdn_conv_sample = qkv_s[None, :, dec_seq - 3:]

    oa = jnp.concatenate([oa_p, oa_s.reshape(ts, 512)], axis=0)
    ob = jnp.concatenate([ob_p, ob_s], axis=0)
    h = _mm_ln(oa, ob, w_out_even[0].astype(BF16), h, ln_g[0, 0], ln_b[0, 0], TM)
    h = _moe_ln(h, rw_pad, rb_pad, *experts[0], ln_g[0, 1], ln_b[0, 1], TM)

    proj = _mm(h, w_in_odd[0].astype(BF16), TM, 896)
    ps = proj[tp:]
    wa_bd, wx_bd = _block_diag(lru_w_a[0]), _block_diag(lru_w_x[0])
    oc_p = _swa_prompt(proj, rel_bias, swa_sinks[0], n_pr, seq, 0, 4, 5)
    od_p, lru_p = _lru_prompt(proj, n_pr, seq, 6, 10, lru_conv_w[0], lru_conv_b[0], wa_bd, lru_b_a[0],
                              wx_bd, lru_b_x[0], lru_lambda[0])
    swa_k_prompt = proj[:tp, 512:512 + kvc].reshape(n_pr, seq, KV_HEADS_C, HEAD_DIM)[None, :, seq - WINDOW:]
    swa_v_prompt = proj[:tp, 640:640 + kvc].reshape(n_pr, seq, KV_HEADS_C, HEAD_DIM)[None, :, seq - WINDOW:]
    lru_conv_prompt = proj[:tp, 768:768 + D_RNN].reshape(n_pr, seq, D_RNN)[None, :, seq - 3:]

    nbuf = cache_k_swa.shape[2]
    kc_s = ps[:, 512:512 + kvc].reshape(n_dec, dec_seq, kvc)
    vc_s = ps[:, 640:640 + kvc].reshape(n_dec, dec_seq, kvc)
    buf_k = cache_k_swa[0].reshape(n_dec, nbuf, kvc)
    buf_v = cache_v_swa[0].reshape(n_dec, nbuf, kvc)
    oc_s = _swa_sample(ps[:, :512].reshape(n_dec, dec_seq, N_ATT_HEADS, HEAD_DIM), kc_s, vc_s, buf_k, buf_v,
                       rel_bias, swa_sinks[0])
    swa_k_sample = jnp.concatenate([buf_k, kc_s], axis=1)[:, dec_seq:].reshape(1, n_dec, nbuf, KV_HEADS_C, HEAD_DIM)
    swa_v_sample = jnp.concatenate([buf_v, vc_s], axis=1)[:, dec_seq:].reshape(1, n_dec, nbuf, KV_HEADS_C, HEAD_DIM)
    xr_s = ps[:, 768:768 + D_RNN].reshape(n_dec, dec_seq, D_RNN)
    gr_s = ps[:, 1280:1280 + D_RNN].reshape(n_dec, dec_seq, D_RNN)
    od_s, lru_s = _lru_sample(xr_s, gr_s, state_lru[0], state_lru_conv[0], lru_conv_w[0], lru_conv_b[0],
                              wa_bd, lru_b_a[0], wx_bd, lru_b_x[0], lru_lambda[0])
    lru_conv_sample = jnp.concatenate([state_lru_conv[0], xr_s], axis=1)[None, :, dec_seq:]

    oc = jnp.concatenate([oc_p, oc_s.reshape(ts, 512)], axis=0)
    od = jnp.concatenate([od_p, od_s.reshape(ts, 512)], axis=0)
    h = _mm_ln(oc, od, w_out_odd[0].astype(BF16), h, ln_g[1, 0], ln_b[1, 0], TM)
    h = _moe_ln(h, rw_pad, rb_pad, *experts[1], ln_g[1, 1], ln_b[1, 1], TM)

    y_prompt = h[:tp].reshape(n_pr, seq, d)
    y_sample = h[tp:].reshape(n_dec, dec_seq, d)
    return (y_prompt, y_sample, moba_k_prompt, moba_v_prompt, moba_k_sample, moba_v_sample,
            gdn_state_prompt, gdn_state_sample[None], gdn_conv_prompt, gdn_conv_sample,
            swa_k_prompt, swa_v_prompt, swa_k_sample, swa_v_sample,
            lru_p[None], lru_s[None], lru_conv_prompt, lru_conv_sample)
```

```python
import functools
import math

import jax
import jax.numpy as jnp
import numpy as np
from jax import lax
from jax.experimental import pallas as pl
from jax.experimental.pallas import tpu as pltpu

F32 = jnp.float32
BF16 = jnp.bfloat16

D_MODEL = 1024
HEAD_DIM = 64
N_ATT_HEADS = 8
KV_HEADS_A = 4
KV_HEADS_C = 2
MOBA_BLOCK = 256
MOBA_TOPK = 3
PAGE_SIZE = 128
GDN_HEADS = 8
GDN_CHUNK = 64
GDN_CONV_CH = 1536
CONV_WIDTH = 4
WINDOW = 128
D_RNN = 512
LRU_BLOCKS = 8
LRU_C = 8.0
T5_BUCKETS = 32
T5_MAX_DIST = 128
N_EXPERTS = 16
N_GROUPS = 4
EXPERTS_PER_GROUP = 4
D_EXPERT = 256
DEPTH = 2
DEEPNORM_ALPHA = (2 * DEPTH) ** 0.25
LN_EPS = 1e-5
NORM_EPS = 1e-6
EVEN_SIZES = (512, 256, 256, 1536, 512, 8, 8)
ODD_SIZES = (512, 128, 128, 512, 512)

LANES = 128
VMEM_LIMIT = 56 * 1024 * 1024
NEG_INF = float("-inf")


def _cparams(sem):
    return pltpu.CompilerParams(dimension_semantics=sem, vmem_limit_bytes=VMEM_LIMIT)


def _dot(a, b):
    return jnp.dot(a.astype(BF16), b.astype(BF16), preferred_element_type=F32)


def _dot_nt(a, b):
    return lax.dot_general(a.astype(BF16), b.astype(BF16), (((1,), (1,)), ((), ())),
                           preferred_element_type=F32)


def _mm_kernel(x_ref, w_ref, o_ref, *, tn):
    xb = x_ref[...].astype(BF16)
    for c in range(w_ref.shape[1] // tn):
        o_ref[:, c * tn:(c + 1) * tn] = jnp.dot(xb, w_ref[:, c * tn:(c + 1) * tn], preferred_element_type=F32)


def _mm(x, w, tm, tn):
    m, k = x.shape
    n = w.shape[1]
    assert m % tm == 0 and n % tn == 0
    return pl.pallas_call(
        functools.partial(_mm_kernel, tn=tn),
        grid=(m // tm,),
        in_specs=[pl.BlockSpec((tm, k), lambda i: (i, 0)),
                  pl.BlockSpec((k, n), lambda i: (0, 0))],
        out_specs=pl.BlockSpec((tm, n), lambda i: (i, 0)),
        out_shape=jax.ShapeDtypeStruct((m, n), F32),
        compiler_params=_cparams(("parallel",)),
        name="in_proj",
    )(x, w)


def _layer_norm_rows(z, g, b):
    mu = jnp.mean(z, axis=-1, keepdims=True)
    zc = z - mu
    var = jnp.mean(zc * zc, axis=-1, keepdims=True)
    return zc * lax.rsqrt(var + LN_EPS) * g + b


def _mm_ln_kernel(xa_ref, xb_ref, wa_ref, wb_ref, h_ref, g_ref, b_ref, o_ref):
    f = (jnp.dot(xa_ref[...].astype(BF16), wa_ref[...], preferred_element_type=F32)
         + jnp.dot(xb_ref[...].astype(BF16), wb_ref[...], preferred_element_type=F32))
    o_ref[...] = _layer_norm_rows(DEEPNORM_ALPHA * h_ref[...] + f, g_ref[...], b_ref[...])


def _mm_ln(xa, xb, w, h, g, b, tm):
    m, ka = xa.shape
    kb = xb.shape[1]
    n = w.shape[1]
    assert m % tm == 0 and ka == kb
    return pl.pallas_call(
        _mm_ln_kernel,
        grid=(m // tm,),
        in_specs=[pl.BlockSpec((tm, ka), lambda i: (i, 0)),
                  pl.BlockSpec((tm, kb), lambda i: (i, 0)),
                  pl.BlockSpec((ka, n), lambda i: (0, 0)),
                  pl.BlockSpec((kb, n), lambda i: (1, 0)),
                  pl.BlockSpec((tm, n), lambda i: (i, 0)),
                  pl.BlockSpec((1, n), lambda i: (0, 0)),
                  pl.BlockSpec((1, n), lambda i: (0, 0))],
        out_specs=pl.BlockSpec((tm, n), lambda i: (i, 0)),
        out_shape=jax.ShapeDtypeStruct((m, n), F32),
        compiler_params=_cparams(("parallel",)),
        name="out_proj_ln",
    )(xa, xb, w, w, h, g.reshape(1, n), b.reshape(1, n))


def _row_max_first(vals, lane):
    mx = jnp.max(vals, axis=1, keepdims=True)
    idx = jnp.min(jnp.where(vals == mx, lane, LANES), axis=1, keepdims=True)
    return mx, idx


def _moe_ln_kernel(t_ref, rw_ref, rb_ref, wg_ref, wu_ref, wd_ref, g_ref, b_ref, o_ref,
                   gates_ref, tb_ref, hb_ref, acc_ref):
    e = pl.program_id(1)

    @pl.when(e == 0)
    def _route():
        t = t_ref[...]
        logits = lax.dot_general(rw_ref[...], t, (((1,), (1,)), ((), ())), preferred_element_type=F32,
                                 precision=lax.Precision.HIGHEST)[:N_EXPERTS]
        s = jax.nn.sigmoid(logits)
        sb = s + rb_ref[...]
        row = lax.broadcasted_iota(jnp.int32, sb.shape, 0)
        grp = row // EXPERTS_PER_GROUP

        def max_first(vals):
            mx = jnp.max(vals, axis=0, keepdims=True)
            return mx, jnp.min(jnp.where(vals == mx, row, N_EXPERTS), axis=0, keepdims=True)

        best = None
        gsel = None
        for gi in range(N_GROUPS):
            vals = jnp.where(grp == gi, sb, NEG_INF)
            m1, i1 = max_first(vals)
            m2 = jnp.max(jnp.where(row == i1, NEG_INF, vals), axis=0, keepdims=True)
            score = m1 + m2
            if gi == 0:
                best, gsel = score, jnp.zeros_like(i1)
            else:
                upd = score > best
                best = jnp.where(upd, score, best)
                gsel = jnp.where(upd, gi, gsel)
        vals = jnp.where(grp == gsel, sb, NEG_INF)
        _, i1 = max_first(vals)
        _, i2 = max_first(jnp.where(row == i1, NEG_INF, vals))
        s1 = jnp.sum(jnp.where(row == i1, s, 0.0), axis=0, keepdims=True)
        s2 = jnp.sum(jnp.where(row == i2, s, 0.0), axis=0, keepdims=True)
        den = s1 + s2
        gates_t = jnp.where(row == i1, s1 / den, 0.0) + jnp.where(row == i2, s2 / den, 0.0)
        gates_t = jnp.concatenate([gates_t, jnp.zeros((LANES - N_EXPERTS, gates_t.shape[1]), F32)], axis=0)
        gates_ref[...] = gates_t.T
        tb_ref[...] = t.astype(BF16)
        acc_ref[...] = jnp.zeros_like(acc_ref)

    tb = tb_ref[...]
    gates = gates_ref[...]
    lane = lax.broadcasted_iota(jnp.int32, gates.shape, 1)
    hg = jnp.dot(tb, wg_ref[...], preferred_element_type=F32)
    hu = jnp.dot(tb, wu_ref[...], preferred_element_type=F32)
    for j in range(EXPERTS_PER_GROUP):
        sl = slice(j * D_EXPERT, (j + 1) * D_EXPERT)
        ge = jnp.sum(jnp.where(lane == e * EXPERTS_PER_GROUP + j, gates, 0.0), axis=1, keepdims=True)
        g = hg[:, sl]
        hb_ref[:, sl] = ((g * jax.nn.sigmoid(g)) * hu[:, sl] * ge).astype(BF16)
    acc_ref[...] += jnp.dot(hb_ref[...], wd_ref[...], preferred_element_type=F32)

    @pl.when(e == pl.num_programs(1) - 1)
    def _finish():
        o_ref[...] = _layer_norm_rows(DEEPNORM_ALPHA * t_ref[...] + acc_ref[...], g_ref[...], b_ref[...])


def _group_experts(w_gate, w_up, w_down):
    ne, d, f = w_gate.shape
    es = EXPERTS_PER_GROUP
    side = lambda w: jnp.transpose(w.astype(BF16).reshape(ne // es, es, d, f), (0, 2, 1, 3)).reshape(ne // es, d, es * f)
    return side(w_gate), side(w_up), w_down.astype(BF16).reshape(ne // es, es * f, d)


def _moe_ln(t, router_w_pad, router_b_pad, wg, wu, wd, g, b, tm):
    m, d = t.shape
    ng, _, gf = wg.shape
    assert m % tm == 0
    return pl.pallas_call(
        _moe_ln_kernel,
        grid=(m // tm, ng),
        in_specs=[pl.BlockSpec((tm, d), lambda i, e: (i, 0)),
                  pl.BlockSpec((LANES, d), lambda i, e: (0, 0)),
                  pl.BlockSpec((N_EXPERTS, 1), lambda i, e: (0, 0)),
                  pl.BlockSpec((None, d, gf), lambda i, e: (e, 0, 0)),
                  pl.BlockSpec((None, d, gf), lambda i, e: (e, 0, 0)),
                  pl.BlockSpec((None, gf, d), lambda i, e: (e, 0, 0)),
                  pl.BlockSpec((1, d), lambda i, e: (0, 0)),
                  pl.BlockSpec((1, d), lambda i, e: (0, 0))],
        out_specs=pl.BlockSpec((tm, d), lambda i, e: (i, 0)),
        out_shape=jax.ShapeDtypeStruct((m, d), F32),
        scratch_shapes=[pltpu.VMEM((tm, LANES), F32),
                        pltpu.VMEM((tm, d), BF16),
                        pltpu.VMEM((tm, gf), BF16),
                        pltpu.VMEM((tm, d), F32)],
        compiler_params=_cparams(("parallel", "arbitrary")),
        name="moe_ln",
    )(t, router_w_pad, router_b_pad, wg, wu, wd, g.reshape(1, d), b.reshape(1, d))


def _t5_bucket_np(rel):
    n = np.maximum(rel, 0)
    max_exact = T5_BUCKETS // 2
    nf = np.maximum(n, 1).astype(np.float32)
    large = max_exact + (np.log(nf / np.float32(max_exact)) / np.float32(math.log(T5_MAX_DIST / max_exact))
                         * np.float32(T5_BUCKETS - max_exact)).astype(np.int32)
    large = np.minimum(large, T5_BUCKETS - 1)
    return np.where(n < max_exact, n, large).astype(np.int32)


def _old_moba_prompt_kernel(q_ref, k_ref, v_ref, bd_ref, bp_ref, bf_ref, o_ref,
                        kmean_ref, sel_ref, m_ref, l_ref, acc_ref):
    i = pl.program_id(2)
    blk = MOBA_BLOCK
    nblk = k_ref.shape[0] // blk
    hd = HEAD_DIM

    @pl.when(i == 0)
    def _means():
        for n in range(nblk):
            kmean_ref[n:n + 1, :] = jnp.mean(k_ref[n * blk:(n + 1) * blk, :], axis=0, keepdims=True)

    q4 = q_ref[...] * (hd ** -0.5)
    for kvl in range(2):
        lo = kvl * 2 * hd
        q2 = jnp.concatenate([q4[:, lo:lo + hd], q4[:, lo + hd:lo + 2 * hd]], axis=0)
        q2b = q2.astype(BF16)
        kcol = slice(kvl * hd, (kvl + 1) * hd)

        kmean = kmean_ref[:, kcol]
        gate = lax.dot_general(q2, kmean, (((1,), (1,)), ((), ())), preferred_element_type=F32,
                               precision=lax.Precision.HIGHEST)
        col = lax.broadcasted_iota(jnp.int32, gate.shape, 1)
        gate = jnp.where(col < i, gate, NEG_INF)
        sel = jnp.zeros(gate.shape, F32)
        for _ in range(MOBA_TOPK):
            mx = jnp.max(gate, axis=1, keepdims=True)
            idx = jnp.min(jnp.where(gate == mx, col, nblk), axis=1, keepdims=True)
            hit = (col == idx) & (mx > NEG_INF)
            sel = jnp.where(hit, 1.0, sel)
            gate = jnp.where(col == idx, NEG_INF, gate)
        sel_ref[...] = sel

        m_ref[...] = jnp.full(m_ref.shape, NEG_INF, F32)
        l_ref[...] = jnp.zeros(l_ref.shape, F32)
        acc_ref[...] = jnp.zeros(acc_ref.shape, F32)

        def update(s, vb):
            m_old = m_ref[...]
            m_new = jnp.maximum(m_old, jnp.max(s, axis=1, keepdims=True))
            alpha = jnp.exp(m_old - m_new)
            p = jnp.exp(s - m_new)
            l_ref[...] = alpha * l_ref[...] + jnp.sum(p, axis=1, keepdims=True)
            acc_ref[...] = alpha * acc_ref[...] + jnp.dot(p.astype(BF16), vb, preferred_element_type=F32)
            m_ref[...] = m_new

        def tile(j):
            r0 = pl.multiple_of(j * blk, blk)
            kb = k_ref[pl.ds(r0, blk), :][:, kcol].astype(BF16)
            vb = v_ref[pl.ds(r0, blk), :][:, kcol].astype(BF16)
            return _dot_nt(q2b, kb), vb

        s, vb = tile(i)
        update(s + bd_ref[kvl], vb)

        @pl.when(i >= 1)
        def _prev():
            s, vb = tile(i - 1)
            live = jnp.sum(jnp.where(col == i - 1, sel_ref[...], 0.0), axis=1, keepdims=True) > 0.0
            update(jnp.where(live, s + bp_ref[kvl], NEG_INF), vb)

        def far(j, carry):
            s, vb = tile(j)
            live = jnp.sum(jnp.where(col == j, sel_ref[...], 0.0), axis=1, keepdims=True) > 0.0
            rb = jnp.where(live, bf_ref[kvl], NEG_INF)
            update(s + rb, vb)
            return carry

        lax.fori_loop(0, jnp.maximum(i - 1, 0), far, 0)

        o = acc_ref[...] / l_ref[...]
        o_ref[:, lo:lo + hd] = o[:blk]
        o_ref[:, lo + hd:lo + 2 * hd] = o[blk:]


def _old_moba_prompt(proj, bias_rel, n_batch, seq, q_col, k_col, v_col):
    blk = MOBA_BLOCK
    nblk = seq // blk
    iq = np.arange(blk)[:, None]
    ik = np.arange(blk)[None, :]
    idx_diag = _t5_bucket_np(iq - ik)
    idx_prev = _t5_bucket_np(blk + iq - ik)
    bt = bias_rel.T
    diag = jnp.where(jnp.asarray(iq >= ik)[None], bt[:, idx_diag], NEG_INF)
    prev = bt[:, idx_prev]
    far = jnp.broadcast_to(bt[:, T5_BUCKETS - 1][:, None, None], (N_ATT_HEADS, blk, 1))
    diag = diag.reshape(KV_HEADS_A, 2 * blk, blk)
    prev = prev.reshape(KV_HEADS_A, 2 * blk, blk)
    far = far.reshape(KV_HEADS_A, 2 * blk, 1)
    rows_per_b = seq // blk
    return pl.pallas_call(
        _moba_prompt_kernel,
        grid=(n_batch, 2, nblk),
        in_specs=[pl.BlockSpec((blk, 2 * LANES), lambda b, p, i: (b * rows_per_b + i, q_col // 2 + p)),
                  pl.BlockSpec((seq, LANES), lambda b, p, i: (b, k_col + p)),
                  pl.BlockSpec((seq, LANES), lambda b, p, i: (b, v_col + p)),
                  pl.BlockSpec((2, 2 * blk, blk), lambda b, p, i: (p, 0, 0)),
                  pl.BlockSpec((2, 2 * blk, blk), lambda b, p, i: (p, 0, 0)),
                  pl.BlockSpec((2, 2 * blk, 1), lambda b, p, i: (p, 0, 0))],
        out_specs=pl.BlockSpec((blk, 2 * LANES), lambda b, p, i: (b * rows_per_b + i, p)),
        out_shape=jax.ShapeDtypeStruct((n_batch * seq, 4 * LANES), F32),
        scratch_shapes=[pltpu.VMEM((nblk, LANES), F32),
                        pltpu.VMEM((2 * blk, nblk), F32),
                        pltpu.VMEM((2 * blk, 1), F32),
                        pltpu.VMEM((2 * blk, 1), F32),
                        pltpu.VMEM((2 * blk, HEAD_DIM), F32)],
        compiler_params=_cparams(("parallel", "parallel", "arbitrary")),
        name="moba_prompt",
    )(proj, proj, proj, diag, prev, far)


def _t5_thresholds():
    b = _t5_bucket_np(np.arange(4 * T5_MAX_DIST))
    return [int(np.argmax(b >= k)) for k in range(1, T5_BUCKETS)]


def _bias_tile_kernel(rb_ref, o_ref, *, off, sign, lo, hi):
    h = pl.program_id(0)
    row = lax.broadcasted_iota(jnp.int32, o_ref.shape, 0)
    col = lax.broadcasted_iota(jnp.int32, o_ref.shape, 1)
    rel = off + sign * (row - col)
    t = jnp.full(o_ref.shape, rb_ref[0, h], F32)
    for k, thr in enumerate(_t5_thresholds(), start=1):
        t = jnp.where(rel >= thr, rb_ref[k, h], t)
    if lo is not None:
        t = jnp.where((rel >= lo) & (rel <= hi), t, NEG_INF)
    o_ref[...] = t


def _bias_tiles(rel_bias, rows, cols, off, sign=1, lo=None, hi=None):
    return pl.pallas_call(
        functools.partial(_bias_tile_kernel, off=off, sign=sign, lo=lo, hi=hi),
        grid=(N_ATT_HEADS,),
        in_specs=[pl.BlockSpec(memory_space=pltpu.SMEM)],
        out_specs=pl.BlockSpec((None, rows, cols), lambda h: (h, 0, 0)),
        out_shape=jax.ShapeDtypeStruct((N_ATT_HEADS, rows, cols), F32),
        compiler_params=_cparams(("parallel",)),
        name="bias_tiles",
    )(rel_bias)


def _moba_prompt_kernel(q_ref, k_ref, v_ref, bd_ref, bp_ref, bf_ref, o_ref, kmean_ref, vt_ref, sel_ref):
    i = pl.program_id(2)
    blk = MOBA_BLOCK
    nblk = k_ref.shape[0] // blk
    hd = HEAD_DIM

    @pl.when(i == 0)
    def _per_sequence():
        for n in range(nblk):
            kmean_ref[n:n + 1, :] = jnp.mean(k_ref[n * blk:(n + 1) * blk, :], axis=0, keepdims=True)
            vt_ref[n] = v_ref[n * blk:(n + 1) * blk, :].T.astype(BF16)

    q4 = q_ref[...] * (hd ** -0.5)
    q2b = []
    for kvl in range(2):
        lo = kvl * 2 * hd
        q2 = jnp.concatenate([q4[:, lo:lo + hd], q4[:, lo + hd:lo + 2 * hd]], axis=0)
        q2b.append(q2.astype(BF16))
        gate = lax.dot_general(kmean_ref[:, kvl * hd:(kvl + 1) * hd], q2, (((1,), (1,)), ((), ())),
                               preferred_element_type=F32, precision=lax.Precision.HIGHEST)
        row = lax.broadcasted_iota(jnp.int32, gate.shape, 0)
        gate = jnp.where(row < i, gate, NEG_INF)
        sel = jnp.zeros(gate.shape, F32)
        for _ in range(MOBA_TOPK):
            mx = jnp.max(gate, axis=0, keepdims=True)
            idx = jnp.min(jnp.where(gate == mx, row, nblk), axis=0, keepdims=True)
            hit = (row == idx) & (mx > NEG_INF)
            sel = jnp.where(hit, 1.0, sel)
            gate = jnp.where(row == idx, NEG_INF, gate)
        sel_ref[kvl] = sel

    def scores(j, kvl):
        r0 = pl.multiple_of(j * blk, blk)
        kb = k_ref[pl.ds(r0, blk), :][:, kvl * hd:(kvl + 1) * hd].astype(BF16)
        return _dot_nt(kb, q2b[kvl])

    def update(state, s, j, kvl):
        m_old, l_old, acc = state
        m_new = jnp.maximum(m_old, jnp.max(s, axis=0, keepdims=True))
        alpha = jnp.exp(m_old - m_new)
        p = jnp.exp(s - m_new)
        l_new = alpha * l_old + jnp.sum(p, axis=0, keepdims=True)
        vt = vt_ref[j][kvl * hd:(kvl + 1) * hd, :]
        acc = alpha * acc + jnp.dot(vt, p.astype(BF16), preferred_element_type=F32)
        return m_new, l_new, acc

    def live_row(j, kvl):
        return sel_ref[kvl, pl.ds(j, 1), :] > 0.0

    init = (jnp.full((1, 2 * blk), NEG_INF, F32), jnp.zeros((1, 2 * blk), F32), jnp.zeros((hd, 2 * blk), F32))
    jp = jnp.maximum(i - 1, 0)
    states = []
    for kvl in range(2):
        st = update(init, scores(i, kvl) + bd_ref[kvl], i, kvl)
        s = jnp.where(live_row(jp, kvl), scores(jp, kvl) + bp_ref[kvl], NEG_INF)
        states.append(update(st, s, jp, kvl))

    def far(pr, carry):
        j0 = 2 * pr
        r0 = pl.multiple_of(j0 * blk, 2 * blk)
        second_far = j0 + 1 < i - 1
        out = []
        for kvl in range(2):
            kb = k_ref[pl.ds(r0, 2 * blk), :][:, kvl * hd:(kvl + 1) * hd].astype(BF16)
            s = _dot_nt(kb, q2b[kvl])
            far_bias = bf_ref[kvl]
            s0 = s[:blk] + jnp.where(live_row(j0, kvl), far_bias, NEG_INF)
            s1 = s[blk:] + jnp.where(live_row(j0 + 1, kvl) & second_far, far_bias, NEG_INF)
            m_old, l_old, acc = carry[kvl]
            m_new = jnp.maximum(m_old, jnp.maximum(jnp.max(s0, axis=0, keepdims=True),
                                                   jnp.max(s1, axis=0, keepdims=True)))
            alpha = jnp.exp(m_old - m_new)
            p = jnp.concatenate([jnp.exp(s0 - m_new), jnp.exp(s1 - m_new)], axis=0)
            l_new = alpha * l_old + jnp.sum(p, axis=0, keepdims=True)
            rows = slice(kvl * hd, (kvl + 1) * hd)
            vt = jnp.concatenate([vt_ref[j0][rows, :], vt_ref[j0 + 1][rows, :]], axis=1)
            acc = alpha * acc + jnp.dot(vt, p.astype(BF16), preferred_element_type=F32)
            out.append((m_new, l_new, acc))
        return tuple(out)

    states = lax.fori_loop(0, i // 2, far, tuple(states))
    ot = jnp.concatenate([acc / l for (_, l, acc) in states], axis=0)
    o = ot.T
    for kvl in range(2):
        for hh in range(2):
            o_ref[:, (2 * kvl + hh) * hd:(2 * kvl + hh + 1) * hd] = o[hh * blk:(hh + 1) * blk, kvl * hd:(kvl + 1) * hd]


def _moba_prompt(proj, bias_rel, n_batch, seq, q_col, k_col, v_col):
    blk = MOBA_BLOCK
    nblk = seq // blk

    def pair_up(t):
        return jnp.transpose(t.reshape(KV_HEADS_A, 2, blk, blk), (0, 2, 1, 3)).reshape(KV_HEADS_A, blk, 2 * blk)

    diag = pair_up(_bias_tiles(bias_rel, blk, blk, 0, sign=-1, lo=0, hi=2 * blk))
    prev = pair_up(_bias_tiles(bias_rel, blk, blk, blk, sign=-1))
    far = jnp.broadcast_to(bias_rel[T5_BUCKETS - 1].reshape(KV_HEADS_A, 1, 2, 1), (KV_HEADS_A, 1, 2, blk))
    far = far.reshape(KV_HEADS_A, 1, 2 * blk)
    rows_per_b = seq // blk
    return pl.pallas_call(
        _moba_prompt_kernel,
        grid=(n_batch, 2, nblk),
        in_specs=[pl.BlockSpec((blk, 2 * LANES), lambda b, p, i: (b * rows_per_b + i, q_col // 2 + p)),
                  pl.BlockSpec((seq, LANES), lambda b, p, i: (b, k_col + p)),
                  pl.BlockSpec((seq, LANES), lambda b, p, i: (b, v_col + p)),
                  pl.BlockSpec((2, blk, 2 * blk), lambda b, p, i: (p, 0, 0)),
                  pl.BlockSpec((2, blk, 2 * blk), lambda b, p, i: (p, 0, 0)),
                  pl.BlockSpec((2, 1, 2 * blk), lambda b, p, i: (p, 0, 0))],
        out_specs=pl.BlockSpec((blk, 2 * LANES), lambda b, p, i: (b * rows_per_b + i, p)),
        out_shape=jax.ShapeDtypeStruct((n_batch * seq, 4 * LANES), F32),
        scratch_shapes=[pltpu.VMEM((nblk, LANES), F32),
                        pltpu.VMEM((nblk, LANES, blk), BF16),
                        pltpu.VMEM((2, nblk, 2 * blk), F32)],
        compiler_params=_cparams(("parallel", "parallel", "arbitrary")),
        name="moba_prompt",
    )(proj, proj, proj, diag, prev, far)


MOBA_DEC_BLOCKS_PER_STEP = 8
PAGES_PER_BLOCK = MOBA_BLOCK // PAGE_SIZE


def _moba_sample_kernel(pt_ref, q_ref, kn_ref, vn_ref, b31_ref, bfar_ref, bown_ref, *rest,
                        n_blocks, nb):
    npg = nb * PAGES_PER_BLOCK
    k_pages = rest[:npg]
    v_pages = rest[npg:2 * npg]
    o_ref = rest[2 * npg]
    g_ref, m_ref, l_ref, part_ref = rest[2 * npg + 1:]
    s = pl.program_id(1)
    q = q_ref[...]
    qb = q.astype(BF16)
    lane = lax.broadcasted_iota(jnp.int32, g_ref.shape, 1)

    @pl.when(s == 0)
    def _init():
        g_ref[...] = jnp.zeros_like(g_ref)
        m_ref[...] = jnp.zeros_like(m_ref)
        l_ref[...] = jnp.zeros_like(l_ref)

    for j in range(nb):
        n = s * nb + j
        kts = [k_pages[j * PAGES_PER_BLOCK + t][...] for t in range(PAGES_PER_BLOCK)]
        vts = [v_pages[j * PAGES_PER_BLOCK + t][...] for t in range(PAGES_PER_BLOCK)]
        sc = jnp.concatenate([_dot(qb, kt) for kt in kts], axis=1)
        gate = jnp.sum(sc, axis=1, keepdims=True) * (1.0 / MOBA_BLOCK)
        sc = sc + jnp.where(n == n_blocks - 1, b31_ref[...], bfar_ref[...])
        mx = jnp.max(sc, axis=1, keepdims=True)
        p = jnp.exp(sc - mx)
        hit = lane == n
        g_ref[...] = jnp.where(hit, gate, g_ref[...])
        m_ref[...] = jnp.where(hit, mx, m_ref[...])
        l_ref[...] = jnp.where(hit, jnp.sum(p, axis=1, keepdims=True), l_ref[...])
        o = None
        for t, vt in enumerate(vts):
            part = _dot_nt(p[:, t * PAGE_SIZE:(t + 1) * PAGE_SIZE], vt)
            o = part if o is None else o + part
        part_ref[n] = o

    @pl.when(s == pl.num_programs(1) - 1)
    def _merge():
        gate = jnp.where(lane < n_blocks, g_ref[...], NEG_INF)
        sel = jnp.zeros(gate.shape, jnp.bool_)
        for _ in range(MOBA_TOPK):
            mxg = jnp.max(gate, axis=1, keepdims=True)
            idx = jnp.min(jnp.where(gate == mxg, lane, LANES), axis=1, keepdims=True)
            hit = (lane == idx) & (mxg > NEG_INF)
            sel = sel | hit
            gate = jnp.where(lane == idx, NEG_INF, gate)
        s_own = _dot_nt(qb, kn_ref[...]) + bown_ref[...]
        m_sel = jnp.where(sel, m_ref[...], NEG_INF)
        m_all = jnp.maximum(jnp.max(m_sel, axis=1, keepdims=True), jnp.max(s_own, axis=1, keepdims=True))
        w = jnp.where(sel, jnp.exp(m_sel - m_all), 0.0)
        p_own = jnp.exp(s_own - m_all)
        den = jnp.sum(w * l_ref[...], axis=1, keepdims=True) + jnp.sum(p_own, axis=1, keepdims=True)
        acc = _dot(p_own, vn_ref[...])
        for n in range(n_blocks):
            wn = jnp.sum(jnp.where(lane == n, w, 0.0), axis=1, keepdims=True)
            acc = acc + wn * part_ref[n]
        o_ref[...] = acc / den


def _moba_sample(q_s, k_new, v_new, cache_k, cache_v, page_table, bias_rel, layer=0):
    n_dec, t_dec = q_s.shape[:2]
    n_pages = page_table.shape[1]
    n_blocks = n_pages // PAGES_PER_BLOCK
    nb = min(MOBA_DEC_BLOCKS_PER_STEP, n_blocks)
    npg = nb * PAGES_PER_BLOCK
    assert n_blocks % nb == 0 and n_blocks <= LANES and t_dec <= 8
    rows = N_ATT_HEADS * t_dec
    kv_w = KV_HEADS_A * HEAD_DIM
    head_kv = np.arange(N_ATT_HEADS) // (N_ATT_HEADS // KV_HEADS_A)
    qh = jnp.transpose(q_s, (0, 2, 1, 3)) * (HEAD_DIM ** -0.5)
    onehot = jnp.asarray(np.eye(KV_HEADS_A, dtype=np.float32)[head_kv])
    qbd = (qh[:, :, :, None, :] * onehot[None, :, None, :, None]).reshape(n_dec, rows, kv_w)
    pad = 8 - t_dec
    kn = jnp.pad(k_new, ((0, 0), (0, pad), (0, 0)))
    vn = jnp.pad(v_new, ((0, 0), (0, pad), (0, 0)))
    bt = bias_rel.T
    past = n_pages * PAGE_SIZE
    tq = np.arange(t_dec)[None, :, None]
    ik = np.arange(MOBA_BLOCK)[None, None, :]
    idx31 = _t5_bucket_np(past + tq - ((n_blocks - 1) * MOBA_BLOCK + ik))[0]
    b31 = bt[:, idx31].reshape(rows, MOBA_BLOCK)
    bfar = jnp.broadcast_to(bt[:, T5_BUCKETS - 1][:, None, None], (N_ATT_HEADS, t_dec, 1)).reshape(rows, 1)
    tk = np.arange(8)[None, :]
    rel_own = np.arange(t_dec)[:, None] - tk
    ok_own = (rel_own >= 0) & (tk < t_dec)
    bown = jnp.where(jnp.asarray(ok_own)[None], bt[:, _t5_bucket_np(rel_own)], NEG_INF).reshape(rows, 8)

    def page_spec(k):
        return pl.BlockSpec((None, kv_w, PAGE_SIZE), lambda b, s, pt, k=k: (pt[b, s * npg + k], layer, 0))

    const2 = lambda b, s, pt: (0, 0)
    grid_spec = pltpu.PrefetchScalarGridSpec(
        num_scalar_prefetch=1,
        grid=(n_dec, n_blocks // nb),
        in_specs=[pl.BlockSpec((None, rows, kv_w), lambda b, s, pt: (b, 0, 0)),
                  pl.BlockSpec((None, 8, kv_w), lambda b, s, pt: (b, 0, 0)),
                  pl.BlockSpec((None, 8, kv_w), lambda b, s, pt: (b, 0, 0)),
                  pl.BlockSpec((rows, MOBA_BLOCK), const2),
                  pl.BlockSpec((rows, 1), const2),
                  pl.BlockSpec((rows, 8), const2)]
                 + [page_spec(k) for k in range(npg)] + [page_spec(k) for k in range(npg)],
        out_specs=pl.BlockSpec((None, rows, kv_w), lambda b, s, pt: (b, 0, 0)),
        scratch_shapes=[pltpu.VMEM((rows, LANES), F32),
                        pltpu.VMEM((rows, LANES), F32),
                        pltpu.VMEM((rows, LANES), F32),
                        pltpu.VMEM((n_blocks, rows, kv_w), F32)],
    )
    out = pl.pallas_call(
        functools.partial(_moba_sample_kernel, n_blocks=n_blocks, nb=nb),
        grid_spec=grid_spec,
        out_shape=jax.ShapeDtypeStruct((n_dec, rows, kv_w), F32),
        compiler_params=_cparams(("parallel", "arbitrary")),
        name="moba_sample",
    )(page_table, qbd, kn, vn, b31, bfar, bown, *([cache_k] * npg), *([cache_v] * npg))
    out = out.reshape(n_dec, N_ATT_HEADS, t_dec, KV_HEADS_A, HEAD_DIM)
    out = out[:, np.arange(N_ATT_HEADS), :, head_kv, :]
    return jnp.transpose(out, (1, 2, 0, 3)).reshape(n_dec, t_dec, N_ATT_HEADS * HEAD_DIM)


SWA_GROUP = N_ATT_HEADS // KV_HEADS_C


def _sink_softmax_pv(parts, sink):
    m = sink
    for s, _ in parts:
        m = jnp.maximum(m, jnp.max(s, axis=1, keepdims=True))
    den = jnp.exp(sink - m)
    ps = []
    for s, _ in parts:
        p = jnp.exp(s - m)
        den = den + jnp.sum(p, axis=1, keepdims=True)
        ps.append(p)
    o = None
    for p, (_, v) in zip(ps, parts):
        t = _dot(p / den, v)
        o = t if o is None else o + t
    return o


def _swa_prompt_kernel(q_ref, kp_ref, ko_ref, vp_ref, vo_ref, bias_ref, sink_ref, o_ref):
    i = pl.program_id(1)
    w, hd, g = WINDOW, HEAD_DIM, SWA_GROUP
    q = q_ref[...] * (hd ** -0.5)
    kcat = jnp.concatenate([kp_ref[...], ko_ref[...]], axis=0)
    vt = jnp.concatenate([vp_ref[...], vo_ref[...]], axis=0).T.astype(BF16)
    row = lax.broadcasted_iota(jnp.int32, (2 * w, g * w), 0)
    has_key = (row >= w) | (i > 0)
    ots = []
    for c in range(KV_HEADS_C):
        qs = jnp.concatenate([q[:, (c * g + j) * hd:(c * g + j + 1) * hd] for j in range(g)], axis=0)
        s = _dot_nt(kcat[:, c * hd:(c + 1) * hd], qs) + bias_ref[c]
        s = jnp.where(has_key, s, NEG_INF)
        sink = sink_ref[c]
        m = jnp.maximum(sink, jnp.max(s, axis=0, keepdims=True))
        p = jnp.exp(s - m)
        den = jnp.sum(p, axis=0, keepdims=True) + jnp.exp(sink - m)
        ots.append(jnp.dot(vt[c * hd:(c + 1) * hd, :], (p / den).astype(BF16), preferred_element_type=F32))
    o = jnp.concatenate(ots, axis=0).T
    for c in range(KV_HEADS_C):
        for j in range(g):
            o_ref[:, (c * g + j) * hd:(c * g + j + 1) * hd] = o[j * w:(j + 1) * w, c * hd:(c + 1) * hd]


def _swa_prompt(proj, bias_rel, sinks, n_batch, seq, q_col, k_col, v_col):
    w, g = WINDOW, SWA_GROUP
    nb = seq // w
    band = _bias_tiles(bias_rel, 2 * w, w, w, sign=-1, lo=0, hi=w)
    band = jnp.transpose(band.reshape(KV_HEADS_C, g, 2 * w, w), (0, 2, 1, 3)).reshape(KV_HEADS_C, 2 * w, g * w)
    sink = jnp.broadcast_to(sinks.reshape(KV_HEADS_C, 1, g, 1), (KV_HEADS_C, 1, g, w)).reshape(KV_HEADS_C, 1, g * w)
    own = lambda c: (lambda b, i: (b * nb + i, c))
    prev = lambda c: (lambda b, i: (b * nb + jnp.maximum(i - 1, 0), c))
    return pl.pallas_call(
        _swa_prompt_kernel,
        grid=(n_batch, nb),
        in_specs=[pl.BlockSpec((w, 4 * LANES), lambda b, i: (b * nb + i, q_col // 4)),
                  pl.BlockSpec((w, LANES), prev(k_col)),
                  pl.BlockSpec((w, LANES), own(k_col)),
                  pl.BlockSpec((w, LANES), prev(v_col)),
                  pl.BlockSpec((w, LANES), own(v_col)),
                  pl.BlockSpec((KV_HEADS_C, 2 * w, g * w), lambda b, i: (0, 0, 0)),
                  pl.BlockSpec((KV_HEADS_C, 1, g * w), lambda b, i: (0, 0, 0))],
        out_specs=pl.BlockSpec((w, 4 * LANES), lambda b, i: (b * nb + i, 0)),
        out_shape=jax.ShapeDtypeStruct((n_batch * seq, 4 * LANES), F32),
        compiler_params=_cparams(("parallel", "parallel")),
        name="swa_prompt",
    )(proj, proj, proj, proj, proj, band, sink)


SWA_DEC_SEQS_PER_STEP = 8


def _swa_sample_kernel(q_ref, kb_ref, vb_ref, kn_ref, vn_ref, bbuf_ref, bnew_ref, sink_ref, o_ref):
    for i in range(SWA_DEC_SEQS_PER_STEP):
        q = q_ref[i]
        s_buf = _dot_nt(q, kb_ref[i]) + bbuf_ref[...]
        s_new = _dot_nt(q, kn_ref[i]) + bnew_ref[...]
        o_ref[i] = _sink_softmax_pv([(s_buf, vb_ref[i]), (s_new, vn_ref[i])], sink_ref[...])


def _swa_sample(q_s, k_new, v_new, buf_k, buf_v, bias_rel, sinks):
    n_dec, t_dec = q_s.shape[:2]
    nbuf = buf_k.shape[1]
    rows = N_ATT_HEADS * t_dec
    kv_w = KV_HEADS_C * HEAD_DIM
    sb = SWA_DEC_SEQS_PER_STEP
    assert n_dec % sb == 0 and t_dec <= 8
    head_kv = np.arange(N_ATT_HEADS) // SWA_GROUP
    qh = jnp.transpose(q_s, (0, 2, 1, 3)) * (HEAD_DIM ** -0.5)
    onehot = jnp.asarray(np.eye(KV_HEADS_C, dtype=np.float32)[head_kv])
    qbd = (qh[:, :, :, None, :] * onehot[None, :, None, :, None]).reshape(n_dec, rows, kv_w)
    pad = 8 - t_dec
    kn = jnp.pad(k_new, ((0, 0), (0, pad), (0, 0)))
    vn = jnp.pad(v_new, ((0, 0), (0, pad), (0, 0)))
    bt = bias_rel.T
    t = np.arange(t_dec)[:, None]
    rel_buf = nbuf + t - np.arange(nbuf)[None, :]
    ok_buf = (rel_buf >= 0) & (rel_buf <= WINDOW)
    bbuf = jnp.where(jnp.asarray(ok_buf)[None], bt[:, _t5_bucket_np(rel_buf)], NEG_INF).reshape(rows, nbuf)
    tk = np.arange(8)[None, :]
    rel_new = t - tk
    ok_new = (rel_new >= 0) & (tk < t_dec)
    bnew = jnp.where(jnp.asarray(ok_new)[None], bt[:, _t5_bucket_np(rel_new)], NEG_INF).reshape(rows, 8)
    sink = jnp.broadcast_to(sinks[:, None, None], (N_ATT_HEADS, t_dec, 1)).reshape(rows, 1)
    seq3 = lambda r, c: pl.BlockSpec((sb, r, c), lambda i: (i, 0, 0))
    const2 = lambda r, c: pl.BlockSpec((r, c), lambda i: (0, 0))
    out = pl.pallas_call(
        _swa_sample_kernel,
        grid=(n_dec // sb,),
        in_specs=[seq3(rows, kv_w), seq3(nbuf, kv_w), seq3(nbuf, kv_w), seq3(8, kv_w), seq3(8, kv_w),
                  const2(rows, nbuf), const2(rows, 8), const2(rows, 1)],
        out_specs=seq3(rows, kv_w),
        out_shape=jax.ShapeDtypeStruct((n_dec, rows, kv_w), F32),
        compiler_params=_cparams(("parallel",)),
        name="swa_sample",
    )(qbd, buf_k, buf_v, kn, vn, bbuf, bnew, sink)
    out = out.reshape(n_dec, N_ATT_HEADS, t_dec, KV_HEADS_C, HEAD_DIM)
    out = out[:, np.arange(N_ATT_HEADS), :, head_kv, :]
    return jnp.transpose(out, (1, 2, 0, 3)).reshape(n_dec, t_dec, N_ATT_HEADS * HEAD_DIM)


def _softplus(x):
    return jnp.maximum(x, 0.0) + jnp.log1p(jnp.exp(-jnp.abs(x)))


def _neg_expm1(y):
    return -jnp.tanh(0.5 * y) * (jnp.exp(y) + 1.0)


def _shifted_rows(prev8, cur, k):
    if k == 0:
        return cur
    ext = jnp.concatenate([prev8, cur], axis=0)
    return pltpu.roll(ext, k, axis=0)[8:]


def _causal_conv_rows(prev8, cur, w):
    y = cur * w[CONV_WIDTH - 1:CONV_WIDTH, :]
    for k in range(1, CONV_WIDTH):
        y = y + _shifted_rows(prev8, cur, k) * w[CONV_WIDTH - 1 - k:CONV_WIDTH - k, :]
    return y


def _lru_gates(xc, wa, ba, wx, bx, sp):
    r = jax.nn.sigmoid(_dot(xc, wa) + ba)
    i = jax.nn.sigmoid(_dot(xc, wx) + bx)
    log_a = -LRU_C * r * sp
    a = jnp.exp(log_a)
    b = jnp.sqrt(_neg_expm1(2.0 * log_a)) * (i * xc)
    return a, b


LRU_ROWS = 256


def _lru_prompt_kernel(x_ref, gr_ref, cw_ref, cb_ref, wa_ref, ba_ref, wx_ref, bx_ref, lam_ref,
                       o_ref, hl_ref):
    n = x_ref.shape[0] // LRU_ROWS
    sp = _softplus(-lam_ref[...])
    row = lax.broadcasted_iota(jnp.int32, (LRU_ROWS, LANES), 0)

    def body(c, carry):
        prev8, h = carry
        r0 = pl.multiple_of(c * LRU_ROWS, LRU_ROWS)
        cur = x_ref[pl.ds(r0, LRU_ROWS), :]
        xc = _causal_conv_rows(prev8, cur, cw_ref[...]) + cb_ref[...]
        a, b = _lru_gates(xc, wa_ref[...], ba_ref[...], wx_ref[...], bx_ref[...], sp)
        d = 1
        while d < LRU_ROWS:
            a_s = jnp.where(row >= d, pltpu.roll(a, d, axis=0), 1.0)
            b_s = jnp.where(row >= d, pltpu.roll(b, d, axis=0), 0.0)
            b = a * b_s + b
            a = a * a_s
            d *= 2
        hs = a * h + b
        o_ref[pl.ds(r0, LRU_ROWS), :] = hs * jax.nn.gelu(gr_ref[pl.ds(r0, LRU_ROWS), :])
        return cur[LRU_ROWS - 8:], hs[LRU_ROWS - 1:]

    _, h = lax.fori_loop(0, n, body, (jnp.zeros((8, LANES), F32), jnp.zeros((1, LANES), F32)))
    hl_ref[...] = h


def _lru_prompt(proj, n_batch, seq, x_col, g_col, conv_w, conv_b, wa_bd, b_a, wx_bd, b_x, lam):
    ncb = D_RNN // LANES
    vec = lambda v: v.reshape(1, D_RNN)
    vspec = pl.BlockSpec((1, LANES), lambda b, c: (0, c))
    out, hl = pl.pallas_call(
        _lru_prompt_kernel,
        grid=(n_batch, ncb),
        in_specs=[pl.BlockSpec((seq, LANES), lambda b, c: (b, x_col + c)),
                  pl.BlockSpec((seq, LANES), lambda b, c: (b, g_col + c)),
                  pl.BlockSpec((CONV_WIDTH, LANES), lambda b, c: (0, c)),
                  vspec,
                  pl.BlockSpec((LANES, LANES), lambda b, c: (c, c)), vspec,
                  pl.BlockSpec((LANES, LANES), lambda b, c: (c, c)), vspec,
                  vspec],
        out_specs=[pl.BlockSpec((seq, LANES), lambda b, c: (b, c)),
                   pl.BlockSpec((None, 1, LANES), lambda b, c: (b, 0, c))],
        out_shape=[jax.ShapeDtypeStruct((n_batch * seq, D_RNN), F32),
                   jax.ShapeDtypeStruct((n_batch, 1, D_RNN), F32)],
        compiler_params=_cparams(("parallel", "parallel")),
        name="lru_prompt",
    )(proj, proj, conv_w, vec(conv_b), wa_bd, vec(b_a), wx_bd, vec(b_x), vec(lam))
    return out, hl.reshape(n_batch, D_RNN)


def _lru_sample_kernel(xp_ref, gr_ref, h0_ref, cw_ref, cb_ref, wa_ref, ba_ref, wx_ref, bx_ref, lam_ref,
                       o_ref, hl_ref):
    t_dec = gr_ref.shape[0]
    sp = _softplus(-lam_ref[...])
    h = h0_ref[...]
    for t in range(t_dec):
        xc = cb_ref[...]
        for j in range(CONV_WIDTH):
            xc = xc + xp_ref[t + j] * cw_ref[j:j + 1, :]
        a, b = _lru_gates(xc, wa_ref[...], ba_ref[...], wx_ref[...], bx_ref[...], sp)
        h = a * h + b
        o_ref[t] = h * jax.nn.gelu(gr_ref[t])
    hl_ref[...] = h


def _lru_sample(xr, gr, h0, conv0, conv_w, conv_b, wa_bd, b_a, wx_bd, b_x, lam):
    n_dec, t_dec, _ = xr.shape
    xp = jnp.transpose(jnp.concatenate([conv0, xr], axis=1), (1, 0, 2))
    grt = jnp.transpose(gr, (1, 0, 2))
    vec = lambda v: v.reshape(1, D_RNN)
    full = lambda *s: pl.BlockSpec(s, lambda i: (0,) * len(s))
    out, hl = pl.pallas_call(
        _lru_sample_kernel,
        grid=(1,),
        in_specs=[full(t_dec + CONV_WIDTH - 1, n_dec, D_RNN), full(t_dec, n_dec, D_RNN), full(n_dec, D_RNN),
                  full(CONV_WIDTH, D_RNN), full(1, D_RNN), full(D_RNN, D_RNN), full(1, D_RNN),
                  full(D_RNN, D_RNN), full(1, D_RNN), full(1, D_RNN)],
        out_specs=[full(t_dec, n_dec, D_RNN), full(n_dec, D_RNN)],
        out_shape=[jax.ShapeDtypeStruct((t_dec, n_dec, D_RNN), F32),
                   jax.ShapeDtypeStruct((n_dec, D_RNN), F32)],
        compiler_params=_cparams(("arbitrary",)),
        name="lru_sample",
    )(xp, grt, h0, conv_w, vec(conv_b), wa_bd, vec(b_a), wx_bd, vec(b_x), vec(lam))
    return jnp.transpose(out, (1, 0, 2)), hl


def _block_diag(w):
    n, k, _ = w.shape
    eye = jnp.asarray(np.eye(n, dtype=np.float32))
    return (w[:, :, None, :] * eye[:, None, :, None]).reshape(n * k, n * k)


GDN_ROWS = 256
GDN_INTRA_ROWS = 2048


def _bmm(a, b):
    return jnp.einsum('nij,njk->nik', a.astype(BF16), b.astype(BF16), preferred_element_type=F32)


def _bmm_nt(a, b):
    return jnp.einsum('nid,njd->nij', a.astype(BF16), b.astype(BF16), preferred_element_type=F32)


def _col(x, lane, idx):
    return jnp.sum(jnp.where(lane == idx, x, 0.0), axis=1, keepdims=True)


def _gdn_gates_kernel(x_ref, a_ref, dt_ref, beta_ref, gc_ref, *, chunk, pad_period, pad_rows):
    x = x_ref[...]
    tm = x.shape[0]
    beta = jax.nn.sigmoid(x)
    g = -jnp.exp(a_ref[...]) * _softplus(x + dt_ref[...])
    if pad_period:
        row = lax.broadcasted_iota(jnp.int32, x.shape, 0)
        valid = (row & (pad_period - 1)) >= pad_rows
        beta = jnp.where(valid, beta, 0.0)
        g = jnp.where(valid, g, 0.0)
    r = lax.broadcasted_iota(jnp.int32, (tm, tm), 0)
    c = lax.broadcasted_iota(jnp.int32, (tm, tm), 1)
    sh = chunk.bit_length() - 1
    tri = jnp.where(((r >> sh) == (c >> sh)) & (c <= r), 1.0, 0.0)
    beta_ref[...] = beta
    gc_ref[...] = jnp.dot(tri, g, preferred_element_type=F32, precision=lax.Precision.HIGHEST)


def _gdn_gates(x, col, rows, a_log, dt_bias, chunk, pad_period=0, pad_rows=0):
    tm = GDN_ROWS
    assert rows % tm == 0 and tm % chunk == 0
    a_pad = jnp.pad(a_log.reshape(1, -1), ((0, 0), (GDN_HEADS, LANES - 2 * GDN_HEADS)))
    dt_pad = jnp.pad(dt_bias.reshape(1, -1), ((0, 0), (GDN_HEADS, LANES - 2 * GDN_HEADS)))
    return pl.pallas_call(
        functools.partial(_gdn_gates_kernel, chunk=chunk, pad_period=pad_period, pad_rows=pad_rows),
        grid=(rows // tm,),
        in_specs=[pl.BlockSpec((tm, LANES), lambda i: (i, col)),
                  pl.BlockSpec((1, LANES), lambda i: (0, 0)),
                  pl.BlockSpec((1, LANES), lambda i: (0, 0))],
        out_specs=[pl.BlockSpec((tm, LANES), lambda i: (i, 0))] * 2,
        out_shape=[jax.ShapeDtypeStruct((rows, LANES), F32)] * 2,
        compiler_params=_cparams(("parallel",)),
        name="gdn_gates",
    )(x, a_pad, dt_pad)


def _gdn_prep_kernel(x_ref, cw_ref, beta_ref, gc_ref, *out_refs, mode):
    hd = HEAD_DIM
    n = x_ref.shape[0] // GDN_ROWS
    h0 = 2 * pl.program_id(1)
    lane = lax.broadcasted_iota(jnp.int32, (GDN_ROWS, LANES), 1)
    lo = lane < hd

    def halves(c0, c1):
        return jnp.where(lo, c0, c1)

    def put(ref, r0, val):
        ref[0, pl.ds(r0, GDN_ROWS), :] = val[:, :hd]
        ref[1, pl.ds(r0, GDN_ROWS), :] = val[:, hd:]

    def body(c, prev8):
        r0 = pl.multiple_of(c * GDN_ROWS, GDN_ROWS)
        cur = x_ref[pl.ds(r0, GDN_ROWS), :]
        y = _causal_conv_rows(prev8, cur, cw_ref[...])
        y = y * jax.nn.sigmoid(y)
        if mode in ("q", "k"):
            ss = y * y
            s0 = jnp.sum(jnp.where(lo, ss, 0.0), axis=1, keepdims=True)
            s1 = jnp.sum(jnp.where(lo, 0.0, ss), axis=1, keepdims=True)
            y = y * halves(lax.rsqrt(s0 + NORM_EPS), lax.rsqrt(s1 + NORM_EPS))
        if mode == "q":
            put(out_refs[0], r0, y * (hd ** -0.5))
        else:
            beta = beta_ref[pl.ds(r0, GDN_ROWS), :]
            bb = halves(_col(beta, lane, h0), _col(beta, lane, h0 + 1))
            if mode == "k":
                gc = gc_ref[pl.ds(r0, GDN_ROWS), :]
                put(out_refs[0], r0, y)
                put(out_refs[1], r0, y * bb)
                put(out_refs[2], r0, halves(_col(gc, lane, GDN_HEADS + h0), _col(gc, lane, GDN_HEADS + h0 + 1)))
            else:
                put(out_refs[0], r0, y * bb)
        return cur[GDN_ROWS - 8:]

    lax.fori_loop(0, n, body, jnp.zeros((8, LANES), F32))


def _gdn_prep(x, x_col, conv_w, beta, gc, n_batch, seq, mode):
    sec = {"q": 0, "k": 4, "v": 8}[mode]
    n_out = {"q": 1, "k": 3, "v": 1}[mode]
    hspec = pl.BlockSpec((2, seq, HEAD_DIM), lambda b, c: (b * 4 + c, 0, 0))
    gspec = pl.BlockSpec((seq, LANES), lambda b, c: (b, 0))
    outs = pl.pallas_call(
        functools.partial(_gdn_prep_kernel, mode=mode),
        grid=(n_batch, 4),
        in_specs=[pl.BlockSpec((seq, LANES), lambda b, c: (b, x_col + c)),
                  pl.BlockSpec((CONV_WIDTH, LANES), lambda b, c: (0, sec + c)),
                  gspec, gspec],
        out_specs=[hspec] * n_out,
        out_shape=[jax.ShapeDtypeStruct((n_batch * GDN_HEADS, seq, HEAD_DIM), F32)] * n_out,
        compiler_params=_cparams(("parallel", "parallel")),
        name="gdn_prep_" + mode,
    )(x, conv_w, beta, gc)
    return outs


def _gdn_intra_kernel(q_ref, k_ref, kb_ref, vb_ref, gcb_ref, gcr_ref,
                      u_ref, w_ref, qg_ref, kd_ref, qk_ref, gl_ref, *, chunk):
    hd = HEAD_DIM
    rows = q_ref.shape[0]
    n = rows // chunk
    r3 = lambda ref: ref[...].reshape(n, chunk, hd)
    q, k, kb, vb, gcb = r3(q_ref), r3(k_ref), r3(kb_ref), r3(vb_ref), r3(gcb_ref)
    gcr = gcr_ref[...]
    ci = lax.broadcasted_iota(jnp.int32, (n, chunk, chunk), 1)
    si = lax.broadcasted_iota(jnp.int32, (n, chunk, chunk), 2)
    decay = jnp.exp(jnp.where(ci >= si, gcb[:, :, :chunk] - gcr[:, None, :], NEG_INF))
    a = jnp.where(ci > si, _bmm_nt(kb, k) * decay, 0.0)
    b = -a
    y = b
    p = _bmm(b, b)
    stages = chunk.bit_length() - 2
    for t in range(stages):
        y = y + p + _bmm(y, p)
        if t < stages - 1:
            p = _bmm(p, p)
    eg = jnp.exp(gcb)
    rhs = jnp.concatenate([vb, kb * eg], axis=-1)
    sol = rhs + _bmm(y, rhs)
    gl = gcb[:, chunk - 1:chunk, :]
    u_ref[...] = sol[:, :, :hd].reshape(rows, hd)
    w_ref[...] = sol[:, :, hd:].reshape(rows, hd)
    qg_ref[...] = (q * eg).reshape(rows, hd)
    kd_ref[...] = (k * jnp.exp(gl - gcb)).reshape(rows, hd)
    qk_ref[...] = (_bmm_nt(q, k) * decay).reshape(rows, chunk)
    gl_ref[...] = jnp.exp(gl).reshape(n, hd)


def _gdn_intra(q, k, kb, vb, gcb, gc_row, chunk, rows_per_step):
    bh, seq, hd = q.shape
    tr = rows_per_step
    n = tr // chunk
    assert seq % tr == 0
    hspec = pl.BlockSpec((None, tr, hd), lambda h, i: (h, i, 0))
    shape = jax.ShapeDtypeStruct((bh, seq, hd), F32)
    return pl.pallas_call(
        functools.partial(_gdn_intra_kernel, chunk=chunk),
        grid=(bh, seq // tr),
        in_specs=[hspec] * 5 + [pl.BlockSpec((None, n, chunk), lambda h, i: (h, i, 0))],
        out_specs=[hspec] * 4 + [pl.BlockSpec((None, tr, chunk), lambda h, i: (h, i, 0)),
                                 pl.BlockSpec((None, n, hd), lambda h, i: (h, i, 0))],
        out_shape=[shape] * 4 + [jax.ShapeDtypeStruct((bh, seq, chunk), F32),
                                 jax.ShapeDtypeStruct((bh, seq // chunk, hd), F32)],
        compiler_params=_cparams(("parallel", "parallel")),
        name="gdn_intra",
    )(q, k, kb, vb, gcb, gc_row)


def _gdn_chunk_update(s, u, w, qg, kd, qk, gl):
    v_new = u - _bmm(w, s)
    o = _bmm(qg, s) + _bmm(qk, v_new)
    s = s * gl + jnp.einsum('ncd,nce->nde', kd.astype(BF16), v_new.astype(BF16), preferred_element_type=F32)
    return s, o


def _gdn_scan_kernel(u_ref, w_ref, qg_ref, kd_ref, qk_ref, gl_ref, o_ref, sfin_ref, s_scr, *, chunk):
    j = pl.program_id(1)

    @pl.when(j == 0)
    def _zero():
        s_scr[...] = jnp.zeros_like(s_scr)

    n = u_ref.shape[1] // chunk
    s = s_scr[...]
    for c in range(n):
        sl = slice(c * chunk, (c + 1) * chunk)
        s, o = _gdn_chunk_update(s, u_ref[:, sl, :], w_ref[:, sl, :], qg_ref[:, sl, :], kd_ref[:, sl, :],
                                 qk_ref[:, sl, :], gl_ref[:, c:c + 1, :])
        o_ref[:, sl, :] = o
    s_scr[...] = s

    @pl.when(j == pl.num_programs(1) - 1)
    def _final():
        sfin_ref[...] = s


def _gdn_scan(u, w, qg, kd, qk, gl, chunk, heads_per_step, rows_per_step):
    bh, seq, hd = u.shape
    hb, tr = heads_per_step, rows_per_step
    n = tr // chunk
    hspec = pl.BlockSpec((hb, tr, hd), lambda h, i: (h, i, 0))
    return pl.pallas_call(
        functools.partial(_gdn_scan_kernel, chunk=chunk),
        grid=(bh // hb, seq // tr),
        in_specs=[hspec] * 4 + [pl.BlockSpec((hb, tr, chunk), lambda h, i: (h, i, 0)),
                                pl.BlockSpec((hb, n, hd), lambda h, i: (h, i, 0))],
        out_specs=[hspec, pl.BlockSpec((hb, hd, hd), lambda h, i: (h, 0, 0))],
        out_shape=[jax.ShapeDtypeStruct((bh, seq, hd), F32), jax.ShapeDtypeStruct((bh, hd, hd), F32)],
        scratch_shapes=[pltpu.VMEM((hb, hd, hd), F32)],
        compiler_params=_cparams(("parallel", "arbitrary")),
        name="gdn_scan",
    )(u, w, qg, kd, qk, gl)


GDN_DEC_SEQS_PER_STEP = 16


def _gdn_step_kernel(s0_ref, u_ref, w_ref, qg_ref, kd_ref, qk_ref, gl_ref, o_ref, s_ref, *, chunk):
    hd = HEAD_DIM
    n = s0_ref.shape[0]
    r3 = lambda ref, last: ref[...].reshape(n, chunk, last)
    s, o = _gdn_chunk_update(s0_ref[...], r3(u_ref, hd), r3(w_ref, hd), r3(qg_ref, hd), r3(kd_ref, hd),
                             r3(qk_ref, chunk), gl_ref[...][:, None, :])
    o_ref[...] = o.reshape(n * chunk, hd)
    s_ref[...] = s


def _gdn_step(s0, u, w, qg, kd, qk, gl, chunk):
    n_dec, nh, hd, _ = s0.shape
    sb = GDN_DEC_SEQS_PER_STEP
    assert n_dec % sb == 0
    hspec = pl.BlockSpec((None, sb * chunk, hd), lambda h, i: (h, i, 0))
    sspec = pl.BlockSpec((sb, None, hd, hd), lambda h, i: (i, h, 0, 0))
    return pl.pallas_call(
        functools.partial(_gdn_step_kernel, chunk=chunk),
        grid=(nh, n_dec // sb),
        in_specs=[sspec] + [hspec] * 4 + [pl.BlockSpec((None, sb * chunk, chunk), lambda h, i: (h, i, 0)),
                                         pl.BlockSpec((None, sb, hd), lambda h, i: (h, i, 0))],
        out_specs=[hspec, sspec],
        out_shape=[jax.ShapeDtypeStruct((nh, n_dec * chunk, hd), F32),
                   jax.ShapeDtypeStruct((n_dec, nh, hd, hd), F32)],
        compiler_params=_cparams(("parallel", "parallel")),
        name="gdn_step",
    )(s0, u, w, qg, kd, qk, gl)


def _gdn_post_kernel(o_ref, z_ref, w_ref, out_ref):
    hd = HEAD_DIM
    for h in range(GDN_HEADS):
        o = o_ref[h]
        r = o * lax.rsqrt(jnp.mean(o * o, axis=-1, keepdims=True) + NORM_EPS) * w_ref[...]
        z = z_ref[:, h * hd:(h + 1) * hd]
        out_ref[:, h * hd:(h + 1) * hd] = r * (z * jax.nn.sigmoid(z))


def _gdn_post(o, z, z_col, norm_w, n_batch, seq, tm):
    nt = seq // tm
    return pl.pallas_call(
        _gdn_post_kernel,
        grid=(n_batch, nt),
        in_specs=[pl.BlockSpec((GDN_HEADS, tm, HEAD_DIM), lambda b, i: (b, i, 0)),
                  pl.BlockSpec((tm, 4 * LANES), lambda b, i: (b * nt + i, z_col)),
                  pl.BlockSpec((1, HEAD_DIM), lambda b, i: (0, 0))],
        out_specs=pl.BlockSpec((tm, 4 * LANES), lambda b, i: (b * nt + i, 0)),
        out_shape=jax.ShapeDtypeStruct((n_batch * seq, 4 * LANES), F32),
        compiler_params=_cparams(("parallel", "parallel")),
        name="gdn_post",
    )(o, z, norm_w.reshape(1, HEAD_DIM))


def _gdn_front(x, qkv_col, gate_col, conv_w, a_log, dt_bias, n_batch, seq, chunk, pad_period=0, pad_rows=0):
    rows = n_batch * seq
    beta, gc = _gdn_gates(x, gate_col, rows, a_log, dt_bias, chunk, pad_period, pad_rows)
    (q,) = _gdn_prep(x, qkv_col, conv_w, beta, gc, n_batch, seq, "q")
    k, kb, gcb = _gdn_prep(x, qkv_col + 4, conv_w, beta, gc, n_batch, seq, "k")
    (vb,) = _gdn_prep(x, qkv_col + 8, conv_w, beta, gc, n_batch, seq, "v")
    gc_row = jnp.transpose(gc[:, GDN_HEADS:2 * GDN_HEADS].reshape(n_batch, seq, GDN_HEADS), (0, 2, 1))
    gc_row = gc_row.reshape(n_batch * GDN_HEADS, seq // chunk, chunk)
    return _gdn_intra(q, k, kb, vb, gcb, gc_row, chunk, min(seq, GDN_INTRA_ROWS))


def _x_l2norm(x):
    return x * lax.rsqrt(jnp.sum(x * x, -1, keepdims=True) + NORM_EPS)


def _x_rms(x, w):
    return x * lax.rsqrt(jnp.mean(x * x, -1, keepdims=True) + NORM_EPS) * w


def _x_t5_bucket(rel):
    n = jnp.maximum(rel, 0)
    max_exact = T5_BUCKETS // 2
    nf = jnp.maximum(n, 1).astype(F32)
    large = max_exact + (jnp.log(nf / max_exact) / math.log(T5_MAX_DIST / max_exact)
                         * (T5_BUCKETS - max_exact)).astype(jnp.int32)
    large = jnp.minimum(large, T5_BUCKETS - 1)
    return jnp.where(n < max_exact, n, large)


def _x_causal_conv(x, buf, w, b=None):
    L = x.shape[1]
    xp = jnp.concatenate([buf.astype(x.dtype), x], axis=1)
    y = xp[:, CONV_WIDTH - 1:] * w[CONV_WIDTH - 1]
    for j in range(CONV_WIDTH - 1):
        y = y + xp[:, j:j + L] * w[j]
    if b is not None:
        y = y + b
    return y, xp[:, L:]


def _x_query_chunk(n_q, batch):
    cap = max(1, min(n_q, 128 // batch))
    return max(d for d in range(1, cap + 1) if n_q % d == 0)


def _x_moba_attention(q, k, v, q0, bias_t):
    B, Lq, Hq, hd = q.shape
    Lk, KV = k.shape[1], k.shape[2]
    G = Hq // KV
    n_full = Lk // MOBA_BLOCK
    n_sel = min(MOBA_TOPK, n_full)
    head_kv = jnp.arange(Hq) // G
    kmean = k[:, :n_full * MOBA_BLOCK].reshape(B, n_full, MOBA_BLOCK, KV, hd).mean(2, dtype=F32)[:, :, head_kv]
    qc = _x_query_chunk(Lq, B)
    nq = Lq // qc
    q_chunks = q.reshape(B, nq, qc, Hq, hd).swapaxes(0, 1)
    starts = q0 + qc * jnp.arange(nq, dtype=jnp.int32)
    b_ix = jnp.arange(B)[:, None, None, None, None]
    h_ix = jnp.arange(Hq)[None, :, None, None, None]
    kv_ix = head_kv[None, :, None, None, None]
    offs = jnp.arange(MOBA_BLOCK)
    scale = hd ** -0.5

    def one_chunk(args):
        qb, s0 = args
        q_pos = s0 + jnp.arange(qc)
        own = jnp.broadcast_to((q_pos // MOBA_BLOCK)[None, None, :, None], (B, Hq, qc, 1))
        gate = jnp.einsum('bqhd,bnhd->bhqn', qb, kmean, preferred_element_type=F32)
        gate = jnp.where(jnp.arange(n_full) < own, gate, -jnp.inf)
        _, sel = lax.top_k(gate, n_sel)
        blocks = jnp.concatenate([sel, own], -1)
        live = jnp.concatenate([sel < own, jnp.ones_like(own, dtype=bool)], -1)
        k_pos = blocks[..., None] * MOBA_BLOCK + offs
        rows = jnp.minimum(k_pos, Lk - 1)
        kg = k[b_ix, rows, kv_ix]
        vg = v[b_ix, rows, kv_ix]
        rel = q_pos[None, None, :, None, None] - k_pos
        s = jnp.einsum('bqhd,bhqsrd->bhqsr', qb, kg, preferred_element_type=F32) * scale
        s = s + bias_t[h_ix, _x_t5_bucket(rel)].astype(F32)
        s = jnp.where(live[..., None] & (rel >= 0), s, -jnp.inf)
        n_keys = s.shape[3] * MOBA_BLOCK
        p = jax.nn.softmax(s.reshape(B, Hq, qc, n_keys), axis=-1).reshape(s.shape)
        o = jnp.einsum('bhqsr,bhqsrd->bqhd', p, vg, preferred_element_type=F32)
        return o.astype(q.dtype)

    out = lax.map(one_chunk, (q_chunks, starts))
    return out.swapaxes(0, 1).reshape(B, Lq, Hq, hd)


def _x_gated_delta_rule(q, k, v, g, beta, s0):
    B, L, H, dk = q.shape
    dv = v.shape[-1]
    C = min(GDN_CHUNK, L)
    pad = (-L) % C
    N = (L + pad) // C

    def chunks(t):
        t = jnp.pad(t.astype(F32), [(0, 0), (0, pad)] + [(0, 0)] * (t.ndim - 2))
        t = t.reshape((B, N, C) + t.shape[2:])
        return jnp.swapaxes(jnp.swapaxes(t, 0, 1), 2, 3)

    q = chunks(q) * dk ** -0.5
    k = chunks(k)
    v = chunks(v)
    g = chunks(g)
    beta = chunks(beta)
    gc = jnp.cumsum(g, -1)
    incl = jnp.tril(jnp.ones((C, C), bool))
    strict = jnp.tril(jnp.ones((C, C), bool), -1)
    decay = jnp.exp(jnp.where(incl, gc[..., :, None] - gc[..., None, :], -jnp.inf))
    kb = k * beta[..., None]
    lower = jnp.where(strict, jnp.einsum('nbhcd,nbhsd->nbhcs', kb, k) * decay, 0.0)
    t_mat = lower + jnp.eye(C, dtype=F32)
    rhs = jnp.concatenate([v * beta[..., None], kb * jnp.exp(gc)[..., None]], -1)
    sol = lax.linalg.triangular_solve(t_mat, rhs, left_side=True, lower=True, unit_diagonal=True)
    u, w = sol[..., :dv], sol[..., dv:]
    qk = jnp.einsum('nbhcd,nbhsd->nbhcs', q, k) * decay
    qg = q * jnp.exp(gc)[..., None]
    kd = k * jnp.exp(gc[..., -1:] - gc)[..., None]
    glast = jnp.exp(gc[..., -1])[..., None, None]

    def step(S, xs):
        u_n, w_n, qk_n, qg_n, kd_n, gl_n = xs
        v_new = u_n - jnp.einsum('bhcd,bhde->bhce', w_n, S)
        o_n = jnp.einsum('bhcd,bhde->bhce', qg_n, S) + jnp.einsum('bhcs,bhse->bhce', qk_n, v_new)
        S = S * gl_n + jnp.einsum('bhcd,bhce->bhde', kd_n, v_new)
        return S, o_n

    S, o = lax.scan(step, s0.astype(F32), (u, w, qk, qg, kd, glast))
    o = jnp.swapaxes(jnp.swapaxes(o, 2, 3), 0, 1).reshape(B, N * C, H, dv)[:, :L]
    return o, S


def _x_sink_attention(q, k, v, q_pos, k_pos, sinks, bias_t):
    *lead, Lq, Hq, hd = q.shape
    KV = k.shape[-2]
    G = Hq // KV
    qg = q.reshape(*lead, Lq, KV, G, hd)
    s = jnp.einsum('...qkgd,...skd->...kgqs', qg, k, preferred_element_type=F32) * hd ** -0.5
    rel = q_pos[..., :, None] - k_pos[..., None, :]
    ok = (rel >= 0) & (rel <= WINDOW) & (k_pos[..., None, :] >= 0)
    bias = jnp.moveaxis(bias_t.reshape(KV, G, T5_BUCKETS)[:, :, _x_t5_bucket(rel)], (0, 1), (-4, -3))
    s = jnp.where(ok[..., None, None, :, :], s + bias.astype(F32), -jnp.inf)
    sink = sinks.astype(F32).reshape(KV, G)[:, :, None, None]
    m = jnp.maximum(s.max(-1, keepdims=True), sink)
    p = jnp.exp(s - m)
    den = p.sum(-1, keepdims=True) + jnp.exp(sink - m)
    o = jnp.einsum('...kgqs,...skd->...qkgd', p / den, v, preferred_element_type=F32)
    return o.reshape(*lead, Lq, Hq, hd).astype(q.dtype)


def _x_swa_prompt(q, k, v, sinks, bias_t):
    B, L, Hq, hd = q.shape
    W = WINDOW
    nb = L // W

    def band(t):
        tb = t.reshape(B, nb, W, t.shape[2], hd)
        prev = jnp.concatenate([jnp.zeros_like(tb[:, :1]), tb[:, :-1]], axis=1)
        return jnp.concatenate([prev, tb], axis=2)

    base = W * jnp.arange(nb)[:, None]
    q_pos = base + jnp.arange(W)
    k_pos = base - W + jnp.arange(2 * W)
    qb = q.reshape(B, nb, W, Hq, hd)
    o = _x_sink_attention(qb, band(k), band(v), q_pos, k_pos, sinks, bias_t)
    return o.reshape(B, nb * W, Hq, hd)


def _x_lru_combine(x, y):
    a1, b1 = x
    a2, b2 = y
    return a1 * a2, a2 * b1 + b2


def _x_rg_lru(x, h0, w_a, b_a, w_x, b_x, lam):
    B, L, _ = x.shape
    xf = x.astype(F32)
    xb = xf.reshape(B, L, LRU_BLOCKS, D_RNN // LRU_BLOCKS)
    r = jax.nn.sigmoid(jnp.einsum('blni,nij->blnj', xb, w_a).reshape(B, L, D_RNN) + b_a)
    i = jax.nn.sigmoid(jnp.einsum('blni,nij->blnj', xb, w_x).reshape(B, L, D_RNN) + b_x)
    log_a = -LRU_C * r * jax.nn.softplus(-lam.astype(F32))
    a = jnp.exp(log_a)
    b = jnp.sqrt(-jnp.expm1(2.0 * log_a)) * (i * xf)
    b = b.at[:, 0].add(a[:, 0] * h0.astype(F32))
    _, h = lax.associative_scan(_x_lru_combine, (a, b), axis=1)
    return h


def _x_even_rest(p, b, l, past_k, past_v, s0, conv0, conv_w, a_log, dt_bias, norm_w, bias_t, o_a=None):
    qa, ka, va, qkv, z, beta_in, decay_in = p
    if o_a is None:
        qa4 = qa.reshape(b, l, N_ATT_HEADS, HEAD_DIM)
        ka4 = ka.reshape(b, l, KV_HEADS_A, HEAD_DIM)
        va4 = va.reshape(b, l, KV_HEADS_A, HEAD_DIM)
        past = past_k.shape[1]
        k_all = jnp.concatenate([past_k, ka4], axis=1)
        v_all = jnp.concatenate([past_v, va4], axis=1)
        o_a = _x_moba_attention(qa4, k_all, v_all, past, bias_t).reshape(b, l, -1)
    c, conv_new = _x_causal_conv(qkv, conv0, conv_w)
    c = jax.nn.silu(c)
    qb, kb, vb = jnp.split(c, [512, 1024], axis=-1)
    qb = _x_l2norm(qb.reshape(b, l, GDN_HEADS, 64))
    kb = _x_l2norm(kb.reshape(b, l, GDN_HEADS, 64))
    vb = vb.reshape(b, l, GDN_HEADS, 64)
    beta = jax.nn.sigmoid(beta_in)
    g = -jnp.exp(a_log) * jax.nn.softplus(decay_in + dt_bias)
    o_b, s_new = _x_gated_delta_rule(qb, kb, vb, g, beta, s0)
    gate = jax.nn.silu(z.reshape(b, l, GDN_HEADS, 64))
    o_b = _x_rms(o_b, norm_w) * gate
    mixed = jnp.concatenate([o_a, o_b.reshape(b, l, -1)], axis=-1)
    return mixed, s_new, conv_new


def _x_odd_rest(p, b, l, buf_k, buf_v, h0, conv0, past_len, sinks, conv_w, conv_b,
                w_a, b_a, w_x, b_x, lam, bias_t):
    qc, kc, vc, xr, gr = p
    qc = qc.reshape(b, l, N_ATT_HEADS, HEAD_DIM)
    kc = kc.reshape(b, l, KV_HEADS_C, HEAD_DIM)
    vc = vc.reshape(b, l, KV_HEADS_C, HEAD_DIM)
    if buf_k is None:
        o_c = _x_swa_prompt(qc, kc, vc, sinks, bias_t)
        k_keep, v_keep = kc[:, -WINDOW:], vc[:, -WINDOW:]
    else:
        nbuf = buf_k.shape[1]
        k_all = jnp.concatenate([buf_k, kc], axis=1)
        v_all = jnp.concatenate([buf_v, vc], axis=1)
        q_pos = past_len + jnp.arange(l)
        k_pos = past_len - nbuf + jnp.arange(nbuf + l)
        o_c = _x_sink_attention(qc, k_all, v_all, q_pos, k_pos, sinks, bias_t)
        k_keep, v_keep = k_all[:, -nbuf:], v_all[:, -nbuf:]
    xc, conv_new = _x_causal_conv(xr, conv0, conv_w, conv_b)
    h = _x_rg_lru(xc, h0, w_a, b_a, w_x, b_x, lam)
    o_d = h * jax.nn.gelu(gr)
    mixed = jnp.concatenate([o_c.reshape(b, l, -1), o_d], axis=-1)
    return mixed, k_keep, v_keep, h[:, -1], conv_new


TM = 768


def _pad_cols(w, n):
    return jnp.pad(w, ((0, 0), (0, n - w.shape[1])))


def _x_kernel_fallback(x_prompt, x_sample, cache_k_moba, cache_v_moba, state_gdn, state_gdn_conv, cache_k_swa, cache_v_swa, state_lru, state_lru_conv, page_table, w_in_even, w_out_even, gdn_conv_w, gdn_a_log, gdn_dt_bias, gdn_norm_w, w_in_odd, w_out_odd, swa_sinks, lru_conv_w, lru_conv_b, lru_w_a, lru_b_a, lru_w_x, lru_b_x, lru_lambda, rel_bias, ln_g, ln_b, router_w, router_b, w_gate, w_up, w_down):
    n_pr, seq, d = x_prompt.shape
    n_dec, dec_seq, _ = x_sample.shape
    tp = n_pr * seq
    ts = n_dec * dec_seq
    past_len = page_table.shape[1] * PAGE_SIZE
    bias_t = rel_bias.T
    h = jnp.concatenate([x_prompt.reshape(tp, d), x_sample.reshape(ts, d)], axis=0)

    rw_pad = _pad_cols(router_w, LANES)
    rb_pad = _pad_cols(router_b.reshape(1, -1), LANES)
    wg_b, wu_b, wd_b = w_gate.astype(BF16), w_up.astype(BF16), w_down.astype(BF16)

    p_even = sum(EVEN_SIZES)
    n_even = 3200
    w0 = _pad_cols(w_in_even[0], n_even).astype(BF16)
    proj = _mm(h, w0, TM, 640)
    splits = np.cumsum(EVEN_SIZES)[:-1].tolist()
    pp = [t.reshape(n_pr, seq, -1) for t in jnp.split(proj[:tp, :p_even], splits, axis=-1)]
    ps = [t.reshape(n_dec, dec_seq, -1) for t in jnp.split(proj[tp:, :p_even], splits, axis=-1)]
    moba_k_prompt = pp[1].reshape(n_pr, seq, 1, KV_HEADS_A, HEAD_DIM)
    moba_v_prompt = pp[2].reshape(n_pr, seq, 1, KV_HEADS_A, HEAD_DIM)
    moba_k_sample = ps[1].reshape(n_dec, dec_seq, 1, KV_HEADS_A, HEAD_DIM)
    moba_v_sample = ps[2].reshape(n_dec, dec_seq, 1, KV_HEADS_A, HEAD_DIM)

    o_a_p = _moba_prompt(proj, rel_bias, n_pr, seq, 0, 4, 6).reshape(n_pr, seq, -1)
    mixed_p, gdn_sp, gdn_cp = _x_even_rest(
        pp, n_pr, seq, None, None, jnp.zeros((n_pr, GDN_HEADS, 64, 64), F32),
        jnp.zeros((n_pr, CONV_WIDTH - 1, GDN_CONV_CH), F32),
        gdn_conv_w[0], gdn_a_log[0], gdn_dt_bias[0], gdn_norm_w[0], bias_t, o_a=o_a_p)
    past_k = cache_k_moba[page_table][:, :, :, 0].reshape(n_dec, past_len, KV_HEADS_A, HEAD_DIM)
    past_v = cache_v_moba[page_table][:, :, :, 0].reshape(n_dec, past_len, KV_HEADS_A, HEAD_DIM)
    mixed_s, gdn_ss, gdn_cs = _x_even_rest(
        ps, n_dec, dec_seq, past_k, past_v, state_gdn[0], state_gdn_conv[0],
        gdn_conv_w[0], gdn_a_log[0], gdn_dt_bias[0], gdn_norm_w[0], bias_t)
    mixed = jnp.concatenate([mixed_p.reshape(tp, -1), mixed_s.reshape(ts, -1)], axis=0)
    h = _mm_ln(mixed, w_out_even[0].astype(BF16), h, ln_g[0, 0], ln_b[0, 0], TM)
    h = _moe_ln(h, rw_pad, rb_pad, wg_b[0], wu_b[0], wd_b[0], ln_g[0, 1], ln_b[0, 1], TM)

    p_odd = sum(ODD_SIZES)
    proj = _mm(h, w_in_odd[0].astype(BF16), TM, 896)
    splits = np.cumsum(ODD_SIZES)[:-1].tolist()
    pp = [t.reshape(n_pr, seq, -1) for t in jnp.split(proj[:tp], splits, axis=-1)]
    ps = [t.reshape(n_dec, dec_seq, -1) for t in jnp.split(proj[tp:], splits, axis=-1)]
    mixed_p, swa_kp, swa_vp, lru_p, lru_cp = _x_odd_rest(
        pp, n_pr, seq, None, None, jnp.zeros((n_pr, D_RNN), F32),
        jnp.zeros((n_pr, CONV_WIDTH - 1, D_RNN), F32), 0,
        swa_sinks[0], lru_conv_w[0], lru_conv_b[0], lru_w_a[0], lru_b_a[0], lru_w_x[0], lru_b_x[0],
        lru_lambda[0], bias_t)
    mixed_s, swa_ks, swa_vs, lru_s, lru_cs = _x_odd_rest(
        ps, n_dec, dec_seq, cache_k_swa[0], cache_v_swa[0], state_lru[0], state_lru_conv[0], past_len,
        swa_sinks[0], lru_conv_w[0], lru_conv_b[0], lru_w_a[0], lru_b_a[0], lru_w_x[0], lru_b_x[0],
        lru_lambda[0], bias_t)
    mixed = jnp.concatenate([mixed_p.reshape(tp, -1), mixed_s.reshape(ts, -1)], axis=0)
    h = _mm_ln(mixed, w_out_odd[0].astype(BF16), h, ln_g[1, 0], ln_b[1, 0], TM)
    h = _moe_ln(h, rw_pad, rb_pad, wg_b[1], wu_b[1], wd_b[1], ln_g[1, 1], ln_b[1, 1], TM)

    y_prompt = h[:tp].reshape(n_pr, seq, d)
    y_sample = h[tp:].reshape(n_dec, dec_seq, d)
    return (y_prompt, y_sample, moba_k_prompt, moba_v_prompt, moba_k_sample, moba_v_sample,
            gdn_sp[None], gdn_ss[None], gdn_cp[None], gdn_cs[None],
            swa_kp[None], swa_vp[None], swa_ks[None], swa_vs[None],
            lru_p[None], lru_s[None], lru_cp[None], lru_cs[None])


GDN_DEC_PERIOD = 8
EVEN_COLS = 3200


def kernel(x_prompt, x_sample, cache_k_moba, cache_v_moba, state_gdn, state_gdn_conv, cache_k_swa, cache_v_swa, state_lru, state_lru_conv, page_table, w_in_even, w_out_even, gdn_conv_w, gdn_a_log, gdn_dt_bias, gdn_norm_w, w_in_odd, w_out_odd, swa_sinks, lru_conv_w, lru_conv_b, lru_w_a, lru_b_a, lru_w_x, lru_b_x, lru_lambda, rel_bias, ln_g, ln_b, router_w, router_b, w_gate, w_up, w_down):
    n_pr, seq, d = x_prompt.shape
    n_dec, dec_seq, _ = x_sample.shape
    tp = n_pr * seq
    ts = n_dec * dec_seq
    n_pool = cache_k_moba.shape[0]
    kva = KV_HEADS_A * HEAD_DIM
    kvc = KV_HEADS_C * HEAD_DIM
    h = jnp.concatenate([x_prompt.reshape(tp, d), x_sample.reshape(ts, d)], axis=0)

    rw_pad = _pad_cols(router_w, LANES).T
    rb_pad = router_b.reshape(N_EXPERTS, 1)
    experts = [_group_experts(w_gate[layer], w_up[layer], w_down[layer]) for layer in range(DEPTH)]

    proj = _mm(h, _pad_cols(w_in_even[0], EVEN_COLS).astype(BF16), TM, 640)
    ps = proj[tp:]
    moba_k_prompt = proj[:tp, 512:512 + kva].reshape(n_pr, seq, 1, KV_HEADS_A, HEAD_DIM)
    moba_v_prompt = proj[:tp, 768:768 + kva].reshape(n_pr, seq, 1, KV_HEADS_A, HEAD_DIM)
    k_new = ps[:, 512:512 + kva].reshape(n_dec, dec_seq, kva)
    v_new = ps[:, 768:768 + kva].reshape(n_dec, dec_seq, kva)
    moba_k_sample = k_new.reshape(n_dec, dec_seq, 1, KV_HEADS_A, HEAD_DIM)
    moba_v_sample = v_new.reshape(n_dec, dec_seq, 1, KV_HEADS_A, HEAD_DIM)

    oa_p = _moba_prompt(proj, rel_bias, n_pr, seq, 0, 4, 6)
    oa_s = _moba_sample(ps[:, :512].reshape(n_dec, dec_seq, N_ATT_HEADS, HEAD_DIM), k_new, v_new,
                        jnp.swapaxes(cache_k_moba.reshape(n_pool, PAGE_SIZE, -1), 1, 2),
                        jnp.swapaxes(cache_v_moba.reshape(n_pool, PAGE_SIZE, -1), 1, 2),
                        page_table, rel_bias, layer=0)

    parts = _gdn_front(proj, 8, 24, gdn_conv_w[0], gdn_a_log[0], gdn_dt_bias[0], n_pr, seq, GDN_CHUNK)
    o_p, s_fin = _gdn_scan(*parts, GDN_CHUNK, GDN_HEADS, 512)
    ob_p = _gdn_post(o_p, proj, 5, gdn_norm_w[0], n_pr, seq, 512)
    gdn_state_prompt = s_fin.reshape(1, n_pr, GDN_HEADS, HEAD_DIM, HEAD_DIM)
    gdn_conv_prompt = proj[:tp, 1024:1024 + GDN_CONV_CH].reshape(n_pr, seq, GDN_CONV_CH)[None, :, seq - 3:]

    period = GDN_DEC_PERIOD
    lead = period - dec_seq
    qkv_s = ps[:, 1024:1024 + GDN_CONV_CH].reshape(n_dec, dec_seq, GDN_CONV_CH)
    x8 = jnp.concatenate([jnp.zeros((n_dec, lead - 3, GDN_CONV_CH), F32), state_gdn_conv[0], qkv_s], axis=1)
    tail8 = jnp.pad(ps[:, 2560:EVEN_COLS].reshape(n_dec, dec_seq, EVEN_COLS - 2560), ((0, 0), (lead, 0), (0, 0)))
    xs = jnp.concatenate([x8, tail8], axis=-1).reshape(n_dec * period, EVEN_COLS - 1024)
    parts = _gdn_front(xs, 0, 16, gdn_conv_w[0], gdn_a_log[0], gdn_dt_bias[0], 1, n_dec * period, period,
                       period, lead)
    o_s, gdn_state_sample = _gdn_step(state_gdn[0], *parts, period)
    ob_s = _gdn_post(o_s, xs, 3, gdn_norm_w[0], 1, n_dec * period, 512)
    ob_s = ob_s.reshape(n_dec, period, 512)[:, lead:].reshape(ts, 512)
    gdn_conv_sample = qkv_s[None, :, dec_seq - 3:]

    oa = jnp.concatenate([oa_p, oa_s.reshape(ts, 512)], axis=0)
    ob = jnp.concatenate([ob_p, ob_s], axis=0)
    h = _mm_ln(oa, ob, w_out_even[0].astype(BF16), h, ln_g[0, 0], ln_b[0, 0], TM)
    h = _moe_ln(h, rw_pad, rb_pad, *experts[0], ln_g[0, 1], ln_b[0, 1], TM)

    proj = _mm(h, w_in_odd[0].astype(BF16), TM, 896)
    ps = proj[tp:]
    wa_bd, wx_bd = _block_diag(lru_w_a[0]), _block_diag(lru_w_x[0])
    oc_p = _swa_prompt(proj, rel_bias, swa_sinks[0], n_pr, seq, 0, 4, 5)
    od_p, lru_p = _lru_prompt(proj, n_pr, seq, 6, 10, lru_conv_w[0], lru_conv_b[0], wa_bd, lru_b_a[0],
                              wx_bd, lru_b_x[0], lru_lambda[0])
    swa_k_prompt = proj[:tp, 512:512 + kvc].reshape(n_pr, seq, KV_HEADS_C, HEAD_DIM)[None, :, seq - WINDOW:]
    swa_v_prompt = proj[:tp, 640:640 + kvc].reshape(n_pr, seq, KV_HEADS_C, HEAD_DIM)[None, :, seq - WINDOW:]
    lru_conv_prompt = proj[:tp, 768:768 + D_RNN].reshape(n_pr, seq, D_RNN)[None, :, seq - 3:]

    nbuf = cache_k_swa.shape[2]
    kc_s = ps[:, 512:512 + kvc].reshape(n_dec, dec_seq, kvc)
    vc_s = ps[:, 640:640 + kvc].reshape(n_dec, dec_seq, kvc)
    buf_k = cache_k_swa[0].reshape(n_dec, nbuf, kvc)
    buf_v = cache_v_swa[0].reshape(n_dec, nbuf, kvc)
    oc_s = _swa_sample(ps[:, :512].reshape(n_dec, dec_seq, N_ATT_HEADS, HEAD_DIM), kc_s, vc_s, buf_k, buf_v,
                       rel_bias, swa_sinks[0])
    swa_k_sample = jnp.concatenate([buf_k, kc_s], axis=1)[:, dec_seq:].reshape(1, n_dec, nbuf, KV_HEADS_C, HEAD_DIM)
    swa_v_sample = jnp.concatenate([buf_v, vc_s], axis=1)[:, dec_seq:].reshape(1, n_dec, nbuf, KV_HEADS_C, HEAD_DIM)
    xr_s = ps[:, 768:768 + D_RNN].reshape(n_dec, dec_seq, D_RNN)
    gr_s = ps[:, 1280:1280 + D_RNN].reshape(n_dec, dec_seq, D_RNN)
    od_s, lru_s = _lru_sample(xr_s, gr_s, state_lru[0], state_lru_conv[0], lru_conv_w[0], lru_conv_b[0],
                              wa_bd, lru_b_a[0], wx_bd, lru_b_x[0], lru_lambda[0])
    lru_conv_sample = jnp.concatenate([state_lru_conv[0], xr_s], axis=1)[None, :, dec_seq:]

    oc = jnp.concatenate([oc_p, oc_s.reshape(ts, 512)], axis=0)
    od = jnp.concatenate([od_p, od_s.reshape(ts, 512)], axis=0)
    h = _mm_ln(oc, od, w_out_odd[0].astype(BF16), h, ln_g[1, 0], ln_b[1, 0], TM)
    h = _moe_ln(h, rw_pad, rb_pad, *experts[1], ln_g[1, 1], ln_b[1, 1], TM)

    y_prompt = h[:tp].reshape(n_pr, seq, d)
    y_sample = h[tp:].reshape(n_dec, dec_seq, d)
    return (y_prompt, y_sample, moba_k_prompt, moba_v_prompt, moba_k_sample, moba_v_sample,
            gdn_state_prompt, gdn_state_sample[None], gdn_conv_prompt, gdn_conv_sample,
            swa_k_prompt, swa_v_prompt, swa_k_sample, swa_v_sample,
            lru_p[None], lru_s[None], lru_conv_prompt, lru_conv_sample)
```

```python
import functools
import math

import jax
import jax.numpy as jnp
import numpy as np
from jax import lax
from jax.experimental import pallas as pl
from jax.experimental.pallas import tpu as pltpu

F32 = jnp.float32
BF16 = jnp.bfloat16

D_MODEL = 1024
HEAD_DIM = 64
N_ATT_HEADS = 8
KV_HEADS_A = 4
KV_HEADS_C = 2
MOBA_BLOCK = 256
MOBA_TOPK = 3
PAGE_SIZE = 128
GDN_HEADS = 8
GDN_CHUNK = 64
GDN_CONV_CH = 1536
CONV_WIDTH = 4
WINDOW = 128
D_RNN = 512
LRU_BLOCKS = 8
LRU_C = 8.0
T5_BUCKETS = 32
T5_MAX_DIST = 128
N_EXPERTS = 16
N_GROUPS = 4
EXPERTS_PER_GROUP = 4
D_EXPERT = 256
DEPTH = 2
DEEPNORM_ALPHA = (2 * DEPTH) ** 0.25
LN_EPS = 1e-5
NORM_EPS = 1e-6
EVEN_SIZES = (512, 256, 256, 1536, 512, 8, 8)
ODD_SIZES = (512, 128, 128, 512, 512)

LANES = 128
VMEM_LIMIT = 56 * 1024 * 1024
NEG_INF = float("-inf")


def _cparams(sem):
    return pltpu.CompilerParams(dimension_semantics=sem, vmem_limit_bytes=VMEM_LIMIT)


def _dot(a, b):
    return jnp.dot(a.astype(BF16), b.astype(BF16), preferred_element_type=F32)


def _dot_nt(a, b):
    return lax.dot_general(a.astype(BF16), b.astype(BF16), (((1,), (1,)), ((), ())),
                           preferred_element_type=F32)


def _seg_specs(segs, tm):
    specs, bounds, lo = [], [], 0
    for a in segs:
        assert a.shape[0] % tm == 0
        nt = a.shape[0] // tm
        specs.append(pl.BlockSpec((tm, a.shape[1]), lambda i, lo=lo, nt=nt: (jnp.clip(i - lo, 0, nt - 1), 0)))
        lo += nt
        bounds.append(lo)
    return specs, bounds


def _seg_tile(i, refs, bounds):
    x = refs[-1][...]
    for r, hi in zip(reversed(refs[:-1]), reversed(bounds[:-1])):
        x = jnp.where(i < hi, r[...], x)
    return x


def _mm_kernel(*refs, tn, bounds):
    ns = len(bounds)
    w_ref, o_ref = refs[ns], refs[ns + 1]
    xb = _seg_tile(pl.program_id(0), refs[:ns], bounds).astype(BF16)
    for c in range(w_ref.shape[1] // tn):
        o_ref[:, c * tn:(c + 1) * tn] = jnp.dot(xb, w_ref[:, c * tn:(c + 1) * tn], preferred_element_type=F32)


def _mm(x_segs, w, tm, tn):
    k, n = w.shape
    assert n % tn == 0
    specs, bounds = _seg_specs(x_segs, tm)
    m = bounds[-1] * tm
    return pl.pallas_call(
        functools.partial(_mm_kernel, tn=tn, bounds=tuple(bounds)),
        grid=(bounds[-1],),
        in_specs=specs + [pl.BlockSpec((k, n), lambda i: (0, 0))],
        out_specs=pl.BlockSpec((tm, n), lambda i: (i, 0)),
        out_shape=jax.ShapeDtypeStruct((m, n), F32),
        compiler_params=_cparams(("parallel",)),
        name="in_proj",
    )(*x_segs, w)


def _layer_norm_rows(z, g, b):
    mu = jnp.mean(z, axis=-1, keepdims=True)
    zc = z - mu
    var = jnp.mean(zc * zc, axis=-1, keepdims=True)
    return zc * lax.rsqrt(var + LN_EPS) * g + b


def _mm_ln_kernel(*refs, bounds_a, bounds_b, bounds_h):
    na, nb, nh = len(bounds_a), len(bounds_b), len(bounds_h)
    xa_refs, xb_refs, h_refs = refs[:na], refs[na:na + nb], refs[na + nb:na + nb + nh]
    wa_ref, wb_ref, g_ref, b_ref, o_ref = refs[na + nb + nh:]
    i = pl.program_id(0)
    f = (jnp.dot(_seg_tile(i, xa_refs, bounds_a).astype(BF16), wa_ref[...], preferred_element_type=F32)
         + jnp.dot(_seg_tile(i, xb_refs, bounds_b).astype(BF16), wb_ref[...], preferred_element_type=F32))
    o_ref[...] = _layer_norm_rows(DEEPNORM_ALPHA * _seg_tile(i, h_refs, bounds_h) + f, g_ref[...], b_ref[...])


def _mm_ln(xa_segs, xb_segs, w, h_segs, g, b, tm):
    ka, kb = xa_segs[0].shape[1], xb_segs[0].shape[1]
    n = w.shape[1]
    assert ka == kb
    sa, ba = _seg_specs(xa_segs, tm)
    sb, bb = _seg_specs(xb_segs, tm)
    sh, bh = _seg_specs(h_segs, tm)
    assert ba[-1] == bb[-1] == bh[-1]
    return pl.pallas_call(
        functools.partial(_mm_ln_kernel, bounds_a=tuple(ba), bounds_b=tuple(bb), bounds_h=tuple(bh)),
        grid=(ba[-1],),
        in_specs=sa + sb + sh + [pl.BlockSpec((ka, n), lambda i: (0, 0)),
                                 pl.BlockSpec((kb, n), lambda i: (1, 0)),
                                 pl.BlockSpec((1, n), lambda i: (0, 0)),
                                 pl.BlockSpec((1, n), lambda i: (0, 0))],
        out_specs=pl.BlockSpec((tm, n), lambda i: (i, 0)),
        out_shape=jax.ShapeDtypeStruct((ba[-1] * tm, n), F32),
        compiler_params=_cparams(("parallel",)),
        name="out_proj_ln",
    )(*xa_segs, *xb_segs, *h_segs, w, w, g.reshape(1, n), b.reshape(1, n))


def _row_max_first(vals, lane):
    mx = jnp.max(vals, axis=1, keepdims=True)
    idx = jnp.min(jnp.where(vals == mx, lane, LANES), axis=1, keepdims=True)
    return mx, idx


def _moe_ln_kernel(t_ref, rw_ref, rb_ref, wg_ref, wu_ref, wd_ref, g_ref, b_ref, o_ref,
                   gates_ref, tb_ref, hb_ref, acc_ref):
    e = pl.program_id(1)

    @pl.when(e == 0)
    def _route():
        t = t_ref[...]
        logits = lax.dot_general(rw_ref[...], t, (((1,), (1,)), ((), ())), preferred_element_type=F32,
                                 precision=lax.Precision.HIGHEST)[:N_EXPERTS]
        s = jax.nn.sigmoid(logits)
        sb = s + rb_ref[...]
        row = lax.broadcasted_iota(jnp.int32, sb.shape, 0)
        grp = row // EXPERTS_PER_GROUP

        def max_first(vals):
            mx = jnp.max(vals, axis=0, keepdims=True)
            return mx, jnp.min(jnp.where(vals == mx, row, N_EXPERTS), axis=0, keepdims=True)

        best = None
        gsel = None
        for gi in range(N_GROUPS):
            vals = jnp.where(grp == gi, sb, NEG_INF)
            m1, i1 = max_first(vals)
            m2 = jnp.max(jnp.where(row == i1, NEG_INF, vals), axis=0, keepdims=True)
            score = m1 + m2
            if gi == 0:
                best, gsel = score, jnp.zeros_like(i1)
            else:
                upd = score > best
                best = jnp.where(upd, score, best)
                gsel = jnp.where(upd, gi, gsel)
        vals = jnp.where(grp == gsel, sb, NEG_INF)
        _, i1 = max_first(vals)
        _, i2 = max_first(jnp.where(row == i1, NEG_INF, vals))
        s1 = jnp.sum(jnp.where(row == i1, s, 0.0), axis=0, keepdims=True)
        s2 = jnp.sum(jnp.where(row == i2, s, 0.0), axis=0, keepdims=True)
        den = s1 + s2
        gates_t = jnp.where(row == i1, s1 / den, 0.0) + jnp.where(row == i2, s2 / den, 0.0)
        gates_t = jnp.concatenate([gates_t, jnp.zeros((LANES - N_EXPERTS, gates_t.shape[1]), F32)], axis=0)
        gates_ref[...] = gates_t.T
        tb_ref[...] = t.astype(BF16)
        acc_ref[...] = jnp.zeros_like(acc_ref)

    tb = tb_ref[...]
    gates = gates_ref[...]
    lane = lax.broadcasted_iota(jnp.int32, gates.shape, 1)
    for j in range(EXPERTS_PER_GROUP):
        sl = slice(j * D_EXPERT, (j + 1) * D_EXPERT)
        ge = jnp.sum(jnp.where(lane == e * EXPERTS_PER_GROUP + j, gates, 0.0), axis=1, keepdims=True)
        g = jnp.dot(tb, wg_ref[j], preferred_element_type=F32)
        u = jnp.dot(tb, wu_ref[j], preferred_element_type=F32)
        hb_ref[:, sl] = ((g * jax.nn.sigmoid(g)) * u * ge).astype(BF16)
    acc_ref[...] += jnp.dot(hb_ref[...], wd_ref[...], preferred_element_type=F32)

    @pl.when(e == pl.num_programs(1) - 1)
    def _finish():
        o_ref[...] = _layer_norm_rows(DEEPNORM_ALPHA * t_ref[...] + acc_ref[...], g_ref[...], b_ref[...])


def _group_experts(w_gate, w_up, w_down):
    ne, d, f = w_gate.shape
    es = EXPERTS_PER_GROUP
    return w_gate.astype(BF16), w_up.astype(BF16), w_down.astype(BF16).reshape(ne // es, es * f, d)


def _moe_ln(t, router_w_pad, router_b_pad, wg, wu, wd, g, b, tm):
    m, d = t.shape
    ng, gf, _ = wd.shape
    es, f = EXPERTS_PER_GROUP, wg.shape[2]
    assert m % tm == 0
    return pl.pallas_call(
        _moe_ln_kernel,
        grid=(m // tm, ng),
        in_specs=[pl.BlockSpec((tm, d), lambda i, e: (i, 0)),
                  pl.BlockSpec((LANES, d), lambda i, e: (0, 0)),
                  pl.BlockSpec((N_EXPERTS, 1), lambda i, e: (0, 0)),
                  pl.BlockSpec((es, d, f), lambda i, e: (e, 0, 0)),
                  pl.BlockSpec((es, d, f), lambda i, e: (e, 0, 0)),
                  pl.BlockSpec((None, gf, d), lambda i, e: (e, 0, 0)),
                  pl.BlockSpec((1, d), lambda i, e: (0, 0)),
                  pl.BlockSpec((1, d), lambda i, e: (0, 0))],
        out_specs=pl.BlockSpec((tm, d), lambda i, e: (i, 0)),
        out_shape=jax.ShapeDtypeStruct((m, d), F32),
        scratch_shapes=[pltpu.VMEM((tm, LANES), F32),
                        pltpu.VMEM((tm, d), BF16),
                        pltpu.VMEM((tm, gf), BF16),
                        pltpu.VMEM((tm, d), F32)],
        compiler_params=_cparams(("parallel", "arbitrary")),
        name="moe_ln",
    )(t, router_w_pad, router_b_pad, wg, wu, wd, g.reshape(1, d), b.reshape(1, d))


def _t5_bucket_np(rel):
    n = np.maximum(rel, 0)
    max_exact = T5_BUCKETS // 2
    nf = np.maximum(n, 1).astype(np.float32)
    large = max_exact + (np.log(nf / np.float32(max_exact)) / np.float32(math.log(T5_MAX_DIST / max_exact))
                         * np.float32(T5_BUCKETS - max_exact)).astype(np.int32)
    large = np.minimum(large, T5_BUCKETS - 1)
    return np.where(n < max_exact, n, large).astype(np.int32)


def _old_moba_prompt_kernel(q_ref, k_ref, v_ref, bd_ref, bp_ref, bf_ref, o_ref,
                        kmean_ref, sel_ref, m_ref, l_ref, acc_ref):
    i = pl.program_id(2)
    blk = MOBA_BLOCK
    nblk = k_ref.shape[0] // blk
    hd = HEAD_DIM

    @pl.when(i == 0)
    def _means():
        for n in range(nblk):
            kmean_ref[n:n + 1, :] = jnp.mean(k_ref[n * blk:(n + 1) * blk, :], axis=0, keepdims=True)

    q4 = q_ref[...] * (hd ** -0.5)
    for kvl in range(2):
        lo = kvl * 2 * hd
        q2 = jnp.concatenate([q4[:, lo:lo + hd], q4[:, lo + hd:lo + 2 * hd]], axis=0)
        q2b = q2.astype(BF16)
        kcol = slice(kvl * hd, (kvl + 1) * hd)

        kmean = kmean_ref[:, kcol]
        gate = lax.dot_general(q2, kmean, (((1,), (1,)), ((), ())), preferred_element_type=F32,
                               precision=lax.Precision.HIGHEST)
        col = lax.broadcasted_iota(jnp.int32, gate.shape, 1)
        gate = jnp.where(col < i, gate, NEG_INF)
        sel = jnp.zeros(gate.shape, F32)
        for _ in range(MOBA_TOPK):
            mx = jnp.max(gate, axis=1, keepdims=True)
            idx = jnp.min(jnp.where(gate == mx, col, nblk), axis=1, keepdims=True)
            hit = (col == idx) & (mx > NEG_INF)
            sel = jnp.where(hit, 1.0, sel)
            gate = jnp.where(col == idx, NEG_INF, gate)
        sel_ref[...] = sel

        m_ref[...] = jnp.full(m_ref.shape, NEG_INF, F32)
        l_ref[...] = jnp.zeros(l_ref.shape, F32)
        acc_ref[...] = jnp.zeros(acc_ref.shape, F32)

        def update(s, vb):
            m_old = m_ref[...]
            m_new = jnp.maximum(m_old, jnp.max(s, axis=1, keepdims=True))
            alpha = jnp.exp(m_old - m_new)
            p = jnp.exp(s - m_new)
            l_ref[...] = alpha * l_ref[...] + jnp.sum(p, axis=1, keepdims=True)
            acc_ref[...] = alpha * acc_ref[...] + jnp.dot(p.astype(BF16), vb, preferred_element_type=F32)
            m_ref[...] = m_new

        def tile(j):
            r0 = pl.multiple_of(j * blk, blk)
            kb = k_ref[pl.ds(r0, blk), :][:, kcol].astype(BF16)
            vb = v_ref[pl.ds(r0, blk), :][:, kcol].astype(BF16)
            return _dot_nt(q2b, kb), vb

        s, vb = tile(i)
        update(s + bd_ref[kvl], vb)

        @pl.when(i >= 1)
        def _prev():
            s, vb = tile(i - 1)
            live = jnp.sum(jnp.where(col == i - 1, sel_ref[...], 0.0), axis=1, keepdims=True) > 0.0
            update(jnp.where(live, s + bp_ref[kvl], NEG_INF), vb)

        def far(j, carry):
            s, vb = tile(j)
            live = jnp.sum(jnp.where(col == j, sel_ref[...], 0.0), axis=1, keepdims=True) > 0.0
            rb = jnp.where(live, bf_ref[kvl], NEG_INF)
            update(s + rb, vb)
            return carry

        lax.fori_loop(0, jnp.maximum(i - 1, 0), far, 0)

        o = acc_ref[...] / l_ref[...]
        o_ref[:, lo:lo + hd] = o[:blk]
        o_ref[:, lo + hd:lo + 2 * hd] = o[blk:]


def _old_moba_prompt(proj, bias_rel, n_batch, seq, q_col, k_col, v_col):
    blk = MOBA_BLOCK
    nblk = seq // blk
    iq = np.arange(blk)[:, None]
    ik = np.arange(blk)[None, :]
    idx_diag = _t5_bucket_np(iq - ik)
    idx_prev = _t5_bucket_np(blk + iq - ik)
    bt = bias_rel.T
    diag = jnp.where(jnp.asarray(iq >= ik)[None], bt[:, idx_diag], NEG_INF)
    prev = bt[:, idx_prev]
    far = jnp.broadcast_to(bt[:, T5_BUCKETS - 1][:, None, None], (N_ATT_HEADS, blk, 1))
    diag = diag.reshape(KV_HEADS_A, 2 * blk, blk)
    prev = prev.reshape(KV_HEADS_A, 2 * blk, blk)
    far = far.reshape(KV_HEADS_A, 2 * blk, 1)
    rows_per_b = seq // blk
    return pl.pallas_call(
        _moba_prompt_kernel,
        grid=(n_batch, 2, nblk),
        in_specs=[pl.BlockSpec((blk, 2 * LANES), lambda b, p, i: (b * rows_per_b + i, q_col // 2 + p)),
                  pl.BlockSpec((seq, LANES), lambda b, p, i: (b, k_col + p)),
                  pl.BlockSpec((seq, LANES), lambda b, p, i: (b, v_col + p)),
                  pl.BlockSpec((2, 2 * blk, blk), lambda b, p, i: (p, 0, 0)),
                  pl.BlockSpec((2, 2 * blk, blk), lambda b, p, i: (p, 0, 0)),
                  pl.BlockSpec((2, 2 * blk, 1), lambda b, p, i: (p, 0, 0))],
        out_specs=pl.BlockSpec((blk, 2 * LANES), lambda b, p, i: (b * rows_per_b + i, p)),
        out_shape=jax.ShapeDtypeStruct((n_batch * seq, 4 * LANES), F32),
        scratch_shapes=[pltpu.VMEM((nblk, LANES), F32),
                        pltpu.VMEM((2 * blk, nblk), F32),
                        pltpu.VMEM((2 * blk, 1), F32),
                        pltpu.VMEM((2 * blk, 1), F32),
                        pltpu.VMEM((2 * blk, HEAD_DIM), F32)],
        compiler_params=_cparams(("parallel", "parallel", "arbitrary")),
        name="moba_prompt",
    )(proj, proj, proj, diag, prev, far)


def _t5_thresholds():
    b = _t5_bucket_np(np.arange(4 * T5_MAX_DIST))
    return [int(np.argmax(b >= k)) for k in range(1, T5_BUCKETS)]


def _bias_tile_kernel(rb_ref, o_ref, *, off, sign, lo, hi):
    h = pl.program_id(0)
    row = lax.broadcasted_iota(jnp.int32, o_ref.shape, 0)
    col = lax.broadcasted_iota(jnp.int32, o_ref.shape, 1)
    rel = off + sign * (row - col)
    t = jnp.full(o_ref.shape, rb_ref[0, h], F32)
    for k, thr in enumerate(_t5_thresholds(), start=1):
        t = jnp.where(rel >= thr, rb_ref[k, h], t)
    if lo is not None:
        t = jnp.where((rel >= lo) & (rel <= hi), t, NEG_INF)
    o_ref[...] = t


def _bias_tiles(rel_bias, rows, cols, off, sign=1, lo=None, hi=None):
    return pl.pallas_call(
        functools.partial(_bias_tile_kernel, off=off, sign=sign, lo=lo, hi=hi),
        grid=(N_ATT_HEADS,),
        in_specs=[pl.BlockSpec(memory_space=pltpu.SMEM)],
        out_specs=pl.BlockSpec((None, rows, cols), lambda h: (h, 0, 0)),
        out_shape=jax.ShapeDtypeStruct((N_ATT_HEADS, rows, cols), F32),
        compiler_params=_cparams(("parallel",)),
        name="bias_tiles",
    )(rel_bias)


def _moba_prompt_kernel(q_ref, k_ref, v_ref, bd_ref, bp_ref, bf_ref, o_ref, kmean_ref, vt_ref, sel_ref):
    i = pl.program_id(2)
    blk = MOBA_BLOCK
    nblk = k_ref.shape[0] // blk
    hd = HEAD_DIM

    @pl.when(i == 0)
    def _per_sequence():
        for n in range(nblk):
            kmean_ref[n:n + 1, :] = jnp.mean(k_ref[n * blk:(n + 1) * blk, :], axis=0, keepdims=True)
            vt_ref[n] = v_ref[n * blk:(n + 1) * blk, :].T.astype(BF16)

    q4 = q_ref[...] * (hd ** -0.5)
    q2b = []
    for kvl in range(2):
        lo = kvl * 2 * hd
        q2 = jnp.concatenate([q4[:, lo:lo + hd], q4[:, lo + hd:lo + 2 * hd]], axis=0)
        q2b.append(q2.astype(BF16))
        gate = lax.dot_general(kmean_ref[:, kvl * hd:(kvl + 1) * hd], q2, (((1,), (1,)), ((), ())),
                               preferred_element_type=F32, precision=lax.Precision.HIGHEST)
        row = lax.broadcasted_iota(jnp.int32, gate.shape, 0)
        gate = jnp.where(row < i, gate, NEG_INF)
        sel = jnp.zeros(gate.shape, F32)
        for _ in range(MOBA_TOPK):
            mx = jnp.max(gate, axis=0, keepdims=True)
            idx = jnp.min(jnp.where(gate == mx, row, nblk), axis=0, keepdims=True)
            hit = (row == idx) & (mx > NEG_INF)
            sel = jnp.where(hit, 1.0, sel)
            gate = jnp.where(row == idx, NEG_INF, gate)
        sel_ref[kvl] = sel

    def scores(j, kvl):
        r0 = pl.multiple_of(j * blk, blk)
        kb = k_ref[pl.ds(r0, blk), :][:, kvl * hd:(kvl + 1) * hd].astype(BF16)
        return _dot_nt(kb, q2b[kvl])

    def update(state, s, j, kvl):
        m_old, l_old, acc = state
        m_new = jnp.maximum(m_old, jnp.max(s, axis=0, keepdims=True))
        alpha = jnp.exp(m_old - m_new)
        p = jnp.exp(s - m_new)
        l_new = alpha * l_old + jnp.sum(p, axis=0, keepdims=True)
        vt = vt_ref[j][kvl * hd:(kvl + 1) * hd, :]
        acc = alpha * acc + jnp.dot(vt, p.astype(BF16), preferred_element_type=F32)
        return m_new, l_new, acc

    def live_row(j, kvl):
        return sel_ref[kvl, pl.ds(j, 1), :] > 0.0

    init = (jnp.full((1, 2 * blk), NEG_INF, F32), jnp.zeros((1, 2 * blk), F32), jnp.zeros((hd, 2 * blk), F32))
    jp = jnp.maximum(i - 1, 0)
    states = []
    for kvl in range(2):
        st = update(init, scores(i, kvl) + bd_ref[kvl], i, kvl)
        s = jnp.where(live_row(jp, kvl), scores(jp, kvl) + bp_ref[kvl], NEG_INF)
        states.append(update(st, s, jp, kvl))

    def far(pr, carry):
        j0 = 2 * pr
        r0 = pl.multiple_of(j0 * blk, 2 * blk)
        second_far = j0 + 1 < i - 1
        out = []
        for kvl in range(2):
            kb = k_ref[pl.ds(r0, 2 * blk), :][:, kvl * hd:(kvl + 1) * hd].astype(BF16)
            s = _dot_nt(kb, q2b[kvl])
            far_bias = bf_ref[kvl]
            s0 = s[:blk] + jnp.where(live_row(j0, kvl), far_bias, NEG_INF)
            s1 = s[blk:] + jnp.where(live_row(j0 + 1, kvl) & second_far, far_bias, NEG_INF)
            m_old, l_old, acc = carry[kvl]
            m_new = jnp.maximum(m_old, jnp.maximum(jnp.max(s0, axis=0, keepdims=True),
                                                   jnp.max(s1, axis=0, keepdims=True)))
            alpha = jnp.exp(m_old - m_new)
            p = jnp.concatenate([jnp.exp(s0 - m_new), jnp.exp(s1 - m_new)], axis=0)
            l_new = alpha * l_old + jnp.sum(p, axis=0, keepdims=True)
            rows = slice(kvl * hd, (kvl + 1) * hd)
            vt = jnp.concatenate([vt_ref[j0][rows, :], vt_ref[j0 + 1][rows, :]], axis=1)
            acc = alpha * acc + jnp.dot(vt, p.astype(BF16), preferred_element_type=F32)
            out.append((m_new, l_new, acc))
        return tuple(out)

    states = lax.fori_loop(0, i // 2, far, tuple(states))
    ot = jnp.concatenate([acc / l for (_, l, acc) in states], axis=0)
    o = ot.T
    for kvl in range(2):
        for hh in range(2):
            o_ref[:, (2 * kvl + hh) * hd:(2 * kvl + hh + 1) * hd] = o[hh * blk:(hh + 1) * blk, kvl * hd:(kvl + 1) * hd]


def _moba_prompt(proj, bias_rel, n_batch, seq, q_col, k_col, v_col):
    blk = MOBA_BLOCK
    nblk = seq // blk

    def pair_up(t):
        return jnp.transpose(t.reshape(KV_HEADS_A, 2, blk, blk), (0, 2, 1, 3)).reshape(KV_HEADS_A, blk, 2 * blk)

    diag = pair_up(_bias_tiles(bias_rel, blk, blk, 0, sign=-1, lo=0, hi=2 * blk))
    prev = pair_up(_bias_tiles(bias_rel, blk, blk, blk, sign=-1))
    far = jnp.broadcast_to(bias_rel[T5_BUCKETS - 1].reshape(KV_HEADS_A, 1, 2, 1), (KV_HEADS_A, 1, 2, blk))
    far = far.reshape(KV_HEADS_A, 1, 2 * blk)
    rows_per_b = seq // blk
    return pl.pallas_call(
        _moba_prompt_kernel,
        grid=(n_batch, 2, nblk),
        in_specs=[pl.BlockSpec((blk, 2 * LANES), lambda b, p, i: (b * rows_per_b + i, q_col // 2 + p)),
                  pl.BlockSpec((seq, LANES), lambda b, p, i: (b, k_col + p)),
                  pl.BlockSpec((seq, LANES), lambda b, p, i: (b, v_col + p)),
                  pl.BlockSpec((2, blk, 2 * blk), lambda b, p, i: (p, 0, 0)),
                  pl.BlockSpec((2, blk, 2 * blk), lambda b, p, i: (p, 0, 0)),
                  pl.BlockSpec((2, 1, 2 * blk), lambda b, p, i: (p, 0, 0))],
        out_specs=pl.BlockSpec((blk, 2 * LANES), lambda b, p, i: (b * rows_per_b + i, p)),
        out_shape=jax.ShapeDtypeStruct((n_batch * seq, 4 * LANES), F32),
        scratch_shapes=[pltpu.VMEM((nblk, LANES), F32),
                        pltpu.VMEM((nblk, LANES, blk), BF16),
                        pltpu.VMEM((2, nblk, 2 * blk), F32)],
        compiler_params=_cparams(("parallel", "parallel", "arbitrary")),
        name="moba_prompt",
    )(proj, proj, proj, diag, prev, far)


MOBA_DEC_BLOCKS_PER_STEP = 16
PAGES_PER_BLOCK = MOBA_BLOCK // PAGE_SIZE


def _moba_sample_kernel(pt_ref, q_ref, kn_ref, vn_ref, b31_ref, bfar_ref, bown_ref, *rest,
                        n_blocks, nb):
    npg = nb * PAGES_PER_BLOCK
    k_pages = rest[:npg]
    v_pages = rest[npg:2 * npg]
    o_ref = rest[2 * npg]
    g_ref, m_ref, l_ref, part_ref = rest[2 * npg + 1:]
    s = pl.program_id(1)
    q = q_ref[...]
    qb = q.astype(BF16)
    lane = lax.broadcasted_iota(jnp.int32, g_ref.shape, 1)

    @pl.when(s == 0)
    def _init():
        g_ref[...] = jnp.zeros_like(g_ref)
        m_ref[...] = jnp.zeros_like(m_ref)
        l_ref[...] = jnp.zeros_like(l_ref)

    for j in range(nb):
        n = s * nb + j
        kts = [k_pages[j * PAGES_PER_BLOCK + t][...] for t in range(PAGES_PER_BLOCK)]
        vts = [v_pages[j * PAGES_PER_BLOCK + t][...] for t in range(PAGES_PER_BLOCK)]
        sc = jnp.concatenate([_dot(qb, kt) for kt in kts], axis=1)
        gate = jnp.sum(sc, axis=1, keepdims=True) * (1.0 / MOBA_BLOCK)
        sc = sc + jnp.where(n == n_blocks - 1, b31_ref[...], bfar_ref[...])
        mx = jnp.max(sc, axis=1, keepdims=True)
        p = jnp.exp(sc - mx)
        hit = lane == n
        g_ref[...] = jnp.where(hit, gate, g_ref[...])
        m_ref[...] = jnp.where(hit, mx, m_ref[...])
        l_ref[...] = jnp.where(hit, jnp.sum(p, axis=1, keepdims=True), l_ref[...])
        o = None
        for t, vt in enumerate(vts):
            part = _dot_nt(p[:, t * PAGE_SIZE:(t + 1) * PAGE_SIZE], vt)
            o = part if o is None else o + part
        part_ref[n] = o

    @pl.when(s == pl.num_programs(1) - 1)
    def _merge():
        gate = jnp.where(lane < n_blocks, g_ref[...], NEG_INF)
        sel = jnp.zeros(gate.shape, jnp.bool_)
        for _ in range(MOBA_TOPK):
            mxg = jnp.max(gate, axis=1, keepdims=True)
            idx = jnp.min(jnp.where(gate == mxg, lane, LANES), axis=1, keepdims=True)
            hit = (lane == idx) & (mxg > NEG_INF)
            sel = sel | hit
            gate = jnp.where(lane == idx, NEG_INF, gate)
        s_own = _dot_nt(qb, kn_ref[...]) + bown_ref[...]
        m_sel = jnp.where(sel, m_ref[...], NEG_INF)
        m_all = jnp.maximum(jnp.max(m_sel, axis=1, keepdims=True), jnp.max(s_own, axis=1, keepdims=True))
        w = jnp.where(sel, jnp.exp(m_sel - m_all), 0.0)
        p_own = jnp.exp(s_own - m_all)
        den = jnp.sum(w * l_ref[...], axis=1, keepdims=True) + jnp.sum(p_own, axis=1, keepdims=True)
        acc = _dot(p_own, vn_ref[...])
        for n in range(n_blocks):
            wn = jnp.sum(jnp.where(lane == n, w, 0.0), axis=1, keepdims=True)
            acc = acc + wn * part_ref[n]
        o_ref[...] = acc / den


def _moba_sample(q_s, k_new, v_new, cache_k, cache_v, page_table, bias_rel, layer=0):
    n_dec, t_dec = q_s.shape[:2]
    n_pages = page_table.shape[1]
    n_blocks = n_pages // PAGES_PER_BLOCK
    nb = min(MOBA_DEC_BLOCKS_PER_STEP, n_blocks)
    npg = nb * PAGES_PER_BLOCK
    assert n_blocks % nb == 0 and n_blocks <= LANES and t_dec <= 8
    rows = N_ATT_HEADS * t_dec
    kv_w = KV_HEADS_A * HEAD_DIM
    head_kv = np.arange(N_ATT_HEADS) // (N_ATT_HEADS // KV_HEADS_A)
    qh = jnp.transpose(q_s, (0, 2, 1, 3)) * (HEAD_DIM ** -0.5)
    onehot = jnp.asarray(np.eye(KV_HEADS_A, dtype=np.float32)[head_kv])
    qbd = (qh[:, :, :, None, :] * onehot[None, :, None, :, None]).reshape(n_dec, rows, kv_w)
    pad = 8 - t_dec
    kn = jnp.pad(k_new, ((0, 0), (0, pad), (0, 0)))
    vn = jnp.pad(v_new, ((0, 0), (0, pad), (0, 0)))
    bt = bias_rel.T
    past = n_pages * PAGE_SIZE
    tq = np.arange(t_dec)[None, :, None]
    ik = np.arange(MOBA_BLOCK)[None, None, :]
    idx31 = _t5_bucket_np(past + tq - ((n_blocks - 1) * MOBA_BLOCK + ik))[0]
    b31 = bt[:, idx31].reshape(rows, MOBA_BLOCK)
    bfar = jnp.broadcast_to(bt[:, T5_BUCKETS - 1][:, None, None], (N_ATT_HEADS, t_dec, 1)).reshape(rows, 1)
    tk = np.arange(8)[None, :]
    rel_own = np.arange(t_dec)[:, None] - tk
    ok_own = (rel_own >= 0) & (tk < t_dec)
    bown = jnp.where(jnp.asarray(ok_own)[None], bt[:, _t5_bucket_np(rel_own)], NEG_INF).reshape(rows, 8)

    def page_spec(k):
        return pl.BlockSpec((None, kv_w, PAGE_SIZE), lambda b, s, pt, k=k: (pt[b, s * npg + k], layer, 0))

    const2 = lambda b, s, pt: (0, 0)
    grid_spec = pltpu.PrefetchScalarGridSpec(
        num_scalar_prefetch=1,
        grid=(n_dec, n_blocks // nb),
        in_specs=[pl.BlockSpec((None, rows, kv_w), lambda b, s, pt: (b, 0, 0)),
                  pl.BlockSpec((None, 8, kv_w), lambda b, s, pt: (b, 0, 0)),
                  pl.BlockSpec((None, 8, kv_w), lambda b, s, pt: (b, 0, 0)),
                  pl.BlockSpec((rows, MOBA_BLOCK), const2),
                  pl.BlockSpec((rows, 1), const2),
                  pl.BlockSpec((rows, 8), const2)]
                 + [page_spec(k) for k in range(npg)] + [page_spec(k) for k in range(npg)],
        out_specs=pl.BlockSpec((None, rows, kv_w), lambda b, s, pt: (b, 0, 0)),
        scratch_shapes=[pltpu.VMEM((rows, LANES), F32),
                        pltpu.VMEM((rows, LANES), F32),
                        pltpu.VMEM((rows, LANES), F32),
                        pltpu.VMEM((n_blocks, rows, kv_w), F32)],
    )
    out = pl.pallas_call(
        functools.partial(_moba_sample_kernel, n_blocks=n_blocks, nb=nb),
        grid_spec=grid_spec,
        out_shape=jax.ShapeDtypeStruct((n_dec, rows, kv_w), F32),
        compiler_params=_cparams(("parallel", "arbitrary")),
        name="moba_sample",
    )(page_table, qbd, kn, vn, b31, bfar, bown, *([cache_k] * npg), *([cache_v] * npg))
    out = out.reshape(n_dec, N_ATT_HEADS, t_dec, KV_HEADS_A, HEAD_DIM)
    out = out[:, np.arange(N_ATT_HEADS), :, head_kv, :]
    return jnp.transpose(out, (1, 2, 0, 3)).reshape(n_dec, t_dec, N_ATT_HEADS * HEAD_DIM)


SWA_GROUP = N_ATT_HEADS // KV_HEADS_C


def _sink_softmax_pv(parts, sink):
    m = sink
    for s, _ in parts:
        m = jnp.maximum(m, jnp.max(s, axis=1, keepdims=True))
    den = jnp.exp(sink - m)
    ps = []
    for s, _ in parts:
        p = jnp.exp(s - m)
        den = den + jnp.sum(p, axis=1, keepdims=True)
        ps.append(p)
    o = None
    for p, (_, v) in zip(ps, parts):
        t = _dot(p / den, v)
        o = t if o is None else o + t
    return o


def _swa_prompt_kernel(q_ref, kp_ref, ko_ref, vp_ref, vo_ref, bias_ref, sink_ref, o_ref):
    i = pl.program_id(1)
    w, hd, g = WINDOW, HEAD_DIM, SWA_GROUP
    q = q_ref[...] * (hd ** -0.5)
    kcat = jnp.concatenate([kp_ref[...], ko_ref[...]], axis=0)
    vt = jnp.concatenate([vp_ref[...], vo_ref[...]], axis=0).T.astype(BF16)
    row = lax.broadcasted_iota(jnp.int32, (2 * w, g * w), 0)
    has_key = (row >= w) | (i > 0)
    ots = []
    for c in range(KV_HEADS_C):
        qs = jnp.concatenate([q[:, (c * g + j) * hd:(c * g + j + 1) * hd] for j in range(g)], axis=0)
        s = _dot_nt(kcat[:, c * hd:(c + 1) * hd], qs) + bias_ref[c]
        s = jnp.where(has_key, s, NEG_INF)
        sink = sink_ref[c]
        m = jnp.maximum(sink, jnp.max(s, axis=0, keepdims=True))
        p = jnp.exp(s - m)
        den = jnp.sum(p, axis=0, keepdims=True) + jnp.exp(sink - m)
        ots.append(jnp.dot(vt[c * hd:(c + 1) * hd, :], (p / den).astype(BF16), preferred_element_type=F32))
    o = jnp.concatenate(ots, axis=0).T
    for c in range(KV_HEADS_C):
        for j in range(g):
            o_ref[:, (c * g + j) * hd:(c * g + j + 1) * hd] = o[j * w:(j + 1) * w, c * hd:(c + 1) * hd]


def _swa_prompt(proj, bias_rel, sinks, n_batch, seq, q_col, k_col, v_col):
    w, g = WINDOW, SWA_GROUP
    nb = seq // w
    band = _bias_tiles(bias_rel, 2 * w, w, w, sign=-1, lo=0, hi=w)
    band = jnp.transpose(band.reshape(KV_HEADS_C, g, 2 * w, w), (0, 2, 1, 3)).reshape(KV_HEADS_C, 2 * w, g * w)
    sink = jnp.broadcast_to(sinks.reshape(KV_HEADS_C, 1, g, 1), (KV_HEADS_C, 1, g, w)).reshape(KV_HEADS_C, 1, g * w)
    own = lambda c: (lambda b, i: (b * nb + i, c))
    prev = lambda c: (lambda b, i: (b * nb + jnp.maximum(i - 1, 0), c))
    return pl.pallas_call(
        _swa_prompt_kernel,
        grid=(n_batch, nb),
        in_specs=[pl.BlockSpec((w, 4 * LANES), lambda b, i: (b * nb + i, q_col // 4)),
                  pl.BlockSpec((w, LANES), prev(k_col)),
                  pl.BlockSpec((w, LANES), own(k_col)),
                  pl.BlockSpec((w, LANES), prev(v_col)),
                  pl.BlockSpec((w, LANES), own(v_col)),
                  pl.BlockSpec((KV_HEADS_C, 2 * w, g * w), lambda b, i: (0, 0, 0)),
                  pl.BlockSpec((KV_HEADS_C, 1, g * w), lambda b, i: (0, 0, 0))],
        out_specs=pl.BlockSpec((w, 4 * LANES), lambda b, i: (b * nb + i, 0)),
        out_shape=jax.ShapeDtypeStruct((n_batch * seq, 4 * LANES), F32),
        compiler_params=_cparams(("parallel", "parallel")),
        name="swa_prompt",
    )(proj, proj, proj, proj, proj, band, sink)


SWA_DEC_SEQS_PER_STEP = 8


def _swa_sample_kernel(q_ref, kb_ref, vb_ref, kn_ref, vn_ref, bbuf_ref, bnew_ref, sink_ref, o_ref):
    for i in range(SWA_DEC_SEQS_PER_STEP):
        q = q_ref[i]
        s_buf = _dot_nt(q, kb_ref[i]) + bbuf_ref[...]
        s_new = _dot_nt(q, kn_ref[i]) + bnew_ref[...]
        o_ref[i] = _sink_softmax_pv([(s_buf, vb_ref[i]), (s_new, vn_ref[i])], sink_ref[...])


def _swa_sample(q_s, k_new, v_new, buf_k, buf_v, bias_rel, sinks):
    n_dec, t_dec = q_s.shape[:2]
    nbuf = buf_k.shape[1]
    rows = N_ATT_HEADS * t_dec
    kv_w = KV_HEADS_C * HEAD_DIM
    sb = SWA_DEC_SEQS_PER_STEP
    assert n_dec % sb == 0 and t_dec <= 8
    head_kv = np.arange(N_ATT_HEADS) // SWA_GROUP
    qh = jnp.transpose(q_s, (0, 2, 1, 3)) * (HEAD_DIM ** -0.5)
    onehot = jnp.asarray(np.eye(KV_HEADS_C, dtype=np.float32)[head_kv])
    qbd = (qh[:, :, :, None, :] * onehot[None, :, None, :, None]).reshape(n_dec, rows, kv_w)
    pad = 8 - t_dec
    kn = jnp.pad(k_new, ((0, 0), (0, pad), (0, 0)))
    vn = jnp.pad(v_new, ((0, 0), (0, pad), (0, 0)))
    bt = bias_rel.T
    t = np.arange(t_dec)[:, None]
    rel_buf = nbuf + t - np.arange(nbuf)[None, :]
    ok_buf = (rel_buf >= 0) & (rel_buf <= WINDOW)
    bbuf = jnp.where(jnp.asarray(ok_buf)[None], bt[:, _t5_bucket_np(rel_buf)], NEG_INF).reshape(rows, nbuf)
    tk = np.arange(8)[None, :]
    rel_new = t - tk
    ok_new = (rel_new >= 0) & (tk < t_dec)
    bnew = jnp.where(jnp.asarray(ok_new)[None], bt[:, _t5_bucket_np(rel_new)], NEG_INF).reshape(rows, 8)
    sink = jnp.broadcast_to(sinks[:, None, None], (N_ATT_HEADS, t_dec, 1)).reshape(rows, 1)
    seq3 = lambda r, c: pl.BlockSpec((sb, r, c), lambda i: (i, 0, 0))
    const2 = lambda r, c: pl.BlockSpec((r, c), lambda i: (0, 0))
    out = pl.pallas_call(
        _swa_sample_kernel,
        grid=(n_dec // sb,),
        in_specs=[seq3(rows, kv_w), seq3(nbuf, kv_w), seq3(nbuf, kv_w), seq3(8, kv_w), seq3(8, kv_w),
                  const2(rows, nbuf), const2(rows, 8), const2(rows, 1)],
        out_specs=seq3(rows, kv_w),
        out_shape=jax.ShapeDtypeStruct((n_dec, rows, kv_w), F32),
        compiler_params=_cparams(("parallel",)),
        name="swa_sample",
    )(qbd, buf_k, buf_v, kn, vn, bbuf, bnew, sink)
    out = out.reshape(n_dec, N_ATT_HEADS, t_dec, KV_HEADS_C, HEAD_DIM)
    out = out[:, np.arange(N_ATT_HEADS), :, head_kv, :]
    return jnp.transpose(out, (1, 2, 0, 3)).reshape(n_dec, t_dec, N_ATT_HEADS * HEAD_DIM)


def _softplus(x):
    return jnp.maximum(x, 0.0) + jnp.log1p(jnp.exp(-jnp.abs(x)))


def _neg_expm1(y):
    return -jnp.tanh(0.5 * y) * (jnp.exp(y) + 1.0)


def _shifted_rows(prev8, cur, k):
    if k == 0:
        return cur
    ext = jnp.concatenate([prev8, cur], axis=0)
    return pltpu.roll(ext, k, axis=0)[8:]


def _causal_conv_rows(prev8, cur, w):
    y = cur * w[CONV_WIDTH - 1:CONV_WIDTH, :]
    for k in range(1, CONV_WIDTH):
        y = y + _shifted_rows(prev8, cur, k) * w[CONV_WIDTH - 1 - k:CONV_WIDTH - k, :]
    return y


def _lru_gates(xc, wa, ba, wx, bx, sp):
    r = jax.nn.sigmoid(_dot(xc, wa) + ba)
    i = jax.nn.sigmoid(_dot(xc, wx) + bx)
    log_a = -LRU_C * r * sp
    a = jnp.exp(log_a)
    b = jnp.sqrt(_neg_expm1(2.0 * log_a)) * (i * xc)
    return a, b


LRU_ROWS = 256


def _lru_prompt_kernel(x_ref, gr_ref, cw_ref, cb_ref, wa_ref, ba_ref, wx_ref, bx_ref, lam_ref,
                       o_ref, hl_ref):
    n = x_ref.shape[0] // LRU_ROWS
    sp = _softplus(-lam_ref[...])
    row = lax.broadcasted_iota(jnp.int32, (LRU_ROWS, LANES), 0)

    def body(c, carry):
        prev8, h = carry
        r0 = pl.multiple_of(c * LRU_ROWS, LRU_ROWS)
        cur = x_ref[pl.ds(r0, LRU_ROWS), :]
        xc = _causal_conv_rows(prev8, cur, cw_ref[...]) + cb_ref[...]
        a, b = _lru_gates(xc, wa_ref[...], ba_ref[...], wx_ref[...], bx_ref[...], sp)
        d = 1
        while d < LRU_ROWS:
            a_s = jnp.where(row >= d, pltpu.roll(a, d, axis=0), 1.0)
            b_s = jnp.where(row >= d, pltpu.roll(b, d, axis=0), 0.0)
            b = a * b_s + b
            a = a * a_s
            d *= 2
        hs = a * h + b
        o_ref[pl.ds(r0, LRU_ROWS), :] = hs * jax.nn.gelu(gr_ref[pl.ds(r0, LRU_ROWS), :])
        return cur[LRU_ROWS - 8:], hs[LRU_ROWS - 1:]

    _, h = lax.fori_loop(0, n, body, (jnp.zeros((8, LANES), F32), jnp.zeros((1, LANES), F32)))
    hl_ref[...] = h


def _lru_prompt(proj, n_batch, seq, x_col, g_col, conv_w, conv_b, wa_bd, b_a, wx_bd, b_x, lam):
    ncb = D_RNN // LANES
    vec = lambda v: v.reshape(1, D_RNN)
    vspec = pl.BlockSpec((1, LANES), lambda b, c: (0, c))
    out, hl = pl.pallas_call(
        _lru_prompt_kernel,
        grid=(n_batch, ncb),
        in_specs=[pl.BlockSpec((seq, LANES), lambda b, c: (b, x_col + c)),
                  pl.BlockSpec((seq, LANES), lambda b, c: (b, g_col + c)),
                  pl.BlockSpec((CONV_WIDTH, LANES), lambda b, c: (0, c)),
                  vspec,
                  pl.BlockSpec((LANES, LANES), lambda b, c: (c, c)), vspec,
                  pl.BlockSpec((LANES, LANES), lambda b, c: (c, c)), vspec,
                  vspec],
        out_specs=[pl.BlockSpec((seq, LANES), lambda b, c: (b, c)),
                   pl.BlockSpec((None, 1, LANES), lambda b, c: (b, 0, c))],
        out_shape=[jax.ShapeDtypeStruct((n_batch * seq, D_RNN), F32),
                   jax.ShapeDtypeStruct((n_batch, 1, D_RNN), F32)],
        compiler_params=_cparams(("parallel", "parallel")),
        name="lru_prompt",
    )(proj, proj, conv_w, vec(conv_b), wa_bd, vec(b_a), wx_bd, vec(b_x), vec(lam))
    return out, hl.reshape(n_batch, D_RNN)


def _lru_sample_kernel(xp_ref, gr_ref, h0_ref, cw_ref, cb_ref, wa_ref, ba_ref, wx_ref, bx_ref, lam_ref,
                       o_ref, hl_ref):
    t_dec = gr_ref.shape[0]
    sp = _softplus(-lam_ref[...])
    h = h0_ref[...]
    for t in range(t_dec):
        xc = cb_ref[...]
        for j in range(CONV_WIDTH):
            xc = xc + xp_ref[t + j] * cw_ref[j:j + 1, :]
        a, b = _lru_gates(xc, wa_ref[...], ba_ref[...], wx_ref[...], bx_ref[...], sp)
        h = a * h + b
        o_ref[t] = h * jax.nn.gelu(gr_ref[t])
    hl_ref[...] = h


def _lru_sample(xr, gr, h0, conv0, conv_w, conv_b, wa_bd, b_a, wx_bd, b_x, lam):
    n_dec, t_dec, _ = xr.shape
    xp = jnp.transpose(jnp.concatenate([conv0, xr], axis=1), (1, 0, 2))
    grt = jnp.transpose(gr, (1, 0, 2))
    vec = lambda v: v.reshape(1, D_RNN)
    full = lambda *s: pl.BlockSpec(s, lambda i: (0,) * len(s))
    out, hl = pl.pallas_call(
        _lru_sample_kernel,
        grid=(1,),
        in_specs=[full(t_dec + CONV_WIDTH - 1, n_dec, D_RNN), full(t_dec, n_dec, D_RNN), full(n_dec, D_RNN),
                  full(CONV_WIDTH, D_RNN), full(1, D_RNN), full(D_RNN, D_RNN), full(1, D_RNN),
                  full(D_RNN, D_RNN), full(1, D_RNN), full(1, D_RNN)],
        out_specs=[full(t_dec, n_dec, D_RNN), full(n_dec, D_RNN)],
        out_shape=[jax.ShapeDtypeStruct((t_dec, n_dec, D_RNN), F32),
                   jax.ShapeDtypeStruct((n_dec, D_RNN), F32)],
        compiler_params=_cparams(("arbitrary",)),
        name="lru_sample",
    )(xp, grt, h0, conv_w, vec(conv_b), wa_bd, vec(b_a), wx_bd, vec(b_x), vec(lam))
    return jnp.transpose(out, (1, 0, 2)), hl


def _block_diag(w):
    n, k, _ = w.shape
    eye = jnp.asarray(np.eye(n, dtype=np.float32))
    return (w[:, :, None, :] * eye[:, None, :, None]).reshape(n * k, n * k)


GDN_ROWS = 256
GDN_INTRA_ROWS = 2048


def _bmm(a, b):
    return jnp.einsum('nij,njk->nik', a.astype(BF16), b.astype(BF16), preferred_element_type=F32)


def _bmm_nt(a, b):
    return jnp.einsum('nid,njd->nij', a.astype(BF16), b.astype(BF16), preferred_element_type=F32)


def _col(x, lane, idx):
    return jnp.sum(jnp.where(lane == idx, x, 0.0), axis=1, keepdims=True)


def _gdn_gates_kernel(x_ref, a_ref, dt_ref, beta_ref, gc_ref, *, chunk, pad_period, pad_rows):
    x = x_ref[...]
    tm = x.shape[0]
    beta = jax.nn.sigmoid(x)
    g = -jnp.exp(a_ref[...]) * _softplus(x + dt_ref[...])
    if pad_period:
        row = lax.broadcasted_iota(jnp.int32, x.shape, 0)
        valid = (row & (pad_period - 1)) >= pad_rows
        beta = jnp.where(valid, beta, 0.0)
        g = jnp.where(valid, g, 0.0)
    r = lax.broadcasted_iota(jnp.int32, (tm, tm), 0)
    c = lax.broadcasted_iota(jnp.int32, (tm, tm), 1)
    sh = chunk.bit_length() - 1
    tri = jnp.where(((r >> sh) == (c >> sh)) & (c <= r), 1.0, 0.0)
    beta_ref[...] = beta
    gc_ref[...] = jnp.dot(tri, g, preferred_element_type=F32, precision=lax.Precision.HIGHEST)


def _gdn_gates(x, col, rows, a_log, dt_bias, chunk, pad_period=0, pad_rows=0):
    tm = GDN_ROWS
    assert rows % tm == 0 and tm % chunk == 0
    a_pad = jnp.pad(a_log.reshape(1, -1), ((0, 0), (GDN_HEADS, LANES - 2 * GDN_HEADS)))
    dt_pad = jnp.pad(dt_bias.reshape(1, -1), ((0, 0), (GDN_HEADS, LANES - 2 * GDN_HEADS)))
    return pl.pallas_call(
        functools.partial(_gdn_gates_kernel, chunk=chunk, pad_period=pad_period, pad_rows=pad_rows),
        grid=(rows // tm,),
        in_specs=[pl.BlockSpec((tm, LANES), lambda i: (i, col)),
                  pl.BlockSpec((1, LANES), lambda i: (0, 0)),
                  pl.BlockSpec((1, LANES), lambda i: (0, 0))],
        out_specs=[pl.BlockSpec((tm, LANES), lambda i: (i, 0))] * 2,
        out_shape=[jax.ShapeDtypeStruct((rows, LANES), F32)] * 2,
        compiler_params=_cparams(("parallel",)),
        name="gdn_gates",
    )(x, a_pad, dt_pad)


def _gdn_prep_kernel(x_ref, cw_ref, beta_ref, gc_ref, *out_refs, mode):
    hd = HEAD_DIM
    n = x_ref.shape[0] // GDN_ROWS
    h0 = 2 * pl.program_id(1)
    lane = lax.broadcasted_iota(jnp.int32, (GDN_ROWS, LANES), 1)
    lo = lane < hd

    def halves(c0, c1):
        return jnp.where(lo, c0, c1)

    def put(ref, r0, val):
        ref[0, pl.ds(r0, GDN_ROWS), :] = val[:, :hd]
        ref[1, pl.ds(r0, GDN_ROWS), :] = val[:, hd:]

    def body(c, prev8):
        r0 = pl.multiple_of(c * GDN_ROWS, GDN_ROWS)
        cur = x_ref[pl.ds(r0, GDN_ROWS), :]
        y = _causal_conv_rows(prev8, cur, cw_ref[...])
        y = y * jax.nn.sigmoid(y)
        if mode in ("q", "k"):
            ss = y * y
            s0 = jnp.sum(jnp.where(lo, ss, 0.0), axis=1, keepdims=True)
            s1 = jnp.sum(jnp.where(lo, 0.0, ss), axis=1, keepdims=True)
            y = y * halves(lax.rsqrt(s0 + NORM_EPS), lax.rsqrt(s1 + NORM_EPS))
        if mode == "q":
            put(out_refs[0], r0, y * (hd ** -0.5))
        else:
            beta = beta_ref[pl.ds(r0, GDN_ROWS), :]
            bb = halves(_col(beta, lane, h0), _col(beta, lane, h0 + 1))
            if mode == "k":
                gc = gc_ref[pl.ds(r0, GDN_ROWS), :]
                put(out_refs[0], r0, y)
                put(out_refs[1], r0, y * bb)
                put(out_refs[2], r0, halves(_col(gc, lane, GDN_HEADS + h0), _col(gc, lane, GDN_HEADS + h0 + 1)))
            else:
                put(out_refs[0], r0, y * bb)
        return cur[GDN_ROWS - 8:]

    lax.fori_loop(0, n, body, jnp.zeros((8, LANES), F32))


def _gdn_prep(x, x_col, conv_w, beta, gc, n_batch, seq, mode):
    sec = {"q": 0, "k": 4, "v": 8}[mode]
    n_out = {"q": 1, "k": 3, "v": 1}[mode]
    hspec = pl.BlockSpec((2, seq, HEAD_DIM), lambda b, c: (b * 4 + c, 0, 0))
    gspec = pl.BlockSpec((seq, LANES), lambda b, c: (b, 0))
    outs = pl.pallas_call(
        functools.partial(_gdn_prep_kernel, mode=mode),
        grid=(n_batch, 4),
        in_specs=[pl.BlockSpec((seq, LANES), lambda b, c: (b, x_col + c)),
                  pl.BlockSpec((CONV_WIDTH, LANES), lambda b, c: (0, sec + c)),
                  gspec, gspec],
        out_specs=[hspec] * n_out,
        out_shape=[jax.ShapeDtypeStruct((n_batch * GDN_HEADS, seq, HEAD_DIM), F32)] * n_out,
        compiler_params=_cparams(("parallel", "parallel")),
        name="gdn_prep_" + mode,
    )(x, conv_w, beta, gc)
    return outs


def _gdn_intra_kernel(q_ref, k_ref, kb_ref, vb_ref, gcb_ref, gcr_ref,
                      u_ref, w_ref, qg_ref, kd_ref, qk_ref, gl_ref, *, chunk):
    hd = HEAD_DIM
    rows = q_ref.shape[0]
    n = rows // chunk
    r3 = lambda ref: ref[...].reshape(n, chunk, hd)
    q, k, kb, vb, gcb = r3(q_ref), r3(k_ref), r3(kb_ref), r3(vb_ref), r3(gcb_ref)
    gcr = gcr_ref[...]
    ci = lax.broadcasted_iota(jnp.int32, (n, chunk, chunk), 1)
    si = lax.broadcasted_iota(jnp.int32, (n, chunk, chunk), 2)
    decay = jnp.exp(jnp.where(ci >= si, gcb[:, :, :chunk] - gcr[:, None, :], NEG_INF))
    a = jnp.where(ci > si, _bmm_nt(kb, k) * decay, 0.0)
    b = -a
    y = b
    p = _bmm(b, b)
    stages = chunk.bit_length() - 2
    for t in range(stages):
        y = y + p + _bmm(y, p)
        if t < stages - 1:
            p = _bmm(p, p)
    eg = jnp.exp(gcb)
    rhs = jnp.concatenate([vb, kb * eg], axis=-1)
    sol = rhs + _bmm(y, rhs)
    gl = gcb[:, chunk - 1:chunk, :]
    u_ref[...] = sol[:, :, :hd].reshape(rows, hd)
    w_ref[...] = sol[:, :, hd:].reshape(rows, hd)
    qg_ref[...] = (q * eg).reshape(rows, hd)
    kd_ref[...] = (k * jnp.exp(gl - gcb)).reshape(rows, hd)
    qk_ref[...] = (_bmm_nt(q, k) * decay).reshape(rows, chunk)
    gl_ref[...] = jnp.exp(gl).reshape(n, hd)


def _gdn_intra(q, k, kb, vb, gcb, gc_row, chunk, rows_per_step):
    bh, seq, hd = q.shape
    tr = rows_per_step
    n = tr // chunk
    assert seq % tr == 0
    hspec = pl.BlockSpec((None, tr, hd), lambda h, i: (h, i, 0))
    shape = jax.ShapeDtypeStruct((bh, seq, hd), F32)
    return pl.pallas_call(
        functools.partial(_gdn_intra_kernel, chunk=chunk),
        grid=(bh, seq // tr),
        in_specs=[hspec] * 5 + [pl.BlockSpec((None, n, chunk), lambda h, i: (h, i, 0))],
        out_specs=[hspec] * 4 + [pl.BlockSpec((None, tr, chunk), lambda h, i: (h, i, 0)),
                                 pl.BlockSpec((None, n, hd), lambda h, i: (h, i, 0))],
        out_shape=[shape] * 4 + [jax.ShapeDtypeStruct((bh, seq, chunk), F32),
                                 jax.ShapeDtypeStruct((bh, seq // chunk, hd), F32)],
        compiler_params=_cparams(("parallel", "parallel")),
        name="gdn_intra",
    )(q, k, kb, vb, gcb, gc_row)


def _gdn_chunk_update(s, u, w, qg, kd, qk, gl):
    v_new = u - _bmm(w, s)
    o = _bmm(qg, s) + _bmm(qk, v_new)
    s = s * gl + jnp.einsum('ncd,nce->nde', kd.astype(BF16), v_new.astype(BF16), preferred_element_type=F32)
    return s, o


def _gdn_scan_kernel(u_ref, w_ref, qg_ref, kd_ref, qk_ref, gl_ref, o_ref, sfin_ref, s_scr, *, chunk):
    j = pl.program_id(1)

    @pl.when(j == 0)
    def _zero():
        s_scr[...] = jnp.zeros_like(s_scr)

    n = u_ref.shape[1] // chunk
    s = s_scr[...]
    for c in range(n):
        sl = slice(c * chunk, (c + 1) * chunk)
        s, o = _gdn_chunk_update(s, u_ref[:, sl, :], w_ref[:, sl, :], qg_ref[:, sl, :], kd_ref[:, sl, :],
                                 qk_ref[:, sl, :], gl_ref[:, c:c + 1, :])
        o_ref[:, sl, :] = o
    s_scr[...] = s

    @pl.when(j == pl.num_programs(1) - 1)
    def _final():
        sfin_ref[...] = s


def _gdn_scan(u, w, qg, kd, qk, gl, chunk, heads_per_step, rows_per_step):
    bh, seq, hd = u.shape
    hb, tr = heads_per_step, rows_per_step
    n = tr // chunk
    hspec = pl.BlockSpec((hb, tr, hd), lambda h, i: (h, i, 0))
    return pl.pallas_call(
        functools.partial(_gdn_scan_kernel, chunk=chunk),
        grid=(bh // hb, seq // tr),
        in_specs=[hspec] * 4 + [pl.BlockSpec((hb, tr, chunk), lambda h, i: (h, i, 0)),
                                pl.BlockSpec((hb, n, hd), lambda h, i: (h, i, 0))],
        out_specs=[hspec, pl.BlockSpec((hb, hd, hd), lambda h, i: (h, 0, 0))],
        out_shape=[jax.ShapeDtypeStruct((bh, seq, hd), F32), jax.ShapeDtypeStruct((bh, hd, hd), F32)],
        scratch_shapes=[pltpu.VMEM((hb, hd, hd), F32)],
        compiler_params=_cparams(("parallel", "arbitrary")),
        name="gdn_scan",
    )(u, w, qg, kd, qk, gl)


GDN_DEC_SEQS_PER_STEP = 16


def _gdn_step_kernel(s0_ref, u_ref, w_ref, qg_ref, kd_ref, qk_ref, gl_ref, o_ref, s_ref, *, chunk):
    hd = HEAD_DIM
    n = s0_ref.shape[0]
    r3 = lambda ref, last: ref[...].reshape(n, chunk, last)
    s, o = _gdn_chunk_update(s0_ref[...], r3(u_ref, hd), r3(w_ref, hd), r3(qg_ref, hd), r3(kd_ref, hd),
                             r3(qk_ref, chunk), gl_ref[...][:, None, :])
    o_ref[...] = o.reshape(n * chunk, hd)
    s_ref[...] = s


def _gdn_step(s0, u, w, qg, kd, qk, gl, chunk):
    n_dec, nh, hd, _ = s0.shape
    sb = GDN_DEC_SEQS_PER_STEP
    assert n_dec % sb == 0
    hspec = pl.BlockSpec((None, sb * chunk, hd), lambda h, i: (h, i, 0))
    sspec = pl.BlockSpec((sb, None, hd, hd), lambda h, i: (i, h, 0, 0))
    return pl.pallas_call(
        functools.partial(_gdn_step_kernel, chunk=chunk),
        grid=(nh, n_dec // sb),
        in_specs=[sspec] + [hspec] * 4 + [pl.BlockSpec((None, sb * chunk, chunk), lambda h, i: (h, i, 0)),
                                         pl.BlockSpec((None, sb, hd), lambda h, i: (h, i, 0))],
        out_specs=[hspec, sspec],
        out_shape=[jax.ShapeDtypeStruct((nh, n_dec * chunk, hd), F32),
                   jax.ShapeDtypeStruct((n_dec, nh, hd, hd), F32)],
        compiler_params=_cparams(("parallel", "parallel")),
        name="gdn_step",
    )(s0, u, w, qg, kd, qk, gl)


def _gdn_post_kernel(o_ref, z_ref, w_ref, out_ref):
    hd = HEAD_DIM
    for h in range(GDN_HEADS):
        o = o_ref[h]
        r = o * lax.rsqrt(jnp.mean(o * o, axis=-1, keepdims=True) + NORM_EPS) * w_ref[...]
        z = z_ref[:, h * hd:(h + 1) * hd]
        out_ref[:, h * hd:(h + 1) * hd] = r * (z * jax.nn.sigmoid(z))


def _gdn_post(o, z, z_col, norm_w, n_batch, seq, tm):
    nt = seq // tm
    return pl.pallas_call(
        _gdn_post_kernel,
        grid=(n_batch, nt),
        in_specs=[pl.BlockSpec((GDN_HEADS, tm, HEAD_DIM), lambda b, i: (b, i, 0)),
                  pl.BlockSpec((tm, 4 * LANES), lambda b, i: (b * nt + i, z_col)),
                  pl.BlockSpec((1, HEAD_DIM), lambda b, i: (0, 0))],
        out_specs=pl.BlockSpec((tm, 4 * LANES), lambda b, i: (b * nt + i, 0)),
        out_shape=jax.ShapeDtypeStruct((n_batch * seq, 4 * LANES), F32),
        compiler_params=_cparams(("parallel", "parallel")),
        name="gdn_post",
    )(o, z, norm_w.reshape(1, HEAD_DIM))


def _gdn_front(x, qkv_col, gate_col, conv_w, a_log, dt_bias, n_batch, seq, chunk, pad_period=0, pad_rows=0):
    rows = n_batch * seq
    beta, gc = _gdn_gates(x, gate_col, rows, a_log, dt_bias, chunk, pad_period, pad_rows)
    (q,) = _gdn_prep(x, qkv_col, conv_w, beta, gc, n_batch, seq, "q")
    k, kb, gcb = _gdn_prep(x, qkv_col + 4, conv_w, beta, gc, n_batch, seq, "k")
    (vb,) = _gdn_prep(x, qkv_col + 8, conv_w, beta, gc, n_batch, seq, "v")
    gc_row = jnp.transpose(gc[:, GDN_HEADS:2 * GDN_HEADS].reshape(n_batch, seq, GDN_HEADS), (0, 2, 1))
    gc_row = gc_row.reshape(n_batch * GDN_HEADS, seq // chunk, chunk)
    return _gdn_intra(q, k, kb, vb, gcb, gc_row, chunk, min(seq, GDN_INTRA_ROWS))


def _x_l2norm(x):
    return x * lax.rsqrt(jnp.sum(x * x, -1, keepdims=True) + NORM_EPS)


def _x_rms(x, w):
    return x * lax.rsqrt(jnp.mean(x * x, -1, keepdims=True) + NORM_EPS) * w


def _x_t5_bucket(rel):
    n = jnp.maximum(rel, 0)
    max_exact = T5_BUCKETS // 2
    nf = jnp.maximum(n, 1).astype(F32)
    large = max_exact + (jnp.log(nf / max_exact) / math.log(T5_MAX_DIST / max_exact)
                         * (T5_BUCKETS - max_exact)).astype(jnp.int32)
    large = jnp.minimum(large, T5_BUCKETS - 1)
    return jnp.where(n < max_exact, n, large)


def _x_causal_conv(x, buf, w, b=None):
    L = x.shape[1]
    xp = jnp.concatenate([buf.astype(x.dtype), x], axis=1)
    y = xp[:, CONV_WIDTH - 1:] * w[CONV_WIDTH - 1]
    for j in range(CONV_WIDTH - 1):
        y = y + xp[:, j:j + L] * w[j]
    if b is not None:
        y = y + b
    return y, xp[:, L:]


def _x_query_chunk(n_q, batch):
    cap = max(1, min(n_q, 128 // batch))
    return max(d for d in range(1, cap + 1) if n_q % d == 0)


def _x_moba_attention(q, k, v, q0, bias_t):
    B, Lq, Hq, hd = q.shape
    Lk, KV = k.shape[1], k.shape[2]
    G = Hq // KV
    n_full = Lk // MOBA_BLOCK
    n_sel = min(MOBA_TOPK, n_full)
    head_kv = jnp.arange(Hq) // G
    kmean = k[:, :n_full * MOBA_BLOCK].reshape(B, n_full, MOBA_BLOCK, KV, hd).mean(2, dtype=F32)[:, :, head_kv]
    qc = _x_query_chunk(Lq, B)
    nq = Lq // qc
    q_chunks = q.reshape(B, nq, qc, Hq, hd).swapaxes(0, 1)
    starts = q0 + qc * jnp.arange(nq, dtype=jnp.int32)
    b_ix = jnp.arange(B)[:, None, None, None, None]
    h_ix = jnp.arange(Hq)[None, :, None, None, None]
    kv_ix = head_kv[None, :, None, None, None]
    offs = jnp.arange(MOBA_BLOCK)
    scale = hd ** -0.5

    def one_chunk(args):
        qb, s0 = args
        q_pos = s0 + jnp.arange(qc)
        own = jnp.broadcast_to((q_pos // MOBA_BLOCK)[None, None, :, None], (B, Hq, qc, 1))
        gate = jnp.einsum('bqhd,bnhd->bhqn', qb, kmean, preferred_element_type=F32)
        gate = jnp.where(jnp.arange(n_full) < own, gate, -jnp.inf)
        _, sel = lax.top_k(gate, n_sel)
        blocks = jnp.concatenate([sel, own], -1)
        live = jnp.concatenate([sel < own, jnp.ones_like(own, dtype=bool)], -1)
        k_pos = blocks[..., None] * MOBA_BLOCK + offs
        rows = jnp.minimum(k_pos, Lk - 1)
        kg = k[b_ix, rows, kv_ix]
        vg = v[b_ix, rows, kv_ix]
        rel = q_pos[None, None, :, None, None] - k_pos
        s = jnp.einsum('bqhd,bhqsrd->bhqsr', qb, kg, preferred_element_type=F32) * scale
        s = s + bias_t[h_ix, _x_t5_bucket(rel)].astype(F32)
        s = jnp.where(live[..., None] & (rel >= 0), s, -jnp.inf)
        n_keys = s.shape[3] * MOBA_BLOCK
        p = jax.nn.softmax(s.reshape(B, Hq, qc, n_keys), axis=-1).reshape(s.shape)
        o = jnp.einsum('bhqsr,bhqsrd->bqhd', p, vg, preferred_element_type=F32)
        return o.astype(q.dtype)

    out = lax.map(one_chunk, (q_chunks, starts))
    return out.swapaxes(0, 1).reshape(B, Lq, Hq, hd)


def _x_gated_delta_rule(q, k, v, g, beta, s0):
    B, L, H, dk = q.shape
    dv = v.shape[-1]
    C = min(GDN_CHUNK, L)
    pad = (-L) % C
    N = (L + pad) // C

    def chunks(t):
        t = jnp.pad(t.astype(F32), [(0, 0), (0, pad)] + [(0, 0)] * (t.ndim - 2))
        t = t.reshape((B, N, C) + t.shape[2:])
        return jnp.swapaxes(jnp.swapaxes(t, 0, 1), 2, 3)

    q = chunks(q) * dk ** -0.5
    k = chunks(k)
    v = chunks(v)
    g = chunks(g)
    beta = chunks(beta)
    gc = jnp.cumsum(g, -1)
    incl = jnp.tril(jnp.ones((C, C), bool))
    strict = jnp.tril(jnp.ones((C, C), bool), -1)
    decay = jnp.exp(jnp.where(incl, gc[..., :, None] - gc[..., None, :], -jnp.inf))
    kb = k * beta[..., None]
    lower = jnp.where(strict, jnp.einsum('nbhcd,nbhsd->nbhcs', kb, k) * decay, 0.0)
    t_mat = lower + jnp.eye(C, dtype=F32)
    rhs = jnp.concatenate([v * beta[..., None], kb * jnp.exp(gc)[..., None]], -1)
    sol = lax.linalg.triangular_solve(t_mat, rhs, left_side=True, lower=True, unit_diagonal=True)
    u, w = sol[..., :dv], sol[..., dv:]
    qk = jnp.einsum('nbhcd,nbhsd->nbhcs', q, k) * decay
    qg = q * jnp.exp(gc)[..., None]
    kd = k * jnp.exp(gc[..., -1:] - gc)[..., None]
    glast = jnp.exp(gc[..., -1])[..., None, None]

    def step(S, xs):
        u_n, w_n, qk_n, qg_n, kd_n, gl_n = xs
        v_new = u_n - jnp.einsum('bhcd,bhde->bhce', w_n, S)
        o_n = jnp.einsum('bhcd,bhde->bhce', qg_n, S) + jnp.einsum('bhcs,bhse->bhce', qk_n, v_new)
        S = S * gl_n + jnp.einsum('bhcd,bhce->bhde', kd_n, v_new)
        return S, o_n

    S, o = lax.scan(step, s0.astype(F32), (u, w, qk, qg, kd, glast))
    o = jnp.swapaxes(jnp.swapaxes(o, 2, 3), 0, 1).reshape(B, N * C, H, dv)[:, :L]
    return o, S


def _x_sink_attention(q, k, v, q_pos, k_pos, sinks, bias_t):
    *lead, Lq, Hq, hd = q.shape
    KV = k.shape[-2]
    G = Hq // KV
    qg = q.reshape(*lead, Lq, KV, G, hd)
    s = jnp.einsum('...qkgd,...skd->...kgqs', qg, k, preferred_element_type=F32) * hd ** -0.5
    rel = q_pos[..., :, None] - k_pos[..., None, :]
    ok = (rel >= 0) & (rel <= WINDOW) & (k_pos[..., None, :] >= 0)
    bias = jnp.moveaxis(bias_t.reshape(KV, G, T5_BUCKETS)[:, :, _x_t5_bucket(rel)], (0, 1), (-4, -3))
    s = jnp.where(ok[..., None, None, :, :], s + bias.astype(F32), -jnp.inf)
    sink = sinks.astype(F32).reshape(KV, G)[:, :, None, None]
    m = jnp.maximum(s.max(-1, keepdims=True), sink)
    p = jnp.exp(s - m)
    den = p.sum(-1, keepdims=True) + jnp.exp(sink - m)
    o = jnp.einsum('...kgqs,...skd->...qkgd', p / den, v, preferred_element_type=F32)
    return o.reshape(*lead, Lq, Hq, hd).astype(q.dtype)


def _x_swa_prompt(q, k, v, sinks, bias_t):
    B, L, Hq, hd = q.shape
    W = WINDOW
    nb = L // W

    def band(t):
        tb = t.reshape(B, nb, W, t.shape[2], hd)
        prev = jnp.concatenate([jnp.zeros_like(tb[:, :1]), tb[:, :-1]], axis=1)
        return jnp.concatenate([prev, tb], axis=2)

    base = W * jnp.arange(nb)[:, None]
    q_pos = base + jnp.arange(W)
    k_pos = base - W + jnp.arange(2 * W)
    qb = q.reshape(B, nb, W, Hq, hd)
    o = _x_sink_attention(qb, band(k), band(v), q_pos, k_pos, sinks, bias_t)
    return o.reshape(B, nb * W, Hq, hd)


def _x_lru_combine(x, y):
    a1, b1 = x
    a2, b2 = y
    return a1 * a2, a2 * b1 + b2


def _x_rg_lru(x, h0, w_a, b_a, w_x, b_x, lam):
    B, L, _ = x.shape
    xf = x.astype(F32)
    xb = xf.reshape(B, L, LRU_BLOCKS, D_RNN // LRU_BLOCKS)
    r = jax.nn.sigmoid(jnp.einsum('blni,nij->blnj', xb, w_a).reshape(B, L, D_RNN) + b_a)
    i = jax.nn.sigmoid(jnp.einsum('blni,nij->blnj', xb, w_x).reshape(B, L, D_RNN) + b_x)
    log_a = -LRU_C * r * jax.nn.softplus(-lam.astype(F32))
    a = jnp.exp(log_a)
    b = jnp.sqrt(-jnp.expm1(2.0 * log_a)) * (i * xf)
    b = b.at[:, 0].add(a[:, 0] * h0.astype(F32))
    _, h = lax.associative_scan(_x_lru_combine, (a, b), axis=1)
    return h


def _x_even_rest(p, b, l, past_k, past_v, s0, conv0, conv_w, a_log, dt_bias, norm_w, bias_t, o_a=None):
    qa, ka, va, qkv, z, beta_in, decay_in = p
    if o_a is None:
        qa4 = qa.reshape(b, l, N_ATT_HEADS, HEAD_DIM)
        ka4 = ka.reshape(b, l, KV_HEADS_A, HEAD_DIM)
        va4 = va.reshape(b, l, KV_HEADS_A, HEAD_DIM)
        past = past_k.shape[1]
        k_all = jnp.concatenate([past_k, ka4], axis=1)
        v_all = jnp.concatenate([past_v, va4], axis=1)
        o_a = _x_moba_attention(qa4, k_all, v_all, past, bias_t).reshape(b, l, -1)
    c, conv_new = _x_causal_conv(qkv, conv0, conv_w)
    c = jax.nn.silu(c)
    qb, kb, vb = jnp.split(c, [512, 1024], axis=-1)
    qb = _x_l2norm(qb.reshape(b, l, GDN_HEADS, 64))
    kb = _x_l2norm(kb.reshape(b, l, GDN_HEADS, 64))
    vb = vb.reshape(b, l, GDN_HEADS, 64)
    beta = jax.nn.sigmoid(beta_in)
    g = -jnp.exp(a_log) * jax.nn.softplus(decay_in + dt_bias)
    o_b, s_new = _x_gated_delta_rule(qb, kb, vb, g, beta, s0)
    gate = jax.nn.silu(z.reshape(b, l, GDN_HEADS, 64))
    o_b = _x_rms(o_b, norm_w) * gate
    mixed = jnp.concatenate([o_a, o_b.reshape(b, l, -1)], axis=-1)
    return mixed, s_new, conv_new


def _x_odd_rest(p, b, l, buf_k, buf_v, h0, conv0, past_len, sinks, conv_w, conv_b,
                w_a, b_a, w_x, b_x, lam, bias_t):
    qc, kc, vc, xr, gr = p
    qc = qc.reshape(b, l, N_ATT_HEADS, HEAD_DIM)
    kc = kc.reshape(b, l, KV_HEADS_C, HEAD_DIM)
    vc = vc.reshape(b, l, KV_HEADS_C, HEAD_DIM)
    if buf_k is None:
        o_c = _x_swa_prompt(qc, kc, vc, sinks, bias_t)
        k_keep, v_keep = kc[:, -WINDOW:], vc[:, -WINDOW:]
    else:
        nbuf = buf_k.shape[1]
        k_all = jnp.concatenate([buf_k, kc], axis=1)
        v_all = jnp.concatenate([buf_v, vc], axis=1)
        q_pos = past_len + jnp.arange(l)
        k_pos = past_len - nbuf + jnp.arange(nbuf + l)
        o_c = _x_sink_attention(qc, k_all, v_all, q_pos, k_pos, sinks, bias_t)
        k_keep, v_keep = k_all[:, -nbuf:], v_all[:, -nbuf:]
    xc, conv_new = _x_causal_conv(xr, conv0, conv_w, conv_b)
    h = _x_rg_lru(xc, h0, w_a, b_a, w_x, b_x, lam)
    o_d = h * jax.nn.gelu(gr)
    mixed = jnp.concatenate([o_c.reshape(b, l, -1), o_d], axis=-1)
    return mixed, k_keep, v_keep, h[:, -1], conv_new


TM = 768
TM_SEG = 512


def _pad_cols(w, n):
    return jnp.pad(w, ((0, 0), (0, n - w.shape[1])))


def _x_kernel_fallback(x_prompt, x_sample, cache_k_moba, cache_v_moba, state_gdn, state_gdn_conv, cache_k_swa, cache_v_swa, state_lru, state_lru_conv, page_table, w_in_even, w_out_even, gdn_conv_w, gdn_a_log, gdn_dt_bias, gdn_norm_w, w_in_odd, w_out_odd, swa_sinks, lru_conv_w, lru_conv_b, lru_w_a, lru_b_a, lru_w_x, lru_b_x, lru_lambda, rel_bias, ln_g, ln_b, router_w, router_b, w_gate, w_up, w_down):
    n_pr, seq, d = x_prompt.shape
    n_dec, dec_seq, _ = x_sample.shape
    tp = n_pr * seq
    ts = n_dec * dec_seq
    past_len = page_table.shape[1] * PAGE_SIZE
    bias_t = rel_bias.T
    h = jnp.concatenate([x_prompt.reshape(tp, d), x_sample.reshape(ts, d)], axis=0)

    rw_pad = _pad_cols(router_w, LANES)
    rb_pad = _pad_cols(router_b.reshape(1, -1), LANES)
    wg_b, wu_b, wd_b = w_gate.astype(BF16), w_up.astype(BF16), w_down.astype(BF16)

    p_even = sum(EVEN_SIZES)
    n_even = 3200
    w0 = _pad_cols(w_in_even[0], n_even).astype(BF16)
    proj = _mm(h, w0, TM, 640)
    splits = np.cumsum(EVEN_SIZES)[:-1].tolist()
    pp = [t.reshape(n_pr, seq, -1) for t in jnp.split(proj[:tp, :p_even], splits, axis=-1)]
    ps = [t.reshape(n_dec, dec_seq, -1) for t in jnp.split(proj[tp:, :p_even], splits, axis=-1)]
    moba_k_prompt = pp[1].reshape(n_pr, seq, 1, KV_HEADS_A, HEAD_DIM)
    moba_v_prompt = pp[2].reshape(n_pr, seq, 1, KV_HEADS_A, HEAD_DIM)
    moba_k_sample = ps[1].reshape(n_dec, dec_seq, 1, KV_HEADS_A, HEAD_DIM)
    moba_v_sample = ps[2].reshape(n_dec, dec_seq, 1, KV_HEADS_A, HEAD_DIM)

    o_a_p = _moba_prompt(proj, rel_bias, n_pr, seq, 0, 4, 6).reshape(n_pr, seq, -1)
    mixed_p, gdn_sp, gdn_cp = _x_even_rest(
        pp, n_pr, seq, None, None, jnp.zeros((n_pr, GDN_HEADS, 64, 64), F32),
        jnp.zeros((n_pr, CONV_WIDTH - 1, GDN_CONV_CH), F32),
        gdn_conv_w[0], gdn_a_log[0], gdn_dt_bias[0], gdn_norm_w[0], bias_t, o_a=o_a_p)
    past_k = cache_k_moba[page_table][:, :, :, 0].reshape(n_dec, past_len, KV_HEADS_A, HEAD_DIM)
    past_v = cache_v_moba[page_table][:, :, :, 0].reshape(n_dec, past_len, KV_HEADS_A, HEAD_DIM)
    mixed_s, gdn_ss, gdn_cs = _x_even_rest(
        ps, n_dec, dec_seq, past_k, past_v, state_gdn[0], state_gdn_conv[0],
        gdn_conv_w[0], gdn_a_log[0], gdn_dt_bias[0], gdn_norm_w[0], bias_t)
    mixed = jnp.concatenate([mixed_p.reshape(tp, -1), mixed_s.reshape(ts, -1)], axis=0)
    h = _mm_ln(mixed, w_out_even[0].astype(BF16), h, ln_g[0, 0], ln_b[0, 0], TM)
    h = _moe_ln(h, rw_pad, rb_pad, wg_b[0], wu_b[0], wd_b[0], ln_g[0, 1], ln_b[0, 1], TM)

    p_odd = sum(ODD_SIZES)
    proj = _mm(h, w_in_odd[0].astype(BF16), TM, 896)
    splits = np.cumsum(ODD_SIZES)[:-1].tolist()
    pp = [t.reshape(n_pr, seq, -1) for t in jnp.split(proj[:tp], splits, axis=-1)]
    ps = [t.reshape(n_dec, dec_seq, -1) for t in jnp.split(proj[tp:], splits, axis=-1)]
    mixed_p, swa_kp, swa_vp, lru_p, lru_cp = _x_odd_rest(
        pp, n_pr, seq, None, None, jnp.zeros((n_pr, D_RNN), F32),
        jnp.zeros((n_pr, CONV_WIDTH - 1, D_RNN), F32), 0,
        swa_sinks[0], lru_conv_w[0], lru_conv_b[0], lru_w_a[0], lru_b_a[0], lru_w_x[0], lru_b_x[0],
        lru_lambda[0], bias_t)
    mixed_s, swa_ks, swa_vs, lru_s, lru_cs = _x_odd_rest(
        ps, n_dec, dec_seq, cache_k_swa[0], cache_v_swa[0], state_lru[0], state_lru_conv[0], past_len,
        swa_sinks[0], lru_conv_w[0], lru_conv_b[0], lru_w_a[0], lru_b_a[0], lru_w_x[0], lru_b_x[0],
        lru_lambda[0], bias_t)
    mixed = jnp.concatenate([mixed_p.reshape(tp, -1), mixed_s.reshape(ts, -1)], axis=0)
    h = _mm_ln(mixed, w_out_odd[0].astype(BF16), h, ln_g[1, 0], ln_b[1, 0], TM)
    h = _moe_ln(h, rw_pad, rb_pad, wg_b[1], wu_b[1], wd_b[1], ln_g[1, 1], ln_b[1, 1], TM)

    y_prompt = h[:tp].reshape(n_pr, seq, d)
    y_sample = h[tp:].reshape(n_dec, dec_seq, d)
    return (y_prompt, y_sample, moba_k_prompt, moba_v_prompt, moba_k_sample, moba_v_sample,
            gdn_sp[None], gdn_ss[None], gdn_cp[None], gdn_cs[None],
            swa_kp[None], swa_vp[None], swa_ks[None], swa_vs[None],
            lru_p[None], lru_s[None], lru_cp[None], lru_cs[None])


def _seq_tails(proj, n_batch, seq, n_rows, col, width):
    return jnp.stack([proj[(b + 1) * seq - n_rows:(b + 1) * seq, col:col + width] for b in range(n_batch)])


GDN_DEC_PERIOD = 8
EVEN_COLS = 3200


def kernel(x_prompt, x_sample, cache_k_moba, cache_v_moba, state_gdn, state_gdn_conv, cache_k_swa, cache_v_swa, state_lru, state_lru_conv, page_table, w_in_even, w_out_even, gdn_conv_w, gdn_a_log, gdn_dt_bias, gdn_norm_w, w_in_odd, w_out_odd, swa_sinks, lru_conv_w, lru_conv_b, lru_w_a, lru_b_a, lru_w_x, lru_b_x, lru_lambda, rel_bias, ln_g, ln_b, router_w, router_b, w_gate, w_up, w_down):
    n_pr, seq, d = x_prompt.shape
    n_dec, dec_seq, _ = x_sample.shape
    tp = n_pr * seq
    ts = n_dec * dec_seq
    n_pool = cache_k_moba.shape[0]
    kva = KV_HEADS_A * HEAD_DIM
    kvc = KV_HEADS_C * HEAD_DIM
    x_segs = [x_prompt.reshape(tp, d), x_sample.reshape(ts, d)]

    rw_pad = _pad_cols(router_w, LANES).T
    rb_pad = router_b.reshape(N_EXPERTS, 1)
    experts = [_group_experts(w_gate[layer], w_up[layer], w_down[layer]) for layer in range(DEPTH)]

    proj = _mm(x_segs, _pad_cols(w_in_even[0], EVEN_COLS).astype(BF16), TM_SEG, 640)
    ps = proj[tp:]
    moba_k_prompt = proj[:tp, 512:512 + kva].reshape(n_pr, seq, 1, KV_HEADS_A, HEAD_DIM)
    moba_v_prompt = proj[:tp, 768:768 + kva].reshape(n_pr, seq, 1, KV_HEADS_A, HEAD_DIM)
    k_new = ps[:, 512:512 + kva].reshape(n_dec, dec_seq, kva)
    v_new = ps[:, 768:768 + kva].reshape(n_dec, dec_seq, kva)
    moba_k_sample = k_new.reshape(n_dec, dec_seq, 1, KV_HEADS_A, HEAD_DIM)
    moba_v_sample = v_new.reshape(n_dec, dec_seq, 1, KV_HEADS_A, HEAD_DIM)

    oa_p = _moba_prompt(proj, rel_bias, n_pr, seq, 0, 4, 6)
    oa_s = _moba_sample(ps[:, :512].reshape(n_dec, dec_seq, N_ATT_HEADS, HEAD_DIM), k_new, v_new,
                        jnp.swapaxes(cache_k_moba.reshape(n_pool, PAGE_SIZE, -1), 1, 2),
                        jnp.swapaxes(cache_v_moba.reshape(n_pool, PAGE_SIZE, -1), 1, 2),
                        page_table, rel_bias, layer=0)

    parts = _gdn_front(proj, 8, 24, gdn_conv_w[0], gdn_a_log[0], gdn_dt_bias[0], n_pr, seq, GDN_CHUNK)
    o_p, s_fin = _gdn_scan(*parts, GDN_CHUNK, GDN_HEADS, 512)
    ob_p = _gdn_post(o_p, proj, 5, gdn_norm_w[0], n_pr, seq, 512)
    gdn_state_prompt = s_fin.reshape(1, n_pr, GDN_HEADS, HEAD_DIM, HEAD_DIM)
    gdn_conv_prompt = _seq_tails(proj, n_pr, seq, 3, 1024, GDN_CONV_CH)[None]

    period = GDN_DEC_PERIOD
    lead = period - dec_seq
    qkv_s = ps[:, 1024:1024 + GDN_CONV_CH].reshape(n_dec, dec_seq, GDN_CONV_CH)
    x8 = jnp.concatenate([jnp.zeros((n_dec, lead - 3, GDN_CONV_CH), F32), state_gdn_conv[0], qkv_s], axis=1)
    tail8 = jnp.pad(ps[:, 2560:EVEN_COLS].reshape(n_dec, dec_seq, EVEN_COLS - 2560), ((0, 0), (lead, 0), (0, 0)))
    xs = jnp.concatenate([x8, tail8], axis=-1).reshape(n_dec * period, EVEN_COLS - 1024)
    parts = _gdn_front(xs, 0, 16, gdn_conv_w[0], gdn_a_log[0], gdn_dt_bias[0], 1, n_dec * period, period,
                       period, lead)
    o_s, gdn_state_sample = _gdn_step(state_gdn[0], *parts, period)
    ob_s = _gdn_post(o_s, xs, 3, gdn_norm_w[0], 1, n_dec * period, 512)
    ob_s = ob_s.reshape(n_dec, period, 512)[:, lead:].reshape(ts, 512)
    gdn_conv_sample = qkv_s[None, :, dec_seq - 3:]

    h = _mm_ln([oa_p, oa_s.reshape(ts, 512)], [ob_p, ob_s], w_out_even[0].astype(BF16), x_segs,
               ln_g[0, 0], ln_b[0, 0], TM_SEG)
    h = _moe_ln(h, rw_pad, rb_pad, *experts[0], ln_g[0, 1], ln_b[0, 1], TM)

    proj = _mm([h], w_in_odd[0].astype(BF16), TM, 896)
    ps = proj[tp:]
    wa_bd, wx_bd = _block_diag(lru_w_a[0]), _block_diag(lru_w_x[0])
    oc_p = _swa_prompt(proj, rel_bias, swa_sinks[0], n_pr, seq, 0, 4, 5)
    od_p, lru_p = _lru_prompt(proj, n_pr, seq, 6, 10, lru_conv_w[0], lru_conv_b[0], wa_bd, lru_b_a[0],
                              wx_bd, lru_b_x[0], lru_lambda[0])
    swa_k_prompt = _seq_tails(proj, n_pr, seq, WINDOW, 512, kvc).reshape(1, n_pr, WINDOW, KV_HEADS_C, HEAD_DIM)
    swa_v_prompt = _seq_tails(proj, n_pr, seq, WINDOW, 640, kvc).reshape(1, n_pr, WINDOW, KV_HEADS_C, HEAD_DIM)
    lru_conv_prompt = _seq_tails(proj, n_pr, seq, 3, 768, D_RNN)[None]

    nbuf = cache_k_swa.shape[2]
    kc_s = ps[:, 512:512 + kvc].reshape(n_dec, dec_seq, kvc)
    vc_s = ps[:, 640:640 + kvc].reshape(n_dec, dec_seq, kvc)
    buf_k = cache_k_swa[0].reshape(n_dec, nbuf, kvc)
    buf_v = cache_v_swa[0].reshape(n_dec, nbuf, kvc)
    oc_s = _swa_sample(ps[:, :512].reshape(n_dec, dec_seq, N_ATT_HEADS, HEAD_DIM), kc_s, vc_s, buf_k, buf_v,
                       rel_bias, swa_sinks[0])
    swa_k_sample = jnp.concatenate([buf_k, kc_s], axis=1)[:, dec_seq:].reshape(1, n_dec, nbuf, KV_HEADS_C, HEAD_DIM)
    swa_v_sample = jnp.concatenate([buf_v, vc_s], axis=1)[:, dec_seq:].reshape(1, n_dec, nbuf, KV_HEADS_C, HEAD_DIM)
    xr_s = ps[:, 768:768 + D_RNN].reshape(n_dec, dec_seq, D_RNN)
    gr_s = ps[:, 1280:1280 + D_RNN].reshape(n_dec, dec_seq, D_RNN)
    od_s, lru_s = _lru_sample(xr_s, gr_s, state_lru[0], state_lru_conv[0], lru_conv_w[0], lru_conv_b[0],
                              wa_bd, lru_b_a[0], wx_bd, lru_b_x[0], lru_lambda[0])
    lru_conv_sample = jnp.concatenate([state_lru_conv[0], xr_s], axis=1)[None, :, dec_seq:]

    h = _mm_ln([oc_p, oc_s.reshape(ts, 512)], [od_p, od_s.reshape(ts, 512)], w_out_odd[0].astype(BF16), [h],
               ln_g[1, 0], ln_b[1, 0], TM_SEG)
    h = _moe_ln(h, rw_pad, rb_pad, *experts[1], ln_g[1, 1], ln_b[1, 1], TM)

    y_prompt = h[:tp].reshape(n_pr, seq, d)
    y_sample = h[tp:].reshape(n_dec, dec_seq, d)
    return (y_prompt, y_sample, moba_k_prompt, moba_v_prompt, moba_k_sample, moba_v_sample,
            gdn_state_prompt, gdn_state_sample[None], gdn_conv_prompt, gdn_conv_sample,
            swa_k_prompt, swa_v_prompt, swa_k_sample, swa_v_sample,
            lru_p[None], lru_s[None], lru_conv_prompt, lru_conv_sample)
```

```python
import functools
import math

import jax
import jax.numpy as jnp
import numpy as np
from jax import lax
from jax.experimental import pallas as pl
from jax.experimental.pallas import tpu as pltpu

F32 = jnp.float32
BF16 = jnp.bfloat16

D_MODEL = 1024
HEAD_DIM = 64
N_ATT_HEADS = 8
KV_HEADS_A = 4
KV_HEADS_C = 2
MOBA_BLOCK = 256
MOBA_TOPK = 3
PAGE_SIZE = 128
GDN_HEADS = 8
GDN_CHUNK = 64
GDN_CONV_CH = 1536
CONV_WIDTH = 4
WINDOW = 128
D_RNN = 512
LRU_BLOCKS = 8
LRU_C = 8.0
T5_BUCKETS = 32
T5_MAX_DIST = 128
N_EXPERTS = 16
N_GROUPS = 4
EXPERTS_PER_GROUP = 4
D_EXPERT = 256
DEPTH = 2
DEEPNORM_ALPHA = (2 * DEPTH) ** 0.25
LN_EPS = 1e-5
NORM_EPS = 1e-6
EVEN_SIZES = (512, 256, 256, 1536, 512, 8, 8)
ODD_SIZES = (512, 128, 128, 512, 512)

LANES = 128
VMEM_LIMIT = 56 * 1024 * 1024
NEG_INF = float("-inf")


def _cparams(sem):
    return pltpu.CompilerParams(dimension_semantics=sem, vmem_limit_bytes=VMEM_LIMIT)


def _dot(a, b):
    return jnp.dot(a.astype(BF16), b.astype(BF16), preferred_element_type=F32)


def _dot_nt(a, b):
    return lax.dot_general(a.astype(BF16), b.astype(BF16), (((1,), (1,)), ((), ())),
                           preferred_element_type=F32)


def _seg_specs(segs, tm):
    specs, bounds, lo = [], [], 0
    for a in segs:
        assert a.shape[0] % tm == 0
        nt = a.shape[0] // tm
        specs.append(pl.BlockSpec((tm, a.shape[1]), lambda i, lo=lo, nt=nt: (jnp.clip(i - lo, 0, nt - 1), 0)))
        lo += nt
        bounds.append(lo)
    return specs, bounds


def _seg_tile(i, refs, bounds):
    x = refs[-1][...]
    for r, hi in zip(reversed(refs[:-1]), reversed(bounds[:-1])):
        x = jnp.where(i < hi, r[...], x)
    return x


def _mm_kernel(*refs, tn, bounds):
    ns = len(bounds)
    w_ref, o_ref = refs[ns], refs[ns + 1]
    xb = _seg_tile(pl.program_id(0), refs[:ns], bounds).astype(BF16)
    for c in range(w_ref.shape[1] // tn):
        o_ref[:, c * tn:(c + 1) * tn] = jnp.dot(xb, w_ref[:, c * tn:(c + 1) * tn], preferred_element_type=F32)


def _mm(x_segs, w, tm, tn):
    k, n = w.shape
    assert n % tn == 0
    specs, bounds = _seg_specs(x_segs, tm)
    m = bounds[-1] * tm
    return pl.pallas_call(
        functools.partial(_mm_kernel, tn=tn, bounds=tuple(bounds)),
        grid=(bounds[-1],),
        in_specs=specs + [pl.BlockSpec((k, n), lambda i: (0, 0))],
        out_specs=pl.BlockSpec((tm, n), lambda i: (i, 0)),
        out_shape=jax.ShapeDtypeStruct((m, n), F32),
        compiler_params=_cparams(("parallel",)),
        name="in_proj",
    )(*x_segs, w)


def _layer_norm_rows(z, g, b):
    mu = jnp.mean(z, axis=-1, keepdims=True)
    zc = z - mu
    var = jnp.mean(zc * zc, axis=-1, keepdims=True)
    return zc * lax.rsqrt(var + LN_EPS) * g + b


def _mm_ln_kernel(*refs, bounds_a, bounds_b, bounds_h):
    na, nb, nh = len(bounds_a), len(bounds_b), len(bounds_h)
    xa_refs, xb_refs, h_refs = refs[:na], refs[na:na + nb], refs[na + nb:na + nb + nh]
    wa_ref, wb_ref, g_ref, b_ref, o_ref = refs[na + nb + nh:]
    i = pl.program_id(0)
    f = (jnp.dot(_seg_tile(i, xa_refs, bounds_a).astype(BF16), wa_ref[...], preferred_element_type=F32)
         + jnp.dot(_seg_tile(i, xb_refs, bounds_b).astype(BF16), wb_ref[...], preferred_element_type=F32))
    o_ref[...] = _layer_norm_rows(DEEPNORM_ALPHA * _seg_tile(i, h_refs, bounds_h) + f, g_ref[...], b_ref[...])


def _mm_ln(xa_segs, xb_segs, w, h_segs, g, b, tm):
    ka, kb = xa_segs[0].shape[1], xb_segs[0].shape[1]
    n = w.shape[1]
    assert ka == kb
    sa, ba = _seg_specs(xa_segs, tm)
    sb, bb = _seg_specs(xb_segs, tm)
    sh, bh = _seg_specs(h_segs, tm)
    assert ba[-1] == bb[-1] == bh[-1]
    return pl.pallas_call(
        functools.partial(_mm_ln_kernel, bounds_a=tuple(ba), bounds_b=tuple(bb), bounds_h=tuple(bh)),
        grid=(ba[-1],),
        in_specs=sa + sb + sh + [pl.BlockSpec((ka, n), lambda i: (0, 0)),
                                 pl.BlockSpec((kb, n), lambda i: (1, 0)),
                                 pl.BlockSpec((1, n), lambda i: (0, 0)),
                                 pl.BlockSpec((1, n), lambda i: (0, 0))],
        out_specs=pl.BlockSpec((tm, n), lambda i: (i, 0)),
        out_shape=jax.ShapeDtypeStruct((ba[-1] * tm, n), F32),
        compiler_params=_cparams(("parallel",)),
        name="out_proj_ln",
    )(*xa_segs, *xb_segs, *h_segs, w, w, g.reshape(1, n), b.reshape(1, n))


def _row_max_first(vals, lane):
    mx = jnp.max(vals, axis=1, keepdims=True)
    idx = jnp.min(jnp.where(vals == mx, lane, LANES), axis=1, keepdims=True)
    return mx, idx


def _moe_ln_kernel(t_ref, rw_ref, rb_ref, wg_ref, wu_ref, wd_ref, g_ref, b_ref, o_ref,
                   gates_ref, tb_ref, hb_ref, acc_ref):
    e = pl.program_id(1)

    @pl.when(e == 0)
    def _route():
        t = t_ref[...]
        logits = lax.dot_general(rw_ref[...], t, (((1,), (1,)), ((), ())), preferred_element_type=F32,
                                 precision=lax.Precision.HIGHEST)[:N_EXPERTS]
        s = jax.nn.sigmoid(logits)
        sb = s + rb_ref[...]
        row = lax.broadcasted_iota(jnp.int32, sb.shape, 0)
        grp = row // EXPERTS_PER_GROUP

        def max_first(vals):
            mx = jnp.max(vals, axis=0, keepdims=True)
            return mx, jnp.min(jnp.where(vals == mx, row, N_EXPERTS), axis=0, keepdims=True)

        best = None
        gsel = None
        for gi in range(N_GROUPS):
            vals = jnp.where(grp == gi, sb, NEG_INF)
            m1, i1 = max_first(vals)
            m2 = jnp.max(jnp.where(row == i1, NEG_INF, vals), axis=0, keepdims=True)
            score = m1 + m2
            if gi == 0:
                best, gsel = score, jnp.zeros_like(i1)
            else:
                upd = score > best
                best = jnp.where(upd, score, best)
                gsel = jnp.where(upd, gi, gsel)
        vals = jnp.where(grp == gsel, sb, NEG_INF)
        _, i1 = max_first(vals)
        _, i2 = max_first(jnp.where(row == i1, NEG_INF, vals))
        s1 = jnp.sum(jnp.where(row == i1, s, 0.0), axis=0, keepdims=True)
        s2 = jnp.sum(jnp.where(row == i2, s, 0.0), axis=0, keepdims=True)
        den = s1 + s2
        gates_t = jnp.where(row == i1, s1 / den, 0.0) + jnp.where(row == i2, s2 / den, 0.0)
        gates_t = jnp.concatenate([gates_t, jnp.zeros((LANES - N_EXPERTS, gates_t.shape[1]), F32)], axis=0)
        gates_ref[...] = gates_t.T
        tb_ref[...] = t.astype(BF16)
        acc_ref[...] = jnp.zeros_like(acc_ref)

    tb = tb_ref[...]
    gates = gates_ref[...]
    lane = lax.broadcasted_iota(jnp.int32, gates.shape, 1)
    for j in range(EXPERTS_PER_GROUP):
        sl = slice(j * D_EXPERT, (j + 1) * D_EXPERT)
        ge = jnp.sum(jnp.where(lane == e * EXPERTS_PER_GROUP + j, gates, 0.0), axis=1, keepdims=True)
        g = jnp.dot(tb, wg_ref[j], preferred_element_type=F32)
        u = jnp.dot(tb, wu_ref[j], preferred_element_type=F32)
        hb_ref[:, sl] = ((g * jax.nn.sigmoid(g)) * u * ge).astype(BF16)
    acc_ref[...] += jnp.dot(hb_ref[...], wd_ref[...], preferred_element_type=F32)

    @pl.when(e == pl.num_programs(1) - 1)
    def _finish():
        o_ref[...] = _layer_norm_rows(DEEPNORM_ALPHA * t_ref[...] + acc_ref[...], g_ref[...], b_ref[...])


def _group_experts(w_gate, w_up, w_down):
    ne, d, f = w_gate.shape
    es = EXPERTS_PER_GROUP
    return w_gate.astype(BF16), w_up.astype(BF16), w_down.astype(BF16).reshape(ne // es, es * f, d)


def _moe_ln(t, router_w_pad, router_b_pad, wg, wu, wd, g, b, tm):
    m, d = t.shape
    ng, gf, _ = wd.shape
    es, f = EXPERTS_PER_GROUP, wg.shape[2]
    assert m % tm == 0
    return pl.pallas_call(
        _moe_ln_kernel,
        grid=(m // tm, ng),
        in_specs=[pl.BlockSpec((tm, d), lambda i, e: (i, 0)),
                  pl.BlockSpec((LANES, d), lambda i, e: (0, 0)),
                  pl.BlockSpec((N_EXPERTS, 1), lambda i, e: (0, 0)),
                  pl.BlockSpec((es, d, f), lambda i, e: (e, 0, 0)),
                  pl.BlockSpec((es, d, f), lambda i, e: (e, 0, 0)),
                  pl.BlockSpec((None, gf, d), lambda i, e: (e, 0, 0)),
                  pl.BlockSpec((1, d), lambda i, e: (0, 0)),
                  pl.BlockSpec((1, d), lambda i, e: (0, 0))],
        out_specs=pl.BlockSpec((tm, d), lambda i, e: (i, 0)),
        out_shape=jax.ShapeDtypeStruct((m, d), F32),
        scratch_shapes=[pltpu.VMEM((tm, LANES), F32),
                        pltpu.VMEM((tm, d), BF16),
                        pltpu.VMEM((tm, gf), BF16),
                        pltpu.VMEM((tm, d), F32)],
        compiler_params=_cparams(("parallel", "arbitrary")),
        name="moe_ln",
    )(t, router_w_pad, router_b_pad, wg, wu, wd, g.reshape(1, d), b.reshape(1, d))


def _t5_bucket_np(rel):
    n = np.maximum(rel, 0)
    max_exact = T5_BUCKETS // 2
    nf = np.maximum(n, 1).astype(np.float32)
    large = max_exact + (np.log(nf / np.float32(max_exact)) / np.float32(math.log(T5_MAX_DIST / max_exact))
                         * np.float32(T5_BUCKETS - max_exact)).astype(np.int32)
    large = np.minimum(large, T5_BUCKETS - 1)
    return np.where(n < max_exact, n, large).astype(np.int32)


def _old_moba_prompt_kernel(q_ref, k_ref, v_ref, bd_ref, bp_ref, bf_ref, o_ref,
                        kmean_ref, sel_ref, m_ref, l_ref, acc_ref):
    i = pl.program_id(2)
    blk = MOBA_BLOCK
    nblk = k_ref.shape[0] // blk
    hd = HEAD_DIM

    @pl.when(i == 0)
    def _means():
        for n in range(nblk):
            kmean_ref[n:n + 1, :] = jnp.mean(k_ref[n * blk:(n + 1) * blk, :], axis=0, keepdims=True)

    q4 = q_ref[...] * (hd ** -0.5)
    for kvl in range(2):
        lo = kvl * 2 * hd
        q2 = jnp.concatenate([q4[:, lo:lo + hd], q4[:, lo + hd:lo + 2 * hd]], axis=0)
        q2b = q2.astype(BF16)
        kcol = slice(kvl * hd, (kvl + 1) * hd)

        kmean = kmean_ref[:, kcol]
        gate = lax.dot_general(q2, kmean, (((1,), (1,)), ((), ())), preferred_element_type=F32,
                               precision=lax.Precision.HIGHEST)
        col = lax.broadcasted_iota(jnp.int32, gate.shape, 1)
        gate = jnp.where(col < i, gate, NEG_INF)
        sel = jnp.zeros(gate.shape, F32)
        for _ in range(MOBA_TOPK):
            mx = jnp.max(gate, axis=1, keepdims=True)
            idx = jnp.min(jnp.where(gate == mx, col, nblk), axis=1, keepdims=True)
            hit = (col == idx) & (mx > NEG_INF)
            sel = jnp.where(hit, 1.0, sel)
            gate = jnp.where(col == idx, NEG_INF, gate)
        sel_ref[...] = sel

        m_ref[...] = jnp.full(m_ref.shape, NEG_INF, F32)
        l_ref[...] = jnp.zeros(l_ref.shape, F32)
        acc_ref[...] = jnp.zeros(acc_ref.shape, F32)

        def update(s, vb):
            m_old = m_ref[...]
            m_new = jnp.maximum(m_old, jnp.max(s, axis=1, keepdims=True))
            alpha = jnp.exp(m_old - m_new)
            p = jnp.exp(s - m_new)
            l_ref[...] = alpha * l_ref[...] + jnp.sum(p, axis=1, keepdims=True)
            acc_ref[...] = alpha * acc_ref[...] + jnp.dot(p.astype(BF16), vb, preferred_element_type=F32)
            m_ref[...] = m_new

        def tile(j):
            r0 = pl.multiple_of(j * blk, blk)
            kb = k_ref[pl.ds(r0, blk), :][:, kcol].astype(BF16)
            vb = v_ref[pl.ds(r0, blk), :][:, kcol].astype(BF16)
            return _dot_nt(q2b, kb), vb

        s, vb = tile(i)
        update(s + bd_ref[kvl], vb)

        @pl.when(i >= 1)
        def _prev():
            s, vb = tile(i - 1)
            live = jnp.sum(jnp.where(col == i - 1, sel_ref[...], 0.0), axis=1, keepdims=True) > 0.0
            update(jnp.where(live, s + bp_ref[kvl], NEG_INF), vb)

        def far(j, carry):
            s, vb = tile(j)
            live = jnp.sum(jnp.where(col == j, sel_ref[...], 0.0), axis=1, keepdims=True) > 0.0
            rb = jnp.where(live, bf_ref[kvl], NEG_INF)
            update(s + rb, vb)
            return carry

        lax.fori_loop(0, jnp.maximum(i - 1, 0), far, 0)

        o = acc_ref[...] / l_ref[...]
        o_ref[:, lo:lo + hd] = o[:blk]
        o_ref[:, lo + hd:lo + 2 * hd] = o[blk:]


def _old_moba_prompt(proj, bias_rel, n_batch, seq, q_col, k_col, v_col):
    blk = MOBA_BLOCK
    nblk = seq // blk
    iq = np.arange(blk)[:, None]
    ik = np.arange(blk)[None, :]
    idx_diag = _t5_bucket_np(iq - ik)
    idx_prev = _t5_bucket_np(blk + iq - ik)
    bt = bias_rel.T
    diag = jnp.where(jnp.asarray(iq >= ik)[None], bt[:, idx_diag], NEG_INF)
    prev = bt[:, idx_prev]
    far = jnp.broadcast_to(bt[:, T5_BUCKETS - 1][:, None, None], (N_ATT_HEADS, blk, 1))
    diag = diag.reshape(KV_HEADS_A, 2 * blk, blk)
    prev = prev.reshape(KV_HEADS_A, 2 * blk, blk)
    far = far.reshape(KV_HEADS_A, 2 * blk, 1)
    rows_per_b = seq // blk
    return pl.pallas_call(
        _moba_prompt_kernel,
        grid=(n_batch, 2, nblk),
        in_specs=[pl.BlockSpec((blk, 2 * LANES), lambda b, p, i: (b * rows_per_b + i, q_col // 2 + p)),
                  pl.BlockSpec((seq, LANES), lambda b, p, i: (b, k_col + p)),
                  pl.BlockSpec((seq, LANES), lambda b, p, i: (b, v_col + p)),
                  pl.BlockSpec((2, 2 * blk, blk), lambda b, p, i: (p, 0, 0)),
                  pl.BlockSpec((2, 2 * blk, blk), lambda b, p, i: (p, 0, 0)),
                  pl.BlockSpec((2, 2 * blk, 1), lambda b, p, i: (p, 0, 0))],
        out_specs=pl.BlockSpec((blk, 2 * LANES), lambda b, p, i: (b * rows_per_b + i, p)),
        out_shape=jax.ShapeDtypeStruct((n_batch * seq, 4 * LANES), F32),
        scratch_shapes=[pltpu.VMEM((nblk, LANES), F32),
                        pltpu.VMEM((2 * blk, nblk), F32),
                        pltpu.VMEM((2 * blk, 1), F32),
                        pltpu.VMEM((2 * blk, 1), F32),
                        pltpu.VMEM((2 * blk, HEAD_DIM), F32)],
        compiler_params=_cparams(("parallel", "parallel", "arbitrary")),
        name="moba_prompt",
    )(proj, proj, proj, diag, prev, far)


def _t5_thresholds():
    b = _t5_bucket_np(np.arange(4 * T5_MAX_DIST))
    return [int(np.argmax(b >= k)) for k in range(1, T5_BUCKETS)]


def _bias_tile_kernel(rb_ref, o_ref, *, off, sign, lo, hi):
    h = pl.program_id(0)
    row = lax.broadcasted_iota(jnp.int32, o_ref.shape, 0)
    col = lax.broadcasted_iota(jnp.int32, o_ref.shape, 1)
    rel = off + sign * (row - col)
    t = jnp.full(o_ref.shape, rb_ref[0, h], F32)
    for k, thr in enumerate(_t5_thresholds(), start=1):
        t = jnp.where(rel >= thr, rb_ref[k, h], t)
    if lo is not None:
        t = jnp.where((rel >= lo) & (rel <= hi), t, NEG_INF)
    o_ref[...] = t


def _bias_tiles(rel_bias, rows, cols, off, sign=1, lo=None, hi=None):
    return pl.pallas_call(
        functools.partial(_bias_tile_kernel, off=off, sign=sign, lo=lo, hi=hi),
        grid=(N_ATT_HEADS,),
        in_specs=[pl.BlockSpec(memory_space=pltpu.SMEM)],
        out_specs=pl.BlockSpec((None, rows, cols), lambda h: (h, 0, 0)),
        out_shape=jax.ShapeDtypeStruct((N_ATT_HEADS, rows, cols), F32),
        compiler_params=_cparams(("parallel",)),
        name="bias_tiles",
    )(rel_bias)


def _moba_prompt_kernel(q_ref, k_ref, v_ref, bd_ref, bp_ref, bf_ref, o_ref, kmean_ref, vt_ref, sel_ref):
    i = pl.program_id(2)
    blk = MOBA_BLOCK
    nblk = k_ref.shape[0] // blk
    hd = HEAD_DIM

    @pl.when(i == 0)
    def _per_sequence():
        for n in range(nblk):
            kmean_ref[n:n + 1, :] = jnp.mean(k_ref[n * blk:(n + 1) * blk, :], axis=0, keepdims=True)
            vt_ref[n] = v_ref[n * blk:(n + 1) * blk, :].T.astype(BF16)

    q4 = q_ref[...] * (hd ** -0.5)
    q2b = []
    for kvl in range(2):
        lo = kvl * 2 * hd
        q2 = jnp.concatenate([q4[:, lo:lo + hd], q4[:, lo + hd:lo + 2 * hd]], axis=0)
        q2b.append(q2.astype(BF16))
        gate = lax.dot_general(kmean_ref[:, kvl * hd:(kvl + 1) * hd], q2, (((1,), (1,)), ((), ())),
                               preferred_element_type=F32, precision=lax.Precision.HIGHEST)
        row = lax.broadcasted_iota(jnp.int32, gate.shape, 0)
        gate = jnp.where(row < i, gate, NEG_INF)
        sel = jnp.zeros(gate.shape, F32)
        for _ in range(MOBA_TOPK):
            mx = jnp.max(gate, axis=0, keepdims=True)
            idx = jnp.min(jnp.where(gate == mx, row, nblk), axis=0, keepdims=True)
            hit = (row == idx) & (mx > NEG_INF)
            sel = jnp.where(hit, 1.0, sel)
            gate = jnp.where(row == idx, NEG_INF, gate)
        sel_ref[kvl] = sel

    def scores(j, kvl):
        r0 = pl.multiple_of(j * blk, blk)
        kb = k_ref[pl.ds(r0, blk), :][:, kvl * hd:(kvl + 1) * hd].astype(BF16)
        return _dot_nt(kb, q2b[kvl])

    def update(state, s, j, kvl):
        m_old, l_old, acc = state
        m_new = jnp.maximum(m_old, jnp.max(s, axis=0, keepdims=True))
        alpha = jnp.exp(m_old - m_new)
        p = jnp.exp(s - m_new)
        l_new = alpha * l_old + jnp.sum(p, axis=0, keepdims=True)
        vt = vt_ref[j][kvl * hd:(kvl + 1) * hd, :]
        acc = alpha * acc + jnp.dot(vt, p.astype(BF16), preferred_element_type=F32)
        return m_new, l_new, acc

    def live_row(j, kvl):
        return sel_ref[kvl, pl.ds(j, 1), :] > 0.0

    init = (jnp.full((1, 2 * blk), NEG_INF, F32), jnp.zeros((1, 2 * blk), F32), jnp.zeros((hd, 2 * blk), F32))
    jp = jnp.maximum(i - 1, 0)
    states = []
    for kvl in range(2):
        st = update(init, scores(i, kvl) + bd_ref[kvl], i, kvl)
        s = jnp.where(live_row(jp, kvl), scores(jp, kvl) + bp_ref[kvl], NEG_INF)
        states.append(update(st, s, jp, kvl))

    n_pairs = i // 2

    def far_pair(state, pr, kvl):
        j0 = jnp.minimum(2 * pr, nblk - 2)
        r0 = pl.multiple_of(j0 * blk, 2 * blk)
        first_far = pr < n_pairs
        second_far = first_far & (j0 + 1 < i - 1)
        kb = k_ref[pl.ds(r0, 2 * blk), :][:, kvl * hd:(kvl + 1) * hd].astype(BF16)
        s = _dot_nt(kb, q2b[kvl])
        far_bias = bf_ref[kvl]
        s0 = s[:blk] + jnp.where(live_row(j0, kvl) & first_far, far_bias, NEG_INF)
        s1 = s[blk:] + jnp.where(live_row(j0 + 1, kvl) & second_far, far_bias, NEG_INF)
        m_old, l_old, acc = state
        m_new = jnp.maximum(m_old, jnp.maximum(jnp.max(s0, axis=0, keepdims=True),
                                               jnp.max(s1, axis=0, keepdims=True)))
        alpha = jnp.exp(m_old - m_new)
        p = jnp.concatenate([jnp.exp(s0 - m_new), jnp.exp(s1 - m_new)], axis=0)
        l_new = alpha * l_old + jnp.sum(p, axis=0, keepdims=True)
        rows = slice(kvl * hd, (kvl + 1) * hd)
        vt = jnp.concatenate([vt_ref[j0][rows, :], vt_ref[j0 + 1][rows, :]], axis=1)
        acc = alpha * acc + jnp.dot(vt, p.astype(BF16), preferred_element_type=F32)
        return m_new, l_new, acc

    def far(pr, carry):
        return tuple(far_pair(carry[kvl], pr, kvl) for kvl in range(2))

    states = lax.fori_loop(0, n_pairs, far, tuple(states))
    ot = jnp.concatenate([acc / l for (_, l, acc) in states], axis=0)
    o = ot.T
    for kvl in range(2):
        for hh in range(2):
            o_ref[:, (2 * kvl + hh) * hd:(2 * kvl + hh + 1) * hd] = o[hh * blk:(hh + 1) * blk, kvl * hd:(kvl + 1) * hd]


def _moba_prompt(proj, bias_rel, n_batch, seq, q_col, k_col, v_col):
    blk = MOBA_BLOCK
    nblk = seq // blk

    def pair_up(t):
        return jnp.transpose(t.reshape(KV_HEADS_A, 2, blk, blk), (0, 2, 1, 3)).reshape(KV_HEADS_A, blk, 2 * blk)

    diag = pair_up(_bias_tiles(bias_rel, blk, blk, 0, sign=-1, lo=0, hi=2 * blk))
    prev = pair_up(_bias_tiles(bias_rel, blk, blk, blk, sign=-1))
    far = jnp.broadcast_to(bias_rel[T5_BUCKETS - 1].reshape(KV_HEADS_A, 1, 2, 1), (KV_HEADS_A, 1, 2, blk))
    far = far.reshape(KV_HEADS_A, 1, 2 * blk)
    rows_per_b = seq // blk
    return pl.pallas_call(
        _moba_prompt_kernel,
        grid=(n_batch, 2, nblk),
        in_specs=[pl.BlockSpec((blk, 2 * LANES), lambda b, p, i: (b * rows_per_b + i, q_col // 2 + p)),
                  pl.BlockSpec((seq, LANES), lambda b, p, i: (b, k_col + p)),
                  pl.BlockSpec((seq, LANES), lambda b, p, i: (b, v_col + p)),
                  pl.BlockSpec((2, blk, 2 * blk), lambda b, p, i: (p, 0, 0)),
                  pl.BlockSpec((2, blk, 2 * blk), lambda b, p, i: (p, 0, 0)),
                  pl.BlockSpec((2, 1, 2 * blk), lambda b, p, i: (p, 0, 0))],
        out_specs=pl.BlockSpec((blk, 2 * LANES), lambda b, p, i: (b * rows_per_b + i, p)),
        out_shape=jax.ShapeDtypeStruct((n_batch * seq, 4 * LANES), F32),
        scratch_shapes=[pltpu.VMEM((nblk, LANES), F32),
                        pltpu.VMEM((nblk, LANES, blk), BF16),
                        pltpu.VMEM((2, nblk, 2 * blk), F32)],
        compiler_params=_cparams(("parallel", "parallel", "arbitrary")),
        name="moba_prompt",
    )(proj, proj, proj, diag, prev, far)


MOBA_DEC_BLOCKS_PER_STEP = 16
PAGES_PER_BLOCK = MOBA_BLOCK // PAGE_SIZE


MOBA_DEC_SLOTS = 3


def _moba_sample_kernel(pt_ref, q_ref, kn_ref, vn_ref, b31_ref, bfar_ref, bown_ref, ck_hbm, cv_hbm, o_ref,
                        g_ref, m_ref, l_ref, part_ref, kbuf, vbuf, sem, *, n_blocks, nb, layer):
    npg = nb * PAGES_PER_BLOCK
    kv_w = kbuf.shape[2]
    b, s = pl.program_id(0), pl.program_id(1)
    n_steps = pl.num_programs(1)
    total = pl.num_programs(0) * n_steps
    g = b * n_steps + s
    q = q_ref[...]
    qb = q.astype(BF16)
    lane = lax.broadcasted_iota(jnp.int32, g_ref.shape, 1)

    def page_copies(step):
        bb = step // n_steps
        ss = step - bb * n_steps
        slot = lax.rem(step, MOBA_DEC_SLOTS)
        out = []
        for k in range(npg):
            page = pt_ref[bb, ss * npg + k]
            for hbm, buf in ((ck_hbm, kbuf), (cv_hbm, vbuf)):
                out.append(pltpu.make_async_copy(hbm.at[page, pl.ds(layer * kv_w, kv_w), :],
                                                 buf.at[slot, k], sem.at[slot]))
        return out

    @pl.when(g == 0)
    def _prime():
        for c in page_copies(g):
            c.start()

    @pl.when((g == 0) & (total > 1))
    def _prime_next():
        for c in page_copies(g + 1):
            c.start()

    @pl.when(s == 0)
    def _init():
        g_ref[...] = jnp.zeros_like(g_ref)
        m_ref[...] = jnp.zeros_like(m_ref)
        l_ref[...] = jnp.zeros_like(l_ref)

    for c in page_copies(g):
        c.wait()
    slot = lax.rem(g, MOBA_DEC_SLOTS)

    for j in range(nb):
        n = s * nb + j
        kts = [kbuf[slot, j * PAGES_PER_BLOCK + t] for t in range(PAGES_PER_BLOCK)]
        vts = [vbuf[slot, j * PAGES_PER_BLOCK + t] for t in range(PAGES_PER_BLOCK)]
        sc = jnp.concatenate([_dot(qb, kt) for kt in kts], axis=1)
        gate = jnp.sum(sc, axis=1, keepdims=True) * (1.0 / MOBA_BLOCK)
        sc = sc + jnp.where(n == n_blocks - 1, b31_ref[...], bfar_ref[...])
        mx = jnp.max(sc, axis=1, keepdims=True)
        p = jnp.exp(sc - mx)
        hit = lane == n
        g_ref[...] = jnp.where(hit, gate, g_ref[...])
        m_ref[...] = jnp.where(hit, mx, m_ref[...])
        l_ref[...] = jnp.where(hit, jnp.sum(p, axis=1, keepdims=True), l_ref[...])
        o = None
        for t, vt in enumerate(vts):
            part = _dot_nt(p[:, t * PAGE_SIZE:(t + 1) * PAGE_SIZE], vt)
            o = part if o is None else o + part
        part_ref[n] = o

    @pl.when(g + 2 < total)
    def _prefetch():
        for c in page_copies(g + 2):
            c.start()

    @pl.when(s == pl.num_programs(1) - 1)
    def _merge():
        gate = jnp.where(lane < n_blocks, g_ref[...], NEG_INF)
        sel = jnp.zeros(gate.shape, jnp.bool_)
        for _ in range(MOBA_TOPK):
            mxg = jnp.max(gate, axis=1, keepdims=True)
            idx = jnp.min(jnp.where(gate == mxg, lane, LANES), axis=1, keepdims=True)
            hit = (lane == idx) & (mxg > NEG_INF)
            sel = sel | hit
            gate = jnp.where(lane == idx, NEG_INF, gate)
        s_own = _dot_nt(qb, kn_ref[...]) + bown_ref[...]
        m_sel = jnp.where(sel, m_ref[...], NEG_INF)
        m_all = jnp.maximum(jnp.max(m_sel, axis=1, keepdims=True), jnp.max(s_own, axis=1, keepdims=True))
        w = jnp.where(sel, jnp.exp(m_sel - m_all), 0.0)
        p_own = jnp.exp(s_own - m_all)
        den = jnp.sum(w * l_ref[...], axis=1, keepdims=True) + jnp.sum(p_own, axis=1, keepdims=True)
        acc = _dot(p_own, vn_ref[...])
        for n in range(n_blocks):
            wn = jnp.sum(jnp.where(lane == n, w, 0.0), axis=1, keepdims=True)
            acc = acc + wn * part_ref[n]
        o_ref[...] = acc / den


def _moba_sample(q_s, k_new, v_new, cache_k, cache_v, page_table, bias_rel, layer=0):
    n_dec, t_dec = q_s.shape[:2]
    n_pages = page_table.shape[1]
    n_blocks = n_pages // PAGES_PER_BLOCK
    nb = min(MOBA_DEC_BLOCKS_PER_STEP, n_blocks)
    npg = nb * PAGES_PER_BLOCK
    assert n_blocks % nb == 0 and n_blocks <= LANES and t_dec <= 8
    rows = N_ATT_HEADS * t_dec
    kv_w = KV_HEADS_A * HEAD_DIM
    head_kv = np.arange(N_ATT_HEADS) // (N_ATT_HEADS // KV_HEADS_A)
    qh = jnp.transpose(q_s, (0, 2, 1, 3)) * (HEAD_DIM ** -0.5)
    onehot = jnp.asarray(np.eye(KV_HEADS_A, dtype=np.float32)[head_kv])
    qbd = (qh[:, :, :, None, :] * onehot[None, :, None, :, None]).reshape(n_dec, rows, kv_w)
    pad = 8 - t_dec
    kn = jnp.pad(k_new, ((0, 0), (0, pad), (0, 0)))
    vn = jnp.pad(v_new, ((0, 0), (0, pad), (0, 0)))
    bt = bias_rel.T
    past = n_pages * PAGE_SIZE
    tq = np.arange(t_dec)[None, :, None]
    ik = np.arange(MOBA_BLOCK)[None, None, :]
    idx31 = _t5_bucket_np(past + tq - ((n_blocks - 1) * MOBA_BLOCK + ik))[0]
    b31 = bt[:, idx31].reshape(rows, MOBA_BLOCK)
    bfar = jnp.broadcast_to(bt[:, T5_BUCKETS - 1][:, None, None], (N_ATT_HEADS, t_dec, 1)).reshape(rows, 1)
    tk = np.arange(8)[None, :]
    rel_own = np.arange(t_dec)[:, None] - tk
    ok_own = (rel_own >= 0) & (tk < t_dec)
    bown = jnp.where(jnp.asarray(ok_own)[None], bt[:, _t5_bucket_np(rel_own)], NEG_INF).reshape(rows, 8)

    const2 = lambda b, s, pt: (0, 0)
    grid_spec = pltpu.PrefetchScalarGridSpec(
        num_scalar_prefetch=1,
        grid=(n_dec, n_blocks // nb),
        in_specs=[pl.BlockSpec((None, rows, kv_w), lambda b, s, pt: (b, 0, 0)),
                  pl.BlockSpec((None, 8, kv_w), lambda b, s, pt: (b, 0, 0)),
                  pl.BlockSpec((None, 8, kv_w), lambda b, s, pt: (b, 0, 0)),
                  pl.BlockSpec((rows, MOBA_BLOCK), const2),
                  pl.BlockSpec((rows, 1), const2),
                  pl.BlockSpec((rows, 8), const2),
                  pl.BlockSpec(memory_space=pl.ANY),
                  pl.BlockSpec(memory_space=pl.ANY)],
        out_specs=pl.BlockSpec((None, rows, kv_w), lambda b, s, pt: (b, 0, 0)),
        scratch_shapes=[pltpu.VMEM((rows, LANES), F32),
                        pltpu.VMEM((rows, LANES), F32),
                        pltpu.VMEM((rows, LANES), F32),
                        pltpu.VMEM((n_blocks, rows, kv_w), F32),
                        pltpu.VMEM((MOBA_DEC_SLOTS, npg, kv_w, PAGE_SIZE), F32),
                        pltpu.VMEM((MOBA_DEC_SLOTS, npg, kv_w, PAGE_SIZE), F32),
                        pltpu.SemaphoreType.DMA((MOBA_DEC_SLOTS,))],
    )
    out = pl.pallas_call(
        functools.partial(_moba_sample_kernel, n_blocks=n_blocks, nb=nb, layer=layer),
        grid_spec=grid_spec,
        out_shape=jax.ShapeDtypeStruct((n_dec, rows, kv_w), F32),
        compiler_params=_cparams(("arbitrary", "arbitrary")),
        name="moba_sample",
    )(page_table, qbd, kn, vn, b31, bfar, bown, cache_k, cache_v)
    out = out.reshape(n_dec, N_ATT_HEADS, t_dec, KV_HEADS_A, HEAD_DIM)
    out = out[:, np.arange(N_ATT_HEADS), :, head_kv, :]
    return jnp.transpose(out, (1, 2, 0, 3)).reshape(n_dec, t_dec, N_ATT_HEADS * HEAD_DIM)


SWA_GROUP = N_ATT_HEADS // KV_HEADS_C


def _sink_softmax_pv(parts, sink):
    m = sink
    for s, _ in parts:
        m = jnp.maximum(m, jnp.max(s, axis=1, keepdims=True))
    den = jnp.exp(sink - m)
    ps = []
    for s, _ in parts:
        p = jnp.exp(s - m)
        den = den + jnp.sum(p, axis=1, keepdims=True)
        ps.append(p)
    o = None
    for p, (_, v) in zip(ps, parts):
        t = _dot(p / den, v)
        o = t if o is None else o + t
    return o


def _swa_prompt_kernel(q_ref, kp_ref, ko_ref, vp_ref, vo_ref, bias_ref, sink_ref, o_ref):
    i = pl.program_id(1)
    w, hd, g = WINDOW, HEAD_DIM, SWA_GROUP
    q = q_ref[...] * (hd ** -0.5)
    kcat = jnp.concatenate([kp_ref[...], ko_ref[...]], axis=0)
    vt = jnp.concatenate([vp_ref[...], vo_ref[...]], axis=0).T.astype(BF16)
    row = lax.broadcasted_iota(jnp.int32, (2 * w, g * w), 0)
    has_key = (row >= w) | (i > 0)
    ots = []
    for c in range(KV_HEADS_C):
        qs = jnp.concatenate([q[:, (c * g + j) * hd:(c * g + j + 1) * hd] for j in range(g)], axis=0)
        s = _dot_nt(kcat[:, c * hd:(c + 1) * hd], qs) + bias_ref[c]
        s = jnp.where(has_key, s, NEG_INF)
        sink = sink_ref[c]
        m = jnp.maximum(sink, jnp.max(s, axis=0, keepdims=True))
        p = jnp.exp(s - m)
        den = jnp.sum(p, axis=0, keepdims=True) + jnp.exp(sink - m)
        ots.append(jnp.dot(vt[c * hd:(c + 1) * hd, :], (p / den).astype(BF16), preferred_element_type=F32))
    o = jnp.concatenate(ots, axis=0).T
    for c in range(KV_HEADS_C):
        for j in range(g):
            o_ref[:, (c * g + j) * hd:(c * g + j + 1) * hd] = o[j * w:(j + 1) * w, c * hd:(c + 1) * hd]


def _swa_prompt(proj, bias_rel, sinks, n_batch, seq, q_col, k_col, v_col):
    w, g = WINDOW, SWA_GROUP
    nb = seq // w
    band = _bias_tiles(bias_rel, 2 * w, w, w, sign=-1, lo=0, hi=w)
    band = jnp.transpose(band.reshape(KV_HEADS_C, g, 2 * w, w), (0, 2, 1, 3)).reshape(KV_HEADS_C, 2 * w, g * w)
    sink = jnp.broadcast_to(sinks.reshape(KV_HEADS_C, 1, g, 1), (KV_HEADS_C, 1, g, w)).reshape(KV_HEADS_C, 1, g * w)
    own = lambda c: (lambda b, i: (b * nb + i, c))
    prev = lambda c: (lambda b, i: (b * nb + jnp.maximum(i - 1, 0), c))
    return pl.pallas_call(
        _swa_prompt_kernel,
        grid=(n_batch, nb),
        in_specs=[pl.BlockSpec((w, 4 * LANES), lambda b, i: (b * nb + i, q_col // 4)),
                  pl.BlockSpec((w, LANES), prev(k_col)),
                  pl.BlockSpec((w, LANES), own(k_col)),
                  pl.BlockSpec((w, LANES), prev(v_col)),
                  pl.BlockSpec((w, LANES), own(v_col)),
                  pl.BlockSpec((KV_HEADS_C, 2 * w, g * w), lambda b, i: (0, 0, 0)),
                  pl.BlockSpec((KV_HEADS_C, 1, g * w), lambda b, i: (0, 0, 0))],
        out_specs=pl.BlockSpec((w, 4 * LANES), lambda b, i: (b * nb + i, 0)),
        out_shape=jax.ShapeDtypeStruct((n_batch * seq, 4 * LANES), F32),
        compiler_params=_cparams(("parallel", "parallel")),
        name="swa_prompt",
    )(proj, proj, proj, proj, proj, band, sink)


SWA_DEC_SEQS_PER_STEP = 8


def _swa_sample_kernel(q_ref, kb_ref, vb_ref, kn_ref, vn_ref, bbuf_ref, bnew_ref, sink_ref, o_ref):
    for i in range(SWA_DEC_SEQS_PER_STEP):
        q = q_ref[i]
        s_buf = _dot_nt(q, kb_ref[i]) + bbuf_ref[...]
        s_new = _dot_nt(q, kn_ref[i]) + bnew_ref[...]
        o_ref[i] = _sink_softmax_pv([(s_buf, vb_ref[i]), (s_new, vn_ref[i])], sink_ref[...])


def _swa_sample(q_s, k_new, v_new, buf_k, buf_v, bias_rel, sinks):
    n_dec, t_dec = q_s.shape[:2]
    nbuf = buf_k.shape[1]
    rows = N_ATT_HEADS * t_dec
    kv_w = KV_HEADS_C * HEAD_DIM
    sb = SWA_DEC_SEQS_PER_STEP
    assert n_dec % sb == 0 and t_dec <= 8
    head_kv = np.arange(N_ATT_HEADS) // SWA_GROUP
    qh = jnp.transpose(q_s, (0, 2, 1, 3)) * (HEAD_DIM ** -0.5)
    onehot = jnp.asarray(np.eye(KV_HEADS_C, dtype=np.float32)[head_kv])
    qbd = (qh[:, :, :, None, :] * onehot[None, :, None, :, None]).reshape(n_dec, rows, kv_w)
    pad = 8 - t_dec
    kn = jnp.pad(k_new, ((0, 0), (0, pad), (0, 0)))
    vn = jnp.pad(v_new, ((0, 0), (0, pad), (0, 0)))
    bt = bias_rel.T
    t = np.arange(t_dec)[:, None]
    rel_buf = nbuf + t - np.arange(nbuf)[None, :]
    ok_buf = (rel_buf >= 0) & (rel_buf <= WINDOW)
    bbuf = jnp.where(jnp.asarray(ok_buf)[None], bt[:, _t5_bucket_np(rel_buf)], NEG_INF).reshape(rows, nbuf)
    tk = np.arange(8)[None, :]
    rel_new = t - tk
    ok_new = (rel_new >= 0) & (tk < t_dec)
    bnew = jnp.where(jnp.asarray(ok_new)[None], bt[:, _t5_bucket_np(rel_new)], NEG_INF).reshape(rows, 8)
    sink = jnp.broadcast_to(sinks[:, None, None], (N_ATT_HEADS, t_dec, 1)).reshape(rows, 1)
    seq3 = lambda r, c: pl.BlockSpec((sb, r, c), lambda i: (i, 0, 0))
    const2 = lambda r, c: pl.BlockSpec((r, c), lambda i: (0, 0))
    out = pl.pallas_call(
        _swa_sample_kernel,
        grid=(n_dec // sb,),
        in_specs=[seq3(rows, kv_w), seq3(nbuf, kv_w), seq3(nbuf, kv_w), seq3(8, kv_w), seq3(8, kv_w),
                  const2(rows, nbuf), const2(rows, 8), const2(rows, 1)],
        out_specs=seq3(rows, kv_w),
        out_shape=jax.ShapeDtypeStruct((n_dec, rows, kv_w), F32),
        compiler_params=_cparams(("parallel",)),
        name="swa_sample",
    )(qbd, buf_k, buf_v, kn, vn, bbuf, bnew, sink)
    out = out.reshape(n_dec, N_ATT_HEADS, t_dec, KV_HEADS_C, HEAD_DIM)
    out = out[:, np.arange(N_ATT_HEADS), :, head_kv, :]
    return jnp.transpose(out, (1, 2, 0, 3)).reshape(n_dec, t_dec, N_ATT_HEADS * HEAD_DIM)


def _softplus(x):
    return jnp.maximum(x, 0.0) + jnp.log1p(jnp.exp(-jnp.abs(x)))


def _neg_expm1(y):
    return -jnp.tanh(0.5 * y) * (jnp.exp(y) + 1.0)


def _shifted_rows(prev8, cur, k):
    if k == 0:
        return cur
    ext = jnp.concatenate([prev8, cur], axis=0)
    return pltpu.roll(ext, k, axis=0)[8:]


def _causal_conv_rows(prev8, cur, w):
    y = cur * w[CONV_WIDTH - 1:CONV_WIDTH, :]
    for k in range(1, CONV_WIDTH):
        y = y + _shifted_rows(prev8, cur, k) * w[CONV_WIDTH - 1 - k:CONV_WIDTH - k, :]
    return y


def _lru_gates(xc, wa, ba, wx, bx, sp):
    r = jax.nn.sigmoid(_dot(xc, wa) + ba)
    i = jax.nn.sigmoid(_dot(xc, wx) + bx)
    log_a = -LRU_C * r * sp
    a = jnp.exp(log_a)
    b = jnp.sqrt(_neg_expm1(2.0 * log_a)) * (i * xc)
    return a, b


LRU_ROWS = 256


def _lru_prompt_kernel(x_ref, gr_ref, cw_ref, cb_ref, wa_ref, ba_ref, wx_ref, bx_ref, lam_ref,
                       o_ref, hl_ref):
    n = x_ref.shape[0] // LRU_ROWS
    sp = _softplus(-lam_ref[...])
    row = lax.broadcasted_iota(jnp.int32, (LRU_ROWS, LANES), 0)

    def body(c, carry):
        prev8, h = carry
        r0 = pl.multiple_of(c * LRU_ROWS, LRU_ROWS)
        cur = x_ref[pl.ds(r0, LRU_ROWS), :]
        xc = _causal_conv_rows(prev8, cur, cw_ref[...]) + cb_ref[...]
        a, b = _lru_gates(xc, wa_ref[...], ba_ref[...], wx_ref[...], bx_ref[...], sp)
        d = 1
        while d < LRU_ROWS:
            a_s = jnp.where(row >= d, pltpu.roll(a, d, axis=0), 1.0)
            b_s = jnp.where(row >= d, pltpu.roll(b, d, axis=0), 0.0)
            b = a * b_s + b
            a = a * a_s
            d *= 2
        hs = a * h + b
        o_ref[pl.ds(r0, LRU_ROWS), :] = hs * jax.nn.gelu(gr_ref[pl.ds(r0, LRU_ROWS), :])
        return cur[LRU_ROWS - 8:], hs[LRU_ROWS - 1:]

    _, h = lax.fori_loop(0, n, body, (jnp.zeros((8, LANES), F32), jnp.zeros((1, LANES), F32)))
    hl_ref[...] = h


def _lru_prompt(proj, n_batch, seq, x_col, g_col, conv_w, conv_b, wa_bd, b_a, wx_bd, b_x, lam):
    ncb = D_RNN // LANES
    vec = lambda v: v.reshape(1, D_RNN)
    vspec = pl.BlockSpec((1, LANES), lambda b, c: (0, c))
    out, hl = pl.pallas_call(
        _lru_prompt_kernel,
        grid=(n_batch, ncb),
        in_specs=[pl.BlockSpec((seq, LANES), lambda b, c: (b, x_col + c)),
                  pl.BlockSpec((seq, LANES), lambda b, c: (b, g_col + c)),
                  pl.BlockSpec((CONV_WIDTH, LANES), lambda b, c: (0, c)),
                  vspec,
                  pl.BlockSpec((LANES, LANES), lambda b, c: (c, c)), vspec,
                  pl.BlockSpec((LANES, LANES), lambda b, c: (c, c)), vspec,
                  vspec],
        out_specs=[pl.BlockSpec((seq, LANES), lambda b, c: (b, c)),
                   pl.BlockSpec((None, 1, LANES), lambda b, c: (b, 0, c))],
        out_shape=[jax.ShapeDtypeStruct((n_batch * seq, D_RNN), F32),
                   jax.ShapeDtypeStruct((n_batch, 1, D_RNN), F32)],
        compiler_params=_cparams(("parallel", "parallel")),
        name="lru_prompt",
    )(proj, proj, conv_w, vec(conv_b), wa_bd, vec(b_a), wx_bd, vec(b_x), vec(lam))
    return out, hl.reshape(n_batch, D_RNN)


def _lru_sample_kernel(xp_ref, gr_ref, h0_ref, cw_ref, cb_ref, wa_ref, ba_ref, wx_ref, bx_ref, lam_ref,
                       o_ref, hl_ref):
    t_dec = gr_ref.shape[0]
    sp = _softplus(-lam_ref[...])
    h = h0_ref[...]
    for t in range(t_dec):
        xc = cb_ref[...]
        for j in range(CONV_WIDTH):
            xc = xc + xp_ref[t + j] * cw_ref[j:j + 1, :]
        a, b = _lru_gates(xc, wa_ref[...], ba_ref[...], wx_ref[...], bx_ref[...], sp)
        h = a * h + b
        o_ref[t] = h * jax.nn.gelu(gr_ref[t])
    hl_ref[...] = h


def _lru_sample(xr, gr, h0, conv0, conv_w, conv_b, wa_bd, b_a, wx_bd, b_x, lam):
    n_dec, t_dec, _ = xr.shape
    xp = jnp.transpose(jnp.concatenate([conv0, xr], axis=1), (1, 0, 2))
    grt = jnp.transpose(gr, (1, 0, 2))
    vec = lambda v: v.reshape(1, D_RNN)
    full = lambda *s: pl.BlockSpec(s, lambda i: (0,) * len(s))
    out, hl = pl.pallas_call(
        _lru_sample_kernel,
        grid=(1,),
        in_specs=[full(t_dec + CONV_WIDTH - 1, n_dec, D_RNN), full(t_dec, n_dec, D_RNN), full(n_dec, D_RNN),
                  full(CONV_WIDTH, D_RNN), full(1, D_RNN), full(D_RNN, D_RNN), full(1, D_RNN),
                  full(D_RNN, D_RNN), full(1, D_RNN), full(1, D_RNN)],
        out_specs=[full(t_dec, n_dec, D_RNN), full(n_dec, D_RNN)],
        out_shape=[jax.ShapeDtypeStruct((t_dec, n_dec, D_RNN), F32),
                   jax.ShapeDtypeStruct((n_dec, D_RNN), F32)],
        compiler_params=_cparams(("arbitrary",)),
        name="lru_sample",
    )(xp, grt, h0, conv_w, vec(conv_b), wa_bd, vec(b_a), wx_bd, vec(b_x), vec(lam))
    return jnp.transpose(out, (1, 0, 2)), hl


def _block_diag(w):
    n, k, _ = w.shape
    eye = jnp.asarray(np.eye(n, dtype=np.float32))
    return (w[:, :, None, :] * eye[:, None, :, None]).reshape(n * k, n * k)


GDN_ROWS = 256
GDN_INTRA_ROWS = 2048


def _bmm(a, b):
    return jnp.einsum('nij,njk->nik', a.astype(BF16), b.astype(BF16), preferred_element_type=F32)


def _bmm_nt(a, b):
    return jnp.einsum('nid,njd->nij', a.astype(BF16), b.astype(BF16), preferred_element_type=F32)


def _col(x, lane, idx):
    return jnp.sum(jnp.where(lane == idx, x, 0.0), axis=1, keepdims=True)


def _gdn_gates_kernel(x_ref, a_ref, dt_ref, beta_ref, gc_ref, *, chunk, pad_period, pad_rows):
    x = x_ref[...]
    tm = x.shape[0]
    beta = jax.nn.sigmoid(x)
    g = -jnp.exp(a_ref[...]) * _softplus(x + dt_ref[...])
    if pad_period:
        row = lax.broadcasted_iota(jnp.int32, x.shape, 0)
        valid = (row & (pad_period - 1)) >= pad_rows
        beta = jnp.where(valid, beta, 0.0)
        g = jnp.where(valid, g, 0.0)
    r = lax.broadcasted_iota(jnp.int32, (tm, tm), 0)
    c = lax.broadcasted_iota(jnp.int32, (tm, tm), 1)
    sh = chunk.bit_length() - 1
    tri = jnp.where(((r >> sh) == (c >> sh)) & (c <= r), 1.0, 0.0)
    beta_ref[...] = beta
    gc_ref[...] = jnp.dot(tri, g, preferred_element_type=F32, precision=lax.Precision.HIGHEST)


def _gdn_gates(x, col, rows, a_log, dt_bias, chunk, pad_period=0, pad_rows=0):
    tm = GDN_ROWS
    assert rows % tm == 0 and tm % chunk == 0
    a_pad = jnp.pad(a_log.reshape(1, -1), ((0, 0), (GDN_HEADS, LANES - 2 * GDN_HEADS)))
    dt_pad = jnp.pad(dt_bias.reshape(1, -1), ((0, 0), (GDN_HEADS, LANES - 2 * GDN_HEADS)))
    return pl.pallas_call(
        functools.partial(_gdn_gates_kernel, chunk=chunk, pad_period=pad_period, pad_rows=pad_rows),
        grid=(rows // tm,),
        in_specs=[pl.BlockSpec((tm, LANES), lambda i: (i, col)),
                  pl.BlockSpec((1, LANES), lambda i: (0, 0)),
                  pl.BlockSpec((1, LANES), lambda i: (0, 0))],
        out_specs=[pl.BlockSpec((tm, LANES), lambda i: (i, 0))] * 2,
        out_shape=[jax.ShapeDtypeStruct((rows, LANES), F32)] * 2,
        compiler_params=_cparams(("parallel",)),
        name="gdn_gates",
    )(x, a_pad, dt_pad)


def _gdn_prep_kernel(x_ref, cw_ref, beta_ref, gc_ref, *out_refs, mode):
    hd = HEAD_DIM
    n = x_ref.shape[0] // GDN_ROWS
    h0 = 2 * pl.program_id(1)
    lane = lax.broadcasted_iota(jnp.int32, (GDN_ROWS, LANES), 1)
    lo = lane < hd

    def halves(c0, c1):
        return jnp.where(lo, c0, c1)

    def put(ref, r0, val):
        ref[0, pl.ds(r0, GDN_ROWS), :] = val[:, :hd]
        ref[1, pl.ds(r0, GDN_ROWS), :] = val[:, hd:]

    def body(c, prev8):
        r0 = pl.multiple_of(c * GDN_ROWS, GDN_ROWS)
        cur = x_ref[pl.ds(r0, GDN_ROWS), :]
        y = _causal_conv_rows(prev8, cur, cw_ref[...])
        y = y * jax.nn.sigmoid(y)
        if mode in ("q", "k"):
            ss = y * y
            s0 = jnp.sum(jnp.where(lo, ss, 0.0), axis=1, keepdims=True)
            s1 = jnp.sum(jnp.where(lo, 0.0, ss), axis=1, keepdims=True)
            y = y * halves(lax.rsqrt(s0 + NORM_EPS), lax.rsqrt(s1 + NORM_EPS))
        if mode == "q":
            put(out_refs[0], r0, y * (hd ** -0.5))
        else:
            beta = beta_ref[pl.ds(r0, GDN_ROWS), :]
            bb = halves(_col(beta, lane, h0), _col(beta, lane, h0 + 1))
            if mode == "k":
                gc = gc_ref[pl.ds(r0, GDN_ROWS), :]
                put(out_refs[0], r0, y)
                put(out_refs[1], r0, y * bb)
                put(out_refs[2], r0, halves(_col(gc, lane, GDN_HEADS + h0), _col(gc, lane, GDN_HEADS + h0 + 1)))
            else:
                put(out_refs[0], r0, y * bb)
        return cur[GDN_ROWS - 8:]

    lax.fori_loop(0, n, body, jnp.zeros((8, LANES), F32))


def _gdn_prep(x, x_col, conv_w, beta, gc, n_batch, seq, mode):
    sec = {"q": 0, "k": 4, "v": 8}[mode]
    n_out = {"q": 1, "k": 3, "v": 1}[mode]
    hspec = pl.BlockSpec((2, seq, HEAD_DIM), lambda b, c: (b * 4 + c, 0, 0))
    gspec = pl.BlockSpec((seq, LANES), lambda b, c: (b, 0))
    outs = pl.pallas_call(
        functools.partial(_gdn_prep_kernel, mode=mode),
        grid=(n_batch, 4),
        in_specs=[pl.BlockSpec((seq, LANES), lambda b, c: (b, x_col + c)),
                  pl.BlockSpec((CONV_WIDTH, LANES), lambda b, c: (0, sec + c)),
                  gspec, gspec],
        out_specs=[hspec] * n_out,
        out_shape=[jax.ShapeDtypeStruct((n_batch * GDN_HEADS, seq, HEAD_DIM), F32)] * n_out,
        compiler_params=_cparams(("parallel", "parallel")),
        name="gdn_prep_" + mode,
    )(x, conv_w, beta, gc)
    return outs


def _gdn_intra_kernel(q_ref, k_ref, kb_ref, vb_ref, gcb_ref, gcr_ref,
                      u_ref, w_ref, qg_ref, kd_ref, qk_ref, gl_ref, *, chunk):
    hd = HEAD_DIM
    rows = q_ref.shape[0]
    n = rows // chunk
    r3 = lambda ref: ref[...].reshape(n, chunk, hd)
    q, k, kb, vb, gcb = r3(q_ref), r3(k_ref), r3(kb_ref), r3(vb_ref), r3(gcb_ref)
    gcr = gcr_ref[...]
    ci = lax.broadcasted_iota(jnp.int32, (n, chunk, chunk), 1)
    si = lax.broadcasted_iota(jnp.int32, (n, chunk, chunk), 2)
    decay = jnp.exp(jnp.where(ci >= si, gcb[:, :, :chunk] - gcr[:, None, :], NEG_INF))
    a = jnp.where(ci > si, _bmm_nt(kb, k) * decay, 0.0)
    b = -a
    y = b
    p = _bmm(b, b)
    stages = chunk.bit_length() - 2
    for t in range(stages):
        y = y + p + _bmm(y, p)
        if t < stages - 1:
            p = _bmm(p, p)
    eg = jnp.exp(gcb)
    rhs = jnp.concatenate([vb, kb * eg], axis=-1)
    sol = rhs + _bmm(y, rhs)
    gl = gcb[:, chunk - 1:chunk, :]
    u_ref[...] = sol[:, :, :hd].reshape(rows, hd)
    w_ref[...] = sol[:, :, hd:].reshape(rows, hd)
    qg_ref[...] = (q * eg).reshape(rows, hd)
    kd_ref[...] = (k * jnp.exp(gl - gcb)).reshape(rows, hd)
    qk_ref[...] = (_bmm_nt(q, k) * decay).reshape(rows, chunk)
    gl_ref[...] = jnp.exp(gl).reshape(n, hd)


def _gdn_intra(q, k, kb, vb, gcb, gc_row, chunk, rows_per_step):
    bh, seq, hd = q.shape
    tr = rows_per_step
    n = tr // chunk
    assert seq % tr == 0
    hspec = pl.BlockSpec((None, tr, hd), lambda h, i: (h, i, 0))
    shape = jax.ShapeDtypeStruct((bh, seq, hd), F32)
    return pl.pallas_call(
        functools.partial(_gdn_intra_kernel, chunk=chunk),
        grid=(bh, seq // tr),
        in_specs=[hspec] * 5 + [pl.BlockSpec((None, n, chunk), lambda h, i: (h, i, 0))],
        out_specs=[hspec] * 4 + [pl.BlockSpec((None, tr, chunk), lambda h, i: (h, i, 0)),
                                 pl.BlockSpec((None, n, hd), lambda h, i: (h, i, 0))],
        out_shape=[shape] * 4 + [jax.ShapeDtypeStruct((bh, seq, chunk), F32),
                                 jax.ShapeDtypeStruct((bh, seq // chunk, hd), F32)],
        compiler_params=_cparams(("parallel", "parallel")),
        name="gdn_intra",
    )(q, k, kb, vb, gcb, gc_row)


def _gdn_chunk_update(s, u, w, qg, kd, qk, gl):
    v_new = u - _bmm(w, s)
    o = _bmm(qg, s) + _bmm(qk, v_new)
    s = s * gl + jnp.einsum('ncd,nce->nde', kd.astype(BF16), v_new.astype(BF16), preferred_element_type=F32)
    return s, o


def _gdn_scan_kernel(u_ref, w_ref, qg_ref, kd_ref, qk_ref, gl_ref, o_ref, sfin_ref, s_scr, *, chunk):
    j = pl.program_id(1)

    @pl.when(j == 0)
    def _zero():
        s_scr[...] = jnp.zeros_like(s_scr)

    n = u_ref.shape[1] // chunk
    s = s_scr[...]
    for c in range(n):
        sl = slice(c * chunk, (c + 1) * chunk)
        s, o = _gdn_chunk_update(s, u_ref[:, sl, :], w_ref[:, sl, :], qg_ref[:, sl, :], kd_ref[:, sl, :],
                                 qk_ref[:, sl, :], gl_ref[:, c:c + 1, :])
        o_ref[:, sl, :] = o
    s_scr[...] = s

    @pl.when(j == pl.num_programs(1) - 1)
    def _final():
        sfin_ref[...] = s


def _gdn_scan(u, w, qg, kd, qk, gl, chunk, heads_per_step, rows_per_step):
    bh, seq, hd = u.shape
    hb, tr = heads_per_step, rows_per_step
    n = tr // chunk
    hspec = pl.BlockSpec((hb, tr, hd), lambda h, i: (h, i, 0))
    return pl.pallas_call(
        functools.partial(_gdn_scan_kernel, chunk=chunk),
        grid=(bh // hb, seq // tr),
        in_specs=[hspec] * 4 + [pl.BlockSpec((hb, tr, chunk), lambda h, i: (h, i, 0)),
                                pl.BlockSpec((hb, n, hd), lambda h, i: (h, i, 0))],
        out_specs=[hspec, pl.BlockSpec((hb, hd, hd), lambda h, i: (h, 0, 0))],
        out_shape=[jax.ShapeDtypeStruct((bh, seq, hd), F32), jax.ShapeDtypeStruct((bh, hd, hd), F32)],
        scratch_shapes=[pltpu.VMEM((hb, hd, hd), F32)],
        compiler_params=_cparams(("parallel", "arbitrary")),
        name="gdn_scan",
    )(u, w, qg, kd, qk, gl)


GDN_DEC_SEQS_PER_STEP = 16


def _gdn_step_kernel(s0_ref, u_ref, w_ref, qg_ref, kd_ref, qk_ref, gl_ref, o_ref, s_ref, *, chunk):
    hd = HEAD_DIM
    n = s0_ref.shape[0]
    r3 = lambda ref, last: ref[...].reshape(n, chunk, last)
    s, o = _gdn_chunk_update(s0_ref[...], r3(u_ref, hd), r3(w_ref, hd), r3(qg_ref, hd), r3(kd_ref, hd),
                             r3(qk_ref, chunk), gl_ref[...][:, None, :])
    o_ref[...] = o.reshape(n * chunk, hd)
    s_ref[...] = s


def _gdn_step(s0, u, w, qg, kd, qk, gl, chunk):
    n_dec, nh, hd, _ = s0.shape
    sb = GDN_DEC_SEQS_PER_STEP
    assert n_dec % sb == 0
    hspec = pl.BlockSpec((None, sb * chunk, hd), lambda h, i: (h, i, 0))
    sspec = pl.BlockSpec((sb, None, hd, hd), lambda h, i: (i, h, 0, 0))
    return pl.pallas_call(
        functools.partial(_gdn_step_kernel, chunk=chunk),
        grid=(nh, n_dec // sb),
        in_specs=[sspec] + [hspec] * 4 + [pl.BlockSpec((None, sb * chunk, chunk), lambda h, i: (h, i, 0)),
                                         pl.BlockSpec((None, sb, hd), lambda h, i: (h, i, 0))],
        out_specs=[hspec, sspec],
        out_shape=[jax.ShapeDtypeStruct((nh, n_dec * chunk, hd), F32),
                   jax.ShapeDtypeStruct((n_dec, nh, hd, hd), F32)],
        compiler_params=_cparams(("parallel", "parallel")),
        name="gdn_step",
    )(s0, u, w, qg, kd, qk, gl)


def _gdn_post_kernel(o_ref, z_ref, w_ref, out_ref):
    hd = HEAD_DIM
    for h in range(GDN_HEADS):
        o = o_ref[h]
        r = o * lax.rsqrt(jnp.mean(o * o, axis=-1, keepdims=True) + NORM_EPS) * w_ref[...]
        z = z_ref[:, h * hd:(h + 1) * hd]
        out_ref[:, h * hd:(h + 1) * hd] = r * (z * jax.nn.sigmoid(z))


def _gdn_post(o, z, z_col, norm_w, n_batch, seq, tm):
    nt = seq // tm
    return pl.pallas_call(
        _gdn_post_kernel,
        grid=(n_batch, nt),
        in_specs=[pl.BlockSpec((GDN_HEADS, tm, HEAD_DIM), lambda b, i: (b, i, 0)),
                  pl.BlockSpec((tm, 4 * LANES), lambda b, i: (b * nt + i, z_col)),
                  pl.BlockSpec((1, HEAD_DIM), lambda b, i: (0, 0))],
        out_specs=pl.BlockSpec((tm, 4 * LANES), lambda b, i: (b * nt + i, 0)),
        out_shape=jax.ShapeDtypeStruct((n_batch * seq, 4 * LANES), F32),
        compiler_params=_cparams(("parallel", "parallel")),
        name="gdn_post",
    )(o, z, norm_w.reshape(1, HEAD_DIM))


def _gdn_front(x, qkv_col, gate_col, conv_w, a_log, dt_bias, n_batch, seq, chunk, pad_period=0, pad_rows=0):
    rows = n_batch * seq
    beta, gc = _gdn_gates(x, gate_col, rows, a_log, dt_bias, chunk, pad_period, pad_rows)
    (q,) = _gdn_prep(x, qkv_col, conv_w, beta, gc, n_batch, seq, "q")
    k, kb, gcb = _gdn_prep(x, qkv_col + 4, conv_w, beta, gc, n_batch, seq, "k")
    (vb,) = _gdn_prep(x, qkv_col + 8, conv_w, beta, gc, n_batch, seq, "v")
    gc_row = jnp.transpose(gc[:, GDN_HEADS:2 * GDN_HEADS].reshape(n_batch, seq, GDN_HEADS), (0, 2, 1))
    gc_row = gc_row.reshape(n_batch * GDN_HEADS, seq // chunk, chunk)
    return _gdn_intra(q, k, kb, vb, gcb, gc_row, chunk, min(seq, GDN_INTRA_ROWS))


def _x_l2norm(x):
    return x * lax.rsqrt(jnp.sum(x * x, -1, keepdims=True) + NORM_EPS)


def _x_rms(x, w):
    return x * lax.rsqrt(jnp.mean(x * x, -1, keepdims=True) + NORM_EPS) * w


def _x_t5_bucket(rel):
    n = jnp.maximum(rel, 0)
    max_exact = T5_BUCKETS // 2
    nf = jnp.maximum(n, 1).astype(F32)
    large = max_exact + (jnp.log(nf / max_exact) / math.log(T5_MAX_DIST / max_exact)
                         * (T5_BUCKETS - max_exact)).astype(jnp.int32)
    large = jnp.minimum(large, T5_BUCKETS - 1)
    return jnp.where(n < max_exact, n, large)


def _x_causal_conv(x, buf, w, b=None):
    L = x.shape[1]
    xp = jnp.concatenate([buf.astype(x.dtype), x], axis=1)
    y = xp[:, CONV_WIDTH - 1:] * w[CONV_WIDTH - 1]
    for j in range(CONV_WIDTH - 1):
        y = y + xp[:, j:j + L] * w[j]
    if b is not None:
        y = y + b
    return y, xp[:, L:]


def _x_query_chunk(n_q, batch):
    cap = max(1, min(n_q, 128 // batch))
    return max(d for d in range(1, cap + 1) if n_q % d == 0)


def _x_moba_attention(q, k, v, q0, bias_t):
    B, Lq, Hq, hd = q.shape
    Lk, KV = k.shape[1], k.shape[2]
    G = Hq // KV
    n_full = Lk // MOBA_BLOCK
    n_sel = min(MOBA_TOPK, n_full)
    head_kv = jnp.arange(Hq) // G
    kmean = k[:, :n_full * MOBA_BLOCK].reshape(B, n_full, MOBA_BLOCK, KV, hd).mean(2, dtype=F32)[:, :, head_kv]
    qc = _x_query_chunk(Lq, B)
    nq = Lq // qc
    q_chunks = q.reshape(B, nq, qc, Hq, hd).swapaxes(0, 1)
    starts = q0 + qc * jnp.arange(nq, dtype=jnp.int32)
    b_ix = jnp.arange(B)[:, None, None, None, None]
    h_ix = jnp.arange(Hq)[None, :, None, None, None]
    kv_ix = head_kv[None, :, None, None, None]
    offs = jnp.arange(MOBA_BLOCK)
    scale = hd ** -0.5

    def one_chunk(args):
        qb, s0 = args
        q_pos = s0 + jnp.arange(qc)
        own = jnp.broadcast_to((q_pos // MOBA_BLOCK)[None, None, :, None], (B, Hq, qc, 1))
        gate = jnp.einsum('bqhd,bnhd->bhqn', qb, kmean, preferred_element_type=F32)
        gate = jnp.where(jnp.arange(n_full) < own, gate, -jnp.inf)
        _, sel = lax.top_k(gate, n_sel)
        blocks = jnp.concatenate([sel, own], -1)
        live = jnp.concatenate([sel < own, jnp.ones_like(own, dtype=bool)], -1)
        k_pos = blocks[..., None] * MOBA_BLOCK + offs
        rows = jnp.minimum(k_pos, Lk - 1)
        kg = k[b_ix, rows, kv_ix]
        vg = v[b_ix, rows, kv_ix]
        rel = q_pos[None, None, :, None, None] - k_pos
        s = jnp.einsum('bqhd,bhqsrd->bhqsr', qb, kg, preferred_element_type=F32) * scale
        s = s + bias_t[h_ix, _x_t5_bucket(rel)].astype(F32)
        s = jnp.where(live[..., None] & (rel >= 0), s, -jnp.inf)
        n_keys = s.shape[3] * MOBA_BLOCK
        p = jax.nn.softmax(s.reshape(B, Hq, qc, n_keys), axis=-1).reshape(s.shape)
        o = jnp.einsum('bhqsr,bhqsrd->bqhd', p, vg, preferred_element_type=F32)
        return o.astype(q.dtype)

    out = lax.map(one_chunk, (q_chunks, starts))
    return out.swapaxes(0, 1).reshape(B, Lq, Hq, hd)


def _x_gated_delta_rule(q, k, v, g, beta, s0):
    B, L, H, dk = q.shape
    dv = v.shape[-1]
    C = min(GDN_CHUNK, L)
    pad = (-L) % C
    N = (L + pad) // C

    def chunks(t):
        t = jnp.pad(t.astype(F32), [(0, 0), (0, pad)] + [(0, 0)] * (t.ndim - 2))
        t = t.reshape((B, N, C) + t.shape[2:])
        return jnp.swapaxes(jnp.swapaxes(t, 0, 1), 2, 3)

    q = chunks(q) * dk ** -0.5
    k = chunks(k)
    v = chunks(v)
    g = chunks(g)
    beta = chunks(beta)
    gc = jnp.cumsum(g, -1)
    incl = jnp.tril(jnp.ones((C, C), bool))
    strict = jnp.tril(jnp.ones((C, C), bool), -1)
    decay = jnp.exp(jnp.where(incl, gc[..., :, None] - gc[..., None, :], -jnp.inf))
    kb = k * beta[..., None]
    lower = jnp.where(strict, jnp.einsum('nbhcd,nbhsd->nbhcs', kb, k) * decay, 0.0)
    t_mat = lower + jnp.eye(C, dtype=F32)
    rhs = jnp.concatenate([v * beta[..., None], kb * jnp.exp(gc)[..., None]], -1)
    sol = lax.linalg.triangular_solve(t_mat, rhs, left_side=True, lower=True, unit_diagonal=True)
    u, w = sol[..., :dv], sol[..., dv:]
    qk = jnp.einsum('nbhcd,nbhsd->nbhcs', q, k) * decay
    qg = q * jnp.exp(gc)[..., None]
    kd = k * jnp.exp(gc[..., -1:] - gc)[..., None]
    glast = jnp.exp(gc[..., -1])[..., None, None]

    def step(S, xs):
        u_n, w_n, qk_n, qg_n, kd_n, gl_n = xs
        v_new = u_n - jnp.einsum('bhcd,bhde->bhce', w_n, S)
        o_n = jnp.einsum('bhcd,bhde->bhce', qg_n, S) + jnp.einsum('bhcs,bhse->bhce', qk_n, v_new)
        S = S * gl_n + jnp.einsum('bhcd,bhce->bhde', kd_n, v_new)
        return S, o_n

    S, o = lax.scan(step, s0.astype(F32), (u, w, qk, qg, kd, glast))
    o = jnp.swapaxes(jnp.swapaxes(o, 2, 3), 0, 1).reshape(B, N * C, H, dv)[:, :L]
    return o, S


def _x_sink_attention(q, k, v, q_pos, k_pos, sinks, bias_t):
    *lead, Lq, Hq, hd = q.shape
    KV = k.shape[-2]
    G = Hq // KV
    qg = q.reshape(*lead, Lq, KV, G, hd)
    s = jnp.einsum('...qkgd,...skd->...kgqs', qg, k, preferred_element_type=F32) * hd ** -0.5
    rel = q_pos[..., :, None] - k_pos[..., None, :]
    ok = (rel >= 0) & (rel <= WINDOW) & (k_pos[..., None, :] >= 0)
    bias = jnp.moveaxis(bias_t.reshape(KV, G, T5_BUCKETS)[:, :, _x_t5_bucket(rel)], (0, 1), (-4, -3))
    s = jnp.where(ok[..., None, None, :, :], s + bias.astype(F32), -jnp.inf)
    sink = sinks.astype(F32).reshape(KV, G)[:, :, None, None]
    m = jnp.maximum(s.max(-1, keepdims=True), sink)
    p = jnp.exp(s - m)
    den = p.sum(-1, keepdims=True) + jnp.exp(sink - m)
    o = jnp.einsum('...kgqs,...skd->...qkgd', p / den, v, preferred_element_type=F32)
    return o.reshape(*lead, Lq, Hq, hd).astype(q.dtype)


def _x_swa_prompt(q, k, v, sinks, bias_t):
    B, L, Hq, hd = q.shape
    W = WINDOW
    nb = L // W

    def band(t):
        tb = t.reshape(B, nb, W, t.shape[2], hd)
        prev = jnp.concatenate([jnp.zeros_like(tb[:, :1]), tb[:, :-1]], axis=1)
        return jnp.concatenate([prev, tb], axis=2)

    base = W * jnp.arange(nb)[:, None]
    q_pos = base + jnp.arange(W)
    k_pos = base - W + jnp.arange(2 * W)
    qb = q.reshape(B, nb, W, Hq, hd)
    o = _x_sink_attention(qb, band(k), band(v), q_pos, k_pos, sinks, bias_t)
    return o.reshape(B, nb * W, Hq, hd)


def _x_lru_combine(x, y):
    a1, b1 = x
    a2, b2 = y
    return a1 * a2, a2 * b1 + b2


def _x_rg_lru(x, h0, w_a, b_a, w_x, b_x, lam):
    B, L, _ = x.shape
    xf = x.astype(F32)
    xb = xf.reshape(B, L, LRU_BLOCKS, D_RNN // LRU_BLOCKS)
    r = jax.nn.sigmoid(jnp.einsum('blni,nij->blnj', xb, w_a).reshape(B, L, D_RNN) + b_a)
    i = jax.nn.sigmoid(jnp.einsum('blni,nij->blnj', xb, w_x).reshape(B, L, D_RNN) + b_x)
    log_a = -LRU_C * r * jax.nn.softplus(-lam.astype(F32))
    a = jnp.exp(log_a)
    b = jnp.sqrt(-jnp.expm1(2.0 * log_a)) * (i * xf)
    b = b.at[:, 0].add(a[:, 0] * h0.astype(F32))
    _, h = lax.associative_scan(_x_lru_combine, (a, b), axis=1)
    return h


def _x_even_rest(p, b, l, past_k, past_v, s0, conv0, conv_w, a_log, dt_bias, norm_w, bias_t, o_a=None):
    qa, ka, va, qkv, z, beta_in, decay_in = p
    if o_a is None:
        qa4 = qa.reshape(b, l, N_ATT_HEADS, HEAD_DIM)
        ka4 = ka.reshape(b, l, KV_HEADS_A, HEAD_DIM)
        va4 = va.reshape(b, l, KV_HEADS_A, HEAD_DIM)
        past = past_k.shape[1]
        k_all = jnp.concatenate([past_k, ka4], axis=1)
        v_all = jnp.concatenate([past_v, va4], axis=1)
        o_a = _x_moba_attention(qa4, k_all, v_all, past, bias_t).reshape(b, l, -1)
    c, conv_new = _x_causal_conv(qkv, conv0, conv_w)
    c = jax.nn.silu(c)
    qb, kb, vb = jnp.split(c, [512, 1024], axis=-1)
    qb = _x_l2norm(qb.reshape(b, l, GDN_HEADS, 64))
    kb = _x_l2norm(kb.reshape(b, l, GDN_HEADS, 64))
    vb = vb.reshape(b, l, GDN_HEADS, 64)
    beta = jax.nn.sigmoid(beta_in)
    g = -jnp.exp(a_log) * jax.nn.softplus(decay_in + dt_bias)
    o_b, s_new = _x_gated_delta_rule(qb, kb, vb, g, beta, s0)
    gate = jax.nn.silu(z.reshape(b, l, GDN_HEADS, 64))
    o_b = _x_rms(o_b, norm_w) * gate
    mixed = jnp.concatenate([o_a, o_b.reshape(b, l, -1)], axis=-1)
    return mixed, s_new, conv_new


def _x_odd_rest(p, b, l, buf_k, buf_v, h0, conv0, past_len, sinks, conv_w, conv_b,
                w_a, b_a, w_x, b_x, lam, bias_t):
    qc, kc, vc, xr, gr = p
    qc = qc.reshape(b, l, N_ATT_HEADS, HEAD_DIM)
    kc = kc.reshape(b, l, KV_HEADS_C, HEAD_DIM)
    vc = vc.reshape(b, l, KV_HEADS_C, HEAD_DIM)
    if buf_k is None:
        o_c = _x_swa_prompt(qc, kc, vc, sinks, bias_t)
        k_keep, v_keep = kc[:, -WINDOW:], vc[:, -WINDOW:]
    else:
        nbuf = buf_k.shape[1]
        k_all = jnp.concatenate([buf_k, kc], axis=1)
        v_all = jnp.concatenate([buf_v, vc], axis=1)
        q_pos = past_len + jnp.arange(l)
        k_pos = past_len - nbuf + jnp.arange(nbuf + l)
        o_c = _x_sink_attention(qc, k_all, v_all, q_pos, k_pos, sinks, bias_t)
        k_keep, v_keep = k_all[:, -nbuf:], v_all[:, -nbuf:]
    xc, conv_new = _x_causal_conv(xr, conv0, conv_w, conv_b)
    h = _x_rg_lru(xc, h0, w_a, b_a, w_x, b_x, lam)
    o_d = h * jax.nn.gelu(gr)
    mixed = jnp.concatenate([o_c.reshape(b, l, -1), o_d], axis=-1)
    return mixed, k_keep, v_keep, h[:, -1], conv_new


TM = 768
TM_SEG = 512


def _pad_cols(w, n):
    return jnp.pad(w, ((0, 0), (0, n - w.shape[1])))


def _x_kernel_fallback(x_prompt, x_sample, cache_k_moba, cache_v_moba, state_gdn, state_gdn_conv, cache_k_swa, cache_v_swa, state_lru, state_lru_conv, page_table, w_in_even, w_out_even, gdn_conv_w, gdn_a_log, gdn_dt_bias, gdn_norm_w, w_in_odd, w_out_odd, swa_sinks, lru_conv_w, lru_conv_b, lru_w_a, lru_b_a, lru_w_x, lru_b_x, lru_lambda, rel_bias, ln_g, ln_b, router_w, router_b, w_gate, w_up, w_down):
    n_pr, seq, d = x_prompt.shape
    n_dec, dec_seq, _ = x_sample.shape
    tp = n_pr * seq
    ts = n_dec * dec_seq
    past_len = page_table.shape[1] * PAGE_SIZE
    bias_t = rel_bias.T
    h = jnp.concatenate([x_prompt.reshape(tp, d), x_sample.reshape(ts, d)], axis=0)

    rw_pad = _pad_cols(router_w, LANES)
    rb_pad = _pad_cols(router_b.reshape(1, -1), LANES)
    wg_b, wu_b, wd_b = w_gate.astype(BF16), w_up.astype(BF16), w_down.astype(BF16)

    p_even = sum(EVEN_SIZES)
    n_even = 3200
    w0 = _pad_cols(w_in_even[0], n_even).astype(BF16)
    proj = _mm(h, w0, TM, 640)
    splits = np.cumsum(EVEN_SIZES)[:-1].tolist()
    pp = [t.reshape(n_pr, seq, -1) for t in jnp.split(proj[:tp, :p_even], splits, axis=-1)]
    ps = [t.reshape(n_dec, dec_seq, -1) for t in jnp.split(proj[tp:, :p_even], splits, axis=-1)]
    moba_k_prompt = pp[1].reshape(n_pr, seq, 1, KV_HEADS_A, HEAD_DIM)
    moba_v_prompt = pp[2].reshape(n_pr, seq, 1, KV_HEADS_A, HEAD_DIM)
    moba_k_sample = ps[1].reshape(n_dec, dec_seq, 1, KV_HEADS_A, HEAD_DIM)
    moba_v_sample = ps[2].reshape(n_dec, dec_seq, 1, KV_HEADS_A, HEAD_DIM)

    o_a_p = _moba_prompt(proj, rel_bias, n_pr, seq, 0, 4, 6).reshape(n_pr, seq, -1)
    mixed_p, gdn_sp, gdn_cp = _x_even_rest(
        pp, n_pr, seq, None, None, jnp.zeros((n_pr, GDN_HEADS, 64, 64), F32),
        jnp.zeros((n_pr, CONV_WIDTH - 1, GDN_CONV_CH), F32),
        gdn_conv_w[0], gdn_a_log[0], gdn_dt_bias[0], gdn_norm_w[0], bias_t, o_a=o_a_p)
    past_k = cache_k_moba[page_table][:, :, :, 0].reshape(n_dec, past_len, KV_HEADS_A, HEAD_DIM)
    past_v = cache_v_moba[page_table][:, :, :, 0].reshape(n_dec, past_len, KV_HEADS_A, HEAD_DIM)
    mixed_s, gdn_ss, gdn_cs = _x_even_rest(
        ps, n_dec, dec_seq, past_k, past_v, state_gdn[0], state_gdn_conv[0],
        gdn_conv_w[0], gdn_a_log[0], gdn_dt_bias[0], gdn_norm_w[0], bias_t)
    mixed = jnp.concatenate([mixed_p.reshape(tp, -1), mixed_s.reshape(ts, -1)], axis=0)
    h = _mm_ln(mixed, w_out_even[0].astype(BF16), h, ln_g[0, 0], ln_b[0, 0], TM)
    h = _moe_ln(h, rw_pad, rb_pad, wg_b[0], wu_b[0], wd_b[0], ln_g[0, 1], ln_b[0, 1], TM)

    p_odd = sum(ODD_SIZES)
    proj = _mm(h, w_in_odd[0].astype(BF16), TM, 896)
    splits = np.cumsum(ODD_SIZES)[:-1].tolist()
    pp = [t.reshape(n_pr, seq, -1) for t in jnp.split(proj[:tp], splits, axis=-1)]
    ps = [t.reshape(n_dec, dec_seq, -1) for t in jnp.split(proj[tp:], splits, axis=-1)]
    mixed_p, swa_kp, swa_vp, lru_p, lru_cp = _x_odd_rest(
        pp, n_pr, seq, None, None, jnp.zeros((n_pr, D_RNN), F32),
        jnp.zeros((n_pr, CONV_WIDTH - 1, D_RNN), F32), 0,
        swa_sinks[0], lru_conv_w[0], lru_conv_b[0], lru_w_a[0], lru_b_a[0], lru_w_x[0], lru_b_x[0],
        lru_lambda[0], bias_t)
    mixed_s, swa_ks, swa_vs, lru_s, lru_cs = _x_odd_rest(
        ps, n_dec, dec_seq, cache_k_swa[0], cache_v_swa[0], state_lru[0], state_lru_conv[0], past_len,
        swa_sinks[0], lru_conv_w[0], lru_conv_b[0], lru_w_a[0], lru_b_a[0], lru_w_x[0], lru_b_x[0],
        lru_lambda[0], bias_t)
    mixed = jnp.concatenate([mixed_p.reshape(tp, -1), mixed_s.reshape(ts, -1)], axis=0)
    h = _mm_ln(mixed, w_out_odd[0].astype(BF16), h, ln_g[1, 0], ln_b[1, 0], TM)
    h = _moe_ln(h, rw_pad, rb_pad, wg_b[1], wu_b[1], wd_b[1], ln_g[1, 1], ln_b[1, 1], TM)

    y_prompt = h[:tp].reshape(n_pr, seq, d)
    y_sample = h[tp:].reshape(n_dec, dec_seq, d)
    return (y_prompt, y_sample, moba_k_prompt, moba_v_prompt, moba_k_sample, moba_v_sample,
            gdn_sp[None], gdn_ss[None], gdn_cp[None], gdn_cs[None],
            swa_kp[None], swa_vp[None], swa_ks[None], swa_vs[None],
            lru_p[None], lru_s[None], lru_cp[None], lru_cs[None])


def _seq_tails(proj, n_batch, seq, n_rows, col, width):
    return jnp.stack([proj[(b + 1) * seq - n_rows:(b + 1) * seq, col:col + width] for b in range(n_batch)])


GDN_DEC_PERIOD = 8
EVEN_COLS = 3200


def kernel(x_prompt, x_sample, cache_k_moba, cache_v_moba, state_gdn, state_gdn_conv, cache_k_swa, cache_v_swa, state_lru, state_lru_conv, page_table, w_in_even, w_out_even, gdn_conv_w, gdn_a_log, gdn_dt_bias, gdn_norm_w, w_in_odd, w_out_odd, swa_sinks, lru_conv_w, lru_conv_b, lru_w_a, lru_b_a, lru_w_x, lru_b_x, lru_lambda, rel_bias, ln_g, ln_b, router_w, router_b, w_gate, w_up, w_down):
    n_pr, seq, d = x_prompt.shape
    n_dec, dec_seq, _ = x_sample.shape
    tp = n_pr * seq
    ts = n_dec * dec_seq
    n_pool = cache_k_moba.shape[0]
    kva = KV_HEADS_A * HEAD_DIM
    kvc = KV_HEADS_C * HEAD_DIM
    x_segs = [x_prompt.reshape(tp, d), x_sample.reshape(ts, d)]

    rw_pad = _pad_cols(router_w, LANES).T
    rb_pad = router_b.reshape(N_EXPERTS, 1)
    experts = [_group_experts(w_gate[layer], w_up[layer], w_down[layer]) for layer in range(DEPTH)]

    proj = _mm(x_segs, _pad_cols(w_in_even[0], EVEN_COLS).astype(BF16), TM_SEG, 640)
    ps = proj[tp:]
    moba_k_prompt = proj[:tp, 512:512 + kva].reshape(n_pr, seq, 1, KV_HEADS_A, HEAD_DIM)
    moba_v_prompt = proj[:tp, 768:768 + kva].reshape(n_pr, seq, 1, KV_HEADS_A, HEAD_DIM)
    k_new = ps[:, 512:512 + kva].reshape(n_dec, dec_seq, kva)
    v_new = ps[:, 768:768 + kva].reshape(n_dec, dec_seq, kva)
    moba_k_sample = k_new.reshape(n_dec, dec_seq, 1, KV_HEADS_A, HEAD_DIM)
    moba_v_sample = v_new.reshape(n_dec, dec_seq, 1, KV_HEADS_A, HEAD_DIM)

    oa_p = _moba_prompt(proj, rel_bias, n_pr, seq, 0, 4, 6)
    oa_s = _moba_sample(ps[:, :512].reshape(n_dec, dec_seq, N_ATT_HEADS, HEAD_DIM), k_new, v_new,
                        jnp.swapaxes(cache_k_moba.reshape(n_pool, PAGE_SIZE, -1), 1, 2),
                        jnp.swapaxes(cache_v_moba.reshape(n_pool, PAGE_SIZE, -1), 1, 2),
                        page_table, rel_bias, layer=0)

    parts = _gdn_front(proj, 8, 24, gdn_conv_w[0], gdn_a_log[0], gdn_dt_bias[0], n_pr, seq, GDN_CHUNK)
    o_p, s_fin = _gdn_scan(*parts, GDN_CHUNK, GDN_HEADS, 512)
    ob_p = _gdn_post(o_p, proj, 5, gdn_norm_w[0], n_pr, seq, 512)
    gdn_state_prompt = s_fin.reshape(1, n_pr, GDN_HEADS, HEAD_DIM, HEAD_DIM)
    gdn_conv_prompt = _seq_tails(proj, n_pr, seq, 3, 1024, GDN_CONV_CH)[None]

    period = GDN_DEC_PERIOD
    lead = period - dec_seq
    qkv_s = ps[:, 1024:1024 + GDN_CONV_CH].reshape(n_dec, dec_seq, GDN_CONV_CH)
    x8 = jnp.concatenate([jnp.zeros((n_dec, lead - 3, GDN_CONV_CH), F32), state_gdn_conv[0], qkv_s], axis=1)
    tail8 = jnp.pad(ps[:, 2560:EVEN_COLS].reshape(n_dec, dec_seq, EVEN_COLS - 2560), ((0, 0), (lead, 0), (0, 0)))
    xs = jnp.concatenate([x8, tail8], axis=-1).reshape(n_dec * period, EVEN_COLS - 1024)
    parts = _gdn_front(xs, 0, 16, gdn_conv_w[0], gdn_a_log[0], gdn_dt_bias[0], 1, n_dec * period, period,
                       period, lead)
    o_s, gdn_state_sample = _gdn_step(state_gdn[0], *parts, period)
    ob_s = _gdn_post(o_s, xs, 3, gdn_norm_w[0], 1, n_dec * period, 512)
    ob_s = ob_s.reshape(n_dec, period, 512)[:, lead:].reshape(ts, 512)
    gdn_conv_sample = qkv_s[None, :, dec_seq - 3:]

    h = _mm_ln([oa_p, oa_s.reshape(ts, 512)], [ob_p, ob_s], w_out_even[0].astype(BF16), x_segs,
               ln_g[0, 0], ln_b[0, 0], TM_SEG)
    h = _moe_ln(h, rw_pad, rb_pad, *experts[0], ln_g[0, 1], ln_b[0, 1], TM)

    proj = _mm([h], w_in_odd[0].astype(BF16), TM, 896)
    ps = proj[tp:]
    wa_bd, wx_bd = _block_diag(lru_w_a[0]), _block_diag(lru_w_x[0])
    oc_p = _swa_prompt(proj, rel_bias, swa_sinks[0], n_pr, seq, 0, 4, 5)
    od_p, lru_p = _lru_prompt(proj, n_pr, seq, 6, 10, lru_conv_w[0], lru_conv_b[0], wa_bd, lru_b_a[0],
                              wx_bd, lru_b_x[0], lru_lambda[0])
    swa_k_prompt = _seq_tails(proj, n_pr, seq, WINDOW, 512, kvc).reshape(1, n_pr, WINDOW, KV_HEADS_C, HEAD_DIM)
    swa_v_prompt = _seq_tails(proj, n_pr, seq, WINDOW, 640, kvc).reshape(1, n_pr, WINDOW, KV_HEADS_C, HEAD_DIM)
    lru_conv_prompt = _seq_tails(proj, n_pr, seq, 3, 768, D_RNN)[None]

    nbuf = cache_k_swa.shape[2]
    kc_s = ps[:, 512:512 + kvc].reshape(n_dec, dec_seq, kvc)
    vc_s = ps[:, 640:640 + kvc].reshape(n_dec, dec_seq, kvc)
    buf_k = cache_k_swa[0].reshape(n_dec, nbuf, kvc)
    buf_v = cache_v_swa[0].reshape(n_dec, nbuf, kvc)
    oc_s = _swa_sample(ps[:, :512].reshape(n_dec, dec_seq, N_ATT_HEADS, HEAD_DIM), kc_s, vc_s, buf_k, buf_v,
                       rel_bias, swa_sinks[0])
    swa_k_sample = jnp.concatenate([buf_k, kc_s], axis=1)[:, dec_seq:].reshape(1, n_dec, nbuf, KV_HEADS_C, HEAD_DIM)
    swa_v_sample = jnp.concatenate([buf_v, vc_s], axis=1)[:, dec_seq:].reshape(1, n_dec, nbuf, KV_HEADS_C, HEAD_DIM)
    xr_s = ps[:, 768:768 + D_RNN].reshape(n_dec, dec_seq, D_RNN)
    gr_s = ps[:, 1280:1280 + D_RNN].reshape(n_dec, dec_seq, D_RNN)
    od_s, lru_s = _lru_sample(xr_s, gr_s, state_lru[0], state_lru_conv[0], lru_conv_w[0], lru_conv_b[0],
                              wa_bd, lru_b_a[0], wx_bd, lru_b_x[0], lru_lambda[0])
    lru_conv_sample = jnp.concatenate([state_lru_conv[0], xr_s], axis=1)[None, :, dec_seq:]

    h = _mm_ln([oc_p, oc_s.reshape(ts, 512)], [od_p, od_s.reshape(ts, 512)], w_out_odd[0].astype(BF16), [h],
               ln_g[1, 0], ln_b[1, 0], TM_SEG)
    h = _moe_ln(h, rw_pad, rb_pad, *experts[1], ln_g[1, 1], ln_b[1, 1], TM)

    y_prompt = h[:tp].reshape(n_pr, seq, d)
    y_sample = h[tp:].reshape(n_dec, dec_seq, d)
    return (y_prompt, y_sample, moba_k_prompt, moba_v_prompt, moba_k_sample, moba_v_sample,
            gdn_state_prompt, gdn_state_sample[None], gdn_conv_prompt, gdn_conv_sample,
            swa_k_prompt, swa_v_prompt, swa_k_sample, swa_v_sample,
            lru_p[None], lru_s[None], lru_conv_prompt, lru_conv_sample)
```

```python
import functools
import math

import jax
import jax.numpy as jnp
import numpy as np
from jax import lax
from jax.experimental import pallas as pl
from jax.experimental.pallas import tpu as pltpu

F32 = jnp.float32
BF16 = jnp.bfloat16

HEAD_DIM = 64
N_ATT_HEADS = 8
KV_HEADS_A = 4
KV_HEADS_C = 2
MOBA_BLOCK = 256
MOBA_TOPK = 3
PAGE_SIZE = 128
GDN_HEADS = 8
GDN_CHUNK = 64
GDN_CONV_CH = 1536
CONV_WIDTH = 4
WINDOW = 128
D_RNN = 512
LRU_C = 8.0
T5_BUCKETS = 32
T5_MAX_DIST = 128
N_EXPERTS = 16
N_GROUPS = 4
EXPERTS_PER_GROUP = 4
D_EXPERT = 256
DEPTH = 2
DEEPNORM_ALPHA = (2 * DEPTH) ** 0.25
LN_EPS = 1e-5
NORM_EPS = 1e-6

LANES = 128
VMEM_LIMIT = 56 * 1024 * 1024
NEG_INF = float("-inf")


def _cparams(sem):
    return pltpu.CompilerParams(dimension_semantics=sem, vmem_limit_bytes=VMEM_LIMIT)


def _dot(a, b):
    return jnp.dot(a.astype(BF16), b.astype(BF16), preferred_element_type=F32)


def _dot_nt(a, b):
    return lax.dot_general(a.astype(BF16), b.astype(BF16), (((1,), (1,)), ((), ())),
                           preferred_element_type=F32)


def _seg_specs(segs, tm):
    specs, bounds, lo = [], [], 0
    for a in segs:
        assert a.shape[0] % tm == 0
        nt = a.shape[0] // tm
        specs.append(pl.BlockSpec((tm, a.shape[1]), lambda i, lo=lo, nt=nt: (jnp.clip(i - lo, 0, nt - 1), 0)))
        lo += nt
        bounds.append(lo)
    return specs, bounds


def _seg_tile(i, refs, bounds):
    x = refs[-1][...]
    for r, hi in zip(reversed(refs[:-1]), reversed(bounds[:-1])):
        x = jnp.where(i < hi, r[...], x)
    return x


def _mm_kernel(*refs, tn, bounds):
    ns = len(bounds)
    w_ref, o_ref = refs[ns], refs[ns + 1]
    xb = _seg_tile(pl.program_id(0), refs[:ns], bounds).astype(BF16)
    for c in range(w_ref.shape[1] // tn):
        o_ref[:, c * tn:(c + 1) * tn] = jnp.dot(xb, w_ref[:, c * tn:(c + 1) * tn], preferred_element_type=F32)


def _mm(x_segs, w, tm, tn):
    k, n = w.shape
    assert n % tn == 0
    specs, bounds = _seg_specs(x_segs, tm)
    m = bounds[-1] * tm
    return pl.pallas_call(
        functools.partial(_mm_kernel, tn=tn, bounds=tuple(bounds)),
        grid=(bounds[-1],),
        in_specs=specs + [pl.BlockSpec((k, n), lambda i: (0, 0))],
        out_specs=pl.BlockSpec((tm, n), lambda i: (i, 0)),
        out_shape=jax.ShapeDtypeStruct((m, n), F32),
        compiler_params=_cparams(("parallel",)),
        name="in_proj",
    )(*x_segs, w)


def _layer_norm_rows(z, g, b):
    mu = jnp.mean(z, axis=-1, keepdims=True)
    zc = z - mu
    var = jnp.mean(zc * zc, axis=-1, keepdims=True)
    return zc * lax.rsqrt(var + LN_EPS) * g + b


def _mm_ln_kernel(*refs, bounds_a, bounds_b, bounds_h):
    na, nb, nh = len(bounds_a), len(bounds_b), len(bounds_h)
    xa_refs, xb_refs, h_refs = refs[:na], refs[na:na + nb], refs[na + nb:na + nb + nh]
    wa_ref, wb_ref, g_ref, b_ref, o_ref = refs[na + nb + nh:]
    i = pl.program_id(0)
    f = (jnp.dot(_seg_tile(i, xa_refs, bounds_a).astype(BF16), wa_ref[...], preferred_element_type=F32)
         + jnp.dot(_seg_tile(i, xb_refs, bounds_b).astype(BF16), wb_ref[...], preferred_element_type=F32))
    o_ref[...] = _layer_norm_rows(DEEPNORM_ALPHA * _seg_tile(i, h_refs, bounds_h) + f, g_ref[...], b_ref[...])


def _mm_ln(xa_segs, xb_segs, w, h_segs, g, b, tm):
    ka, kb = xa_segs[0].shape[1], xb_segs[0].shape[1]
    n = w.shape[1]
    assert ka == kb
    sa, ba = _seg_specs(xa_segs, tm)
    sb, bb = _seg_specs(xb_segs, tm)
    sh, bh = _seg_specs(h_segs, tm)
    assert ba[-1] == bb[-1] == bh[-1]
    return pl.pallas_call(
        functools.partial(_mm_ln_kernel, bounds_a=tuple(ba), bounds_b=tuple(bb), bounds_h=tuple(bh)),
        grid=(ba[-1],),
        in_specs=sa + sb + sh + [pl.BlockSpec((ka, n), lambda i: (0, 0)),
                                 pl.BlockSpec((kb, n), lambda i: (1, 0)),
                                 pl.BlockSpec((1, n), lambda i: (0, 0)),
                                 pl.BlockSpec((1, n), lambda i: (0, 0))],
        out_specs=pl.BlockSpec((tm, n), lambda i: (i, 0)),
        out_shape=jax.ShapeDtypeStruct((ba[-1] * tm, n), F32),
        compiler_params=_cparams(("parallel",)),
        name="out_proj_ln",
    )(*xa_segs, *xb_segs, *h_segs, w, w, g.reshape(1, n), b.reshape(1, n))


def _moe_ln_kernel(t_ref, rw_ref, rb_ref, wg_ref, wu_ref, wd_ref, g_ref, b_ref, o_ref,
                   gates_ref, tb_ref, hb_ref, acc_ref):
    e = pl.program_id(1)

    @pl.when(e == 0)
    def _route():
        t = t_ref[...]
        logits = _dot_nt(rw_ref[...], t)[:N_EXPERTS]
        s = jax.nn.sigmoid(logits)
        sb = s + rb_ref[...]
        row = lax.broadcasted_iota(jnp.int32, sb.shape, 0)
        grp = row // EXPERTS_PER_GROUP

        def max_first(vals):
            mx = jnp.max(vals, axis=0, keepdims=True)
            return mx, jnp.min(jnp.where(vals == mx, row, N_EXPERTS), axis=0, keepdims=True)

        best = None
        gsel = None
        for gi in range(N_GROUPS):
            vals = jnp.where(grp == gi, sb, NEG_INF)
            m1, i1 = max_first(vals)
            m2 = jnp.max(jnp.where(row == i1, NEG_INF, vals), axis=0, keepdims=True)
            score = m1 + m2
            if gi == 0:
                best, gsel = score, jnp.zeros_like(i1)
            else:
                upd = score > best
                best = jnp.where(upd, score, best)
                gsel = jnp.where(upd, gi, gsel)
        vals = jnp.where(grp == gsel, sb, NEG_INF)
        _, i1 = max_first(vals)
        _, i2 = max_first(jnp.where(row == i1, NEG_INF, vals))
        s1 = jnp.sum(jnp.where(row == i1, s, 0.0), axis=0, keepdims=True)
        s2 = jnp.sum(jnp.where(row == i2, s, 0.0), axis=0, keepdims=True)
        den = s1 + s2
        gates_t = jnp.where(row == i1, s1 / den, 0.0) + jnp.where(row == i2, s2 / den, 0.0)
        gates_t = jnp.concatenate([gates_t, jnp.zeros((LANES - N_EXPERTS, gates_t.shape[1]), F32)], axis=0)
        gates_ref[...] = gates_t.T
        tb_ref[...] = t.astype(BF16)
        acc_ref[...] = jnp.zeros_like(acc_ref)

    tb = tb_ref[...]
    gates = gates_ref[...]
    lane = lax.broadcasted_iota(jnp.int32, gates.shape, 1)
    for j in range(EXPERTS_PER_GROUP):
        sl = slice(j * D_EXPERT, (j + 1) * D_EXPERT)
        ge = jnp.sum(jnp.where(lane == e * EXPERTS_PER_GROUP + j, gates, 0.0), axis=1, keepdims=True)
        g = jnp.dot(tb, wg_ref[j], preferred_element_type=F32)
        u = jnp.dot(tb, wu_ref[j], preferred_element_type=F32)
        hb_ref[:, sl] = ((g * jax.nn.sigmoid(g)) * u * ge).astype(BF16)
    acc_ref[...] += jnp.dot(hb_ref[...], wd_ref[...], preferred_element_type=F32)

    @pl.when(e == pl.num_programs(1) - 1)
    def _finish():
        o_ref[...] = _layer_norm_rows(DEEPNORM_ALPHA * t_ref[...] + acc_ref[...], g_ref[...], b_ref[...])


def _group_experts(w_gate, w_up, w_down):
    ne, d, f = w_gate.shape
    es = EXPERTS_PER_GROUP
    return w_gate.astype(BF16), w_up.astype(BF16), w_down.astype(BF16).reshape(ne // es, es * f, d)


def _moe_ln(t, router_w_pad, router_b_pad, wg, wu, wd, g, b, tm):
    m, d = t.shape
    ng, gf, _ = wd.shape
    es, f = EXPERTS_PER_GROUP, wg.shape[2]
    assert m % tm == 0
    return pl.pallas_call(
        _moe_ln_kernel,
        grid=(m // tm, ng),
        in_specs=[pl.BlockSpec((tm, d), lambda i, e: (i, 0)),
                  pl.BlockSpec((LANES, d), lambda i, e: (0, 0)),
                  pl.BlockSpec((N_EXPERTS, 1), lambda i, e: (0, 0)),
                  pl.BlockSpec((es, d, f), lambda i, e: (e, 0, 0)),
                  pl.BlockSpec((es, d, f), lambda i, e: (e, 0, 0)),
                  pl.BlockSpec((None, gf, d), lambda i, e: (e, 0, 0)),
                  pl.BlockSpec((1, d), lambda i, e: (0, 0)),
                  pl.BlockSpec((1, d), lambda i, e: (0, 0))],
        out_specs=pl.BlockSpec((tm, d), lambda i, e: (i, 0)),
        out_shape=jax.ShapeDtypeStruct((m, d), F32),
        scratch_shapes=[pltpu.VMEM((tm, LANES), F32),
                        pltpu.VMEM((tm, d), BF16),
                        pltpu.VMEM((tm, gf), BF16),
                        pltpu.VMEM((tm, d), F32)],
        compiler_params=_cparams(("parallel", "arbitrary")),
        name="moe_ln",
    )(t, router_w_pad, router_b_pad, wg, wu, wd, g.reshape(1, d), b.reshape(1, d))


def _t5_bucket_np(rel):
    n = np.maximum(rel, 0)
    max_exact = T5_BUCKETS // 2
    nf = np.maximum(n, 1).astype(np.float32)
    large = max_exact + (np.log(nf / np.float32(max_exact)) / np.float32(math.log(T5_MAX_DIST / max_exact))
                         * np.float32(T5_BUCKETS - max_exact)).astype(np.int32)
    large = np.minimum(large, T5_BUCKETS - 1)
    return np.where(n < max_exact, n, large).astype(np.int32)


def _t5_thresholds():
    b = _t5_bucket_np(np.arange(4 * T5_MAX_DIST))
    return [int(np.argmax(b >= k)) for k in range(1, T5_BUCKETS)]


def _bias_tile_kernel(rb_ref, o_ref, *, off, sign, lo, hi):
    h = pl.program_id(0)
    row = lax.broadcasted_iota(jnp.int32, o_ref.shape, 0)
    col = lax.broadcasted_iota(jnp.int32, o_ref.shape, 1)
    rel = off + sign * (row - col)
    t = jnp.full(o_ref.shape, rb_ref[0, h], F32)
    for k, thr in enumerate(_t5_thresholds(), start=1):
        t = jnp.where(rel >= thr, rb_ref[k, h], t)
    if lo is not None:
        t = jnp.where((rel >= lo) & (rel <= hi), t, NEG_INF)
    o_ref[...] = t


def _bias_tiles(rel_bias, rows, cols, off, sign=1, lo=None, hi=None):
    return pl.pallas_call(
        functools.partial(_bias_tile_kernel, off=off, sign=sign, lo=lo, hi=hi),
        grid=(N_ATT_HEADS,),
        in_specs=[pl.BlockSpec(memory_space=pltpu.SMEM)],
        out_specs=pl.BlockSpec((None, rows, cols), lambda h: (h, 0, 0)),
        out_shape=jax.ShapeDtypeStruct((N_ATT_HEADS, rows, cols), F32),
        compiler_params=_cparams(("parallel",)),
        name="bias_tiles",
    )(rel_bias)


def _moba_prompt_kernel(q_ref, k_ref, v_ref, bd_ref, bp_ref, bf_ref, o_ref, kmean_ref, vt_ref, sel_ref):
    i = pl.program_id(2)
    blk = MOBA_BLOCK
    nblk = k_ref.shape[0] // blk
    hd = HEAD_DIM

    @pl.when(i == 0)
    def _per_sequence():
        for n in range(nblk):
            kmean_ref[n:n + 1, :] = jnp.mean(k_ref[n * blk:(n + 1) * blk, :], axis=0, keepdims=True)
            vt_ref[n] = v_ref[n * blk:(n + 1) * blk, :].T.astype(BF16)

    q4 = q_ref[...] * (hd ** -0.5)
    q2b = []
    for kvl in range(2):
        lo = kvl * 2 * hd
        q2 = jnp.concatenate([q4[:, lo:lo + hd], q4[:, lo + hd:lo + 2 * hd]], axis=0)
        q2b.append(q2.astype(BF16))
        gate = _dot_nt(kmean_ref[:, kvl * hd:(kvl + 1) * hd], q2)
        row = lax.broadcasted_iota(jnp.int32, gate.shape, 0)
        gate = jnp.where(row < i, gate, NEG_INF)
        sel = jnp.zeros(gate.shape, F32)
        for _ in range(MOBA_TOPK):
            mx = jnp.max(gate, axis=0, keepdims=True)
            idx = jnp.min(jnp.where(gate == mx, row, nblk), axis=0, keepdims=True)
            hit = (row == idx) & (mx > NEG_INF)
            sel = jnp.where(hit, 1.0, sel)
            gate = jnp.where(row == idx, NEG_INF, gate)
        sel_ref[kvl] = sel

    def scores(j, kvl):
        r0 = pl.multiple_of(j * blk, blk)
        kb = k_ref[pl.ds(r0, blk), :][:, kvl * hd:(kvl + 1) * hd].astype(BF16)
        return _dot_nt(kb, q2b[kvl])

    def update(state, s, j, kvl):
        m_old, l_old, acc = state
        m_new = jnp.maximum(m_old, jnp.max(s, axis=0, keepdims=True))
        alpha = jnp.exp(m_old - m_new)
        p = jnp.exp(s - m_new)
        l_new = alpha * l_old + jnp.sum(p, axis=0, keepdims=True)
        vt = vt_ref[j][kvl * hd:(kvl + 1) * hd, :]
        acc = alpha * acc + jnp.dot(vt, p.astype(BF16), preferred_element_type=F32)
        return m_new, l_new, acc

    def live_row(j, kvl):
        return sel_ref[kvl, pl.ds(j, 1), :] > 0.0

    init = (jnp.full((1, 2 * blk), NEG_INF, F32), jnp.zeros((1, 2 * blk), F32), jnp.zeros((hd, 2 * blk), F32))
    jp = jnp.maximum(i - 1, 0)
    states = []
    for kvl in range(2):
        st = update(init, scores(i, kvl) + bd_ref[kvl], i, kvl)
        s = jnp.where(live_row(jp, kvl), scores(jp, kvl) + bp_ref[kvl], NEG_INF)
        states.append(update(st, s, jp, kvl))

    n_pairs = i // 2

    def far_pair(state, pr, kvl):
        j0 = 2 * pr
        r0 = pl.multiple_of(j0 * blk, 2 * blk)
        second_far = j0 + 1 < i - 1
        kb = k_ref[pl.ds(r0, 2 * blk), :][:, kvl * hd:(kvl + 1) * hd].astype(BF16)
        s = _dot_nt(kb, q2b[kvl])
        far_bias = bf_ref[kvl]
        s0 = s[:blk] + jnp.where(live_row(j0, kvl), far_bias, NEG_INF)
        s1 = s[blk:] + jnp.where(live_row(j0 + 1, kvl) & second_far, far_bias, NEG_INF)
        m_old, l_old, acc = state
        m_new = jnp.maximum(m_old, jnp.maximum(jnp.max(s0, axis=0, keepdims=True),
                                               jnp.max(s1, axis=0, keepdims=True)))
        alpha = jnp.exp(m_old - m_new)
        p = jnp.concatenate([jnp.exp(s0 - m_new), jnp.exp(s1 - m_new)], axis=0)
        l_new = alpha * l_old + jnp.sum(p, axis=0, keepdims=True)
        rows = slice(kvl * hd, (kvl + 1) * hd)
        vt = jnp.concatenate([vt_ref[j0][rows, :], vt_ref[j0 + 1][rows, :]], axis=1)
        acc = alpha * acc + jnp.dot(vt, p.astype(BF16), preferred_element_type=F32)
        return m_new, l_new, acc

    def far(pr, carry):
        return tuple(far_pair(carry[kvl], pr, kvl) for kvl in range(2))

    states = lax.fori_loop(0, n_pairs, far, tuple(states))
    ot = jnp.concatenate([acc / l for (_, l, acc) in states], axis=0)
    o = ot.T
    for kvl in range(2):
        for hh in range(2):
            o_ref[:, (2 * kvl + hh) * hd:(2 * kvl + hh + 1) * hd] = o[hh * blk:(hh + 1) * blk, kvl * hd:(kvl + 1) * hd]


def _moba_prompt(proj, bias_rel, n_batch, seq, q_col, k_col, v_col):
    blk = MOBA_BLOCK
    nblk = seq // blk

    def pair_up(t):
        return jnp.transpose(t.reshape(KV_HEADS_A, 2, blk, blk), (0, 2, 1, 3)).reshape(KV_HEADS_A, blk, 2 * blk)

    diag = pair_up(_bias_tiles(bias_rel, blk, blk, 0, sign=-1, lo=0, hi=2 * blk))
    prev = pair_up(_bias_tiles(bias_rel, blk, blk, blk, sign=-1))
    far = jnp.broadcast_to(bias_rel[T5_BUCKETS - 1].reshape(KV_HEADS_A, 1, 2, 1), (KV_HEADS_A, 1, 2, blk))
    far = far.reshape(KV_HEADS_A, 1, 2 * blk)
    rows_per_b = seq // blk
    return pl.pallas_call(
        _moba_prompt_kernel,
        grid=(n_batch, 2, nblk),
        in_specs=[pl.BlockSpec((blk, 2 * LANES), lambda b, p, i: (b * rows_per_b + i, q_col // 2 + p)),
                  pl.BlockSpec((seq, LANES), lambda b, p, i: (b, k_col + p)),
                  pl.BlockSpec((seq, LANES), lambda b, p, i: (b, v_col + p)),
                  pl.BlockSpec((2, blk, 2 * blk), lambda b, p, i: (p, 0, 0)),
                  pl.BlockSpec((2, blk, 2 * blk), lambda b, p, i: (p, 0, 0)),
                  pl.BlockSpec((2, 1, 2 * blk), lambda b, p, i: (p, 0, 0))],
        out_specs=pl.BlockSpec((blk, 2 * LANES), lambda b, p, i: (b * rows_per_b + i, p)),
        out_shape=jax.ShapeDtypeStruct((n_batch * seq, 4 * LANES), F32),
        scratch_shapes=[pltpu.VMEM((nblk, LANES), F32),
                        pltpu.VMEM((nblk, LANES, blk), BF16),
                        pltpu.VMEM((2, nblk, 2 * blk), F32)],
        compiler_params=_cparams(("parallel", "parallel", "arbitrary")),
        name="moba_prompt",
    )(proj, proj, proj, diag, prev, far)


MOBA_DEC_BLOCKS_PER_STEP = 16
PAGES_PER_BLOCK = MOBA_BLOCK // PAGE_SIZE


MOBA_DEC_SLOTS = 3


def _moba_sample_kernel(pt_ref, q_ref, kn_ref, vn_ref, b31_ref, bfar_ref, bown_ref, ck_hbm, cv_hbm, o_ref,
                        g_ref, m_ref, l_ref, part_ref, kbuf, vbuf, sem, *, n_blocks, nb, layer):
    npg = nb * PAGES_PER_BLOCK
    kv_w = kbuf.shape[2]
    b, s = pl.program_id(0), pl.program_id(1)
    n_steps = pl.num_programs(1)
    total = pl.num_programs(0) * n_steps
    g = b * n_steps + s
    q = q_ref[...]
    qb = q.astype(BF16)
    lane = lax.broadcasted_iota(jnp.int32, g_ref.shape, 1)

    def page_copies(step):
        bb = step // n_steps
        ss = step - bb * n_steps
        slot = lax.rem(step, MOBA_DEC_SLOTS)
        out = []
        for k in range(npg):
            page = pt_ref[bb, ss * npg + k]
            for hbm, buf in ((ck_hbm, kbuf), (cv_hbm, vbuf)):
                out.append(pltpu.make_async_copy(hbm.at[page, pl.ds(layer * kv_w, kv_w), :],
                                                 buf.at[slot, k], sem.at[slot]))
        return out

    @pl.when(g == 0)
    def _prime():
        for c in page_copies(g):
            c.start()

    @pl.when((g == 0) & (total > 1))
    def _prime_next():
        for c in page_copies(g + 1):
            c.start()

    @pl.when(s == 0)
    def _init():
        g_ref[...] = jnp.zeros_like(g_ref)
        m_ref[...] = jnp.zeros_like(m_ref)
        l_ref[...] = jnp.zeros_like(l_ref)

    for c in page_copies(g):
        c.wait()
    slot = lax.rem(g, MOBA_DEC_SLOTS)

    for j in range(nb):
        n = s * nb + j
        kts = [kbuf[slot, j * PAGES_PER_BLOCK + t] for t in range(PAGES_PER_BLOCK)]
        vts = [vbuf[slot, j * PAGES_PER_BLOCK + t] for t in range(PAGES_PER_BLOCK)]
        sc = jnp.concatenate([_dot(qb, kt) for kt in kts], axis=1)
        gate = jnp.sum(sc, axis=1, keepdims=True) * (1.0 / MOBA_BLOCK)
        sc = sc + jnp.where(n == n_blocks - 1, b31_ref[...], bfar_ref[...])
        mx = jnp.max(sc, axis=1, keepdims=True)
        p = jnp.exp(sc - mx)
        hit = lane == n
        g_ref[...] = jnp.where(hit, gate, g_ref[...])
        m_ref[...] = jnp.where(hit, mx, m_ref[...])
        l_ref[...] = jnp.where(hit, jnp.sum(p, axis=1, keepdims=True), l_ref[...])
        o = None
        for t, vt in enumerate(vts):
            part = _dot_nt(p[:, t * PAGE_SIZE:(t + 1) * PAGE_SIZE], vt)
            o = part if o is None else o + part
        part_ref[n] = o

    @pl.when(g + 2 < total)
    def _prefetch():
        for c in page_copies(g + 2):
            c.start()

    @pl.when(s == pl.num_programs(1) - 1)
    def _merge():
        gate = jnp.where(lane < n_blocks, g_ref[...], NEG_INF)
        sel = jnp.zeros(gate.shape, jnp.bool_)
        for _ in range(MOBA_TOPK):
            mxg = jnp.max(gate, axis=1, keepdims=True)
            idx = jnp.min(jnp.where(gate == mxg, lane, LANES), axis=1, keepdims=True)
            hit = (lane == idx) & (mxg > NEG_INF)
            sel = sel | hit
            gate = jnp.where(lane == idx, NEG_INF, gate)
        s_own = _dot_nt(qb, kn_ref[...]) + bown_ref[...]
        m_sel = jnp.where(sel, m_ref[...], NEG_INF)
        m_all = jnp.maximum(jnp.max(m_sel, axis=1, keepdims=True), jnp.max(s_own, axis=1, keepdims=True))
        w = jnp.where(sel, jnp.exp(m_sel - m_all), 0.0)
        p_own = jnp.exp(s_own - m_all)
        den = jnp.sum(w * l_ref[...], axis=1, keepdims=True) + jnp.sum(p_own, axis=1, keepdims=True)
        acc = _dot(p_own, vn_ref[...])
        for n in range(n_blocks):
            acc = acc + w[:, n:n + 1] * part_ref[n]
        o_ref[...] = acc / den


def _moba_sample(q_s, k_new, v_new, cache_k, cache_v, page_table, bias_rel, layer=0):
    n_dec, t_dec = q_s.shape[:2]
    n_pages = page_table.shape[1]
    n_blocks = n_pages // PAGES_PER_BLOCK
    nb = min(MOBA_DEC_BLOCKS_PER_STEP, n_blocks)
    npg = nb * PAGES_PER_BLOCK
    assert n_blocks % nb == 0 and n_blocks <= LANES and t_dec <= 8
    rows = N_ATT_HEADS * t_dec
    kv_w = KV_HEADS_A * HEAD_DIM
    head_kv = np.arange(N_ATT_HEADS) // (N_ATT_HEADS // KV_HEADS_A)
    qh = jnp.transpose(q_s, (0, 2, 1, 3)) * (HEAD_DIM ** -0.5)
    onehot = jnp.asarray(np.eye(KV_HEADS_A, dtype=np.float32)[head_kv])
    qbd = (qh[:, :, :, None, :] * onehot[None, :, None, :, None]).reshape(n_dec, rows, kv_w)
    pad = 8 - t_dec
    kn = jnp.pad(k_new, ((0, 0), (0, pad), (0, 0)))
    vn = jnp.pad(v_new, ((0, 0), (0, pad), (0, 0)))
    bt = bias_rel.T
    past = n_pages * PAGE_SIZE
    tq = np.arange(t_dec)[None, :, None]
    ik = np.arange(MOBA_BLOCK)[None, None, :]
    idx31 = _t5_bucket_np(past + tq - ((n_blocks - 1) * MOBA_BLOCK + ik))[0]
    b31 = bt[:, idx31].reshape(rows, MOBA_BLOCK)
    bfar = jnp.broadcast_to(bt[:, T5_BUCKETS - 1][:, None, None], (N_ATT_HEADS, t_dec, 1)).reshape(rows, 1)
    tk = np.arange(8)[None, :]
    rel_own = np.arange(t_dec)[:, None] - tk
    ok_own = (rel_own >= 0) & (tk < t_dec)
    bown = jnp.where(jnp.asarray(ok_own)[None], bt[:, _t5_bucket_np(rel_own)], NEG_INF).reshape(rows, 8)

    const2 = lambda b, s, pt: (0, 0)
    grid_spec = pltpu.PrefetchScalarGridSpec(
        num_scalar_prefetch=1,
        grid=(n_dec, n_blocks // nb),
        in_specs=[pl.BlockSpec((None, rows, kv_w), lambda b, s, pt: (b, 0, 0)),
                  pl.BlockSpec((None, 8, kv_w), lambda b, s, pt: (b, 0, 0)),
                  pl.BlockSpec((None, 8, kv_w), lambda b, s, pt: (b, 0, 0)),
                  pl.BlockSpec((rows, MOBA_BLOCK), const2),
                  pl.BlockSpec((rows, 1), const2),
                  pl.BlockSpec((rows, 8), const2),
                  pl.BlockSpec(memory_space=pl.ANY),
                  pl.BlockSpec(memory_space=pl.ANY)],
        out_specs=pl.BlockSpec((None, rows, kv_w), lambda b, s, pt: (b, 0, 0)),
        scratch_shapes=[pltpu.VMEM((rows, LANES), F32),
                        pltpu.VMEM((rows, LANES), F32),
                        pltpu.VMEM((rows, LANES), F32),
                        pltpu.VMEM((n_blocks, rows, kv_w), F32),
                        pltpu.VMEM((MOBA_DEC_SLOTS, npg, kv_w, PAGE_SIZE), F32),
                        pltpu.VMEM((MOBA_DEC_SLOTS, npg, kv_w, PAGE_SIZE), F32),
                        pltpu.SemaphoreType.DMA((MOBA_DEC_SLOTS,))],
    )
    out = pl.pallas_call(
        functools.partial(_moba_sample_kernel, n_blocks=n_blocks, nb=nb, layer=layer),
        grid_spec=grid_spec,
        out_shape=jax.ShapeDtypeStruct((n_dec, rows, kv_w), F32),
        compiler_params=_cparams(("arbitrary", "arbitrary")),
        name="moba_sample",
    )(page_table, qbd, kn, vn, b31, bfar, bown, cache_k, cache_v)
    out = out.reshape(n_dec, N_ATT_HEADS, t_dec, KV_HEADS_A, HEAD_DIM)
    out = out[:, np.arange(N_ATT_HEADS), :, head_kv, :]
    return jnp.transpose(out, (1, 2, 0, 3)).reshape(n_dec, t_dec, N_ATT_HEADS * HEAD_DIM)


SWA_GROUP = N_ATT_HEADS // KV_HEADS_C


def _sink_softmax_pv(parts, sink):
    m = sink
    for s, _ in parts:
        m = jnp.maximum(m, jnp.max(s, axis=1, keepdims=True))
    den = jnp.exp(sink - m)
    ps = []
    for s, _ in parts:
        p = jnp.exp(s - m)
        den = den + jnp.sum(p, axis=1, keepdims=True)
        ps.append(p)
    o = None
    for p, (_, v) in zip(ps, parts):
        t = _dot(p / den, v)
        o = t if o is None else o + t
    return o


def _swa_prompt_kernel(q_ref, kp_ref, ko_ref, vp_ref, vo_ref, bias_ref, sink_ref, o_ref):
    i = pl.program_id(1)
    w, hd, g = WINDOW, HEAD_DIM, SWA_GROUP
    q = q_ref[...] * (hd ** -0.5)
    kcat = jnp.concatenate([kp_ref[...], ko_ref[...]], axis=0)
    vt = jnp.concatenate([vp_ref[...], vo_ref[...]], axis=0).T.astype(BF16)
    row = lax.broadcasted_iota(jnp.int32, (2 * w, g * w), 0)
    has_key = (row >= w) | (i > 0)
    ots = []
    for c in range(KV_HEADS_C):
        qs = jnp.concatenate([q[:, (c * g + j) * hd:(c * g + j + 1) * hd] for j in range(g)], axis=0)
        s = _dot_nt(kcat[:, c * hd:(c + 1) * hd], qs) + bias_ref[c]
        s = jnp.where(has_key, s, NEG_INF)
        sink = sink_ref[c]
        m = jnp.maximum(sink, jnp.max(s, axis=0, keepdims=True))
        p = jnp.exp(s - m)
        den = jnp.sum(p, axis=0, keepdims=True) + jnp.exp(sink - m)
        ots.append(jnp.dot(vt[c * hd:(c + 1) * hd, :], (p / den).astype(BF16), preferred_element_type=F32))
    o = jnp.concatenate(ots, axis=0).T
    for c in range(KV_HEADS_C):
        for j in range(g):
            o_ref[:, (c * g + j) * hd:(c * g + j + 1) * hd] = o[j * w:(j + 1) * w, c * hd:(c + 1) * hd]


def _swa_prompt(proj, bias_rel, sinks, n_batch, seq, q_col, k_col, v_col):
    w, g = WINDOW, SWA_GROUP
    nb = seq // w
    band = _bias_tiles(bias_rel, 2 * w, w, w, sign=-1, lo=0, hi=w)
    band = jnp.transpose(band.reshape(KV_HEADS_C, g, 2 * w, w), (0, 2, 1, 3)).reshape(KV_HEADS_C, 2 * w, g * w)
    sink = jnp.broadcast_to(sinks.reshape(KV_HEADS_C, 1, g, 1), (KV_HEADS_C, 1, g, w)).reshape(KV_HEADS_C, 1, g * w)
    own = lambda c: (lambda b, i: (b * nb + i, c))
    prev = lambda c: (lambda b, i: (b * nb + jnp.maximum(i - 1, 0), c))
    return pl.pallas_call(
        _swa_prompt_kernel,
        grid=(n_batch, nb),
        in_specs=[pl.BlockSpec((w, 4 * LANES), lambda b, i: (b * nb + i, q_col // 4)),
                  pl.BlockSpec((w, LANES), prev(k_col)),
                  pl.BlockSpec((w, LANES), own(k_col)),
                  pl.BlockSpec((w, LANES), prev(v_col)),
                  pl.BlockSpec((w, LANES), own(v_col)),
                  pl.BlockSpec((KV_HEADS_C, 2 * w, g * w), lambda b, i: (0, 0, 0)),
                  pl.BlockSpec((KV_HEADS_C, 1, g * w), lambda b, i: (0, 0, 0))],
        out_specs=pl.BlockSpec((w, 4 * LANES), lambda b, i: (b * nb + i, 0)),
        out_shape=jax.ShapeDtypeStruct((n_batch * seq, 4 * LANES), F32),
        compiler_params=_cparams(("parallel", "parallel")),
        name="swa_prompt",
    )(proj, proj, proj, proj, proj, band, sink)


SWA_DEC_SEQS_PER_STEP = 8


def _swa_sample_kernel(q_ref, kb_ref, vb_ref, kn_ref, vn_ref, bbuf_ref, bnew_ref, sink_ref, o_ref):
    for i in range(SWA_DEC_SEQS_PER_STEP):
        q = q_ref[i]
        s_buf = _dot_nt(q, kb_ref[i]) + bbuf_ref[...]
        s_new = _dot_nt(q, kn_ref[i]) + bnew_ref[...]
        o_ref[i] = _sink_softmax_pv([(s_buf, vb_ref[i]), (s_new, vn_ref[i])], sink_ref[...])


def _swa_sample(q_s, k_new, v_new, buf_k, buf_v, bias_rel, sinks):
    n_dec, t_dec = q_s.shape[:2]
    nbuf = buf_k.shape[1]
    rows = N_ATT_HEADS * t_dec
    kv_w = KV_HEADS_C * HEAD_DIM
    sb = SWA_DEC_SEQS_PER_STEP
    assert n_dec % sb == 0 and t_dec <= 8
    head_kv = np.arange(N_ATT_HEADS) // SWA_GROUP
    qh = jnp.transpose(q_s, (0, 2, 1, 3)) * (HEAD_DIM ** -0.5)
    onehot = jnp.asarray(np.eye(KV_HEADS_C, dtype=np.float32)[head_kv])
    qbd = (qh[:, :, :, None, :] * onehot[None, :, None, :, None]).reshape(n_dec, rows, kv_w)
    pad = 8 - t_dec
    kn = jnp.pad(k_new, ((0, 0), (0, pad), (0, 0)))
    vn = jnp.pad(v_new, ((0, 0), (0, pad), (0, 0)))
    bt = bias_rel.T
    t = np.arange(t_dec)[:, None]
    rel_buf = nbuf + t - np.arange(nbuf)[None, :]
    ok_buf = (rel_buf >= 0) & (rel_buf <= WINDOW)
    bbuf = jnp.where(jnp.asarray(ok_buf)[None], bt[:, _t5_bucket_np(rel_buf)], NEG_INF).reshape(rows, nbuf)
    tk = np.arange(8)[None, :]
    rel_new = t - tk
    ok_new = (rel_new >= 0) & (tk < t_dec)
    bnew = jnp.where(jnp.asarray(ok_new)[None], bt[:, _t5_bucket_np(rel_new)], NEG_INF).reshape(rows, 8)
    sink = jnp.broadcast_to(sinks[:, None, None], (N_ATT_HEADS, t_dec, 1)).reshape(rows, 1)
    seq3 = lambda r, c: pl.BlockSpec((sb, r, c), lambda i: (i, 0, 0))
    const2 = lambda r, c: pl.BlockSpec((r, c), lambda i: (0, 0))
    out = pl.pallas_call(
        _swa_sample_kernel,
        grid=(n_dec // sb,),
        in_specs=[seq3(rows, kv_w), seq3(nbuf, kv_w), seq3(nbuf, kv_w), seq3(8, kv_w), seq3(8, kv_w),
                  const2(rows, nbuf), const2(rows, 8), const2(rows, 1)],
        out_specs=seq3(rows, kv_w),
        out_shape=jax.ShapeDtypeStruct((n_dec, rows, kv_w), F32),
        compiler_params=_cparams(("parallel",)),
        name="swa_sample",
    )(qbd, buf_k, buf_v, kn, vn, bbuf, bnew, sink)
    out = out.reshape(n_dec, N_ATT_HEADS, t_dec, KV_HEADS_C, HEAD_DIM)
    out = out[:, np.arange(N_ATT_HEADS), :, head_kv, :]
    return jnp.transpose(out, (1, 2, 0, 3)).reshape(n_dec, t_dec, N_ATT_HEADS * HEAD_DIM)


def _softplus(x):
    return jnp.maximum(x, 0.0) + jnp.log1p(jnp.exp(-jnp.abs(x)))


def _neg_expm1(y):
    return -jnp.tanh(0.5 * y) * (jnp.exp(y) + 1.0)


def _shifted_rows(prev8, cur, k):
    if k == 0:
        return cur
    ext = jnp.concatenate([prev8, cur], axis=0)
    return pltpu.roll(ext, k, axis=0)[8:]


def _causal_conv_rows(prev8, cur, w):
    y = cur * w[CONV_WIDTH - 1:CONV_WIDTH, :]
    for k in range(1, CONV_WIDTH):
        y = y + _shifted_rows(prev8, cur, k) * w[CONV_WIDTH - 1 - k:CONV_WIDTH - k, :]
    return y


def _lru_gates(xc, wa, ba, wx, bx, sp):
    r = jax.nn.sigmoid(_dot(xc, wa) + ba)
    i = jax.nn.sigmoid(_dot(xc, wx) + bx)
    log_a = -LRU_C * r * sp
    a = jnp.exp(log_a)
    b = jnp.sqrt(_neg_expm1(2.0 * log_a)) * (i * xc)
    return a, b


LRU_ROWS = 256


def _lru_prompt_kernel(x_ref, gr_ref, cw_ref, cb_ref, wa_ref, ba_ref, wx_ref, bx_ref, lam_ref,
                       o_ref, hl_ref):
    n = x_ref.shape[0] // LRU_ROWS
    sp = _softplus(-lam_ref[...])
    row = lax.broadcasted_iota(jnp.int32, (LRU_ROWS, LANES), 0)

    def body(c, carry):
        prev8, h = carry
        r0 = pl.multiple_of(c * LRU_ROWS, LRU_ROWS)
        cur = x_ref[pl.ds(r0, LRU_ROWS), :]
        xc = _causal_conv_rows(prev8, cur, cw_ref[...]) + cb_ref[...]
        a, b = _lru_gates(xc, wa_ref[...], ba_ref[...], wx_ref[...], bx_ref[...], sp)
        d = 1
        while d < LRU_ROWS:
            a_s = jnp.where(row >= d, pltpu.roll(a, d, axis=0), 1.0)
            b_s = jnp.where(row >= d, pltpu.roll(b, d, axis=0), 0.0)
            b = a * b_s + b
            a = a * a_s
            d *= 2
        hs = a * h + b
        o_ref[pl.ds(r0, LRU_ROWS), :] = hs * jax.nn.gelu(gr_ref[pl.ds(r0, LRU_ROWS), :])
        return cur[LRU_ROWS - 8:], hs[LRU_ROWS - 1:]

    _, h = lax.fori_loop(0, n, body, (jnp.zeros((8, LANES), F32), jnp.zeros((1, LANES), F32)))
    hl_ref[...] = h


def _lru_prompt(proj, n_batch, seq, x_col, g_col, conv_w, conv_b, wa_bd, b_a, wx_bd, b_x, lam):
    ncb = D_RNN // LANES
    vec = lambda v: v.reshape(1, D_RNN)
    vspec = pl.BlockSpec((1, LANES), lambda b, c: (0, c))
    out, hl = pl.pallas_call(
        _lru_prompt_kernel,
        grid=(n_batch, ncb),
        in_specs=[pl.BlockSpec((seq, LANES), lambda b, c: (b, x_col + c)),
                  pl.BlockSpec((seq, LANES), lambda b, c: (b, g_col + c)),
                  pl.BlockSpec((CONV_WIDTH, LANES), lambda b, c: (0, c)),
                  vspec,
                  pl.BlockSpec((LANES, LANES), lambda b, c: (c, c)), vspec,
                  pl.BlockSpec((LANES, LANES), lambda b, c: (c, c)), vspec,
                  vspec],
        out_specs=[pl.BlockSpec((seq, LANES), lambda b, c: (b, c)),
                   pl.BlockSpec((None, 1, LANES), lambda b, c: (b, 0, c))],
        out_shape=[jax.ShapeDtypeStruct((n_batch * seq, D_RNN), F32),
                   jax.ShapeDtypeStruct((n_batch, 1, D_RNN), F32)],
        compiler_params=_cparams(("parallel", "parallel")),
        name="lru_prompt",
    )(proj, proj, conv_w, vec(conv_b), wa_bd, vec(b_a), wx_bd, vec(b_x), vec(lam))
    return out, hl.reshape(n_batch, D_RNN)


def _lru_sample_kernel(xp_ref, gr_ref, h0_ref, cw_ref, cb_ref, wa_ref, ba_ref, wx_ref, bx_ref, lam_ref,
                       o_ref, hl_ref):
    t_dec = gr_ref.shape[0]
    sp = _softplus(-lam_ref[...])
    h = h0_ref[...]
    for t in range(t_dec):
        xc = cb_ref[...]
        for j in range(CONV_WIDTH):
            xc = xc + xp_ref[t + j] * cw_ref[j:j + 1, :]
        a, b = _lru_gates(xc, wa_ref[...], ba_ref[...], wx_ref[...], bx_ref[...], sp)
        h = a * h + b
        o_ref[t] = h * jax.nn.gelu(gr_ref[t])
    hl_ref[...] = h


def _lru_sample(xr, gr, h0, conv0, conv_w, conv_b, wa_bd, b_a, wx_bd, b_x, lam):
    n_dec, t_dec, _ = xr.shape
    xp = jnp.transpose(jnp.concatenate([conv0, xr], axis=1), (1, 0, 2))
    grt = jnp.transpose(gr, (1, 0, 2))
    vec = lambda v: v.reshape(1, D_RNN)
    full = lambda *s: pl.BlockSpec(s, lambda i: (0,) * len(s))
    out, hl = pl.pallas_call(
        _lru_sample_kernel,
        grid=(1,),
        in_specs=[full(t_dec + CONV_WIDTH - 1, n_dec, D_RNN), full(t_dec, n_dec, D_RNN), full(n_dec, D_RNN),
                  full(CONV_WIDTH, D_RNN), full(1, D_RNN), full(D_RNN, D_RNN), full(1, D_RNN),
                  full(D_RNN, D_RNN), full(1, D_RNN), full(1, D_RNN)],
        out_specs=[full(t_dec, n_dec, D_RNN), full(n_dec, D_RNN)],
        out_shape=[jax.ShapeDtypeStruct((t_dec, n_dec, D_RNN), F32),
                   jax.ShapeDtypeStruct((n_dec, D_RNN), F32)],
        compiler_params=_cparams(("arbitrary",)),
        name="lru_sample",
    )(xp, grt, h0, conv_w, vec(conv_b), wa_bd, vec(b_a), wx_bd, vec(b_x), vec(lam))
    return jnp.transpose(out, (1, 0, 2)), hl


def _block_diag(w):
    n, k, _ = w.shape
    eye = jnp.asarray(np.eye(n, dtype=np.float32))
    return (w[:, :, None, :] * eye[:, None, :, None]).reshape(n * k, n * k)


GDN_ROWS = 256
GDN_INTRA_ROWS = 2048


def _bmm(a, b):
    return jnp.einsum('nij,njk->nik', a.astype(BF16), b.astype(BF16), preferred_element_type=F32)


def _bmm_nt(a, b):
    return jnp.einsum('nid,njd->nij', a.astype(BF16), b.astype(BF16), preferred_element_type=F32)


def _col(x, lane, idx):
    return jnp.sum(jnp.where(lane == idx, x, 0.0), axis=1, keepdims=True)


def _gdn_gates_kernel(x_ref, a_ref, dt_ref, beta_ref, gc_ref, *, chunk, pad_period, pad_rows):
    x = x_ref[...]
    tm = x.shape[0]
    beta = jax.nn.sigmoid(x)
    g = -jnp.exp(a_ref[...]) * _softplus(x + dt_ref[...])
    if pad_period:
        row = lax.broadcasted_iota(jnp.int32, x.shape, 0)
        valid = (row & (pad_period - 1)) >= pad_rows
        beta = jnp.where(valid, beta, 0.0)
        g = jnp.where(valid, g, 0.0)
    r = lax.broadcasted_iota(jnp.int32, (tm, tm), 0)
    c = lax.broadcasted_iota(jnp.int32, (tm, tm), 1)
    sh = chunk.bit_length() - 1
    tri = jnp.where(((r >> sh) == (c >> sh)) & (c <= r), 1.0, 0.0)
    beta_ref[...] = beta
    gc_ref[...] = jnp.dot(tri, g, preferred_element_type=F32, precision=lax.Precision.HIGHEST)


def _gdn_gates(x, col, rows, a_log, dt_bias, chunk, pad_period=0, pad_rows=0):
    tm = GDN_ROWS
    assert rows % tm == 0 and tm % chunk == 0
    a_pad = jnp.pad(a_log.reshape(1, -1), ((0, 0), (GDN_HEADS, LANES - 2 * GDN_HEADS)))
    dt_pad = jnp.pad(dt_bias.reshape(1, -1), ((0, 0), (GDN_HEADS, LANES - 2 * GDN_HEADS)))
    return pl.pallas_call(
        functools.partial(_gdn_gates_kernel, chunk=chunk, pad_period=pad_period, pad_rows=pad_rows),
        grid=(rows // tm,),
        in_specs=[pl.BlockSpec((tm, LANES), lambda i: (i, col)),
                  pl.BlockSpec((1, LANES), lambda i: (0, 0)),
                  pl.BlockSpec((1, LANES), lambda i: (0, 0))],
        out_specs=[pl.BlockSpec((tm, LANES), lambda i: (i, 0))] * 2,
        out_shape=[jax.ShapeDtypeStruct((rows, LANES), F32)] * 2,
        compiler_params=_cparams(("parallel",)),
        name="gdn_gates",
    )(x, a_pad, dt_pad)


def _gdn_prep_kernel(x_ref, cw_ref, beta_ref, gc_ref, *out_refs, mode):
    hd = HEAD_DIM
    n = x_ref.shape[0] // GDN_ROWS
    h0 = 2 * pl.program_id(1)
    lane = lax.broadcasted_iota(jnp.int32, (GDN_ROWS, LANES), 1)
    lo = lane < hd

    def halves(c0, c1):
        return jnp.where(lo, c0, c1)

    def put(ref, r0, val):
        ref[0, pl.ds(r0, GDN_ROWS), :] = val[:, :hd]
        ref[1, pl.ds(r0, GDN_ROWS), :] = val[:, hd:]

    def body(c, prev8):
        r0 = pl.multiple_of(c * GDN_ROWS, GDN_ROWS)
        cur = x_ref[pl.ds(r0, GDN_ROWS), :]
        y = _causal_conv_rows(prev8, cur, cw_ref[...])
        y = y * jax.nn.sigmoid(y)
        if mode in ("q", "k"):
            ss = y * y
            s0 = jnp.sum(jnp.where(lo, ss, 0.0), axis=1, keepdims=True)
            s1 = jnp.sum(jnp.where(lo, 0.0, ss), axis=1, keepdims=True)
            y = y * halves(lax.rsqrt(s0 + NORM_EPS), lax.rsqrt(s1 + NORM_EPS))
        if mode == "q":
            put(out_refs[0], r0, y * (hd ** -0.5))
        else:
            beta = beta_ref[pl.ds(r0, GDN_ROWS), :]
            bb = halves(_col(beta, lane, h0), _col(beta, lane, h0 + 1))
            if mode == "k":
                gc = gc_ref[pl.ds(r0, GDN_ROWS), :]
                put(out_refs[0], r0, y)
                put(out_refs[1], r0, y * bb)
                put(out_refs[2], r0, halves(_col(gc, lane, GDN_HEADS + h0), _col(gc, lane, GDN_HEADS + h0 + 1)))
            else:
                put(out_refs[0], r0, y * bb)
        return cur[GDN_ROWS - 8:]

    lax.fori_loop(0, n, body, jnp.zeros((8, LANES), F32))


def _gdn_prep(x, x_col, conv_w, beta, gc, n_batch, seq, mode):
    sec = {"q": 0, "k": 4, "v": 8}[mode]
    n_out = {"q": 1, "k": 3, "v": 1}[mode]
    hspec = pl.BlockSpec((2, seq, HEAD_DIM), lambda b, c: (b * 4 + c, 0, 0))
    gspec = pl.BlockSpec((seq, LANES), lambda b, c: (b, 0))
    outs = pl.pallas_call(
        functools.partial(_gdn_prep_kernel, mode=mode),
        grid=(n_batch, 4),
        in_specs=[pl.BlockSpec((seq, LANES), lambda b, c: (b, x_col + c)),
                  pl.BlockSpec((CONV_WIDTH, LANES), lambda b, c: (0, sec + c)),
                  gspec, gspec],
        out_specs=[hspec] * n_out,
        out_shape=[jax.ShapeDtypeStruct((n_batch * GDN_HEADS, seq, HEAD_DIM), F32)] * n_out,
        compiler_params=_cparams(("parallel", "parallel")),
        name="gdn_prep_" + mode,
    )(x, conv_w, beta, gc)
    return outs


def _gdn_intra_kernel(q_ref, k_ref, kb_ref, vb_ref, gcb_ref, gcr_ref,
                      u_ref, w_ref, qg_ref, kd_ref, qk_ref, gl_ref, *, chunk):
    hd = HEAD_DIM
    rows = q_ref.shape[0]
    n = rows // chunk
    r3 = lambda ref: ref[...].reshape(n, chunk, hd)
    q, k, kb, vb, gcb = r3(q_ref), r3(k_ref), r3(kb_ref), r3(vb_ref), r3(gcb_ref)
    gcr = gcr_ref[...]
    ci = lax.broadcasted_iota(jnp.int32, (n, chunk, chunk), 1)
    si = lax.broadcasted_iota(jnp.int32, (n, chunk, chunk), 2)
    decay = jnp.exp(jnp.where(ci >= si, gcb[:, :, :chunk] - gcr[:, None, :], NEG_INF))
    a = jnp.where(ci > si, _bmm_nt(kb, k) * decay, 0.0)
    b = -a
    y = b
    p = _bmm(b, b)
    stages = chunk.bit_length() - 2
    for t in range(stages):
        y = y + p + _bmm(y, p)
        if t < stages - 1:
            p = _bmm(p, p)
    eg = jnp.exp(gcb)
    rhs = jnp.concatenate([vb, kb * eg], axis=-1)
    sol = rhs + _bmm(y, rhs)
    gl = gcb[:, chunk - 1:chunk, :]
    u_ref[...] = sol[:, :, :hd].reshape(rows, hd)
    w_ref[...] = sol[:, :, hd:].reshape(rows, hd)
    qg_ref[...] = (q * eg).reshape(rows, hd)
    kd_ref[...] = (k * jnp.exp(gl - gcb)).reshape(rows, hd)
    qk_ref[...] = (_bmm_nt(q, k) * decay).reshape(rows, chunk)
    gl_ref[...] = jnp.exp(gl).reshape(n, hd)


def _gdn_intra(q, k, kb, vb, gcb, gc_row, chunk, rows_per_step):
    bh, seq, hd = q.shape
    tr = rows_per_step
    n = tr // chunk
    assert seq % tr == 0
    hspec = pl.BlockSpec((None, tr, hd), lambda h, i: (h, i, 0))
    shape = jax.ShapeDtypeStruct((bh, seq, hd), F32)
    return pl.pallas_call(
        functools.partial(_gdn_intra_kernel, chunk=chunk),
        grid=(bh, seq // tr),
        in_specs=[hspec] * 5 + [pl.BlockSpec((None, n, chunk), lambda h, i: (h, i, 0))],
        out_specs=[hspec] * 4 + [pl.BlockSpec((None, tr, chunk), lambda h, i: (h, i, 0)),
                                 pl.BlockSpec((None, n, hd), lambda h, i: (h, i, 0))],
        out_shape=[shape] * 4 + [jax.ShapeDtypeStruct((bh, seq, chunk), F32),
                                 jax.ShapeDtypeStruct((bh, seq // chunk, hd), F32)],
        compiler_params=_cparams(("parallel", "parallel")),
        name="gdn_intra",
    )(q, k, kb, vb, gcb, gc_row)


def _gdn_chunk_update(s, u, w, qg, kd, qk, gl):
    v_new = u - _bmm(w, s)
    o = _bmm(qg, s) + _bmm(qk, v_new)
    s = s * gl + jnp.einsum('ncd,nce->nde', kd.astype(BF16), v_new.astype(BF16), preferred_element_type=F32)
    return s, o


def _gdn_scan_kernel(u_ref, w_ref, qg_ref, kd_ref, qk_ref, gl_ref, o_ref, sfin_ref, s_scr, *, chunk):
    j = pl.program_id(1)

    @pl.when(j == 0)
    def _zero():
        s_scr[...] = jnp.zeros_like(s_scr)

    n = u_ref.shape[1] // chunk
    s = s_scr[...]
    for c in range(n):
        sl = slice(c * chunk, (c + 1) * chunk)
        s, o = _gdn_chunk_update(s, u_ref[:, sl, :], w_ref[:, sl, :], qg_ref[:, sl, :], kd_ref[:, sl, :],
                                 qk_ref[:, sl, :], gl_ref[:, c:c + 1, :])
        o_ref[:, sl, :] = o
    s_scr[...] = s

    @pl.when(j == pl.num_programs(1) - 1)
    def _final():
        sfin_ref[...] = s


def _gdn_scan(u, w, qg, kd, qk, gl, chunk, heads_per_step, rows_per_step):
    bh, seq, hd = u.shape
    hb, tr = heads_per_step, rows_per_step
    n = tr // chunk
    hspec = pl.BlockSpec((hb, tr, hd), lambda h, i: (h, i, 0))
    return pl.pallas_call(
        functools.partial(_gdn_scan_kernel, chunk=chunk),
        grid=(bh // hb, seq // tr),
        in_specs=[hspec] * 4 + [pl.BlockSpec((hb, tr, chunk), lambda h, i: (h, i, 0)),
                                pl.BlockSpec((hb, n, hd), lambda h, i: (h, i, 0))],
        out_specs=[hspec, pl.BlockSpec((hb, hd, hd), lambda h, i: (h, 0, 0))],
        out_shape=[jax.ShapeDtypeStruct((bh, seq, hd), F32), jax.ShapeDtypeStruct((bh, hd, hd), F32)],
        scratch_shapes=[pltpu.VMEM((hb, hd, hd), F32)],
        compiler_params=_cparams(("parallel", "arbitrary")),
        name="gdn_scan",
    )(u, w, qg, kd, qk, gl)


GDN_DEC_SEQS_PER_STEP = 16


def _gdn_step_kernel(s0_ref, u_ref, w_ref, qg_ref, kd_ref, qk_ref, gl_ref, o_ref, s_ref, *, chunk):
    hd = HEAD_DIM
    n = s0_ref.shape[0]
    r3 = lambda ref, last: ref[...].reshape(n, chunk, last)
    s, o = _gdn_chunk_update(s0_ref[...], r3(u_ref, hd), r3(w_ref, hd), r3(qg_ref, hd), r3(kd_ref, hd),
                             r3(qk_ref, chunk), gl_ref[...][:, None, :])
    o_ref[...] = o.reshape(n * chunk, hd)
    s_ref[...] = s


def _gdn_step(s0, u, w, qg, kd, qk, gl, chunk):
    n_dec, nh, hd, _ = s0.shape
    sb = GDN_DEC_SEQS_PER_STEP
    assert n_dec % sb == 0
    hspec = pl.BlockSpec((None, sb * chunk, hd), lambda h, i: (h, i, 0))
    sspec = pl.BlockSpec((sb, None, hd, hd), lambda h, i: (i, h, 0, 0))
    return pl.pallas_call(
        functools.partial(_gdn_step_kernel, chunk=chunk),
        grid=(nh, n_dec // sb),
        in_specs=[sspec] + [hspec] * 4 + [pl.BlockSpec((None, sb * chunk, chunk), lambda h, i: (h, i, 0)),
                                         pl.BlockSpec((None, sb, hd), lambda h, i: (h, i, 0))],
        out_specs=[hspec, sspec],
        out_shape=[jax.ShapeDtypeStruct((nh, n_dec * chunk, hd), F32),
                   jax.ShapeDtypeStruct((n_dec, nh, hd, hd), F32)],
        compiler_params=_cparams(("parallel", "parallel")),
        name="gdn_step",
    )(s0, u, w, qg, kd, qk, gl)


def _gdn_post_kernel(o_ref, z_ref, w_ref, out_ref):
    hd = HEAD_DIM
    for h in range(GDN_HEADS):
        o = o_ref[h]
        r = o * lax.rsqrt(jnp.mean(o * o, axis=-1, keepdims=True) + NORM_EPS) * w_ref[...]
        z = z_ref[:, h * hd:(h + 1) * hd]
        out_ref[:, h * hd:(h + 1) * hd] = r * (z * jax.nn.sigmoid(z))


def _gdn_post(o, z, z_col, norm_w, n_batch, seq, tm):
    nt = seq // tm
    return pl.pallas_call(
        _gdn_post_kernel,
        grid=(n_batch, nt),
        in_specs=[pl.BlockSpec((GDN_HEADS, tm, HEAD_DIM), lambda b, i: (b, i, 0)),
                  pl.BlockSpec((tm, 4 * LANES), lambda b, i: (b * nt + i, z_col)),
                  pl.BlockSpec((1, HEAD_DIM), lambda b, i: (0, 0))],
        out_specs=pl.BlockSpec((tm, 4 * LANES), lambda b, i: (b * nt + i, 0)),
        out_shape=jax.ShapeDtypeStruct((n_batch * seq, 4 * LANES), F32),
        compiler_params=_cparams(("parallel", "parallel")),
        name="gdn_post",
    )(o, z, norm_w.reshape(1, HEAD_DIM))


def _gdn_front(x, qkv_col, gate_col, conv_w, a_log, dt_bias, n_batch, seq, chunk, pad_period=0, pad_rows=0):
    rows = n_batch * seq
    beta, gc = _gdn_gates(x, gate_col, rows, a_log, dt_bias, chunk, pad_period, pad_rows)
    (q,) = _gdn_prep(x, qkv_col, conv_w, beta, gc, n_batch, seq, "q")
    k, kb, gcb = _gdn_prep(x, qkv_col + 4, conv_w, beta, gc, n_batch, seq, "k")
    (vb,) = _gdn_prep(x, qkv_col + 8, conv_w, beta, gc, n_batch, seq, "v")
    gc_row = jnp.transpose(gc[:, GDN_HEADS:2 * GDN_HEADS].reshape(n_batch, seq, GDN_HEADS), (0, 2, 1))
    gc_row = gc_row.reshape(n_batch * GDN_HEADS, seq // chunk, chunk)
    return _gdn_intra(q, k, kb, vb, gcb, gc_row, chunk, min(seq, GDN_INTRA_ROWS))


TM = 768
TM_SEG = 512


def _pad_cols(w, n):
    return jnp.pad(w, ((0, 0), (0, n - w.shape[1])))


def _seq_tails(proj, n_batch, seq, n_rows, col, width):
    return jnp.stack([proj[(b + 1) * seq - n_rows:(b + 1) * seq, col:col + width] for b in range(n_batch)])


GDN_DEC_PERIOD = 8
EVEN_COLS = 3200


def kernel(x_prompt, x_sample, cache_k_moba, cache_v_moba, state_gdn, state_gdn_conv, cache_k_swa, cache_v_swa, state_lru, state_lru_conv, page_table, w_in_even, w_out_even, gdn_conv_w, gdn_a_log, gdn_dt_bias, gdn_norm_w, w_in_odd, w_out_odd, swa_sinks, lru_conv_w, lru_conv_b, lru_w_a, lru_b_a, lru_w_x, lru_b_x, lru_lambda, rel_bias, ln_g, ln_b, router_w, router_b, w_gate, w_up, w_down):
    n_pr, seq, d = x_prompt.shape
    n_dec, dec_seq, _ = x_sample.shape
    tp = n_pr * seq
    ts = n_dec * dec_seq
    n_pool = cache_k_moba.shape[0]
    kva = KV_HEADS_A * HEAD_DIM
    kvc = KV_HEADS_C * HEAD_DIM
    x_segs = [x_prompt.reshape(tp, d), x_sample.reshape(ts, d)]

    rw_pad = _pad_cols(router_w, LANES).T
    rb_pad = router_b.reshape(N_EXPERTS, 1)
    experts = [_group_experts(w_gate[layer], w_up[layer], w_down[layer]) for layer in range(DEPTH)]

    proj = _mm(x_segs, _pad_cols(w_in_even[0], EVEN_COLS).astype(BF16), TM_SEG, 640)
    ps = proj[tp:]
    moba_k_prompt = proj[:tp, 512:512 + kva].reshape(n_pr, seq, 1, KV_HEADS_A, HEAD_DIM)
    moba_v_prompt = proj[:tp, 768:768 + kva].reshape(n_pr, seq, 1, KV_HEADS_A, HEAD_DIM)
    k_new = ps[:, 512:512 + kva].reshape(n_dec, dec_seq, kva)
    v_new = ps[:, 768:768 + kva].reshape(n_dec, dec_seq, kva)
    moba_k_sample = k_new.reshape(n_dec, dec_seq, 1, KV_HEADS_A, HEAD_DIM)
    moba_v_sample = v_new.reshape(n_dec, dec_seq, 1, KV_HEADS_A, HEAD_DIM)

    oa_p = _moba_prompt(proj, rel_bias, n_pr, seq, 0, 4, 6)
    oa_s = _moba_sample(ps[:, :512].reshape(n_dec, dec_seq, N_ATT_HEADS, HEAD_DIM), k_new, v_new,
                        jnp.swapaxes(cache_k_moba.reshape(n_pool, PAGE_SIZE, -1), 1, 2),
                        jnp.swapaxes(cache_v_moba.reshape(n_pool, PAGE_SIZE, -1), 1, 2),
                        page_table, rel_bias, layer=0)

    parts = _gdn_front(proj, 8, 24, gdn_conv_w[0], gdn_a_log[0], gdn_dt_bias[0], n_pr, seq, GDN_CHUNK)
    o_p, s_fin = _gdn_scan(*parts, GDN_CHUNK, GDN_HEADS, 512)
    ob_p = _gdn_post(o_p, proj, 5, gdn_norm_w[0], n_pr, seq, 512)
    gdn_state_prompt = s_fin.reshape(1, n_pr, GDN_HEADS, HEAD_DIM, HEAD_DIM)
    gdn_conv_prompt = _seq_tails(proj, n_pr, seq, 3, 1024, GDN_CONV_CH)[None]

    period = GDN_DEC_PERIOD
    lead = period - dec_seq
    qkv_s = ps[:, 1024:1024 + GDN_CONV_CH].reshape(n_dec, dec_seq, GDN_CONV_CH)
    x8 = jnp.concatenate([jnp.zeros((n_dec, lead - 3, GDN_CONV_CH), F32), state_gdn_conv[0], qkv_s], axis=1)
    tail8 = jnp.pad(ps[:, 2560:EVEN_COLS].reshape(n_dec, dec_seq, EVEN_COLS - 2560), ((0, 0), (lead, 0), (0, 0)))
    xs = jnp.concatenate([x8, tail8], axis=-1).reshape(n_dec * period, EVEN_COLS - 1024)
    parts = _gdn_front(xs, 0, 16, gdn_conv_w[0], gdn_a_log[0], gdn_dt_bias[0], 1, n_dec * period, period,
                       period, lead)
    o_s, gdn_state_sample = _gdn_step(state_gdn[0], *parts, period)
    ob_s = _gdn_post(o_s, xs, 3, gdn_norm_w[0], 1, n_dec * period, 512)
    ob_s = ob_s.reshape(n_dec, period, 512)[:, lead:].reshape(ts, 512)
    gdn_conv_sample = qkv_s[None, :, dec_seq - 3:]

    h = _mm_ln([oa_p, oa_s.reshape(ts, 512)], [ob_p, ob_s], w_out_even[0].astype(BF16), x_segs,
               ln_g[0, 0], ln_b[0, 0], TM_SEG)
    h = _moe_ln(h, rw_pad, rb_pad, *experts[0], ln_g[0, 1], ln_b[0, 1], TM)

    proj = _mm([h], w_in_odd[0].astype(BF16), TM, 896)
    ps = proj[tp:]
    wa_bd, wx_bd = _block_diag(lru_w_a[0]), _block_diag(lru_w_x[0])
    oc_p = _swa_prompt(proj, rel_bias, swa_sinks[0], n_pr, seq, 0, 4, 5)
    od_p, lru_p = _lru_prompt(proj, n_pr, seq, 6, 10, lru_conv_w[0], lru_conv_b[0], wa_bd, lru_b_a[0],
                              wx_bd, lru_b_x[0], lru_lambda[0])
    swa_k_prompt = _seq_tails(proj, n_pr, seq, WINDOW, 512, kvc).reshape(1, n_pr, WINDOW, KV_HEADS_C, HEAD_DIM)
    swa_v_prompt = _seq_tails(proj, n_pr, seq, WINDOW, 640, kvc).reshape(1, n_pr, WINDOW, KV_HEADS_C, HEAD_DIM)
    lru_conv_prompt = _seq_tails(proj, n_pr, seq, 3, 768, D_RNN)[None]

    nbuf = cache_k_swa.shape[2]
    kc_s = ps[:, 512:512 + kvc].reshape(n_dec, dec_seq, kvc)
    vc_s = ps[:, 640:640 + kvc].reshape(n_dec, dec_seq, kvc)
    buf_k = cache_k_swa[0].reshape(n_dec, nbuf, kvc)
    buf_v = cache_v_swa[0].reshape(n_dec, nbuf, kvc)
    oc_s = _swa_sample(ps[:, :512].reshape(n_dec, dec_seq, N_ATT_HEADS, HEAD_DIM), kc_s, vc_s, buf_k, buf_v,
                       rel_bias, swa_sinks[0])
    swa_k_sample = jnp.concatenate([buf_k, kc_s], axis=1)[:, dec_seq:].reshape(1, n_dec, nbuf, KV_HEADS_C, HEAD_DIM)
    swa_v_sample = jnp.concatenate([buf_v, vc_s], axis=1)[:, dec_seq:].reshape(1, n_dec, nbuf, KV_HEADS_C, HEAD_DIM)
    xr_s = ps[:, 768:768 + D_RNN].reshape(n_dec, dec_seq, D_RNN)
    gr_s = ps[:, 1280:1280 + D_RNN].reshape(n_dec, dec_seq, D_RNN)
    od_s, lru_s = _lru_sample(xr_s, gr_s, state_lru[0], state_lru_conv[0], lru_conv_w[0], lru_conv_b[0],
                              wa_bd, lru_b_a[0], wx_bd, lru_b_x[0], lru_lambda[0])
    lru_conv_sample = jnp.concatenate([state_lru_conv[0], xr_s], axis=1)[None, :, dec_seq:]

    h = _mm_ln([oc_p, oc_s.reshape(ts, 512)], [od_p, od_s.reshape(ts, 512)], w_out_odd[0].astype(BF16), [h],
               ln_g[1, 0], ln_b[1, 0], TM_SEG)
    h = _moe_ln(h, rw_pad, rb_pad, *experts[1], ln_g[1, 1], ln_b[1, 1], TM)

    y_prompt = h[:tp].reshape(n_pr, seq, d)
    y_sample = h[tp:].reshape(n_dec, dec_seq, d)
    return (y_prompt, y_sample, moba_k_prompt, moba_v_prompt, moba_k_sample, moba_v_sample,
            gdn_state_prompt, gdn_state_sample[None], gdn_conv_prompt, gdn_conv_sample,
            swa_k_prompt, swa_v_prompt, swa_k_sample, swa_v_sample,
            lru_p[None], lru_s[None], lru_conv_prompt, lru_conv_sample)
```

```python
import functools
import math

import jax
import jax.numpy as jnp
import numpy as np
from jax import lax
from jax.experimental import pallas as pl
from jax.experimental.pallas import tpu as pltpu

F32 = jnp.float32
BF16 = jnp.bfloat16

HEAD_DIM = 64
N_ATT_HEADS = 8
KV_HEADS_A = 4
KV_HEADS_C = 2
MOBA_BLOCK = 256
MOBA_TOPK = 3
PAGE_SIZE = 128
GDN_HEADS = 8
GDN_CHUNK = 64
GDN_CONV_CH = 1536
CONV_WIDTH = 4
WINDOW = 128
D_RNN = 512
LRU_C = 8.0
T5_BUCKETS = 32
T5_MAX_DIST = 128
N_EXPERTS = 16
N_GROUPS = 4
EXPERTS_PER_GROUP = 4
D_EXPERT = 256
DEPTH = 2
DEEPNORM_ALPHA = (2 * DEPTH) ** 0.25
LN_EPS = 1e-5
NORM_EPS = 1e-6

LANES = 128
VMEM_LIMIT = 56 * 1024 * 1024
NEG_INF = float("-inf")


def _cparams(sem):
    return pltpu.CompilerParams(dimension_semantics=sem, vmem_limit_bytes=VMEM_LIMIT)


def _dot(a, b):
    return jnp.dot(a.astype(BF16), b.astype(BF16), preferred_element_type=F32)


def _dot_nt(a, b):
    return lax.dot_general(a.astype(BF16), b.astype(BF16), (((1,), (1,)), ((), ())),
                           preferred_element_type=F32)


def _seg_specs(segs, tm):
    specs, bounds, lo = [], [], 0
    for a in segs:
        assert a.shape[0] % tm == 0
        nt = a.shape[0] // tm
        specs.append(pl.BlockSpec((tm, a.shape[1]), lambda i, lo=lo, nt=nt: (jnp.clip(i - lo, 0, nt - 1), 0)))
        lo += nt
        bounds.append(lo)
    return specs, bounds


def _seg_tile(i, refs, bounds):
    x = refs[-1][...]
    for r, hi in zip(reversed(refs[:-1]), reversed(bounds[:-1])):
        x = jnp.where(i < hi, r[...], x)
    return x


def _mm_kernel(*refs, tn, bounds):
    ns = len(bounds)
    w_ref, o_ref = refs[ns], refs[ns + 1]
    xb = _seg_tile(pl.program_id(0), refs[:ns], bounds).astype(BF16)
    for c in range(w_ref.shape[1] // tn):
        o_ref[:, c * tn:(c + 1) * tn] = jnp.dot(xb, w_ref[:, c * tn:(c + 1) * tn], preferred_element_type=F32)


def _mm(x_segs, w, tm, tn):
    k, n = w.shape
    assert n % tn == 0
    specs, bounds = _seg_specs(x_segs, tm)
    m = bounds[-1] * tm
    return pl.pallas_call(
        functools.partial(_mm_kernel, tn=tn, bounds=tuple(bounds)),
        grid=(bounds[-1],),
        in_specs=specs + [pl.BlockSpec((k, n), lambda i: (0, 0))],
        out_specs=pl.BlockSpec((tm, n), lambda i: (i, 0)),
        out_shape=jax.ShapeDtypeStruct((m, n), F32),
        compiler_params=_cparams(("parallel",)),
        name="in_proj",
    )(*x_segs, w)


def _layer_norm_rows(z, g, b):
    mu = jnp.mean(z, axis=-1, keepdims=True)
    zc = z - mu
    var = jnp.mean(zc * zc, axis=-1, keepdims=True)
    return zc * lax.rsqrt(var + LN_EPS) * g + b


def _mm_ln_kernel(*refs, bounds_a, bounds_b, bounds_h):
    na, nb, nh = len(bounds_a), len(bounds_b), len(bounds_h)
    xa_refs, xb_refs, h_refs = refs[:na], refs[na:na + nb], refs[na + nb:na + nb + nh]
    wa_ref, wb_ref, g_ref, b_ref, o_ref = refs[na + nb + nh:]
    i = pl.program_id(0)
    f = (jnp.dot(_seg_tile(i, xa_refs, bounds_a).astype(BF16), wa_ref[...], preferred_element_type=F32)
         + jnp.dot(_seg_tile(i, xb_refs, bounds_b).astype(BF16), wb_ref[...], preferred_element_type=F32))
    o_ref[...] = _layer_norm_rows(DEEPNORM_ALPHA * _seg_tile(i, h_refs, bounds_h) + f, g_ref[...], b_ref[...])


def _mm_ln(xa_segs, xb_segs, w, h_segs, g, b, tm):
    ka, kb = xa_segs[0].shape[1], xb_segs[0].shape[1]
    n = w.shape[1]
    assert ka == kb
    sa, ba = _seg_specs(xa_segs, tm)
    sb, bb = _seg_specs(xb_segs, tm)
    sh, bh = _seg_specs(h_segs, tm)
    assert ba[-1] == bb[-1] == bh[-1]
    return pl.pallas_call(
        functools.partial(_mm_ln_kernel, bounds_a=tuple(ba), bounds_b=tuple(bb), bounds_h=tuple(bh)),
        grid=(ba[-1],),
        in_specs=sa + sb + sh + [pl.BlockSpec((ka, n), lambda i: (0, 0)),
                                 pl.BlockSpec((kb, n), lambda i: (1, 0)),
                                 pl.BlockSpec((1, n), lambda i: (0, 0)),
                                 pl.BlockSpec((1, n), lambda i: (0, 0))],
        out_specs=pl.BlockSpec((tm, n), lambda i: (i, 0)),
        out_shape=jax.ShapeDtypeStruct((ba[-1] * tm, n), F32),
        compiler_params=_cparams(("parallel",)),
        name="out_proj_ln",
    )(*xa_segs, *xb_segs, *h_segs, w, w, g.reshape(1, n), b.reshape(1, n))


def _moe_ln_kernel(t_ref, rw_ref, rb_ref, wg_ref, wu_ref, wd_ref, g_ref, b_ref, o_ref,
                   gates_ref, tb_ref, hb_ref, acc_ref):
    e = pl.program_id(1)

    @pl.when(e == 0)
    def _route():
        t = t_ref[...]
        logits = _dot_nt(rw_ref[...], t)[:N_EXPERTS]
        s = jax.nn.sigmoid(logits)
        sb = s + rb_ref[...]
        row = lax.broadcasted_iota(jnp.int32, sb.shape, 0)
        grp = row // EXPERTS_PER_GROUP

        def max_first(vals):
            mx = jnp.max(vals, axis=0, keepdims=True)
            return mx, jnp.min(jnp.where(vals == mx, row, N_EXPERTS), axis=0, keepdims=True)

        best = None
        gsel = None
        for gi in range(N_GROUPS):
            vals = jnp.where(grp == gi, sb, NEG_INF)
            m1, i1 = max_first(vals)
            m2 = jnp.max(jnp.where(row == i1, NEG_INF, vals), axis=0, keepdims=True)
            score = m1 + m2
            if gi == 0:
                best, gsel = score, jnp.zeros_like(i1)
            else:
                upd = score > best
                best = jnp.where(upd, score, best)
                gsel = jnp.where(upd, gi, gsel)
        vals = jnp.where(grp == gsel, sb, NEG_INF)
        _, i1 = max_first(vals)
        _, i2 = max_first(jnp.where(row == i1, NEG_INF, vals))
        s1 = jnp.sum(jnp.where(row == i1, s, 0.0), axis=0, keepdims=True)
        s2 = jnp.sum(jnp.where(row == i2, s, 0.0), axis=0, keepdims=True)
        den = s1 + s2
        gates_t = jnp.where(row == i1, s1 / den, 0.0) + jnp.where(row == i2, s2 / den, 0.0)
        gates_t = jnp.concatenate([gates_t, jnp.zeros((LANES - N_EXPERTS, gates_t.shape[1]), F32)], axis=0)
        gates_ref[...] = gates_t.T
        tb_ref[...] = t.astype(BF16)
        acc_ref[...] = jnp.zeros_like(acc_ref)

    tb = tb_ref[...]
    gates = gates_ref[...]
    lane = lax.broadcasted_iota(jnp.int32, gates.shape, 1)
    for j in range(EXPERTS_PER_GROUP):
        sl = slice(j * D_EXPERT, (j + 1) * D_EXPERT)
        ge = jnp.sum(jnp.where(lane == e * EXPERTS_PER_GROUP + j, gates, 0.0), axis=1, keepdims=True)
        g = jnp.dot(tb, wg_ref[j], preferred_element_type=F32)
        u = jnp.dot(tb, wu_ref[j], preferred_element_type=F32)
        hb_ref[:, sl] = ((g * jax.nn.sigmoid(g)) * u * ge).astype(BF16)
    acc_ref[...] += jnp.dot(hb_ref[...], wd_ref[...], preferred_element_type=F32)

    @pl.when(e == pl.num_programs(1) - 1)
    def _finish():
        o_ref[...] = _layer_norm_rows(DEEPNORM_ALPHA * t_ref[...] + acc_ref[...], g_ref[...], b_ref[...])


def _group_experts(w_gate, w_up, w_down):
    ne, d, f = w_gate.shape
    es = EXPERTS_PER_GROUP
    return w_gate.astype(BF16), w_up.astype(BF16), w_down.astype(BF16).reshape(ne // es, es * f, d)


def _moe_ln(t, router_w_pad, router_b_pad, wg, wu, wd, g, b, tm):
    m, d = t.shape
    ng, gf, _ = wd.shape
    es, f = EXPERTS_PER_GROUP, wg.shape[2]
    assert m % tm == 0
    return pl.pallas_call(
        _moe_ln_kernel,
        grid=(m // tm, ng),
        in_specs=[pl.BlockSpec((tm, d), lambda i, e: (i, 0)),
                  pl.BlockSpec((LANES, d), lambda i, e: (0, 0)),
                  pl.BlockSpec((N_EXPERTS, 1), lambda i, e: (0, 0)),
                  pl.BlockSpec((es, d, f), lambda i, e: (e, 0, 0)),
                  pl.BlockSpec((es, d, f), lambda i, e: (e, 0, 0)),
                  pl.BlockSpec((None, gf, d), lambda i, e: (e, 0, 0)),
                  pl.BlockSpec((1, d), lambda i, e: (0, 0)),
                  pl.BlockSpec((1, d), lambda i, e: (0, 0))],
        out_specs=pl.BlockSpec((tm, d), lambda i, e: (i, 0)),
        out_shape=jax.ShapeDtypeStruct((m, d), F32),
        scratch_shapes=[pltpu.VMEM((tm, LANES), F32),
                        pltpu.VMEM((tm, d), BF16),
                        pltpu.VMEM((tm, gf), BF16),
                        pltpu.VMEM((tm, d), F32)],
        compiler_params=_cparams(("parallel", "arbitrary")),
        name="moe_ln",
    )(t, router_w_pad, router_b_pad, wg, wu, wd, g.reshape(1, d), b.reshape(1, d))


def _t5_bucket_np(rel):
    n = np.maximum(rel, 0)
    max_exact = T5_BUCKETS // 2
    nf = np.maximum(n, 1).astype(np.float32)
    large = max_exact + (np.log(nf / np.float32(max_exact)) / np.float32(math.log(T5_MAX_DIST / max_exact))
                         * np.float32(T5_BUCKETS - max_exact)).astype(np.int32)
    large = np.minimum(large, T5_BUCKETS - 1)
    return np.where(n < max_exact, n, large).astype(np.int32)


def _t5_thresholds():
    b = _t5_bucket_np(np.arange(4 * T5_MAX_DIST))
    return [int(np.argmax(b >= k)) for k in range(1, T5_BUCKETS)]


def _bias_tile_kernel(rb_ref, o_ref, *, off, sign, lo, hi):
    h = pl.program_id(0)
    row = lax.broadcasted_iota(jnp.int32, o_ref.shape, 0)
    col = lax.broadcasted_iota(jnp.int32, o_ref.shape, 1)
    rel = off + sign * (row - col)
    t = jnp.full(o_ref.shape, rb_ref[0, h], F32)
    for k, thr in enumerate(_t5_thresholds(), start=1):
        t = jnp.where(rel >= thr, rb_ref[k, h], t)
    if lo is not None:
        t = jnp.where((rel >= lo) & (rel <= hi), t, NEG_INF)
    o_ref[...] = t


def _bias_tiles(rel_bias, rows, cols, off, sign=1, lo=None, hi=None):
    return pl.pallas_call(
        functools.partial(_bias_tile_kernel, off=off, sign=sign, lo=lo, hi=hi),
        grid=(N_ATT_HEADS,),
        in_specs=[pl.BlockSpec(memory_space=pltpu.SMEM)],
        out_specs=pl.BlockSpec((None, rows, cols), lambda h: (h, 0, 0)),
        out_shape=jax.ShapeDtypeStruct((N_ATT_HEADS, rows, cols), F32),
        compiler_params=_cparams(("parallel",)),
        name="bias_tiles",
    )(rel_bias)


def _moba_prompt_kernel(q_ref, k_ref, v_ref, bd_ref, bp_ref, bf_ref, o_ref, kmean_ref, vt_ref, sel_ref):
    i = pl.program_id(2)
    blk = MOBA_BLOCK
    nblk = k_ref.shape[0] // blk
    hd = HEAD_DIM

    @pl.when(i == 0)
    def _per_sequence():
        for n in range(nblk):
            kmean_ref[n:n + 1, :] = jnp.mean(k_ref[n * blk:(n + 1) * blk, :], axis=0, keepdims=True)
            vt_ref[n] = v_ref[n * blk:(n + 1) * blk, :].T.astype(BF16)

    q4 = q_ref[...] * (hd ** -0.5)
    q2b = []
    for kvl in range(2):
        lo = kvl * 2 * hd
        q2 = jnp.concatenate([q4[:, lo:lo + hd], q4[:, lo + hd:lo + 2 * hd]], axis=0)
        q2b.append(q2.astype(BF16))
        gate = _dot_nt(kmean_ref[:, kvl * hd:(kvl + 1) * hd], q2)
        row = lax.broadcasted_iota(jnp.int32, gate.shape, 0)
        gate = jnp.where(row < i, gate, NEG_INF)
        sel = jnp.zeros(gate.shape, F32)
        for _ in range(MOBA_TOPK):
            mx = jnp.max(gate, axis=0, keepdims=True)
            idx = jnp.min(jnp.where(gate == mx, row, nblk), axis=0, keepdims=True)
            hit = (row == idx) & (mx > NEG_INF)
            sel = jnp.where(hit, 1.0, sel)
            gate = jnp.where(row == idx, NEG_INF, gate)
        sel_ref[kvl] = sel

    def scores(j, kvl):
        r0 = pl.multiple_of(j * blk, blk)
        kb = k_ref[pl.ds(r0, blk), :][:, kvl * hd:(kvl + 1) * hd].astype(BF16)
        return _dot_nt(kb, q2b[kvl])

    def update(state, s, j, kvl):
        m_old, l_old, acc = state
        m_new = jnp.maximum(m_old, jnp.max(s, axis=0, keepdims=True))
        alpha = jnp.exp(m_old - m_new)
        p = jnp.exp(s - m_new)
        l_new = alpha * l_old + jnp.sum(p, axis=0, keepdims=True)
        vt = vt_ref[j][kvl * hd:(kvl + 1) * hd, :]
        acc = alpha * acc + jnp.dot(vt, p.astype(BF16), preferred_element_type=F32)
        return m_new, l_new, acc

    def live_row(j, kvl):
        return sel_ref[kvl, pl.ds(j, 1), :] > 0.0

    init = (jnp.full((1, 2 * blk), NEG_INF, F32), jnp.zeros((1, 2 * blk), F32), jnp.zeros((hd, 2 * blk), F32))
    jp = jnp.maximum(i - 1, 0)
    states = []
    for kvl in range(2):
        st = update(init, scores(i, kvl) + bd_ref[kvl], i, kvl)
        s = jnp.where(live_row(jp, kvl), scores(jp, kvl) + bp_ref[kvl], NEG_INF)
        states.append(update(st, s, jp, kvl))

    n_pairs = i // 2

    def far_pair(state, pr, kvl):
        j0 = 2 * pr
        r0 = pl.multiple_of(j0 * blk, 2 * blk)
        second_far = j0 + 1 < i - 1
        kb = k_ref[pl.ds(r0, 2 * blk), :][:, kvl * hd:(kvl + 1) * hd].astype(BF16)
        s = _dot_nt(kb, q2b[kvl])
        far_bias = bf_ref[kvl]
        s0 = s[:blk] + jnp.where(live_row(j0, kvl), far_bias, NEG_INF)
        s1 = s[blk:] + jnp.where(live_row(j0 + 1, kvl) & second_far, far_bias, NEG_INF)
        m_old, l_old, acc = state
        m_new = jnp.maximum(m_old, jnp.maximum(jnp.max(s0, axis=0, keepdims=True),
                                               jnp.max(s1, axis=0, keepdims=True)))
        alpha = jnp.exp(m_old - m_new)
        p = jnp.concatenate([jnp.exp(s0 - m_new), jnp.exp(s1 - m_new)], axis=0)
        l_new = alpha * l_old + jnp.sum(p, axis=0, keepdims=True)
        rows = slice(kvl * hd, (kvl + 1) * hd)
        vt = jnp.concatenate([vt_ref[j0][rows, :], vt_ref[j0 + 1][rows, :]], axis=1)
        acc = alpha * acc + jnp.dot(vt, p.astype(BF16), preferred_element_type=F32)
        return m_new, l_new, acc

    def far(pr, carry):
        return tuple(far_pair(carry[kvl], pr, kvl) for kvl in range(2))

    states = lax.fori_loop(0, n_pairs, far, tuple(states))
    ot = jnp.concatenate([acc / l for (_, l, acc) in states], axis=0)
    o = ot.T
    for kvl in range(2):
        for hh in range(2):
            o_ref[:, (2 * kvl + hh) * hd:(2 * kvl + hh + 1) * hd] = o[hh * blk:(hh + 1) * blk, kvl * hd:(kvl + 1) * hd]


def _moba_prompt(proj, bias_rel, n_batch, seq, q_col, k_col, v_col):
    blk = MOBA_BLOCK
    nblk = seq // blk

    def pair_up(t):
        return jnp.transpose(t.reshape(KV_HEADS_A, 2, blk, blk), (0, 2, 1, 3)).reshape(KV_HEADS_A, blk, 2 * blk)

    diag = pair_up(_bias_tiles(bias_rel, blk, blk, 0, sign=-1, lo=0, hi=2 * blk))
    prev = pair_up(_bias_tiles(bias_rel, blk, blk, blk, sign=-1))
    far = jnp.broadcast_to(bias_rel[T5_BUCKETS - 1].reshape(KV_HEADS_A, 1, 2, 1), (KV_HEADS_A, 1, 2, blk))
    far = far.reshape(KV_HEADS_A, 1, 2 * blk)
    rows_per_b = seq // blk
    return pl.pallas_call(
        _moba_prompt_kernel,
        grid=(n_batch, 2, nblk),
        in_specs=[pl.BlockSpec((blk, 2 * LANES), lambda b, p, i: (b * rows_per_b + i, q_col // 2 + p)),
                  pl.BlockSpec((seq, LANES), lambda b, p, i: (b, k_col + p)),
                  pl.BlockSpec((seq, LANES), lambda b, p, i: (b, v_col + p)),
                  pl.BlockSpec((2, blk, 2 * blk), lambda b, p, i: (p, 0, 0)),
                  pl.BlockSpec((2, blk, 2 * blk), lambda b, p, i: (p, 0, 0)),
                  pl.BlockSpec((2, 1, 2 * blk), lambda b, p, i: (p, 0, 0))],
        out_specs=pl.BlockSpec((blk, 2 * LANES), lambda b, p, i: (b * rows_per_b + i, p)),
        out_shape=jax.ShapeDtypeStruct((n_batch * seq, 4 * LANES), F32),
        scratch_shapes=[pltpu.VMEM((nblk, LANES), F32),
                        pltpu.VMEM((nblk, LANES, blk), BF16),
                        pltpu.VMEM((2, nblk, 2 * blk), F32)],
        compiler_params=_cparams(("parallel", "parallel", "arbitrary")),
        name="moba_prompt",
    )(proj, proj, proj, diag, prev, far)


MOBA_DEC_BLOCKS_PER_STEP = 16
PAGES_PER_BLOCK = MOBA_BLOCK // PAGE_SIZE


MOBA_DEC_SLOTS = 3


def _moba_sample_kernel(pt_ref, q_ref, kn_ref, vn_ref, b31_ref, bfar_ref, bown_ref, ck_hbm, cv_hbm, o_ref,
                        g_ref, m_ref, l_ref, part_ref, kbuf, vbuf, sem, *, n_blocks, nb, layer):
    npg = nb * PAGES_PER_BLOCK
    kv_w = kbuf.shape[2]
    b, s = pl.program_id(0), pl.program_id(1)
    n_steps = pl.num_programs(1)
    total = pl.num_programs(0) * n_steps
    g = b * n_steps + s
    q = q_ref[...]
    qb = q.astype(BF16)
    lane = lax.broadcasted_iota(jnp.int32, g_ref.shape, 1)

    def page_copies(step):
        bb = step // n_steps
        ss = step - bb * n_steps
        slot = lax.rem(step, MOBA_DEC_SLOTS)
        out = []
        for k in range(npg):
            page = pt_ref[bb, ss * npg + k]
            for hbm, buf in ((ck_hbm, kbuf), (cv_hbm, vbuf)):
                out.append(pltpu.make_async_copy(hbm.at[page, pl.ds(layer * kv_w, kv_w), :],
                                                 buf.at[slot, k], sem.at[slot]))
        return out

    @pl.when(g == 0)
    def _prime():
        for c in page_copies(g):
            c.start()

    @pl.when((g == 0) & (total > 1))
    def _prime_next():
        for c in page_copies(g + 1):
            c.start()

    @pl.when(s == 0)
    def _init():
        g_ref[...] = jnp.zeros_like(g_ref)
        m_ref[...] = jnp.zeros_like(m_ref)
        l_ref[...] = jnp.zeros_like(l_ref)

    for c in page_copies(g):
        c.wait()
    slot = lax.rem(g, MOBA_DEC_SLOTS)
    lane_k = lax.broadcasted_iota(jnp.int32, (kv_w, LANES), 1)
    kmean = jnp.zeros((kv_w, LANES), F32)

    for j in range(nb):
        n = s * nb + j
        kts = [kbuf[slot, j * PAGES_PER_BLOCK + t] for t in range(PAGES_PER_BLOCK)]
        vts = [vbuf[slot, j * PAGES_PER_BLOCK + t] for t in range(PAGES_PER_BLOCK)]
        sc = jnp.concatenate([_dot(qb, kt) for kt in kts], axis=1)
        ksum = kts[0].sum(axis=1, keepdims=True)
        for kt in kts[1:]:
            ksum = ksum + kt.sum(axis=1, keepdims=True)
        kmean = kmean + jnp.where(lane_k == n, ksum * (1.0 / MOBA_BLOCK), 0.0)
        sc = sc + jnp.where(n == n_blocks - 1, b31_ref[...], bfar_ref[...])
        mx = jnp.max(sc, axis=1, keepdims=True)
        p = jnp.exp(sc - mx)
        hit = lane == n
        m_ref[...] = jnp.where(hit, mx, m_ref[...])
        l_ref[...] = jnp.where(hit, jnp.sum(p, axis=1, keepdims=True), l_ref[...])
        o = None
        for t, vt in enumerate(vts):
            part = _dot_nt(p[:, t * PAGE_SIZE:(t + 1) * PAGE_SIZE], vt)
            o = part if o is None else o + part
        part_ref[n] = o
    g_ref[...] += _dot(qb, kmean)

    @pl.when(g + 2 < total)
    def _prefetch():
        for c in page_copies(g + 2):
            c.start()

    @pl.when(s == pl.num_programs(1) - 1)
    def _merge():
        gate = jnp.where(lane < n_blocks, g_ref[...], NEG_INF)
        sel = jnp.zeros(gate.shape, jnp.bool_)
        for _ in range(MOBA_TOPK):
            mxg = jnp.max(gate, axis=1, keepdims=True)
            idx = jnp.min(jnp.where(gate == mxg, lane, LANES), axis=1, keepdims=True)
            hit = (lane == idx) & (mxg > NEG_INF)
            sel = sel | hit
            gate = jnp.where(lane == idx, NEG_INF, gate)
        s_own = _dot_nt(qb, kn_ref[...]) + bown_ref[...]
        m_sel = jnp.where(sel, m_ref[...], NEG_INF)
        m_all = jnp.maximum(jnp.max(m_sel, axis=1, keepdims=True), jnp.max(s_own, axis=1, keepdims=True))
        w = jnp.where(sel, jnp.exp(m_sel - m_all), 0.0)
        p_own = jnp.exp(s_own - m_all)
        den = jnp.sum(w * l_ref[...], axis=1, keepdims=True) + jnp.sum(p_own, axis=1, keepdims=True)
        acc = _dot(p_own, vn_ref[...])
        for n in range(n_blocks):
            acc = acc + w[:, n:n + 1] * part_ref[n]
        o_ref[...] = acc / den


def _moba_sample(q_s, k_new, v_new, cache_k, cache_v, page_table, bias_rel, layer=0):
    n_dec, t_dec = q_s.shape[:2]
    n_pages = page_table.shape[1]
    n_blocks = n_pages // PAGES_PER_BLOCK
    nb = min(MOBA_DEC_BLOCKS_PER_STEP, n_blocks)
    npg = nb * PAGES_PER_BLOCK
    assert n_blocks % nb == 0 and n_blocks <= LANES and t_dec <= 8
    rows = N_ATT_HEADS * t_dec
    kv_w = KV_HEADS_A * HEAD_DIM
    head_kv = np.arange(N_ATT_HEADS) // (N_ATT_HEADS // KV_HEADS_A)
    qh = jnp.transpose(q_s, (0, 2, 1, 3)) * (HEAD_DIM ** -0.5)
    onehot = jnp.asarray(np.eye(KV_HEADS_A, dtype=np.float32)[head_kv])
    qbd = (qh[:, :, :, None, :] * onehot[None, :, None, :, None]).reshape(n_dec, rows, kv_w)
    pad = 8 - t_dec
    kn = jnp.pad(k_new, ((0, 0), (0, pad), (0, 0)))
    vn = jnp.pad(v_new, ((0, 0), (0, pad), (0, 0)))
    bt = bias_rel.T
    past = n_pages * PAGE_SIZE
    tq = np.arange(t_dec)[None, :, None]
    ik = np.arange(MOBA_BLOCK)[None, None, :]
    idx31 = _t5_bucket_np(past + tq - ((n_blocks - 1) * MOBA_BLOCK + ik))[0]
    b31 = bt[:, idx31].reshape(rows, MOBA_BLOCK)
    bfar = jnp.broadcast_to(bt[:, T5_BUCKETS - 1][:, None, None], (N_ATT_HEADS, t_dec, 1)).reshape(rows, 1)
    tk = np.arange(8)[None, :]
    rel_own = np.arange(t_dec)[:, None] - tk
    ok_own = (rel_own >= 0) & (tk < t_dec)
    bown = jnp.where(jnp.asarray(ok_own)[None], bt[:, _t5_bucket_np(rel_own)], NEG_INF).reshape(rows, 8)

    const2 = lambda b, s, pt: (0, 0)
    grid_spec = pltpu.PrefetchScalarGridSpec(
        num_scalar_prefetch=1,
        grid=(n_dec, n_blocks // nb),
        in_specs=[pl.BlockSpec((None, rows, kv_w), lambda b, s, pt: (b, 0, 0)),
                  pl.BlockSpec((None, 8, kv_w), lambda b, s, pt: (b, 0, 0)),
                  pl.BlockSpec((None, 8, kv_w), lambda b, s, pt: (b, 0, 0)),
                  pl.BlockSpec((rows, MOBA_BLOCK), const2),
                  pl.BlockSpec((rows, 1), const2),
                  pl.BlockSpec((rows, 8), const2),
                  pl.BlockSpec(memory_space=pl.ANY),
                  pl.BlockSpec(memory_space=pl.ANY)],
        out_specs=pl.BlockSpec((None, rows, kv_w), lambda b, s, pt: (b, 0, 0)),
        scratch_shapes=[pltpu.VMEM((rows, LANES), F32),
                        pltpu.VMEM((rows, LANES), F32),
                        pltpu.VMEM((rows, LANES), F32),
                        pltpu.VMEM((n_blocks, rows, kv_w), F32),
                        pltpu.VMEM((MOBA_DEC_SLOTS, npg, kv_w, PAGE_SIZE), F32),
                        pltpu.VMEM((MOBA_DEC_SLOTS, npg, kv_w, PAGE_SIZE), F32),
                        pltpu.SemaphoreType.DMA((MOBA_DEC_SLOTS,))],
    )
    out = pl.pallas_call(
        functools.partial(_moba_sample_kernel, n_blocks=n_blocks, nb=nb, layer=layer),
        grid_spec=grid_spec,
        out_shape=jax.ShapeDtypeStruct((n_dec, rows, kv_w), F32),
        compiler_params=_cparams(("arbitrary", "arbitrary")),
        name="moba_sample",
    )(page_table, qbd, kn, vn, b31, bfar, bown, cache_k, cache_v)
    out = out.reshape(n_dec, N_ATT_HEADS, t_dec, KV_HEADS_A, HEAD_DIM)
    out = out[:, np.arange(N_ATT_HEADS), :, head_kv, :]
    return jnp.transpose(out, (1, 2, 0, 3)).reshape(n_dec, t_dec, N_ATT_HEADS * HEAD_DIM)


SWA_GROUP = N_ATT_HEADS // KV_HEADS_C


def _sink_softmax_pv(parts, sink):
    m = sink
    for s, _ in parts:
        m = jnp.maximum(m, jnp.max(s, axis=1, keepdims=True))
    den = jnp.exp(sink - m)
    ps = []
    for s, _ in parts:
        p = jnp.exp(s - m)
        den = den + jnp.sum(p, axis=1, keepdims=True)
        ps.append(p)
    o = None
    for p, (_, v) in zip(ps, parts):
        t = _dot(p / den, v)
        o = t if o is None else o + t
    return o


def _swa_prompt_kernel(q_ref, kp_ref, ko_ref, vp_ref, vo_ref, bias_ref, sink_ref, o_ref):
    i = pl.program_id(1)
    w, hd, g = WINDOW, HEAD_DIM, SWA_GROUP
    q = q_ref[...] * (hd ** -0.5)
    kcat = jnp.concatenate([kp_ref[...], ko_ref[...]], axis=0)
    vt = jnp.concatenate([vp_ref[...], vo_ref[...]], axis=0).T.astype(BF16)
    row = lax.broadcasted_iota(jnp.int32, (2 * w, g * w), 0)
    has_key = (row >= w) | (i > 0)
    ots = []
    for c in range(KV_HEADS_C):
        qs = jnp.concatenate([q[:, (c * g + j) * hd:(c * g + j + 1) * hd] for j in range(g)], axis=0)
        s = _dot_nt(kcat[:, c * hd:(c + 1) * hd], qs) + bias_ref[c]
        s = jnp.where(has_key, s, NEG_INF)
        sink = sink_ref[c]
        m = jnp.maximum(sink, jnp.max(s, axis=0, keepdims=True))
        p = jnp.exp(s - m)
        den = jnp.sum(p, axis=0, keepdims=True) + jnp.exp(sink - m)
        ots.append(jnp.dot(vt[c * hd:(c + 1) * hd, :], (p / den).astype(BF16), preferred_element_type=F32))
    o = jnp.concatenate(ots, axis=0).T
    for c in range(KV_HEADS_C):
        for j in range(g):
            o_ref[:, (c * g + j) * hd:(c * g + j + 1) * hd] = o[j * w:(j + 1) * w, c * hd:(c + 1) * hd]


def _swa_prompt(proj, bias_rel, sinks, n_batch, seq, q_col, k_col, v_col):
    w, g = WINDOW, SWA_GROUP
    nb = seq // w
    band = _bias_tiles(bias_rel, 2 * w, w, w, sign=-1, lo=0, hi=w)
    band = jnp.transpose(band.reshape(KV_HEADS_C, g, 2 * w, w), (0, 2, 1, 3)).reshape(KV_HEADS_C, 2 * w, g * w)
    sink = jnp.broadcast_to(sinks.reshape(KV_HEADS_C, 1, g, 1), (KV_HEADS_C, 1, g, w)).reshape(KV_HEADS_C, 1, g * w)
    own = lambda c: (lambda b, i: (b * nb + i, c))
    prev = lambda c: (lambda b, i: (b * nb + jnp.maximum(i - 1, 0), c))
    return pl.pallas_call(
        _swa_prompt_kernel,
        grid=(n_batch, nb),
        in_specs=[pl.BlockSpec((w, 4 * LANES), lambda b, i: (b * nb + i, q_col // 4)),
                  pl.BlockSpec((w, LANES), prev(k_col)),
                  pl.BlockSpec((w, LANES), own(k_col)),
                  pl.BlockSpec((w, LANES), prev(v_col)),
                  pl.BlockSpec((w, LANES), own(v_col)),
                  pl.BlockSpec((KV_HEADS_C, 2 * w, g * w), lambda b, i: (0, 0, 0)),
                  pl.BlockSpec((KV_HEADS_C, 1, g * w), lambda b, i: (0, 0, 0))],
        out_specs=pl.BlockSpec((w, 4 * LANES), lambda b, i: (b * nb + i, 0)),
        out_shape=jax.ShapeDtypeStruct((n_batch * seq, 4 * LANES), F32),
        compiler_params=_cparams(("parallel", "parallel")),
        name="swa_prompt",
    )(proj, proj, proj, proj, proj, band, sink)


SWA_DEC_SEQS_PER_STEP = 8


def _swa_sample_kernel(q_ref, kb_ref, vb_ref, kn_ref, vn_ref, bbuf_ref, bnew_ref, sink_ref, o_ref):
    for i in range(SWA_DEC_SEQS_PER_STEP):
        q = q_ref[i]
        s_buf = _dot_nt(q, kb_ref[i]) + bbuf_ref[...]
        s_new = _dot_nt(q, kn_ref[i]) + bnew_ref[...]
        o_ref[i] = _sink_softmax_pv([(s_buf, vb_ref[i]), (s_new, vn_ref[i])], sink_ref[...])


def _swa_sample(q_s, k_new, v_new, buf_k, buf_v, bias_rel, sinks):
    n_dec, t_dec = q_s.shape[:2]
    nbuf = buf_k.shape[1]
    rows = N_ATT_HEADS * t_dec
    kv_w = KV_HEADS_C * HEAD_DIM
    sb = SWA_DEC_SEQS_PER_STEP
    assert n_dec % sb == 0 and t_dec <= 8
    head_kv = np.arange(N_ATT_HEADS) // SWA_GROUP
    qh = jnp.transpose(q_s, (0, 2, 1, 3)) * (HEAD_DIM ** -0.5)
    onehot = jnp.asarray(np.eye(KV_HEADS_C, dtype=np.float32)[head_kv])
    qbd = (qh[:, :, :, None, :] * onehot[None, :, None, :, None]).reshape(n_dec, rows, kv_w)
    pad = 8 - t_dec
    kn = jnp.pad(k_new, ((0, 0), (0, pad), (0, 0)))
    vn = jnp.pad(v_new, ((0, 0), (0, pad), (0, 0)))
    bt = bias_rel.T
    t = np.arange(t_dec)[:, None]
    rel_buf = nbuf + t - np.arange(nbuf)[None, :]
    ok_buf = (rel_buf >= 0) & (rel_buf <= WINDOW)
    bbuf = jnp.where(jnp.asarray(ok_buf)[None], bt[:, _t5_bucket_np(rel_buf)], NEG_INF).reshape(rows, nbuf)
    tk = np.arange(8)[None, :]
    rel_new = t - tk
    ok_new = (rel_new >= 0) & (tk < t_dec)
    bnew = jnp.where(jnp.asarray(ok_new)[None], bt[:, _t5_bucket_np(rel_new)], NEG_INF).reshape(rows, 8)
    sink = jnp.broadcast_to(sinks[:, None, None], (N_ATT_HEADS, t_dec, 1)).reshape(rows, 1)
    seq3 = lambda r, c: pl.BlockSpec((sb, r, c), lambda i: (i, 0, 0))
    const2 = lambda r, c: pl.BlockSpec((r, c), lambda i: (0, 0))
    out = pl.pallas_call(
        _swa_sample_kernel,
        grid=(n_dec // sb,),
        in_specs=[seq3(rows, kv_w), seq3(nbuf, kv_w), seq3(nbuf, kv_w), seq3(8, kv_w), seq3(8, kv_w),
                  const2(rows, nbuf), const2(rows, 8), const2(rows, 1)],
        out_specs=seq3(rows, kv_w),
        out_shape=jax.ShapeDtypeStruct((n_dec, rows, kv_w), F32),
        compiler_params=_cparams(("parallel",)),
        name="swa_sample",
    )(qbd, buf_k, buf_v, kn, vn, bbuf, bnew, sink)
    out = out.reshape(n_dec, N_ATT_HEADS, t_dec, KV_HEADS_C, HEAD_DIM)
    out = out[:, np.arange(N_ATT_HEADS), :, head_kv, :]
    return jnp.transpose(out, (1, 2, 0, 3)).reshape(n_dec, t_dec, N_ATT_HEADS * HEAD_DIM)


def _softplus(x):
    return jnp.maximum(x, 0.0) + jnp.log1p(jnp.exp(-jnp.abs(x)))


def _neg_expm1(y):
    return -jnp.tanh(0.5 * y) * (jnp.exp(y) + 1.0)


def _shifted_rows(prev8, cur, k):
    if k == 0:
        return cur
    ext = jnp.concatenate([prev8, cur], axis=0)
    return pltpu.roll(ext, k, axis=0)[8:]


def _causal_conv_rows(prev8, cur, w):
    y = cur * w[CONV_WIDTH - 1:CONV_WIDTH, :]
    for k in range(1, CONV_WIDTH):
        y = y + _shifted_rows(prev8, cur, k) * w[CONV_WIDTH - 1 - k:CONV_WIDTH - k, :]
    return y


def _lru_gates(xc, wa, ba, wx, bx, sp):
    r = jax.nn.sigmoid(_dot(xc, wa) + ba)
    i = jax.nn.sigmoid(_dot(xc, wx) + bx)
    log_a = -LRU_C * r * sp
    a = jnp.exp(log_a)
    b = jnp.sqrt(_neg_expm1(2.0 * log_a)) * (i * xc)
    return a, b


LRU_ROWS = 256


def _lru_prompt_kernel(x_ref, gr_ref, cw_ref, cb_ref, wa_ref, ba_ref, wx_ref, bx_ref, lam_ref,
                       o_ref, hl_ref):
    n = x_ref.shape[0] // LRU_ROWS
    sp = _softplus(-lam_ref[...])
    row = lax.broadcasted_iota(jnp.int32, (LRU_ROWS, LANES), 0)

    def body(c, carry):
        prev8, h = carry
        r0 = pl.multiple_of(c * LRU_ROWS, LRU_ROWS)
        cur = x_ref[pl.ds(r0, LRU_ROWS), :]
        xc = _causal_conv_rows(prev8, cur, cw_ref[...]) + cb_ref[...]
        a, b = _lru_gates(xc, wa_ref[...], ba_ref[...], wx_ref[...], bx_ref[...], sp)
        d = 1
        while d < LRU_ROWS:
            a_s = jnp.where(row >= d, pltpu.roll(a, d, axis=0), 1.0)
            b_s = jnp.where(row >= d, pltpu.roll(b, d, axis=0), 0.0)
            b = a * b_s + b
            a = a * a_s
            d *= 2
        hs = a * h + b
        o_ref[pl.ds(r0, LRU_ROWS), :] = hs * jax.nn.gelu(gr_ref[pl.ds(r0, LRU_ROWS), :])
        return cur[LRU_ROWS - 8:], hs[LRU_ROWS - 1:]

    _, h = lax.fori_loop(0, n, body, (jnp.zeros((8, LANES), F32), jnp.zeros((1, LANES), F32)))
    hl_ref[...] = h


def _lru_prompt(proj, n_batch, seq, x_col, g_col, conv_w, conv_b, wa_bd, b_a, wx_bd, b_x, lam):
    ncb = D_RNN // LANES
    vec = lambda v: v.reshape(1, D_RNN)
    vspec = pl.BlockSpec((1, LANES), lambda b, c: (0, c))
    out, hl = pl.pallas_call(
        _lru_prompt_kernel,
        grid=(n_batch, ncb),
        in_specs=[pl.BlockSpec((seq, LANES), lambda b, c: (b, x_col + c)),
                  pl.BlockSpec((seq, LANES), lambda b, c: (b, g_col + c)),
                  pl.BlockSpec((CONV_WIDTH, LANES), lambda b, c: (0, c)),
                  vspec,
                  pl.BlockSpec((LANES, LANES), lambda b, c: (c, c)), vspec,
                  pl.BlockSpec((LANES, LANES), lambda b, c: (c, c)), vspec,
                  vspec],
        out_specs=[pl.BlockSpec((seq, LANES), lambda b, c: (b, c)),
                   pl.BlockSpec((None, 1, LANES), lambda b, c: (b, 0, c))],
        out_shape=[jax.ShapeDtypeStruct((n_batch * seq, D_RNN), F32),
                   jax.ShapeDtypeStruct((n_batch, 1, D_RNN), F32)],
        compiler_params=_cparams(("parallel", "parallel")),
        name="lru_prompt",
    )(proj, proj, conv_w, vec(conv_b), wa_bd, vec(b_a), wx_bd, vec(b_x), vec(lam))
    return out, hl.reshape(n_batch, D_RNN)


def _lru_sample_kernel(xp_ref, gr_ref, h0_ref, cw_ref, cb_ref, wa_ref, ba_ref, wx_ref, bx_ref, lam_ref,
                       o_ref, hl_ref):
    t_dec = gr_ref.shape[0]
    sp = _softplus(-lam_ref[...])
    h = h0_ref[...]
    for t in range(t_dec):
        xc = cb_ref[...]
        for j in range(CONV_WIDTH):
            xc = xc + xp_ref[t + j] * cw_ref[j:j + 1, :]
        a, b = _lru_gates(xc, wa_ref[...], ba_ref[...], wx_ref[...], bx_ref[...], sp)
        h = a * h + b
        o_ref[t] = h * jax.nn.gelu(gr_ref[t])
    hl_ref[...] = h


def _lru_sample(xr, gr, h0, conv0, conv_w, conv_b, wa_bd, b_a, wx_bd, b_x, lam):
    n_dec, t_dec, _ = xr.shape
    xp = jnp.transpose(jnp.concatenate([conv0, xr], axis=1), (1, 0, 2))
    grt = jnp.transpose(gr, (1, 0, 2))
    vec = lambda v: v.reshape(1, D_RNN)
    full = lambda *s: pl.BlockSpec(s, lambda i: (0,) * len(s))
    out, hl = pl.pallas_call(
        _lru_sample_kernel,
        grid=(1,),
        in_specs=[full(t_dec + CONV_WIDTH - 1, n_dec, D_RNN), full(t_dec, n_dec, D_RNN), full(n_dec, D_RNN),
                  full(CONV_WIDTH, D_RNN), full(1, D_RNN), full(D_RNN, D_RNN), full(1, D_RNN),
                  full(D_RNN, D_RNN), full(1, D_RNN), full(1, D_RNN)],
        out_specs=[full(t_dec, n_dec, D_RNN), full(n_dec, D_RNN)],
        out_shape=[jax.ShapeDtypeStruct((t_dec, n_dec, D_RNN), F32),
                   jax.ShapeDtypeStruct((n_dec, D_RNN), F32)],
        compiler_params=_cparams(("arbitrary",)),
        name="lru_sample",
    )(xp, grt, h0, conv_w, vec(conv_b), wa_bd, vec(b_a), wx_bd, vec(b_x), vec(lam))
    return jnp.transpose(out, (1, 0, 2)), hl


def _block_diag(w):
    n, k, _ = w.shape
    eye = jnp.asarray(np.eye(n, dtype=np.float32))
    return (w[:, :, None, :] * eye[:, None, :, None]).reshape(n * k, n * k)


GDN_ROWS = 256
GDN_INTRA_ROWS = 2048


def _bmm(a, b):
    return jnp.einsum('nij,njk->nik', a.astype(BF16), b.astype(BF16), preferred_element_type=F32)


def _bmm_nt(a, b):
    return jnp.einsum('nid,njd->nij', a.astype(BF16), b.astype(BF16), preferred_element_type=F32)


def _col(x, lane, idx):
    return jnp.sum(jnp.where(lane == idx, x, 0.0), axis=1, keepdims=True)


def _gdn_gates_kernel(x_ref, a_ref, dt_ref, beta_ref, gc_ref, *, chunk, pad_period, pad_rows):
    x = x_ref[...]
    tm = x.shape[0]
    beta = jax.nn.sigmoid(x)
    g = -jnp.exp(a_ref[...]) * _softplus(x + dt_ref[...])
    if pad_period:
        row = lax.broadcasted_iota(jnp.int32, x.shape, 0)
        valid = (row & (pad_period - 1)) >= pad_rows
        beta = jnp.where(valid, beta, 0.0)
        g = jnp.where(valid, g, 0.0)
    r = lax.broadcasted_iota(jnp.int32, (tm, tm), 0)
    c = lax.broadcasted_iota(jnp.int32, (tm, tm), 1)
    sh = chunk.bit_length() - 1
    tri = jnp.where(((r >> sh) == (c >> sh)) & (c <= r), 1.0, 0.0)
    beta_ref[...] = beta
    gc_ref[...] = jnp.dot(tri, g, preferred_element_type=F32, precision=lax.Precision.HIGHEST)


def _gdn_gates(x, col, rows, a_log, dt_bias, chunk, pad_period=0, pad_rows=0):
    tm = GDN_ROWS
    assert rows % tm == 0 and tm % chunk == 0
    a_pad = jnp.pad(a_log.reshape(1, -1), ((0, 0), (GDN_HEADS, LANES - 2 * GDN_HEADS)))
    dt_pad = jnp.pad(dt_bias.reshape(1, -1), ((0, 0), (GDN_HEADS, LANES - 2 * GDN_HEADS)))
    return pl.pallas_call(
        functools.partial(_gdn_gates_kernel, chunk=chunk, pad_period=pad_period, pad_rows=pad_rows),
        grid=(rows // tm,),
        in_specs=[pl.BlockSpec((tm, LANES), lambda i: (i, col)),
                  pl.BlockSpec((1, LANES), lambda i: (0, 0)),
                  pl.BlockSpec((1, LANES), lambda i: (0, 0))],
        out_specs=[pl.BlockSpec((tm, LANES), lambda i: (i, 0))] * 2,
        out_shape=[jax.ShapeDtypeStruct((rows, LANES), F32)] * 2,
        compiler_params=_cparams(("parallel",)),
        name="gdn_gates",
    )(x, a_pad, dt_pad)


def _gdn_prep_kernel(x_ref, cw_ref, beta_ref, gc_ref, *out_refs, mode):
    hd = HEAD_DIM
    n = x_ref.shape[0] // GDN_ROWS
    h0 = 2 * pl.program_id(1)
    lane = lax.broadcasted_iota(jnp.int32, (GDN_ROWS, LANES), 1)
    lo = lane < hd

    def halves(c0, c1):
        return jnp.where(lo, c0, c1)

    def put(ref, r0, val):
        ref[0, pl.ds(r0, GDN_ROWS), :] = val[:, :hd]
        ref[1, pl.ds(r0, GDN_ROWS), :] = val[:, hd:]

    def body(c, prev8):
        r0 = pl.multiple_of(c * GDN_ROWS, GDN_ROWS)
        cur = x_ref[pl.ds(r0, GDN_ROWS), :]
        y = _causal_conv_rows(prev8, cur, cw_ref[...])
        y = y * jax.nn.sigmoid(y)
        if mode in ("q", "k"):
            ss = y * y
            s0 = jnp.sum(jnp.where(lo, ss, 0.0), axis=1, keepdims=True)
            s1 = jnp.sum(jnp.where(lo, 0.0, ss), axis=1, keepdims=True)
            y = y * halves(lax.rsqrt(s0 + NORM_EPS), lax.rsqrt(s1 + NORM_EPS))
        if mode == "q":
            put(out_refs[0], r0, y * (hd ** -0.5))
        else:
            beta = beta_ref[pl.ds(r0, GDN_ROWS), :]
            bb = halves(_col(beta, lane, h0), _col(beta, lane, h0 + 1))
            if mode == "k":
                gc = gc_ref[pl.ds(r0, GDN_ROWS), :]
                put(out_refs[0], r0, y)
                put(out_refs[1], r0, y * bb)
                put(out_refs[2], r0, halves(_col(gc, lane, GDN_HEADS + h0), _col(gc, lane, GDN_HEADS + h0 + 1)))
            else:
                put(out_refs[0], r0, y * bb)
        return cur[GDN_ROWS - 8:]

    lax.fori_loop(0, n, body, jnp.zeros((8, LANES), F32))


def _gdn_prep(x, x_col, conv_w, beta, gc, n_batch, seq, mode):
    sec = {"q": 0, "k": 4, "v": 8}[mode]
    n_out = {"q": 1, "k": 3, "v": 1}[mode]
    hspec = pl.BlockSpec((2, seq, HEAD_DIM), lambda b, c: (b * 4 + c, 0, 0))
    gspec = pl.BlockSpec((seq, LANES), lambda b, c: (b, 0))
    outs = pl.pallas_call(
        functools.partial(_gdn_prep_kernel, mode=mode),
        grid=(n_batch, 4),
        in_specs=[pl.BlockSpec((seq, LANES), lambda b, c: (b, x_col + c)),
                  pl.BlockSpec((CONV_WIDTH, LANES), lambda b, c: (0, sec + c)),
                  gspec, gspec],
        out_specs=[hspec] * n_out,
        out_shape=[jax.ShapeDtypeStruct((n_batch * GDN_HEADS, seq, HEAD_DIM), F32)] * n_out,
        compiler_params=_cparams(("parallel", "parallel")),
        name="gdn_prep_" + mode,
    )(x, conv_w, beta, gc)
    return outs


def _gdn_intra_kernel(q_ref, k_ref, kb_ref, vb_ref, gcb_ref, gcr_ref,
                      u_ref, w_ref, qg_ref, kd_ref, qk_ref, gl_ref, *, chunk):
    hd = HEAD_DIM
    rows = q_ref.shape[0]
    n = rows // chunk
    r3 = lambda ref: ref[...].reshape(n, chunk, hd)
    q, k, kb, vb, gcb = r3(q_ref), r3(k_ref), r3(kb_ref), r3(vb_ref), r3(gcb_ref)
    gcr = gcr_ref[...]
    ci = lax.broadcasted_iota(jnp.int32, (n, chunk, chunk), 1)
    si = lax.broadcasted_iota(jnp.int32, (n, chunk, chunk), 2)
    decay = jnp.exp(jnp.where(ci >= si, gcb[:, :, :chunk] - gcr[:, None, :], NEG_INF))
    a = jnp.where(ci > si, _bmm_nt(kb, k) * decay, 0.0)
    b = -a
    y = b
    p = _bmm(b, b)
    stages = chunk.bit_length() - 2
    for t in range(stages):
        y = y + p + _bmm(y, p)
        if t < stages - 1:
            p = _bmm(p, p)
    eg = jnp.exp(gcb)
    rhs = jnp.concatenate([vb, kb * eg], axis=-1)
    sol = rhs + _bmm(y, rhs)
    gl = gcb[:, chunk - 1:chunk, :]
    u_ref[...] = sol[:, :, :hd].reshape(rows, hd)
    w_ref[...] = sol[:, :, hd:].reshape(rows, hd)
    qg_ref[...] = (q * eg).reshape(rows, hd)
    kd_ref[...] = (k * jnp.exp(gl - gcb)).reshape(rows, hd)
    qk_ref[...] = (_bmm_nt(q, k) * decay).reshape(rows, chunk)
    gl_ref[...] = jnp.exp(gl).reshape(n, hd)


def _gdn_intra(q, k, kb, vb, gcb, gc_row, chunk, rows_per_step):
    bh, seq, hd = q.shape
    tr = rows_per_step
    n = tr // chunk
    assert seq % tr == 0
    hspec = pl.BlockSpec((None, tr, hd), lambda h, i: (h, i, 0))
    shape = jax.ShapeDtypeStruct((bh, seq, hd), F32)
    return pl.pallas_call(
        functools.partial(_gdn_intra_kernel, chunk=chunk),
        grid=(bh, seq // tr),
        in_specs=[hspec] * 5 + [pl.BlockSpec((None, n, chunk), lambda h, i: (h, i, 0))],
        out_specs=[hspec] * 4 + [pl.BlockSpec((None, tr, chunk), lambda h, i: (h, i, 0)),
                                 pl.BlockSpec((None, n, hd), lambda h, i: (h, i, 0))],
        out_shape=[shape] * 4 + [jax.ShapeDtypeStruct((bh, seq, chunk), F32),
                                 jax.ShapeDtypeStruct((bh, seq // chunk, hd), F32)],
        compiler_params=_cparams(("parallel", "parallel")),
        name="gdn_intra",
    )(q, k, kb, vb, gcb, gc_row)


def _gdn_chunk_update(s, u, w, qg, kd, qk, gl):
    v_new = u - _bmm(w, s)
    o = _bmm(qg, s) + _bmm(qk, v_new)
    s = s * gl + jnp.einsum('ncd,nce->nde', kd.astype(BF16), v_new.astype(BF16), preferred_element_type=F32)
    return s, o


def _gdn_scan_kernel(u_ref, w_ref, qg_ref, kd_ref, qk_ref, gl_ref, o_ref, sfin_ref, s_scr, *, chunk):
    j = pl.program_id(1)

    @pl.when(j == 0)
    def _zero():
        s_scr[...] = jnp.zeros_like(s_scr)

    n = u_ref.shape[1] // chunk
    s = s_scr[...]
    for c in range(n):
        sl = slice(c * chunk, (c + 1) * chunk)
        s, o = _gdn_chunk_update(s, u_ref[:, sl, :], w_ref[:, sl, :], qg_ref[:, sl, :], kd_ref[:, sl, :],
                                 qk_ref[:, sl, :], gl_ref[:, c:c + 1, :])
        o_ref[:, sl, :] = o
    s_scr[...] = s

    @pl.when(j == pl.num_programs(1) - 1)
    def _final():
        sfin_ref[...] = s


def _gdn_scan(u, w, qg, kd, qk, gl, chunk, heads_per_step, rows_per_step):
    bh, seq, hd = u.shape
    hb, tr = heads_per_step, rows_per_step
    n = tr // chunk
    hspec = pl.BlockSpec((hb, tr, hd), lambda h, i: (h, i, 0))
    return pl.pallas_call(
        functools.partial(_gdn_scan_kernel, chunk=chunk),
        grid=(bh // hb, seq // tr),
        in_specs=[hspec] * 4 + [pl.BlockSpec((hb, tr, chunk), lambda h, i: (h, i, 0)),
                                pl.BlockSpec((hb, n, hd), lambda h, i: (h, i, 0))],
        out_specs=[hspec, pl.BlockSpec((hb, hd, hd), lambda h, i: (h, 0, 0))],
        out_shape=[jax.ShapeDtypeStruct((bh, seq, hd), F32), jax.ShapeDtypeStruct((bh, hd, hd), F32)],
        scratch_shapes=[pltpu.VMEM((hb, hd, hd), F32)],
        compiler_params=_cparams(("parallel", "arbitrary")),
        name="gdn_scan",
    )(u, w, qg, kd, qk, gl)


GDN_DEC_SEQS_PER_STEP = 16


def _gdn_step_kernel(s0_ref, u_ref, w_ref, qg_ref, kd_ref, qk_ref, gl_ref, o_ref, s_ref, *, chunk):
    hd = HEAD_DIM
    n = s0_ref.shape[0]
    r3 = lambda ref, last: ref[...].reshape(n, chunk, last)
    s, o = _gdn_chunk_update(s0_ref[...], r3(u_ref, hd), r3(w_ref, hd), r3(qg_ref, hd), r3(kd_ref, hd),
                             r3(qk_ref, chunk), gl_ref[...][:, None, :])
    o_ref[...] = o.reshape(n * chunk, hd)
    s_ref[...] = s


def _gdn_step(s0, u, w, qg, kd, qk, gl, chunk):
    n_dec, nh, hd, _ = s0.shape
    sb = GDN_DEC_SEQS_PER_STEP
    assert n_dec % sb == 0
    hspec = pl.BlockSpec((None, sb * chunk, hd), lambda h, i: (h, i, 0))
    sspec = pl.BlockSpec((sb, None, hd, hd), lambda h, i: (i, h, 0, 0))
    return pl.pallas_call(
        functools.partial(_gdn_step_kernel, chunk=chunk),
        grid=(nh, n_dec // sb),
        in_specs=[sspec] + [hspec] * 4 + [pl.BlockSpec((None, sb * chunk, chunk), lambda h, i: (h, i, 0)),
                                         pl.BlockSpec((None, sb, hd), lambda h, i: (h, i, 0))],
        out_specs=[hspec, sspec],
        out_shape=[jax.ShapeDtypeStruct((nh, n_dec * chunk, hd), F32),
                   jax.ShapeDtypeStruct((n_dec, nh, hd, hd), F32)],
        compiler_params=_cparams(("parallel", "parallel")),
        name="gdn_step",
    )(s0, u, w, qg, kd, qk, gl)


def _gdn_post_kernel(o_ref, z_ref, w_ref, out_ref):
    hd = HEAD_DIM
    for h in range(GDN_HEADS):
        o = o_ref[h]
        r = o * lax.rsqrt(jnp.mean(o * o, axis=-1, keepdims=True) + NORM_EPS) * w_ref[...]
        z = z_ref[:, h * hd:(h + 1) * hd]
        out_ref[:, h * hd:(h + 1) * hd] = r * (z * jax.nn.sigmoid(z))


def _gdn_post(o, z, z_col, norm_w, n_batch, seq, tm):
    nt = seq // tm
    return pl.pallas_call(
        _gdn_post_kernel,
        grid=(n_batch, nt),
        in_specs=[pl.BlockSpec((GDN_HEADS, tm, HEAD_DIM), lambda b, i: (b, i, 0)),
                  pl.BlockSpec((tm, 4 * LANES), lambda b, i: (b * nt + i, z_col)),
                  pl.BlockSpec((1, HEAD_DIM), lambda b, i: (0, 0))],
        out_specs=pl.BlockSpec((tm, 4 * LANES), lambda b, i: (b * nt + i, 0)),
        out_shape=jax.ShapeDtypeStruct((n_batch * seq, 4 * LANES), F32),
        compiler_params=_cparams(("parallel", "parallel")),
        name="gdn_post",
    )(o, z, norm_w.reshape(1, HEAD_DIM))


def _gdn_front(x, qkv_col, gate_col, conv_w, a_log, dt_bias, n_batch, seq, chunk, pad_period=0, pad_rows=0):
    rows = n_batch * seq
    beta, gc = _gdn_gates(x, gate_col, rows, a_log, dt_bias, chunk, pad_period, pad_rows)
    (q,) = _gdn_prep(x, qkv_col, conv_w, beta, gc, n_batch, seq, "q")
    k, kb, gcb = _gdn_prep(x, qkv_col + 4, conv_w, beta, gc, n_batch, seq, "k")
    (vb,) = _gdn_prep(x, qkv_col + 8, conv_w, beta, gc, n_batch, seq, "v")
    gc_row = jnp.transpose(gc[:, GDN_HEADS:2 * GDN_HEADS].reshape(n_batch, seq, GDN_HEADS), (0, 2, 1))
    gc_row = gc_row.reshape(n_batch * GDN_HEADS, seq // chunk, chunk)
    return _gdn_intra(q, k, kb, vb, gcb, gc_row, chunk, min(seq, GDN_INTRA_ROWS))


TM = 768
TM_SEG = 512


def _pad_cols(w, n):
    return jnp.pad(w, ((0, 0), (0, n - w.shape[1])))


def _seq_tails(proj, n_batch, seq, n_rows, col, width):
    return jnp.stack([proj[(b + 1) * seq - n_rows:(b + 1) * seq, col:col + width] for b in range(n_batch)])


GDN_DEC_PERIOD = 8
EVEN_COLS = 3200


def kernel(x_prompt, x_sample, cache_k_moba, cache_v_moba, state_gdn, state_gdn_conv, cache_k_swa, cache_v_swa, state_lru, state_lru_conv, page_table, w_in_even, w_out_even, gdn_conv_w, gdn_a_log, gdn_dt_bias, gdn_norm_w, w_in_odd, w_out_odd, swa_sinks, lru_conv_w, lru_conv_b, lru_w_a, lru_b_a, lru_w_x, lru_b_x, lru_lambda, rel_bias, ln_g, ln_b, router_w, router_b, w_gate, w_up, w_down):
    n_pr, seq, d = x_prompt.shape
    n_dec, dec_seq, _ = x_sample.shape
    tp = n_pr * seq
    ts = n_dec * dec_seq
    n_pool = cache_k_moba.shape[0]
    kva = KV_HEADS_A * HEAD_DIM
    kvc = KV_HEADS_C * HEAD_DIM
    x_segs = [x_prompt.reshape(tp, d), x_sample.reshape(ts, d)]

    rw_pad = _pad_cols(router_w, LANES).T
    rb_pad = router_b.reshape(N_EXPERTS, 1)
    experts = [_group_experts(w_gate[layer], w_up[layer], w_down[layer]) for layer in range(DEPTH)]

    proj = _mm(x_segs, _pad_cols(w_in_even[0], EVEN_COLS).astype(BF16), TM_SEG, 640)
    ps = proj[tp:]
    moba_k_prompt = proj[:tp, 512:512 + kva].reshape(n_pr, seq, 1, KV_HEADS_A, HEAD_DIM)
    moba_v_prompt = proj[:tp, 768:768 + kva].reshape(n_pr, seq, 1, KV_HEADS_A, HEAD_DIM)
    k_new = ps[:, 512:512 + kva].reshape(n_dec, dec_seq, kva)
    v_new = ps[:, 768:768 + kva].reshape(n_dec, dec_seq, kva)
    moba_k_sample = k_new.reshape(n_dec, dec_seq, 1, KV_HEADS_A, HEAD_DIM)
    moba_v_sample = v_new.reshape(n_dec, dec_seq, 1, KV_HEADS_A, HEAD_DIM)

    oa_p = _moba_prompt(proj, rel_bias, n_pr, seq, 0, 4, 6)
    oa_s = _moba_sample(ps[:, :512].reshape(n_dec, dec_seq, N_ATT_HEADS, HEAD_DIM), k_new, v_new,
                        jnp.swapaxes(cache_k_moba.reshape(n_pool, PAGE_SIZE, -1), 1, 2),
                        jnp.swapaxes(cache_v_moba.reshape(n_pool, PAGE_SIZE, -1), 1, 2),
                        page_table, rel_bias, layer=0)

    parts = _gdn_front(proj, 8, 24, gdn_conv_w[0], gdn_a_log[0], gdn_dt_bias[0], n_pr, seq, GDN_CHUNK)
    o_p, s_fin = _gdn_scan(*parts, GDN_CHUNK, GDN_HEADS, 512)
    ob_p = _gdn_post(o_p, proj, 5, gdn_norm_w[0], n_pr, seq, 512)
    gdn_state_prompt = s_fin.reshape(1, n_pr, GDN_HEADS, HEAD_DIM, HEAD_DIM)
    gdn_conv_prompt = _seq_tails(proj, n_pr, seq, 3, 1024, GDN_CONV_CH)[None]

    period = GDN_DEC_PERIOD
    lead = period - dec_seq
    qkv_s = ps[:, 1024:1024 + GDN_CONV_CH].reshape(n_dec, dec_seq, GDN_CONV_CH)
    x8 = jnp.concatenate([jnp.zeros((n_dec, lead - 3, GDN_CONV_CH), F32), state_gdn_conv[0], qkv_s], axis=1)
    tail8 = jnp.pad(ps[:, 2560:EVEN_COLS].reshape(n_dec, dec_seq, EVEN_COLS - 2560), ((0, 0), (lead, 0), (0, 0)))
    xs = jnp.concatenate([x8, tail8], axis=-1).reshape(n_dec * period, EVEN_COLS - 1024)
    parts = _gdn_front(xs, 0, 16, gdn_conv_w[0], gdn_a_log[0], gdn_dt_bias[0], 1, n_dec * period, period,
                       period, lead)
    o_s, gdn_state_sample = _gdn_step(state_gdn[0], *parts, period)
    ob_s = _gdn_post(o_s, xs, 3, gdn_norm_w[0], 1, n_dec * period, 512)
    ob_s = ob_s.reshape(n_dec, period, 512)[:, lead:].reshape(ts, 512)
    gdn_conv_sample = qkv_s[None, :, dec_seq - 3:]

    h = _mm_ln([oa_p, oa_s.reshape(ts, 512)], [ob_p, ob_s], w_out_even[0].astype(BF16), x_segs,
               ln_g[0, 0], ln_b[0, 0], TM_SEG)
    h = _moe_ln(h, rw_pad, rb_pad, *experts[0], ln_g[0, 1], ln_b[0, 1], TM)

    proj = _mm([h], w_in_odd[0].astype(BF16), TM, 896)
    ps = proj[tp:]
    wa_bd, wx_bd = _block_diag(lru_w_a[0]), _block_diag(lru_w_x[0])
    oc_p = _swa_prompt(proj, rel_bias, swa_sinks[0], n_pr, seq, 0, 4, 5)
    od_p, lru_p = _lru_prompt(proj, n_pr, seq, 6, 10, lru_conv_w[0], lru_conv_b[0], wa_bd, lru_b_a[0],
                              wx_bd, lru_b_x[0], lru_lambda[0])
    swa_k_prompt = _seq_tails(proj, n_pr, seq, WINDOW, 512, kvc).reshape(1, n_pr, WINDOW, KV_HEADS_C, HEAD_DIM)
    swa_v_prompt = _seq_tails(proj, n_pr, seq, WINDOW, 640, kvc).reshape(1, n_pr, WINDOW, KV_HEADS_C, HEAD_DIM)
    lru_conv_prompt = _seq_tails(proj, n_pr, seq, 3, 768, D_RNN)[None]

    nbuf = cache_k_swa.shape[2]
    kc_s = ps[:, 512:512 + kvc].reshape(n_dec, dec_seq, kvc)
    vc_s = ps[:, 640:640 + kvc].reshape(n_dec, dec_seq, kvc)
    buf_k = cache_k_swa[0].reshape(n_dec, nbuf, kvc)
    buf_v = cache_v_swa[0].reshape(n_dec, nbuf, kvc)
    oc_s = _swa_sample(ps[:, :512].reshape(n_dec, dec_seq, N_ATT_HEADS, HEAD_DIM), kc_s, vc_s, buf_k, buf_v,
                       rel_bias, swa_sinks[0])
    swa_k_sample = jnp.concatenate([buf_k, kc_s], axis=1)[:, dec_seq:].reshape(1, n_dec, nbuf, KV_HEADS_C, HEAD_DIM)
    swa_v_sample = jnp.concatenate([buf_v, vc_s], axis=1)[:, dec_seq:].reshape(1, n_dec, nbuf, KV_HEADS_C, HEAD_DIM)
    xr_s = ps[:, 768:768 + D_RNN].reshape(n_dec, dec_seq, D_RNN)
    gr_s = ps[:, 1280:1280 + D_RNN].reshape(n_dec, dec_seq, D_RNN)
    od_s, lru_s = _lru_sample(xr_s, gr_s, state_lru[0], state_lru_conv[0], lru_conv_w[0], lru_conv_b[0],
                              wa_bd, lru_b_a[0], wx_bd, lru_b_x[0], lru_lambda[0])
    lru_conv_sample = jnp.concatenate([state_lru_conv[0], xr_s], axis=1)[None, :, dec_seq:]

    h = _mm_ln([oc_p, oc_s.reshape(ts, 512)], [od_p, od_s.reshape(ts, 512)], w_out_odd[0].astype(BF16), [h],
               ln_g[1, 0], ln_b[1, 0], TM_SEG)
    h = _moe_ln(h, rw_pad, rb_pad, *experts[1], ln_g[1, 1], ln_b[1, 1], TM)

    y_prompt = h[:tp].reshape(n_pr, seq, d)
    y_sample = h[tp:].reshape(n_dec, dec_seq, d)
    return (y_prompt, y_sample, moba_k_prompt, moba_v_prompt, moba_k_sample, moba_v_sample,
            gdn_state_prompt, gdn_state_sample[None], gdn_conv_prompt, gdn_conv_sample,
            swa_k_prompt, swa_v_prompt, swa_k_sample, swa_v_sample,
            lru_p[None], lru_s[None], lru_conv_prompt, lru_conv_sample)
```

```python
import functools
import math

import jax
import jax.numpy as jnp
import numpy as np
from jax import lax
from jax.experimental import pallas as pl
from jax.experimental.pallas import tpu as pltpu

F32 = jnp.float32
BF16 = jnp.bfloat16

HEAD_DIM = 64
N_ATT_HEADS = 8
KV_HEADS_A = 4
KV_HEADS_C = 2
MOBA_BLOCK = 256
MOBA_TOPK = 3
PAGE_SIZE = 128
GDN_HEADS = 8
GDN_CHUNK = 64
GDN_CONV_CH = 1536
CONV_WIDTH = 4
WINDOW = 128
D_RNN = 512
LRU_C = 8.0
T5_BUCKETS = 32
T5_MAX_DIST = 128
N_EXPERTS = 16
N_GROUPS = 4
EXPERTS_PER_GROUP = 4
D_EXPERT = 256
DEPTH = 2
DEEPNORM_ALPHA = (2 * DEPTH) ** 0.25
LN_EPS = 1e-5
NORM_EPS = 1e-6

LANES = 128
VMEM_LIMIT = 56 * 1024 * 1024
NEG_INF = float("-inf")


def _cparams(sem):
    return pltpu.CompilerParams(dimension_semantics=sem, vmem_limit_bytes=VMEM_LIMIT)


def _dot(a, b):
    return jnp.dot(a.astype(BF16), b.astype(BF16), preferred_element_type=F32)


def _dot_nt(a, b):
    return lax.dot_general(a.astype(BF16), b.astype(BF16), (((1,), (1,)), ((), ())),
                           preferred_element_type=F32)


def _seg_specs(segs, tm):
    specs, bounds, lo = [], [], 0
    for a in segs:
        assert a.shape[0] % tm == 0
        nt = a.shape[0] // tm
        specs.append(pl.BlockSpec((tm, a.shape[1]), lambda i, lo=lo, nt=nt: (jnp.clip(i - lo, 0, nt - 1), 0)))
        lo += nt
        bounds.append(lo)
    return specs, bounds


def _seg_tile(i, refs, bounds):
    x = refs[-1][...]
    for r, hi in zip(reversed(refs[:-1]), reversed(bounds[:-1])):
        x = jnp.where(i < hi, r[...], x)
    return x


def _mm_kernel(*refs, tn, bounds):
    ns = len(bounds)
    w_ref, o_ref = refs[ns], refs[ns + 1]
    xb = _seg_tile(pl.program_id(0), refs[:ns], bounds).astype(BF16)
    for c in range(w_ref.shape[1] // tn):
        o_ref[:, c * tn:(c + 1) * tn] = jnp.dot(xb, w_ref[:, c * tn:(c + 1) * tn], preferred_element_type=F32)


def _mm(x_segs, w, tm, tn):
    k, n = w.shape
    assert n % tn == 0
    specs, bounds = _seg_specs(x_segs, tm)
    m = bounds[-1] * tm
    return pl.pallas_call(
        functools.partial(_mm_kernel, tn=tn, bounds=tuple(bounds)),
        grid=(bounds[-1],),
        in_specs=specs + [pl.BlockSpec((k, n), lambda i: (0, 0))],
        out_specs=pl.BlockSpec((tm, n), lambda i: (i, 0)),
        out_shape=jax.ShapeDtypeStruct((m, n), F32),
        compiler_params=_cparams(("parallel",)),
        name="in_proj",
    )(*x_segs, w)


def _layer_norm_rows(z, g, b):
    mu = jnp.mean(z, axis=-1, keepdims=True)
    zc = z - mu
    var = jnp.mean(zc * zc, axis=-1, keepdims=True)
    return zc * lax.rsqrt(var + LN_EPS) * g + b


def _mm_ln_kernel(*refs, bounds_a, bounds_b, bounds_h):
    na, nb, nh = len(bounds_a), len(bounds_b), len(bounds_h)
    xa_refs, xb_refs, h_refs = refs[:na], refs[na:na + nb], refs[na + nb:na + nb + nh]
    wa_ref, wb_ref, g_ref, b_ref, o_ref = refs[na + nb + nh:]
    i = pl.program_id(0)
    f = (jnp.dot(_seg_tile(i, xa_refs, bounds_a).astype(BF16), wa_ref[...], preferred_element_type=F32)
         + jnp.dot(_seg_tile(i, xb_refs, bounds_b).astype(BF16), wb_ref[...], preferred_element_type=F32))
    o_ref[...] = _layer_norm_rows(DEEPNORM_ALPHA * _seg_tile(i, h_refs, bounds_h) + f, g_ref[...], b_ref[...])


def _mm_ln(xa_segs, xb_segs, w, h_segs, g, b, tm):
    ka, kb = xa_segs[0].shape[1], xb_segs[0].shape[1]
    n = w.shape[1]
    assert ka == kb
    sa, ba = _seg_specs(xa_segs, tm)
    sb, bb = _seg_specs(xb_segs, tm)
    sh, bh = _seg_specs(h_segs, tm)
    assert ba[-1] == bb[-1] == bh[-1]
    return pl.pallas_call(
        functools.partial(_mm_ln_kernel, bounds_a=tuple(ba), bounds_b=tuple(bb), bounds_h=tuple(bh)),
        grid=(ba[-1],),
        in_specs=sa + sb + sh + [pl.BlockSpec((ka, n), lambda i: (0, 0)),
                                 pl.BlockSpec((kb, n), lambda i: (1, 0)),
                                 pl.BlockSpec((1, n), lambda i: (0, 0)),
                                 pl.BlockSpec((1, n), lambda i: (0, 0))],
        out_specs=pl.BlockSpec((tm, n), lambda i: (i, 0)),
        out_shape=jax.ShapeDtypeStruct((ba[-1] * tm, n), F32),
        compiler_params=_cparams(("parallel",)),
        name="out_proj_ln",
    )(*xa_segs, *xb_segs, *h_segs, w, w, g.reshape(1, n), b.reshape(1, n))


def _moe_ln_kernel(t_ref, rw_ref, rb_ref, wg_ref, wu_ref, wd_ref, g_ref, b_ref, o_ref,
                   gates_ref, tb_ref, hb_ref, acc_ref):
    e = pl.program_id(1)

    @pl.when(e == 0)
    def _route():
        t = t_ref[...]
        logits = _dot_nt(rw_ref[...], t)[:N_EXPERTS]
        s = jax.nn.sigmoid(logits)
        sb = s + rb_ref[...]
        row = lax.broadcasted_iota(jnp.int32, sb.shape, 0)
        grp = row // EXPERTS_PER_GROUP

        def max_first(vals):
            mx = jnp.max(vals, axis=0, keepdims=True)
            return mx, jnp.min(jnp.where(vals == mx, row, N_EXPERTS), axis=0, keepdims=True)

        best = None
        gsel = None
        for gi in range(N_GROUPS):
            vals = jnp.where(grp == gi, sb, NEG_INF)
            m1, i1 = max_first(vals)
            m2 = jnp.max(jnp.where(row == i1, NEG_INF, vals), axis=0, keepdims=True)
            score = m1 + m2
            if gi == 0:
                best, gsel = score, jnp.zeros_like(i1)
            else:
                upd = score > best
                best = jnp.where(upd, score, best)
                gsel = jnp.where(upd, gi, gsel)
        vals = jnp.where(grp == gsel, sb, NEG_INF)
        _, i1 = max_first(vals)
        _, i2 = max_first(jnp.where(row == i1, NEG_INF, vals))
        s1 = jnp.sum(jnp.where(row == i1, s, 0.0), axis=0, keepdims=True)
        s2 = jnp.sum(jnp.where(row == i2, s, 0.0), axis=0, keepdims=True)
        den = s1 + s2
        gates_t = jnp.where(row == i1, s1 / den, 0.0) + jnp.where(row == i2, s2 / den, 0.0)
        gates_t = jnp.concatenate([gates_t, jnp.zeros((LANES - N_EXPERTS, gates_t.shape[1]), F32)], axis=0)
        gates_ref[...] = gates_t.T
        tb_ref[...] = t.astype(BF16)
        acc_ref[...] = jnp.zeros_like(acc_ref)

    tb = tb_ref[...]
    gates = gates_ref[...]
    lane = lax.broadcasted_iota(jnp.int32, gates.shape, 1)
    for j in range(EXPERTS_PER_GROUP):
        sl = slice(j * D_EXPERT, (j + 1) * D_EXPERT)
        ge = jnp.sum(jnp.where(lane == e * EXPERTS_PER_GROUP + j, gates, 0.0), axis=1, keepdims=True)
        g = jnp.dot(tb, wg_ref[j], preferred_element_type=F32)
        u = jnp.dot(tb, wu_ref[j], preferred_element_type=F32)
        hb_ref[:, sl] = ((g * jax.nn.sigmoid(g)) * u * ge).astype(BF16)
    acc_ref[...] += jnp.dot(hb_ref[...], wd_ref[...], preferred_element_type=F32)

    @pl.when(e == pl.num_programs(1) - 1)
    def _finish():
        o_ref[...] = _layer_norm_rows(DEEPNORM_ALPHA * t_ref[...] + acc_ref[...], g_ref[...], b_ref[...])


def _group_experts(w_gate, w_up, w_down):
    ne, d, f = w_gate.shape
    es = EXPERTS_PER_GROUP
    return w_gate.astype(BF16), w_up.astype(BF16), w_down.astype(BF16).reshape(ne // es, es * f, d)


def _moe_ln(t, router_w_pad, router_b_pad, wg, wu, wd, g, b, tm):
    m, d = t.shape
    ng, gf, _ = wd.shape
    es, f = EXPERTS_PER_GROUP, wg.shape[2]
    assert m % tm == 0
    return pl.pallas_call(
        _moe_ln_kernel,
        grid=(m // tm, ng),
        in_specs=[pl.BlockSpec((tm, d), lambda i, e: (i, 0)),
                  pl.BlockSpec((LANES, d), lambda i, e: (0, 0)),
                  pl.BlockSpec((N_EXPERTS, 1), lambda i, e: (0, 0)),
                  pl.BlockSpec((es, d, f), lambda i, e: (e, 0, 0)),
                  pl.BlockSpec((es, d, f), lambda i, e: (e, 0, 0)),
                  pl.BlockSpec((None, gf, d), lambda i, e: (e, 0, 0)),
                  pl.BlockSpec((1, d), lambda i, e: (0, 0)),
                  pl.BlockSpec((1, d), lambda i, e: (0, 0))],
        out_specs=pl.BlockSpec((tm, d), lambda i, e: (i, 0)),
        out_shape=jax.ShapeDtypeStruct((m, d), F32),
        scratch_shapes=[pltpu.VMEM((tm, LANES), F32),
                        pltpu.VMEM((tm, d), BF16),
                        pltpu.VMEM((tm, gf), BF16),
                        pltpu.VMEM((tm, d), F32)],
        compiler_params=_cparams(("parallel", "arbitrary")),
        name="moe_ln",
    )(t, router_w_pad, router_b_pad, wg, wu, wd, g.reshape(1, d), b.reshape(1, d))


def _t5_bucket_np(rel):
    n = np.maximum(rel, 0)
    max_exact = T5_BUCKETS // 2
    nf = np.maximum(n, 1).astype(np.float32)
    large = max_exact + (np.log(nf / np.float32(max_exact)) / np.float32(math.log(T5_MAX_DIST / max_exact))
                         * np.float32(T5_BUCKETS - max_exact)).astype(np.int32)
    large = np.minimum(large, T5_BUCKETS - 1)
    return np.where(n < max_exact, n, large).astype(np.int32)


def _t5_thresholds():
    b = _t5_bucket_np(np.arange(4 * T5_MAX_DIST))
    return [int(np.argmax(b >= k)) for k in range(1, T5_BUCKETS)]


def _bias_tile_kernel(rb_ref, o_ref, *, off, sign, lo, hi):
    h = pl.program_id(0)
    row = lax.broadcasted_iota(jnp.int32, o_ref.shape, 0)
    col = lax.broadcasted_iota(jnp.int32, o_ref.shape, 1)
    rel = off + sign * (row - col)
    t = jnp.full(o_ref.shape, rb_ref[0, h], F32)
    for k, thr in enumerate(_t5_thresholds(), start=1):
        t = jnp.where(rel >= thr, rb_ref[k, h], t)
    if lo is not None:
        t = jnp.where((rel >= lo) & (rel <= hi), t, NEG_INF)
    o_ref[...] = t


def _bias_tiles(rel_bias, rows, cols, off, sign=1, lo=None, hi=None):
    return pl.pallas_call(
        functools.partial(_bias_tile_kernel, off=off, sign=sign, lo=lo, hi=hi),
        grid=(N_ATT_HEADS,),
        in_specs=[pl.BlockSpec(memory_space=pltpu.SMEM)],
        out_specs=pl.BlockSpec((None, rows, cols), lambda h: (h, 0, 0)),
        out_shape=jax.ShapeDtypeStruct((N_ATT_HEADS, rows, cols), F32),
        compiler_params=_cparams(("parallel",)),
        name="bias_tiles",
    )(rel_bias)


def _moba_prompt_kernel(q_ref, k_ref, v_ref, bd_ref, bp_ref, bf_ref, o_ref, kmean_ref, vt_ref, sel_ref):
    i = pl.program_id(2)
    blk = MOBA_BLOCK
    nblk = k_ref.shape[0] // blk
    hd = HEAD_DIM

    @pl.when(i == 0)
    def _per_sequence():
        for n in range(nblk):
            kmean_ref[n:n + 1, :] = jnp.mean(k_ref[n * blk:(n + 1) * blk, :], axis=0, keepdims=True)
            vt_ref[n] = v_ref[n * blk:(n + 1) * blk, :].T.astype(BF16)

    q4 = q_ref[...] * (hd ** -0.5)
    q2b = []
    for kvl in range(2):
        lo = kvl * 2 * hd
        q2 = jnp.concatenate([q4[:, lo:lo + hd], q4[:, lo + hd:lo + 2 * hd]], axis=0)
        q2b.append(q2.astype(BF16))
        gate = _dot_nt(kmean_ref[:, kvl * hd:(kvl + 1) * hd], q2)
        row = lax.broadcasted_iota(jnp.int32, gate.shape, 0)
        gate = jnp.where(row < i, gate, NEG_INF)
        sel = jnp.zeros(gate.shape, F32)
        for _ in range(MOBA_TOPK):
            mx = jnp.max(gate, axis=0, keepdims=True)
            idx = jnp.min(jnp.where(gate == mx, row, nblk), axis=0, keepdims=True)
            hit = (row == idx) & (mx > NEG_INF)
            sel = jnp.where(hit, 1.0, sel)
            gate = jnp.where(row == idx, NEG_INF, gate)
        sel_ref[kvl] = sel

    def scores(j, kvl):
        r0 = pl.multiple_of(j * blk, blk)
        kb = k_ref[pl.ds(r0, blk), :][:, kvl * hd:(kvl + 1) * hd].astype(BF16)
        return _dot_nt(kb, q2b[kvl])

    def update(state, s, j, kvl):
        m_old, l_old, acc = state
        m_new = jnp.maximum(m_old, jnp.max(s, axis=0, keepdims=True))
        alpha = jnp.exp(m_old - m_new)
        p = jnp.exp(s - m_new)
        l_new = alpha * l_old + jnp.sum(p, axis=0, keepdims=True)
        vt = vt_ref[j][kvl * hd:(kvl + 1) * hd, :]
        acc = alpha * acc + jnp.dot(vt, p.astype(BF16), preferred_element_type=F32)
        return m_new, l_new, acc

    def live_row(j, kvl):
        return sel_ref[kvl, pl.ds(j, 1), :] > 0.0

    init = (jnp.full((1, 2 * blk), NEG_INF, F32), jnp.zeros((1, 2 * blk), F32), jnp.zeros((hd, 2 * blk), F32))
    jp = jnp.maximum(i - 1, 0)
    states = []
    for kvl in range(2):
        st = update(init, scores(i, kvl) + bd_ref[kvl], i, kvl)
        s = jnp.where(live_row(jp, kvl), scores(jp, kvl) + bp_ref[kvl], NEG_INF)
        states.append(update(st, s, jp, kvl))

    n_pairs = i // 2

    def far_pair(state, pr, kvl):
        j0 = 2 * pr
        r0 = pl.multiple_of(j0 * blk, 2 * blk)
        second_far = j0 + 1 < i - 1
        kb = k_ref[pl.ds(r0, 2 * blk), :][:, kvl * hd:(kvl + 1) * hd].astype(BF16)
        s = _dot_nt(kb, q2b[kvl])
        far_bias = bf_ref[kvl]
        s0 = s[:blk] + jnp.where(live_row(j0, kvl), far_bias, NEG_INF)
        s1 = s[blk:] + jnp.where(live_row(j0 + 1, kvl) & second_far, far_bias, NEG_INF)
        m_old, l_old, acc = state
        m_new = jnp.maximum(m_old, jnp.maximum(jnp.max(s0, axis=0, keepdims=True),
                                               jnp.max(s1, axis=0, keepdims=True)))
        alpha = jnp.exp(m_old - m_new)
        p = jnp.concatenate([jnp.exp(s0 - m_new), jnp.exp(s1 - m_new)], axis=0)
        l_new = alpha * l_old + jnp.sum(p, axis=0, keepdims=True)
        rows = slice(kvl * hd, (kvl + 1) * hd)
        vt = jnp.concatenate([vt_ref[j0][rows, :], vt_ref[j0 + 1][rows, :]], axis=1)
        acc = alpha * acc + jnp.dot(vt, p.astype(BF16), preferred_element_type=F32)
        return m_new, l_new, acc

    def far(pr, carry):
        return tuple(far_pair(carry[kvl], pr, kvl) for kvl in range(2))

    states = lax.fori_loop(0, n_pairs, far, tuple(states))
    ot = jnp.concatenate([acc / l for (_, l, acc) in states], axis=0)
    o = ot.T
    for kvl in range(2):
        for hh in range(2):
            o_ref[:, (2 * kvl + hh) * hd:(2 * kvl + hh + 1) * hd] = o[hh * blk:(hh + 1) * blk, kvl * hd:(kvl + 1) * hd]


def _moba_prompt(proj, bias_rel, n_batch, seq, q_col, k_col, v_col):
    blk = MOBA_BLOCK
    nblk = seq // blk

    def pair_up(t):
        return jnp.transpose(t.reshape(KV_HEADS_A, 2, blk, blk), (0, 2, 1, 3)).reshape(KV_HEADS_A, blk, 2 * blk)

    diag = pair_up(_bias_tiles(bias_rel, blk, blk, 0, sign=-1, lo=0, hi=2 * blk))
    prev = pair_up(_bias_tiles(bias_rel, blk, blk, blk, sign=-1))
    far = jnp.broadcast_to(bias_rel[T5_BUCKETS - 1].reshape(KV_HEADS_A, 1, 2, 1), (KV_HEADS_A, 1, 2, blk))
    far = far.reshape(KV_HEADS_A, 1, 2 * blk)
    rows_per_b = seq // blk
    return pl.pallas_call(
        _moba_prompt_kernel,
        grid=(n_batch, 2, nblk),
        in_specs=[pl.BlockSpec((blk, 2 * LANES), lambda b, p, i: (b * rows_per_b + i, q_col // 2 + p)),
                  pl.BlockSpec((seq, LANES), lambda b, p, i: (b, k_col + p)),
                  pl.BlockSpec((seq, LANES), lambda b, p, i: (b, v_col + p)),
                  pl.BlockSpec((2, blk, 2 * blk), lambda b, p, i: (p, 0, 0)),
                  pl.BlockSpec((2, blk, 2 * blk), lambda b, p, i: (p, 0, 0)),
                  pl.BlockSpec((2, 1, 2 * blk), lambda b, p, i: (p, 0, 0))],
        out_specs=pl.BlockSpec((blk, 2 * LANES), lambda b, p, i: (b * rows_per_b + i, p)),
        out_shape=jax.ShapeDtypeStruct((n_batch * seq, 4 * LANES), F32),
        scratch_shapes=[pltpu.VMEM((nblk, LANES), F32),
                        pltpu.VMEM((nblk, LANES, blk), BF16),
                        pltpu.VMEM((2, nblk, 2 * blk), F32)],
        compiler_params=_cparams(("parallel", "parallel", "arbitrary")),
        name="moba_prompt",
    )(proj, proj, proj, diag, prev, far)


MOBA_DEC_BLOCKS_PER_STEP = 16
PAGES_PER_BLOCK = MOBA_BLOCK // PAGE_SIZE


MOBA_DEC_SLOTS = 3


def _moba_sample_kernel(pt_ref, q_ref, kn_ref, vn_ref, b31_ref, bfar_ref, bown_ref, ck_hbm, cv_hbm, o_ref,
                        g_ref, m_ref, l_ref, part_ref, kbuf, vbuf, sem, *, n_blocks, nb, layer):
    npg = nb * PAGES_PER_BLOCK
    kv_w = kbuf.shape[2]
    b, s = pl.program_id(0), pl.program_id(1)
    n_steps = pl.num_programs(1)
    total = pl.num_programs(0) * n_steps
    g = b * n_steps + s
    q = q_ref[...]
    qb = q.astype(BF16)
    lane = lax.broadcasted_iota(jnp.int32, g_ref.shape, 1)

    def page_copies(step):
        bb = step // n_steps
        ss = step - bb * n_steps
        slot = lax.rem(step, MOBA_DEC_SLOTS)
        out = []
        for k in range(npg):
            page = pt_ref[bb, ss * npg + k]
            for hbm, buf in ((ck_hbm, kbuf), (cv_hbm, vbuf)):
                out.append(pltpu.make_async_copy(hbm.at[page, pl.ds(layer * kv_w, kv_w), :],
                                                 buf.at[slot, k], sem.at[slot]))
        return out

    @pl.when(g == 0)
    def _prime():
        for c in page_copies(g):
            c.start()

    @pl.when((g == 0) & (total > 1))
    def _prime_next():
        for c in page_copies(g + 1):
            c.start()

    @pl.when(s == 0)
    def _init():
        g_ref[...] = jnp.zeros_like(g_ref)
        m_ref[...] = jnp.zeros_like(m_ref)
        l_ref[...] = jnp.zeros_like(l_ref)

    for c in page_copies(g):
        c.wait()
    slot = lax.rem(g, MOBA_DEC_SLOTS)
    lane_k = lax.broadcasted_iota(jnp.int32, (kv_w, LANES), 1)
    kmean = jnp.zeros((kv_w, LANES), F32)

    for j in range(nb):
        n = s * nb + j
        kts = [kbuf[slot, j * PAGES_PER_BLOCK + t] for t in range(PAGES_PER_BLOCK)]
        vts = [vbuf[slot, j * PAGES_PER_BLOCK + t] for t in range(PAGES_PER_BLOCK)]
        sc = jnp.concatenate([_dot(qb, kt) for kt in kts], axis=1)
        ksum = kts[0].sum(axis=1, keepdims=True)
        for kt in kts[1:]:
            ksum = ksum + kt.sum(axis=1, keepdims=True)
        kmean = kmean + jnp.where(lane_k == n, ksum * (1.0 / MOBA_BLOCK), 0.0)
        sc = sc + jnp.where(n == n_blocks - 1, b31_ref[...], bfar_ref[...])
        mx = jnp.max(sc, axis=1, keepdims=True)
        p = jnp.exp(sc - mx)
        hit = lane == n
        m_ref[...] = jnp.where(hit, mx, m_ref[...])
        l_ref[...] = jnp.where(hit, jnp.sum(p, axis=1, keepdims=True), l_ref[...])
        o = None
        for t, vt in enumerate(vts):
            part = _dot_nt(p[:, t * PAGE_SIZE:(t + 1) * PAGE_SIZE], vt)
            o = part if o is None else o + part
        part_ref[n] = o
    g_ref[...] += _dot(qb, kmean)

    @pl.when(g + 2 < total)
    def _prefetch():
        for c in page_copies(g + 2):
            c.start()

    @pl.when(s == pl.num_programs(1) - 1)
    def _merge():
        gate = jnp.where(lane < n_blocks, g_ref[...], NEG_INF)
        sel = jnp.zeros(gate.shape, jnp.bool_)
        for _ in range(MOBA_TOPK):
            mxg = jnp.max(gate, axis=1, keepdims=True)
            idx = jnp.min(jnp.where(gate == mxg, lane, LANES), axis=1, keepdims=True)
            hit = (lane == idx) & (mxg > NEG_INF)
            sel = sel | hit
            gate = jnp.where(lane == idx, NEG_INF, gate)
        s_own = _dot_nt(qb, kn_ref[...]) + bown_ref[...]
        m_sel = jnp.where(sel, m_ref[...], NEG_INF)
        m_all = jnp.maximum(jnp.max(m_sel, axis=1, keepdims=True), jnp.max(s_own, axis=1, keepdims=True))
        w = jnp.where(sel, jnp.exp(m_sel - m_all), 0.0)
        p_own = jnp.exp(s_own - m_all)
        den = jnp.sum(w * l_ref[...], axis=1, keepdims=True) + jnp.sum(p_own, axis=1, keepdims=True)
        acc = _dot(p_own, vn_ref[...])
        for n in range(n_blocks):
            acc = acc + w[:, n:n + 1] * part_ref[n]
        o_ref[...] = acc / den


def _moba_sample(q_s, k_new, v_new, cache_k, cache_v, page_table, bias_rel, layer=0):
    n_dec, t_dec = q_s.shape[:2]
    n_pages = page_table.shape[1]
    n_blocks = n_pages // PAGES_PER_BLOCK
    nb = min(MOBA_DEC_BLOCKS_PER_STEP, n_blocks)
    npg = nb * PAGES_PER_BLOCK
    assert n_blocks % nb == 0 and n_blocks <= LANES and t_dec <= 8
    rows = N_ATT_HEADS * t_dec
    kv_w = KV_HEADS_A * HEAD_DIM
    head_kv = np.arange(N_ATT_HEADS) // (N_ATT_HEADS // KV_HEADS_A)
    qh = jnp.transpose(q_s, (0, 2, 1, 3)) * (HEAD_DIM ** -0.5)
    onehot = jnp.asarray(np.eye(KV_HEADS_A, dtype=np.float32)[head_kv])
    qbd = (qh[:, :, :, None, :] * onehot[None, :, None, :, None]).reshape(n_dec, rows, kv_w)
    pad = 8 - t_dec
    kn = jnp.pad(k_new, ((0, 0), (0, pad), (0, 0)))
    vn = jnp.pad(v_new, ((0, 0), (0, pad), (0, 0)))
    bt = bias_rel.T
    past = n_pages * PAGE_SIZE
    tq = np.arange(t_dec)[None, :, None]
    ik = np.arange(MOBA_BLOCK)[None, None, :]
    idx31 = _t5_bucket_np(past + tq - ((n_blocks - 1) * MOBA_BLOCK + ik))[0]
    b31 = bt[:, idx31].reshape(rows, MOBA_BLOCK)
    bfar = jnp.broadcast_to(bt[:, T5_BUCKETS - 1][:, None, None], (N_ATT_HEADS, t_dec, 1)).reshape(rows, 1)
    tk = np.arange(8)[None, :]
    rel_own = np.arange(t_dec)[:, None] - tk
    ok_own = (rel_own >= 0) & (tk < t_dec)
    bown = jnp.where(jnp.asarray(ok_own)[None], bt[:, _t5_bucket_np(rel_own)], NEG_INF).reshape(rows, 8)

    const2 = lambda b, s, pt: (0, 0)
    grid_spec = pltpu.PrefetchScalarGridSpec(
        num_scalar_prefetch=1,
        grid=(n_dec, n_blocks // nb),
        in_specs=[pl.BlockSpec((None, rows, kv_w), lambda b, s, pt: (b, 0, 0)),
                  pl.BlockSpec((None, 8, kv_w), lambda b, s, pt: (b, 0, 0)),
                  pl.BlockSpec((None, 8, kv_w), lambda b, s, pt: (b, 0, 0)),
                  pl.BlockSpec((rows, MOBA_BLOCK), const2),
                  pl.BlockSpec((rows, 1), const2),
                  pl.BlockSpec((rows, 8), const2),
                  pl.BlockSpec(memory_space=pl.ANY),
                  pl.BlockSpec(memory_space=pl.ANY)],
        out_specs=pl.BlockSpec((None, rows, kv_w), lambda b, s, pt: (b, 0, 0)),
        scratch_shapes=[pltpu.VMEM((rows, LANES), F32),
                        pltpu.VMEM((rows, LANES), F32),
                        pltpu.VMEM((rows, LANES), F32),
                        pltpu.VMEM((n_blocks, rows, kv_w), F32),
                        pltpu.VMEM((MOBA_DEC_SLOTS, npg, kv_w, PAGE_SIZE), F32),
                        pltpu.VMEM((MOBA_DEC_SLOTS, npg, kv_w, PAGE_SIZE), F32),
                        pltpu.SemaphoreType.DMA((MOBA_DEC_SLOTS,))],
    )
    out = pl.pallas_call(
        functools.partial(_moba_sample_kernel, n_blocks=n_blocks, nb=nb, layer=layer),
        grid_spec=grid_spec,
        out_shape=jax.ShapeDtypeStruct((n_dec, rows, kv_w), F32),
        compiler_params=_cparams(("arbitrary", "arbitrary")),
        name="moba_sample",
    )(page_table, qbd, kn, vn, b31, bfar, bown, cache_k, cache_v)
    out = out.reshape(n_dec, N_ATT_HEADS, t_dec, KV_HEADS_A, HEAD_DIM)
    out = out[:, np.arange(N_ATT_HEADS), :, head_kv, :]
    return jnp.transpose(out, (1, 2, 0, 3)).reshape(n_dec, t_dec, N_ATT_HEADS * HEAD_DIM)


SWA_GROUP = N_ATT_HEADS // KV_HEADS_C


def _sink_softmax_pv(parts, sink):
    m = sink
    for s, _ in parts:
        m = jnp.maximum(m, jnp.max(s, axis=1, keepdims=True))
    den = jnp.exp(sink - m)
    ps = []
    for s, _ in parts:
        p = jnp.exp(s - m)
        den = den + jnp.sum(p, axis=1, keepdims=True)
        ps.append(p)
    o = None
    for p, (_, v) in zip(ps, parts):
        t = _dot(p / den, v)
        o = t if o is None else o + t
    return o


def _swa_prompt_kernel(q_ref, kp_ref, ko_ref, vp_ref, vo_ref, bias_ref, sink_ref, o_ref):
    i = pl.program_id(1)
    w, hd, g = WINDOW, HEAD_DIM, SWA_GROUP
    q = q_ref[...] * (hd ** -0.5)
    kcat = jnp.concatenate([kp_ref[...], ko_ref[...]], axis=0)
    vt = jnp.concatenate([vp_ref[...], vo_ref[...]], axis=0).T.astype(BF16)
    row = lax.broadcasted_iota(jnp.int32, (2 * w, g * w), 0)
    has_key = (row >= w) | (i > 0)
    ots = []
    for c in range(KV_HEADS_C):
        qs = jnp.concatenate([q[:, (c * g + j) * hd:(c * g + j + 1) * hd] for j in range(g)], axis=0)
        s = _dot_nt(kcat[:, c * hd:(c + 1) * hd], qs) + bias_ref[c]
        s = jnp.where(has_key, s, NEG_INF)
        sink = sink_ref[c]
        m = jnp.maximum(sink, jnp.max(s, axis=0, keepdims=True))
        p = jnp.exp(s - m)
        den = jnp.sum(p, axis=0, keepdims=True) + jnp.exp(sink - m)
        ots.append(jnp.dot(vt[c * hd:(c + 1) * hd, :], (p / den).astype(BF16), preferred_element_type=F32))
    o = jnp.concatenate(ots, axis=0).T
    for c in range(KV_HEADS_C):
        for j in range(g):
            o_ref[:, (c * g + j) * hd:(c * g + j + 1) * hd] = o[j * w:(j + 1) * w, c * hd:(c + 1) * hd]


def _swa_prompt(proj, bias_rel, sinks, n_batch, seq, q_col, k_col, v_col):
    w, g = WINDOW, SWA_GROUP
    nb = seq // w
    band = _bias_tiles(bias_rel, 2 * w, w, w, sign=-1, lo=0, hi=w)
    band = jnp.transpose(band.reshape(KV_HEADS_C, g, 2 * w, w), (0, 2, 1, 3)).reshape(KV_HEADS_C, 2 * w, g * w)
    sink = jnp.broadcast_to(sinks.reshape(KV_HEADS_C, 1, g, 1), (KV_HEADS_C, 1, g, w)).reshape(KV_HEADS_C, 1, g * w)
    own = lambda c: (lambda b, i: (b * nb + i, c))
    prev = lambda c: (lambda b, i: (b * nb + jnp.maximum(i - 1, 0), c))
    return pl.pallas_call(
        _swa_prompt_kernel,
        grid=(n_batch, nb),
        in_specs=[pl.BlockSpec((w, 4 * LANES), lambda b, i: (b * nb + i, q_col // 4)),
                  pl.BlockSpec((w, LANES), prev(k_col)),
                  pl.BlockSpec((w, LANES), own(k_col)),
                  pl.BlockSpec((w, LANES), prev(v_col)),
                  pl.BlockSpec((w, LANES), own(v_col)),
                  pl.BlockSpec((KV_HEADS_C, 2 * w, g * w), lambda b, i: (0, 0, 0)),
                  pl.BlockSpec((KV_HEADS_C, 1, g * w), lambda b, i: (0, 0, 0))],
        out_specs=pl.BlockSpec((w, 4 * LANES), lambda b, i: (b * nb + i, 0)),
        out_shape=jax.ShapeDtypeStruct((n_batch * seq, 4 * LANES), F32),
        compiler_params=_cparams(("parallel", "parallel")),
        name="swa_prompt",
    )(proj, proj, proj, proj, proj, band, sink)


SWA_DEC_SEQS_PER_STEP = 8


def _swa_sample_kernel(q_ref, kb_ref, vb_ref, kn_ref, vn_ref, bbuf_ref, bnew_ref, sink_ref, o_ref):
    for i in range(SWA_DEC_SEQS_PER_STEP):
        q = q_ref[i]
        s_buf = _dot_nt(q, kb_ref[i]) + bbuf_ref[...]
        s_new = _dot_nt(q, kn_ref[i]) + bnew_ref[...]
        o_ref[i] = _sink_softmax_pv([(s_buf, vb_ref[i]), (s_new, vn_ref[i])], sink_ref[...])


def _swa_sample(q_s, k_new, v_new, buf_k, buf_v, bias_rel, sinks):
    n_dec, t_dec = q_s.shape[:2]
    nbuf = buf_k.shape[1]
    rows = N_ATT_HEADS * t_dec
    kv_w = KV_HEADS_C * HEAD_DIM
    sb = SWA_DEC_SEQS_PER_STEP
    assert n_dec % sb == 0 and t_dec <= 8
    head_kv = np.arange(N_ATT_HEADS) // SWA_GROUP
    qh = jnp.transpose(q_s, (0, 2, 1, 3)) * (HEAD_DIM ** -0.5)
    onehot = jnp.asarray(np.eye(KV_HEADS_C, dtype=np.float32)[head_kv])
    qbd = (qh[:, :, :, None, :] * onehot[None, :, None, :, None]).reshape(n_dec, rows, kv_w)
    pad = 8 - t_dec
    kn = jnp.pad(k_new, ((0, 0), (0, pad), (0, 0)))
    vn = jnp.pad(v_new, ((0, 0), (0, pad), (0, 0)))
    bt = bias_rel.T
    t = np.arange(t_dec)[:, None]
    rel_buf = nbuf + t - np.arange(nbuf)[None, :]
    ok_buf = (rel_buf >= 0) & (rel_buf <= WINDOW)
    bbuf = jnp.where(jnp.asarray(ok_buf)[None], bt[:, _t5_bucket_np(rel_buf)], NEG_INF).reshape(rows, nbuf)
    tk = np.arange(8)[None, :]
    rel_new = t - tk
    ok_new = (rel_new >= 0) & (tk < t_dec)
    bnew = jnp.where(jnp.asarray(ok_new)[None], bt[:, _t5_bucket_np(rel_new)], NEG_INF).reshape(rows, 8)
    sink = jnp.broadcast_to(sinks[:, None, None], (N_ATT_HEADS, t_dec, 1)).reshape(rows, 1)
    seq3 = lambda r, c: pl.BlockSpec((sb, r, c), lambda i: (i, 0, 0))
    const2 = lambda r, c: pl.BlockSpec((r, c), lambda i: (0, 0))
    out = pl.pallas_call(
        _swa_sample_kernel,
        grid=(n_dec // sb,),
        in_specs=[seq3(rows, kv_w), seq3(nbuf, kv_w), seq3(nbuf, kv_w), seq3(8, kv_w), seq3(8, kv_w),
                  const2(rows, nbuf), const2(rows, 8), const2(rows, 1)],
        out_specs=seq3(rows, kv_w),
        out_shape=jax.ShapeDtypeStruct((n_dec, rows, kv_w), F32),
        compiler_params=_cparams(("parallel",)),
        name="swa_sample",
    )(qbd, buf_k, buf_v, kn, vn, bbuf, bnew, sink)
    out = out.reshape(n_dec, N_ATT_HEADS, t_dec, KV_HEADS_C, HEAD_DIM)
    out = out[:, np.arange(N_ATT_HEADS), :, head_kv, :]
    return jnp.transpose(out, (1, 2, 0, 3)).reshape(n_dec, t_dec, N_ATT_HEADS * HEAD_DIM)


def _softplus(x):
    return jnp.maximum(x, 0.0) + jnp.log1p(jnp.exp(-jnp.abs(x)))


def _neg_expm1(y):
    return -jnp.tanh(0.5 * y) * (jnp.exp(y) + 1.0)


def _shifted_rows(prev8, cur, k):
    if k == 0:
        return cur
    ext = jnp.concatenate([prev8, cur], axis=0)
    return pltpu.roll(ext, k, axis=0)[8:]


def _causal_conv_rows(prev8, cur, w):
    y = cur * w[CONV_WIDTH - 1:CONV_WIDTH, :]
    for k in range(1, CONV_WIDTH):
        y = y + _shifted_rows(prev8, cur, k) * w[CONV_WIDTH - 1 - k:CONV_WIDTH - k, :]
    return y


def _lru_gates(xc, wa, ba, wx, bx, sp):
    r = jax.nn.sigmoid(_dot(xc, wa) + ba)
    i = jax.nn.sigmoid(_dot(xc, wx) + bx)
    log_a = -LRU_C * r * sp
    a = jnp.exp(log_a)
    b = jnp.sqrt(_neg_expm1(2.0 * log_a)) * (i * xc)
    return a, b


LRU_ROWS = 256


def _lru_prompt_kernel(x_ref, gr_ref, cw_ref, cb_ref, wa_ref, ba_ref, wx_ref, bx_ref, lam_ref,
                       o_ref, hl_ref):
    n = x_ref.shape[0] // LRU_ROWS
    sp = _softplus(-lam_ref[...])
    row = lax.broadcasted_iota(jnp.int32, (LRU_ROWS, LANES), 0)

    def body(c, carry):
        prev8, h = carry
        r0 = pl.multiple_of(c * LRU_ROWS, LRU_ROWS)
        cur = x_ref[pl.ds(r0, LRU_ROWS), :]
        xc = _causal_conv_rows(prev8, cur, cw_ref[...]) + cb_ref[...]
        a, b = _lru_gates(xc, wa_ref[...], ba_ref[...], wx_ref[...], bx_ref[...], sp)
        d = 1
        while d < LRU_ROWS:
            a_s = jnp.where(row >= d, pltpu.roll(a, d, axis=0), 1.0)
            b_s = jnp.where(row >= d, pltpu.roll(b, d, axis=0), 0.0)
            b = a * b_s + b
            a = a * a_s
            d *= 2
        hs = a * h + b
        o_ref[pl.ds(r0, LRU_ROWS), :] = hs * jax.nn.gelu(gr_ref[pl.ds(r0, LRU_ROWS), :])
        return cur[LRU_ROWS - 8:], hs[LRU_ROWS - 1:]

    _, h = lax.fori_loop(0, n, body, (jnp.zeros((8, LANES), F32), jnp.zeros((1, LANES), F32)))
    hl_ref[...] = h


def _lru_prompt(proj, n_batch, seq, x_col, g_col, conv_w, conv_b, wa_bd, b_a, wx_bd, b_x, lam):
    ncb = D_RNN // LANES
    vec = lambda v: v.reshape(1, D_RNN)
    vspec = pl.BlockSpec((1, LANES), lambda b, c: (0, c))
    out, hl = pl.pallas_call(
        _lru_prompt_kernel,
        grid=(n_batch, ncb),
        in_specs=[pl.BlockSpec((seq, LANES), lambda b, c: (b, x_col + c)),
                  pl.BlockSpec((seq, LANES), lambda b, c: (b, g_col + c)),
                  pl.BlockSpec((CONV_WIDTH, LANES), lambda b, c: (0, c)),
                  vspec,
                  pl.BlockSpec((LANES, LANES), lambda b, c: (c, c)), vspec,
                  pl.BlockSpec((LANES, LANES), lambda b, c: (c, c)), vspec,
                  vspec],
        out_specs=[pl.BlockSpec((seq, LANES), lambda b, c: (b, c)),
                   pl.BlockSpec((None, 1, LANES), lambda b, c: (b, 0, c))],
        out_shape=[jax.ShapeDtypeStruct((n_batch * seq, D_RNN), F32),
                   jax.ShapeDtypeStruct((n_batch, 1, D_RNN), F32)],
        compiler_params=_cparams(("parallel", "parallel")),
        name="lru_prompt",
    )(proj, proj, conv_w, vec(conv_b), wa_bd, vec(b_a), wx_bd, vec(b_x), vec(lam))
    return out, hl.reshape(n_batch, D_RNN)


def _lru_sample_kernel(xp_ref, gr_ref, h0_ref, cw_ref, cb_ref, wa_ref, ba_ref, wx_ref, bx_ref, lam_ref,
                       o_ref, hl_ref):
    t_dec = gr_ref.shape[0]
    sp = _softplus(-lam_ref[...])
    h = h0_ref[...]
    for t in range(t_dec):
        xc = cb_ref[...]
        for j in range(CONV_WIDTH):
            xc = xc + xp_ref[t + j] * cw_ref[j:j + 1, :]
        a, b = _lru_gates(xc, wa_ref[...], ba_ref[...], wx_ref[...], bx_ref[...], sp)
        h = a * h + b
        o_ref[t] = h * jax.nn.gelu(gr_ref[t])
    hl_ref[...] = h


def _lru_sample(xr, gr, h0, conv0, conv_w, conv_b, wa_bd, b_a, wx_bd, b_x, lam):
    n_dec, t_dec, _ = xr.shape
    xp = jnp.transpose(jnp.concatenate([conv0, xr], axis=1), (1, 0, 2))
    grt = jnp.transpose(gr, (1, 0, 2))
    vec = lambda v: v.reshape(1, D_RNN)
    full = lambda *s: pl.BlockSpec(s, lambda i: (0,) * len(s))
    out, hl = pl.pallas_call(
        _lru_sample_kernel,
        grid=(1,),
        in_specs=[full(t_dec + CONV_WIDTH - 1, n_dec, D_RNN), full(t_dec, n_dec, D_RNN), full(n_dec, D_RNN),
                  full(CONV_WIDTH, D_RNN), full(1, D_RNN), full(D_RNN, D_RNN), full(1, D_RNN),
                  full(D_RNN, D_RNN), full(1, D_RNN), full(1, D_RNN)],
        out_specs=[full(t_dec, n_dec, D_RNN), full(n_dec, D_RNN)],
        out_shape=[jax.ShapeDtypeStruct((t_dec, n_dec, D_RNN), F32),
                   jax.ShapeDtypeStruct((n_dec, D_RNN), F32)],
        compiler_params=_cparams(("arbitrary",)),
        name="lru_sample",
    )(xp, grt, h0, conv_w, vec(conv_b), wa_bd, vec(b_a), wx_bd, vec(b_x), vec(lam))
    return jnp.transpose(out, (1, 0, 2)), hl


def _block_diag(w):
    n, k, _ = w.shape
    eye = jnp.asarray(np.eye(n, dtype=np.float32))
    return (w[:, :, None, :] * eye[:, None, :, None]).reshape(n * k, n * k)


GDN_ROWS = 256
GDN_INTRA_ROWS = 2048


def _bmm(a, b):
    return jnp.einsum('nij,njk->nik', a.astype(BF16), b.astype(BF16), preferred_element_type=F32)


def _bmm_nt(a, b):
    return jnp.einsum('nid,njd->nij', a.astype(BF16), b.astype(BF16), preferred_element_type=F32)


def _col(x, lane, idx):
    return jnp.sum(jnp.where(lane == idx, x, 0.0), axis=1, keepdims=True)


def _gdn_gates_kernel(x_ref, a_ref, dt_ref, beta_ref, gc_ref, *, chunk, pad_period, pad_rows):
    x = x_ref[...]
    tm = x.shape[0]
    beta = jax.nn.sigmoid(x)
    g = -jnp.exp(a_ref[...]) * _softplus(x + dt_ref[...])
    if pad_period:
        row = lax.broadcasted_iota(jnp.int32, x.shape, 0)
        valid = (row & (pad_period - 1)) >= pad_rows
        beta = jnp.where(valid, beta, 0.0)
        g = jnp.where(valid, g, 0.0)
    r = lax.broadcasted_iota(jnp.int32, (tm, tm), 0)
    c = lax.broadcasted_iota(jnp.int32, (tm, tm), 1)
    sh = chunk.bit_length() - 1
    tri = jnp.where(((r >> sh) == (c >> sh)) & (c <= r), 1.0, 0.0)
    beta_ref[...] = beta
    gc_ref[...] = jnp.dot(tri, g, preferred_element_type=F32, precision=lax.Precision.HIGHEST)


def _gdn_gates(x, col, rows, a_log, dt_bias, chunk, pad_period=0, pad_rows=0):
    tm = GDN_ROWS
    assert rows % tm == 0 and tm % chunk == 0
    a_pad = jnp.pad(a_log.reshape(1, -1), ((0, 0), (GDN_HEADS, LANES - 2 * GDN_HEADS)))
    dt_pad = jnp.pad(dt_bias.reshape(1, -1), ((0, 0), (GDN_HEADS, LANES - 2 * GDN_HEADS)))
    return pl.pallas_call(
        functools.partial(_gdn_gates_kernel, chunk=chunk, pad_period=pad_period, pad_rows=pad_rows),
        grid=(rows // tm,),
        in_specs=[pl.BlockSpec((tm, LANES), lambda i: (i, col)),
                  pl.BlockSpec((1, LANES), lambda i: (0, 0)),
                  pl.BlockSpec((1, LANES), lambda i: (0, 0))],
        out_specs=[pl.BlockSpec((tm, LANES), lambda i: (i, 0))] * 2,
        out_shape=[jax.ShapeDtypeStruct((rows, LANES), F32)] * 2,
        compiler_params=_cparams(("parallel",)),
        name="gdn_gates",
    )(x, a_pad, dt_pad)


def _gdn_prep_kernel(x_ref, cw_ref, beta_ref, gc_ref, *out_refs, mode):
    hd = HEAD_DIM
    n = x_ref.shape[0] // GDN_ROWS
    h0 = 2 * pl.program_id(1)
    lane = lax.broadcasted_iota(jnp.int32, (GDN_ROWS, LANES), 1)
    lo = lane < hd

    def halves(c0, c1):
        return jnp.where(lo, c0, c1)

    def put(ref, r0, val):
        ref[0, pl.ds(r0, GDN_ROWS), :] = val[:, :hd]
        ref[1, pl.ds(r0, GDN_ROWS), :] = val[:, hd:]

    def body(c, prev8):
        r0 = pl.multiple_of(c * GDN_ROWS, GDN_ROWS)
        cur = x_ref[pl.ds(r0, GDN_ROWS), :]
        y = _causal_conv_rows(prev8, cur, cw_ref[...])
        y = y * jax.nn.sigmoid(y)
        if mode in ("q", "k"):
            ss = y * y
            s0 = jnp.sum(jnp.where(lo, ss, 0.0), axis=1, keepdims=True)
            s1 = jnp.sum(jnp.where(lo, 0.0, ss), axis=1, keepdims=True)
            y = y * halves(lax.rsqrt(s0 + NORM_EPS), lax.rsqrt(s1 + NORM_EPS))
        if mode == "q":
            put(out_refs[0], r0, y * (hd ** -0.5))
        else:
            beta = beta_ref[pl.ds(r0, GDN_ROWS), :]
            bb = halves(_col(beta, lane, h0), _col(beta, lane, h0 + 1))
            if mode == "k":
                gc = gc_ref[pl.ds(r0, GDN_ROWS), :]
                put(out_refs[0], r0, y)
                put(out_refs[1], r0, y * bb)
                put(out_refs[2], r0, halves(_col(gc, lane, GDN_HEADS + h0), _col(gc, lane, GDN_HEADS + h0 + 1)))
            else:
                put(out_refs[0], r0, y * bb)
        return cur[GDN_ROWS - 8:]

    lax.fori_loop(0, n, body, jnp.zeros((8, LANES), F32))


def _gdn_prep(x, x_col, conv_w, beta, gc, n_batch, seq, mode):
    sec = {"q": 0, "k": 4, "v": 8}[mode]
    n_out = {"q": 1, "k": 3, "v": 1}[mode]
    hspec = pl.BlockSpec((2, seq, HEAD_DIM), lambda b, c: (b * 4 + c, 0, 0))
    gspec = pl.BlockSpec((seq, LANES), lambda b, c: (b, 0))
    outs = pl.pallas_call(
        functools.partial(_gdn_prep_kernel, mode=mode),
        grid=(n_batch, 4),
        in_specs=[pl.BlockSpec((seq, LANES), lambda b, c: (b, x_col + c)),
                  pl.BlockSpec((CONV_WIDTH, LANES), lambda b, c: (0, sec + c)),
                  gspec, gspec],
        out_specs=[hspec] * n_out,
        out_shape=[jax.ShapeDtypeStruct((n_batch * GDN_HEADS, seq, HEAD_DIM), F32)] * n_out,
        compiler_params=_cparams(("parallel", "parallel")),
        name="gdn_prep_" + mode,
    )(x, conv_w, beta, gc)
    return outs


def _gdn_intra_kernel(q_ref, k_ref, kb_ref, vb_ref, gcb_ref, gcr_ref,
                      u_ref, w_ref, qg_ref, kd_ref, qk_ref, gl_ref, *, chunk):
    hd = HEAD_DIM
    rows = q_ref.shape[0]
    n = rows // chunk
    r3 = lambda ref: ref[...].reshape(n, chunk, hd)
    q, k, kb, vb, gcb = r3(q_ref), r3(k_ref), r3(kb_ref), r3(vb_ref), r3(gcb_ref)
    gcr = gcr_ref[...]
    ci = lax.broadcasted_iota(jnp.int32, (n, chunk, chunk), 1)
    si = lax.broadcasted_iota(jnp.int32, (n, chunk, chunk), 2)
    decay = jnp.exp(jnp.where(ci >= si, gcb[:, :, :chunk] - gcr[:, None, :], NEG_INF))
    a = jnp.where(ci > si, _bmm_nt(kb, k) * decay, 0.0)
    b = -a
    y = b
    p = _bmm(b, b)
    stages = chunk.bit_length() - 2
    for t in range(stages):
        y = y + p + _bmm(y, p)
        if t < stages - 1:
            p = _bmm(p, p)
    eg = jnp.exp(gcb)
    rhs = jnp.concatenate([vb, kb * eg], axis=-1)
    sol = rhs + _bmm(y, rhs)
    gl = gcb[:, chunk - 1:chunk, :]
    u_ref[...] = sol[:, :, :hd].reshape(rows, hd)
    w_ref[...] = sol[:, :, hd:].reshape(rows, hd)
    qg_ref[...] = (q * eg).reshape(rows, hd)
    kd_ref[...] = (k * jnp.exp(gl - gcb)).reshape(rows, hd)
    qk_ref[...] = (_bmm_nt(q, k) * decay).reshape(rows, chunk)
    gl_ref[...] = jnp.exp(gl).reshape(n, hd)


def _gdn_intra(q, k, kb, vb, gcb, gc_row, chunk, rows_per_step):
    bh, seq, hd = q.shape
    tr = rows_per_step
    n = tr // chunk
    assert seq % tr == 0
    hspec = pl.BlockSpec((None, tr, hd), lambda h, i: (h, i, 0))
    shape = jax.ShapeDtypeStruct((bh, seq, hd), F32)
    return pl.pallas_call(
        functools.partial(_gdn_intra_kernel, chunk=chunk),
        grid=(bh, seq // tr),
        in_specs=[hspec] * 5 + [pl.BlockSpec((None, n, chunk), lambda h, i: (h, i, 0))],
        out_specs=[hspec] * 4 + [pl.BlockSpec((None, tr, chunk), lambda h, i: (h, i, 0)),
                                 pl.BlockSpec((None, n, hd), lambda h, i: (h, i, 0))],
        out_shape=[shape] * 4 + [jax.ShapeDtypeStruct((bh, seq, chunk), F32),
                                 jax.ShapeDtypeStruct((bh, seq // chunk, hd), F32)],
        compiler_params=_cparams(("parallel", "parallel")),
        name="gdn_intra",
    )(q, k, kb, vb, gcb, gc_row)


def _gdn_chunk_update(s, u, w, qg, kd, qk, gl):
    v_new = u - _bmm(w, s)
    o = _bmm(qg, s) + _bmm(qk, v_new)
    s = s * gl + jnp.einsum('ncd,nce->nde', kd.astype(BF16), v_new.astype(BF16), preferred_element_type=F32)
    return s, o


def _gdn_scan_kernel(u_ref, w_ref, qg_ref, kd_ref, qk_ref, gl_ref, o_ref, sfin_ref, s_scr, *, chunk):
    j = pl.program_id(1)

    @pl.when(j == 0)
    def _zero():
        s_scr[...] = jnp.zeros_like(s_scr)

    n = u_ref.shape[1] // chunk
    s = s_scr[...]
    for c in range(n):
        sl = slice(c * chunk, (c + 1) * chunk)
        s, o = _gdn_chunk_update(s, u_ref[:, sl, :], w_ref[:, sl, :], qg_ref[:, sl, :], kd_ref[:, sl, :],
                                 qk_ref[:, sl, :], gl_ref[:, c:c + 1, :])
        o_ref[:, sl, :] = o
    s_scr[...] = s

    @pl.when(j == pl.num_programs(1) - 1)
    def _final():
        sfin_ref[...] = s


def _gdn_scan(u, w, qg, kd, qk, gl, chunk, heads_per_step, rows_per_step):
    bh, seq, hd = u.shape
    hb, tr = heads_per_step, rows_per_step
    n = tr // chunk
    hspec = pl.BlockSpec((hb, tr, hd), lambda h, i: (h, i, 0))
    return pl.pallas_call(
        functools.partial(_gdn_scan_kernel, chunk=chunk),
        grid=(bh // hb, seq // tr),
        in_specs=[hspec] * 4 + [pl.BlockSpec((hb, tr, chunk), lambda h, i: (h, i, 0)),
                                pl.BlockSpec((hb, n, hd), lambda h, i: (h, i, 0))],
        out_specs=[hspec, pl.BlockSpec((hb, hd, hd), lambda h, i: (h, 0, 0))],
        out_shape=[jax.ShapeDtypeStruct((bh, seq, hd), F32), jax.ShapeDtypeStruct((bh, hd, hd), F32)],
        scratch_shapes=[pltpu.VMEM((hb, hd, hd), F32)],
        compiler_params=_cparams(("parallel", "arbitrary")),
        name="gdn_scan",
    )(u, w, qg, kd, qk, gl)


GDN_DEC_SEQS_PER_STEP = 16


def _gdn_step_kernel(s0_ref, u_ref, w_ref, qg_ref, kd_ref, qk_ref, gl_ref, o_ref, s_ref, *, chunk):
    hd = HEAD_DIM
    n = s0_ref.shape[0]
    r3 = lambda ref, last: ref[...].reshape(n, chunk, last)
    s, o = _gdn_chunk_update(s0_ref[...], r3(u_ref, hd), r3(w_ref, hd), r3(qg_ref, hd), r3(kd_ref, hd),
                             r3(qk_ref, chunk), gl_ref[...][:, None, :])
    o_ref[...] = o.reshape(n * chunk, hd)
    s_ref[...] = s


def _gdn_step(s0, u, w, qg, kd, qk, gl, chunk):
    n_dec, nh, hd, _ = s0.shape
    sb = GDN_DEC_SEQS_PER_STEP
    assert n_dec % sb == 0
    hspec = pl.BlockSpec((None, sb * chunk, hd), lambda h, i: (h, i, 0))
    sspec = pl.BlockSpec((sb, None, hd, hd), lambda h, i: (i, h, 0, 0))
    return pl.pallas_call(
        functools.partial(_gdn_step_kernel, chunk=chunk),
        grid=(nh, n_dec // sb),
        in_specs=[sspec] + [hspec] * 4 + [pl.BlockSpec((None, sb * chunk, chunk), lambda h, i: (h, i, 0)),
                                         pl.BlockSpec((None, sb, hd), lambda h, i: (h, i, 0))],
        out_specs=[hspec, sspec],
        out_shape=[jax.ShapeDtypeStruct((nh, n_dec * chunk, hd), F32),
                   jax.ShapeDtypeStruct((n_dec, nh, hd, hd), F32)],
        compiler_params=_cparams(("parallel", "parallel")),
        name="gdn_step",
    )(s0, u, w, qg, kd, qk, gl)


def _gdn_post_kernel(o_ref, z_ref, w_ref, out_ref):
    hd = HEAD_DIM
    for h in range(GDN_HEADS):
        o = o_ref[h]
        r = o * lax.rsqrt(jnp.mean(o * o, axis=-1, keepdims=True) + NORM_EPS) * w_ref[...]
        z = z_ref[:, h * hd:(h + 1) * hd]
        out_ref[:, h * hd:(h + 1) * hd] = r * (z * jax.nn.sigmoid(z))


def _gdn_post(o, z, z_col, norm_w, n_batch, seq, tm):
    nt = seq // tm
    return pl.pallas_call(
        _gdn_post_kernel,
        grid=(n_batch, nt),
        in_specs=[pl.BlockSpec((GDN_HEADS, tm, HEAD_DIM), lambda b, i: (b, i, 0)),
                  pl.BlockSpec((tm, 4 * LANES), lambda b, i: (b * nt + i, z_col)),
                  pl.BlockSpec((1, HEAD_DIM), lambda b, i: (0, 0))],
        out_specs=pl.BlockSpec((tm, 4 * LANES), lambda b, i: (b * nt + i, 0)),
        out_shape=jax.ShapeDtypeStruct((n_batch * seq, 4 * LANES), F32),
        compiler_params=_cparams(("parallel", "parallel")),
        name="gdn_post",
    )(o, z, norm_w.reshape(1, HEAD_DIM))


def _gdn_front(x, qkv_col, gate_col, conv_w, a_log, dt_bias, n_batch, seq, chunk, pad_period=0, pad_rows=0):
    rows = n_batch * seq
    beta, gc = _gdn_gates(x, gate_col, rows, a_log, dt_bias, chunk, pad_period, pad_rows)
    (q,) = _gdn_prep(x, qkv_col, conv_w, beta, gc, n_batch, seq, "q")
    k, kb, gcb = _gdn_prep(x, qkv_col + 4, conv_w, beta, gc, n_batch, seq, "k")
    (vb,) = _gdn_prep(x, qkv_col + 8, conv_w, beta, gc, n_batch, seq, "v")
    gc_row = jnp.transpose(gc[:, GDN_HEADS:2 * GDN_HEADS].reshape(n_batch, seq, GDN_HEADS), (0, 2, 1))
    gc_row = gc_row.reshape(n_batch * GDN_HEADS, seq // chunk, chunk)
    return _gdn_intra(q, k, kb, vb, gcb, gc_row, chunk, min(seq, GDN_INTRA_ROWS))


TM = 768
TM_SEG = 512


def _pad_cols(w, n):
    return jnp.pad(w, ((0, 0), (0, n - w.shape[1])))


def _seq_tails(proj, n_batch, seq, n_rows, col, width):
    return jnp.stack([proj[(b + 1) * seq - n_rows:(b + 1) * seq, col:col + width] for b in range(n_batch)])


GDN_DEC_PERIOD = 8
MIX_HALF = N_ATT_HEADS * HEAD_DIM

EVEN_K = MIX_HALF
EVEN_V = EVEN_K + KV_HEADS_A * HEAD_DIM
EVEN_CONV = EVEN_V + KV_HEADS_A * HEAD_DIM
EVEN_Z = EVEN_CONV + GDN_CONV_CH
EVEN_GATE = EVEN_Z + GDN_HEADS * HEAD_DIM
EVEN_COLS = EVEN_GATE + LANES
EVEN_TN = 5 * LANES

ODD_K = MIX_HALF
ODD_V = ODD_K + KV_HEADS_C * HEAD_DIM
ODD_X = ODD_V + KV_HEADS_C * HEAD_DIM
ODD_G = ODD_X + D_RNN
ODD_COLS = ODD_G + D_RNN
ODD_TN = ODD_COLS // 2


def kernel(x_prompt, x_sample, cache_k_moba, cache_v_moba, state_gdn, state_gdn_conv, cache_k_swa, cache_v_swa, state_lru, state_lru_conv, page_table, w_in_even, w_out_even, gdn_conv_w, gdn_a_log, gdn_dt_bias, gdn_norm_w, w_in_odd, w_out_odd, swa_sinks, lru_conv_w, lru_conv_b, lru_w_a, lru_b_a, lru_w_x, lru_b_x, lru_lambda, rel_bias, ln_g, ln_b, router_w, router_b, w_gate, w_up, w_down):
    n_pr, seq, d = x_prompt.shape
    n_dec, dec_seq, _ = x_sample.shape
    tp = n_pr * seq
    ts = n_dec * dec_seq
    n_pool = cache_k_moba.shape[0]
    kva = KV_HEADS_A * HEAD_DIM
    kvc = KV_HEADS_C * HEAD_DIM
    x_segs = [x_prompt.reshape(tp, d), x_sample.reshape(ts, d)]

    rw_pad = _pad_cols(router_w, LANES).T
    rb_pad = router_b.reshape(N_EXPERTS, 1)
    experts = [_group_experts(w_gate[layer], w_up[layer], w_down[layer]) for layer in range(DEPTH)]

    n_hist = CONV_WIDTH - 1
    blk_of = lambda col: col // LANES

    proj = _mm(x_segs, _pad_cols(w_in_even[0], EVEN_COLS).astype(BF16), TM_SEG, EVEN_TN)
    ps = proj[tp:]
    moba_k_prompt = proj[:tp, EVEN_K:EVEN_K + kva].reshape(n_pr, seq, 1, KV_HEADS_A, HEAD_DIM)
    moba_v_prompt = proj[:tp, EVEN_V:EVEN_V + kva].reshape(n_pr, seq, 1, KV_HEADS_A, HEAD_DIM)
    k_new = ps[:, EVEN_K:EVEN_K + kva].reshape(n_dec, dec_seq, kva)
    v_new = ps[:, EVEN_V:EVEN_V + kva].reshape(n_dec, dec_seq, kva)
    moba_k_sample = k_new.reshape(n_dec, dec_seq, 1, KV_HEADS_A, HEAD_DIM)
    moba_v_sample = v_new.reshape(n_dec, dec_seq, 1, KV_HEADS_A, HEAD_DIM)

    oa_p = _moba_prompt(proj, rel_bias, n_pr, seq, 0, blk_of(EVEN_K), blk_of(EVEN_V))
    oa_s = _moba_sample(ps[:, :MIX_HALF].reshape(n_dec, dec_seq, N_ATT_HEADS, HEAD_DIM), k_new, v_new,
                        jnp.swapaxes(cache_k_moba.reshape(n_pool, PAGE_SIZE, -1), 1, 2),
                        jnp.swapaxes(cache_v_moba.reshape(n_pool, PAGE_SIZE, -1), 1, 2),
                        page_table, rel_bias, layer=0)

    parts = _gdn_front(proj, blk_of(EVEN_CONV), blk_of(EVEN_GATE), gdn_conv_w[0], gdn_a_log[0], gdn_dt_bias[0],
                       n_pr, seq, GDN_CHUNK)
    o_p, s_fin = _gdn_scan(*parts, GDN_CHUNK, GDN_HEADS, TM_SEG)
    ob_p = _gdn_post(o_p, proj, EVEN_Z // MIX_HALF, gdn_norm_w[0], n_pr, seq, TM_SEG)
    gdn_state_prompt = s_fin.reshape(1, n_pr, GDN_HEADS, HEAD_DIM, HEAD_DIM)
    gdn_conv_prompt = _seq_tails(proj, n_pr, seq, n_hist, EVEN_CONV, GDN_CONV_CH)[None]

    period = GDN_DEC_PERIOD
    lead = period - dec_seq
    qkv_s = ps[:, EVEN_CONV:EVEN_Z].reshape(n_dec, dec_seq, GDN_CONV_CH)
    x8 = jnp.concatenate([jnp.zeros((n_dec, lead - n_hist, GDN_CONV_CH), F32), state_gdn_conv[0], qkv_s], axis=1)
    tail8 = jnp.pad(ps[:, EVEN_Z:].reshape(n_dec, dec_seq, EVEN_COLS - EVEN_Z), ((0, 0), (lead, 0), (0, 0)))
    xs = jnp.concatenate([x8, tail8], axis=-1).reshape(n_dec * period, EVEN_COLS - EVEN_CONV)
    parts = _gdn_front(xs, 0, blk_of(EVEN_GATE - EVEN_CONV), gdn_conv_w[0], gdn_a_log[0], gdn_dt_bias[0],
                       1, n_dec * period, period, period, lead)
    o_s, gdn_state_sample = _gdn_step(state_gdn[0], *parts, period)
    ob_s = _gdn_post(o_s, xs, (EVEN_Z - EVEN_CONV) // MIX_HALF, gdn_norm_w[0], 1, n_dec * period, TM_SEG)
    ob_s = ob_s.reshape(n_dec, period, MIX_HALF)[:, lead:].reshape(ts, MIX_HALF)
    gdn_conv_sample = qkv_s[None, :, dec_seq - n_hist:]

    h = _mm_ln([oa_p, oa_s.reshape(ts, MIX_HALF)], [ob_p, ob_s], w_out_even[0].astype(BF16), x_segs,
               ln_g[0, 0], ln_b[0, 0], TM_SEG)
    h = _moe_ln(h, rw_pad, rb_pad, *experts[0], ln_g[0, 1], ln_b[0, 1], TM)

    proj = _mm([h], w_in_odd[0].astype(BF16), TM, ODD_TN)
    ps = proj[tp:]
    wa_bd, wx_bd = _block_diag(lru_w_a[0]), _block_diag(lru_w_x[0])
    oc_p = _swa_prompt(proj, rel_bias, swa_sinks[0], n_pr, seq, 0, blk_of(ODD_K), blk_of(ODD_V))
    od_p, lru_p = _lru_prompt(proj, n_pr, seq, blk_of(ODD_X), blk_of(ODD_G), lru_conv_w[0], lru_conv_b[0],
                              wa_bd, lru_b_a[0], wx_bd, lru_b_x[0], lru_lambda[0])
    kv5 = (1, n_pr, WINDOW, KV_HEADS_C, HEAD_DIM)
    swa_k_prompt = _seq_tails(proj, n_pr, seq, WINDOW, ODD_K, kvc).reshape(kv5)
    swa_v_prompt = _seq_tails(proj, n_pr, seq, WINDOW, ODD_V, kvc).reshape(kv5)
    lru_conv_prompt = _seq_tails(proj, n_pr, seq, n_hist, ODD_X, D_RNN)[None]

    nbuf = cache_k_swa.shape[2]
    kc_s = ps[:, ODD_K:ODD_K + kvc].reshape(n_dec, dec_seq, kvc)
    vc_s = ps[:, ODD_V:ODD_V + kvc].reshape(n_dec, dec_seq, kvc)
    buf_k = cache_k_swa[0].reshape(n_dec, nbuf, kvc)
    buf_v = cache_v_swa[0].reshape(n_dec, nbuf, kvc)
    oc_s = _swa_sample(ps[:, :MIX_HALF].reshape(n_dec, dec_seq, N_ATT_HEADS, HEAD_DIM), kc_s, vc_s, buf_k, buf_v,
                       rel_bias, swa_sinks[0])
    keep5 = (1, n_dec, nbuf, KV_HEADS_C, HEAD_DIM)
    swa_k_sample = jnp.concatenate([buf_k, kc_s], axis=1)[:, dec_seq:].reshape(keep5)
    swa_v_sample = jnp.concatenate([buf_v, vc_s], axis=1)[:, dec_seq:].reshape(keep5)
    xr_s = ps[:, ODD_X:ODD_X + D_RNN].reshape(n_dec, dec_seq, D_RNN)
    gr_s = ps[:, ODD_G:ODD_G + D_RNN].reshape(n_dec, dec_seq, D_RNN)
    od_s, lru_s = _lru_sample(xr_s, gr_s, state_lru[0], state_lru_conv[0], lru_conv_w[0], lru_conv_b[0],
                              wa_bd, lru_b_a[0], wx_bd, lru_b_x[0], lru_lambda[0])
    lru_conv_sample = jnp.concatenate([state_lru_conv[0], xr_s], axis=1)[None, :, dec_seq:]

    h = _mm_ln([oc_p, oc_s.reshape(ts, MIX_HALF)], [od_p, od_s.reshape(ts, MIX_HALF)],
               w_out_odd[0].astype(BF16), [h], ln_g[1, 0], ln_b[1, 0], TM_SEG)
    h = _moe_ln(h, rw_pad, rb_pad, *experts[1], ln_g[1, 1], ln_b[1, 1], TM)

    y_prompt = h[:tp].reshape(n_pr, seq, d)
    y_sample = h[tp:].reshape(n_dec, dec_seq, d)
    return (y_prompt, y_sample, moba_k_prompt, moba_v_prompt, moba_k_sample, moba_v_sample,
            gdn_state_prompt, gdn_state_sample[None], gdn_conv_prompt, gdn_conv_sample,
            swa_k_prompt, swa_v_prompt, swa_k_sample, swa_v_sample,
            lru_p[None], lru_s[None], lru_conv_prompt, lru_conv_sample)
```

```python
import functools
import math

import jax
import jax.numpy as jnp
import numpy as np
from jax import lax
from jax.experimental import pallas as pl
from jax.experimental.pallas import tpu as pltpu

F32 = jnp.float32
BF16 = jnp.bfloat16

HEAD_DIM = 64
N_ATT_HEADS = 8
KV_HEADS_A = 4
KV_HEADS_C = 2
MOBA_BLOCK = 256
MOBA_TOPK = 3
PAGE_SIZE = 128
GDN_HEADS = 8
GDN_CHUNK = 64
GDN_CONV_CH = 1536
CONV_WIDTH = 4
WINDOW = 128
D_RNN = 512
LRU_C = 8.0
T5_BUCKETS = 32
T5_MAX_DIST = 128
N_EXPERTS = 16
N_GROUPS = 4
EXPERTS_PER_GROUP = 4
D_EXPERT = 256
DEPTH = 2
DEEPNORM_ALPHA = (2 * DEPTH) ** 0.25
LN_EPS = 1e-5
NORM_EPS = 1e-6

LANES = 128
VMEM_LIMIT = 56 * 1024 * 1024
NEG_INF = float("-inf")


def _cparams(sem):
    return pltpu.CompilerParams(dimension_semantics=sem, vmem_limit_bytes=VMEM_LIMIT)


def _dot(a, b):
    return jnp.dot(a.astype(BF16), b.astype(BF16), preferred_element_type=F32)


def _dot_nt(a, b):
    return lax.dot_general(a.astype(BF16), b.astype(BF16), (((1,), (1,)), ((), ())),
                           preferred_element_type=F32)


def _seg_specs(segs, tm):
    specs, bounds, lo = [], [], 0
    for a in segs:
        assert a.shape[0] % tm == 0
        nt = a.shape[0] // tm
        specs.append(pl.BlockSpec((tm, a.shape[1]), lambda i, lo=lo, nt=nt: (jnp.clip(i - lo, 0, nt - 1), 0)))
        lo += nt
        bounds.append(lo)
    return specs, bounds


def _seg_tile(i, refs, bounds):
    x = refs[-1][...]
    for r, hi in zip(reversed(refs[:-1]), reversed(bounds[:-1])):
        x = jnp.where(i < hi, r[...], x)
    return x


def _mm_kernel(*refs, tn, bounds):
    ns = len(bounds)
    w_ref, o_ref = refs[ns], refs[ns + 1]
    xb = _seg_tile(pl.program_id(0), refs[:ns], bounds).astype(BF16)
    for c in range(w_ref.shape[1] // tn):
        o_ref[:, c * tn:(c + 1) * tn] = jnp.dot(xb, w_ref[:, c * tn:(c + 1) * tn], preferred_element_type=F32)


def _mm(x_segs, w, tm, tn):
    k, n = w.shape
    assert n % tn == 0
    specs, bounds = _seg_specs(x_segs, tm)
    m = bounds[-1] * tm
    return pl.pallas_call(
        functools.partial(_mm_kernel, tn=tn, bounds=tuple(bounds)),
        grid=(bounds[-1],),
        in_specs=specs + [pl.BlockSpec((k, n), lambda i: (0, 0))],
        out_specs=pl.BlockSpec((tm, n), lambda i: (i, 0)),
        out_shape=jax.ShapeDtypeStruct((m, n), F32),
        compiler_params=_cparams(("parallel",)),
        name="in_proj",
    )(*x_segs, w)


def _layer_norm_rows(z, g, b):
    mu = jnp.mean(z, axis=-1, keepdims=True)
    zc = z - mu
    var = jnp.mean(zc * zc, axis=-1, keepdims=True)
    return zc * lax.rsqrt(var + LN_EPS) * g + b


def _mm_ln_kernel(*refs, bounds_a, bounds_b, bounds_h):
    na, nb, nh = len(bounds_a), len(bounds_b), len(bounds_h)
    xa_refs, xb_refs, h_refs = refs[:na], refs[na:na + nb], refs[na + nb:na + nb + nh]
    wa_ref, wb_ref, g_ref, b_ref, o_ref = refs[na + nb + nh:]
    i = pl.program_id(0)
    f = (jnp.dot(_seg_tile(i, xa_refs, bounds_a).astype(BF16), wa_ref[...], preferred_element_type=F32)
         + jnp.dot(_seg_tile(i, xb_refs, bounds_b).astype(BF16), wb_ref[...], preferred_element_type=F32))
    o_ref[...] = _layer_norm_rows(DEEPNORM_ALPHA * _seg_tile(i, h_refs, bounds_h) + f, g_ref[...], b_ref[...])


def _mm_ln(xa_segs, xb_segs, w, h_segs, g, b, tm):
    ka, kb = xa_segs[0].shape[1], xb_segs[0].shape[1]
    n = w.shape[1]
    assert ka == kb
    sa, ba = _seg_specs(xa_segs, tm)
    sb, bb = _seg_specs(xb_segs, tm)
    sh, bh = _seg_specs(h_segs, tm)
    assert ba[-1] == bb[-1] == bh[-1]
    return pl.pallas_call(
        functools.partial(_mm_ln_kernel, bounds_a=tuple(ba), bounds_b=tuple(bb), bounds_h=tuple(bh)),
        grid=(ba[-1],),
        in_specs=sa + sb + sh + [pl.BlockSpec((ka, n), lambda i: (0, 0)),
                                 pl.BlockSpec((kb, n), lambda i: (1, 0)),
                                 pl.BlockSpec((1, n), lambda i: (0, 0)),
                                 pl.BlockSpec((1, n), lambda i: (0, 0))],
        out_specs=pl.BlockSpec((tm, n), lambda i: (i, 0)),
        out_shape=jax.ShapeDtypeStruct((ba[-1] * tm, n), F32),
        compiler_params=_cparams(("parallel",)),
        name="out_proj_ln",
    )(*xa_segs, *xb_segs, *h_segs, w, w, g.reshape(1, n), b.reshape(1, n))


def _moe_ln_kernel(t_ref, rw_ref, rb_ref, wg_ref, wu_ref, wd_ref, g_ref, b_ref, o_ref,
                   gates_ref, tb_ref, hb_ref, acc_ref):
    e = pl.program_id(1)

    @pl.when(e == 0)
    def _route():
        t = t_ref[...]
        logits = _dot_nt(rw_ref[...], t)[:N_EXPERTS]
        s = jax.nn.sigmoid(logits)
        sb = s + rb_ref[...]
        row = lax.broadcasted_iota(jnp.int32, sb.shape, 0)
        grp = row // EXPERTS_PER_GROUP

        def max_first(vals):
            mx = jnp.max(vals, axis=0, keepdims=True)
            return mx, jnp.min(jnp.where(vals == mx, row, N_EXPERTS), axis=0, keepdims=True)

        best = None
        gsel = None
        for gi in range(N_GROUPS):
            vals = jnp.where(grp == gi, sb, NEG_INF)
            m1, i1 = max_first(vals)
            m2 = jnp.max(jnp.where(row == i1, NEG_INF, vals), axis=0, keepdims=True)
            score = m1 + m2
            if gi == 0:
                best, gsel = score, jnp.zeros_like(i1)
            else:
                upd = score > best
                best = jnp.where(upd, score, best)
                gsel = jnp.where(upd, gi, gsel)
        vals = jnp.where(grp == gsel, sb, NEG_INF)
        _, i1 = max_first(vals)
        _, i2 = max_first(jnp.where(row == i1, NEG_INF, vals))
        s1 = jnp.sum(jnp.where(row == i1, s, 0.0), axis=0, keepdims=True)
        s2 = jnp.sum(jnp.where(row == i2, s, 0.0), axis=0, keepdims=True)
        den = s1 + s2
        gates_t = jnp.where(row == i1, s1 / den, 0.0) + jnp.where(row == i2, s2 / den, 0.0)
        gates_t = jnp.concatenate([gates_t, jnp.zeros((LANES - N_EXPERTS, gates_t.shape[1]), F32)], axis=0)
        gates_ref[...] = gates_t.T
        tb_ref[...] = t.astype(BF16)
        acc_ref[...] = jnp.zeros_like(acc_ref)

    tb = tb_ref[...]
    gates = gates_ref[...]
    lane = lax.broadcasted_iota(jnp.int32, gates.shape, 1)
    for j in range(EXPERTS_PER_GROUP):
        sl = slice(j * D_EXPERT, (j + 1) * D_EXPERT)
        ge = jnp.sum(jnp.where(lane == e * EXPERTS_PER_GROUP + j, gates, 0.0), axis=1, keepdims=True)
        g = jnp.dot(tb, wg_ref[j], preferred_element_type=F32)
        u = jnp.dot(tb, wu_ref[j], preferred_element_type=F32)
        hb_ref[:, sl] = ((g * jax.nn.sigmoid(g)) * u * ge).astype(BF16)
    acc_ref[...] += jnp.dot(hb_ref[...], wd_ref[...], preferred_element_type=F32)

    @pl.when(e == pl.num_programs(1) - 1)
    def _finish():
        o_ref[...] = _layer_norm_rows(DEEPNORM_ALPHA * t_ref[...] + acc_ref[...], g_ref[...], b_ref[...])


def _group_experts(w_gate, w_up, w_down):
    ne, d, f = w_gate.shape
    es = EXPERTS_PER_GROUP
    return w_gate.astype(BF16), w_up.astype(BF16), w_down.astype(BF16).reshape(ne // es, es * f, d)


def _moe_ln(t, router_w_pad, router_b_pad, wg, wu, wd, g, b, tm):
    m, d = t.shape
    ng, gf, _ = wd.shape
    es, f = EXPERTS_PER_GROUP, wg.shape[2]
    assert m % tm == 0
    return pl.pallas_call(
        _moe_ln_kernel,
        grid=(m // tm, ng),
        in_specs=[pl.BlockSpec((tm, d), lambda i, e: (i, 0)),
                  pl.BlockSpec((LANES, d), lambda i, e: (0, 0)),
                  pl.BlockSpec((N_EXPERTS, 1), lambda i, e: (0, 0)),
                  pl.BlockSpec((es, d, f), lambda i, e: (e, 0, 0)),
                  pl.BlockSpec((es, d, f), lambda i, e: (e, 0, 0)),
                  pl.BlockSpec((None, gf, d), lambda i, e: (e, 0, 0)),
                  pl.BlockSpec((1, d), lambda i, e: (0, 0)),
                  pl.BlockSpec((1, d), lambda i, e: (0, 0))],
        out_specs=pl.BlockSpec((tm, d), lambda i, e: (i, 0)),
        out_shape=jax.ShapeDtypeStruct((m, d), F32),
        scratch_shapes=[pltpu.VMEM((tm, LANES), F32),
                        pltpu.VMEM((tm, d), BF16),
                        pltpu.VMEM((tm, gf), BF16),
                        pltpu.VMEM((tm, d), F32)],
        compiler_params=_cparams(("parallel", "arbitrary")),
        name="moe_ln",
    )(t, router_w_pad, router_b_pad, wg, wu, wd, g.reshape(1, d), b.reshape(1, d))


def _t5_bucket_np(rel):
    n = np.maximum(rel, 0)
    max_exact = T5_BUCKETS // 2
    nf = np.maximum(n, 1).astype(np.float32)
    large = max_exact + (np.log(nf / np.float32(max_exact)) / np.float32(math.log(T5_MAX_DIST / max_exact))
                         * np.float32(T5_BUCKETS - max_exact)).astype(np.int32)
    large = np.minimum(large, T5_BUCKETS - 1)
    return np.where(n < max_exact, n, large).astype(np.int32)


def _t5_thresholds():
    b = _t5_bucket_np(np.arange(4 * T5_MAX_DIST))
    return [int(np.argmax(b >= k)) for k in range(1, T5_BUCKETS)]


def _bias_tile_kernel(rb_ref, o_ref, *, off, sign, lo, hi):
    h = pl.program_id(0)
    row = lax.broadcasted_iota(jnp.int32, o_ref.shape, 0)
    col = lax.broadcasted_iota(jnp.int32, o_ref.shape, 1)
    rel = off + sign * (row - col)
    t = jnp.full(o_ref.shape, rb_ref[0, h], F32)
    for k, thr in enumerate(_t5_thresholds(), start=1):
        t = jnp.where(rel >= thr, rb_ref[k, h], t)
    if lo is not None:
        t = jnp.where((rel >= lo) & (rel <= hi), t, NEG_INF)
    o_ref[...] = t


def _bias_tiles(rel_bias, rows, cols, off, sign=1, lo=None, hi=None):
    return pl.pallas_call(
        functools.partial(_bias_tile_kernel, off=off, sign=sign, lo=lo, hi=hi),
        grid=(N_ATT_HEADS,),
        in_specs=[pl.BlockSpec(memory_space=pltpu.SMEM)],
        out_specs=pl.BlockSpec((None, rows, cols), lambda h: (h, 0, 0)),
        out_shape=jax.ShapeDtypeStruct((N_ATT_HEADS, rows, cols), F32),
        compiler_params=_cparams(("parallel",)),
        name="bias_tiles",
    )(rel_bias)


def _moba_prompt_kernel(q_ref, k_ref, v_ref, bd_ref, bp_ref, bf_ref, o_ref, kmean_ref, vt_ref, sel_ref):
    i = pl.program_id(2)
    blk = MOBA_BLOCK
    nblk = k_ref.shape[0] // blk
    hd = HEAD_DIM

    @pl.when(i == 0)
    def _per_sequence():
        for n in range(nblk):
            kmean_ref[n:n + 1, :] = jnp.mean(k_ref[n * blk:(n + 1) * blk, :], axis=0, keepdims=True)
            vt_ref[n] = v_ref[n * blk:(n + 1) * blk, :].T.astype(BF16)

    q4 = q_ref[...] * (hd ** -0.5)
    q2b = []
    for kvl in range(2):
        lo = kvl * 2 * hd
        q2 = jnp.concatenate([q4[:, lo:lo + hd], q4[:, lo + hd:lo + 2 * hd]], axis=0)
        q2b.append(q2.astype(BF16))
        gate = _dot_nt(kmean_ref[:, kvl * hd:(kvl + 1) * hd], q2)
        row = lax.broadcasted_iota(jnp.int32, gate.shape, 0)
        gate = jnp.where(row < i, gate, NEG_INF)
        sel = jnp.zeros(gate.shape, F32)
        for _ in range(MOBA_TOPK):
            mx = jnp.max(gate, axis=0, keepdims=True)
            idx = jnp.min(jnp.where(gate == mx, row, nblk), axis=0, keepdims=True)
            hit = (row == idx) & (mx > NEG_INF)
            sel = jnp.where(hit, 1.0, sel)
            gate = jnp.where(row == idx, NEG_INF, gate)
        sel_ref[kvl] = sel

    def scores(j, kvl):
        r0 = pl.multiple_of(j * blk, blk)
        kb = k_ref[pl.ds(r0, blk), :][:, kvl * hd:(kvl + 1) * hd].astype(BF16)
        return _dot_nt(kb, q2b[kvl])

    def update(state, s, j, kvl):
        m_old, l_old, acc = state
        m_new = jnp.maximum(m_old, jnp.max(s, axis=0, keepdims=True))
        alpha = jnp.exp(m_old - m_new)
        p = jnp.exp(s - m_new)
        l_new = alpha * l_old + jnp.sum(p, axis=0, keepdims=True)
        vt = vt_ref[j][kvl * hd:(kvl + 1) * hd, :]
        acc = alpha * acc + jnp.dot(vt, p.astype(BF16), preferred_element_type=F32)
        return m_new, l_new, acc

    def live_row(j, kvl):
        return sel_ref[kvl, pl.ds(j, 1), :] > 0.0

    init = (jnp.full((1, 2 * blk), NEG_INF, F32), jnp.zeros((1, 2 * blk), F32), jnp.zeros((hd, 2 * blk), F32))
    jp = jnp.maximum(i - 1, 0)
    states = []
    for kvl in range(2):
        st = update(init, scores(i, kvl) + bd_ref[kvl], i, kvl)
        s = jnp.where(live_row(jp, kvl), scores(jp, kvl) + bp_ref[kvl], NEG_INF)
        states.append(update(st, s, jp, kvl))

    n_pairs = i // 2

    def far_pair(state, pr, kvl):
        j0 = 2 * pr
        r0 = pl.multiple_of(j0 * blk, 2 * blk)
        second_far = j0 + 1 < i - 1
        kb = k_ref[pl.ds(r0, 2 * blk), :][:, kvl * hd:(kvl + 1) * hd].astype(BF16)
        s = _dot_nt(kb, q2b[kvl])
        far_bias = bf_ref[kvl]
        s0 = s[:blk] + jnp.where(live_row(j0, kvl), far_bias, NEG_INF)
        s1 = s[blk:] + jnp.where(live_row(j0 + 1, kvl) & second_far, far_bias, NEG_INF)
        m_old, l_old, acc = state
        m_new = jnp.maximum(m_old, jnp.maximum(jnp.max(s0, axis=0, keepdims=True),
                                               jnp.max(s1, axis=0, keepdims=True)))
        alpha = jnp.exp(m_old - m_new)
        p = jnp.concatenate([jnp.exp(s0 - m_new), jnp.exp(s1 - m_new)], axis=0)
        l_new = alpha * l_old + jnp.sum(p, axis=0, keepdims=True)
        rows = slice(kvl * hd, (kvl + 1) * hd)
        vt = jnp.concatenate([vt_ref[j0][rows, :], vt_ref[j0 + 1][rows, :]], axis=1)
        acc = alpha * acc + jnp.dot(vt, p.astype(BF16), preferred_element_type=F32)
        return m_new, l_new, acc

    def far(pr, carry):
        return tuple(far_pair(carry[kvl], pr, kvl) for kvl in range(2))

    states = lax.fori_loop(0, n_pairs, far, tuple(states))
    ot = jnp.concatenate([acc / l for (_, l, acc) in states], axis=0)
    o = ot.T
    for kvl in range(2):
        for hh in range(2):
            o_ref[:, (2 * kvl + hh) * hd:(2 * kvl + hh + 1) * hd] = o[hh * blk:(hh + 1) * blk, kvl * hd:(kvl + 1) * hd]


def _moba_prompt(proj, bias_rel, n_batch, seq, q_col, k_col, v_col):
    blk = MOBA_BLOCK
    nblk = seq // blk

    def pair_up(t):
        return jnp.transpose(t.reshape(KV_HEADS_A, 2, blk, blk), (0, 2, 1, 3)).reshape(KV_HEADS_A, blk, 2 * blk)

    diag = pair_up(_bias_tiles(bias_rel, blk, blk, 0, sign=-1, lo=0, hi=2 * blk))
    prev = pair_up(_bias_tiles(bias_rel, blk, blk, blk, sign=-1))
    far = jnp.broadcast_to(bias_rel[T5_BUCKETS - 1].reshape(KV_HEADS_A, 1, 2, 1), (KV_HEADS_A, 1, 2, blk))
    far = far.reshape(KV_HEADS_A, 1, 2 * blk)
    rows_per_b = seq // blk
    return pl.pallas_call(
        _moba_prompt_kernel,
        grid=(n_batch, 2, nblk),
        in_specs=[pl.BlockSpec((blk, 2 * LANES), lambda b, p, i: (b * rows_per_b + i, q_col // 2 + p)),
                  pl.BlockSpec((seq, LANES), lambda b, p, i: (b, k_col + p)),
                  pl.BlockSpec((seq, LANES), lambda b, p, i: (b, v_col + p)),
                  pl.BlockSpec((2, blk, 2 * blk), lambda b, p, i: (p, 0, 0)),
                  pl.BlockSpec((2, blk, 2 * blk), lambda b, p, i: (p, 0, 0)),
                  pl.BlockSpec((2, 1, 2 * blk), lambda b, p, i: (p, 0, 0))],
        out_specs=pl.BlockSpec((blk, 2 * LANES), lambda b, p, i: (b * rows_per_b + i, p)),
        out_shape=jax.ShapeDtypeStruct((n_batch * seq, 4 * LANES), F32),
        scratch_shapes=[pltpu.VMEM((nblk, LANES), F32),
                        pltpu.VMEM((nblk, LANES, blk), BF16),
                        pltpu.VMEM((2, nblk, 2 * blk), F32)],
        compiler_params=_cparams(("parallel", "parallel", "arbitrary")),
        name="moba_prompt",
    )(proj, proj, proj, diag, prev, far)


MOBA_DEC_BLOCKS_PER_STEP = 16
PAGES_PER_BLOCK = MOBA_BLOCK // PAGE_SIZE


MOBA_DEC_SLOTS = 3


def _moba_sample_kernel(pt_ref, q_ref, kn_ref, vn_ref, b31_ref, bfar_ref, bown_ref, ck_hbm, cv_hbm, o_ref,
                        g_ref, m_ref, l_ref, part_ref, kbuf, vbuf, sem, *, n_blocks, nb, layer):
    npg = nb * PAGES_PER_BLOCK
    kv_w = kbuf.shape[2]
    b, s = pl.program_id(0), pl.program_id(1)
    n_steps = pl.num_programs(1)
    total = pl.num_programs(0) * n_steps
    g = b * n_steps + s
    q = q_ref[...]
    qb = q.astype(BF16)
    lane = lax.broadcasted_iota(jnp.int32, g_ref.shape, 1)

    def page_copies(step):
        bb = step // n_steps
        ss = step - bb * n_steps
        slot = lax.rem(step, MOBA_DEC_SLOTS)
        out = []
        for k in range(npg):
            page = pt_ref[bb, ss * npg + k]
            for hbm, buf in ((ck_hbm, kbuf), (cv_hbm, vbuf)):
                out.append(pltpu.make_async_copy(hbm.at[page, pl.ds(layer * kv_w, kv_w), :],
                                                 buf.at[slot, k], sem.at[slot]))
        return out

    @pl.when(g == 0)
    def _prime():
        for c in page_copies(g):
            c.start()

    @pl.when((g == 0) & (total > 1))
    def _prime_next():
        for c in page_copies(g + 1):
            c.start()

    @pl.when(s == 0)
    def _init():
        g_ref[...] = jnp.zeros_like(g_ref)
        m_ref[...] = jnp.zeros_like(m_ref)
        l_ref[...] = jnp.zeros_like(l_ref)

    for c in page_copies(g):
        c.wait()
    slot = lax.rem(g, MOBA_DEC_SLOTS)
    lane_k = lax.broadcasted_iota(jnp.int32, (kv_w, LANES), 1)
    kmean = jnp.zeros((kv_w, LANES), F32)

    for j in range(nb):
        n = s * nb + j
        kts = [kbuf[slot, j * PAGES_PER_BLOCK + t] for t in range(PAGES_PER_BLOCK)]
        vts = [vbuf[slot, j * PAGES_PER_BLOCK + t] for t in range(PAGES_PER_BLOCK)]
        sc = jnp.concatenate([_dot(qb, kt) for kt in kts], axis=1)
        ksum = kts[0].sum(axis=1, keepdims=True)
        for kt in kts[1:]:
            ksum = ksum + kt.sum(axis=1, keepdims=True)
        kmean = kmean + jnp.where(lane_k == n, ksum * (1.0 / MOBA_BLOCK), 0.0)
        sc = sc + jnp.where(n == n_blocks - 1, b31_ref[...], bfar_ref[...])
        mx = jnp.max(sc, axis=1, keepdims=True)
        p = jnp.exp(sc - mx)
        hit = lane == n
        m_ref[...] = jnp.where(hit, mx, m_ref[...])
        l_ref[...] = jnp.where(hit, jnp.sum(p, axis=1, keepdims=True), l_ref[...])
        o = None
        for t, vt in enumerate(vts):
            part = _dot_nt(p[:, t * PAGE_SIZE:(t + 1) * PAGE_SIZE], vt)
            o = part if o is None else o + part
        part_ref[n] = o
    g_ref[...] += _dot(qb, kmean)

    @pl.when(g + 2 < total)
    def _prefetch():
        for c in page_copies(g + 2):
            c.start()

    @pl.when(s == pl.num_programs(1) - 1)
    def _merge():
        gate = jnp.where(lane < n_blocks, g_ref[...], NEG_INF)
        sel = jnp.zeros(gate.shape, jnp.bool_)
        for _ in range(MOBA_TOPK):
            mxg = jnp.max(gate, axis=1, keepdims=True)
            idx = jnp.min(jnp.where(gate == mxg, lane, LANES), axis=1, keepdims=True)
            hit = (lane == idx) & (mxg > NEG_INF)
            sel = sel | hit
            gate = jnp.where(lane == idx, NEG_INF, gate)
        s_own = _dot_nt(qb, kn_ref[...]) + bown_ref[...]
        m_sel = jnp.where(sel, m_ref[...], NEG_INF)
        m_all = jnp.maximum(jnp.max(m_sel, axis=1, keepdims=True), jnp.max(s_own, axis=1, keepdims=True))
        w = jnp.where(sel, jnp.exp(m_sel - m_all), 0.0)
        p_own = jnp.exp(s_own - m_all)
        den = jnp.sum(w * l_ref[...], axis=1, keepdims=True) + jnp.sum(p_own, axis=1, keepdims=True)
        acc = _dot(p_own, vn_ref[...])
        for n in range(n_blocks):
            acc = acc + w[:, n:n + 1] * part_ref[n]
        o_ref[...] = acc / den


def _moba_sample(q_s, k_new, v_new, cache_k, cache_v, page_table, bias_rel, layer=0):
    n_dec, t_dec = q_s.shape[:2]
    n_pages = page_table.shape[1]
    n_blocks = n_pages // PAGES_PER_BLOCK
    nb = min(MOBA_DEC_BLOCKS_PER_STEP, n_blocks)
    npg = nb * PAGES_PER_BLOCK
    assert n_blocks % nb == 0 and n_blocks <= LANES and t_dec <= 8
    rows = N_ATT_HEADS * t_dec
    kv_w = KV_HEADS_A * HEAD_DIM
    head_kv = np.arange(N_ATT_HEADS) // (N_ATT_HEADS // KV_HEADS_A)
    qh = jnp.transpose(q_s, (0, 2, 1, 3)) * (HEAD_DIM ** -0.5)
    onehot = jnp.asarray(np.eye(KV_HEADS_A, dtype=np.float32)[head_kv])
    qbd = (qh[:, :, :, None, :] * onehot[None, :, None, :, None]).reshape(n_dec, rows, kv_w)
    pad = 8 - t_dec
    kn = jnp.pad(k_new, ((0, 0), (0, pad), (0, 0)))
    vn = jnp.pad(v_new, ((0, 0), (0, pad), (0, 0)))
    bt = bias_rel.T
    past = n_pages * PAGE_SIZE
    tq = np.arange(t_dec)[None, :, None]
    ik = np.arange(MOBA_BLOCK)[None, None, :]
    idx31 = _t5_bucket_np(past + tq - ((n_blocks - 1) * MOBA_BLOCK + ik))[0]
    b31 = bt[:, idx31].reshape(rows, MOBA_BLOCK)
    bfar = jnp.broadcast_to(bt[:, T5_BUCKETS - 1][:, None, None], (N_ATT_HEADS, t_dec, 1)).reshape(rows, 1)
    tk = np.arange(8)[None, :]
    rel_own = np.arange(t_dec)[:, None] - tk
    ok_own = (rel_own >= 0) & (tk < t_dec)
    bown = jnp.where(jnp.asarray(ok_own)[None], bt[:, _t5_bucket_np(rel_own)], NEG_INF).reshape(rows, 8)

    const2 = lambda b, s, pt: (0, 0)
    grid_spec = pltpu.PrefetchScalarGridSpec(
        num_scalar_prefetch=1,
        grid=(n_dec, n_blocks // nb),
        in_specs=[pl.BlockSpec((None, rows, kv_w), lambda b, s, pt: (b, 0, 0)),
                  pl.BlockSpec((None, 8, kv_w), lambda b, s, pt: (b, 0, 0)),
                  pl.BlockSpec((None, 8, kv_w), lambda b, s, pt: (b, 0, 0)),
                  pl.BlockSpec((rows, MOBA_BLOCK), const2),
                  pl.BlockSpec((rows, 1), const2),
                  pl.BlockSpec((rows, 8), const2),
                  pl.BlockSpec(memory_space=pl.ANY),
                  pl.BlockSpec(memory_space=pl.ANY)],
        out_specs=pl.BlockSpec((None, rows, kv_w), lambda b, s, pt: (b, 0, 0)),
        scratch_shapes=[pltpu.VMEM((rows, LANES), F32),
                        pltpu.VMEM((rows, LANES), F32),
                        pltpu.VMEM((rows, LANES), F32),
                        pltpu.VMEM((n_blocks, rows, kv_w), F32),
                        pltpu.VMEM((MOBA_DEC_SLOTS, npg, kv_w, PAGE_SIZE), F32),
                        pltpu.VMEM((MOBA_DEC_SLOTS, npg, kv_w, PAGE_SIZE), F32),
                        pltpu.SemaphoreType.DMA((MOBA_DEC_SLOTS,))],
    )
    out = pl.pallas_call(
        functools.partial(_moba_sample_kernel, n_blocks=n_blocks, nb=nb, layer=layer),
        grid_spec=grid_spec,
        out_shape=jax.ShapeDtypeStruct((n_dec, rows, kv_w), F32),
        compiler_params=_cparams(("arbitrary", "arbitrary")),
        name="moba_sample",
    )(page_table, qbd, kn, vn, b31, bfar, bown, cache_k, cache_v)
    out = out.reshape(n_dec, N_ATT_HEADS, t_dec, KV_HEADS_A, HEAD_DIM)
    out = out[:, np.arange(N_ATT_HEADS), :, head_kv, :]
    return jnp.transpose(out, (1, 2, 0, 3)).reshape(n_dec, t_dec, N_ATT_HEADS * HEAD_DIM)


SWA_GROUP = N_ATT_HEADS // KV_HEADS_C


def _sink_softmax_pv(parts, sink):
    m = sink
    for s, _ in parts:
        m = jnp.maximum(m, jnp.max(s, axis=1, keepdims=True))
    den = jnp.exp(sink - m)
    ps = []
    for s, _ in parts:
        p = jnp.exp(s - m)
        den = den + jnp.sum(p, axis=1, keepdims=True)
        ps.append(p)
    o = None
    for p, (_, v) in zip(ps, parts):
        t = _dot(p / den, v)
        o = t if o is None else o + t
    return o


def _swa_prompt_kernel(q_ref, kp_ref, ko_ref, vp_ref, vo_ref, bias_ref, sink_ref, o_ref):
    i = pl.program_id(1)
    w, hd, g = WINDOW, HEAD_DIM, SWA_GROUP
    q = q_ref[...] * (hd ** -0.5)
    kcat = jnp.concatenate([kp_ref[...], ko_ref[...]], axis=0)
    vt = jnp.concatenate([vp_ref[...], vo_ref[...]], axis=0).T.astype(BF16)
    row = lax.broadcasted_iota(jnp.int32, (2 * w, g * w), 0)
    has_key = (row >= w) | (i > 0)
    ots = []
    for c in range(KV_HEADS_C):
        qs = jnp.concatenate([q[:, (c * g + j) * hd:(c * g + j + 1) * hd] for j in range(g)], axis=0)
        s = _dot_nt(kcat[:, c * hd:(c + 1) * hd], qs) + bias_ref[c]
        s = jnp.where(has_key, s, NEG_INF)
        sink = sink_ref[c]
        m = jnp.maximum(sink, jnp.max(s, axis=0, keepdims=True))
        p = jnp.exp(s - m)
        den = jnp.sum(p, axis=0, keepdims=True) + jnp.exp(sink - m)
        ots.append(jnp.dot(vt[c * hd:(c + 1) * hd, :], (p / den).astype(BF16), preferred_element_type=F32))
    o = jnp.concatenate(ots, axis=0).T
    for c in range(KV_HEADS_C):
        for j in range(g):
            o_ref[:, (c * g + j) * hd:(c * g + j + 1) * hd] = o[j * w:(j + 1) * w, c * hd:(c + 1) * hd]


def _swa_prompt(proj, bias_rel, sinks, n_batch, seq, q_col, k_col, v_col):
    w, g = WINDOW, SWA_GROUP
    nb = seq // w
    band = _bias_tiles(bias_rel, 2 * w, w, w, sign=-1, lo=0, hi=w)
    band = jnp.transpose(band.reshape(KV_HEADS_C, g, 2 * w, w), (0, 2, 1, 3)).reshape(KV_HEADS_C, 2 * w, g * w)
    sink = jnp.broadcast_to(sinks.reshape(KV_HEADS_C, 1, g, 1), (KV_HEADS_C, 1, g, w)).reshape(KV_HEADS_C, 1, g * w)
    own = lambda c: (lambda b, i: (b * nb + i, c))
    prev = lambda c: (lambda b, i: (b * nb + jnp.maximum(i - 1, 0), c))
    return pl.pallas_call(
        _swa_prompt_kernel,
        grid=(n_batch, nb),
        in_specs=[pl.BlockSpec((w, 4 * LANES), lambda b, i: (b * nb + i, q_col // 4)),
                  pl.BlockSpec((w, LANES), prev(k_col)),
                  pl.BlockSpec((w, LANES), own(k_col)),
                  pl.BlockSpec((w, LANES), prev(v_col)),
                  pl.BlockSpec((w, LANES), own(v_col)),
                  pl.BlockSpec((KV_HEADS_C, 2 * w, g * w), lambda b, i: (0, 0, 0)),
                  pl.BlockSpec((KV_HEADS_C, 1, g * w), lambda b, i: (0, 0, 0))],
        out_specs=pl.BlockSpec((w, 4 * LANES), lambda b, i: (b * nb + i, 0)),
        out_shape=jax.ShapeDtypeStruct((n_batch * seq, 4 * LANES), F32),
        compiler_params=_cparams(("parallel", "parallel")),
        name="swa_prompt",
    )(proj, proj, proj, proj, proj, band, sink)


SWA_DEC_SEQS_PER_STEP = 8


def _swa_sample_kernel(q_ref, kb_ref, vb_ref, kn_ref, vn_ref, bbuf_ref, bnew_ref, sink_ref, o_ref):
    for i in range(SWA_DEC_SEQS_PER_STEP):
        q = q_ref[i]
        s_buf = _dot_nt(q, kb_ref[i]) + bbuf_ref[...]
        s_new = _dot_nt(q, kn_ref[i]) + bnew_ref[...]
        o_ref[i] = _sink_softmax_pv([(s_buf, vb_ref[i]), (s_new, vn_ref[i])], sink_ref[...])


def _swa_sample(q_s, k_new, v_new, buf_k, buf_v, bias_rel, sinks):
    n_dec, t_dec = q_s.shape[:2]
    nbuf = buf_k.shape[1]
    rows = N_ATT_HEADS * t_dec
    kv_w = KV_HEADS_C * HEAD_DIM
    sb = SWA_DEC_SEQS_PER_STEP
    assert n_dec % sb == 0 and t_dec <= 8
    head_kv = np.arange(N_ATT_HEADS) // SWA_GROUP
    qh = jnp.transpose(q_s, (0, 2, 1, 3)) * (HEAD_DIM ** -0.5)
    onehot = jnp.asarray(np.eye(KV_HEADS_C, dtype=np.float32)[head_kv])
    qbd = (qh[:, :, :, None, :] * onehot[None, :, None, :, None]).reshape(n_dec, rows, kv_w)
    pad = 8 - t_dec
    kn = jnp.pad(k_new, ((0, 0), (0, pad), (0, 0)))
    vn = jnp.pad(v_new, ((0, 0), (0, pad), (0, 0)))
    bt = bias_rel.T
    t = np.arange(t_dec)[:, None]
    rel_buf = nbuf + t - np.arange(nbuf)[None, :]
    ok_buf = (rel_buf >= 0) & (rel_buf <= WINDOW)
    bbuf = jnp.where(jnp.asarray(ok_buf)[None], bt[:, _t5_bucket_np(rel_buf)], NEG_INF).reshape(rows, nbuf)
    tk = np.arange(8)[None, :]
    rel_new = t - tk
    ok_new = (rel_new >= 0) & (tk < t_dec)
    bnew = jnp.where(jnp.asarray(ok_new)[None], bt[:, _t5_bucket_np(rel_new)], NEG_INF).reshape(rows, 8)
    sink = jnp.broadcast_to(sinks[:, None, None], (N_ATT_HEADS, t_dec, 1)).reshape(rows, 1)
    seq3 = lambda r, c: pl.BlockSpec((sb, r, c), lambda i: (i, 0, 0))
    const2 = lambda r, c: pl.BlockSpec((r, c), lambda i: (0, 0))
    out = pl.pallas_call(
        _swa_sample_kernel,
        grid=(n_dec // sb,),
        in_specs=[seq3(rows, kv_w), seq3(nbuf, kv_w), seq3(nbuf, kv_w), seq3(8, kv_w), seq3(8, kv_w),
                  const2(rows, nbuf), const2(rows, 8), const2(rows, 1)],
        out_specs=seq3(rows, kv_w),
        out_shape=jax.ShapeDtypeStruct((n_dec, rows, kv_w), F32),
        compiler_params=_cparams(("parallel",)),
        name="swa_sample",
    )(qbd, buf_k, buf_v, kn, vn, bbuf, bnew, sink)
    out = out.reshape(n_dec, N_ATT_HEADS, t_dec, KV_HEADS_C, HEAD_DIM)
    out = out[:, np.arange(N_ATT_HEADS), :, head_kv, :]
    return jnp.transpose(out, (1, 2, 0, 3)).reshape(n_dec, t_dec, N_ATT_HEADS * HEAD_DIM)


def _softplus(x):
    return jnp.maximum(x, 0.0) + jnp.log1p(jnp.exp(-jnp.abs(x)))


def _neg_expm1(y):
    return -jnp.tanh(0.5 * y) * (jnp.exp(y) + 1.0)


def _shifted_rows(prev8, cur, k):
    if k == 0:
        return cur
    ext = jnp.concatenate([prev8, cur], axis=0)
    return pltpu.roll(ext, k, axis=0)[8:]


def _causal_conv_rows(prev8, cur, w):
    y = cur * w[CONV_WIDTH - 1:CONV_WIDTH, :]
    for k in range(1, CONV_WIDTH):
        y = y + _shifted_rows(prev8, cur, k) * w[CONV_WIDTH - 1 - k:CONV_WIDTH - k, :]
    return y


def _lru_gates(xc, wa, ba, wx, bx, sp):
    r = jax.nn.sigmoid(_dot(xc, wa) + ba)
    i = jax.nn.sigmoid(_dot(xc, wx) + bx)
    log_a = -LRU_C * r * sp
    a = jnp.exp(log_a)
    b = jnp.sqrt(_neg_expm1(2.0 * log_a)) * (i * xc)
    return a, b


LRU_ROWS = 256


def _lru_prompt_kernel(x_ref, gr_ref, cw_ref, cb_ref, wa_ref, ba_ref, wx_ref, bx_ref, lam_ref,
                       o_ref, hl_ref):
    n = x_ref.shape[0] // LRU_ROWS
    sp = _softplus(-lam_ref[...])
    row = lax.broadcasted_iota(jnp.int32, (LRU_ROWS, LANES), 0)

    def body(c, carry):
        prev8, h = carry
        r0 = pl.multiple_of(c * LRU_ROWS, LRU_ROWS)
        cur = x_ref[pl.ds(r0, LRU_ROWS), :]
        xc = _causal_conv_rows(prev8, cur, cw_ref[...]) + cb_ref[...]
        a, b = _lru_gates(xc, wa_ref[...], ba_ref[...], wx_ref[...], bx_ref[...], sp)
        d = 1
        while d < LRU_ROWS:
            a_s = jnp.where(row >= d, pltpu.roll(a, d, axis=0), 1.0)
            b_s = jnp.where(row >= d, pltpu.roll(b, d, axis=0), 0.0)
            b = a * b_s + b
            a = a * a_s
            d *= 2
        hs = a * h + b
        o_ref[pl.ds(r0, LRU_ROWS), :] = hs * jax.nn.gelu(gr_ref[pl.ds(r0, LRU_ROWS), :])
        return cur[LRU_ROWS - 8:], hs[LRU_ROWS - 1:]

    _, h = lax.fori_loop(0, n, body, (jnp.zeros((8, LANES), F32), jnp.zeros((1, LANES), F32)))
    hl_ref[...] = h


def _lru_prompt(proj, n_batch, seq, x_col, g_col, conv_w, conv_b, wa_bd, b_a, wx_bd, b_x, lam):
    ncb = D_RNN // LANES
    vec = lambda v: v.reshape(1, D_RNN)
    vspec = pl.BlockSpec((1, LANES), lambda b, c: (0, c))
    out, hl = pl.pallas_call(
        _lru_prompt_kernel,
        grid=(n_batch, ncb),
        in_specs=[pl.BlockSpec((seq, LANES), lambda b, c: (b, x_col + c)),
                  pl.BlockSpec((seq, LANES), lambda b, c: (b, g_col + c)),
                  pl.BlockSpec((CONV_WIDTH, LANES), lambda b, c: (0, c)),
                  vspec,
                  pl.BlockSpec((LANES, LANES), lambda b, c: (c, c)), vspec,
                  pl.BlockSpec((LANES, LANES), lambda b, c: (c, c)), vspec,
                  vspec],
        out_specs=[pl.BlockSpec((seq, LANES), lambda b, c: (b, c)),
                   pl.BlockSpec((None, 1, LANES), lambda b, c: (b, 0, c))],
        out_shape=[jax.ShapeDtypeStruct((n_batch * seq, D_RNN), F32),
                   jax.ShapeDtypeStruct((n_batch, 1, D_RNN), F32)],
        compiler_params=_cparams(("parallel", "parallel")),
        name="lru_prompt",
    )(proj, proj, conv_w, vec(conv_b), wa_bd, vec(b_a), wx_bd, vec(b_x), vec(lam))
    return out, hl.reshape(n_batch, D_RNN)


def _lru_sample_kernel(xp_ref, gr_ref, h0_ref, cw_ref, cb_ref, wa_ref, ba_ref, wx_ref, bx_ref, lam_ref,
                       o_ref, hl_ref):
    t_dec = gr_ref.shape[0]
    sp = _softplus(-lam_ref[...])
    h = h0_ref[...]
    for t in range(t_dec):
        xc = cb_ref[...]
        for j in range(CONV_WIDTH):
            xc = xc + xp_ref[t + j] * cw_ref[j:j + 1, :]
        a, b = _lru_gates(xc, wa_ref[...], ba_ref[...], wx_ref[...], bx_ref[...], sp)
        h = a * h + b
        o_ref[t] = h * jax.nn.gelu(gr_ref[t])
    hl_ref[...] = h


def _lru_sample(xr, gr, h0, conv0, conv_w, conv_b, wa_bd, b_a, wx_bd, b_x, lam):
    n_dec, t_dec, _ = xr.shape
    xp = jnp.transpose(jnp.concatenate([conv0, xr], axis=1), (1, 0, 2))
    grt = jnp.transpose(gr, (1, 0, 2))
    vec = lambda v: v.reshape(1, D_RNN)
    full = lambda *s: pl.BlockSpec(s, lambda i: (0,) * len(s))
    out, hl = pl.pallas_call(
        _lru_sample_kernel,
        grid=(1,),
        in_specs=[full(t_dec + CONV_WIDTH - 1, n_dec, D_RNN), full(t_dec, n_dec, D_RNN), full(n_dec, D_RNN),
                  full(CONV_WIDTH, D_RNN), full(1, D_RNN), full(D_RNN, D_RNN), full(1, D_RNN),
                  full(D_RNN, D_RNN), full(1, D_RNN), full(1, D_RNN)],
        out_specs=[full(t_dec, n_dec, D_RNN), full(n_dec, D_RNN)],
        out_shape=[jax.ShapeDtypeStruct((t_dec, n_dec, D_RNN), F32),
                   jax.ShapeDtypeStruct((n_dec, D_RNN), F32)],
        compiler_params=_cparams(("arbitrary",)),
        name="lru_sample",
    )(xp, grt, h0, conv_w, vec(conv_b), wa_bd, vec(b_a), wx_bd, vec(b_x), vec(lam))
    return jnp.transpose(out, (1, 0, 2)), hl


def _block_diag(w):
    n, k, _ = w.shape
    eye = jnp.asarray(np.eye(n, dtype=np.float32))
    return (w[:, :, None, :] * eye[:, None, :, None]).reshape(n * k, n * k)


GDN_ROWS = 256
GDN_INTRA_ROWS = 2048


def _bmm(a, b):
    return jnp.einsum('nij,njk->nik', a.astype(BF16), b.astype(BF16), preferred_element_type=F32)


def _bmm_nt(a, b):
    return jnp.einsum('nid,njd->nij', a.astype(BF16), b.astype(BF16), preferred_element_type=F32)


def _col(x, lane, idx):
    return jnp.sum(jnp.where(lane == idx, x, 0.0), axis=1, keepdims=True)


def _gdn_gates_kernel(x_ref, a_ref, dt_ref, beta_ref, gc_ref, *, chunk, pad_period, pad_rows):
    x = x_ref[...]
    tm = x.shape[0]
    beta = jax.nn.sigmoid(x)
    g = -jnp.exp(a_ref[...]) * _softplus(x + dt_ref[...])
    if pad_period:
        row = lax.broadcasted_iota(jnp.int32, x.shape, 0)
        valid = (row & (pad_period - 1)) >= pad_rows
        beta = jnp.where(valid, beta, 0.0)
        g = jnp.where(valid, g, 0.0)
    r = lax.broadcasted_iota(jnp.int32, (tm, tm), 0)
    c = lax.broadcasted_iota(jnp.int32, (tm, tm), 1)
    sh = chunk.bit_length() - 1
    tri = jnp.where(((r >> sh) == (c >> sh)) & (c <= r), 1.0, 0.0)
    beta_ref[...] = beta
    gc_ref[...] = jnp.dot(tri, g, preferred_element_type=F32, precision=lax.Precision.HIGHEST)


def _gdn_gates(x, col, rows, a_log, dt_bias, chunk, pad_period=0, pad_rows=0):
    tm = GDN_ROWS
    assert rows % tm == 0 and tm % chunk == 0
    a_pad = jnp.pad(a_log.reshape(1, -1), ((0, 0), (GDN_HEADS, LANES - 2 * GDN_HEADS)))
    dt_pad = jnp.pad(dt_bias.reshape(1, -1), ((0, 0), (GDN_HEADS, LANES - 2 * GDN_HEADS)))
    return pl.pallas_call(
        functools.partial(_gdn_gates_kernel, chunk=chunk, pad_period=pad_period, pad_rows=pad_rows),
        grid=(rows // tm,),
        in_specs=[pl.BlockSpec((tm, LANES), lambda i: (i, col)),
                  pl.BlockSpec((1, LANES), lambda i: (0, 0)),
                  pl.BlockSpec((1, LANES), lambda i: (0, 0))],
        out_specs=[pl.BlockSpec((tm, LANES), lambda i: (i, 0))] * 2,
        out_shape=[jax.ShapeDtypeStruct((rows, LANES), F32)] * 2,
        compiler_params=_cparams(("parallel",)),
        name="gdn_gates",
    )(x, a_pad, dt_pad)


def _gdn_prep_kernel(x_ref, cw_ref, beta_ref, gc_ref, *out_refs, mode):
    hd = HEAD_DIM
    n = x_ref.shape[0] // GDN_ROWS
    h0 = 2 * pl.program_id(1)
    lane = lax.broadcasted_iota(jnp.int32, (GDN_ROWS, LANES), 1)
    lo = lane < hd

    def halves(c0, c1):
        return jnp.where(lo, c0, c1)

    def put(ref, r0, val):
        ref[0, pl.ds(r0, GDN_ROWS), :] = val[:, :hd]
        ref[1, pl.ds(r0, GDN_ROWS), :] = val[:, hd:]

    def body(c, prev8):
        r0 = pl.multiple_of(c * GDN_ROWS, GDN_ROWS)
        cur = x_ref[pl.ds(r0, GDN_ROWS), :]
        y = _causal_conv_rows(prev8, cur, cw_ref[...])
        y = y * jax.nn.sigmoid(y)
        if mode in ("q", "k"):
            ss = y * y
            s0 = jnp.sum(jnp.where(lo, ss, 0.0), axis=1, keepdims=True)
            s1 = jnp.sum(jnp.where(lo, 0.0, ss), axis=1, keepdims=True)
            y = y * halves(lax.rsqrt(s0 + NORM_EPS), lax.rsqrt(s1 + NORM_EPS))
        if mode == "q":
            put(out_refs[0], r0, y * (hd ** -0.5))
        else:
            beta = beta_ref[pl.ds(r0, GDN_ROWS), :]
            bb = halves(_col(beta, lane, h0), _col(beta, lane, h0 + 1))
            if mode == "k":
                gc = gc_ref[pl.ds(r0, GDN_ROWS), :]
                put(out_refs[0], r0, y)
                put(out_refs[1], r0, y * bb)
                put(out_refs[2], r0, halves(_col(gc, lane, GDN_HEADS + h0), _col(gc, lane, GDN_HEADS + h0 + 1)))
            else:
                put(out_refs[0], r0, y * bb)
        return cur[GDN_ROWS - 8:]

    lax.fori_loop(0, n, body, jnp.zeros((8, LANES), F32))


def _gdn_prep(x, x_col, conv_w, beta, gc, n_batch, seq, mode):
    sec = {"q": 0, "k": 4, "v": 8}[mode]
    n_out = {"q": 1, "k": 3, "v": 1}[mode]
    hspec = pl.BlockSpec((2, seq, HEAD_DIM), lambda b, c: (b * 4 + c, 0, 0))
    gspec = pl.BlockSpec((seq, LANES), lambda b, c: (b, 0))
    outs = pl.pallas_call(
        functools.partial(_gdn_prep_kernel, mode=mode),
        grid=(n_batch, 4),
        in_specs=[pl.BlockSpec((seq, LANES), lambda b, c: (b, x_col + c)),
                  pl.BlockSpec((CONV_WIDTH, LANES), lambda b, c: (0, sec + c)),
                  gspec, gspec],
        out_specs=[hspec] * n_out,
        out_shape=[jax.ShapeDtypeStruct((n_batch * GDN_HEADS, seq, HEAD_DIM), F32)] * n_out,
        compiler_params=_cparams(("parallel", "parallel")),
        name="gdn_prep_" + mode,
    )(x, conv_w, beta, gc)
    return outs


def _gdn_intra_kernel(q_ref, k_ref, kb_ref, vb_ref, gcb_ref, gcr_ref,
                      u_ref, w_ref, qg_ref, kd_ref, qk_ref, gl_ref, *, chunk):
    hd = HEAD_DIM
    rows = q_ref.shape[0]
    n = rows // chunk
    r3 = lambda ref: ref[...].reshape(n, chunk, hd)
    q, k, kb, vb, gcb = r3(q_ref), r3(k_ref), r3(kb_ref), r3(vb_ref), r3(gcb_ref)
    gcr = gcr_ref[...]
    ci = lax.broadcasted_iota(jnp.int32, (n, chunk, chunk), 1)
    si = lax.broadcasted_iota(jnp.int32, (n, chunk, chunk), 2)
    decay = jnp.exp(jnp.where(ci >= si, gcb[:, :, :chunk] - gcr[:, None, :], NEG_INF))
    a = jnp.where(ci > si, _bmm_nt(kb, k) * decay, 0.0)
    b = -a
    y = b
    p = _bmm(b, b)
    stages = chunk.bit_length() - 2
    for t in range(stages):
        y = y + p + _bmm(y, p)
        if t < stages - 1:
            p = _bmm(p, p)
    eg = jnp.exp(gcb)
    rhs = jnp.concatenate([vb, kb * eg], axis=-1)
    sol = rhs + _bmm(y, rhs)
    gl = gcb[:, chunk - 1:chunk, :]
    u_ref[...] = sol[:, :, :hd].reshape(rows, hd)
    w_ref[...] = sol[:, :, hd:].reshape(rows, hd)
    qg_ref[...] = (q * eg).reshape(rows, hd)
    kd_ref[...] = (k * jnp.exp(gl - gcb)).reshape(rows, hd)
    qk_ref[...] = (_bmm_nt(q, k) * decay).reshape(rows, chunk)
    gl_ref[...] = jnp.exp(gl).reshape(n, hd)


def _gdn_intra(q, k, kb, vb, gcb, gc_row, chunk, rows_per_step):
    bh, seq, hd = q.shape
    tr = rows_per_step
    n = tr // chunk
    assert seq % tr == 0
    hspec = pl.BlockSpec((None, tr, hd), lambda h, i: (h, i, 0))
    shape = jax.ShapeDtypeStruct((bh, seq, hd), F32)
    return pl.pallas_call(
        functools.partial(_gdn_intra_kernel, chunk=chunk),
        grid=(bh, seq // tr),
        in_specs=[hspec] * 5 + [pl.BlockSpec((None, n, chunk), lambda h, i: (h, i, 0))],
        out_specs=[hspec] * 4 + [pl.BlockSpec((None, tr, chunk), lambda h, i: (h, i, 0)),
                                 pl.BlockSpec((None, n, hd), lambda h, i: (h, i, 0))],
        out_shape=[shape] * 4 + [jax.ShapeDtypeStruct((bh, seq, chunk), F32),
                                 jax.ShapeDtypeStruct((bh, seq // chunk, hd), F32)],
        compiler_params=_cparams(("parallel", "parallel")),
        name="gdn_intra",
    )(q, k, kb, vb, gcb, gc_row)


def _gdn_chunk_update(s, u, w, qg, kd, qk, gl):
    v_new = u - _bmm(w, s)
    o = _bmm(qg, s) + _bmm(qk, v_new)
    s = s * gl + jnp.einsum('ncd,nce->nde', kd.astype(BF16), v_new.astype(BF16), preferred_element_type=F32)
    return s, o


def _gated_rms_heads(o_ref, z_ref, w_ref, out_ref):
    hd = HEAD_DIM
    for h in range(GDN_HEADS):
        o = o_ref[h]
        r = o * lax.rsqrt(jnp.mean(o * o, axis=-1, keepdims=True) + NORM_EPS) * w_ref[...]
        z = z_ref[:, h * hd:(h + 1) * hd]
        out_ref[:, h * hd:(h + 1) * hd] = r * (z * jax.nn.sigmoid(z))


def _gdn_scan_kernel(u_ref, w_ref, qg_ref, kd_ref, qk_ref, gl_ref, z_ref, nw_ref, out_ref, sfin_ref,
                     s_scr, o_scr, *, chunk):
    j = pl.program_id(1)

    @pl.when(j == 0)
    def _zero():
        s_scr[...] = jnp.zeros_like(s_scr)

    n = u_ref.shape[1] // chunk
    s = s_scr[...]
    for c in range(n):
        sl = slice(c * chunk, (c + 1) * chunk)
        s, o = _gdn_chunk_update(s, u_ref[:, sl, :], w_ref[:, sl, :], qg_ref[:, sl, :], kd_ref[:, sl, :],
                                 qk_ref[:, sl, :], gl_ref[:, c:c + 1, :])
        o_scr[:, sl, :] = o
    s_scr[...] = s
    _gated_rms_heads(o_scr, z_ref, nw_ref, out_ref)

    @pl.when(j == pl.num_programs(1) - 1)
    def _final():
        sfin_ref[...] = s


def _gdn_scan(u, w, qg, kd, qk, gl, z, z_col, norm_w, chunk, rows_per_step):
    bh, seq, hd = u.shape
    hb, tr = GDN_HEADS, rows_per_step
    n = tr // chunk
    nt = seq // tr
    hspec = pl.BlockSpec((hb, tr, hd), lambda b, i: (b, i, 0))
    return pl.pallas_call(
        functools.partial(_gdn_scan_kernel, chunk=chunk),
        grid=(bh // hb, nt),
        in_specs=[hspec] * 4 + [pl.BlockSpec((hb, tr, chunk), lambda b, i: (b, i, 0)),
                                pl.BlockSpec((hb, n, hd), lambda b, i: (b, i, 0)),
                                pl.BlockSpec((tr, 4 * LANES), lambda b, i: (b * nt + i, z_col)),
                                pl.BlockSpec((1, hd), lambda b, i: (0, 0))],
        out_specs=[pl.BlockSpec((tr, 4 * LANES), lambda b, i: (b * nt + i, 0)),
                   pl.BlockSpec((hb, hd, hd), lambda b, i: (b, 0, 0))],
        out_shape=[jax.ShapeDtypeStruct((bh // hb * seq, 4 * LANES), F32),
                   jax.ShapeDtypeStruct((bh, hd, hd), F32)],
        scratch_shapes=[pltpu.VMEM((hb, hd, hd), F32), pltpu.VMEM((hb, tr, hd), F32)],
        compiler_params=_cparams(("parallel", "arbitrary")),
        name="gdn_scan",
    )(u, w, qg, kd, qk, gl, z, norm_w.reshape(1, hd))


GDN_DEC_SEQS_PER_STEP = 16


def _gdn_step_kernel(s0_ref, u_ref, w_ref, qg_ref, kd_ref, qk_ref, gl_ref, o_ref, s_ref, *, chunk):
    hd = HEAD_DIM
    n = s0_ref.shape[0]
    r3 = lambda ref, last: ref[...].reshape(n, chunk, last)
    s, o = _gdn_chunk_update(s0_ref[...], r3(u_ref, hd), r3(w_ref, hd), r3(qg_ref, hd), r3(kd_ref, hd),
                             r3(qk_ref, chunk), gl_ref[...][:, None, :])
    o_ref[...] = o.reshape(n * chunk, hd)
    s_ref[...] = s


def _gdn_step(s0, u, w, qg, kd, qk, gl, chunk):
    n_dec, nh, hd, _ = s0.shape
    sb = GDN_DEC_SEQS_PER_STEP
    assert n_dec % sb == 0
    hspec = pl.BlockSpec((None, sb * chunk, hd), lambda h, i: (h, i, 0))
    sspec = pl.BlockSpec((sb, None, hd, hd), lambda h, i: (i, h, 0, 0))
    return pl.pallas_call(
        functools.partial(_gdn_step_kernel, chunk=chunk),
        grid=(nh, n_dec // sb),
        in_specs=[sspec] + [hspec] * 4 + [pl.BlockSpec((None, sb * chunk, chunk), lambda h, i: (h, i, 0)),
                                         pl.BlockSpec((None, sb, hd), lambda h, i: (h, i, 0))],
        out_specs=[hspec, sspec],
        out_shape=[jax.ShapeDtypeStruct((nh, n_dec * chunk, hd), F32),
                   jax.ShapeDtypeStruct((n_dec, nh, hd, hd), F32)],
        compiler_params=_cparams(("parallel", "parallel")),
        name="gdn_step",
    )(s0, u, w, qg, kd, qk, gl)


def _gdn_post_kernel(o_ref, z_ref, w_ref, out_ref):
    _gated_rms_heads(o_ref, z_ref, w_ref, out_ref)


def _gdn_post(o, z, z_col, norm_w, n_batch, seq, tm):
    nt = seq // tm
    return pl.pallas_call(
        _gdn_post_kernel,
        grid=(n_batch, nt),
        in_specs=[pl.BlockSpec((GDN_HEADS, tm, HEAD_DIM), lambda b, i: (b, i, 0)),
                  pl.BlockSpec((tm, 4 * LANES), lambda b, i: (b * nt + i, z_col)),
                  pl.BlockSpec((1, HEAD_DIM), lambda b, i: (0, 0))],
        out_specs=pl.BlockSpec((tm, 4 * LANES), lambda b, i: (b * nt + i, 0)),
        out_shape=jax.ShapeDtypeStruct((n_batch * seq, 4 * LANES), F32),
        compiler_params=_cparams(("parallel", "parallel")),
        name="gdn_post",
    )(o, z, norm_w.reshape(1, HEAD_DIM))


def _gdn_front(x, qkv_col, gate_col, conv_w, a_log, dt_bias, n_batch, seq, chunk, pad_period=0, pad_rows=0):
    rows = n_batch * seq
    beta, gc = _gdn_gates(x, gate_col, rows, a_log, dt_bias, chunk, pad_period, pad_rows)
    (q,) = _gdn_prep(x, qkv_col, conv_w, beta, gc, n_batch, seq, "q")
    k, kb, gcb = _gdn_prep(x, qkv_col + 4, conv_w, beta, gc, n_batch, seq, "k")
    (vb,) = _gdn_prep(x, qkv_col + 8, conv_w, beta, gc, n_batch, seq, "v")
    gc_row = jnp.transpose(gc[:, GDN_HEADS:2 * GDN_HEADS].reshape(n_batch, seq, GDN_HEADS), (0, 2, 1))
    gc_row = gc_row.reshape(n_batch * GDN_HEADS, seq // chunk, chunk)
    return _gdn_intra(q, k, kb, vb, gcb, gc_row, chunk, min(seq, GDN_INTRA_ROWS))


TM = 768
TM_SEG = 512


def _pad_cols(w, n):
    return jnp.pad(w, ((0, 0), (0, n - w.shape[1])))


def _seq_tails(proj, n_batch, seq, n_rows, col, width):
    return jnp.stack([proj[(b + 1) * seq - n_rows:(b + 1) * seq, col:col + width] for b in range(n_batch)])


GDN_DEC_PERIOD = 8
MIX_HALF = N_ATT_HEADS * HEAD_DIM

EVEN_K = MIX_HALF
EVEN_V = EVEN_K + KV_HEADS_A * HEAD_DIM
EVEN_CONV = EVEN_V + KV_HEADS_A * HEAD_DIM
EVEN_Z = EVEN_CONV + GDN_CONV_CH
EVEN_GATE = EVEN_Z + GDN_HEADS * HEAD_DIM
EVEN_COLS = EVEN_GATE + LANES
EVEN_TN = 5 * LANES

ODD_K = MIX_HALF
ODD_V = ODD_K + KV_HEADS_C * HEAD_DIM
ODD_X = ODD_V + KV_HEADS_C * HEAD_DIM
ODD_G = ODD_X + D_RNN
ODD_COLS = ODD_G + D_RNN
ODD_TN = ODD_COLS // 2


def kernel(x_prompt, x_sample, cache_k_moba, cache_v_moba, state_gdn, state_gdn_conv, cache_k_swa, cache_v_swa, state_lru, state_lru_conv, page_table, w_in_even, w_out_even, gdn_conv_w, gdn_a_log, gdn_dt_bias, gdn_norm_w, w_in_odd, w_out_odd, swa_sinks, lru_conv_w, lru_conv_b, lru_w_a, lru_b_a, lru_w_x, lru_b_x, lru_lambda, rel_bias, ln_g, ln_b, router_w, router_b, w_gate, w_up, w_down):
    n_pr, seq, d = x_prompt.shape
    n_dec, dec_seq, _ = x_sample.shape
    tp = n_pr * seq
    ts = n_dec * dec_seq
    n_pool = cache_k_moba.shape[0]
    kva = KV_HEADS_A * HEAD_DIM
    kvc = KV_HEADS_C * HEAD_DIM
    x_segs = [x_prompt.reshape(tp, d), x_sample.reshape(ts, d)]

    rw_pad = _pad_cols(router_w, LANES).T
    rb_pad = router_b.reshape(N_EXPERTS, 1)
    experts = [_group_experts(w_gate[layer], w_up[layer], w_down[layer]) for layer in range(DEPTH)]

    n_hist = CONV_WIDTH - 1
    blk_of = lambda col: col // LANES

    proj = _mm(x_segs, _pad_cols(w_in_even[0], EVEN_COLS).astype(BF16), TM_SEG, EVEN_TN)
    ps = proj[tp:]
    moba_k_prompt = proj[:tp, EVEN_K:EVEN_K + kva].reshape(n_pr, seq, 1, KV_HEADS_A, HEAD_DIM)
    moba_v_prompt = proj[:tp, EVEN_V:EVEN_V + kva].reshape(n_pr, seq, 1, KV_HEADS_A, HEAD_DIM)
    k_new = ps[:, EVEN_K:EVEN_K + kva].reshape(n_dec, dec_seq, kva)
    v_new = ps[:, EVEN_V:EVEN_V + kva].reshape(n_dec, dec_seq, kva)
    moba_k_sample = k_new.reshape(n_dec, dec_seq, 1, KV_HEADS_A, HEAD_DIM)
    moba_v_sample = v_new.reshape(n_dec, dec_seq, 1, KV_HEADS_A, HEAD_DIM)

    oa_p = _moba_prompt(proj, rel_bias, n_pr, seq, 0, blk_of(EVEN_K), blk_of(EVEN_V))
    oa_s = _moba_sample(ps[:, :MIX_HALF].reshape(n_dec, dec_seq, N_ATT_HEADS, HEAD_DIM), k_new, v_new,
                        jnp.swapaxes(cache_k_moba.reshape(n_pool, PAGE_SIZE, -1), 1, 2),
                        jnp.swapaxes(cache_v_moba.reshape(n_pool, PAGE_SIZE, -1), 1, 2),
                        page_table, rel_bias, layer=0)

    parts = _gdn_front(proj, blk_of(EVEN_CONV), blk_of(EVEN_GATE), gdn_conv_w[0], gdn_a_log[0], gdn_dt_bias[0],
                       n_pr, seq, GDN_CHUNK)
    ob_p, s_fin = _gdn_scan(*parts, proj, EVEN_Z // MIX_HALF, gdn_norm_w[0], GDN_CHUNK, TM_SEG)
    gdn_state_prompt = s_fin.reshape(1, n_pr, GDN_HEADS, HEAD_DIM, HEAD_DIM)
    gdn_conv_prompt = _seq_tails(proj, n_pr, seq, n_hist, EVEN_CONV, GDN_CONV_CH)[None]

    period = GDN_DEC_PERIOD
    lead = period - dec_seq
    qkv_s = ps[:, EVEN_CONV:EVEN_Z].reshape(n_dec, dec_seq, GDN_CONV_CH)
    x8 = jnp.concatenate([jnp.zeros((n_dec, lead - n_hist, GDN_CONV_CH), F32), state_gdn_conv[0], qkv_s], axis=1)
    tail8 = jnp.pad(ps[:, EVEN_Z:].reshape(n_dec, dec_seq, EVEN_COLS - EVEN_Z), ((0, 0), (lead, 0), (0, 0)))
    xs = jnp.concatenate([x8, tail8], axis=-1).reshape(n_dec * period, EVEN_COLS - EVEN_CONV)
    parts = _gdn_front(xs, 0, blk_of(EVEN_GATE - EVEN_CONV), gdn_conv_w[0], gdn_a_log[0], gdn_dt_bias[0],
                       1, n_dec * period, period, period, lead)
    o_s, gdn_state_sample = _gdn_step(state_gdn[0], *parts, period)
    ob_s = _gdn_post(o_s, xs, (EVEN_Z - EVEN_CONV) // MIX_HALF, gdn_norm_w[0], 1, n_dec * period, TM_SEG)
    ob_s = ob_s.reshape(n_dec, period, MIX_HALF)[:, lead:].reshape(ts, MIX_HALF)
    gdn_conv_sample = qkv_s[None, :, dec_seq - n_hist:]

    h = _mm_ln([oa_p, oa_s.reshape(ts, MIX_HALF)], [ob_p, ob_s], w_out_even[0].astype(BF16), x_segs,
               ln_g[0, 0], ln_b[0, 0], TM_SEG)
    h = _moe_ln(h, rw_pad, rb_pad, *experts[0], ln_g[0, 1], ln_b[0, 1], TM)

    proj = _mm([h], w_in_odd[0].astype(BF16), TM, ODD_TN)
    ps = proj[tp:]
    wa_bd, wx_bd = _block_diag(lru_w_a[0]), _block_diag(lru_w_x[0])
    oc_p = _swa_prompt(proj, rel_bias, swa_sinks[0], n_pr, seq, 0, blk_of(ODD_K), blk_of(ODD_V))
    od_p, lru_p = _lru_prompt(proj, n_pr, seq, blk_of(ODD_X), blk_of(ODD_G), lru_conv_w[0], lru_conv_b[0],
                              wa_bd, lru_b_a[0], wx_bd, lru_b_x[0], lru_lambda[0])
    kv5 = (1, n_pr, WINDOW, KV_HEADS_C, HEAD_DIM)
    swa_k_prompt = _seq_tails(proj, n_pr, seq, WINDOW, ODD_K, kvc).reshape(kv5)
    swa_v_prompt = _seq_tails(proj, n_pr, seq, WINDOW, ODD_V, kvc).reshape(kv5)
    lru_conv_prompt = _seq_tails(proj, n_pr, seq, n_hist, ODD_X, D_RNN)[None]

    nbuf = cache_k_swa.shape[2]
    kc_s = ps[:, ODD_K:ODD_K + kvc].reshape(n_dec, dec_seq, kvc)
    vc_s = ps[:, ODD_V:ODD_V + kvc].reshape(n_dec, dec_seq, kvc)
    buf_k = cache_k_swa[0].reshape(n_dec, nbuf, kvc)
    buf_v = cache_v_swa[0].reshape(n_dec, nbuf, kvc)
    oc_s = _swa_sample(ps[:, :MIX_HALF].reshape(n_dec, dec_seq, N_ATT_HEADS, HEAD_DIM), kc_s, vc_s, buf_k, buf_v,
                       rel_bias, swa_sinks[0])
    keep5 = (1, n_dec, nbuf, KV_HEADS_C, HEAD_DIM)
    swa_k_sample = jnp.concatenate([buf_k, kc_s], axis=1)[:, dec_seq:].reshape(keep5)
    swa_v_sample = jnp.concatenate([buf_v, vc_s], axis=1)[:, dec_seq:].reshape(keep5)
    xr_s = ps[:, ODD_X:ODD_X + D_RNN].reshape(n_dec, dec_seq, D_RNN)
    gr_s = ps[:, ODD_G:ODD_G + D_RNN].reshape(n_dec, dec_seq, D_RNN)
    od_s, lru_s = _lru_sample(xr_s, gr_s, state_lru[0], state_lru_conv[0], lru_conv_w[0], lru_conv_b[0],
                              wa_bd, lru_b_a[0], wx_bd, lru_b_x[0], lru_lambda[0])
    lru_conv_sample = jnp.concatenate([state_lru_conv[0], xr_s], axis=1)[None, :, dec_seq:]

    h = _mm_ln([oc_p, oc_s.reshape(ts, MIX_HALF)], [od_p, od_s.reshape(ts, MIX_HALF)],
               w_out_odd[0].astype(BF16), [h], ln_g[1, 0], ln_b[1, 0], TM_SEG)
    h = _moe_ln(h, rw_pad, rb_pad, *experts[1], ln_g[1, 1], ln_b[1, 1], TM)

    y_prompt = h[:tp].reshape(n_pr, seq, d)
    y_sample = h[tp:].reshape(n_dec, dec_seq, d)
    return (y_prompt, y_sample, moba_k_prompt, moba_v_prompt, moba_k_sample, moba_v_sample,
            gdn_state_prompt, gdn_state_sample[None], gdn_conv_prompt, gdn_conv_sample,
            swa_k_prompt, swa_v_prompt, swa_k_sample, swa_v_sample,
            lru_p[None], lru_s[None], lru_conv_prompt, lru_conv_sample)
```
